```python
import jax, jax.numpy as jnp
from jax import lax
import numpy as np

D_MODEL = 1024
BATCH = 8
SEQ = 8192
DEPTH = 4

PLE_DIM = 256
N_BRANCH = 4
BRANCH_WIDTH = 256
HEAD_DIM = 64
N_HEADS = BRANCH_WIDTH // HEAD_DIM
CONV_WIDTH = 3
ATTN_BLOCK = 128
GLA_CHUNK = 64
SPATIAL_CHUNK = 128
EPS = 1e-6
MASK_VALUE = -1e30
IN_COLS = 15 * BRANCH_WIDTH + N_HEADS + N_BRANCH * D_MODEL

kernel_name = "hybrid_conv_fox_hgrn2_gmlp_gated_merge"


def _split_points():
    W = BRANCH_WIDTH
    sizes = [W] * 4 + [W] * 4 + [N_HEADS] + [W] * 4 + [W] * 3 + [N_BRANCH * D_MODEL]
    return [int(s) for s in np.cumsum(sizes)[:-1]]


def rms_norm(x, g):
    xf = x.astype(jnp.float32)
    return xf * lax.rsqrt(jnp.mean(xf * xf, axis=-1, keepdims=True) + EPS) * g.astype(jnp.float32)


def group_rms_norm(x, g):
    Bn, S, W = x.shape
    xg = x.astype(jnp.float32).reshape(Bn, S, N_HEADS, HEAD_DIM)
    xg = xg * lax.rsqrt(jnp.mean(xg * xg, axis=-1, keepdims=True) + EPS)
    return (xg * g.astype(jnp.float32).reshape(N_HEADS, HEAD_DIM)).reshape(Bn, S, W)


def short_conv_mixer(x_in, b, c, w, bias):
    S = x_in.shape[1]
    z = c.astype(jnp.float32) * x_in.astype(jnp.float32)
    zp = jnp.pad(z, ((0, 0), (CONV_WIDTH - 1, 0), (0, 0)))
    wf = w.astype(jnp.float32)
    y = zp[:, 0:S] * wf[0]
    for tap in range(1, CONV_WIDTH):
        y = y + zp[:, tap:tap + S] * wf[tap]
    return b.astype(jnp.float32) * (y + bias.astype(jnp.float32))


def forgetting_attention(q, k, v, f_logit, gq, gk):
    Bn, S, _ = q.shape
    f32 = jnp.float32

    def heads(t):
        return t.astype(f32).reshape(Bn, S, N_HEADS, HEAD_DIM).transpose(0, 2, 1, 3)

    qh = rms_norm(heads(q), gq)
    kh = rms_norm(heads(k), gk)
    vh = heads(v)
    cum = jnp.cumsum(jax.nn.log_sigmoid(f_logit.astype(f32)).transpose(0, 2, 1), axis=-1)
    nb = S // ATTN_BLOCK
    qb = qh.reshape(Bn, N_HEADS, nb, ATTN_BLOCK, HEAD_DIM).transpose(2, 0, 1, 3, 4)
    cb = cum.reshape(Bn, N_HEADS, nb, ATTN_BLOCK).transpose(2, 0, 1, 3)
    kpos = jnp.arange(S)
    scale = HEAD_DIM ** -0.5

    def block(args):
        qi, ci, bi = args
        logits = jnp.einsum('bhqd,bhkd->bhqk', qi, kh) * scale + (ci[..., None] - cum[:, :, None, :])
        qpos = bi * ATTN_BLOCK + jnp.arange(ATTN_BLOCK)
        mask = qpos[:, None] >= kpos[None, :]
        probs = jax.nn.softmax(jnp.where(mask, logits, MASK_VALUE), axis=-1)
        return jnp.einsum('bhqk,bhkd->bhqd', probs, vh)

    o = lax.map(block, (qb, cb, jnp.arange(nb)))
    return o.transpose(1, 0, 3, 2, 4).reshape(Bn, S, N_HEADS * HEAD_DIM)


def hgrn2_recurrence(q, f_logit, i_in, lb, gain):
    Bn, S, W = q.shape
    f32 = jnp.float32
    qf = jax.nn.silu(q.astype(f32))
    fl = f_logit.astype(f32)
    lbf = lb.astype(f32)
    log_g = jnp.log(lbf + (1.0 - lbf) * jax.nn.sigmoid(fl))
    kf = (1.0 - lbf) * jax.nn.sigmoid(-fl)
    vf = i_in.astype(f32)
    nc = S // GLA_CHUNK

    def chunks(t):
        return t.reshape(Bn, nc, GLA_CHUNK, N_HEADS, HEAD_DIM).transpose(1, 0, 3, 2, 4)

    causal = jnp.tril(jnp.ones((GLA_CHUNK, GLA_CHUNK), dtype=bool))[:, :, None]

    def step(state, inp):
        qc, kc, vc, gc = inp
        b = jnp.cumsum(gc, axis=2)
        o_inter = jnp.einsum('bhtk,bhkv->bhtv', qc * jnp.exp(b), state)
        diff = b[:, :, :, None, :] - b[:, :, None, :, :]
        decay = jnp.where(causal, jnp.exp(jnp.where(causal, diff, 0.0)), 0.0)
        scores = jnp.einsum('bhtk,bhsk,bhtsk->bhts', qc, kc, decay)
        o_intra = jnp.einsum('bhts,bhsv->bhtv', scores, vc)
        b_last = b[:, :, -1]
        new_state = jnp.exp(b_last)[..., None] * state + jnp.einsum(
            'bhsk,bhsv->bhkv', kc * jnp.exp(b_last[:, :, None] - b), vc)
        return new_state, o_inter + o_intra

    state0 = jnp.zeros((Bn, N_HEADS, HEAD_DIM, HEAD_DIM), f32)
    _, o = lax.scan(step, state0, (chunks(qf), chunks(kf), chunks(vf), chunks(log_g)))
    o = o.transpose(1, 0, 3, 2, 4).reshape(Bn, S, W)
    return group_rms_norm(o, gain)


def spatial_gating_mixer(u, v, gv, w_s, b_s):
    Bn, S, W = u.shape
    vn = group_rms_norm(v, gv).reshape(Bn, S // SPATIAL_CHUNK, SPATIAL_CHUNK, N_HEADS, HEAD_DIM)
    causal = jnp.tril(jnp.ones((SPATIAL_CHUNK, SPATIAL_CHUNK), dtype=jnp.float32))
    w = w_s.astype(jnp.float32) * causal
    s = jnp.einsum('gts,bnsgc->bntgc', w, vn) + b_s.astype(jnp.float32).T[None, None, :, :, None]
    return u.astype(jnp.float32) * s.reshape(Bn, S, W)


def _fwd_setup_inputs(seed: int = 0) -> dict:
    key = jax.random.key(seed)
    ks = jax.random.split(key, 24)
    W = BRANCH_WIDTH
    n = jax.random.normal
    f32 = jnp.float32
    return {
        "x": n(ks[0], (BATCH, SEQ, D_MODEL), f32),
        "p": n(ks[1], (DEPTH, BATCH, SEQ, PLE_DIM), f32),
        "norm_mix": 1.0 + 0.02 * n(ks[2], (DEPTH, D_MODEL), f32),
        "w_in": n(ks[3], (DEPTH, D_MODEL, IN_COLS), f32) * D_MODEL ** -0.5,
        "conv_w": n(ks[4], (DEPTH, CONV_WIDTH, W), f32) * CONV_WIDTH ** -0.5,
        "conv_b": 0.02 * n(ks[5], (DEPTH, W), f32),
        "fgate_bias": jnp.linspace(1.0, 4.0, N_HEADS, dtype=f32) + 0.1 * n(ks[6], (DEPTH, N_HEADS), f32),
        "q_norm": 1.0 + 0.02 * n(ks[7], (DEPTH, HEAD_DIM), f32),
        "k_norm": 1.0 + 0.02 * n(ks[8], (DEPTH, HEAD_DIM), f32),
        "lb_logits": 0.5 * n(ks[9], (DEPTH, W), f32),
        "hgrn_norm": 1.0 + 0.02 * n(ks[10], (DEPTH, W), f32),
        "sgu_norm": 1.0 + 0.02 * n(ks[11], (DEPTH, W), f32),
        "spatial_w": 0.5 * n(ks[12], (DEPTH, N_HEADS, SPATIAL_CHUNK, SPATIAL_CHUNK), f32) * SPATIAL_CHUNK ** -0.5,
        "spatial_b": 1.0 + 0.02 * n(ks[13], (DEPTH, N_HEADS, SPATIAL_CHUNK), f32),
        "w_up": n(ks[14], (DEPTH, N_BRANCH, W, D_MODEL), f32) * W ** -0.5,
        "merge_b": 0.02 * n(ks[15], (DEPTH, N_BRANCH, D_MODEL), f32),
        "w_o": n(ks[16], (DEPTH, D_MODEL, D_MODEL), f32) * (0.5 * D_MODEL ** -0.5),
        "norm_ple": 1.0 + 0.02 * n(ks[17], (DEPTH, D_MODEL), f32),
        "w_ple_gate": n(ks[18], (DEPTH, D_MODEL, D_MODEL), f32) * D_MODEL ** -0.5,
        "w_ple_proj": n(ks[19], (DEPTH, PLE_DIM, D_MODEL), f32) * (0.5 * PLE_DIM ** -0.5),
    }


def _fwd_reference(x, p, norm_mix, w_in, conv_w, conv_b, fgate_bias, q_norm, k_norm, lb_logits,
              hgrn_norm, sgu_norm, spatial_w, spatial_b, w_up, merge_b, w_o, norm_ple,
              w_ple_gate, w_ple_proj):
    dt = x.dtype
    Bn, S, _ = x.shape
    splits = _split_points()
    lb_p = jax.nn.softmax(lb_logits.astype(jnp.float32), axis=0)
    lower_bounds = jnp.clip(jnp.cumsum(lb_p, axis=0) - lb_p[0], 0.0, 1.0)
    for li in range(DEPTH):
        h = rms_norm(x, norm_mix[li]).astype(dt)
        z = h @ w_in[li]
        (a_x, a_b, a_c, a_g,
         b_q, b_k, b_v, b_g, b_f,
         c_q, c_f, c_i, c_g,
         d_u, d_v, d_g, m_logits) = jnp.split(z, splits, axis=-1)

        y_a = short_conv_mixer(a_x, a_b, a_c, conv_w[li], conv_b[li]).astype(dt) * jax.nn.silu(a_g)
        y_b = forgetting_attention(b_q, b_k, b_v, b_f + fgate_bias[li], q_norm[li], k_norm[li]).astype(dt) * jax.nn.silu(b_g)
        y_c = hgrn2_recurrence(c_q, c_f, c_i, lower_bounds[li], hgrn_norm[li]).astype(dt) * jax.nn.silu(c_g)
        y_d = spatial_gating_mixer(d_u, d_v, sgu_norm[li], spatial_w[li], spatial_b[li]).astype(dt) * jax.nn.silu(d_g)

        branches = (y_a, y_b, y_c, y_d)
        gate_logits = m_logits.reshape(Bn, S, N_BRANCH, D_MODEL)
        merged = jax.nn.sigmoid(gate_logits[:, :, 0] + merge_b[li, 0]) * (branches[0] @ w_up[li, 0])
        for bi in range(1, N_BRANCH):
            merged = merged + jax.nn.sigmoid(gate_logits[:, :, bi] + merge_b[li, bi]) * (branches[bi] @ w_up[li, bi])
        x = x + merged @ w_o[li]

        hp = rms_norm(x, norm_ple[li]).astype(dt)
        x = x + jax.nn.sigmoid(hp @ w_ple_gate[li]) * (p[li] @ w_ple_proj[li])
    return x


import jax as _jax
import jax.numpy as _jnp

TWIN_FORMAT = 'train_step'
FWD_PARAMS = ['x', 'p', 'norm_mix', 'w_in', 'conv_w', 'conv_b', 'fgate_bias', 'q_norm', 'k_norm', 'lb_logits', 'hgrn_norm', 'sgu_norm', 'spatial_w', 'spatial_b', 'w_up', 'merge_b', 'w_o', 'norm_ple', 'w_ple_gate', 'w_ple_proj']
TWIN_WEIGHTS = ['norm_mix', 'w_in', 'conv_w', 'conv_b', 'fgate_bias', 'q_norm', 'k_norm', 'lb_logits', 'hgrn_norm', 'sgu_norm', 'spatial_w', 'spatial_b', 'w_up', 'merge_b', 'w_o', 'norm_ple', 'w_ple_gate', 'w_ple_proj']
TWIN_DIFF_INPUT = 'x'
TWIN_INPUTS = ['x', 'p', 'norm_mix', 'w_in', 'conv_w', 'conv_b', 'fgate_bias', 'q_norm', 'k_norm', 'lb_logits', 'hgrn_norm', 'sgu_norm', 'spatial_w', 'spatial_b', 'w_up', 'merge_b', 'w_o', 'norm_ple', 'w_ple_gate', 'w_ple_proj', 'loss_target', 'm_norm_mix', 'm_w_in', 'm_conv_w', 'm_conv_b', 'm_fgate_bias', 'm_q_norm', 'm_k_norm', 'm_lb_logits', 'm_hgrn_norm', 'm_sgu_norm', 'm_spatial_w', 'm_spatial_b', 'm_w_up', 'm_merge_b', 'm_w_o', 'm_norm_ple', 'm_w_ple_gate', 'm_w_ple_proj', 'v_norm_mix', 'v_w_in', 'v_conv_w', 'v_conv_b', 'v_fgate_bias', 'v_q_norm', 'v_k_norm', 'v_lb_logits', 'v_hgrn_norm', 'v_sgu_norm', 'v_spatial_w', 'v_spatial_b', 'v_w_up', 'v_merge_b', 'v_w_o', 'v_norm_ple', 'v_w_ple_gate', 'v_w_ple_proj']
TWIN_OUTPUTS = ['loss', 'grad_x', 'grad_norm_mix', 'grad_w_in', 'grad_conv_w', 'grad_conv_b', 'grad_fgate_bias', 'grad_q_norm', 'grad_k_norm', 'grad_lb_logits', 'grad_hgrn_norm', 'grad_sgu_norm', 'grad_spatial_w', 'grad_spatial_b', 'grad_w_up', 'grad_merge_b', 'grad_w_o', 'grad_norm_ple', 'grad_w_ple_gate', 'grad_w_ple_proj', 'delta_norm_mix', 'delta_w_in', 'delta_conv_w', 'delta_conv_b', 'delta_fgate_bias', 'delta_q_norm', 'delta_k_norm', 'delta_lb_logits', 'delta_hgrn_norm', 'delta_sgu_norm', 'delta_spatial_w', 'delta_spatial_b', 'delta_w_up', 'delta_merge_b', 'delta_w_o', 'delta_norm_ple', 'delta_w_ple_gate', 'delta_w_ple_proj', 'new_m_norm_mix', 'new_m_w_in', 'new_m_conv_w', 'new_m_conv_b', 'new_m_fgate_bias', 'new_m_q_norm', 'new_m_k_norm', 'new_m_lb_logits', 'new_m_hgrn_norm', 'new_m_sgu_norm', 'new_m_spatial_w', 'new_m_spatial_b', 'new_m_w_up', 'new_m_merge_b', 'new_m_w_o', 'new_m_norm_ple', 'new_m_w_ple_gate', 'new_m_w_ple_proj', 'new_v_norm_mix', 'new_v_w_in', 'new_v_conv_w', 'new_v_conv_b', 'new_v_fgate_bias', 'new_v_q_norm', 'new_v_k_norm', 'new_v_lb_logits', 'new_v_hgrn_norm', 'new_v_sgu_norm', 'new_v_spatial_w', 'new_v_spatial_b', 'new_v_w_up', 'new_v_merge_b', 'new_v_w_o', 'new_v_norm_ple', 'new_v_w_ple_gate', 'new_v_w_ple_proj']
TWIN_LEAF_KINDS = {'loss': 'loss', 'grad_x': 'grad_x', 'grad_norm_mix': 'grad_w', 'grad_w_in': 'grad_w', 'grad_conv_w': 'grad_w', 'grad_conv_b': 'grad_w', 'grad_fgate_bias': 'grad_w', 'grad_q_norm': 'grad_w', 'grad_k_norm': 'grad_w', 'grad_lb_logits': 'grad_w', 'grad_hgrn_norm': 'grad_w', 'grad_sgu_norm': 'grad_w', 'grad_spatial_w': 'grad_w', 'grad_spatial_b': 'grad_w', 'grad_w_up': 'grad_w', 'grad_merge_b': 'grad_w', 'grad_w_o': 'grad_w', 'grad_norm_ple': 'grad_w', 'grad_w_ple_gate': 'grad_w', 'grad_w_ple_proj': 'grad_w', 'delta_norm_mix': 'delta_w', 'delta_w_in': 'delta_w', 'delta_conv_w': 'delta_w', 'delta_conv_b': 'delta_w', 'delta_fgate_bias': 'delta_w', 'delta_q_norm': 'delta_w', 'delta_k_norm': 'delta_w', 'delta_lb_logits': 'delta_w', 'delta_hgrn_norm': 'delta_w', 'delta_sgu_norm': 'delta_w', 'delta_spatial_w': 'delta_w', 'delta_spatial_b': 'delta_w', 'delta_w_up': 'delta_w', 'delta_merge_b': 'delta_w', 'delta_w_o': 'delta_w', 'delta_norm_ple': 'delta_w', 'delta_w_ple_gate': 'delta_w', 'delta_w_ple_proj': 'delta_w', 'new_m_norm_mix': 'new_m', 'new_m_w_in': 'new_m', 'new_m_conv_w': 'new_m', 'new_m_conv_b': 'new_m', 'new_m_fgate_bias': 'new_m', 'new_m_q_norm': 'new_m', 'new_m_k_norm': 'new_m', 'new_m_lb_logits': 'new_m', 'new_m_hgrn_norm': 'new_m', 'new_m_sgu_norm': 'new_m', 'new_m_spatial_w': 'new_m', 'new_m_spatial_b': 'new_m', 'new_m_w_up': 'new_m', 'new_m_merge_b': 'new_m', 'new_m_w_o': 'new_m', 'new_m_norm_ple': 'new_m', 'new_m_w_ple_gate': 'new_m', 'new_m_w_ple_proj': 'new_m', 'new_v_norm_mix': 'new_v', 'new_v_w_in': 'new_v', 'new_v_conv_w': 'new_v', 'new_v_conv_b': 'new_v', 'new_v_fgate_bias': 'new_v', 'new_v_q_norm': 'new_v', 'new_v_k_norm': 'new_v', 'new_v_lb_logits': 'new_v', 'new_v_hgrn_norm': 'new_v', 'new_v_sgu_norm': 'new_v', 'new_v_spatial_w': 'new_v', 'new_v_spatial_b': 'new_v', 'new_v_w_up': 'new_v', 'new_v_merge_b': 'new_v', 'new_v_w_o': 'new_v', 'new_v_norm_ple': 'new_v', 'new_v_w_ple_gate': 'new_v', 'new_v_w_ple_proj': 'new_v'}


def _forward(args):
    return _fwd_reference(*[args[k] for k in FWD_PARAMS])


def _output_shape():
    def fwd():
        inp = _fwd_setup_inputs(0)
        return _fwd_reference(*[inp[k] for k in FWD_PARAMS])
    out = _jax.eval_shape(fwd)
    return out.shape, out.dtype

N_MICROBATCH = 1
ADAM_LR = 0.001
ADAM_B1 = 0.9
ADAM_B2 = 0.999
ADAM_EPS = 1e-08
ADAM_WD = 0.01
ADAM_STEP = 10
PER_EXAMPLE_BATCH_AXIS = {'x': 0, 'p': 1, 'loss_target': 0}
SHARED_INPUTS = []
_WEIGHT_DTYPES = {'norm_mix': _jnp.float32, 'w_in': _jnp.float32, 'conv_w': _jnp.float32, 'conv_b': _jnp.float32, 'fgate_bias': _jnp.float32, 'q_norm': _jnp.float32, 'k_norm': _jnp.float32, 'lb_logits': _jnp.float32, 'hgrn_norm': _jnp.float32, 'sgu_norm': _jnp.float32, 'spatial_w': _jnp.float32, 'spatial_b': _jnp.float32, 'w_up': _jnp.float32, 'merge_b': _jnp.float32, 'w_o': _jnp.float32, 'norm_ple': _jnp.float32, 'w_ple_gate': _jnp.float32, 'w_ple_proj': _jnp.float32}
MOMENT_SCALE = {'norm_mix': 1.389348e+01, 'w_in': 1.631076e-01, 'conv_w': 3.563597e+00, 'conv_b': 2.907618e-01, 'fgate_bias': 1.906983e+01, 'q_norm': 1.205124e+00, 'k_norm': 1.205344e+00, 'lb_logits': 2.811153e-02, 'hgrn_norm': 6.608481e+00, 'sgu_norm': 7.802634e-01, 'spatial_w': 1.280836e-01, 'spatial_b': 3.251105e+00, 'w_up': 1.248783e-01, 'merge_b': 5.137468e-01, 'w_o': 4.761279e-01, 'norm_ple': 4.407231e-01, 'w_ple_gate': 6.287456e-02, 'w_ple_proj': 4.658461e-01}


def _to_microbatches(a, axis):
    t = _jnp.moveaxis(a, axis, 0)
    t = t.reshape((N_MICROBATCH, t.shape[0] // N_MICROBATCH) + t.shape[1:])
    return _jnp.moveaxis(t, 1, axis + 1)


def setup_inputs(seed: int = 0) -> dict:
    inp = _fwd_setup_inputs(seed)
    key = _jax.random.fold_in(_jax.random.key(seed), 7919)
    shape, _ = _output_shape()
    out = dict(inp)
    out["loss_target"] = _jax.random.normal(_jax.random.fold_in(key, 0), shape, _jnp.float32)
    for i, name in enumerate(TWIN_WEIGHTS):
        w = inp[name].astype(_jnp.float32)
        if MOMENT_SCALE is None:
            s = _jnp.sqrt(_jnp.mean(_jnp.square(w)) + 1e-30)
        else:
            s = MOMENT_SCALE[name]
        km, kv = _jax.random.split(_jax.random.fold_in(key, i + 1))
        out[name] = w
        out["m_" + name] = s * _jax.random.normal(km, w.shape, _jnp.float32)
        out["v_" + name] = (s * s) * _jax.random.uniform(kv, w.shape, _jnp.float32, 0.5, 1.5)
    if N_MICROBATCH > 1:
        for name, axis in PER_EXAMPLE_BATCH_AXIS.items():
            out[name] = _to_microbatches(out[name], axis)
    return {'x': out['x'], 'p': out['p'], 'norm_mix': out['norm_mix'], 'w_in': out['w_in'], 'conv_w': out['conv_w'], 'conv_b': out['conv_b'], 'fgate_bias': out['fgate_bias'], 'q_norm': out['q_norm'], 'k_norm': out['k_norm'], 'lb_logits': out['lb_logits'], 'hgrn_norm': out['hgrn_norm'], 'sgu_norm': out['sgu_norm'], 'spatial_w': out['spatial_w'], 'spatial_b': out['spatial_b'], 'w_up': out['w_up'], 'merge_b': out['merge_b'], 'w_o': out['w_o'], 'norm_ple': out['norm_ple'], 'w_ple_gate': out['w_ple_gate'], 'w_ple_proj': out['w_ple_proj'], 'loss_target': out['loss_target'], 'm_norm_mix': out['m_norm_mix'], 'm_w_in': out['m_w_in'], 'm_conv_w': out['m_conv_w'], 'm_conv_b': out['m_conv_b'], 'm_fgate_bias': out['m_fgate_bias'], 'm_q_norm': out['m_q_norm'], 'm_k_norm': out['m_k_norm'], 'm_lb_logits': out['m_lb_logits'], 'm_hgrn_norm': out['m_hgrn_norm'], 'm_sgu_norm': out['m_sgu_norm'], 'm_spatial_w': out['m_spatial_w'], 'm_spatial_b': out['m_spatial_b'], 'm_w_up': out['m_w_up'], 'm_merge_b': out['m_merge_b'], 'm_w_o': out['m_w_o'], 'm_norm_ple': out['m_norm_ple'], 'm_w_ple_gate': out['m_w_ple_gate'], 'm_w_ple_proj': out['m_w_ple_proj'], 'v_norm_mix': out['v_norm_mix'], 'v_w_in': out['v_w_in'], 'v_conv_w': out['v_conv_w'], 'v_conv_b': out['v_conv_b'], 'v_fgate_bias': out['v_fgate_bias'], 'v_q_norm': out['v_q_norm'], 'v_k_norm': out['v_k_norm'], 'v_lb_logits': out['v_lb_logits'], 'v_hgrn_norm': out['v_hgrn_norm'], 'v_sgu_norm': out['v_sgu_norm'], 'v_spatial_w': out['v_spatial_w'], 'v_spatial_b': out['v_spatial_b'], 'v_w_up': out['v_w_up'], 'v_merge_b': out['v_merge_b'], 'v_w_o': out['v_w_o'], 'v_norm_ple': out['v_norm_ple'], 'v_w_ple_gate': out['v_w_ple_gate'], 'v_w_ple_proj': out['v_w_ple_proj']}


def _loss(weights, diff, rest, loss_target):
    with _jax.named_scope("forward"):
        args = {**rest, TWIN_DIFF_INPUT: diff, **{k: w.astype(_WEIGHT_DTYPES[k]) for k, w in weights.items()}}
        y = _forward(args)
    with _jax.named_scope("loss_head"):
        err = _jnp.square(y.astype(_jnp.float32) - loss_target)
        return 0.5 * _jnp.sum(_jnp.mean(err, axis=-1)) if err.ndim else 0.5 * err


def _adamw(w, g, m, v):
    m = ADAM_B1 * m + (1.0 - ADAM_B1) * g
    v = ADAM_B2 * v + (1.0 - ADAM_B2) * _jnp.square(g)
    m_hat = m / (1.0 - ADAM_B1 ** ADAM_STEP)
    v_hat = v / (1.0 - ADAM_B2 ** ADAM_STEP)
    delta = -ADAM_LR * (m_hat / (_jnp.sqrt(v_hat) + ADAM_EPS) + ADAM_WD * w)
    return delta, m, v


def reference(x, p, norm_mix, w_in, conv_w, conv_b, fgate_bias, q_norm, k_norm, lb_logits, hgrn_norm, sgu_norm, spatial_w, spatial_b, w_up, merge_b, w_o, norm_ple, w_ple_gate, w_ple_proj, loss_target, m_norm_mix, m_w_in, m_conv_w, m_conv_b, m_fgate_bias, m_q_norm, m_k_norm, m_lb_logits, m_hgrn_norm, m_sgu_norm, m_spatial_w, m_spatial_b, m_w_up, m_merge_b, m_w_o, m_norm_ple, m_w_ple_gate, m_w_ple_proj, v_norm_mix, v_w_in, v_conv_w, v_conv_b, v_fgate_bias, v_q_norm, v_k_norm, v_lb_logits, v_hgrn_norm, v_sgu_norm, v_spatial_w, v_spatial_b, v_w_up, v_merge_b, v_w_o, v_norm_ple, v_w_ple_gate, v_w_ple_proj):
    given = dict(x=x, p=p, norm_mix=norm_mix, w_in=w_in, conv_w=conv_w, conv_b=conv_b, fgate_bias=fgate_bias, q_norm=q_norm, k_norm=k_norm, lb_logits=lb_logits, hgrn_norm=hgrn_norm, sgu_norm=sgu_norm, spatial_w=spatial_w, spatial_b=spatial_b, w_up=w_up, merge_b=merge_b, w_o=w_o, norm_ple=norm_ple, w_ple_gate=w_ple_gate, w_ple_proj=w_ple_proj, loss_target=loss_target, m_norm_mix=m_norm_mix, m_w_in=m_w_in, m_conv_w=m_conv_w, m_conv_b=m_conv_b, m_fgate_bias=m_fgate_bias, m_q_norm=m_q_norm, m_k_norm=m_k_norm, m_lb_logits=m_lb_logits, m_hgrn_norm=m_hgrn_norm, m_sgu_norm=m_sgu_norm, m_spatial_w=m_spatial_w, m_spatial_b=m_spatial_b, m_w_up=m_w_up, m_merge_b=m_merge_b, m_w_o=m_w_o, m_norm_ple=m_norm_ple, m_w_ple_gate=m_w_ple_gate, m_w_ple_proj=m_w_ple_proj, v_norm_mix=v_norm_mix, v_w_in=v_w_in, v_conv_w=v_conv_w, v_conv_b=v_conv_b, v_fgate_bias=v_fgate_bias, v_q_norm=v_q_norm, v_k_norm=v_k_norm, v_lb_logits=v_lb_logits, v_hgrn_norm=v_hgrn_norm, v_sgu_norm=v_sgu_norm, v_spatial_w=v_spatial_w, v_spatial_b=v_spatial_b, v_w_up=v_w_up, v_merge_b=v_merge_b, v_w_o=v_w_o, v_norm_ple=v_norm_ple, v_w_ple_gate=v_w_ple_gate, v_w_ple_proj=v_w_ple_proj)
    weights = {n: given[n] for n in TWIN_WEIGHTS}
    shared = {n: given[n] for n in SHARED_INPUTS}
    per_example = {n: given[n] for n in ['x', 'p']}
    grad_fn = _jax.value_and_grad(_loss, argnums=(0, 1))

    def one_microbatch(ex, loss_target):
        ex = dict(ex)
        diff = ex.pop(TWIN_DIFF_INPUT)
        return grad_fn(weights, diff, {**shared, **ex}, loss_target)

    if N_MICROBATCH == 1:
        loss, (grad_w, grad_x) = one_microbatch(per_example, given["loss_target"])
    else:
        def body(carry, xs):
            loss_sum, grad_sum = carry
            l_k, (gw_k, gx_k) = one_microbatch(xs[0], xs[1])
            with _jax.named_scope("update"):
                return (loss_sum + l_k, _jax.tree.map(_jnp.add, grad_sum, gw_k)), gx_k

        init = (_jnp.zeros((), _jnp.float32), _jax.tree.map(_jnp.zeros_like, weights))
        (loss, grad_w), grad_x = _jax.lax.scan(body, init, (per_example, given["loss_target"]))
    with _jax.named_scope("update"):
        delta_w, new_m, new_v = {}, {}, {}
        for n in TWIN_WEIGHTS:
            delta_w[n], new_m[n], new_v[n] = _adamw(weights[n], grad_w[n], given["m_" + n], given["v_" + n])
    return (loss, grad_x, *[grad_w[n] for n in TWIN_WEIGHTS], *[delta_w[n] for n in TWIN_WEIGHTS],
            *[new_m[n] for n in TWIN_WEIGHTS], *[new_v[n] for n in TWIN_WEIGHTS])
```

```python
import functools

import numpy as np
import jax
import jax.numpy as jnp
from jax import lax
from jax.experimental import pallas as pl
from jax.experimental.pallas import tpu as pltpu

F32 = jnp.float32
MXU_DTYPE = jnp.bfloat16

D_MODEL = 1024
BW = 256
NH = 4
DH = 64
DEPTH = 4
NBR = 4
PLE = 256
CHUNK = 64
SCHUNK = 128
EPS = 1e-6
MASK_VALUE = -1e30
NLEV = 6

ADAM_LR, ADAM_B1, ADAM_B2, ADAM_EPS, ADAM_WD, ADAM_STEP = 0.001, 0.9, 0.999, 1e-08, 0.01, 10

ZM, ZA, ZB, ZC, ZD, ZF = 0, 4096, 5120, 6144, 7168, 8192
ZCOLS = 8320
_OA, _OB, _OF, _OC, _OD, _OM, _OEND = 0, 1024, 2048, 2052, 3076, 3844, 7940

VMEM_LIMIT = 56 * 1024 * 1024
ROW_TILE = 512
FLASH_TILE = 512
MERGE_TILE = 256
MESH = pl.DeviceIdType.MESH


def _tile(n, pref):
    t = min(n, pref)
    assert n % t == 0, (n, t)
    return t


def _call(body, *, name, grid, in_specs, out_specs, out_shape, scratch=(), sem=None):
    return pl.pallas_call(
        functools.partial(body), name=name, grid=grid, in_specs=in_specs, out_specs=out_specs,
        out_shape=out_shape, scratch_shapes=list(scratch),
        compiler_params=pltpu.CompilerParams(dimension_semantics=sem, vmem_limit_bytes=VMEM_LIMIT))


def _mx(x):
    return x.astype(MXU_DTYPE)


def _dot(a, b):
    return jnp.dot(_mx(a), _mx(b), preferred_element_type=F32)


def _dot_nt(a, b):
    return lax.dot_general(_mx(a), _mx(b), (((1,), (1,)), ((), ())), preferred_element_type=F32)


def _dot_tn(a, b):
    return lax.dot_general(_mx(a), _mx(b), (((0,), (0,)), ((), ())), preferred_element_type=F32)


def _top16(x):
    u = lax.bitcast_convert_type(x, jnp.uint32) & jnp.uint32(0xFFFF0000)
    return lax.bitcast_convert_type(u, F32)


def _split3(x):
    hi = _top16(x)
    r1 = x - hi
    mid = _top16(r1)
    return _mx(hi), _mx(mid), _mx(r1 - mid)


def _dot_exact_lhs01(t, x):
    hi, mid, lo = _split3(x)
    t = _mx(t)
    return (jnp.dot(t, hi, preferred_element_type=F32) + jnp.dot(t, mid, preferred_element_type=F32)
            + jnp.dot(t, lo, preferred_element_type=F32))


def _sigmoid(x):
    return jax.nn.sigmoid(x)


def _silu(x):
    return x * _sigmoid(x)


def _dsilu(x):
    s = _sigmoid(x)
    return s * (1.0 + x * (1.0 - s))


def _log_sigmoid(x):
    return jnp.minimum(x, 0.0) - jnp.log(1.0 + jnp.exp(-jnp.abs(x)))


def _rms_fwd(x, g):
    r = lax.rsqrt(jnp.mean(x * x, axis=-1, keepdims=True) + EPS)
    return x * r * g


def _rms_bwd(x, g, dy):
    r = lax.rsqrt(jnp.mean(x * x, axis=-1, keepdims=True) + EPS)
    n = x * r
    t = dy * g
    dx = r * (t - n * jnp.mean(t * n, axis=-1, keepdims=True))
    return dx, dy * n


def _tri(n, upper=False):
    r = lax.broadcasted_iota(jnp.int32, (n, n), 0)
    c = lax.broadcasted_iota(jnp.int32, (n, n), 1)
    return jnp.where((c >= r) if upper else (r >= c), 1.0, 0.0).astype(F32)


def _hs(h):
    return slice(h * DH, (h + 1) * DH)


def _mm_nn(a, b, out_dtype, *, name, tn=None, tk=None):
    S, K = a.shape
    N = b.shape[1]
    ts = _tile(S, ROW_TILE)
    tn = _tile(N, tn or N)
    tk = _tile(K, tk or K)
    nk = K // tk

    def body(a_ref, b_ref, o_ref, acc_ref):
        k = pl.program_id(2)
        part = jnp.dot(a_ref[...], b_ref[...], preferred_element_type=F32)
        if nk == 1:
            o_ref[...] = part.astype(o_ref.dtype)
        else:
            @pl.when(k == 0)
            def _():
                acc_ref[...] = part

            @pl.when(k > 0)
            def _():
                acc_ref[...] += part

            @pl.when(k == nk - 1)
            def _():
                o_ref[...] = acc_ref[...].astype(o_ref.dtype)

    return _call(
        body, name=name, grid=(N // tn, S // ts, nk),
        in_specs=[pl.BlockSpec((ts, tk), lambda j, i, k: (i, k)), pl.BlockSpec((tk, tn), lambda j, i, k: (k, j))],
        out_specs=pl.BlockSpec((ts, tn), lambda j, i, k: (i, j)),
        out_shape=jax.ShapeDtypeStruct((S, N), out_dtype),
        scratch=[pltpu.VMEM((ts, tn) if nk > 1 else (8, 128), F32)],
        sem=("parallel", "parallel", "arbitrary"))(a, b)


def _mm_tn(x, y, *, name, tn=None, ycol=0, n=None):
    S, M = x.shape
    n = n or y.shape[1]
    ts = _tile(S, ROW_TILE)
    tn = _tile(n, tn or n)
    nj = n // tn

    def body(x_ref, y_ref, o_ref):
        @pl.when(pl.program_id(1) == 0)
        def _():
            o_ref[...] = jnp.zeros_like(o_ref)

        o_ref[...] += lax.dot_general(x_ref[...], y_ref[...], (((0,), (0,)), ((), ())), preferred_element_type=F32)

    return _call(
        body, name=name, grid=(nj, S // ts),
        in_specs=[pl.BlockSpec((ts, M), lambda j, s: (s, 0)), pl.BlockSpec((ts, tn), lambda j, s: (s, ycol * nj + j))],
        out_specs=pl.BlockSpec((M, tn), lambda j, s: (0, j)),
        out_shape=jax.ShapeDtypeStruct((M, n), F32),
        sem=("parallel", "arbitrary"))(x, y)


def _norm_fwd(x, g):
    S = x.shape[0]
    ts = _tile(S, ROW_TILE)

    def body(x_ref, g_ref, h_ref):
        h_ref[...] = _mx(_rms_fwd(x_ref[...], g_ref[...]))

    return _call(
        body, name="norm_fwd", grid=(S // ts,),
        in_specs=[pl.BlockSpec((ts, D_MODEL), lambda i: (i, 0)), pl.BlockSpec((1, D_MODEL), lambda i: (0, 0))],
        out_specs=pl.BlockSpec((ts, D_MODEL), lambda i: (i, 0)),
        out_shape=jax.ShapeDtypeStruct((S, D_MODEL), MXU_DTYPE), sem=("parallel",))(x, g)


def _norm_bwd(x, g, dh, dres):
    S = x.shape[0]
    ts = _tile(S, ROW_TILE)

    def body(x_ref, g_ref, dh_ref, dr_ref, dx_ref, st_ref):
        @pl.when(pl.program_id(0) == 0)
        def _():
            st_ref[...] = jnp.zeros_like(st_ref)

        dx, dgr = _rms_bwd(x_ref[...], g_ref[...], dh_ref[...])
        dx_ref[...] = dr_ref[...] + dx
        st_ref[0:1, :] += jnp.sum(dgr, axis=0, keepdims=True)

    row = pl.BlockSpec((ts, D_MODEL), lambda i: (i, 0))
    return _call(
        body, name="norm_bwd", grid=(S // ts,),
        in_specs=[row, pl.BlockSpec((1, D_MODEL), lambda i: (0, 0)), row, row],
        out_specs=[row, pl.BlockSpec((8, D_MODEL), lambda i: (0, 0))],
        out_shape=[jax.ShapeDtypeStruct((S, D_MODEL), F32), jax.ShapeDtypeStruct((8, D_MODEL), F32)],
        sem=("arbitrary",))(x, g, dh, dres)


def _loss_fwd_bwd(y, target):
    S = y.shape[0]
    ts = _tile(S, ROW_TILE)

    def body(y_ref, t_ref, l_ref, dy_ref):
        @pl.when(pl.program_id(0) == 0)
        def _():
            l_ref[...] = jnp.zeros_like(l_ref)

        err = y_ref[...] - t_ref[...]
        dy_ref[...] = err * (1.0 / D_MODEL)
        rowloss = jnp.mean(err * err, axis=-1, keepdims=True)
        l_ref[...] += 0.5 * jnp.sum(rowloss, axis=0, keepdims=True)

    row = pl.BlockSpec((ts, D_MODEL), lambda i: (i, 0))
    return _call(
        body, name="loss", grid=(S // ts,), in_specs=[row, row],
        out_specs=[pl.BlockSpec((8, 128), lambda i: (0, 0)), row],
        out_shape=[jax.ShapeDtypeStruct((8, 128), F32), jax.ShapeDtypeStruct((S, D_MODEL), F32)],
        sem=("arbitrary",))(y, target)


def _shift_down(x, k, halo, rows):
    y = pltpu.roll(x, k, 0)
    for j in range(k):
        y = jnp.where(rows == j, halo[8 - k + j:8 - k + j + 1, :], y)
    return y


def _shift_up(x, k, halo, rows, n):
    y = pltpu.roll(x, n - k, 0)
    for j in range(k):
        y = jnp.where(rows == n - k + j, halo[j:j + 1, :], y)
    return y


def _conv_parts(za, zh, cw, first, rows):
    ax, ab, ac, ag = za[:, 0:BW], za[:, BW:2 * BW], za[:, 2 * BW:3 * BW], za[:, 3 * BW:4 * BW]
    zz = ac * ax
    hz = jnp.where(first, 0.0, zh[:, 2 * BW:3 * BW] * zh[:, 0:BW])
    zz1 = _shift_down(zz, 1, hz, rows)
    zz2 = _shift_down(zz, 2, hz, rows)
    conv = zz2 * cw[0:1, :] + zz1 * cw[1:2, :] + zz * cw[2:3, :] + cw[3:4, :]
    return ax, ab, ac, ag, zz, zz1, zz2, conv


def _conv_fwd(z, cw):
    S = z.shape[0]
    ts = _tile(S, ROW_TILE)
    hb = ts // 8

    def body(za_ref, zh_ref, cw_ref, y_ref):
        i = pl.program_id(0)
        rows = lax.broadcasted_iota(jnp.int32, (ts, BW), 0)
        ax, ab, ac, ag, zz, zz1, zz2, conv = _conv_parts(za_ref[...], zh_ref[...], cw_ref[...], i == 0, rows)
        y_ref[...] = _mx(ab * conv * _silu(ag))

    return _call(
        body, name="conv_fwd", grid=(S // ts,),
        in_specs=[pl.BlockSpec((ts, 1024), lambda i: (i, ZA // 1024)),
                  pl.BlockSpec((8, 1024), lambda i: (jnp.maximum(i * hb - 1, 0), ZA // 1024)),
                  pl.BlockSpec((8, BW), lambda i: (0, 0))],
        out_specs=pl.BlockSpec((ts, BW), lambda i: (i, 0)),
        out_shape=jax.ShapeDtypeStruct((S, BW), MXU_DTYPE), sem=("parallel",))(z, z, cw)


def _conv_bwd(z, cw, dy):
    S = z.shape[0]
    ts = _tile(S, ROW_TILE)
    hb = ts // 8
    nt = S // ts

    def body(za_ref, zh_ref, zn_ref, cw_ref, dy_ref, dyn_ref, dz_ref, st_ref):
        i = pl.program_id(0)

        @pl.when(i == 0)
        def _():
            st_ref[...] = jnp.zeros_like(st_ref)

        cw = cw_ref[...]
        rows = lax.broadcasted_iota(jnp.int32, (ts, BW), 0)
        ax, ab, ac, ag, zz, zz1, zz2, conv = _conv_parts(za_ref[...], zh_ref[...], cw, i == 0, rows)
        dy = dy_ref[...]
        sg = _silu(ag)
        dc = dy * ab * sg
        zn = zn_ref[...]
        dcn = jnp.where(i == nt - 1, 0.0, dyn_ref[...] * zn[:, BW:2 * BW] * _silu(zn[:, 3 * BW:4 * BW]))
        dc1 = _shift_up(dc, 1, dcn, rows, ts)
        dc2 = _shift_up(dc, 2, dcn, rows, ts)
        dzz = dc * cw[2:3, :] + dc1 * cw[1:2, :] + dc2 * cw[0:1, :]
        dz_ref[:, 0:BW] = _mx(dzz * ac)
        dz_ref[:, BW:2 * BW] = _mx(dy * conv * sg)
        dz_ref[:, 2 * BW:3 * BW] = _mx(dzz * ax)
        dz_ref[:, 3 * BW:4 * BW] = _mx(dy * ab * conv * _dsilu(ag))
        st_ref[0:1, :] += jnp.sum(dc * zz2, axis=0, keepdims=True)
        st_ref[1:2, :] += jnp.sum(dc * zz1, axis=0, keepdims=True)
        st_ref[2:3, :] += jnp.sum(dc * zz, axis=0, keepdims=True)
        st_ref[3:4, :] += jnp.sum(dc, axis=0, keepdims=True)

    return _call(
        body, name="conv_bwd", grid=(nt,),
        in_specs=[pl.BlockSpec((ts, 1024), lambda i: (i, ZA // 1024)),
                  pl.BlockSpec((8, 1024), lambda i: (jnp.maximum(i * hb - 1, 0), ZA // 1024)),
                  pl.BlockSpec((8, 1024), lambda i: (jnp.minimum((i + 1) * hb, S // 8 - 1), ZA // 1024)),
                  pl.BlockSpec((8, BW), lambda i: (0, 0)),
                  pl.BlockSpec((ts, BW), lambda i: (i, 0)),
                  pl.BlockSpec((8, BW), lambda i: (jnp.minimum((i + 1) * hb, S // 8 - 1), 0))],
        out_specs=[pl.BlockSpec((ts, 1024), lambda i: (i, 0)), pl.BlockSpec((8, BW), lambda i: (0, 0))],
        out_shape=[jax.ShapeDtypeStruct((S, 1024), MXU_DTYPE), jax.ShapeDtypeStruct((8, BW), F32)],
        sem=("arbitrary",))(z, z, z, cw, dy, dy)


def _attn_prep_fwd(z, fb, gq, gk):
    S = z.shape[0]
    ts = _tile(S, ROW_TILE)

    def body(zb_ref, zf_ref, fb_ref, gq_ref, gk_ref, q_ref, k_ref, v_ref, cc_ref, cr_ref, carry):
        @pl.when(pl.program_id(0) == 0)
        def _():
            carry[...] = jnp.zeros_like(carry)

        zb = zb_ref[...]
        gqv, gkv = gq_ref[...], gk_ref[...]
        for h in range(NH):
            q_ref[:, _hs(h)] = _mx(_rms_fwd(zb[:, _hs(h)], gqv) * (DH ** -0.5))
            k_ref[:, _hs(h)] = _mx(_rms_fwd(zb[:, BW + h * DH:BW + (h + 1) * DH], gkv))
        v_ref[...] = _mx(zb[:, 2 * BW:3 * BW])
        lf = _log_sigmoid(zf_ref[...] + fb_ref[...])
        cum = _dot_exact_lhs01(_tri(ts), lf) + carry[...]
        carry[...] = cum[ts - 1:ts, :]
        cc_ref[...] = cum
        cr_ref[...] = cum.T[0:8, :]

    row = lambda w: pl.BlockSpec((ts, w), lambda i: (i, 0))
    return _call(
        body, name="attn_prep_fwd", grid=(S // ts,),
        in_specs=[pl.BlockSpec((ts, 1024), lambda i: (i, ZB // 1024)), pl.BlockSpec((ts, 128), lambda i: (i, ZF // 128)),
                  pl.BlockSpec((1, 128), lambda i: (0, 0)), pl.BlockSpec((1, DH), lambda i: (0, 0)),
                  pl.BlockSpec((1, DH), lambda i: (0, 0))],
        out_specs=[row(BW), row(BW), row(BW), row(128), pl.BlockSpec((8, ts), lambda i: (0, i))],
        out_shape=[jax.ShapeDtypeStruct((S, BW), MXU_DTYPE)] * 3
        + [jax.ShapeDtypeStruct((S, 128), F32), jax.ShapeDtypeStruct((8, S), F32)],
        scratch=[pltpu.VMEM((1, 128), F32)], sem=("arbitrary",))(z, z, fb, gq, gk)


def _attn_scores(q, k, cc, cr, h, i, j, t):
    s = _dot_nt(q[:, _hs(h)], k[:, _hs(h)]) + (cc[:, h:h + 1] - cr[h:h + 1, :])
    qpos = i * t + lax.broadcasted_iota(jnp.int32, (t, t), 0)
    kpos = j * t + lax.broadcasted_iota(jnp.int32, (t, t), 1)
    return s, qpos >= kpos


def _flash_fwd(q, k, v, cc, cr, z):
    S = q.shape[0]
    t = _tile(S, FLASH_TILE)
    n = S // t

    def body(q_ref, k_ref, v_ref, cc_ref, cr_ref, zb_ref, o_ref, lse_ref, y_ref, m_sc, l_sc, acc):
        i, j = pl.program_id(0), pl.program_id(1)

        @pl.when(j == 0)
        def _():
            m_sc[...] = jnp.full_like(m_sc, MASK_VALUE)
            l_sc[...] = jnp.zeros_like(l_sc)
            acc[...] = jnp.zeros_like(acc)

        @pl.when(j <= i)
        def _():
            qv, kv, vv, cc, cr = q_ref[...], k_ref[...], v_ref[...], cc_ref[...], cr_ref[...]
            for h in range(NH):
                s, mask = _attn_scores(qv, kv, cc, cr, h, i, j, t)
                s = jnp.where(mask, s, MASK_VALUE)
                m_old = m_sc[h]
                m_new = jnp.maximum(m_old, jnp.max(s, axis=-1, keepdims=True))
                p = jnp.exp(s - m_new)
                alpha = jnp.exp(m_old - m_new)
                l_sc[h] = alpha * l_sc[h] + jnp.sum(p, axis=-1, keepdims=True)
                acc[:, _hs(h)] = alpha * acc[:, _hs(h)] + _dot(p, vv[:, _hs(h)])
                m_sc[h] = m_new

        @pl.when(j == i)
        def _():
            lse_ref[...] = jnp.zeros_like(lse_ref)
            for h in range(NH):
                o_ref[:, _hs(h)] = acc[:, _hs(h)] / l_sc[h]
                lse_ref[:, h:h + 1] = m_sc[h] + jnp.log(l_sc[h])
            y_ref[...] = _mx(o_ref[...] * _silu(zb_ref[:, 3 * BW:4 * BW]))

    qspec = lambda w: pl.BlockSpec((t, w), lambda i, j: (i, 0))
    kspec = lambda w: pl.BlockSpec((t, w), lambda i, j: (jnp.minimum(j, i), 0))
    return _call(
        body, name="flash_fwd", grid=(n, n),
        in_specs=[qspec(BW), kspec(BW), kspec(BW), qspec(128), pl.BlockSpec((8, t), lambda i, j: (0, jnp.minimum(j, i))),
                  pl.BlockSpec((t, 1024), lambda i, j: (i, ZB // 1024))],
        out_specs=[qspec(BW), qspec(128), qspec(BW)],
        out_shape=[jax.ShapeDtypeStruct((S, BW), F32), jax.ShapeDtypeStruct((S, 128), F32),
                   jax.ShapeDtypeStruct((S, BW), MXU_DTYPE)],
        scratch=[pltpu.VMEM((NH, t, 1), F32), pltpu.VMEM((NH, t, 1), F32), pltpu.VMEM((t, BW), F32)],
        sem=("parallel", "arbitrary"))(q, k, v, cc, cr, z)


def _attn_gate_bwd(dy, o, z):
    S = dy.shape[0]
    ts = _tile(S, ROW_TILE)

    def body(dy_ref, o_ref, zb_ref, do_ref, dg_ref):
        g = zb_ref[:, 3 * BW:4 * BW]
        dy = dy_ref[...]
        do_ref[...] = dy * _silu(g)
        dg_ref[...] = _mx(dy * o_ref[...] * _dsilu(g))

    row = pl.BlockSpec((ts, BW), lambda i: (i, 0))
    return _call(
        body, name="attn_gate_bwd", grid=(S // ts,),
        in_specs=[row, row, pl.BlockSpec((ts, 1024), lambda i: (i, ZB // 1024))],
        out_specs=[row, row],
        out_shape=[jax.ShapeDtypeStruct((S, BW), F32), jax.ShapeDtypeStruct((S, BW), MXU_DTYPE)],
        sem=("parallel",))(dy, o, z)


def _flash_probs(qv, kv, vv, cc, cr, lse, do, o, h, i, j, t, drow=None):
    s, mask = _attn_scores(qv, kv, cc, cr, h, i, j, t)
    p = jnp.where(mask, jnp.exp(s - lse[:, h:h + 1]), 0.0)
    doh = do[:, _hs(h)]
    dp = _dot_nt(doh, vv[:, _hs(h)])
    delta = jnp.sum(doh * o[:, _hs(h)], axis=-1, keepdims=True)
    if drow is not None:
        delta = delta + drow[:, h:h + 1]
    return p, p * (dp - delta), doh


def _flash_bwd_dq(q, k, v, cc, cr, lse, o, do):
    S = q.shape[0]
    t = _tile(S, FLASH_TILE)
    n = S // t

    def body(q_ref, k_ref, v_ref, cc_ref, cr_ref, lse_ref, o_ref, do_ref, dq_ref, dr_ref):
        i, j = pl.program_id(0), pl.program_id(1)

        @pl.when(j == 0)
        def _():
            dq_ref[...] = jnp.zeros_like(dq_ref)
            dr_ref[...] = jnp.zeros_like(dr_ref)

        @pl.when(j <= i)
        def _():
            qv, kv, vv = q_ref[...], k_ref[...], v_ref[...]
            for h in range(NH):
                p, ds, doh = _flash_probs(qv, kv, vv, cc_ref[...], cr_ref[...], lse_ref[...], do_ref[...], o_ref[...],
                                          h, i, j, t)
                dq_ref[:, _hs(h)] += _dot(ds, kv[:, _hs(h)])
                dr_ref[:, h:h + 1] += jnp.sum(ds, axis=-1, keepdims=True)

    qspec = lambda w: pl.BlockSpec((t, w), lambda i, j: (i, 0))
    kspec = lambda w: pl.BlockSpec((t, w), lambda i, j: (jnp.minimum(j, i), 0))
    return _call(
        body, name="flash_bwd_dq", grid=(n, n),
        in_specs=[qspec(BW), kspec(BW), kspec(BW), qspec(128), pl.BlockSpec((8, t), lambda i, j: (0, jnp.minimum(j, i))),
                  qspec(128), qspec(BW), qspec(BW)],
        out_specs=[qspec(BW), qspec(128)],
        out_shape=[jax.ShapeDtypeStruct((S, BW), F32), jax.ShapeDtypeStruct((S, 128), F32)],
        sem=("parallel", "arbitrary"))(q, k, v, cc, cr, lse, o, do)


def _flash_bwd_dkv(q, k, v, cc, cr, lse, o, do, drow):
    S = q.shape[0]
    t = _tile(S, FLASH_TILE)
    n = S // t

    def body(q_ref, k_ref, v_ref, cc_ref, cr_ref, lse_ref, o_ref, do_ref, dr_ref, dk_ref, dv_ref, dc_ref):
        j, i = pl.program_id(0), pl.program_id(1)

        @pl.when(i == 0)
        def _():
            dk_ref[...] = jnp.zeros_like(dk_ref)
            dv_ref[...] = jnp.zeros_like(dv_ref)
            dc_ref[...] = jnp.zeros_like(dc_ref)

        @pl.when(i >= j)
        def _():
            qv, kv, vv = q_ref[...], k_ref[...], v_ref[...]
            for h in range(NH):
                p, ds, doh = _flash_probs(qv, kv, vv, cc_ref[...], cr_ref[...], lse_ref[...], do_ref[...], o_ref[...],
                                          h, i, j, t, drow=dr_ref[...])
                dv_ref[:, _hs(h)] += _dot_tn(p, doh)
                dk_ref[:, _hs(h)] += _dot_tn(ds, qv[:, _hs(h)])
                dc_ref[h:h + 1, :] += -jnp.sum(ds, axis=0, keepdims=True)

    qspec = lambda w: pl.BlockSpec((t, w), lambda j, i: (jnp.maximum(i, j), 0))
    kspec = lambda w: pl.BlockSpec((t, w), lambda j, i: (j, 0))
    return _call(
        body, name="flash_bwd_dkv", grid=(n, n),
        in_specs=[qspec(BW), kspec(BW), kspec(BW), qspec(128), pl.BlockSpec((8, t), lambda j, i: (0, j)),
                  qspec(128), qspec(BW), qspec(BW), qspec(128)],
        out_specs=[kspec(BW), kspec(BW), pl.BlockSpec((8, t), lambda j, i: (0, j))],
        out_shape=[jax.ShapeDtypeStruct((S, BW), F32), jax.ShapeDtypeStruct((S, BW), F32),
                   jax.ShapeDtypeStruct((8, S), F32)],
        sem=("parallel", "arbitrary"))(q, k, v, cc, cr, lse, o, do, drow)


def _attn_prep_bwd(z, fb, gq, gk, dq, dk, dv, dgate, dcc):
    S = z.shape[0]
    ts = _tile(S, ROW_TILE)
    nt = S // ts

    def body(zb_ref, zf_ref, fb_ref, gq_ref, gk_ref, dq_ref, dk_ref, dv_ref, dg_ref, dcc_ref, dzb_ref, dzf_ref,
             st_ref, carry):
        @pl.when(pl.program_id(0) == 0)
        def _():
            carry[...] = jnp.zeros_like(carry)
            st_ref[...] = jnp.zeros_like(st_ref)

        zb = zb_ref[...]
        gqv, gkv = gq_ref[...], gk_ref[...]
        dqv, dkv = dq_ref[...], dk_ref[...]
        sq = jnp.zeros((1, DH), F32)
        sk = jnp.zeros((1, DH), F32)
        for h in range(NH):
            dx, dgr = _rms_bwd(zb[:, _hs(h)], gqv, dqv[:, _hs(h)] * (DH ** -0.5))
            dzb_ref[:, _hs(h)] = _mx(dx)
            sq = sq + jnp.sum(dgr, axis=0, keepdims=True)
            ks = slice(BW + h * DH, BW + (h + 1) * DH)
            dx, dgr = _rms_bwd(zb[:, ks], gkv, dkv[:, _hs(h)])
            dzb_ref[:, ks] = _mx(dx)
            sk = sk + jnp.sum(dgr, axis=0, keepdims=True)
        dzb_ref[:, 2 * BW:3 * BW] = _mx(dv_ref[...])
        dzb_ref[:, 3 * BW:4 * BW] = dg_ref[...]
        dc = dcc_ref[...]
        dlf = _dot_exact_lhs01(_tri(ts, upper=True), dc) + carry[...]
        carry[...] = dlf[0:1, :]
        dfz = dlf * _sigmoid(-(zf_ref[...] + fb_ref[...]))
        dzf_ref[...] = _mx(dfz)
        st_ref[0:1, 0:DH] += sq
        st_ref[1:2, 0:DH] += sk
        st_ref[2:3, :] += jnp.sum(dfz, axis=0, keepdims=True)

    rev = lambda w, c=0: pl.BlockSpec((ts, w), lambda i: (nt - 1 - i, c))
    one = lambda w: pl.BlockSpec((1, w), lambda i: (0, 0))
    return _call(
        body, name="attn_prep_bwd", grid=(nt,),
        in_specs=[rev(1024, ZB // 1024), rev(128, ZF // 128), one(128), one(DH), one(DH),
                  rev(BW), rev(BW), rev(BW), rev(BW), rev(128)],
        out_specs=[rev(1024), rev(128), pl.BlockSpec((8, 128), lambda i: (0, 0))],
        out_shape=[jax.ShapeDtypeStruct((S, 1024), MXU_DTYPE), jax.ShapeDtypeStruct((S, 128), MXU_DTYPE),
                   jax.ShapeDtypeStruct((8, 128), F32)],
        scratch=[pltpu.VMEM((1, 128), F32)], sem=("arbitrary",))(z, z, fb, gq, gk, dq, dk, dv, dgate, dcc)


def _hgrn_consts():
    C = CHUNK
    t = np.arange(C)[:, None]
    j = np.arange(C)[None, :]
    mats = [(j <= t)]
    tq, tk, masks = [], [], []
    for lev in range(NLEV):
        m = C >> (lev + 1)
        blk, pos = t // (2 * m), t % (2 * m)
        mid = blk * 2 * m + m - 1
        tq.append((pos >= m) & (j > mid) & (j <= t))
        tk.append((pos < m) & (j > t) & (j <= mid))
        sblk, spos = j // (2 * m), j % (2 * m)
        masks.append((blk == sblk) & (pos >= m) & (spos < m))
    masks.append(t == j)
    tmat = np.concatenate(mats + tq + tk, axis=0).astype(np.float32)
    return tmat, np.stack(masks).astype(np.float32)


def _hgrn_chunk_fwd(zc, lb, tmat):
    cq, cf, ci = zc[:, 0:BW], zc[:, BW:2 * BW], zc[:, 2 * BW:3 * BW]
    q = _silu(cq)
    sg = _sigmoid(cf)
    g = lb + (1.0 - lb) * sg
    lg = jnp.log(g)
    kf = (1.0 - lb) * _sigmoid(-cf)
    e = _dot_exact_lhs01(tmat, lg)
    b = e[0:CHUNK]
    blast = b[CHUNK - 1:CHUNK, :]
    return cq, cf, q, sg, g, kf, ci, e, b, blast


def _round_mx(x):
    if MXU_DTYPE != jnp.bfloat16:
        return x
    u = lax.bitcast_convert_type(x, jnp.uint32)
    u = (u + jnp.uint32(0x7FFF) + ((u >> 16) & jnp.uint32(1))) & jnp.uint32(0xFFFF0000)
    return lax.bitcast_convert_type(u, F32)


def _hgrn_scores(q, kf, e, masks, h):
    qh, kh = q[:, _hs(h)], kf[:, _hs(h)]
    ql, kl = [], []
    a = None
    for lev in range(NLEV + 1):
        if lev < NLEV:
            eq = jnp.exp(e[(1 + lev) * CHUNK:(2 + lev) * CHUNK, _hs(h)])
            ek = jnp.exp(e[(1 + NLEV + lev) * CHUNK:(2 + NLEV + lev) * CHUNK, _hs(h)])
            ql.append((_round_mx(qh * eq), eq))
            kl.append((_round_mx(kh * ek), ek))
        else:
            ql.append((_round_mx(qh), None))
            kl.append((_round_mx(kh), None))
        term = masks[lev] * _dot_nt(ql[-1][0], kl[-1][0])
        a = term if a is None else a + term
    return a, ql, kl


def _hgrn_fwd(z, lb, gain, tmat, masks):
    S = z.shape[0]
    ts = _tile(S, ROW_TILE)
    nc = ts // CHUNK

    def body(zc_ref, lb_ref, gn_ref, tm_ref, mk_ref, o_ref, st_ref, y_ref, state):
        @pl.when(pl.program_id(0) == 0)
        def _():
            state[...] = jnp.zeros_like(state)

        lbv, gn, tm, mk = lb_ref[...], gn_ref[...], tm_ref[...], mk_ref[...]

        def chunk(c, carry):
            r0 = pl.multiple_of(c * CHUNK, CHUNK)
            zc = zc_ref[pl.ds(r0, CHUNK), :]
            cq, cf, q, sg, g, kf, v, e, b, blast = _hgrn_chunk_fwd(zc, lbv, tm)
            qe = q * jnp.exp(b)
            kd = kf * jnp.exp(blast - b)
            st_ref[pl.ds(r0, CHUNK), :] = state[...]
            for h in range(NH):
                sth = state[:, _hs(h)]
                a, _, _ = _hgrn_scores(q, kf, e, mk, h)
                oh = _dot_nt(qe[:, _hs(h)], sth) + _dot(a, v[:, _hs(h)])
                state[:, _hs(h)] = sth * jnp.exp(blast[:, _hs(h)]) + _dot_tn(v[:, _hs(h)], kd[:, _hs(h)])
                o_ref[pl.ds(r0, CHUNK), _hs(h)] = oh
                yn = _rms_fwd(oh, gn[:, _hs(h)])
                y_ref[pl.ds(r0, CHUNK), _hs(h)] = _mx(yn * _silu(zc[:, 3 * BW + h * DH:3 * BW + (h + 1) * DH]))
            return carry

        lax.fori_loop(0, nc, chunk, 0)

    row = pl.BlockSpec((ts, BW), lambda i: (i, 0))
    one = pl.BlockSpec((1, BW), lambda i: (0, 0))
    return _call(
        body, name="hgrn_fwd", grid=(S // ts,),
        in_specs=[pl.BlockSpec((ts, 1024), lambda i: (i, ZC // 1024)), one, one,
                  pl.BlockSpec(tmat.shape, lambda i: (0, 0)), pl.BlockSpec(masks.shape, lambda i: (0, 0, 0))],
        out_specs=[row, row, row],
        out_shape=[jax.ShapeDtypeStruct((S, BW), F32), jax.ShapeDtypeStruct((S, BW), F32),
                   jax.ShapeDtypeStruct((S, BW), MXU_DTYPE)],
        scratch=[pltpu.VMEM((CHUNK, BW), F32)], sem=("arbitrary",))(z, lb, gain, tmat, masks)


def _hgrn_bwd(z, lb, gain, tmat, masks, o_pre, states, dy):
    S = z.shape[0]
    ts = _tile(S, ROW_TILE)
    nt = S // ts
    nc = ts // CHUNK

    def body(zc_ref, lb_ref, gn_ref, tm_ref, mk_ref, o_ref, st_ref, dy_ref, dz_ref, stat_ref, dstate):
        @pl.when(pl.program_id(0) == 0)
        def _():
            dstate[...] = jnp.zeros_like(dstate)
            stat_ref[...] = jnp.zeros_like(stat_ref)

        lbv, gn, tm, mk = lb_ref[...], gn_ref[...], tm_ref[...], mk_ref[...]
        upper = _tri(CHUNK, upper=True)
        lower_strict = 1.0 - upper

        def chunk(cc, carry):
            c = nc - 1 - cc
            r0 = pl.multiple_of(c * CHUNK, CHUNK)
            zc = zc_ref[pl.ds(r0, CHUNK), :]
            cq, cf, q, sg, g, kf, v, e, b, blast = _hgrn_chunk_fwd(zc, lbv, tm)
            eb = jnp.exp(b)
            ebl = jnp.exp(blast - b)
            qe = q * eb
            kd = kf * ebl
            o = o_ref[pl.ds(r0, CHUNK), :]
            dyv = dy_ref[pl.ds(r0, CHUNK), :]
            stp = st_ref[pl.ds(r0, CHUNK), :]
            cg = zc[:, 3 * BW:4 * BW]
            sgate = _silu(cg)
            dq_parts, dk_parts, dv_parts, dcg_parts = [], [], [], []
            dgain, up_parts, lo_parts, const_parts = [], [], [], []
            for h in range(NH):
                hs = _hs(h)
                oh = o[:, hs]
                r = lax.rsqrt(jnp.mean(oh * oh, axis=-1, keepdims=True) + EPS)
                nrm = oh * r
                dyn = dyv[:, hs] * sgate[:, hs]
                dcg_parts.append(dyv[:, hs] * nrm * gn[:, hs] * _dsilu(cg[:, hs]))
                dgain.append(jnp.sum(dyn * nrm, axis=0, keepdims=True))
                tt = dyn * gn[:, hs]
                doh = r * (tt - nrm * jnp.mean(tt * nrm, axis=-1, keepdims=True))
                a, ql, kl = _hgrn_scores(q, kf, e, mk, h)
                da = _dot_nt(doh, v[:, hs])
                dsth = dstate[:, hs]
                ebh = jnp.exp(blast[:, hs])
                dv_parts.append(_dot_tn(a, doh) + _dot_nt(kd[:, hs], dsth))
                dq_inter = eb[:, hs] * _dot(doh, stp[:, hs])
                dk_state = ebl[:, hs] * _dot(v[:, hs], dsth)
                dqh, dkh, gh = dq_inter, dk_state, None
                for lev in range(NLEV + 1):
                    dal = mk[lev] * da
                    xq = _dot(dal, kl[lev][0])
                    yk = _dot_tn(dal, ql[lev][0])
                    gterm = ql[lev][0] * xq - kl[lev][0] * yk
                    gh = gterm if gh is None else gh + gterm
                    dqh = dqh + (xq if lev == NLEV else ql[lev][1] * xq)
                    dkh = dkh + (yk if lev == NLEV else kl[lev][1] * yk)
                up_parts.append(gh + q[:, hs] * dq_inter)
                lo_parts.append(kf[:, hs] * dk_state)
                const_parts.append(jnp.sum(dsth * stp[:, hs], axis=0, keepdims=True) * ebh)
                dstate[:, hs] = dsth * ebh + _dot_tn(doh, qe[:, hs])
                dq_parts.append(dqh)
                dk_parts.append(dkh)
            dq = jnp.concatenate(dq_parts, axis=1)
            dk = jnp.concatenate(dk_parts, axis=1)
            dlg = (_dot_exact_lhs01(upper, jnp.concatenate(up_parts, axis=1))
                   + _dot_exact_lhs01(lower_strict, jnp.concatenate(lo_parts, axis=1))
                   + jnp.concatenate(const_parts, axis=1))
            dsg = sg * (1.0 - sg)
            dz_ref[pl.ds(r0, CHUNK), 0:BW] = _mx(dq * _dsilu(cq))
            dz_ref[pl.ds(r0, CHUNK), BW:2 * BW] = _mx((dlg / g - dk) * (1.0 - lbv) * dsg)
            dz_ref[pl.ds(r0, CHUNK), 2 * BW:3 * BW] = _mx(jnp.concatenate(dv_parts, axis=1))
            dz_ref[pl.ds(r0, CHUNK), 3 * BW:4 * BW] = _mx(jnp.concatenate(dcg_parts, axis=1))
            stat_ref[0:1, :] += jnp.concatenate(dgain, axis=1)
            stat_ref[1:2, :] += jnp.sum((dlg / g - dk) * (1.0 - sg), axis=0, keepdims=True)
            return carry

        lax.fori_loop(0, nc, chunk, 0)

    rev = lambda w, c=0: pl.BlockSpec((ts, w), lambda i: (nt - 1 - i, c))
    one = pl.BlockSpec((1, BW), lambda i: (0, 0))
    return _call(
        body, name="hgrn_bwd", grid=(nt,),
        in_specs=[rev(1024, ZC // 1024), one, one, pl.BlockSpec(tmat.shape, lambda i: (0, 0)),
                  pl.BlockSpec(masks.shape, lambda i: (0, 0, 0)), rev(BW), rev(BW), rev(BW)],
        out_specs=[rev(1024), pl.BlockSpec((8, BW), lambda i: (0, 0))],
        out_shape=[jax.ShapeDtypeStruct((S, 1024), MXU_DTYPE), jax.ShapeDtypeStruct((8, BW), F32)],
        scratch=[pltpu.VMEM((CHUNK, BW), F32)],
        sem=("arbitrary",))(z, lb, gain, tmat, masks, o_pre, states, dy)


def _lower_bounds_fwd(lb_logits):
    def body(l_ref, o_ref):
        l = l_ref[...]
        m = jnp.max(l, axis=0, keepdims=True)
        ex = jnp.exp(l - m)
        p = ex / jnp.sum(ex, axis=0, keepdims=True)
        cs = p[0:1, :]
        o_ref[0:1, :] = jnp.clip(cs - p[0:1, :], 0.0, 1.0)
        for d in range(1, DEPTH):
            cs = cs + p[d:d + 1, :]
            o_ref[d:d + 1, :] = jnp.clip(cs - p[0:1, :], 0.0, 1.0)

    full = pl.BlockSpec((DEPTH, BW), lambda: (0, 0))
    return _call(body, name="lower_bounds_fwd", grid=(), in_specs=[full], out_specs=full,
                 out_shape=jax.ShapeDtypeStruct((DEPTH, BW), F32))(lb_logits)


def _lower_bounds_bwd(lb_logits, dlow):
    def body(l_ref, d_ref, o_ref):
        l = l_ref[...]
        m = jnp.max(l, axis=0, keepdims=True)
        ex = jnp.exp(l - m)
        p = ex / jnp.sum(ex, axis=0, keepdims=True)
        dl = d_ref[...]
        cs = p[0:1, :]
        dcs = []
        for d in range(DEPTH):
            if d > 0:
                cs = cs + p[d:d + 1, :]
            val = cs - p[0:1, :]
            dcs.append(jnp.where((val > 0.0) & (val < 1.0), dl[d:d + 1, :], 0.0))
        total = dcs[0] + dcs[1] + dcs[2] + dcs[3]
        dp = []
        for j in range(DEPTH):
            s = dcs[j]
            for d in range(j + 1, DEPTH):
                s = s + dcs[d]
            dp.append(s - total if j == 0 else s)
        inner = p[0:1, :] * dp[0]
        for j in range(1, DEPTH):
            inner = inner + p[j:j + 1, :] * dp[j]
        for j in range(DEPTH):
            o_ref[j:j + 1, :] = p[j:j + 1, :] * (dp[j] - inner)

    full = pl.BlockSpec((DEPTH, BW), lambda: (0, 0))
    return _call(body, name="lower_bounds_bwd", grid=(), in_specs=[full, full], out_specs=full,
                 out_shape=jax.ShapeDtypeStruct((DEPTH, BW), F32))(lb_logits, dlow)


def _sgu_fwd(z, gv, ws, bs):
    S = z.shape[0]
    ts = _tile(S, ROW_TILE)
    nc = ts // SCHUNK

    def body(zd_ref, gv_ref, ws_ref, bs_ref, y_ref):
        gvv, bsv = gv_ref[...], bs_ref[...]
        tril = _tri(SCHUNK)
        for c in range(nc):
            rs = slice(c * SCHUNK, (c + 1) * SCHUNK)
            zd = zd_ref[rs, :]
            for h in range(NH):
                vn = _rms_fwd(zd[:, BW + h * DH:BW + (h + 1) * DH], gvv[:, _hs(h)])
                s = _dot(ws_ref[h] * tril, vn) + bsv[:, h:h + 1]
                y_ref[rs, _hs(h)] = _mx(zd[:, _hs(h)] * s * _silu(zd[:, 2 * BW + h * DH:2 * BW + (h + 1) * DH]))

    return _call(
        body, name="sgu_fwd", grid=(S // ts,),
        in_specs=[pl.BlockSpec((ts, 1024), lambda i: (i, ZD // 1024)), pl.BlockSpec((1, BW), lambda i: (0, 0)),
                  pl.BlockSpec((NH, SCHUNK, SCHUNK), lambda i: (0, 0, 0)), pl.BlockSpec((SCHUNK, 128), lambda i: (0, 0))],
        out_specs=pl.BlockSpec((ts, BW), lambda i: (i, 0)),
        out_shape=jax.ShapeDtypeStruct((S, BW), MXU_DTYPE), sem=("parallel",))(z, gv, ws, bs)


def _sgu_bwd(z, gv, ws, bs, dy):
    S = z.shape[0]
    ts = _tile(S, ROW_TILE)
    nc = ts // SCHUNK

    def body(zd_ref, gv_ref, ws_ref, bs_ref, dy_ref, dz_ref, dws_ref, dbs_ref, st_ref):
        @pl.when(pl.program_id(0) == 0)
        def _():
            dws_ref[...] = jnp.zeros_like(dws_ref)
            dbs_ref[...] = jnp.zeros_like(dbs_ref)
            st_ref[...] = jnp.zeros_like(st_ref)

        gvv, bsv = gv_ref[...], bs_ref[...]
        tril = _tri(SCHUNK)
        dz_ref[:, 3 * BW:4 * BW] = jnp.zeros((ts, BW), MXU_DTYPE)
        for c in range(nc):
            rs = slice(c * SCHUNK, (c + 1) * SCHUNK)
            zd = zd_ref[rs, :]
            dyv = dy_ref[rs, :]
            for h in range(NH):
                hs = _hs(h)
                u = zd[:, hs]
                vraw = zd[:, BW + h * DH:BW + (h + 1) * DH]
                gt = zd[:, 2 * BW + h * DH:2 * BW + (h + 1) * DH]
                gvh = gvv[:, hs]
                vn = _rms_fwd(vraw, gvh)
                wm = ws_ref[h] * tril
                s = _dot(wm, vn) + bsv[:, h:h + 1]
                sil = _silu(gt)
                d = dyv[:, hs]
                ds = d * u * sil
                dz_ref[rs, hs] = _mx(d * s * sil)
                dz_ref[rs, 2 * BW + h * DH:2 * BW + (h + 1) * DH] = _mx(d * u * s * _dsilu(gt))
                dws_ref[h] += tril * _dot_nt(ds, vn)
                dbs_ref[:, h:h + 1] += jnp.sum(ds, axis=-1, keepdims=True)
                dvn = _dot_tn(wm, ds)
                dx, dgr = _rms_bwd(vraw, gvh, dvn)
                dz_ref[rs, BW + h * DH:BW + (h + 1) * DH] = _mx(dx)
                st_ref[0:1, hs] += jnp.sum(dgr, axis=0, keepdims=True)

    return _call(
        body, name="sgu_bwd", grid=(S // ts,),
        in_specs=[pl.BlockSpec((ts, 1024), lambda i: (i, ZD // 1024)), pl.BlockSpec((1, BW), lambda i: (0, 0)),
                  pl.BlockSpec((NH, SCHUNK, SCHUNK), lambda i: (0, 0, 0)), pl.BlockSpec((SCHUNK, 128), lambda i: (0, 0)),
                  pl.BlockSpec((ts, BW), lambda i: (i, 0))],
        out_specs=[pl.BlockSpec((ts, 1024), lambda i: (i, 0)), pl.BlockSpec((NH, SCHUNK, SCHUNK), lambda i: (0, 0, 0)),
                   pl.BlockSpec((SCHUNK, 128), lambda i: (0, 0)), pl.BlockSpec((8, BW), lambda i: (0, 0))],
        out_shape=[jax.ShapeDtypeStruct((S, 1024), MXU_DTYPE), jax.ShapeDtypeStruct((NH, SCHUNK, SCHUNK), F32),
                   jax.ShapeDtypeStruct((SCHUNK, 128), F32), jax.ShapeDtypeStruct((8, BW), F32)],
        sem=("arbitrary",))(z, gv, ws, bs, dy)


def _merge_fwd(x, z, ys, wup, mb, wo):
    S = x.shape[0]
    ts = _tile(S, MERGE_TILE)

    def body(x_ref, zm_ref, ya_ref, yb_ref, yc_ref, yd_ref, wup_ref, mb_ref, wo_ref, x1_ref, mg_ref):
        yrefs = (ya_ref, yb_ref, yc_ref, yd_ref)
        mbv = mb_ref[...]
        merged = None
        for b in range(NBR):
            cs = slice(b * D_MODEL, (b + 1) * D_MODEL)
            term = _sigmoid(zm_ref[:, cs] + mbv[b:b + 1, :]) * jnp.dot(yrefs[b][...], wup_ref[b],
                                                                      preferred_element_type=F32)
            merged = term if merged is None else merged + term
        mg = _mx(merged)
        mg_ref[...] = mg
        x1_ref[...] = x_ref[...] + jnp.dot(mg, wo_ref[...], preferred_element_type=F32)

    row = lambda w: pl.BlockSpec((ts, w), lambda i: (i, 0))
    return _call(
        body, name="merge_fwd", grid=(S // ts,),
        in_specs=[row(D_MODEL), pl.BlockSpec((ts, 4096), lambda i: (i, 0)), row(BW), row(BW), row(BW), row(BW),
                  pl.BlockSpec((NBR, BW, D_MODEL), lambda i: (0, 0, 0)), pl.BlockSpec((NBR, D_MODEL), lambda i: (0, 0)),
                  pl.BlockSpec((D_MODEL, D_MODEL), lambda i: (0, 0))],
        out_specs=[row(D_MODEL), row(D_MODEL)],
        out_shape=[jax.ShapeDtypeStruct((S, D_MODEL), F32), jax.ShapeDtypeStruct((S, D_MODEL), MXU_DTYPE)],
        sem=("parallel",))(x, z, *ys, wup, mb, wo)


def _merge_bwd(dx1, z, ys, wup, wup_t, mb, wo_t):
    S = dx1.shape[0]
    ts = _tile(S, MERGE_TILE)

    def body(dx_ref, zm_ref, ya_ref, yb_ref, yc_ref, yd_ref, wup_ref, wupt_ref, mb_ref, wot_ref,
             dzm_ref, du_ref, dxb_ref, dy_ref, st_ref):
        @pl.when(pl.program_id(0) == 0)
        def _():
            st_ref[...] = jnp.zeros_like(st_ref)

        yrefs = (ya_ref, yb_ref, yc_ref, yd_ref)
        mbv = mb_ref[...]
        dxb = _mx(dx_ref[...])
        dxb_ref[...] = dxb
        dmerged = jnp.dot(dxb, wot_ref[...], preferred_element_type=F32)
        for b in range(NBR):
            cs = slice(b * D_MODEL, (b + 1) * D_MODEL)
            u = jnp.dot(yrefs[b][...], wup_ref[b], preferred_element_type=F32)
            sg = _sigmoid(zm_ref[:, cs] + mbv[b:b + 1, :])
            du = _mx(dmerged * sg)
            du_ref[:, cs] = du
            dzm = dmerged * u * sg * (1.0 - sg)
            dzm_ref[:, cs] = _mx(dzm)
            st_ref[b:b + 1, :] += jnp.sum(dzm, axis=0, keepdims=True)
            dy_ref[:, b * BW:(b + 1) * BW] = jnp.dot(du, wupt_ref[b], preferred_element_type=F32)

    row = lambda w: pl.BlockSpec((ts, w), lambda i: (i, 0))
    return _call(
        body, name="merge_bwd", grid=(S // ts,),
        in_specs=[row(D_MODEL), pl.BlockSpec((ts, 4096), lambda i: (i, 0)), row(BW), row(BW), row(BW), row(BW),
                  pl.BlockSpec((NBR, BW, D_MODEL), lambda i: (0, 0, 0)), pl.BlockSpec((NBR, D_MODEL, BW), lambda i: (0, 0, 0)),
                  pl.BlockSpec((NBR, D_MODEL), lambda i: (0, 0)), pl.BlockSpec((D_MODEL, D_MODEL), lambda i: (0, 0))],
        out_specs=[row(4096), row(4096), row(D_MODEL), row(D_MODEL), pl.BlockSpec((8, D_MODEL), lambda i: (0, 0))],
        out_shape=[jax.ShapeDtypeStruct((S, 4096), MXU_DTYPE), jax.ShapeDtypeStruct((S, 4096), MXU_DTYPE),
                   jax.ShapeDtypeStruct((S, D_MODEL), MXU_DTYPE), jax.ShapeDtypeStruct((S, D_MODEL), F32),
                   jax.ShapeDtypeStruct((8, D_MODEL), F32)],
        sem=("arbitrary",))(dx1, z, *ys, wup, wup_t, mb, wo_t)


def _ple_fwd(x1, p, g, wg, wp):
    S = x1.shape[0]
    ts = _tile(S, ROW_TILE)

    def body(x_ref, p_ref, g_ref, wg_ref, wp_ref, o_ref):
        x = x_ref[...]
        hp = _mx(_rms_fwd(x, g_ref[...]))
        gate = _sigmoid(jnp.dot(hp, wg_ref[...], preferred_element_type=F32))
        pp = jnp.dot(_mx(p_ref[...]), wp_ref[...], preferred_element_type=F32)
        o_ref[...] = x + gate * pp

    row = lambda w: pl.BlockSpec((ts, w), lambda i: (i, 0))
    return _call(
        body, name="ple_fwd", grid=(S // ts,),
        in_specs=[row(D_MODEL), row(PLE), pl.BlockSpec((1, D_MODEL), lambda i: (0, 0)),
                  pl.BlockSpec((D_MODEL, D_MODEL), lambda i: (0, 0)), pl.BlockSpec((PLE, D_MODEL), lambda i: (0, 0))],
        out_specs=row(D_MODEL), out_shape=jax.ShapeDtypeStruct((S, D_MODEL), F32),
        sem=("parallel",))(x1, p, g, wg, wp)


def _ple_bwd(x1, p, dx2, g, wg, wg_t, wp):
    S = x1.shape[0]
    ts = _tile(S, ROW_TILE)

    def body(x_ref, p_ref, dx_ref, g_ref, wg_ref, wgt_ref, wp_ref, dx1_ref, hp_ref, dgl_ref, dpp_ref, pb_ref, st_ref):
        @pl.when(pl.program_id(0) == 0)
        def _():
            st_ref[...] = jnp.zeros_like(st_ref)

        x, gv, dx2 = x_ref[...], g_ref[...], dx_ref[...]
        hp = _mx(_rms_fwd(x, gv))
        hp_ref[...] = hp
        gate = _sigmoid(jnp.dot(hp, wg_ref[...], preferred_element_type=F32))
        pb = _mx(p_ref[...])
        pb_ref[...] = pb
        pp = jnp.dot(pb, wp_ref[...], preferred_element_type=F32)
        dgl = _mx(dx2 * pp * gate * (1.0 - gate))
        dgl_ref[...] = dgl
        dpp_ref[...] = _mx(dx2 * gate)
        dhp = jnp.dot(dgl, wgt_ref[...], preferred_element_type=F32)
        dxn, dgr = _rms_bwd(x, gv, dhp)
        dx1_ref[...] = dx2 + dxn
        st_ref[0:1, :] += jnp.sum(dgr, axis=0, keepdims=True)

    row = lambda w: pl.BlockSpec((ts, w), lambda i: (i, 0))
    sq = pl.BlockSpec((D_MODEL, D_MODEL), lambda i: (0, 0))
    return _call(
        body, name="ple_bwd", grid=(S // ts,),
        in_specs=[row(D_MODEL), row(PLE), row(D_MODEL), pl.BlockSpec((1, D_MODEL), lambda i: (0, 0)), sq, sq,
                  pl.BlockSpec((PLE, D_MODEL), lambda i: (0, 0))],
        out_specs=[row(D_MODEL), row(D_MODEL), row(D_MODEL), row(D_MODEL), row(PLE),
                   pl.BlockSpec((8, D_MODEL), lambda i: (0, 0))],
        out_shape=[jax.ShapeDtypeStruct((S, D_MODEL), F32)] + [jax.ShapeDtypeStruct((S, D_MODEL), MXU_DTYPE)] * 3
        + [jax.ShapeDtypeStruct((S, PLE), MXU_DTYPE), jax.ShapeDtypeStruct((8, D_MODEL), F32)],
        sem=("arbitrary",))(x1, p, dx2, g, wg, wg_t, wp)


def _pad_rows(a, rows=8):
    return jnp.concatenate([a, jnp.zeros((rows - a.shape[0],) + a.shape[1:], a.dtype)], axis=0)


def _pad_lanes(a, lanes=128):
    return jnp.concatenate([a, jnp.zeros(a.shape[:-1] + (lanes - a.shape[-1],), a.dtype)], axis=-1)


def _wz_from_w_in(w):
    zeros = lambda n: jnp.zeros((w.shape[0], n), w.dtype)
    return jnp.concatenate([w[:, _OM:_OEND], w[:, _OA:_OB], w[:, _OB:_OF], w[:, _OC:_OD], w[:, _OD:_OM], zeros(256),
                            w[:, _OF:_OC], zeros(124)], axis=1)


def _w_in_from_wz(g):
    return jnp.concatenate([g[:, ZA:ZB], g[:, ZB:ZC], g[:, ZF:ZF + 4], g[:, ZC:ZD], g[:, ZD:ZD + 768], g[:, ZM:ZA]], axis=1)


def _local_step(x, p, target, wts):
    tmat_np, masks_np = _hgrn_consts()
    tmat = jnp.asarray(tmat_np, MXU_DTYPE)
    masks = jnp.asarray(masks_np, F32)
    lower = _lower_bounds_fwd(wts["lb_logits"])
    saved = []
    for li in range(DEPTH):
        wz = _wz_from_w_in(wts["w_in"][li])
        g_mix = wts["norm_mix"][li][None, :]
        h = _norm_fwd(x, g_mix)
        z = _mm_nn(h, wz, F32, name="mm_z", tn=1664)
        cw = _pad_rows(jnp.concatenate([wts["conv_w"][li], wts["conv_b"][li][None, :]], axis=0))
        ya = _conv_fwd(z, cw)
        fb = _pad_lanes(wts["fgate_bias"][li][None, :])
        gq, gk = wts["q_norm"][li][None, :], wts["k_norm"][li][None, :]
        qs, kn, vb, cc, cr = _attn_prep_fwd(z, fb, gq, gk)
        o, lse, yb = _flash_fwd(qs, kn, vb, cc, cr, z)
        lb = lower[li][None, :]
        gh = wts["hgrn_norm"][li][None, :]
        o_pre, states, yc = _hgrn_fwd(z, lb, gh, tmat, masks)
        gv = wts["sgu_norm"][li][None, :]
        ws = wts["spatial_w"][li]
        bs = _pad_lanes(wts["spatial_b"][li].T)
        yd = _sgu_fwd(z, gv, ws, bs)
        ys = (ya, yb, yc, yd)
        x1, merged = _merge_fwd(x, z, ys, wts["w_up"][li], wts["merge_b"][li], wts["w_o"][li])
        g_ple = wts["norm_ple"][li][None, :]
        x2 = _ple_fwd(x1, p[li], g_ple, wts["w_ple_gate"][li], wts["w_ple_proj"][li])
        saved.append(dict(x=x, h=h, z=z, wz=wz, cw=cw, fb=fb, gq=gq, gk=gk, qs=qs, kn=kn, vb=vb, cc=cc, cr=cr, o=o, lse=lse,
                          lb=lb, gh=gh, o_pre=o_pre, states=states, gv=gv, ws=ws, bs=bs, ys=ys, x1=x1, merged=merged,
                          g_mix=g_mix, g_ple=g_ple))
        x = x2

    loss, dx = _loss_fwd_bwd(x, target)

    names = ["norm_mix", "w_in", "conv_w", "conv_b", "fgate_bias", "q_norm", "k_norm", "lb", "hgrn_norm", "sgu_norm",
             "spatial_w", "spatial_b", "w_up", "merge_b", "w_o", "norm_ple", "w_ple_gate", "w_ple_proj"]
    gl = {n: [None] * DEPTH for n in names}
    for li in reversed(range(DEPTH)):
        s = saved[li]
        z = s["z"]
        wg, wp = wts["w_ple_gate"][li], wts["w_ple_proj"][li]
        dx1, hp, dgl, dpp, pb, st = _ple_bwd(s["x1"], p[li], dx, s["g_ple"], wg, wg.T, wp)
        gl["norm_ple"][li] = st[0]
        gl["w_ple_gate"][li] = _mm_tn(hp, dgl, name="mm_dwg")
        gl["w_ple_proj"][li] = _mm_tn(pb, dpp, name="mm_dwp")
        wup, wo = wts["w_up"][li], wts["w_o"][li]
        dzm, du, dxb, dy, st = _merge_bwd(dx1, z, s["ys"], wup, jnp.swapaxes(wup, 1, 2), wts["merge_b"][li], wo.T)
        gl["merge_b"][li] = st[0:NBR]
        gl["w_o"][li] = _mm_tn(s["merged"], dxb, name="mm_dwo")
        gl["w_up"][li] = jnp.stack([_mm_tn(s["ys"][b], du, name="mm_dwup", ycol=b, n=D_MODEL) for b in range(NBR)])
        dza, st = _conv_bwd(z, s["cw"], dy[:, 0:BW])
        gl["conv_w"][li] = st[0:3]
        gl["conv_b"][li] = st[3]
        do, dgate = _attn_gate_bwd(dy[:, BW:2 * BW], s["o"], z)
        fa = (s["qs"], s["kn"], s["vb"], s["cc"], s["cr"], s["lse"], s["o"], do)
        dq, drow = _flash_bwd_dq(*fa)
        dk, dv, dcr = _flash_bwd_dkv(*fa, drow)
        dcc = _pad_lanes(dcr.T)
        dzb, dzf, st = _attn_prep_bwd(z, s["fb"], s["gq"], s["gk"], dq, dk, dv, dgate, dcc)
        gl["q_norm"][li] = st[0, 0:DH]
        gl["k_norm"][li] = st[1, 0:DH]
        gl["fgate_bias"][li] = st[2, 0:NH]
        dzc, st = _hgrn_bwd(z, s["lb"], s["gh"], tmat, masks, s["o_pre"], s["states"], dy[:, 2 * BW:3 * BW])
        gl["hgrn_norm"][li] = st[0]
        gl["lb"][li] = st[1]
        dzd, dws, dbs, st = _sgu_bwd(z, s["gv"], s["ws"], s["bs"], dy[:, 3 * BW:4 * BW])
        gl["sgu_norm"][li] = st[0]
        gl["spatial_w"][li] = dws
        gl["spatial_b"][li] = dbs[:, 0:NH].T
        dz = jnp.concatenate([dzm, dza, dzb, dzc, dzd, dzf], axis=1)
        gl["w_in"][li] = _w_in_from_wz(_mm_tn(s["h"], dz, name="mm_dwz", tn=1664))
        dh = _mm_nn(dz, s["wz"].T, F32, name="mm_dh", tk=1664)
        dx, st = _norm_bwd(s["x"], s["g_mix"], dh, dx1)
        gl["norm_mix"][li] = st[0]

    grads = {n: jnp.stack(v) for n, v in gl.items()}
    grads["lb_logits"] = _lower_bounds_bwd(wts["lb_logits"], grads.pop("lb"))
    return loss, dx, grads


def _my_pos():
    return lax.axis_index("x"), lax.axis_index("y"), lax.axis_index("c")


def _gather_xy(shards):
    n = len(shards)

    def body(*refs):
        ins, outs = refs[:n], refs[n:2 * n]
        send, recv, lsem = refs[2 * n:]
        x, y, c = _my_pos()
        me = 2 * x + y
        peers = [(1 - x, y), (x, 1 - y), (1 - x, 1 - y)]
        copies = []
        for t in range(n):
            loc = pltpu.make_async_copy(ins[t], outs[t].at[me], lsem.at[t])
            loc.start()
            copies.append(loc)
            for j, (px, py) in enumerate(peers):
                cp = pltpu.make_async_remote_copy(src_ref=ins[t], dst_ref=outs[t].at[me], send_sem=send.at[t, j],
                                                  recv_sem=recv.at[t, j], device_id=(px, py, c), device_id_type=MESH)
                cp.start()
                copies.append(cp)
        for cp in copies:
            cp.wait()

    hbm = pl.BlockSpec(memory_space=pl.ANY)
    return pl.pallas_call(
        functools.partial(body), name="gather_weights",
        in_specs=[hbm] * n, out_specs=[hbm] * n,
        out_shape=[jax.ShapeDtypeStruct((4,) + a.shape, a.dtype) for a in shards],
        scratch_shapes=[pltpu.SemaphoreType.DMA((n, 3)), pltpu.SemaphoreType.DMA((n, 3)), pltpu.SemaphoreType.DMA((n,))],
        )(*shards)


def _exchange_slabs(slabs):
    n = len(slabs)

    def body(*refs):
        ins, outs = refs[:n], refs[n:2 * n]
        send, recv, lsem = refs[2 * n:]
        x, y, c = _my_pos()
        me = 2 * x + y
        sib = (x, y, 1 - c)
        chips = [(1 - x, y), (x, 1 - y), (1 - x, 1 - y)]

        def rc(t, k, src, dst, dev):
            return pltpu.make_async_remote_copy(src_ref=src, dst_ref=dst, send_sem=send.at[t, k], recv_sem=recv.at[t, k],
                                                device_id=dev, device_id_type=MESH)

        first, passed, local = [], [], []
        for t in range(n):
            loc = pltpu.make_async_copy(ins[t].at[me], outs[t].at[2 * me + c], lsem.at[t])
            loc.start()
            local.append(loc)
            first.append(rc(t, 0, ins[t].at[me], outs[t].at[2 * me + c], sib))
            for j, (px, py) in enumerate(chips):
                first.append(rc(t, 1 + j, ins[t].at[2 * px + py], outs[t].at[2 * me + c], (px, py, c)))
        for cp in first:
            cp.start()
        for t in range(n):
            for j, (px, py) in enumerate(chips):
                slot = outs[t].at[2 * (2 * px + py) + c]
                rc(t, 1 + j, slot, slot, (x, y, c)).wait_recv()
                fwd = rc(t, 4 + j, slot, slot, sib)
                fwd.start()
                passed.append(fwd)
        for t in range(n):
            s0 = outs[t].at[2 * me + (1 - c)]
            rc(t, 0, s0, s0, (x, y, c)).wait_recv()
            for j, (px, py) in enumerate(chips):
                slot = outs[t].at[2 * (2 * px + py) + (1 - c)]
                rc(t, 4 + j, slot, slot, (x, y, c)).wait_recv()
        for cp in first + passed:
            cp.wait_send()
        for loc in local:
            loc.wait()

    hbm = pl.BlockSpec(memory_space=pl.ANY)
    return pl.pallas_call(
        functools.partial(body), name="exchange_grads",
        in_specs=[hbm] * n, out_specs=[hbm] * n,
        out_shape=[jax.ShapeDtypeStruct((8,) + a.shape[1:], a.dtype) for a in slabs],
        scratch_shapes=[pltpu.SemaphoreType.DMA((n, 7)), pltpu.SemaphoreType.DMA((n, 7)), pltpu.SemaphoreType.DMA((n,))],
        )(*slabs)


def _allreduce_small(v):
    R = v.shape[0]

    def body(v_ref, o_ref, buf, send, recv):
        x, y, c = _my_pos()
        me = 4 * x + 2 * y + c
        buf[me] = v_ref[...]
        copies = []
        k = 0
        for dx in range(2):
            for dy in range(2):
                for dc in range(2):
                    if dx == 0 and dy == 0 and dc == 0:
                        continue
                    cp = pltpu.make_async_remote_copy(
                        src_ref=v_ref, dst_ref=buf.at[me], send_sem=send.at[k - 0], recv_sem=recv.at[k],
                        device_id=(jnp.bitwise_xor(x, dx), jnp.bitwise_xor(y, dy), jnp.bitwise_xor(c, dc)),
                        device_id_type=MESH)
                    cp.start()
                    copies.append(cp)
                    k += 1
        for cp in copies:
            cp.wait()
        acc = buf[0]
        for d in range(1, 8):
            acc = acc + buf[d]
        o_ref[...] = acc

    vm = pl.BlockSpec(memory_space=pltpu.VMEM)
    return pl.pallas_call(
        functools.partial(body), name="allreduce_small", in_specs=[vm], out_specs=vm,
        out_shape=jax.ShapeDtypeStruct((R, 128), F32),
        scratch_shapes=[pltpu.VMEM((8, R, 128), F32), pltpu.SemaphoreType.DMA((7,)), pltpu.SemaphoreType.DMA((7,))],
        compiler_params=pltpu.CompilerParams(vmem_limit_bytes=VMEM_LIMIT))(v)


def _adamw(w, m, v, parts, nparts):
    A, R, C = w.shape
    tr = R if R <= 128 else _tile(R, 128)

    def body(w_ref, m_ref, v_ref, p_ref, g_ref, d_ref, nm_ref, nv_ref):
        g = p_ref[0, 0]
        for k in range(1, nparts):
            g = g + p_ref[k, 0]
        mm = ADAM_B1 * m_ref[0] + (1.0 - ADAM_B1) * g
        vv = ADAM_B2 * v_ref[0] + (1.0 - ADAM_B2) * jnp.square(g)
        m_hat = mm / (1.0 - ADAM_B1 ** ADAM_STEP)
        v_hat = vv / (1.0 - ADAM_B2 ** ADAM_STEP)
        g_ref[0] = g
        d_ref[0] = -ADAM_LR * (m_hat / (jnp.sqrt(v_hat) + ADAM_EPS) + ADAM_WD * w_ref[0])
        nm_ref[0] = mm
        nv_ref[0] = vv

    blk = pl.BlockSpec((1, tr, C), lambda a, r: (a, r, 0))
    return _call(
        body, name="adamw", grid=(A, R // tr),
        in_specs=[blk, blk, blk, pl.BlockSpec((nparts, 1, tr, C), lambda a, r: (0, a, r, 0))],
        out_specs=[blk] * 4, out_shape=[jax.ShapeDtypeStruct((A, R, C), F32)] * 4,
        sem=("parallel", "parallel"))(w, m, v, parts)


def _as3d(a):
    if a.ndim == 2:
        return a[None]
    if a.ndim == 3:
        return a
    return a.reshape((-1,) + a.shape[-2:])


_WEIGHTS = ["norm_mix", "w_in", "conv_w", "conv_b", "fgate_bias", "q_norm", "k_norm", "lb_logits", "hgrn_norm", "sgu_norm",
            "spatial_w", "spatial_b", "w_up", "merge_b", "w_o", "norm_ple", "w_ple_gate", "w_ple_proj"]
_BIG = ["w_in", "w_up", "w_o", "w_ple_gate", "w_ple_proj"]
_SHARD_AXIS = {"w_in": 2, "w_up": 3, "w_o": 1, "w_ple_gate": 1, "w_ple_proj": 2, "conv_w": 2, "merge_b": 2}
_SMALL = [n for n in _WEIGHTS if n not in _BIG]


def _full_from_shards(name, g):
    ax = _SHARD_AXIS[name]
    return jnp.concatenate([g[k] for k in range(4)], axis=ax)


def _slabs_of(name, full):
    ax = _SHARD_AXIS[name]
    return jnp.stack(jnp.split(full, 4, axis=ax))


def kernel(x, p, norm_mix, w_in, conv_w, conv_b, fgate_bias, q_norm, k_norm, lb_logits, hgrn_norm, sgu_norm, spatial_w, spatial_b, w_up, merge_b, w_o, norm_ple, w_ple_gate, w_ple_proj, loss_target, m_norm_mix, m_w_in, m_conv_w, m_conv_b, m_fgate_bias, m_q_norm, m_k_norm, m_lb_logits, m_hgrn_norm, m_sgu_norm, m_spatial_w, m_spatial_b, m_w_up, m_merge_b, m_w_o, m_norm_ple, m_w_ple_gate, m_w_ple_proj, v_norm_mix, v_w_in, v_conv_w, v_conv_b, v_fgate_bias, v_q_norm, v_k_norm, v_lb_logits, v_hgrn_norm, v_sgu_norm, v_spatial_w, v_spatial_b, v_w_up, v_merge_b, v_w_o, v_norm_ple, v_w_ple_gate, v_w_ple_proj):
    loc = dict(locals())
    w = {n: loc[n] for n in _WEIGHTS}
    m = {n: loc["m_" + n] for n in _WEIGHTS}
    v = {n: loc["v_" + n] for n in _WEIGHTS}
    chip = 2 * lax.axis_index("x") + lax.axis_index("y")

    gathered = _gather_xy([_mx(w[n]) for n in _BIG] + [w["conv_w"], w["merge_b"]])
    full = dict(w)
    for n, g in zip(_BIG + ["conv_w", "merge_b"], gathered):
        full[n] = _full_from_shards(n, g)

    loss_blk, dx, grads = _local_step(x[0], p[:, 0], loss_target[0], full)
    loss = lax.psum(loss_blk[0, 0], ("x", "y", "c"))

    flat = jnp.concatenate([grads[n].reshape(-1) for n in _SMALL])
    npad = (-flat.shape[0]) % 1024
    packed = jnp.concatenate([flat, jnp.zeros((npad,), F32)]).reshape(-1, 128)
    red = _allreduce_small(packed).reshape(-1)
    small, off = {}, 0
    for n in _SMALL:
        sz = int(np.prod(grads[n].shape))
        small[n] = red[off:off + sz].reshape(grads[n].shape)
        off += sz
    for n in ("conv_w", "merge_b"):
        ax = _SHARD_AXIS[n]
        width = small[n].shape[ax] // 4
        small[n] = lax.dynamic_slice_in_dim(small[n], chip * width, width, axis=ax)

    contrib = _exchange_slabs([_slabs_of(n, grads[n]) for n in _BIG])

    out_g, out_d, out_m, out_v = {}, {}, {}, {}
    for n in _WEIGHTS:
        shp = w[n].shape
        if n in _BIG:
            c8 = contrib[_BIG.index(n)]
            parts, k = c8.reshape((8,) + _as3d(w[n]).shape), 8
        else:
            parts, k = _as3d(small[n])[None], 1
        g, d, nm, nv = _adamw(_as3d(w[n]), _as3d(m[n]), _as3d(v[n]), parts, k)
        out_g[n], out_d[n], out_m[n], out_v[n] = (a.reshape(shp) for a in (g, d, nm, nv))

    return (loss, dx[None], *[out_g[n] for n in _WEIGHTS], *[out_d[n] for n in _WEIGHTS],
            *[out_m[n] for n in _WEIGHTS], *[out_v[n] for n in _WEIGHTS])
```

```python
import functools

import numpy as np
import jax
import jax.numpy as jnp
from jax import lax
from jax.experimental import pallas as pl
from jax.experimental.pallas import tpu as pltpu

F32 = jnp.float32
MXU_DTYPE = jnp.bfloat16

D_MODEL = 1024
BW = 256
NH = 4
DH = 64
DEPTH = 4
NBR = 4
PLE = 256
CHUNK = 64
SCHUNK = 128
EPS = 1e-6
MASK_VALUE = -1e30
NLEV = 6

ADAM_LR, ADAM_B1, ADAM_B2, ADAM_EPS, ADAM_WD, ADAM_STEP = 0.001, 0.9, 0.999, 1e-08, 0.01, 10

ZM, ZA, ZB, ZC, ZD, ZF = 0, 4096, 5120, 6144, 7168, 8192
ZCOLS = 8320
_OA, _OB, _OF, _OC, _OD, _OM, _OEND = 0, 1024, 2048, 2052, 3076, 3844, 7940

VMEM_LIMIT = 56 * 1024 * 1024
ROW_TILE = 512
FLASH_TILE = 512
MERGE_TILE = 256
MESH = pl.DeviceIdType.MESH


def _tile(n, pref):
    t = min(n, pref)
    assert n % t == 0, (n, t)
    return t


def _call(body, *, name, grid, in_specs, out_specs, out_shape, scratch=(), sem=None):
    return pl.pallas_call(
        functools.partial(body), name=name, grid=grid, in_specs=in_specs, out_specs=out_specs,
        out_shape=out_shape, scratch_shapes=list(scratch),
        compiler_params=pltpu.CompilerParams(dimension_semantics=sem, vmem_limit_bytes=VMEM_LIMIT))


def _mx(x):
    return x.astype(MXU_DTYPE)


def _dot(a, b):
    return jnp.dot(_mx(a), _mx(b), preferred_element_type=F32)


def _dot_nt(a, b):
    return lax.dot_general(_mx(a), _mx(b), (((1,), (1,)), ((), ())), preferred_element_type=F32)


def _dot_tn(a, b):
    return lax.dot_general(_mx(a), _mx(b), (((0,), (0,)), ((), ())), preferred_element_type=F32)


def _top16(x):
    u = lax.bitcast_convert_type(x, jnp.uint32) & jnp.uint32(0xFFFF0000)
    return lax.bitcast_convert_type(u, F32)


def _split3(x):
    hi = _top16(x)
    r1 = x - hi
    mid = _top16(r1)
    return _mx(hi), _mx(mid), _mx(r1 - mid)


def _dot_exact_lhs01(t, x):
    hi, mid, lo = _split3(x)
    t = _mx(t)
    return (jnp.dot(t, hi, preferred_element_type=F32) + jnp.dot(t, mid, preferred_element_type=F32)
            + jnp.dot(t, lo, preferred_element_type=F32))


def _sigmoid(x):
    return jax.nn.sigmoid(x)


def _silu(x):
    return x * _sigmoid(x)


def _dsilu(x):
    s = _sigmoid(x)
    return s * (1.0 + x * (1.0 - s))


def _log_sigmoid(x):
    return jnp.minimum(x, 0.0) - jnp.log(1.0 + jnp.exp(-jnp.abs(x)))


def _rms_fwd(x, g):
    r = lax.rsqrt(jnp.mean(x * x, axis=-1, keepdims=True) + EPS)
    return x * r * g


def _rms_bwd(x, g, dy):
    r = lax.rsqrt(jnp.mean(x * x, axis=-1, keepdims=True) + EPS)
    n = x * r
    t = dy * g
    dx = r * (t - n * jnp.mean(t * n, axis=-1, keepdims=True))
    return dx, dy * n


def _tri(n, upper=False):
    r = lax.broadcasted_iota(jnp.int32, (n, n), 0)
    c = lax.broadcasted_iota(jnp.int32, (n, n), 1)
    return jnp.where((c >= r) if upper else (r >= c), 1.0, 0.0).astype(F32)


def _hs(h):
    return slice(h * DH, (h + 1) * DH)


def _mm_nn(a, b, out_dtype, *, name, tn=None, tk=None):
    S, K = a.shape
    N = b.shape[1]
    ts = _tile(S, ROW_TILE)
    tn = _tile(N, tn or N)
    tk = _tile(K, tk or K)
    nk = K // tk

    def body(a_ref, b_ref, o_ref, acc_ref):
        k = pl.program_id(2)
        part = jnp.dot(a_ref[...], b_ref[...], preferred_element_type=F32)
        if nk == 1:
            o_ref[...] = part.astype(o_ref.dtype)
        else:
            @pl.when(k == 0)
            def _():
                acc_ref[...] = part

            @pl.when(k > 0)
            def _():
                acc_ref[...] += part

            @pl.when(k == nk - 1)
            def _():
                o_ref[...] = acc_ref[...].astype(o_ref.dtype)

    return _call(
        body, name=name, grid=(N // tn, S // ts, nk),
        in_specs=[pl.BlockSpec((ts, tk), lambda j, i, k: (i, k)), pl.BlockSpec((tk, tn), lambda j, i, k: (k, j))],
        out_specs=pl.BlockSpec((ts, tn), lambda j, i, k: (i, j)),
        out_shape=jax.ShapeDtypeStruct((S, N), out_dtype),
        scratch=[pltpu.VMEM((ts, tn) if nk > 1 else (8, 128), F32)],
        sem=("parallel", "parallel", "arbitrary"))(a, b)


def _mm_tn(x, y, *, name, tn=None, ycol=0, n=None):
    S, M = x.shape
    n = n or y.shape[1]
    ts = _tile(S, ROW_TILE)
    tn = _tile(n, tn or n)
    nj = n // tn

    def body(x_ref, y_ref, o_ref):
        @pl.when(pl.program_id(1) == 0)
        def _():
            o_ref[...] = jnp.zeros_like(o_ref)

        o_ref[...] += lax.dot_general(x_ref[...], y_ref[...], (((0,), (0,)), ((), ())), preferred_element_type=F32)

    return _call(
        body, name=name, grid=(nj, S // ts),
        in_specs=[pl.BlockSpec((ts, M), lambda j, s: (s, 0)), pl.BlockSpec((ts, tn), lambda j, s: (s, ycol * nj + j))],
        out_specs=pl.BlockSpec((M, tn), lambda j, s: (0, j)),
        out_shape=jax.ShapeDtypeStruct((M, n), F32),
        sem=("parallel", "arbitrary"))(x, y)


def _norm_fwd(x, g):
    S = x.shape[0]
    ts = _tile(S, ROW_TILE)

    def body(x_ref, g_ref, h_ref):
        h_ref[...] = _mx(_rms_fwd(x_ref[...], g_ref[...]))

    return _call(
        body, name="norm_fwd", grid=(S // ts,),
        in_specs=[pl.BlockSpec((ts, D_MODEL), lambda i: (i, 0)), pl.BlockSpec((1, D_MODEL), lambda i: (0, 0))],
        out_specs=pl.BlockSpec((ts, D_MODEL), lambda i: (i, 0)),
        out_shape=jax.ShapeDtypeStruct((S, D_MODEL), MXU_DTYPE), sem=("parallel",))(x, g)


def _norm_bwd(x, g, dh, dres):
    S = x.shape[0]
    ts = _tile(S, ROW_TILE)

    def body(x_ref, g_ref, dh_ref, dr_ref, dx_ref, st_ref):
        @pl.when(pl.program_id(0) == 0)
        def _():
            st_ref[...] = jnp.zeros_like(st_ref)

        dx, dgr = _rms_bwd(x_ref[...], g_ref[...], dh_ref[...])
        dx_ref[...] = dr_ref[...] + dx
        st_ref[0:1, :] += jnp.sum(dgr, axis=0, keepdims=True)

    row = pl.BlockSpec((ts, D_MODEL), lambda i: (i, 0))
    return _call(
        body, name="norm_bwd", grid=(S // ts,),
        in_specs=[row, pl.BlockSpec((1, D_MODEL), lambda i: (0, 0)), row, row],
        out_specs=[row, pl.BlockSpec((8, D_MODEL), lambda i: (0, 0))],
        out_shape=[jax.ShapeDtypeStruct((S, D_MODEL), F32), jax.ShapeDtypeStruct((8, D_MODEL), F32)],
        sem=("arbitrary",))(x, g, dh, dres)


def _loss_fwd_bwd(y, target):
    S = y.shape[0]
    ts = _tile(S, ROW_TILE)

    def body(y_ref, t_ref, l_ref, dy_ref):
        @pl.when(pl.program_id(0) == 0)
        def _():
            l_ref[...] = jnp.zeros_like(l_ref)

        err = y_ref[...] - t_ref[...]
        dy_ref[...] = err * (1.0 / D_MODEL)
        rowloss = jnp.mean(err * err, axis=-1, keepdims=True)
        l_ref[...] += 0.5 * jnp.sum(rowloss, axis=0, keepdims=True)

    row = pl.BlockSpec((ts, D_MODEL), lambda i: (i, 0))
    return _call(
        body, name="loss", grid=(S // ts,), in_specs=[row, row],
        out_specs=[pl.BlockSpec((8, 128), lambda i: (0, 0)), row],
        out_shape=[jax.ShapeDtypeStruct((8, 128), F32), jax.ShapeDtypeStruct((S, D_MODEL), F32)],
        sem=("arbitrary",))(y, target)


def _shift_down(x, k, halo, rows):
    y = pltpu.roll(x, k, 0)
    for j in range(k):
        y = jnp.where(rows == j, halo[8 - k + j:8 - k + j + 1, :], y)
    return y


def _shift_up(x, k, halo, rows, n):
    y = pltpu.roll(x, n - k, 0)
    for j in range(k):
        y = jnp.where(rows == n - k + j, halo[j:j + 1, :], y)
    return y


def _conv_parts(za, zh, cw, first, rows):
    ax, ab, ac, ag = za[:, 0:BW], za[:, BW:2 * BW], za[:, 2 * BW:3 * BW], za[:, 3 * BW:4 * BW]
    zz = ac * ax
    hz = jnp.where(first, 0.0, zh[:, 2 * BW:3 * BW] * zh[:, 0:BW])
    zz1 = _shift_down(zz, 1, hz, rows)
    zz2 = _shift_down(zz, 2, hz, rows)
    conv = zz2 * cw[0:1, :] + zz1 * cw[1:2, :] + zz * cw[2:3, :] + cw[3:4, :]
    return ax, ab, ac, ag, zz, zz1, zz2, conv


def _conv_fwd(z, cw):
    S = z.shape[0]
    ts = _tile(S, ROW_TILE)
    hb = ts // 8

    def body(za_ref, zh_ref, cw_ref, y_ref):
        i = pl.program_id(0)
        rows = lax.broadcasted_iota(jnp.int32, (ts, BW), 0)
        ax, ab, ac, ag, zz, zz1, zz2, conv = _conv_parts(za_ref[...], zh_ref[...], cw_ref[...], i == 0, rows)
        y_ref[...] = _mx(ab * conv * _silu(ag))

    return _call(
        body, name="conv_fwd", grid=(S // ts,),
        in_specs=[pl.BlockSpec((ts, 1024), lambda i: (i, ZA // 1024)),
                  pl.BlockSpec((8, 1024), lambda i: (jnp.maximum(i * hb - 1, 0), ZA // 1024)),
                  pl.BlockSpec((8, BW), lambda i: (0, 0))],
        out_specs=pl.BlockSpec((ts, BW), lambda i: (i, 0)),
        out_shape=jax.ShapeDtypeStruct((S, BW), MXU_DTYPE), sem=("parallel",))(z, z, cw)


def _conv_bwd(z, cw, dy):
    S = z.shape[0]
    ts = _tile(S, ROW_TILE)
    hb = ts // 8
    nt = S // ts

    def body(za_ref, zh_ref, zn_ref, cw_ref, dy_ref, dyn_ref, dz_ref, st_ref):
        i = pl.program_id(0)

        @pl.when(i == 0)
        def _():
            st_ref[...] = jnp.zeros_like(st_ref)

        cw = cw_ref[...]
        rows = lax.broadcasted_iota(jnp.int32, (ts, BW), 0)
        ax, ab, ac, ag, zz, zz1, zz2, conv = _conv_parts(za_ref[...], zh_ref[...], cw, i == 0, rows)
        dy = dy_ref[...]
        sg = _silu(ag)
        dc = dy * ab * sg
        zn = zn_ref[...]
        dcn = jnp.where(i == nt - 1, 0.0, dyn_ref[...] * zn[:, BW:2 * BW] * _silu(zn[:, 3 * BW:4 * BW]))
        dc1 = _shift_up(dc, 1, dcn, rows, ts)
        dc2 = _shift_up(dc, 2, dcn, rows, ts)
        dzz = dc * cw[2:3, :] + dc1 * cw[1:2, :] + dc2 * cw[0:1, :]
        dz_ref[:, 0:BW] = _mx(dzz * ac)
        dz_ref[:, BW:2 * BW] = _mx(dy * conv * sg)
        dz_ref[:, 2 * BW:3 * BW] = _mx(dzz * ax)
        dz_ref[:, 3 * BW:4 * BW] = _mx(dy * ab * conv * _dsilu(ag))
        st_ref[0:1, :] += jnp.sum(dc * zz2, axis=0, keepdims=True)
        st_ref[1:2, :] += jnp.sum(dc * zz1, axis=0, keepdims=True)
        st_ref[2:3, :] += jnp.sum(dc * zz, axis=0, keepdims=True)
        st_ref[3:4, :] += jnp.sum(dc, axis=0, keepdims=True)

    return _call(
        body, name="conv_bwd", grid=(nt,),
        in_specs=[pl.BlockSpec((ts, 1024), lambda i: (i, ZA // 1024)),
                  pl.BlockSpec((8, 1024), lambda i: (jnp.maximum(i * hb - 1, 0), ZA // 1024)),
                  pl.BlockSpec((8, 1024), lambda i: (jnp.minimum((i + 1) * hb, S // 8 - 1), ZA // 1024)),
                  pl.BlockSpec((8, BW), lambda i: (0, 0)),
                  pl.BlockSpec((ts, BW), lambda i: (i, 0)),
                  pl.BlockSpec((8, BW), lambda i: (jnp.minimum((i + 1) * hb, S // 8 - 1), 0))],
        out_specs=[pl.BlockSpec((ts, 1024), lambda i: (i, 0)), pl.BlockSpec((8, BW), lambda i: (0, 0))],
        out_shape=[jax.ShapeDtypeStruct((S, 1024), MXU_DTYPE), jax.ShapeDtypeStruct((8, BW), F32)],
        sem=("arbitrary",))(z, z, z, cw, dy, dy)


def _attn_prep_fwd(z, fb, gq, gk):
    S = z.shape[0]
    ts = _tile(S, ROW_TILE)

    def body(zb_ref, zf_ref, fb_ref, gq_ref, gk_ref, q_ref, k_ref, v_ref, cc_ref, cr_ref, carry):
        @pl.when(pl.program_id(0) == 0)
        def _():
            carry[...] = jnp.zeros_like(carry)

        zb = zb_ref[...]
        gqv, gkv = gq_ref[...], gk_ref[...]
        for h in range(NH):
            q_ref[:, _hs(h)] = _mx(_rms_fwd(zb[:, _hs(h)], gqv) * (DH ** -0.5))
            k_ref[:, _hs(h)] = _mx(_rms_fwd(zb[:, BW + h * DH:BW + (h + 1) * DH], gkv))
        v_ref[...] = _mx(zb[:, 2 * BW:3 * BW])
        lf = _log_sigmoid(zf_ref[...] + fb_ref[...])
        cum = _dot_exact_lhs01(_tri(ts), lf) + carry[...]
        carry[...] = cum[ts - 1:ts, :]
        cc_ref[...] = cum
        cr_ref[...] = cum.T[0:8, :]

    row = lambda w: pl.BlockSpec((ts, w), lambda i: (i, 0))
    return _call(
        body, name="attn_prep_fwd", grid=(S // ts,),
        in_specs=[pl.BlockSpec((ts, 1024), lambda i: (i, ZB // 1024)), pl.BlockSpec((ts, 128), lambda i: (i, ZF // 128)),
                  pl.BlockSpec((1, 128), lambda i: (0, 0)), pl.BlockSpec((1, DH), lambda i: (0, 0)),
                  pl.BlockSpec((1, DH), lambda i: (0, 0))],
        out_specs=[row(BW), row(BW), row(BW), row(128), pl.BlockSpec((8, ts), lambda i: (0, i))],
        out_shape=[jax.ShapeDtypeStruct((S, BW), MXU_DTYPE)] * 3
        + [jax.ShapeDtypeStruct((S, 128), F32), jax.ShapeDtypeStruct((8, S), F32)],
        scratch=[pltpu.VMEM((1, 128), F32)], sem=("arbitrary",))(z, z, fb, gq, gk)


def _attn_scores(q, k, cc, cr, h, i, j, t, masked):
    s = _dot_nt(q[:, _hs(h)], k[:, _hs(h)]) + (cc[:, h:h + 1] - cr[h:h + 1, :])
    if not masked:
        return s, None
    qpos = i * t + lax.broadcasted_iota(jnp.int32, (t, t), 0)
    kpos = j * t + lax.broadcasted_iota(jnp.int32, (t, t), 1)
    return s, qpos >= kpos


def _fuse_comm(core, n_in, n_out, comm, n):
    nc = 0 if comm is None else len(comm[1])

    def body(*refs):
        cin, xin = refs[:n_in], refs[n_in:n_in + nc]
        a = n_in + nc
        cout, xout = refs[a:a + n_out], refs[a + n_out:a + n_out + nc]
        rest = refs[a + n_out + nc:]
        if nc == 0:
            core(*cin, *cout, *rest)
            return
        cscr, sems = rest[:-3], rest[-3:]
        i, j = pl.program_id(0), pl.program_id(1)
        phase = _COMM_PHASES[comm[0]]

        @pl.when((i == 0) & (j == 0))
        def _():
            phase(0, xin, xout, sems)

        core(*cin, *cout, *cscr)

        @pl.when((i == n // 2) & (j == 0))
        def _():
            phase(1, xin, xout, sems)

        @pl.when((i == n - 1) & (j == n - 1))
        def _():
            phase(2, xin, xout, sems)

    return body


def _comm_specs(comm):
    if comm is None:
        return [], [], [], []
    hbm = pl.BlockSpec(memory_space=pl.ANY)
    nc = len(comm[1])
    return [hbm] * nc, [hbm] * nc, _COMM_OUT[comm[0]](comm[1]), _comm_sems(comm[0], nc)


def _flash_fwd(q, k, v, cc, cr, z, comm=None):
    S = q.shape[0]
    t = _tile(S, FLASH_TILE)
    n = S // t

    def core(q_ref, k_ref, v_ref, cc_ref, cr_ref, zb_ref, o_ref, lse_ref, y_ref, m_sc, l_sc, acc):
        i, j = pl.program_id(0), pl.program_id(1)

        @pl.when(j == 0)
        def _():
            m_sc[...] = jnp.full_like(m_sc, MASK_VALUE)
            l_sc[...] = jnp.zeros_like(l_sc)
            acc[...] = jnp.zeros_like(acc)

        def block(masked):
            qv, kv, vv, cc, cr = q_ref[...], k_ref[...], v_ref[...], cc_ref[...], cr_ref[...]
            for h in range(NH):
                s, mask = _attn_scores(qv, kv, cc, cr, h, i, j, t, masked)
                if masked:
                    s = jnp.where(mask, s, MASK_VALUE)
                m_old = m_sc[h]
                m_new = jnp.maximum(m_old, jnp.max(s, axis=-1, keepdims=True))
                p = jnp.exp(s - m_new)
                alpha = jnp.exp(m_old - m_new)
                l_sc[h] = alpha * l_sc[h] + jnp.sum(p, axis=-1, keepdims=True)
                acc[:, _hs(h)] = alpha * acc[:, _hs(h)] + _dot(p, vv[:, _hs(h)])
                m_sc[h] = m_new

        @pl.when(j < i)
        def _():
            block(False)

        @pl.when(j == i)
        def _():
            block(True)
            lse_ref[...] = jnp.zeros_like(lse_ref)
            for h in range(NH):
                o_ref[:, _hs(h)] = acc[:, _hs(h)] / l_sc[h]
                lse_ref[:, h:h + 1] = m_sc[h] + jnp.log(l_sc[h])
            y_ref[...] = _mx(o_ref[...] * _silu(zb_ref[:, 3 * BW:4 * BW]))

    qspec = lambda w: pl.BlockSpec((t, w), lambda i, j: (i, 0))
    kspec = lambda w: pl.BlockSpec((t, w), lambda i, j: (jnp.minimum(j, i), 0))
    xin, xout, xshape, xsem = _comm_specs(comm)
    res = _call(
        _fuse_comm(core, 6, 3, comm, n), name="flash_fwd", grid=(n, n),
        in_specs=[qspec(BW), kspec(BW), kspec(BW), qspec(128), pl.BlockSpec((8, t), lambda i, j: (0, jnp.minimum(j, i))),
                  pl.BlockSpec((t, 1024), lambda i, j: (i, ZB // 1024))] + xin,
        out_specs=[qspec(BW), qspec(128), qspec(BW)] + xout,
        out_shape=[jax.ShapeDtypeStruct((S, BW), F32), jax.ShapeDtypeStruct((S, 128), F32),
                   jax.ShapeDtypeStruct((S, BW), MXU_DTYPE)] + xshape,
        scratch=[pltpu.VMEM((NH, t, 1), F32), pltpu.VMEM((NH, t, 1), F32), pltpu.VMEM((t, BW), F32)] + xsem,
        sem=("arbitrary", "arbitrary"))(q, k, v, cc, cr, z, *(comm[1] if comm else []))
    return res[0], res[1], res[2], list(res[3:])


def _attn_gate_bwd(dy, o, z):
    S = dy.shape[0]
    ts = _tile(S, ROW_TILE)

    def body(dy_ref, o_ref, zb_ref, do_ref, dg_ref):
        g = zb_ref[:, 3 * BW:4 * BW]
        dy = dy_ref[...]
        do_ref[...] = dy * _silu(g)
        dg_ref[...] = _mx(dy * o_ref[...] * _dsilu(g))

    row = pl.BlockSpec((ts, BW), lambda i: (i, 0))
    return _call(
        body, name="attn_gate_bwd", grid=(S // ts,),
        in_specs=[row, row, pl.BlockSpec((ts, 1024), lambda i: (i, ZB // 1024))],
        out_specs=[row, row],
        out_shape=[jax.ShapeDtypeStruct((S, BW), F32), jax.ShapeDtypeStruct((S, BW), MXU_DTYPE)],
        sem=("parallel",))(dy, o, z)


def _flash_probs(qv, kv, vv, cc, cr, lse, do, o, h, i, j, t, masked, drow=None):
    s, mask = _attn_scores(qv, kv, cc, cr, h, i, j, t, masked)
    p = jnp.exp(s - lse[:, h:h + 1])
    if masked:
        p = jnp.where(mask, p, 0.0)
    doh = do[:, _hs(h)]
    dp = _dot_nt(doh, vv[:, _hs(h)])
    delta = jnp.sum(doh * o[:, _hs(h)], axis=-1, keepdims=True)
    if drow is not None:
        delta = delta + drow[:, h:h + 1]
    return p, p * (dp - delta), doh


def _flash_bwd_dq(q, k, v, cc, cr, lse, o, do, comm=None):
    S = q.shape[0]
    t = _tile(S, FLASH_TILE)
    n = S // t

    def core(q_ref, k_ref, v_ref, cc_ref, cr_ref, lse_ref, o_ref, do_ref, dq_ref, dr_ref):
        i, j = pl.program_id(0), pl.program_id(1)

        @pl.when(j == 0)
        def _():
            dq_ref[...] = jnp.zeros_like(dq_ref)
            dr_ref[...] = jnp.zeros_like(dr_ref)

        def block(masked):
            qv, kv, vv = q_ref[...], k_ref[...], v_ref[...]
            for h in range(NH):
                p, ds, doh = _flash_probs(qv, kv, vv, cc_ref[...], cr_ref[...], lse_ref[...], do_ref[...], o_ref[...],
                                          h, i, j, t, masked)
                dq_ref[:, _hs(h)] += _dot(ds, kv[:, _hs(h)])
                dr_ref[:, h:h + 1] += jnp.sum(ds, axis=-1, keepdims=True)

        @pl.when(j < i)
        def _():
            block(False)

        @pl.when(j == i)
        def _():
            block(True)

    qspec = lambda w: pl.BlockSpec((t, w), lambda i, j: (i, 0))
    kspec = lambda w: pl.BlockSpec((t, w), lambda i, j: (jnp.minimum(j, i), 0))
    xin, xout, xshape, xsem = _comm_specs(comm)
    res = _call(
        _fuse_comm(core, 8, 2, comm, n), name="flash_bwd_dq", grid=(n, n),
        in_specs=[qspec(BW), kspec(BW), kspec(BW), qspec(128), pl.BlockSpec((8, t), lambda i, j: (0, jnp.minimum(j, i))),
                  qspec(128), qspec(BW), qspec(BW)] + xin,
        out_specs=[qspec(BW), qspec(128)] + xout,
        out_shape=[jax.ShapeDtypeStruct((S, BW), F32), jax.ShapeDtypeStruct((S, 128), F32)] + xshape,
        scratch=xsem,
        sem=("arbitrary", "arbitrary"))(q, k, v, cc, cr, lse, o, do, *(comm[1] if comm else []))
    return res[0], res[1], list(res[2:])


def _flash_bwd_dkv(q, k, v, cc, cr, lse, o, do, drow):
    S = q.shape[0]
    t = _tile(S, FLASH_TILE)
    n = S // t

    def body(q_ref, k_ref, v_ref, cc_ref, cr_ref, lse_ref, o_ref, do_ref, dr_ref, dk_ref, dv_ref, dc_ref):
        j, i = pl.program_id(0), pl.program_id(1)

        @pl.when(i == 0)
        def _():
            dk_ref[...] = jnp.zeros_like(dk_ref)
            dv_ref[...] = jnp.zeros_like(dv_ref)
            dc_ref[...] = jnp.zeros_like(dc_ref)

        def block(masked):
            qv, kv, vv = q_ref[...], k_ref[...], v_ref[...]
            for h in range(NH):
                p, ds, doh = _flash_probs(qv, kv, vv, cc_ref[...], cr_ref[...], lse_ref[...], do_ref[...], o_ref[...],
                                          h, i, j, t, masked, drow=dr_ref[...])
                dv_ref[:, _hs(h)] += _dot_tn(p, doh)
                dk_ref[:, _hs(h)] += _dot_tn(ds, qv[:, _hs(h)])
                dc_ref[h:h + 1, :] += -jnp.sum(ds, axis=0, keepdims=True)

        @pl.when(i > j)
        def _():
            block(False)

        @pl.when(i == j)
        def _():
            block(True)

    qspec = lambda w: pl.BlockSpec((t, w), lambda j, i: (jnp.maximum(i, j), 0))
    kspec = lambda w: pl.BlockSpec((t, w), lambda j, i: (j, 0))
    return _call(
        body, name="flash_bwd_dkv", grid=(n, n),
        in_specs=[qspec(BW), kspec(BW), kspec(BW), qspec(128), pl.BlockSpec((8, t), lambda j, i: (0, j)),
                  qspec(128), qspec(BW), qspec(BW), qspec(128)],
        out_specs=[kspec(BW), kspec(BW), pl.BlockSpec((8, t), lambda j, i: (0, j))],
        out_shape=[jax.ShapeDtypeStruct((S, BW), F32), jax.ShapeDtypeStruct((S, BW), F32),
                   jax.ShapeDtypeStruct((8, S), F32)],
        sem=("parallel", "arbitrary"))(q, k, v, cc, cr, lse, o, do, drow)


def _attn_prep_bwd(z, fb, gq, gk, dq, dk, dv, dgate, dcc):
    S = z.shape[0]
    ts = _tile(S, ROW_TILE)
    nt = S // ts

    def body(zb_ref, zf_ref, fb_ref, gq_ref, gk_ref, dq_ref, dk_ref, dv_ref, dg_ref, dcc_ref, dzb_ref, dzf_ref,
             st_ref, carry):
        @pl.when(pl.program_id(0) == 0)
        def _():
            carry[...] = jnp.zeros_like(carry)
            st_ref[...] = jnp.zeros_like(st_ref)

        zb = zb_ref[...]
        gqv, gkv = gq_ref[...], gk_ref[...]
        dqv, dkv = dq_ref[...], dk_ref[...]
        sq = jnp.zeros((1, DH), F32)
        sk = jnp.zeros((1, DH), F32)
        for h in range(NH):
            dx, dgr = _rms_bwd(zb[:, _hs(h)], gqv, dqv[:, _hs(h)] * (DH ** -0.5))
            dzb_ref[:, _hs(h)] = _mx(dx)
            sq = sq + jnp.sum(dgr, axis=0, keepdims=True)
            ks = slice(BW + h * DH, BW + (h + 1) * DH)
            dx, dgr = _rms_bwd(zb[:, ks], gkv, dkv[:, _hs(h)])
            dzb_ref[:, ks] = _mx(dx)
            sk = sk + jnp.sum(dgr, axis=0, keepdims=True)
        dzb_ref[:, 2 * BW:3 * BW] = _mx(dv_ref[...])
        dzb_ref[:, 3 * BW:4 * BW] = dg_ref[...]
        dc = dcc_ref[...]
        dlf = _dot_exact_lhs01(_tri(ts, upper=True), dc) + carry[...]
        carry[...] = dlf[0:1, :]
        dfz = dlf * _sigmoid(-(zf_ref[...] + fb_ref[...]))
        dzf_ref[...] = _mx(dfz)
        st_ref[0:1, 0:DH] += sq
        st_ref[1:2, 0:DH] += sk
        st_ref[2:3, :] += jnp.sum(dfz, axis=0, keepdims=True)

    rev = lambda w, c=0: pl.BlockSpec((ts, w), lambda i: (nt - 1 - i, c))
    one = lambda w: pl.BlockSpec((1, w), lambda i: (0, 0))
    return _call(
        body, name="attn_prep_bwd", grid=(nt,),
        in_specs=[rev(1024, ZB // 1024), rev(128, ZF // 128), one(128), one(DH), one(DH),
                  rev(BW), rev(BW), rev(BW), rev(BW), rev(128)],
        out_specs=[rev(1024), rev(128), pl.BlockSpec((8, 128), lambda i: (0, 0))],
        out_shape=[jax.ShapeDtypeStruct((S, 1024), MXU_DTYPE), jax.ShapeDtypeStruct((S, 128), MXU_DTYPE),
                   jax.ShapeDtypeStruct((8, 128), F32)],
        scratch=[pltpu.VMEM((1, 128), F32)], sem=("arbitrary",))(z, z, fb, gq, gk, dq, dk, dv, dgate, dcc)


def _hgrn_consts():
    C = CHUNK
    t = np.arange(C)[:, None]
    j = np.arange(C)[None, :]
    mats = [(j <= t)]
    tq, tk, masks = [], [], []
    for lev in range(NLEV):
        m = C >> (lev + 1)
        blk, pos = t // (2 * m), t % (2 * m)
        mid = blk * 2 * m + m - 1
        tq.append((pos >= m) & (j > mid) & (j <= t))
        tk.append((pos < m) & (j > t) & (j <= mid))
        sblk, spos = j // (2 * m), j % (2 * m)
        masks.append((blk == sblk) & (pos >= m) & (spos < m))
    masks.append(t == j)
    tmat = np.concatenate(mats + tq + tk, axis=0).astype(np.float32)
    return tmat, np.stack(masks).astype(np.float32)


def _hgrn_chunk_fwd(zc, lb, tmat):
    cq, cf, ci = zc[:, 0:BW], zc[:, BW:2 * BW], zc[:, 2 * BW:3 * BW]
    q = _silu(cq)
    sg = _sigmoid(cf)
    g = lb + (1.0 - lb) * sg
    lg = jnp.log(g)
    kf = (1.0 - lb) * _sigmoid(-cf)
    e = _dot_exact_lhs01(tmat, lg)
    b = e[0:CHUNK]
    blast = b[CHUNK - 1:CHUNK, :]
    return cq, cf, q, sg, g, kf, ci, e, b, blast


def _round_mx(x):
    if MXU_DTYPE != jnp.bfloat16:
        return x
    u = lax.bitcast_convert_type(x, jnp.uint32)
    u = (u + jnp.uint32(0x7FFF) + ((u >> 16) & jnp.uint32(1))) & jnp.uint32(0xFFFF0000)
    return lax.bitcast_convert_type(u, F32)


def _hgrn_scores(q, kf, e, masks, h):
    qh, kh = q[:, _hs(h)], kf[:, _hs(h)]
    ql, kl = [], []
    a = None
    for lev in range(NLEV + 1):
        if lev < NLEV:
            eq = jnp.exp(e[(1 + lev) * CHUNK:(2 + lev) * CHUNK, _hs(h)])
            ek = jnp.exp(e[(1 + NLEV + lev) * CHUNK:(2 + NLEV + lev) * CHUNK, _hs(h)])
            ql.append((_round_mx(qh * eq), eq))
            kl.append((_round_mx(kh * ek), ek))
        else:
            ql.append((_round_mx(qh), None))
            kl.append((_round_mx(kh), None))
        term = masks[lev] * _dot_nt(ql[-1][0], kl[-1][0])
        a = term if a is None else a + term
    return a, ql, kl


def _hgrn_fwd(z, lb, gain, tmat, masks):
    S = z.shape[0]
    ts = _tile(S, ROW_TILE)
    nc = ts // CHUNK

    def body(zc_ref, lb_ref, gn_ref, tm_ref, mk_ref, o_ref, st_ref, y_ref, state):
        @pl.when(pl.program_id(0) == 0)
        def _():
            state[...] = jnp.zeros_like(state)

        lbv, gn, tm, mk = lb_ref[...], gn_ref[...], tm_ref[...], mk_ref[...]

        def chunk(c, carry):
            r0 = pl.multiple_of(c * CHUNK, CHUNK)
            zc = zc_ref[pl.ds(r0, CHUNK), :]
            cq, cf, q, sg, g, kf, v, e, b, blast = _hgrn_chunk_fwd(zc, lbv, tm)
            qe = q * jnp.exp(b)
            kd = kf * jnp.exp(blast - b)
            st_ref[pl.ds(r0, CHUNK), :] = state[...]
            for h in range(NH):
                sth = state[:, _hs(h)]
                a, _, _ = _hgrn_scores(q, kf, e, mk, h)
                oh = _dot_nt(qe[:, _hs(h)], sth) + _dot(a, v[:, _hs(h)])
                state[:, _hs(h)] = sth * jnp.exp(blast[:, _hs(h)]) + _dot_tn(v[:, _hs(h)], kd[:, _hs(h)])
                o_ref[pl.ds(r0, CHUNK), _hs(h)] = oh
                yn = _rms_fwd(oh, gn[:, _hs(h)])
                y_ref[pl.ds(r0, CHUNK), _hs(h)] = _mx(yn * _silu(zc[:, 3 * BW + h * DH:3 * BW + (h + 1) * DH]))
            return carry

        lax.fori_loop(0, nc, chunk, 0)

    row = pl.BlockSpec((ts, BW), lambda i: (i, 0))
    one = pl.BlockSpec((1, BW), lambda i: (0, 0))
    return _call(
        body, name="hgrn_fwd", grid=(S // ts,),
        in_specs=[pl.BlockSpec((ts, 1024), lambda i: (i, ZC // 1024)), one, one,
                  pl.BlockSpec(tmat.shape, lambda i: (0, 0)), pl.BlockSpec(masks.shape, lambda i: (0, 0, 0))],
        out_specs=[row, row, row],
        out_shape=[jax.ShapeDtypeStruct((S, BW), F32), jax.ShapeDtypeStruct((S, BW), F32),
                   jax.ShapeDtypeStruct((S, BW), MXU_DTYPE)],
        scratch=[pltpu.VMEM((CHUNK, BW), F32)], sem=("arbitrary",))(z, lb, gain, tmat, masks)


def _hgrn_bwd(z, lb, gain, tmat, masks, o_pre, states, dy):
    S = z.shape[0]
    ts = _tile(S, ROW_TILE)
    nt = S // ts
    nc = ts // CHUNK

    def body(zc_ref, lb_ref, gn_ref, tm_ref, mk_ref, o_ref, st_ref, dy_ref, dz_ref, stat_ref, dstate):
        @pl.when(pl.program_id(0) == 0)
        def _():
            dstate[...] = jnp.zeros_like(dstate)
            stat_ref[...] = jnp.zeros_like(stat_ref)

        lbv, gn, tm, mk = lb_ref[...], gn_ref[...], tm_ref[...], mk_ref[...]
        upper = _tri(CHUNK, upper=True)
        lower_strict = 1.0 - upper

        def chunk(cc, carry):
            c = nc - 1 - cc
            r0 = pl.multiple_of(c * CHUNK, CHUNK)
            zc = zc_ref[pl.ds(r0, CHUNK), :]
            cq, cf, q, sg, g, kf, v, e, b, blast = _hgrn_chunk_fwd(zc, lbv, tm)
            eb = jnp.exp(b)
            ebl = jnp.exp(blast - b)
            qe = q * eb
            kd = kf * ebl
            o = o_ref[pl.ds(r0, CHUNK), :]
            dyv = dy_ref[pl.ds(r0, CHUNK), :]
            stp = st_ref[pl.ds(r0, CHUNK), :]
            cg = zc[:, 3 * BW:4 * BW]
            sgate = _silu(cg)
            dq_parts, dk_parts, dv_parts, dcg_parts = [], [], [], []
            dgain, up_parts, lo_parts, const_parts = [], [], [], []
            for h in range(NH):
                hs = _hs(h)
                oh = o[:, hs]
                r = lax.rsqrt(jnp.mean(oh * oh, axis=-1, keepdims=True) + EPS)
                nrm = oh * r
                dyn = dyv[:, hs] * sgate[:, hs]
                dcg_parts.append(dyv[:, hs] * nrm * gn[:, hs] * _dsilu(cg[:, hs]))
                dgain.append(jnp.sum(dyn * nrm, axis=0, keepdims=True))
                tt = dyn * gn[:, hs]
                doh = r * (tt - nrm * jnp.mean(tt * nrm, axis=-1, keepdims=True))
                a, ql, kl = _hgrn_scores(q, kf, e, mk, h)
                da = _dot_nt(doh, v[:, hs])
                dsth = dstate[:, hs]
                ebh = jnp.exp(blast[:, hs])
                dv_parts.append(_dot_tn(a, doh) + _dot_nt(kd[:, hs], dsth))
                dq_inter = eb[:, hs] * _dot(doh, stp[:, hs])
                dk_state = ebl[:, hs] * _dot(v[:, hs], dsth)
                dqh, dkh, gh = dq_inter, dk_state, None
                for lev in range(NLEV + 1):
                    dal = mk[lev] * da
                    xq = _dot(dal, kl[lev][0])
                    yk = _dot_tn(dal, ql[lev][0])
                    gterm = ql[lev][0] * xq - kl[lev][0] * yk
                    gh = gterm if gh is None else gh + gterm
                    dqh = dqh + (xq if lev == NLEV else ql[lev][1] * xq)
                    dkh = dkh + (yk if lev == NLEV else kl[lev][1] * yk)
                up_parts.append(gh + q[:, hs] * dq_inter)
                lo_parts.append(kf[:, hs] * dk_state)
                const_parts.append(jnp.sum(dsth * stp[:, hs], axis=0, keepdims=True) * ebh)
                dstate[:, hs] = dsth * ebh + _dot_tn(doh, qe[:, hs])
                dq_parts.append(dqh)
                dk_parts.append(dkh)
            dq = jnp.concatenate(dq_parts, axis=1)
            dk = jnp.concatenate(dk_parts, axis=1)
            dlg = (_dot_exact_lhs01(upper, jnp.concatenate(up_parts, axis=1))
                   + _dot_exact_lhs01(lower_strict, jnp.concatenate(lo_parts, axis=1))
                   + jnp.concatenate(const_parts, axis=1))
            dsg = sg * (1.0 - sg)
            dz_ref[pl.ds(r0, CHUNK), 0:BW] = _mx(dq * _dsilu(cq))
            dz_ref[pl.ds(r0, CHUNK), BW:2 * BW] = _mx((dlg / g - dk) * (1.0 - lbv) * dsg)
            dz_ref[pl.ds(r0, CHUNK), 2 * BW:3 * BW] = _mx(jnp.concatenate(dv_parts, axis=1))
            dz_ref[pl.ds(r0, CHUNK), 3 * BW:4 * BW] = _mx(jnp.concatenate(dcg_parts, axis=1))
            stat_ref[0:1, :] += jnp.concatenate(dgain, axis=1)
            stat_ref[1:2, :] += jnp.sum((dlg / g - dk) * (1.0 - sg), axis=0, keepdims=True)
            return carry

        lax.fori_loop(0, nc, chunk, 0)

    rev = lambda w, c=0: pl.BlockSpec((ts, w), lambda i: (nt - 1 - i, c))
    one = pl.BlockSpec((1, BW), lambda i: (0, 0))
    return _call(
        body, name="hgrn_bwd", grid=(nt,),
        in_specs=[rev(1024, ZC // 1024), one, one, pl.BlockSpec(tmat.shape, lambda i: (0, 0)),
                  pl.BlockSpec(masks.shape, lambda i: (0, 0, 0)), rev(BW), rev(BW), rev(BW)],
        out_specs=[rev(1024), pl.BlockSpec((8, BW), lambda i: (0, 0))],
        out_shape=[jax.ShapeDtypeStruct((S, 1024), MXU_DTYPE), jax.ShapeDtypeStruct((8, BW), F32)],
        scratch=[pltpu.VMEM((CHUNK, BW), F32)],
        sem=("arbitrary",))(z, lb, gain, tmat, masks, o_pre, states, dy)


def _lower_bounds_fwd(lb_logits):
    def body(l_ref, o_ref):
        l = l_ref[...]
        m = jnp.max(l, axis=0, keepdims=True)
        ex = jnp.exp(l - m)
        p = ex / jnp.sum(ex, axis=0, keepdims=True)
        cs = p[0:1, :]
        o_ref[0:1, :] = jnp.clip(cs - p[0:1, :], 0.0, 1.0)
        for d in range(1, DEPTH):
            cs = cs + p[d:d + 1, :]
            o_ref[d:d + 1, :] = jnp.clip(cs - p[0:1, :], 0.0, 1.0)

    full = pl.BlockSpec((DEPTH, BW), lambda: (0, 0))
    return _call(body, name="lower_bounds_fwd", grid=(), in_specs=[full], out_specs=full,
                 out_shape=jax.ShapeDtypeStruct((DEPTH, BW), F32))(lb_logits)


def _lower_bounds_bwd(lb_logits, dlow):
    def body(l_ref, d_ref, o_ref):
        l = l_ref[...]
        m = jnp.max(l, axis=0, keepdims=True)
        ex = jnp.exp(l - m)
        p = ex / jnp.sum(ex, axis=0, keepdims=True)
        dl = d_ref[...]
        cs = p[0:1, :]
        dcs = []
        for d in range(DEPTH):
            if d > 0:
                cs = cs + p[d:d + 1, :]
            val = cs - p[0:1, :]
            dcs.append(jnp.where((val > 0.0) & (val < 1.0), dl[d:d + 1, :], 0.0))
        total = dcs[0] + dcs[1] + dcs[2] + dcs[3]
        dp = []
        for j in range(DEPTH):
            s = dcs[j]
            for d in range(j + 1, DEPTH):
                s = s + dcs[d]
            dp.append(s - total if j == 0 else s)
        inner = p[0:1, :] * dp[0]
        for j in range(1, DEPTH):
            inner = inner + p[j:j + 1, :] * dp[j]
        for j in range(DEPTH):
            o_ref[j:j + 1, :] = p[j:j + 1, :] * (dp[j] - inner)

    full = pl.BlockSpec((DEPTH, BW), lambda: (0, 0))
    return _call(body, name="lower_bounds_bwd", grid=(), in_specs=[full, full], out_specs=full,
                 out_shape=jax.ShapeDtypeStruct((DEPTH, BW), F32))(lb_logits, dlow)


def _sgu_fwd(z, gv, ws, bs):
    S = z.shape[0]
    ts = _tile(S, ROW_TILE)
    nc = ts // SCHUNK

    def body(zd_ref, gv_ref, ws_ref, bs_ref, y_ref):
        gvv, bsv = gv_ref[...], bs_ref[...]
        tril = _tri(SCHUNK)
        for c in range(nc):
            rs = slice(c * SCHUNK, (c + 1) * SCHUNK)
            zd = zd_ref[rs, :]
            for h in range(NH):
                vn = _rms_fwd(zd[:, BW + h * DH:BW + (h + 1) * DH], gvv[:, _hs(h)])
                s = _dot(ws_ref[h] * tril, vn) + bsv[:, h:h + 1]
                y_ref[rs, _hs(h)] = _mx(zd[:, _hs(h)] * s * _silu(zd[:, 2 * BW + h * DH:2 * BW + (h + 1) * DH]))

    return _call(
        body, name="sgu_fwd", grid=(S // ts,),
        in_specs=[pl.BlockSpec((ts, 1024), lambda i: (i, ZD // 1024)), pl.BlockSpec((1, BW), lambda i: (0, 0)),
                  pl.BlockSpec((NH, SCHUNK, SCHUNK), lambda i: (0, 0, 0)), pl.BlockSpec((SCHUNK, 128), lambda i: (0, 0))],
        out_specs=pl.BlockSpec((ts, BW), lambda i: (i, 0)),
        out_shape=jax.ShapeDtypeStruct((S, BW), MXU_DTYPE), sem=("parallel",))(z, gv, ws, bs)


def _sgu_bwd(z, gv, ws, bs, dy):
    S = z.shape[0]
    ts = _tile(S, ROW_TILE)
    nc = ts // SCHUNK

    def body(zd_ref, gv_ref, ws_ref, bs_ref, dy_ref, dz_ref, dws_ref, dbs_ref, st_ref):
        @pl.when(pl.program_id(0) == 0)
        def _():
            dws_ref[...] = jnp.zeros_like(dws_ref)
            dbs_ref[...] = jnp.zeros_like(dbs_ref)
            st_ref[...] = jnp.zeros_like(st_ref)

        gvv, bsv = gv_ref[...], bs_ref[...]
        tril = _tri(SCHUNK)
        dz_ref[:, 3 * BW:4 * BW] = jnp.zeros((ts, BW), MXU_DTYPE)
        for c in range(nc):
            rs = slice(c * SCHUNK, (c + 1) * SCHUNK)
            zd = zd_ref[rs, :]
            dyv = dy_ref[rs, :]
            for h in range(NH):
                hs = _hs(h)
                u = zd[:, hs]
                vraw = zd[:, BW + h * DH:BW + (h + 1) * DH]
                gt = zd[:, 2 * BW + h * DH:2 * BW + (h + 1) * DH]
                gvh = gvv[:, hs]
                vn = _rms_fwd(vraw, gvh)
                wm = ws_ref[h] * tril
                s = _dot(wm, vn) + bsv[:, h:h + 1]
                sil = _silu(gt)
                d = dyv[:, hs]
                ds = d * u * sil
                dz_ref[rs, hs] = _mx(d * s * sil)
                dz_ref[rs, 2 * BW + h * DH:2 * BW + (h + 1) * DH] = _mx(d * u * s * _dsilu(gt))
                dws_ref[h] += tril * _dot_nt(ds, vn)
                dbs_ref[:, h:h + 1] += jnp.sum(ds, axis=-1, keepdims=True)
                dvn = _dot_tn(wm, ds)
                dx, dgr = _rms_bwd(vraw, gvh, dvn)
                dz_ref[rs, BW + h * DH:BW + (h + 1) * DH] = _mx(dx)
                st_ref[0:1, hs] += jnp.sum(dgr, axis=0, keepdims=True)

    return _call(
        body, name="sgu_bwd", grid=(S // ts,),
        in_specs=[pl.BlockSpec((ts, 1024), lambda i: (i, ZD // 1024)), pl.BlockSpec((1, BW), lambda i: (0, 0)),
                  pl.BlockSpec((NH, SCHUNK, SCHUNK), lambda i: (0, 0, 0)), pl.BlockSpec((SCHUNK, 128), lambda i: (0, 0)),
                  pl.BlockSpec((ts, BW), lambda i: (i, 0))],
        out_specs=[pl.BlockSpec((ts, 1024), lambda i: (i, 0)), pl.BlockSpec((NH, SCHUNK, SCHUNK), lambda i: (0, 0, 0)),
                   pl.BlockSpec((SCHUNK, 128), lambda i: (0, 0)), pl.BlockSpec((8, BW), lambda i: (0, 0))],
        out_shape=[jax.ShapeDtypeStruct((S, 1024), MXU_DTYPE), jax.ShapeDtypeStruct((NH, SCHUNK, SCHUNK), F32),
                   jax.ShapeDtypeStruct((SCHUNK, 128), F32), jax.ShapeDtypeStruct((8, BW), F32)],
        sem=("arbitrary",))(z, gv, ws, bs, dy)


def _merge_fwd(x, z, ys, wup, mb, wo):
    S = x.shape[0]
    ts = _tile(S, MERGE_TILE)

    def body(x_ref, zm_ref, ya_ref, yb_ref, yc_ref, yd_ref, wup_ref, mb_ref, wo_ref, x1_ref, mg_ref):
        yrefs = (ya_ref, yb_ref, yc_ref, yd_ref)
        mbv = mb_ref[...]
        merged = None
        for b in range(NBR):
            cs = slice(b * D_MODEL, (b + 1) * D_MODEL)
            term = _sigmoid(zm_ref[:, cs] + mbv[b:b + 1, :]) * jnp.dot(yrefs[b][...], wup_ref[b],
                                                                      preferred_element_type=F32)
            merged = term if merged is None else merged + term
        mg = _mx(merged)
        mg_ref[...] = mg
        x1_ref[...] = x_ref[...] + jnp.dot(mg, wo_ref[...], preferred_element_type=F32)

    row = lambda w: pl.BlockSpec((ts, w), lambda i: (i, 0))
    return _call(
        body, name="merge_fwd", grid=(S // ts,),
        in_specs=[row(D_MODEL), pl.BlockSpec((ts, 4096), lambda i: (i, 0)), row(BW), row(BW), row(BW), row(BW),
                  pl.BlockSpec((NBR, BW, D_MODEL), lambda i: (0, 0, 0)), pl.BlockSpec((NBR, D_MODEL), lambda i: (0, 0)),
                  pl.BlockSpec((D_MODEL, D_MODEL), lambda i: (0, 0))],
        out_specs=[row(D_MODEL), row(D_MODEL)],
        out_shape=[jax.ShapeDtypeStruct((S, D_MODEL), F32), jax.ShapeDtypeStruct((S, D_MODEL), MXU_DTYPE)],
        sem=("parallel",))(x, z, *ys, wup, mb, wo)


def _merge_bwd(dx1, z, ys, wup, wup_t, mb, wo_t):
    S = dx1.shape[0]
    ts = _tile(S, MERGE_TILE)

    def body(dx_ref, zm_ref, ya_ref, yb_ref, yc_ref, yd_ref, wup_ref, wupt_ref, mb_ref, wot_ref,
             dzm_ref, du_ref, dxb_ref, dy_ref, st_ref):
        @pl.when(pl.program_id(0) == 0)
        def _():
            st_ref[...] = jnp.zeros_like(st_ref)

        yrefs = (ya_ref, yb_ref, yc_ref, yd_ref)
        mbv = mb_ref[...]
        dxb = _mx(dx_ref[...])
        dxb_ref[...] = dxb
        dmerged = jnp.dot(dxb, wot_ref[...], preferred_element_type=F32)
        for b in range(NBR):
            cs = slice(b * D_MODEL, (b + 1) * D_MODEL)
            u = jnp.dot(yrefs[b][...], wup_ref[b], preferred_element_type=F32)
            sg = _sigmoid(zm_ref[:, cs] + mbv[b:b + 1, :])
            du = _mx(dmerged * sg)
            du_ref[:, cs] = du
            dzm = dmerged * u * sg * (1.0 - sg)
            dzm_ref[:, cs] = _mx(dzm)
            st_ref[b:b + 1, :] += jnp.sum(dzm, axis=0, keepdims=True)
            dy_ref[:, b * BW:(b + 1) * BW] = jnp.dot(du, wupt_ref[b], preferred_element_type=F32)

    row = lambda w: pl.BlockSpec((ts, w), lambda i: (i, 0))
    return _call(
        body, name="merge_bwd", grid=(S // ts,),
        in_specs=[row(D_MODEL), pl.BlockSpec((ts, 4096), lambda i: (i, 0)), row(BW), row(BW), row(BW), row(BW),
                  pl.BlockSpec((NBR, BW, D_MODEL), lambda i: (0, 0, 0)), pl.BlockSpec((NBR, D_MODEL, BW), lambda i: (0, 0, 0)),
                  pl.BlockSpec((NBR, D_MODEL), lambda i: (0, 0)), pl.BlockSpec((D_MODEL, D_MODEL), lambda i: (0, 0))],
        out_specs=[row(4096), row(4096), row(D_MODEL), row(D_MODEL), pl.BlockSpec((8, D_MODEL), lambda i: (0, 0))],
        out_shape=[jax.ShapeDtypeStruct((S, 4096), MXU_DTYPE), jax.ShapeDtypeStruct((S, 4096), MXU_DTYPE),
                   jax.ShapeDtypeStruct((S, D_MODEL), MXU_DTYPE), jax.ShapeDtypeStruct((S, D_MODEL), F32),
                   jax.ShapeDtypeStruct((8, D_MODEL), F32)],
        sem=("arbitrary",))(dx1, z, *ys, wup, wup_t, mb, wo_t)


def _ple_fwd(x1, p, g, wg, wp):
    S = x1.shape[0]
    ts = _tile(S, ROW_TILE)

    def body(x_ref, p_ref, g_ref, wg_ref, wp_ref, o_ref):
        x = x_ref[...]
        hp = _mx(_rms_fwd(x, g_ref[...]))
        gate = _sigmoid(jnp.dot(hp, wg_ref[...], preferred_element_type=F32))
        pp = jnp.dot(_mx(p_ref[...]), wp_ref[...], preferred_element_type=F32)
        o_ref[...] = x + gate * pp

    row = lambda w: pl.BlockSpec((ts, w), lambda i: (i, 0))
    return _call(
        body, name="ple_fwd", grid=(S // ts,),
        in_specs=[row(D_MODEL), row(PLE), pl.BlockSpec((1, D_MODEL), lambda i: (0, 0)),
                  pl.BlockSpec((D_MODEL, D_MODEL), lambda i: (0, 0)), pl.BlockSpec((PLE, D_MODEL), lambda i: (0, 0))],
        out_specs=row(D_MODEL), out_shape=jax.ShapeDtypeStruct((S, D_MODEL), F32),
        sem=("parallel",))(x1, p, g, wg, wp)


def _ple_bwd(x1, p, dx2, g, wg, wg_t, wp):
    S = x1.shape[0]
    ts = _tile(S, ROW_TILE)

    def body(x_ref, p_ref, dx_ref, g_ref, wg_ref, wgt_ref, wp_ref, dx1_ref, hp_ref, dgl_ref, dpp_ref, pb_ref, st_ref):
        @pl.when(pl.program_id(0) == 0)
        def _():
            st_ref[...] = jnp.zeros_like(st_ref)

        x, gv, dx2 = x_ref[...], g_ref[...], dx_ref[...]
        hp = _mx(_rms_fwd(x, gv))
        hp_ref[...] = hp
        gate = _sigmoid(jnp.dot(hp, wg_ref[...], preferred_element_type=F32))
        pb = _mx(p_ref[...])
        pb_ref[...] = pb
        pp = jnp.dot(pb, wp_ref[...], preferred_element_type=F32)
        dgl = _mx(dx2 * pp * gate * (1.0 - gate))
        dgl_ref[...] = dgl
        dpp_ref[...] = _mx(dx2 * gate)
        dhp = jnp.dot(dgl, wgt_ref[...], preferred_element_type=F32)
        dxn, dgr = _rms_bwd(x, gv, dhp)
        dx1_ref[...] = dx2 + dxn
        st_ref[0:1, :] += jnp.sum(dgr, axis=0, keepdims=True)

    row = lambda w: pl.BlockSpec((ts, w), lambda i: (i, 0))
    sq = pl.BlockSpec((D_MODEL, D_MODEL), lambda i: (0, 0))
    return _call(
        body, name="ple_bwd", grid=(S // ts,),
        in_specs=[row(D_MODEL), row(PLE), row(D_MODEL), pl.BlockSpec((1, D_MODEL), lambda i: (0, 0)), sq, sq,
                  pl.BlockSpec((PLE, D_MODEL), lambda i: (0, 0))],
        out_specs=[row(D_MODEL), row(D_MODEL), row(D_MODEL), row(D_MODEL), row(PLE),
                   pl.BlockSpec((8, D_MODEL), lambda i: (0, 0))],
        out_shape=[jax.ShapeDtypeStruct((S, D_MODEL), F32)] + [jax.ShapeDtypeStruct((S, D_MODEL), MXU_DTYPE)] * 3
        + [jax.ShapeDtypeStruct((S, PLE), MXU_DTYPE), jax.ShapeDtypeStruct((8, D_MODEL), F32)],
        sem=("arbitrary",))(x1, p, dx2, g, wg, wg_t, wp)


def _pad_rows(a, rows=8):
    return jnp.concatenate([a, jnp.zeros((rows - a.shape[0],) + a.shape[1:], a.dtype)], axis=0)


def _pad_lanes(a, lanes=128):
    return jnp.concatenate([a, jnp.zeros(a.shape[:-1] + (lanes - a.shape[-1],), a.dtype)], axis=-1)


def _wz_from_w_in(w):
    zeros = lambda n: jnp.zeros((w.shape[0], n), w.dtype)
    return jnp.concatenate([w[:, _OM:_OEND], w[:, _OA:_OB], w[:, _OB:_OF], w[:, _OC:_OD], w[:, _OD:_OM], zeros(256),
                            w[:, _OF:_OC], zeros(124)], axis=1)


def _w_in_from_wz(g):
    return jnp.concatenate([g[:, ZA:ZB], g[:, ZB:ZC], g[:, ZF:ZF + 4], g[:, ZC:ZD], g[:, ZD:ZD + 768], g[:, ZM:ZA]], axis=1)


def _local_step(x, p, target, wts, dist=None):
    tmat_np, masks_np = _hgrn_consts()
    tmat = jnp.asarray(tmat_np, MXU_DTYPE)
    masks = jnp.asarray(masks_np, F32)
    lower = _lower_bounds_fwd(wts["lb_logits"])
    saved = []
    for li in range(DEPTH):
        big = dist.weights(li) if dist else {n: wts[n][li] for n in _GATHERED}
        wz = _wz_from_w_in(big["w_in"])
        g_mix = wts["norm_mix"][li][None, :]
        h = _norm_fwd(x, g_mix)
        z = _mm_nn(h, wz, F32, name="mm_z", tn=1664)
        cw = _pad_rows(jnp.concatenate([big["conv_w"], wts["conv_b"][li][None, :]], axis=0))
        ya = _conv_fwd(z, cw)
        fb = _pad_lanes(wts["fgate_bias"][li][None, :])
        gq, gk = wts["q_norm"][li][None, :], wts["k_norm"][li][None, :]
        qs, kn, vb, cc, cr = _attn_prep_fwd(z, fb, gq, gk)
        o, lse, yb, got = _flash_fwd(qs, kn, vb, cc, cr, z, comm=dist.fwd_comm(li) if dist else None)
        if dist:
            dist.fwd_done(li, got)
        lb = lower[li][None, :]
        gh = wts["hgrn_norm"][li][None, :]
        o_pre, states, yc = _hgrn_fwd(z, lb, gh, tmat, masks)
        gv = wts["sgu_norm"][li][None, :]
        ws = wts["spatial_w"][li]
        bs = _pad_lanes(wts["spatial_b"][li].T)
        yd = _sgu_fwd(z, gv, ws, bs)
        ys = (ya, yb, yc, yd)
        x1, merged = _merge_fwd(x, z, ys, big["w_up"], big["merge_b"], big["w_o"])
        g_ple = wts["norm_ple"][li][None, :]
        x2 = _ple_fwd(x1, p[li], g_ple, big["w_ple_gate"], big["w_ple_proj"])
        saved.append(dict(x=x, h=h, z=z, wz=wz, cw=cw, fb=fb, gq=gq, gk=gk, qs=qs, kn=kn, vb=vb, cc=cc, cr=cr, o=o, lse=lse,
                          lb=lb, gh=gh, o_pre=o_pre, states=states, gv=gv, ws=ws, bs=bs, ys=ys, x1=x1, merged=merged,
                          g_mix=g_mix, g_ple=g_ple, big=big))
        x = x2

    loss, dx = _loss_fwd_bwd(x, target)

    names = ["norm_mix", "w_in", "conv_w", "conv_b", "fgate_bias", "q_norm", "k_norm", "lb", "hgrn_norm", "sgu_norm",
             "spatial_w", "spatial_b", "w_up", "merge_b", "w_o", "norm_ple", "w_ple_gate", "w_ple_proj"]
    gl = {n: [None] * DEPTH for n in names}
    for li in reversed(range(DEPTH)):
        s = saved[li]
        z, big = s["z"], s["big"]
        wg, wp = big["w_ple_gate"], big["w_ple_proj"]
        dx1, hp, dgl, dpp, pb, st = _ple_bwd(s["x1"], p[li], dx, s["g_ple"], wg, wg.T, wp)
        gl["norm_ple"][li] = st[0]
        gl["w_ple_gate"][li] = _mm_tn(hp, dgl, name="mm_dwg")
        gl["w_ple_proj"][li] = _mm_tn(pb, dpp, name="mm_dwp")
        wup, wo = big["w_up"], big["w_o"]
        dzm, du, dxb, dy, st = _merge_bwd(dx1, z, s["ys"], wup, jnp.swapaxes(wup, 1, 2), big["merge_b"], wo.T)
        gl["merge_b"][li] = st[0:NBR]
        gl["w_o"][li] = _mm_tn(s["merged"], dxb, name="mm_dwo")
        gl["w_up"][li] = jnp.stack([_mm_tn(s["ys"][b], du, name="mm_dwup", ycol=b, n=D_MODEL) for b in range(NBR)])
        dza, st = _conv_bwd(z, s["cw"], dy[:, 0:BW])
        gl["conv_w"][li] = st[0:3]
        gl["conv_b"][li] = st[3]
        do, dgate = _attn_gate_bwd(dy[:, BW:2 * BW], s["o"], z)
        fa = (s["qs"], s["kn"], s["vb"], s["cc"], s["cr"], s["lse"], s["o"], do)
        dq, drow, got = _flash_bwd_dq(*fa, comm=dist.bwd_comm() if dist else None)
        if dist:
            dist.bwd_done(got)
        dk, dv, dcr = _flash_bwd_dkv(*fa, drow)
        dcc = _pad_lanes(dcr.T)
        dzb, dzf, st = _attn_prep_bwd(z, s["fb"], s["gq"], s["gk"], dq, dk, dv, dgate, dcc)
        gl["q_norm"][li] = st[0, 0:DH]
        gl["k_norm"][li] = st[1, 0:DH]
        gl["fgate_bias"][li] = st[2, 0:NH]
        dzc, st = _hgrn_bwd(z, s["lb"], s["gh"], tmat, masks, s["o_pre"], s["states"], dy[:, 2 * BW:3 * BW])
        gl["hgrn_norm"][li] = st[0]
        gl["lb"][li] = st[1]
        dzd, dws, dbs, st = _sgu_bwd(z, s["gv"], s["ws"], s["bs"], dy[:, 3 * BW:4 * BW])
        gl["sgu_norm"][li] = st[0]
        gl["spatial_w"][li] = dws
        gl["spatial_b"][li] = dbs[:, 0:NH].T
        dz = jnp.concatenate([dzm, dza, dzb, dzc, dzd, dzf], axis=1)
        gl["w_in"][li] = _w_in_from_wz(_mm_tn(s["h"], dz, name="mm_dwz", tn=1664))
        dh = _mm_nn(dz, s["wz"].T, F32, name="mm_dh", tk=1664)
        dx, st = _norm_bwd(s["x"], s["g_mix"], dh, dx1)
        gl["norm_mix"][li] = st[0]
        if dist:
            dist.push(li, {n: gl[n][li] for n in _BIG})

    if dist:
        dist.finish()
    grads = {n: jnp.stack(v) for n, v in gl.items() if not (dist and n in _BIG)}
    grads["lb_logits"] = _lower_bounds_bwd(wts["lb_logits"], grads.pop("lb"))
    return loss, dx, grads


def _my_pos():
    return lax.axis_index("x"), lax.axis_index("y"), lax.axis_index("c")


def _gather_phase(phase, ins, outs, sems):
    if phase == 1:
        return
    send, recv, lsem = sems
    x, y, c = _my_pos()
    me = 2 * x + y
    peers = [(1 - x, y), (x, 1 - y), (1 - x, 1 - y)]
    copies = []
    for t in range(len(ins)):
        copies.append(pltpu.make_async_copy(ins[t], outs[t].at[me], lsem.at[t]))
        for j, (px, py) in enumerate(peers):
            copies.append(pltpu.make_async_remote_copy(
                src_ref=ins[t], dst_ref=outs[t].at[me], send_sem=send.at[t, j], recv_sem=recv.at[t, j],
                device_id=(px, py, c), device_id_type=MESH))
    for cp in copies:
        if phase == 0:
            cp.start()
        else:
            cp.wait()


def _exchange_phase(phase, ins, outs, sems):
    send, recv, lsem = sems
    x, y, c = _my_pos()
    me = 2 * x + y
    sib = (x, y, 1 - c)
    chips = [(1 - x, y), (x, 1 - y), (1 - x, 1 - y)]

    def rc(t, k, src, dst, dev):
        return pltpu.make_async_remote_copy(src_ref=src, dst_ref=dst, send_sem=send.at[t, k], recv_sem=recv.at[t, k],
                                            device_id=dev, device_id_type=MESH)

    for t in range(len(ins)):
        local = pltpu.make_async_copy(ins[t].at[me], outs[t].at[2 * me + c], lsem.at[t])
        first = [rc(t, 0, ins[t].at[me], outs[t].at[2 * me + c], sib)]
        first += [rc(t, 1 + j, ins[t].at[2 * px + py], outs[t].at[2 * me + c], (px, py, c))
                  for j, (px, py) in enumerate(chips)]
        landed = [outs[t].at[2 * (2 * px + py) + c] for px, py in chips]
        passed = [rc(t, 4 + j, slot, slot, sib) for j, slot in enumerate(landed)]
        if phase == 0:
            local.start()
            for cp in first:
                cp.start()
        elif phase == 1:
            for j, slot in enumerate(landed):
                rc(t, 1 + j, slot, slot, (x, y, c)).wait_recv()
                passed[j].start()
        else:
            s0 = outs[t].at[2 * me + (1 - c)]
            rc(t, 0, s0, s0, (x, y, c)).wait_recv()
            for j, (px, py) in enumerate(chips):
                slot = outs[t].at[2 * (2 * px + py) + (1 - c)]
                rc(t, 4 + j, slot, slot, (x, y, c)).wait_recv()
            for cp in first + passed:
                cp.wait_send()
            local.wait()


_COMM_PHASES = {"gather": _gather_phase, "exchange": _exchange_phase}
_COMM_OUT = {"gather": lambda arrays: [jax.ShapeDtypeStruct((4,) + a.shape, a.dtype) for a in arrays],
             "exchange": lambda arrays: [jax.ShapeDtypeStruct((8,) + a.shape[1:], a.dtype) for a in arrays]}


def _comm_sems(kind, nc):
    k = 3 if kind == "gather" else 7
    return [pltpu.SemaphoreType.DMA((nc, k)), pltpu.SemaphoreType.DMA((nc, k)), pltpu.SemaphoreType.DMA((nc,))]


def _comm_alone(kind, arrays, name):
    nc = len(arrays)

    def body(*refs):
        for phase in range(3):
            _COMM_PHASES[kind](phase, refs[:nc], refs[nc:2 * nc], refs[2 * nc:])

    hbm = pl.BlockSpec(memory_space=pl.ANY)
    return pl.pallas_call(
        functools.partial(body), name=name, in_specs=[hbm] * nc, out_specs=[hbm] * nc,
        out_shape=_COMM_OUT[kind](arrays), scratch_shapes=_comm_sems(kind, nc))(*arrays)


def _allreduce_small(v):
    R = v.shape[0]

    def body(v_ref, o_ref, buf, send, recv):
        x, y, c = _my_pos()
        me = 4 * x + 2 * y + c
        buf[me] = v_ref[...]
        copies = []
        k = 0
        for dx in range(2):
            for dy in range(2):
                for dc in range(2):
                    if dx == 0 and dy == 0 and dc == 0:
                        continue
                    cp = pltpu.make_async_remote_copy(
                        src_ref=v_ref, dst_ref=buf.at[me], send_sem=send.at[k - 0], recv_sem=recv.at[k],
                        device_id=(jnp.bitwise_xor(x, dx), jnp.bitwise_xor(y, dy), jnp.bitwise_xor(c, dc)),
                        device_id_type=MESH)
                    cp.start()
                    copies.append(cp)
                    k += 1
        for cp in copies:
            cp.wait()
        acc = buf[0]
        for d in range(1, 8):
            acc = acc + buf[d]
        o_ref[...] = acc

    vm = pl.BlockSpec(memory_space=pltpu.VMEM)
    return pl.pallas_call(
        functools.partial(body), name="allreduce_small", in_specs=[vm], out_specs=vm,
        out_shape=jax.ShapeDtypeStruct((R, 128), F32),
        scratch_shapes=[pltpu.VMEM((8, R, 128), F32), pltpu.SemaphoreType.DMA((7,)), pltpu.SemaphoreType.DMA((7,))],
        compiler_params=pltpu.CompilerParams(vmem_limit_bytes=VMEM_LIMIT))(v)


def _adamw(w, m, v, parts, nparts):
    A, R, C = w.shape
    per_layer = isinstance(parts, (list, tuple))
    tr = _tile(R, 32) if per_layer else (R if R <= 128 else _tile(R, 128))
    nparr = len(parts) if per_layer else 1

    def body(*refs):
        w_ref, m_ref, v_ref = refs[:3]
        p_refs = refs[3:3 + nparr]
        g_ref, d_ref, nm_ref, nv_ref = refs[3 + nparr:]

        def update(read):
            g = read(0)
            for k in range(1, nparts):
                g = g + read(k)
            mm = ADAM_B1 * m_ref[0] + (1.0 - ADAM_B1) * g
            vv = ADAM_B2 * v_ref[0] + (1.0 - ADAM_B2) * jnp.square(g)
            m_hat = mm / (1.0 - ADAM_B1 ** ADAM_STEP)
            v_hat = vv / (1.0 - ADAM_B2 ** ADAM_STEP)
            g_ref[0] = g
            d_ref[0] = -ADAM_LR * (m_hat / (jnp.sqrt(v_hat) + ADAM_EPS) + ADAM_WD * w_ref[0])
            nm_ref[0] = mm
            nv_ref[0] = vv

        if not per_layer:
            update(lambda k: p_refs[0][k, 0])
        else:
            for a in range(A):
                @pl.when(pl.program_id(0) == a)
                def _(a=a):
                    update(lambda k: p_refs[a][k])

    blk = pl.BlockSpec((1, tr, C), lambda a, r: (a, r, 0))
    if per_layer:
        pspecs = [pl.BlockSpec((nparts, tr, C), functools.partial(lambda a, r, l: (0, jnp.where(a == l, r, 0), 0), l=l))
                  for l in range(A)]
        pargs = list(parts)
    else:
        pspecs = [pl.BlockSpec((nparts, 1, tr, C), lambda a, r: (0, a, r, 0))]
        pargs = [parts]
    return _call(
        body, name="adamw", grid=(A, R // tr), in_specs=[blk, blk, blk] + pspecs,
        out_specs=[blk] * 4, out_shape=[jax.ShapeDtypeStruct((A, R, C), F32)] * 4,
        sem=("arbitrary", "arbitrary"))(w, m, v, *pargs)


def _as3d(a):
    if a.ndim == 2:
        return a[None]
    if a.ndim == 3:
        return a
    return a.reshape((-1,) + a.shape[-2:])


_WEIGHTS = ["norm_mix", "w_in", "conv_w", "conv_b", "fgate_bias", "q_norm", "k_norm", "lb_logits", "hgrn_norm", "sgu_norm",
            "spatial_w", "spatial_b", "w_up", "merge_b", "w_o", "norm_ple", "w_ple_gate", "w_ple_proj"]
_BIG = ["w_in", "w_up", "w_o", "w_ple_gate", "w_ple_proj"]
_GATHERED = _BIG + ["conv_w", "merge_b"]
_SHARD_AXIS = {"w_in": 2, "w_up": 3, "w_o": 1, "w_ple_gate": 1, "w_ple_proj": 2, "conv_w": 2, "merge_b": 2}
_SMALL = [n for n in _WEIGHTS if n not in _BIG]


class _Dist:
    def __init__(self, w):
        self.w = w
        self.full = {0: self._unpack(_comm_alone("gather", self._shards(0), "gather_weights"))}
        self.contrib = {}
        self.pending = None

    def _shards(self, l):
        return [_mx(self.w[n][l]) for n in _BIG] + [self.w["conv_w"][l], self.w["merge_b"][l]]

    def _unpack(self, gathered):
        return {n: jnp.concatenate([g[k] for k in range(4)], axis=_SHARD_AXIS[n] - 1) for n, g in zip(_GATHERED, gathered)}

    def weights(self, l):
        return self.full[l]

    def fwd_comm(self, l):
        return ("gather", self._shards(l + 1)) if l + 1 < DEPTH else None

    def fwd_done(self, l, got):
        if got:
            self.full[l + 1] = self._unpack(got)

    def push(self, l, grads):
        self.pending = (l, [jnp.stack(jnp.split(grads[n], 4, axis=_SHARD_AXIS[n] - 1)) for n in _BIG])

    def bwd_comm(self):
        return ("exchange", self.pending[1]) if self.pending else None

    def bwd_done(self, got):
        if got:
            self.contrib[self.pending[0]] = got
            self.pending = None

    def finish(self):
        if self.pending:
            self.contrib[self.pending[0]] = _comm_alone("exchange", self.pending[1], "exchange_grads")
            self.pending = None


def kernel(x, p, norm_mix, w_in, conv_w, conv_b, fgate_bias, q_norm, k_norm, lb_logits, hgrn_norm, sgu_norm, spatial_w, spatial_b, w_up, merge_b, w_o, norm_ple, w_ple_gate, w_ple_proj, loss_target, m_norm_mix, m_w_in, m_conv_w, m_conv_b, m_fgate_bias, m_q_norm, m_k_norm, m_lb_logits, m_hgrn_norm, m_sgu_norm, m_spatial_w, m_spatial_b, m_w_up, m_merge_b, m_w_o, m_norm_ple, m_w_ple_gate, m_w_ple_proj, v_norm_mix, v_w_in, v_conv_w, v_conv_b, v_fgate_bias, v_q_norm, v_k_norm, v_lb_logits, v_hgrn_norm, v_sgu_norm, v_spatial_w, v_spatial_b, v_w_up, v_merge_b, v_w_o, v_norm_ple, v_w_ple_gate, v_w_ple_proj):
    loc = dict(locals())
    w = {n: loc[n] for n in _WEIGHTS}
    m = {n: loc["m_" + n] for n in _WEIGHTS}
    v = {n: loc["v_" + n] for n in _WEIGHTS}
    chip = 2 * lax.axis_index("x") + lax.axis_index("y")

    dist = _Dist(w)
    loss_blk, dx, grads = _local_step(x[0], p[:, 0], loss_target[0], w, dist)
    loss = lax.psum(loss_blk[0, 0], ("x", "y", "c"))

    flat = jnp.concatenate([grads[n].reshape(-1) for n in _SMALL])
    npad = (-flat.shape[0]) % 1024
    packed = jnp.concatenate([flat, jnp.zeros((npad,), F32)]).reshape(-1, 128)
    red = _allreduce_small(packed).reshape(-1)
    small, off = {}, 0
    for n in _SMALL:
        sz = int(np.prod(grads[n].shape))
        small[n] = red[off:off + sz].reshape(grads[n].shape)
        off += sz
    for n in ("conv_w", "merge_b"):
        ax = _SHARD_AXIS[n]
        width = small[n].shape[ax] // 4
        small[n] = lax.dynamic_slice_in_dim(small[n], chip * width, width, axis=ax)

    out_g, out_d, out_m, out_v = {}, {}, {}, {}
    for n in _WEIGHTS:
        shp = w[n].shape
        if n in _BIG:
            w3, m3, v3 = (a.reshape((DEPTH, -1, shp[-1])) for a in (w[n], m[n], v[n]))
            parts = [dist.contrib[l][_BIG.index(n)].reshape((8,) + w3.shape[1:]) for l in range(DEPTH)]
            g, d, nm, nv = _adamw(w3, m3, v3, parts, 8)
        else:
            g, d, nm, nv = _adamw(_as3d(w[n]), _as3d(m[n]), _as3d(v[n]), _as3d(small[n])[None], 1)
        out_g[n], out_d[n], out_m[n], out_v[n] = (a.reshape(shp) for a in (g, d, nm, nv))

    return (loss, dx[None], *[out_g[n] for n in _WEIGHTS], *[out_d[n] for n in _WEIGHTS],
            *[out_m[n] for n in _WEIGHTS], *[out_v[n] for n in _WEIGHTS])
```

```python
import functools

import numpy as np
import jax
import jax.numpy as jnp
from jax import lax
from jax.experimental import pallas as pl
from jax.experimental.pallas import tpu as pltpu

F32 = jnp.float32
MXU_DTYPE = jnp.bfloat16

D_MODEL = 1024
BW = 256
NH = 4
DH = 64
DEPTH = 4
NBR = 4
PLE = 256
CHUNK = 64
SCHUNK = 128
EPS = 1e-6
MASK_VALUE = -1e30
NLEV = 6

ADAM_LR, ADAM_B1, ADAM_B2, ADAM_EPS, ADAM_WD, ADAM_STEP = 0.001, 0.9, 0.999, 1e-08, 0.01, 10

ZM, ZA, ZB, ZC, ZD, ZF = 0, 4096, 5120, 6144, 7168, 8192
ZCOLS = 8320
_OA, _OB, _OF, _OC, _OD, _OM, _OEND = 0, 1024, 2048, 2052, 3076, 3844, 7940

VMEM_LIMIT = 56 * 1024 * 1024
ROW_TILE = 512
FLASH_TILE = 512
MERGE_TILE = 256
MESH = pl.DeviceIdType.MESH
_ANY = pl.BlockSpec(memory_space=pl.ANY)


def _tile(n, pref):
    t = min(n, pref)
    assert n % t == 0, (n, t)
    return t


def _call(body, *, name, grid, in_specs, out_specs, out_shape, scratch=(), sem=None, aliases=None):
    return pl.pallas_call(
        functools.partial(body), name=name, grid=grid, in_specs=in_specs, out_specs=out_specs,
        out_shape=out_shape, scratch_shapes=list(scratch), input_output_aliases=aliases or {},
        compiler_params=pltpu.CompilerParams(dimension_semantics=sem, vmem_limit_bytes=VMEM_LIMIT))


def _mx(x):
    return x.astype(MXU_DTYPE)


def _dot(a, b):
    return jnp.dot(_mx(a), _mx(b), preferred_element_type=F32)


def _dot_nt(a, b):
    return lax.dot_general(_mx(a), _mx(b), (((1,), (1,)), ((), ())), preferred_element_type=F32)


def _dot_tn(a, b):
    return lax.dot_general(_mx(a), _mx(b), (((0,), (0,)), ((), ())), preferred_element_type=F32)


def _top16(x):
    u = lax.bitcast_convert_type(x, jnp.uint32) & jnp.uint32(0xFFFF0000)
    return lax.bitcast_convert_type(u, F32)


def _split3(x):
    hi = _top16(x)
    r1 = x - hi
    mid = _top16(r1)
    return _mx(hi), _mx(mid), _mx(r1 - mid)


def _dot_exact_lhs01(t, x):
    hi, mid, lo = _split3(x)
    t = _mx(t)
    return (jnp.dot(t, hi, preferred_element_type=F32) + jnp.dot(t, mid, preferred_element_type=F32)
            + jnp.dot(t, lo, preferred_element_type=F32))


def _sigmoid(x):
    return jax.nn.sigmoid(x)


def _silu(x):
    return x * _sigmoid(x)


def _dsilu(x):
    s = _sigmoid(x)
    return s * (1.0 + x * (1.0 - s))


def _log_sigmoid(x):
    return jnp.minimum(x, 0.0) - jnp.log(1.0 + jnp.exp(-jnp.abs(x)))


def _rms_fwd(x, g):
    r = lax.rsqrt(jnp.mean(x * x, axis=-1, keepdims=True) + EPS)
    return x * r * g


def _rms_bwd(x, g, dy):
    r = lax.rsqrt(jnp.mean(x * x, axis=-1, keepdims=True) + EPS)
    n = x * r
    t = dy * g
    dx = r * (t - n * jnp.mean(t * n, axis=-1, keepdims=True))
    return dx, dy * n


def _tri(n, upper=False):
    r = lax.broadcasted_iota(jnp.int32, (n, n), 0)
    c = lax.broadcasted_iota(jnp.int32, (n, n), 1)
    return jnp.where((c >= r) if upper else (r >= c), 1.0, 0.0).astype(F32)


def _hs(h):
    return slice(h * DH, (h + 1) * DH)


def _mm_nn(a, b, out_dtype, *, name, tn=None, tk=None, b_transposed=False):
    S, K = a.shape
    N = b.shape[0] if b_transposed else b.shape[1]
    ts = _tile(S, ROW_TILE)
    tn = _tile(N, tn or N)
    tk = _tile(K, tk or K)
    nk = K // tk
    b_spec = (pl.BlockSpec((tn, tk), lambda j, i, k: (j, k)) if b_transposed
              else pl.BlockSpec((tk, tn), lambda j, i, k: (k, j)))

    def body(a_ref, b_ref, o_ref, acc_ref):
        k = pl.program_id(2)
        part = _dot_nt(a_ref[...], b_ref[...]) if b_transposed else jnp.dot(a_ref[...], b_ref[...],
                                                                          preferred_element_type=F32)
        if nk == 1:
            o_ref[...] = part.astype(o_ref.dtype)
        else:
            @pl.when(k == 0)
            def _():
                acc_ref[...] = part

            @pl.when(k > 0)
            def _():
                acc_ref[...] += part

            @pl.when(k == nk - 1)
            def _():
                o_ref[...] = acc_ref[...].astype(o_ref.dtype)

    return _call(
        body, name=name, grid=(N // tn, S // ts, nk),
        in_specs=[pl.BlockSpec((ts, tk), lambda j, i, k: (i, k)), b_spec],
        out_specs=pl.BlockSpec((ts, tn), lambda j, i, k: (i, j)),
        out_shape=jax.ShapeDtypeStruct((S, N), out_dtype),
        scratch=[pltpu.VMEM((ts, tn) if nk > 1 else (8, 128), F32)],
        sem=("parallel", "parallel", "arbitrary"))(a, b)


def _mm_tn(x, y, *, name, tn=None, ycol=0, n=None):
    S, M = x.shape
    n = n or y.shape[1]
    ts = _tile(S, ROW_TILE)
    tn = _tile(n, tn or n)
    nj = n // tn

    def body(x_ref, y_ref, o_ref):
        @pl.when(pl.program_id(1) == 0)
        def _():
            o_ref[...] = jnp.zeros_like(o_ref)

        o_ref[...] += lax.dot_general(x_ref[...], y_ref[...], (((0,), (0,)), ((), ())), preferred_element_type=F32)

    return _call(
        body, name=name, grid=(nj, S // ts),
        in_specs=[pl.BlockSpec((ts, M), lambda j, s: (s, 0)), pl.BlockSpec((ts, tn), lambda j, s: (s, ycol * nj + j))],
        out_specs=pl.BlockSpec((M, tn), lambda j, s: (0, j)),
        out_shape=jax.ShapeDtypeStruct((M, n), F32),
        sem=("parallel", "arbitrary"))(x, y)


def _norm_fwd(x, g):
    S = x.shape[0]
    ts = _tile(S, ROW_TILE)

    def body(x_ref, g_ref, h_ref):
        h_ref[...] = _mx(_rms_fwd(x_ref[...], g_ref[...]))

    return _call(
        body, name="norm_fwd", grid=(S // ts,),
        in_specs=[pl.BlockSpec((ts, D_MODEL), lambda i: (i, 0)), pl.BlockSpec((1, D_MODEL), lambda i: (0, 0))],
        out_specs=pl.BlockSpec((ts, D_MODEL), lambda i: (i, 0)),
        out_shape=jax.ShapeDtypeStruct((S, D_MODEL), MXU_DTYPE), sem=("parallel",))(x, g)


def _norm_bwd(x, g, dh, dres):
    S = x.shape[0]
    ts = _tile(S, ROW_TILE)

    def body(x_ref, g_ref, dh_ref, dr_ref, dx_ref, st_ref):
        @pl.when(pl.program_id(0) == 0)
        def _():
            st_ref[...] = jnp.zeros_like(st_ref)

        dx, dgr = _rms_bwd(x_ref[...], g_ref[...], dh_ref[...])
        dx_ref[...] = dr_ref[...] + dx
        st_ref[0:1, :] += jnp.sum(dgr, axis=0, keepdims=True)

    row = pl.BlockSpec((ts, D_MODEL), lambda i: (i, 0))
    return _call(
        body, name="norm_bwd", grid=(S // ts,),
        in_specs=[row, pl.BlockSpec((1, D_MODEL), lambda i: (0, 0)), row, row],
        out_specs=[row, pl.BlockSpec((8, D_MODEL), lambda i: (0, 0))],
        out_shape=[jax.ShapeDtypeStruct((S, D_MODEL), F32), jax.ShapeDtypeStruct((8, D_MODEL), F32)],
        sem=("arbitrary",))(x, g, dh, dres)


def _loss_fwd_bwd(y, target):
    S = y.shape[0]
    ts = _tile(S, ROW_TILE)

    def body(y_ref, t_ref, l_ref, dy_ref):
        @pl.when(pl.program_id(0) == 0)
        def _():
            l_ref[...] = jnp.zeros_like(l_ref)

        err = y_ref[...] - t_ref[...]
        dy_ref[...] = err * (1.0 / D_MODEL)
        rowloss = jnp.mean(err * err, axis=-1, keepdims=True)
        l_ref[...] += 0.5 * jnp.sum(rowloss, axis=0, keepdims=True)

    row = pl.BlockSpec((ts, D_MODEL), lambda i: (i, 0))
    return _call(
        body, name="loss", grid=(S // ts,), in_specs=[row, row],
        out_specs=[pl.BlockSpec((8, 128), lambda i: (0, 0)), row],
        out_shape=[jax.ShapeDtypeStruct((8, 128), F32), jax.ShapeDtypeStruct((S, D_MODEL), F32)],
        sem=("arbitrary",))(y, target)


def _shift_down(x, k, halo, rows):
    y = pltpu.roll(x, k, 0)
    for j in range(k):
        y = jnp.where(rows == j, halo[8 - k + j:8 - k + j + 1, :], y)
    return y


def _shift_up(x, k, halo, rows, n):
    y = pltpu.roll(x, n - k, 0)
    for j in range(k):
        y = jnp.where(rows == n - k + j, halo[j:j + 1, :], y)
    return y


def _conv_parts(za, zh, cw, first, rows):
    ax, ab, ac, ag = za[:, 0:BW], za[:, BW:2 * BW], za[:, 2 * BW:3 * BW], za[:, 3 * BW:4 * BW]
    zz = ac * ax
    hz = jnp.where(first, 0.0, zh[:, 2 * BW:3 * BW] * zh[:, 0:BW])
    zz1 = _shift_down(zz, 1, hz, rows)
    zz2 = _shift_down(zz, 2, hz, rows)
    conv = zz2 * cw[0:1, :] + zz1 * cw[1:2, :] + zz * cw[2:3, :] + cw[3:4, :]
    return ax, ab, ac, ag, zz, zz1, zz2, conv


def _conv_fwd(z, cw):
    S = z.shape[0]
    ts = _tile(S, ROW_TILE)
    hb = ts // 8

    def body(za_ref, zh_ref, cw_ref, y_ref):
        i = pl.program_id(0)
        rows = lax.broadcasted_iota(jnp.int32, (ts, BW), 0)
        ax, ab, ac, ag, zz, zz1, zz2, conv = _conv_parts(za_ref[...], zh_ref[...], cw_ref[...], i == 0, rows)
        y_ref[...] = _mx(ab * conv * _silu(ag))

    return _call(
        body, name="conv_fwd", grid=(S // ts,),
        in_specs=[pl.BlockSpec((ts, 1024), lambda i: (i, ZA // 1024)),
                  pl.BlockSpec((8, 1024), lambda i: (jnp.maximum(i * hb - 1, 0), ZA // 1024)),
                  pl.BlockSpec((8, BW), lambda i: (0, 0))],
        out_specs=pl.BlockSpec((ts, BW), lambda i: (i, 0)),
        out_shape=jax.ShapeDtypeStruct((S, BW), MXU_DTYPE), sem=("parallel",))(z, z, cw)


def _conv_bwd(z, cw, dy, dz):
    S = z.shape[0]
    ts = _tile(S, ROW_TILE)
    hb = ts // 8
    nt = S // ts

    def body(za_ref, zh_ref, zn_ref, cw_ref, dy_ref, dyn_ref, dz_in, dz_ref, st_ref):
        i = pl.program_id(0)

        @pl.when(i == 0)
        def _():
            st_ref[...] = jnp.zeros_like(st_ref)

        cw = cw_ref[...]
        rows = lax.broadcasted_iota(jnp.int32, (ts, BW), 0)
        ax, ab, ac, ag, zz, zz1, zz2, conv = _conv_parts(za_ref[...], zh_ref[...], cw, i == 0, rows)
        dy = dy_ref[...]
        sg = _silu(ag)
        dc = dy * ab * sg
        zn = zn_ref[...]
        dcn = jnp.where(i == nt - 1, 0.0, dyn_ref[...] * zn[:, BW:2 * BW] * _silu(zn[:, 3 * BW:4 * BW]))
        dc1 = _shift_up(dc, 1, dcn, rows, ts)
        dc2 = _shift_up(dc, 2, dcn, rows, ts)
        dzz = dc * cw[2:3, :] + dc1 * cw[1:2, :] + dc2 * cw[0:1, :]
        dz_ref[:, 0:BW] = _mx(dzz * ac)
        dz_ref[:, BW:2 * BW] = _mx(dy * conv * sg)
        dz_ref[:, 2 * BW:3 * BW] = _mx(dzz * ax)
        dz_ref[:, 3 * BW:4 * BW] = _mx(dy * ab * conv * _dsilu(ag))
        st_ref[0:1, :] += jnp.sum(dc * zz2, axis=0, keepdims=True)
        st_ref[1:2, :] += jnp.sum(dc * zz1, axis=0, keepdims=True)
        st_ref[2:3, :] += jnp.sum(dc * zz, axis=0, keepdims=True)
        st_ref[3:4, :] += jnp.sum(dc, axis=0, keepdims=True)

    return _call(
        body, name="conv_bwd", grid=(nt,),
        in_specs=[pl.BlockSpec((ts, 1024), lambda i: (i, ZA // 1024)),
                  pl.BlockSpec((8, 1024), lambda i: (jnp.maximum(i * hb - 1, 0), ZA // 1024)),
                  pl.BlockSpec((8, 1024), lambda i: (jnp.minimum((i + 1) * hb, S // 8 - 1), ZA // 1024)),
                  pl.BlockSpec((8, BW), lambda i: (0, 0)),
                  pl.BlockSpec((ts, BW), lambda i: (i, 0)),
                  pl.BlockSpec((8, BW), lambda i: (jnp.minimum((i + 1) * hb, S // 8 - 1), 0)), _ANY],
        out_specs=[pl.BlockSpec((ts, 1024), lambda i: (i, ZA // 1024)), pl.BlockSpec((8, BW), lambda i: (0, 0))],
        out_shape=[jax.ShapeDtypeStruct((S, ZCOLS), MXU_DTYPE), jax.ShapeDtypeStruct((8, BW), F32)],
        sem=("arbitrary",), aliases={6: 0})(z, z, z, cw, dy, dy, dz)


def _attn_prep_fwd(z, fb, gq, gk):
    S = z.shape[0]
    ts = _tile(S, ROW_TILE)

    def body(zb_ref, zf_ref, fb_ref, gq_ref, gk_ref, q_ref, k_ref, v_ref, cc_ref, cr_ref, carry):
        @pl.when(pl.program_id(0) == 0)
        def _():
            carry[...] = jnp.zeros_like(carry)

        zb = zb_ref[...]
        gqv, gkv = gq_ref[...], gk_ref[...]
        for h in range(NH):
            q_ref[:, _hs(h)] = _mx(_rms_fwd(zb[:, _hs(h)], gqv) * (DH ** -0.5))
            k_ref[:, _hs(h)] = _mx(_rms_fwd(zb[:, BW + h * DH:BW + (h + 1) * DH], gkv))
        v_ref[...] = _mx(zb[:, 2 * BW:3 * BW])
        lf = _log_sigmoid(zf_ref[...] + fb_ref[...])
        cum = _dot_exact_lhs01(_tri(ts), lf) + carry[...]
        carry[...] = cum[ts - 1:ts, :]
        cc_ref[...] = cum
        cr_ref[...] = cum.T[0:8, :]

    row = lambda w: pl.BlockSpec((ts, w), lambda i: (i, 0))
    return _call(
        body, name="attn_prep_fwd", grid=(S // ts,),
        in_specs=[pl.BlockSpec((ts, 1024), lambda i: (i, ZB // 1024)), pl.BlockSpec((ts, 128), lambda i: (i, ZF // 128)),
                  pl.BlockSpec((1, 128), lambda i: (0, 0)), pl.BlockSpec((1, DH), lambda i: (0, 0)),
                  pl.BlockSpec((1, DH), lambda i: (0, 0))],
        out_specs=[row(BW), row(BW), row(BW), row(128), pl.BlockSpec((8, ts), lambda i: (0, i))],
        out_shape=[jax.ShapeDtypeStruct((S, BW), MXU_DTYPE)] * 3
        + [jax.ShapeDtypeStruct((S, 128), F32), jax.ShapeDtypeStruct((8, S), F32)],
        scratch=[pltpu.VMEM((1, 128), F32)], sem=("arbitrary",))(z, z, fb, gq, gk)


def _attn_scores(q, k, cc, cr, h, i, j, t, masked):
    s = _dot_nt(q[:, _hs(h)], k[:, _hs(h)]) + (cc[:, h:h + 1] - cr[h:h + 1, :])
    if not masked:
        return s, None
    qpos = i * t + lax.broadcasted_iota(jnp.int32, (t, t), 0)
    kpos = j * t + lax.broadcasted_iota(jnp.int32, (t, t), 1)
    return s, qpos >= kpos


ROW_CHUNK = 256


def _fuse_comm(core, n_in, n_out, comm, n):
    nc = 0 if comm is None else len(comm[1])

    def body(*refs):
        cin, xin = refs[:n_in], refs[n_in:n_in + nc]
        a = n_in + nc
        cout, xout = refs[a:a + n_out], refs[a + n_out:a + n_out + nc]
        rest = refs[a + n_out + nc:]
        if nc == 0:
            core(*cin, *cout, *rest)
            return
        cscr, sems = rest[:-3], rest[-3:]
        i, j = pl.program_id(0), pl.program_id(1)
        phase = _COMM_PHASES[comm[0]]

        @pl.when((i == 0) & (j == 0))
        def _():
            phase(0, xin, xout, sems)

        core(*cin, *cout, *cscr)

        @pl.when((i == max(n - 2, 0)) & (j == 0))
        def _():
            phase(1, xin, xout, sems)

        @pl.when((i == n - 1) & (j == n - 1))
        def _():
            phase(2, xin, xout, sems)

    return body


def _comm_specs(comm):
    if comm is None:
        return [], [], [], []
    hbm = pl.BlockSpec(memory_space=pl.ANY)
    nc = len(comm[1])
    return [hbm] * nc, [hbm] * nc, _COMM_OUT[comm[0]](comm[1]), _comm_sems(comm[0], nc)


def _flash_fwd(q, k, v, cc, cr, z, comm=None):
    S = q.shape[0]
    t = _tile(S, FLASH_TILE)
    n = S // t

    rch = _tile(t, ROW_CHUNK)

    def core(q_ref, k_ref, v_ref, cc_ref, cr_ref, zb_ref, o_ref, lse_ref, y_ref, m_sc, l_sc, acc):
        i, j = pl.program_id(0), pl.program_id(1)

        @pl.when(j == 0)
        def _():
            m_sc[...] = jnp.full_like(m_sc, MASK_VALUE)
            l_sc[...] = jnp.zeros_like(l_sc)
            acc[...] = jnp.zeros_like(acc)

        def block(masked):
            for h in range(NH):
                for rc in range(t // rch):
                    rows = slice(rc * rch, (rc + 1) * rch)
                    s = (_dot_nt(q_ref[rows, _hs(h)], k_ref[:, _hs(h)])
                         + (cc_ref[rows, h:h + 1] - cr_ref[h:h + 1, :]))
                    if masked:
                        qpos = rc * rch + lax.broadcasted_iota(jnp.int32, (rch, t), 0)
                        s = jnp.where(qpos >= lax.broadcasted_iota(jnp.int32, (rch, t), 1), s, MASK_VALUE)
                    m_old = m_sc[h, rows, :]
                    m_new = jnp.maximum(m_old, jnp.max(s, axis=-1, keepdims=True))
                    p = jnp.exp(s - m_new)
                    alpha = jnp.exp(m_old - m_new)
                    l_sc[h, rows, :] = alpha * l_sc[h, rows, :] + jnp.sum(p, axis=-1, keepdims=True)
                    acc[rows, _hs(h)] = alpha * acc[rows, _hs(h)] + _dot(p, v_ref[:, _hs(h)])
                    m_sc[h, rows, :] = m_new

        @pl.when(j < i)
        def _():
            block(False)

        @pl.when(j == i)
        def _():
            block(True)
            lse_ref[...] = jnp.zeros_like(lse_ref)
            for h in range(NH):
                o_ref[:, _hs(h)] = acc[:, _hs(h)] / l_sc[h]
                lse_ref[:, h:h + 1] = m_sc[h] + jnp.log(l_sc[h])
            y_ref[...] = _mx(o_ref[...] * _silu(zb_ref[:, 3 * BW:4 * BW]))

    qspec = lambda w: pl.BlockSpec((t, w), lambda i, j: (i, 0))
    kspec = lambda w: pl.BlockSpec((t, w), lambda i, j: (jnp.minimum(j, i), 0))
    xin, xout, xshape, xsem = _comm_specs(comm)
    res = _call(
        _fuse_comm(core, 6, 3, comm, n), name="flash_fwd", grid=(n, n),
        in_specs=[qspec(BW), kspec(BW), kspec(BW), qspec(128), pl.BlockSpec((8, t), lambda i, j: (0, jnp.minimum(j, i))),
                  pl.BlockSpec((t, 1024), lambda i, j: (i, ZB // 1024))] + xin,
        out_specs=[qspec(BW), qspec(128), qspec(BW)] + xout,
        out_shape=[jax.ShapeDtypeStruct((S, BW), F32), jax.ShapeDtypeStruct((S, 128), F32),
                   jax.ShapeDtypeStruct((S, BW), MXU_DTYPE)] + xshape,
        scratch=[pltpu.VMEM((NH, t, 1), F32), pltpu.VMEM((NH, t, 1), F32), pltpu.VMEM((t, BW), F32)] + xsem,
        sem=("arbitrary", "arbitrary"))(q, k, v, cc, cr, z, *(comm[1] if comm else []))
    return res[0], res[1], res[2], list(res[3:])


def _attn_gate_bwd(dy, o, z):
    S = dy.shape[0]
    ts = _tile(S, ROW_TILE)

    def body(dy_ref, o_ref, zb_ref, do_ref, dg_ref):
        g = zb_ref[:, 3 * BW:4 * BW]
        dy = dy_ref[...]
        do_ref[...] = dy * _silu(g)
        dg_ref[...] = _mx(dy * o_ref[...] * _dsilu(g))

    row = pl.BlockSpec((ts, BW), lambda i: (i, 0))
    return _call(
        body, name="attn_gate_bwd", grid=(S // ts,),
        in_specs=[pl.BlockSpec((ts, BW), lambda i: (i, 1)), row, pl.BlockSpec((ts, 1024), lambda i: (i, ZB // 1024))],
        out_specs=[row, row],
        out_shape=[jax.ShapeDtypeStruct((S, BW), F32), jax.ShapeDtypeStruct((S, BW), MXU_DTYPE)],
        sem=("parallel",))(dy, o, z)


def _flash_probs(qv, kv, vv, cc, cr, lse, do, o, h, i, j, t, masked, drow=None):
    s, mask = _attn_scores(qv, kv, cc, cr, h, i, j, t, masked)
    p = jnp.exp(s - lse[:, h:h + 1])
    if masked:
        p = jnp.where(mask, p, 0.0)
    doh = do[:, _hs(h)]
    dp = _dot_nt(doh, vv[:, _hs(h)])
    delta = jnp.sum(doh * o[:, _hs(h)], axis=-1, keepdims=True)
    if drow is not None:
        delta = delta + drow[:, h:h + 1]
    return p, p * (dp - delta), doh


def _flash_bwd_dq(q, k, v, cc, cr, lse, o, do, comm=None):
    S = q.shape[0]
    t = _tile(S, FLASH_TILE)
    n = S // t

    def core(q_ref, k_ref, v_ref, cc_ref, cr_ref, lse_ref, o_ref, do_ref, dq_ref, dr_ref):
        i, j = pl.program_id(0), pl.program_id(1)

        @pl.when(j == 0)
        def _():
            dq_ref[...] = jnp.zeros_like(dq_ref)
            dr_ref[...] = jnp.zeros_like(dr_ref)

        def block(masked):
            qv, kv, vv = q_ref[...], k_ref[...], v_ref[...]
            for h in range(NH):
                p, ds, doh = _flash_probs(qv, kv, vv, cc_ref[...], cr_ref[...], lse_ref[...], do_ref[...], o_ref[...],
                                          h, i, j, t, masked)
                dq_ref[:, _hs(h)] += _dot(ds, kv[:, _hs(h)])
                dr_ref[:, h:h + 1] += jnp.sum(ds, axis=-1, keepdims=True)

        @pl.when(j < i)
        def _():
            block(False)

        @pl.when(j == i)
        def _():
            block(True)

    qspec = lambda w: pl.BlockSpec((t, w), lambda i, j: (i, 0))
    kspec = lambda w: pl.BlockSpec((t, w), lambda i, j: (jnp.minimum(j, i), 0))
    xin, xout, xshape, xsem = _comm_specs(comm)
    res = _call(
        _fuse_comm(core, 8, 2, comm, n), name="flash_bwd_dq", grid=(n, n),
        in_specs=[qspec(BW), kspec(BW), kspec(BW), qspec(128), pl.BlockSpec((8, t), lambda i, j: (0, jnp.minimum(j, i))),
                  qspec(128), qspec(BW), qspec(BW)] + xin,
        out_specs=[qspec(BW), qspec(128)] + xout,
        out_shape=[jax.ShapeDtypeStruct((S, BW), F32), jax.ShapeDtypeStruct((S, 128), F32)] + xshape,
        scratch=xsem,
        sem=("arbitrary", "arbitrary"))(q, k, v, cc, cr, lse, o, do, *(comm[1] if comm else []))
    return res[0], res[1], list(res[2:])


def _flash_bwd_dkv(q, k, v, cc, cr, lse, o, do, drow):
    S = q.shape[0]
    t = _tile(S, FLASH_TILE)
    n = S // t

    def body(q_ref, k_ref, v_ref, cc_ref, cr_ref, lse_ref, o_ref, do_ref, dr_ref, dk_ref, dv_ref, dc_ref):
        j, i = pl.program_id(0), pl.program_id(1)

        @pl.when(i == 0)
        def _():
            dk_ref[...] = jnp.zeros_like(dk_ref)
            dv_ref[...] = jnp.zeros_like(dv_ref)
            dc_ref[...] = jnp.zeros_like(dc_ref)

        def block(masked):
            qv, kv, vv = q_ref[...], k_ref[...], v_ref[...]
            for h in range(NH):
                p, ds, doh = _flash_probs(qv, kv, vv, cc_ref[...], cr_ref[...], lse_ref[...], do_ref[...], o_ref[...],
                                          h, i, j, t, masked, drow=dr_ref[...])
                dv_ref[:, _hs(h)] += _dot_tn(p, doh)
                dk_ref[:, _hs(h)] += _dot_tn(ds, qv[:, _hs(h)])
                dc_ref[h:h + 1, :] += -jnp.sum(ds, axis=0, keepdims=True)

        @pl.when(i > j)
        def _():
            block(False)

        @pl.when(i == j)
        def _():
            block(True)

    qspec = lambda w: pl.BlockSpec((t, w), lambda j, i: (jnp.maximum(i, j), 0))
    kspec = lambda w: pl.BlockSpec((t, w), lambda j, i: (j, 0))
    return _call(
        body, name="flash_bwd_dkv", grid=(n, n),
        in_specs=[qspec(BW), kspec(BW), kspec(BW), qspec(128), pl.BlockSpec((8, t), lambda j, i: (0, j)),
                  qspec(128), qspec(BW), qspec(BW), qspec(128)],
        out_specs=[kspec(BW), kspec(BW), pl.BlockSpec((8, t), lambda j, i: (0, j))],
        out_shape=[jax.ShapeDtypeStruct((S, BW), F32), jax.ShapeDtypeStruct((S, BW), F32),
                   jax.ShapeDtypeStruct((8, S), F32)],
        sem=("parallel", "arbitrary"))(q, k, v, cc, cr, lse, o, do, drow)


def _attn_prep_bwd(z, fb, gq, gk, dq, dk, dv, dgate, dcc, dz):
    S = z.shape[0]
    ts = _tile(S, ROW_TILE)
    nt = S // ts

    def body(zb_ref, zf_ref, fb_ref, gq_ref, gk_ref, dq_ref, dk_ref, dv_ref, dg_ref, dcc_ref, dz_in, dzb_ref, dzf_ref,
             st_ref, carry):
        @pl.when(pl.program_id(0) == 0)
        def _():
            carry[...] = jnp.zeros_like(carry)
            st_ref[...] = jnp.zeros_like(st_ref)

        zb = zb_ref[...]
        gqv, gkv = gq_ref[...], gk_ref[...]
        dqv, dkv = dq_ref[...], dk_ref[...]
        sq = jnp.zeros((1, DH), F32)
        sk = jnp.zeros((1, DH), F32)
        for h in range(NH):
            dx, dgr = _rms_bwd(zb[:, _hs(h)], gqv, dqv[:, _hs(h)] * (DH ** -0.5))
            dzb_ref[:, _hs(h)] = _mx(dx)
            sq = sq + jnp.sum(dgr, axis=0, keepdims=True)
            ks = slice(BW + h * DH, BW + (h + 1) * DH)
            dx, dgr = _rms_bwd(zb[:, ks], gkv, dkv[:, _hs(h)])
            dzb_ref[:, ks] = _mx(dx)
            sk = sk + jnp.sum(dgr, axis=0, keepdims=True)
        dzb_ref[:, 2 * BW:3 * BW] = _mx(dv_ref[...])
        dzb_ref[:, 3 * BW:4 * BW] = dg_ref[...]
        dc = dcc_ref[...]
        dlf = _dot_exact_lhs01(_tri(ts, upper=True), dc) + carry[...]
        carry[...] = dlf[0:1, :]
        dfz = dlf * _sigmoid(-(zf_ref[...] + fb_ref[...]))
        dzf_ref[...] = _mx(dfz)
        st_ref[0:1, 0:DH] += sq
        st_ref[1:2, 0:DH] += sk
        st_ref[2:3, :] += jnp.sum(dfz, axis=0, keepdims=True)

    rev = lambda w, c=0: pl.BlockSpec((ts, w), lambda i: (nt - 1 - i, c))
    one = lambda w: pl.BlockSpec((1, w), lambda i: (0, 0))
    return _call(
        body, name="attn_prep_bwd", grid=(nt,),
        in_specs=[rev(1024, ZB // 1024), rev(128, ZF // 128), one(128), one(DH), one(DH),
                  rev(BW), rev(BW), rev(BW), rev(BW), rev(128), _ANY],
        out_specs=[rev(1024, ZB // 1024), rev(128), pl.BlockSpec((8, 128), lambda i: (0, 0))],
        out_shape=[jax.ShapeDtypeStruct((S, ZCOLS), MXU_DTYPE), jax.ShapeDtypeStruct((S, 128), MXU_DTYPE),
                   jax.ShapeDtypeStruct((8, 128), F32)],
        scratch=[pltpu.VMEM((1, 128), F32)], sem=("arbitrary",), aliases={10: 0})(z, z, fb, gq, gk, dq, dk, dv, dgate, dcc, dz)


def _put_fgate_cols(dzf, dz):
    S = dzf.shape[0]
    ts = _tile(S, ROW_TILE)

    def body(f_ref, dz_in, o_ref):
        o_ref[...] = f_ref[...]

    return _call(
        body, name="put_fgate_cols", grid=(S // ts,),
        in_specs=[pl.BlockSpec((ts, 128), lambda i: (i, 0)), _ANY],
        out_specs=pl.BlockSpec((ts, 128), lambda i: (i, ZF // 128)),
        out_shape=jax.ShapeDtypeStruct((S, ZCOLS), MXU_DTYPE), sem=("parallel",), aliases={1: 0})(dzf, dz)


def _hgrn_consts():
    C = CHUNK
    t = np.arange(C)[:, None]
    j = np.arange(C)[None, :]
    mats = [(j <= t)]
    tq, tk, masks = [], [], []
    for lev in range(NLEV):
        m = C >> (lev + 1)
        blk, pos = t // (2 * m), t % (2 * m)
        mid = blk * 2 * m + m - 1
        tq.append((pos >= m) & (j > mid) & (j <= t))
        tk.append((pos < m) & (j > t) & (j <= mid))
        sblk, spos = j // (2 * m), j % (2 * m)
        masks.append((blk == sblk) & (pos >= m) & (spos < m))
    masks.append(t == j)
    tmat = np.concatenate(mats + tq + tk, axis=0).astype(np.float32)
    return tmat, np.stack(masks).astype(np.float32)


def _hgrn_chunk_fwd(zc, lb, tmat):
    cq, cf, ci = zc[:, 0:BW], zc[:, BW:2 * BW], zc[:, 2 * BW:3 * BW]
    q = _silu(cq)
    sg = _sigmoid(cf)
    g = lb + (1.0 - lb) * sg
    lg = jnp.log(g)
    kf = (1.0 - lb) * _sigmoid(-cf)
    e = _dot_exact_lhs01(tmat, lg)
    b = e[0:CHUNK]
    blast = b[CHUNK - 1:CHUNK, :]
    return cq, cf, q, sg, g, kf, ci, e, b, blast


def _round_mx(x):
    if MXU_DTYPE != jnp.bfloat16:
        return x
    u = lax.bitcast_convert_type(x, jnp.uint32)
    u = (u + jnp.uint32(0x7FFF) + ((u >> 16) & jnp.uint32(1))) & jnp.uint32(0xFFFF0000)
    return lax.bitcast_convert_type(u, F32)


def _hgrn_scores(q, kf, e, masks, h):
    qh, kh = q[:, _hs(h)], kf[:, _hs(h)]
    ql, kl = [], []
    a = None
    for lev in range(NLEV + 1):
        if lev < NLEV:
            eq = jnp.exp(e[(1 + lev) * CHUNK:(2 + lev) * CHUNK, _hs(h)])
            ek = jnp.exp(e[(1 + NLEV + lev) * CHUNK:(2 + NLEV + lev) * CHUNK, _hs(h)])
            ql.append((_round_mx(qh * eq), eq))
            kl.append((_round_mx(kh * ek), ek))
        else:
            ql.append((_round_mx(qh), None))
            kl.append((_round_mx(kh), None))
        term = masks[lev] * _dot_nt(ql[-1][0], kl[-1][0])
        a = term if a is None else a + term
    return a, ql, kl


def _hgrn_fwd(z, lb, gain, tmat, masks):
    S = z.shape[0]
    ts = _tile(S, ROW_TILE)
    nc = ts // CHUNK

    def body(zc_ref, lb_ref, gn_ref, tm_ref, mk_ref, o_ref, st_ref, y_ref, state):
        @pl.when(pl.program_id(0) == 0)
        def _():
            state[...] = jnp.zeros_like(state)

        lbv, gn, tm, mk = lb_ref[...], gn_ref[...], tm_ref[...], mk_ref[...]

        def chunk(c, carry):
            r0 = pl.multiple_of(c * CHUNK, CHUNK)
            zc = zc_ref[pl.ds(r0, CHUNK), :]
            cq, cf, q, sg, g, kf, v, e, b, blast = _hgrn_chunk_fwd(zc, lbv, tm)
            qe = q * jnp.exp(b)
            kd = kf * jnp.exp(blast - b)
            st_ref[pl.ds(r0, CHUNK), :] = state[...]
            for h in range(NH):
                sth = state[:, _hs(h)]
                a, _, _ = _hgrn_scores(q, kf, e, mk, h)
                oh = _dot_nt(qe[:, _hs(h)], sth) + _dot(a, v[:, _hs(h)])
                state[:, _hs(h)] = sth * jnp.exp(blast[:, _hs(h)]) + _dot_tn(v[:, _hs(h)], kd[:, _hs(h)])
                o_ref[pl.ds(r0, CHUNK), _hs(h)] = oh
                yn = _rms_fwd(oh, gn[:, _hs(h)])
                y_ref[pl.ds(r0, CHUNK), _hs(h)] = _mx(yn * _silu(zc[:, 3 * BW + h * DH:3 * BW + (h + 1) * DH]))
            return carry

        lax.fori_loop(0, nc, chunk, 0)

    row = pl.BlockSpec((ts, BW), lambda i: (i, 0))
    one = pl.BlockSpec((1, BW), lambda i: (0, 0))
    return _call(
        body, name="hgrn_fwd", grid=(S // ts,),
        in_specs=[pl.BlockSpec((ts, 1024), lambda i: (i, ZC // 1024)), one, one,
                  pl.BlockSpec(tmat.shape, lambda i: (0, 0)), pl.BlockSpec(masks.shape, lambda i: (0, 0, 0))],
        out_specs=[row, row, row],
        out_shape=[jax.ShapeDtypeStruct((S, BW), F32), jax.ShapeDtypeStruct((S, BW), F32),
                   jax.ShapeDtypeStruct((S, BW), MXU_DTYPE)],
        scratch=[pltpu.VMEM((CHUNK, BW), F32)], sem=("arbitrary",))(z, lb, gain, tmat, masks)


def _hgrn_bwd(z, lb, gain, tmat, masks, o_pre, states, dy, dz):
    S = z.shape[0]
    ts = _tile(S, ROW_TILE)
    nt = S // ts
    nc = ts // CHUNK

    def body(zc_ref, lb_ref, gn_ref, tm_ref, mk_ref, o_ref, st_ref, dy_ref, dz_in, dz_ref, stat_ref, dstate):
        @pl.when(pl.program_id(0) == 0)
        def _():
            dstate[...] = jnp.zeros_like(dstate)
            stat_ref[...] = jnp.zeros_like(stat_ref)

        lbv, gn, tm, mk = lb_ref[...], gn_ref[...], tm_ref[...], mk_ref[...]
        upper = _tri(CHUNK, upper=True)
        lower_strict = 1.0 - upper

        def chunk(cc, carry):
            c = nc - 1 - cc
            r0 = pl.multiple_of(c * CHUNK, CHUNK)
            zc = zc_ref[pl.ds(r0, CHUNK), :]
            cq, cf, q, sg, g, kf, v, e, b, blast = _hgrn_chunk_fwd(zc, lbv, tm)
            eb = jnp.exp(b)
            ebl = jnp.exp(blast - b)
            qe = q * eb
            kd = kf * ebl
            o = o_ref[pl.ds(r0, CHUNK), :]
            dyv = dy_ref[pl.ds(r0, CHUNK), :]
            stp = st_ref[pl.ds(r0, CHUNK), :]
            cg = zc[:, 3 * BW:4 * BW]
            sgate = _silu(cg)
            dq_parts, dk_parts, dv_parts, dcg_parts = [], [], [], []
            dgain, up_parts, lo_parts, const_parts = [], [], [], []
            for h in range(NH):
                hs = _hs(h)
                oh = o[:, hs]
                r = lax.rsqrt(jnp.mean(oh * oh, axis=-1, keepdims=True) + EPS)
                nrm = oh * r
                dyn = dyv[:, hs] * sgate[:, hs]
                dcg_parts.append(dyv[:, hs] * nrm * gn[:, hs] * _dsilu(cg[:, hs]))
                dgain.append(jnp.sum(dyn * nrm, axis=0, keepdims=True))
                tt = dyn * gn[:, hs]
                doh = r * (tt - nrm * jnp.mean(tt * nrm, axis=-1, keepdims=True))
                a, ql, kl = _hgrn_scores(q, kf, e, mk, h)
                da = _dot_nt(doh, v[:, hs])
                dsth = dstate[:, hs]
                ebh = jnp.exp(blast[:, hs])
                dv_parts.append(_dot_tn(a, doh) + _dot_nt(kd[:, hs], dsth))
                dq_inter = eb[:, hs] * _dot(doh, stp[:, hs])
                dk_state = ebl[:, hs] * _dot(v[:, hs], dsth)
                dqh, dkh, gh = dq_inter, dk_state, None
                for lev in range(NLEV + 1):
                    dal = mk[lev] * da
                    xq = _dot(dal, kl[lev][0])
                    yk = _dot_tn(dal, ql[lev][0])
                    gterm = ql[lev][0] * xq - kl[lev][0] * yk
                    gh = gterm if gh is None else gh + gterm
                    dqh = dqh + (xq if lev == NLEV else ql[lev][1] * xq)
                    dkh = dkh + (yk if lev == NLEV else kl[lev][1] * yk)
                up_parts.append(gh + q[:, hs] * dq_inter)
                lo_parts.append(kf[:, hs] * dk_state)
                const_parts.append(jnp.sum(dsth * stp[:, hs], axis=0, keepdims=True) * ebh)
                dstate[:, hs] = dsth * ebh + _dot_tn(doh, qe[:, hs])
                dq_parts.append(dqh)
                dk_parts.append(dkh)
            dq = jnp.concatenate(dq_parts, axis=1)
            dk = jnp.concatenate(dk_parts, axis=1)
            dlg = (_dot_exact_lhs01(upper, jnp.concatenate(up_parts, axis=1))
                   + _dot_exact_lhs01(lower_strict, jnp.concatenate(lo_parts, axis=1))
                   + jnp.concatenate(const_parts, axis=1))
            dsg = sg * (1.0 - sg)
            dz_ref[pl.ds(r0, CHUNK), 0:BW] = _mx(dq * _dsilu(cq))
            dz_ref[pl.ds(r0, CHUNK), BW:2 * BW] = _mx((dlg / g - dk) * (1.0 - lbv) * dsg)
            dz_ref[pl.ds(r0, CHUNK), 2 * BW:3 * BW] = _mx(jnp.concatenate(dv_parts, axis=1))
            dz_ref[pl.ds(r0, CHUNK), 3 * BW:4 * BW] = _mx(jnp.concatenate(dcg_parts, axis=1))
            stat_ref[0:1, :] += jnp.concatenate(dgain, axis=1)
            stat_ref[1:2, :] += jnp.sum((dlg / g - dk) * (1.0 - sg), axis=0, keepdims=True)
            return carry

        lax.fori_loop(0, nc, chunk, 0)

    rev = lambda w, c=0: pl.BlockSpec((ts, w), lambda i: (nt - 1 - i, c))
    one = pl.BlockSpec((1, BW), lambda i: (0, 0))
    return _call(
        body, name="hgrn_bwd", grid=(nt,),
        in_specs=[rev(1024, ZC // 1024), one, one, pl.BlockSpec(tmat.shape, lambda i: (0, 0)),
                  pl.BlockSpec(masks.shape, lambda i: (0, 0, 0)), rev(BW), rev(BW), rev(BW, 2), _ANY],
        out_specs=[rev(1024, ZC // 1024), pl.BlockSpec((8, BW), lambda i: (0, 0))],
        out_shape=[jax.ShapeDtypeStruct((S, ZCOLS), MXU_DTYPE), jax.ShapeDtypeStruct((8, BW), F32)],
        scratch=[pltpu.VMEM((CHUNK, BW), F32)],
        sem=("arbitrary",), aliases={8: 0})(z, lb, gain, tmat, masks, o_pre, states, dy, dz)


def _lower_bounds_fwd(lb_logits):
    def body(l_ref, o_ref):
        l = l_ref[...]
        m = jnp.max(l, axis=0, keepdims=True)
        ex = jnp.exp(l - m)
        p = ex / jnp.sum(ex, axis=0, keepdims=True)
        cs = p[0:1, :]
        o_ref[0:1, :] = jnp.clip(cs - p[0:1, :], 0.0, 1.0)
        for d in range(1, DEPTH):
            cs = cs + p[d:d + 1, :]
            o_ref[d:d + 1, :] = jnp.clip(cs - p[0:1, :], 0.0, 1.0)

    full = pl.BlockSpec((DEPTH, BW), lambda: (0, 0))
    return _call(body, name="lower_bounds_fwd", grid=(), in_specs=[full], out_specs=full,
                 out_shape=jax.ShapeDtypeStruct((DEPTH, BW), F32))(lb_logits)


def _lower_bounds_bwd(lb_logits, dlow):
    def body(l_ref, d_ref, o_ref):
        l = l_ref[...]
        m = jnp.max(l, axis=0, keepdims=True)
        ex = jnp.exp(l - m)
        p = ex / jnp.sum(ex, axis=0, keepdims=True)
        dl = d_ref[...]
        cs = p[0:1, :]
        dcs = []
        for d in range(DEPTH):
            if d > 0:
                cs = cs + p[d:d + 1, :]
            val = cs - p[0:1, :]
            dcs.append(jnp.where((val > 0.0) & (val < 1.0), dl[d:d + 1, :], 0.0))
        total = dcs[0] + dcs[1] + dcs[2] + dcs[3]
        dp = []
        for j in range(DEPTH):
            s = dcs[j]
            for d in range(j + 1, DEPTH):
                s = s + dcs[d]
            dp.append(s - total if j == 0 else s)
        inner = p[0:1, :] * dp[0]
        for j in range(1, DEPTH):
            inner = inner + p[j:j + 1, :] * dp[j]
        for j in range(DEPTH):
            o_ref[j:j + 1, :] = p[j:j + 1, :] * (dp[j] - inner)

    full = pl.BlockSpec((DEPTH, BW), lambda: (0, 0))
    return _call(body, name="lower_bounds_bwd", grid=(), in_specs=[full, full], out_specs=full,
                 out_shape=jax.ShapeDtypeStruct((DEPTH, BW), F32))(lb_logits, dlow)


def _sgu_fwd(z, gv, ws, bs):
    S = z.shape[0]
    ts = _tile(S, ROW_TILE)
    nc = ts // SCHUNK

    def body(zd_ref, gv_ref, ws_ref, bs_ref, y_ref):
        gvv, bsv = gv_ref[...], bs_ref[...]
        tril = _tri(SCHUNK)
        for c in range(nc):
            rs = slice(c * SCHUNK, (c + 1) * SCHUNK)
            zd = zd_ref[rs, :]
            for h in range(NH):
                vn = _rms_fwd(zd[:, BW + h * DH:BW + (h + 1) * DH], gvv[:, _hs(h)])
                s = _dot(ws_ref[h] * tril, vn) + bsv[:, h:h + 1]
                y_ref[rs, _hs(h)] = _mx(zd[:, _hs(h)] * s * _silu(zd[:, 2 * BW + h * DH:2 * BW + (h + 1) * DH]))

    return _call(
        body, name="sgu_fwd", grid=(S // ts,),
        in_specs=[pl.BlockSpec((ts, 1024), lambda i: (i, ZD // 1024)), pl.BlockSpec((1, BW), lambda i: (0, 0)),
                  pl.BlockSpec((NH, SCHUNK, SCHUNK), lambda i: (0, 0, 0)), pl.BlockSpec((SCHUNK, 128), lambda i: (0, 0))],
        out_specs=pl.BlockSpec((ts, BW), lambda i: (i, 0)),
        out_shape=jax.ShapeDtypeStruct((S, BW), MXU_DTYPE), sem=("parallel",))(z, gv, ws, bs)


def _sgu_bwd(z, gv, ws, bs, dy, dz):
    S = z.shape[0]
    ts = _tile(S, ROW_TILE)
    nc = ts // SCHUNK

    def body(zd_ref, gv_ref, ws_ref, bs_ref, dy_ref, dz_in, dz_ref, dws_ref, dbs_ref, st_ref):
        @pl.when(pl.program_id(0) == 0)
        def _():
            dws_ref[...] = jnp.zeros_like(dws_ref)
            dbs_ref[...] = jnp.zeros_like(dbs_ref)
            st_ref[...] = jnp.zeros_like(st_ref)

        gvv, bsv = gv_ref[...], bs_ref[...]
        tril = _tri(SCHUNK)
        dz_ref[:, 3 * BW:4 * BW] = jnp.zeros((ts, BW), MXU_DTYPE)
        for c in range(nc):
            rs = slice(c * SCHUNK, (c + 1) * SCHUNK)
            zd = zd_ref[rs, :]
            dyv = dy_ref[rs, :]
            for h in range(NH):
                hs = _hs(h)
                u = zd[:, hs]
                vraw = zd[:, BW + h * DH:BW + (h + 1) * DH]
                gt = zd[:, 2 * BW + h * DH:2 * BW + (h + 1) * DH]
                gvh = gvv[:, hs]
                vn = _rms_fwd(vraw, gvh)
                wm = ws_ref[h] * tril
                s = _dot(wm, vn) + bsv[:, h:h + 1]
                sil = _silu(gt)
                d = dyv[:, hs]
                ds = d * u * sil
                dz_ref[rs, hs] = _mx(d * s * sil)
                dz_ref[rs, 2 * BW + h * DH:2 * BW + (h + 1) * DH] = _mx(d * u * s * _dsilu(gt))
                dws_ref[h] += tril * _dot_nt(ds, vn)
                dbs_ref[:, h:h + 1] += jnp.sum(ds, axis=-1, keepdims=True)
                dvn = _dot_tn(wm, ds)
                dx, dgr = _rms_bwd(vraw, gvh, dvn)
                dz_ref[rs, BW + h * DH:BW + (h + 1) * DH] = _mx(dx)
                st_ref[0:1, hs] += jnp.sum(dgr, axis=0, keepdims=True)

    return _call(
        body, name="sgu_bwd", grid=(S // ts,),
        in_specs=[pl.BlockSpec((ts, 1024), lambda i: (i, ZD // 1024)), pl.BlockSpec((1, BW), lambda i: (0, 0)),
                  pl.BlockSpec((NH, SCHUNK, SCHUNK), lambda i: (0, 0, 0)), pl.BlockSpec((SCHUNK, 128), lambda i: (0, 0)),
                  pl.BlockSpec((ts, BW), lambda i: (i, 3)), _ANY],
        out_specs=[pl.BlockSpec((ts, 1024), lambda i: (i, ZD // 1024)), pl.BlockSpec((NH, SCHUNK, SCHUNK), lambda i: (0, 0, 0)),
                   pl.BlockSpec((SCHUNK, 128), lambda i: (0, 0)), pl.BlockSpec((8, BW), lambda i: (0, 0))],
        out_shape=[jax.ShapeDtypeStruct((S, ZCOLS), MXU_DTYPE), jax.ShapeDtypeStruct((NH, SCHUNK, SCHUNK), F32),
                   jax.ShapeDtypeStruct((SCHUNK, 128), F32), jax.ShapeDtypeStruct((8, BW), F32)],
        sem=("arbitrary",), aliases={5: 0})(z, gv, ws, bs, dy, dz)


def _merge_fwd(x, z, ys, wup, mb, wo):
    S = x.shape[0]
    ts = _tile(S, MERGE_TILE)

    def body(x_ref, zm_ref, ya_ref, yb_ref, yc_ref, yd_ref, wup_ref, mb_ref, wo_ref, x1_ref, mg_ref):
        yrefs = (ya_ref, yb_ref, yc_ref, yd_ref)
        mbv = mb_ref[...]
        merged = None
        for b in range(NBR):
            cs = slice(b * D_MODEL, (b + 1) * D_MODEL)
            term = _sigmoid(zm_ref[:, cs] + mbv[b:b + 1, :]) * jnp.dot(yrefs[b][...], wup_ref[b],
                                                                      preferred_element_type=F32)
            merged = term if merged is None else merged + term
        mg = _mx(merged)
        mg_ref[...] = mg
        x1_ref[...] = x_ref[...] + jnp.dot(mg, wo_ref[...], preferred_element_type=F32)

    row = lambda w: pl.BlockSpec((ts, w), lambda i: (i, 0))
    return _call(
        body, name="merge_fwd", grid=(S // ts,),
        in_specs=[row(D_MODEL), pl.BlockSpec((ts, 4096), lambda i: (i, 0)), row(BW), row(BW), row(BW), row(BW),
                  pl.BlockSpec((NBR, BW, D_MODEL), lambda i: (0, 0, 0)), pl.BlockSpec((NBR, D_MODEL), lambda i: (0, 0)),
                  pl.BlockSpec((D_MODEL, D_MODEL), lambda i: (0, 0))],
        out_specs=[row(D_MODEL), row(D_MODEL)],
        out_shape=[jax.ShapeDtypeStruct((S, D_MODEL), F32), jax.ShapeDtypeStruct((S, D_MODEL), MXU_DTYPE)],
        sem=("parallel",))(x, z, *ys, wup, mb, wo)


def _merge_bwd(dx1, z, ys, wup, mb, wo):
    S = dx1.shape[0]
    ts = _tile(S, MERGE_TILE)

    def body(dx_ref, zm_ref, ya_ref, yb_ref, yc_ref, yd_ref, wup_ref, mb_ref, wo_ref,
             dzm_ref, du_ref, dxb_ref, dy_ref, st_ref):
        @pl.when(pl.program_id(0) == 0)
        def _():
            st_ref[...] = jnp.zeros_like(st_ref)

        yrefs = (ya_ref, yb_ref, yc_ref, yd_ref)
        mbv = mb_ref[...]
        dxb = _mx(dx_ref[...])
        dxb_ref[...] = dxb
        dmerged = _dot_nt(dxb, wo_ref[...])
        for b in range(NBR):
            cs = slice(b * D_MODEL, (b + 1) * D_MODEL)
            u = jnp.dot(yrefs[b][...], wup_ref[b], preferred_element_type=F32)
            sg = _sigmoid(zm_ref[:, cs] + mbv[b:b + 1, :])
            du = _mx(dmerged * sg)
            du_ref[:, cs] = du
            dzm = dmerged * u * sg * (1.0 - sg)
            dzm_ref[:, cs] = _mx(dzm)
            st_ref[b:b + 1, :] += jnp.sum(dzm, axis=0, keepdims=True)
            dy_ref[:, b * BW:(b + 1) * BW] = _dot_nt(du, wup_ref[b])

    row = lambda w: pl.BlockSpec((ts, w), lambda i: (i, 0))
    return _call(
        body, name="merge_bwd", grid=(S // ts,),
        in_specs=[row(D_MODEL), pl.BlockSpec((ts, 4096), lambda i: (i, 0)), row(BW), row(BW), row(BW), row(BW),
                  pl.BlockSpec((NBR, BW, D_MODEL), lambda i: (0, 0, 0)),
                  pl.BlockSpec((NBR, D_MODEL), lambda i: (0, 0)), pl.BlockSpec((D_MODEL, D_MODEL), lambda i: (0, 0))],
        out_specs=[row(4096), row(4096), row(D_MODEL), row(D_MODEL), pl.BlockSpec((8, D_MODEL), lambda i: (0, 0))],
        out_shape=[jax.ShapeDtypeStruct((S, ZCOLS), MXU_DTYPE), jax.ShapeDtypeStruct((S, 4096), MXU_DTYPE),
                   jax.ShapeDtypeStruct((S, D_MODEL), MXU_DTYPE), jax.ShapeDtypeStruct((S, D_MODEL), F32),
                   jax.ShapeDtypeStruct((8, D_MODEL), F32)],
        sem=("arbitrary",))(dx1, z, *ys, wup, mb, wo)


def _ple_fwd(x1, p, g, wg, wp):
    S = x1.shape[0]
    ts = _tile(S, ROW_TILE)

    def body(x_ref, p_ref, g_ref, wg_ref, wp_ref, o_ref):
        x = x_ref[...]
        hp = _mx(_rms_fwd(x, g_ref[...]))
        gate = _sigmoid(jnp.dot(hp, wg_ref[...], preferred_element_type=F32))
        pp = jnp.dot(_mx(p_ref[...]), wp_ref[...], preferred_element_type=F32)
        o_ref[...] = x + gate * pp

    row = lambda w: pl.BlockSpec((ts, w), lambda i: (i, 0))
    return _call(
        body, name="ple_fwd", grid=(S // ts,),
        in_specs=[row(D_MODEL), row(PLE), pl.BlockSpec((1, D_MODEL), lambda i: (0, 0)),
                  pl.BlockSpec((D_MODEL, D_MODEL), lambda i: (0, 0)), pl.BlockSpec((PLE, D_MODEL), lambda i: (0, 0))],
        out_specs=row(D_MODEL), out_shape=jax.ShapeDtypeStruct((S, D_MODEL), F32),
        sem=("parallel",))(x1, p, g, wg, wp)


def _ple_bwd(x1, p, dx2, g, wg, wp):
    S = x1.shape[0]
    ts = _tile(S, ROW_TILE)

    def body(x_ref, p_ref, dx_ref, g_ref, wg_ref, wp_ref, dx1_ref, hp_ref, dgl_ref, dpp_ref, pb_ref, st_ref):
        @pl.when(pl.program_id(0) == 0)
        def _():
            st_ref[...] = jnp.zeros_like(st_ref)

        x, gv, dx2 = x_ref[...], g_ref[...], dx_ref[...]
        hp = _mx(_rms_fwd(x, gv))
        hp_ref[...] = hp
        gate = _sigmoid(jnp.dot(hp, wg_ref[...], preferred_element_type=F32))
        pb = _mx(p_ref[...])
        pb_ref[...] = pb
        pp = jnp.dot(pb, wp_ref[...], preferred_element_type=F32)
        dgl = _mx(dx2 * pp * gate * (1.0 - gate))
        dgl_ref[...] = dgl
        dpp_ref[...] = _mx(dx2 * gate)
        dhp = _dot_nt(dgl, wg_ref[...])
        dxn, dgr = _rms_bwd(x, gv, dhp)
        dx1_ref[...] = dx2 + dxn
        st_ref[0:1, :] += jnp.sum(dgr, axis=0, keepdims=True)

    row = lambda w: pl.BlockSpec((ts, w), lambda i: (i, 0))
    sq = pl.BlockSpec((D_MODEL, D_MODEL), lambda i: (0, 0))
    return _call(
        body, name="ple_bwd", grid=(S // ts,),
        in_specs=[row(D_MODEL), row(PLE), row(D_MODEL), pl.BlockSpec((1, D_MODEL), lambda i: (0, 0)), sq,
                  pl.BlockSpec((PLE, D_MODEL), lambda i: (0, 0))],
        out_specs=[row(D_MODEL), row(D_MODEL), row(D_MODEL), row(D_MODEL), row(PLE),
                   pl.BlockSpec((8, D_MODEL), lambda i: (0, 0))],
        out_shape=[jax.ShapeDtypeStruct((S, D_MODEL), F32)] + [jax.ShapeDtypeStruct((S, D_MODEL), MXU_DTYPE)] * 3
        + [jax.ShapeDtypeStruct((S, PLE), MXU_DTYPE), jax.ShapeDtypeStruct((8, D_MODEL), F32)],
        sem=("arbitrary",))(x1, p, dx2, g, wg, wp)


def _pad_rows(a, rows=8):
    return jnp.concatenate([a, jnp.zeros((rows - a.shape[0],) + a.shape[1:], a.dtype)], axis=0)


def _pad_lanes(a, lanes=128):
    return jnp.concatenate([a, jnp.zeros(a.shape[:-1] + (lanes - a.shape[-1],), a.dtype)], axis=-1)


def _wz_from_w_in(w):
    zeros = lambda n: jnp.zeros((w.shape[0], n), w.dtype)
    return jnp.concatenate([w[:, _OM:_OEND], w[:, _OA:_OB], w[:, _OB:_OF], w[:, _OC:_OD], w[:, _OD:_OM], zeros(256),
                            w[:, _OF:_OC], zeros(124)], axis=1)


def _w_in_from_wz(g):
    return jnp.concatenate([g[:, ZA:ZB], g[:, ZB:ZC], g[:, ZF:ZF + 4], g[:, ZC:ZD], g[:, ZD:ZD + 768], g[:, ZM:ZA]], axis=1)


_W_IN_GROUPS = [(_OA, _OB, ZA), (_OB, _OF, ZB), (_OF, _OC, ZF), (_OC, _OD, ZC), (_OD, _OM, ZD), (_OM, _OEND, ZM)]


def _wz_from_shards(g):
    n = g.shape[-1]
    pieces, pos = [], 0
    for a, b, zs in sorted(_W_IN_GROUPS, key=lambda grp: grp[2]):
        if zs > pos:
            pieces.append(jnp.zeros((g.shape[1], zs - pos), g.dtype))
        for k in range(4):
            lo, hi = max(a, k * n), min(b, (k + 1) * n)
            if lo < hi:
                pieces.append(g[k][:, lo - k * n:hi - k * n])
        pos = zs + (b - a)
    pieces.append(jnp.zeros((g.shape[1], ZCOLS - pos), g.dtype))
    return jnp.concatenate(pieces, axis=1)


def _w_in_slabs_from_wz(g):
    n = _OEND // 4
    slabs = []
    for k in range(4):
        pieces = []
        for a, b, zs in _W_IN_GROUPS:
            lo, hi = max(a, k * n), min(b, (k + 1) * n)
            if lo < hi:
                pieces.append(g[:, zs + lo - a:zs + hi - a])
        slabs.append(jnp.concatenate(pieces, axis=1))
    return jnp.stack(slabs)


def _local_step(x, p, target, wts, dist=None):
    tmat_np, masks_np = _hgrn_consts()
    tmat = jnp.asarray(tmat_np, MXU_DTYPE)
    masks = jnp.asarray(masks_np, F32)
    lower = _lower_bounds_fwd(wts["lb_logits"])
    saved = []
    for li in range(DEPTH):
        big = dist.weights(li) if dist else {n: wts[n][li] for n in _GATHERED}
        wz = _wz_from_shards(big["w_in"]) if big["w_in"].ndim == 3 else _wz_from_w_in(big["w_in"])
        g_mix = wts["norm_mix"][li][None, :]
        h = _norm_fwd(x, g_mix)
        z = _mm_nn(h, wz, F32, name="mm_z", tn=1664)
        cw = _pad_rows(jnp.concatenate([big["conv_w"], wts["conv_b"][li][None, :]], axis=0))
        ya = _conv_fwd(z, cw)
        fb = _pad_lanes(wts["fgate_bias"][li][None, :])
        gq, gk = wts["q_norm"][li][None, :], wts["k_norm"][li][None, :]
        qs, kn, vb, cc, cr = _attn_prep_fwd(z, fb, gq, gk)
        o, lse, yb, got = _flash_fwd(qs, kn, vb, cc, cr, z, comm=dist.fwd_comm(li) if dist else None)
        if dist:
            dist.fwd_done(li, got)
        lb = lower[li][None, :]
        gh = wts["hgrn_norm"][li][None, :]
        o_pre, states, yc = _hgrn_fwd(z, lb, gh, tmat, masks)
        gv = wts["sgu_norm"][li][None, :]
        ws = wts["spatial_w"][li]
        bs = _pad_lanes(wts["spatial_b"][li].T)
        yd = _sgu_fwd(z, gv, ws, bs)
        ys = (ya, yb, yc, yd)
        x1, merged = _merge_fwd(x, z, ys, big["w_up"], big["merge_b"], big["w_o"])
        g_ple = wts["norm_ple"][li][None, :]
        x2 = _ple_fwd(x1, p[li], g_ple, big["w_ple_gate"], big["w_ple_proj"])
        saved.append(dict(x=x, h=h, z=z, wz=wz, cw=cw, fb=fb, gq=gq, gk=gk, qs=qs, kn=kn, vb=vb, cc=cc, cr=cr, o=o, lse=lse,
                          lb=lb, gh=gh, o_pre=o_pre, states=states, gv=gv, ws=ws, bs=bs, ys=ys, x1=x1, merged=merged,
                          g_mix=g_mix, g_ple=g_ple, big=big))
        x = x2

    loss, dx = _loss_fwd_bwd(x, target)

    names = ["norm_mix", "w_in", "conv_w", "conv_b", "fgate_bias", "q_norm", "k_norm", "lb", "hgrn_norm", "sgu_norm",
             "spatial_w", "spatial_b", "w_up", "merge_b", "w_o", "norm_ple", "w_ple_gate", "w_ple_proj"]
    gl = {n: [None] * DEPTH for n in names}
    for li in reversed(range(DEPTH)):
        s = saved[li]
        z, big = s["z"], s["big"]
        wg, wp = big["w_ple_gate"], big["w_ple_proj"]
        dx1, hp, dgl, dpp, pb, st = _ple_bwd(s["x1"], p[li], dx, s["g_ple"], wg, wp)
        gl["norm_ple"][li] = st[0]
        gl["w_ple_gate"][li] = _mm_tn(hp, dgl, name="mm_dwg")
        gl["w_ple_proj"][li] = _mm_tn(pb, dpp, name="mm_dwp")
        wup, wo = big["w_up"], big["w_o"]
        dz, du, dxb, dy, st = _merge_bwd(dx1, z, s["ys"], wup, big["merge_b"], wo)
        gl["merge_b"][li] = st[0:NBR]
        gl["w_o"][li] = _mm_tn(s["merged"], dxb, name="mm_dwo")
        gl["w_up"][li] = jnp.stack([_mm_tn(s["ys"][b], du, name="mm_dwup", ycol=b, n=D_MODEL) for b in range(NBR)])
        dz, st = _conv_bwd(z, s["cw"], dy, dz)
        gl["conv_w"][li] = st[0:3]
        gl["conv_b"][li] = st[3]
        do, dgate = _attn_gate_bwd(dy, s["o"], z)
        fa = (s["qs"], s["kn"], s["vb"], s["cc"], s["cr"], s["lse"], s["o"], do)
        dq, drow, got = _flash_bwd_dq(*fa, comm=dist.bwd_comm() if dist else None)
        if dist:
            dist.bwd_done(got)
        dk, dv, dcr = _flash_bwd_dkv(*fa, drow)
        dcc = _pad_lanes(dcr.T)
        dz, dzf, st = _attn_prep_bwd(z, s["fb"], s["gq"], s["gk"], dq, dk, dv, dgate, dcc, dz)
        dz = _put_fgate_cols(dzf, dz)
        gl["q_norm"][li] = st[0, 0:DH]
        gl["k_norm"][li] = st[1, 0:DH]
        gl["fgate_bias"][li] = st[2, 0:NH]
        dz, st = _hgrn_bwd(z, s["lb"], s["gh"], tmat, masks, s["o_pre"], s["states"], dy, dz)
        gl["hgrn_norm"][li] = st[0]
        gl["lb"][li] = st[1]
        dz, dws, dbs, st = _sgu_bwd(z, s["gv"], s["ws"], s["bs"], dy, dz)
        gl["sgu_norm"][li] = st[0]
        gl["spatial_w"][li] = dws
        gl["spatial_b"][li] = dbs[:, 0:NH].T
        dwz = _mm_tn(s["h"], dz, name="mm_dwz", tn=1664)
        gl["w_in"][li] = dwz if dist else _w_in_from_wz(dwz)
        dh = _mm_nn(dz, s["wz"], F32, name="mm_dh", tk=1664, b_transposed=True)
        dx, st = _norm_bwd(s["x"], s["g_mix"], dh, dx1)
        gl["norm_mix"][li] = st[0]
        if dist:
            dist.push(li, {n: gl[n][li] for n in _BIG})

    if dist:
        dist.finish()
    grads = {n: jnp.stack(v) for n, v in gl.items() if not (dist and n in _BIG)}
    grads["lb_logits"] = _lower_bounds_bwd(wts["lb_logits"], grads.pop("lb"))
    return loss, dx, grads


def _my_pos():
    return lax.axis_index("x"), lax.axis_index("y"), lax.axis_index("c")


def _gather_phase(phase, ins, outs, sems):
    if phase == 1:
        return
    send, recv, lsem = sems
    x, y, c = _my_pos()
    me = 2 * x + y
    peers = [(1 - x, y), (x, 1 - y), (1 - x, 1 - y)]
    copies = []
    for t in range(len(ins)):
        copies.append(pltpu.make_async_copy(ins[t], outs[t].at[me], lsem.at[t]))
        for j, (px, py) in enumerate(peers):
            copies.append(pltpu.make_async_remote_copy(
                src_ref=ins[t], dst_ref=outs[t].at[me], send_sem=send.at[t, j], recv_sem=recv.at[t, j],
                device_id=(px, py, c), device_id_type=MESH))
    for cp in copies:
        if phase == 0:
            cp.start()
        else:
            cp.wait()


def _exchange_phase(phase, ins, outs, sems):
    send, recv, lsem = sems
    x, y, c = _my_pos()
    me = 2 * x + y
    sib = (x, y, 1 - c)
    chips = [(1 - x, y), (x, 1 - y), (1 - x, 1 - y)]

    def rc(t, k, src, dst, dev):
        return pltpu.make_async_remote_copy(src_ref=src, dst_ref=dst, send_sem=send.at[t, k], recv_sem=recv.at[t, k],
                                            device_id=dev, device_id_type=MESH)

    for t in range(len(ins)):
        local = pltpu.make_async_copy(ins[t].at[me], outs[t].at[2 * me + c], lsem.at[t])
        first = [rc(t, 0, ins[t].at[me], outs[t].at[2 * me + c], sib)]
        first += [rc(t, 1 + j, ins[t].at[2 * px + py], outs[t].at[2 * me + c], (px, py, c))
                  for j, (px, py) in enumerate(chips)]
        landed = [outs[t].at[2 * (2 * px + py) + c] for px, py in chips]
        passed = [rc(t, 4 + j, slot, slot, sib) for j, slot in enumerate(landed)]
        if phase == 0:
            local.start()
            for cp in first:
                cp.start()
        elif phase == 1:
            for j, slot in enumerate(landed):
                rc(t, 1 + j, slot, slot, (x, y, c)).wait_recv()
                passed[j].start()
        else:
            s0 = outs[t].at[2 * me + (1 - c)]
            rc(t, 0, s0, s0, (x, y, c)).wait_recv()
            for j, (px, py) in enumerate(chips):
                slot = outs[t].at[2 * (2 * px + py) + (1 - c)]
                rc(t, 4 + j, slot, slot, (x, y, c)).wait_recv()
            for cp in first + passed:
                cp.wait_send()
            local.wait()


_COMM_PHASES = {"gather": _gather_phase, "exchange": _exchange_phase}
_COMM_OUT = {"gather": lambda arrays: [jax.ShapeDtypeStruct((4,) + a.shape, a.dtype) for a in arrays],
             "exchange": lambda arrays: [jax.ShapeDtypeStruct((8,) + a.shape[1:], a.dtype) for a in arrays]}


def _comm_sems(kind, nc):
    k = 3 if kind == "gather" else 7
    return [pltpu.SemaphoreType.DMA((nc, k)), pltpu.SemaphoreType.DMA((nc, k)), pltpu.SemaphoreType.DMA((nc,))]


def _comm_alone(kind, arrays, name):
    nc = len(arrays)

    def body(*refs):
        for phase in range(3):
            _COMM_PHASES[kind](phase, refs[:nc], refs[nc:2 * nc], refs[2 * nc:])

    hbm = pl.BlockSpec(memory_space=pl.ANY)
    return pl.pallas_call(
        functools.partial(body), name=name, in_specs=[hbm] * nc, out_specs=[hbm] * nc,
        out_shape=_COMM_OUT[kind](arrays), scratch_shapes=_comm_sems(kind, nc))(*arrays)


def _allreduce_small(v):
    R = v.shape[0]

    def body(v_ref, o_ref, buf, send, recv):
        x, y, c = _my_pos()
        me = 4 * x + 2 * y + c
        buf[me] = v_ref[...]
        copies = []
        k = 0
        for dx in range(2):
            for dy in range(2):
                for dc in range(2):
                    if dx == 0 and dy == 0 and dc == 0:
                        continue
                    cp = pltpu.make_async_remote_copy(
                        src_ref=v_ref, dst_ref=buf.at[me], send_sem=send.at[k - 0], recv_sem=recv.at[k],
                        device_id=(jnp.bitwise_xor(x, dx), jnp.bitwise_xor(y, dy), jnp.bitwise_xor(c, dc)),
                        device_id_type=MESH)
                    cp.start()
                    copies.append(cp)
                    k += 1
        for cp in copies:
            cp.wait()
        acc = buf[0]
        for d in range(1, 8):
            acc = acc + buf[d]
        o_ref[...] = acc

    vm = pl.BlockSpec(memory_space=pltpu.VMEM)
    return pl.pallas_call(
        functools.partial(body), name="allreduce_small", in_specs=[vm], out_specs=vm,
        out_shape=jax.ShapeDtypeStruct((R, 128), F32),
        scratch_shapes=[pltpu.VMEM((8, R, 128), F32), pltpu.SemaphoreType.DMA((7,)), pltpu.SemaphoreType.DMA((7,))],
        compiler_params=pltpu.CompilerParams(vmem_limit_bytes=VMEM_LIMIT))(v)


def _adamw(w, m, v, parts, nparts):
    A, R, C = w.shape
    per_layer = isinstance(parts, (list, tuple))
    tr = _tile(R, 64) if per_layer else (R if R <= 128 else _tile(R, 128))
    nparr = len(parts) if per_layer else 1

    def body(*refs):
        w_ref, m_ref, v_ref = refs[:3]
        p_refs = refs[3:3 + nparr]
        g_ref, d_ref, nm_ref, nv_ref = refs[3 + nparr:]

        def update(read):
            g = read(0)
            for k in range(1, nparts):
                g = g + read(k)
            mm = ADAM_B1 * m_ref[0] + (1.0 - ADAM_B1) * g
            vv = ADAM_B2 * v_ref[0] + (1.0 - ADAM_B2) * jnp.square(g)
            m_hat = mm / (1.0 - ADAM_B1 ** ADAM_STEP)
            v_hat = vv / (1.0 - ADAM_B2 ** ADAM_STEP)
            g_ref[0] = g
            d_ref[0] = -ADAM_LR * (m_hat / (jnp.sqrt(v_hat) + ADAM_EPS) + ADAM_WD * w_ref[0])
            nm_ref[0] = mm
            nv_ref[0] = vv

        if not per_layer:
            update(lambda k: p_refs[0][k, 0])
        else:
            for a in range(A):
                @pl.when(pl.program_id(0) == a)
                def _(a=a):
                    update(lambda k: p_refs[a][k])

    blk = pl.BlockSpec((1, tr, C), lambda a, r: (a, r, 0))
    if per_layer:
        pspecs = [pl.BlockSpec((nparts, tr, C), functools.partial(lambda a, r, l: (0, jnp.where(a == l, r, 0), 0), l=l))
                  for l in range(A)]
        pargs = list(parts)
    else:
        pspecs = [pl.BlockSpec((nparts, 1, tr, C), lambda a, r: (0, a, r, 0))]
        pargs = [parts]
    return _call(
        body, name="adamw", grid=(A, R // tr), in_specs=[blk, blk, blk] + pspecs,
        out_specs=[blk] * 4, out_shape=[jax.ShapeDtypeStruct((A, R, C), F32)] * 4,
        sem=("arbitrary", "arbitrary"))(w, m, v, *pargs)


def _as3d(a):
    if a.ndim == 2:
        return a[None]
    if a.ndim == 3:
        return a
    return a.reshape((-1,) + a.shape[-2:])


_WEIGHTS = ["norm_mix", "w_in", "conv_w", "conv_b", "fgate_bias", "q_norm", "k_norm", "lb_logits", "hgrn_norm", "sgu_norm",
            "spatial_w", "spatial_b", "w_up", "merge_b", "w_o", "norm_ple", "w_ple_gate", "w_ple_proj"]
_BIG = ["w_in", "w_up", "w_o", "w_ple_gate", "w_ple_proj"]
_GATHERED = _BIG + ["conv_w", "merge_b"]
_SHARD_AXIS = {"w_in": 2, "w_up": 3, "w_o": 1, "w_ple_gate": 1, "w_ple_proj": 2, "conv_w": 2, "merge_b": 2}
_SMALL = [n for n in _WEIGHTS if n not in _BIG]


class _Dist:
    def __init__(self, w):
        self.w = w
        self.full = {0: self._unpack(_comm_alone("gather", self._shards(0), "gather_weights"))}
        self.contrib = {}
        self.pending = None

    def _shards(self, l):
        return [_mx(self.w[n][l]) for n in _BIG] + [self.w["conv_w"][l], self.w["merge_b"][l]]

    def _unpack(self, gathered):
        return {n: g if n == "w_in" else jnp.concatenate([g[k] for k in range(4)], axis=_SHARD_AXIS[n] - 1)
                for n, g in zip(_GATHERED, gathered)}

    def weights(self, l):
        return self.full[l]

    def fwd_comm(self, l):
        return ("gather", self._shards(l + 1)) if l + 1 < DEPTH else None

    def fwd_done(self, l, got):
        if got:
            self.full[l + 1] = self._unpack(got)

    def push(self, l, grads):
        self.pending = (l, [_w_in_slabs_from_wz(grads[n]) if n == "w_in"
                            else jnp.stack(jnp.split(grads[n], 4, axis=_SHARD_AXIS[n] - 1)) for n in _BIG])

    def bwd_comm(self):
        return ("exchange", self.pending[1]) if self.pending else None

    def bwd_done(self, got):
        if got:
            self.contrib[self.pending[0]] = got
            self.pending = None

    def finish(self):
        if self.pending:
            self.contrib[self.pending[0]] = _comm_alone("exchange", self.pending[1], "exchange_grads")
            self.pending = None


def kernel(x, p, norm_mix, w_in, conv_w, conv_b, fgate_bias, q_norm, k_norm, lb_logits, hgrn_norm, sgu_norm, spatial_w, spatial_b, w_up, merge_b, w_o, norm_ple, w_ple_gate, w_ple_proj, loss_target, m_norm_mix, m_w_in, m_conv_w, m_conv_b, m_fgate_bias, m_q_norm, m_k_norm, m_lb_logits, m_hgrn_norm, m_sgu_norm, m_spatial_w, m_spatial_b, m_w_up, m_merge_b, m_w_o, m_norm_ple, m_w_ple_gate, m_w_ple_proj, v_norm_mix, v_w_in, v_conv_w, v_conv_b, v_fgate_bias, v_q_norm, v_k_norm, v_lb_logits, v_hgrn_norm, v_sgu_norm, v_spatial_w, v_spatial_b, v_w_up, v_merge_b, v_w_o, v_norm_ple, v_w_ple_gate, v_w_ple_proj):
    loc = dict(locals())
    w = {n: loc[n] for n in _WEIGHTS}
    m = {n: loc["m_" + n] for n in _WEIGHTS}
    v = {n: loc["v_" + n] for n in _WEIGHTS}
    chip = 2 * lax.axis_index("x") + lax.axis_index("y")

    dist = _Dist(w)
    loss_blk, dx, grads = _local_step(x[0], p[:, 0], loss_target[0], w, dist)
    loss = lax.psum(loss_blk[0, 0], ("x", "y", "c"))

    flat = jnp.concatenate([grads[n].reshape(-1) for n in _SMALL])
    npad = (-flat.shape[0]) % 1024
    packed = jnp.concatenate([flat, jnp.zeros((npad,), F32)]).reshape(-1, 128)
    red = _allreduce_small(packed).reshape(-1)
    small, off = {}, 0
    for n in _SMALL:
        sz = int(np.prod(grads[n].shape))
        small[n] = red[off:off + sz].reshape(grads[n].shape)
        off += sz
    for n in ("conv_w", "merge_b"):
        ax = _SHARD_AXIS[n]
        width = small[n].shape[ax] // 4
        small[n] = lax.dynamic_slice_in_dim(small[n], chip * width, width, axis=ax)

    out_g, out_d, out_m, out_v = {}, {}, {}, {}
    for n in _WEIGHTS:
        shp = w[n].shape
        if n in _BIG:
            w3, m3, v3 = (a.reshape((DEPTH, -1, shp[-1])) for a in (w[n], m[n], v[n]))
            parts = [dist.contrib[l][_BIG.index(n)].reshape((8,) + w3.shape[1:]) for l in range(DEPTH)]
            g, d, nm, nv = _adamw(w3, m3, v3, parts, 8)
        else:
            g, d, nm, nv = _adamw(_as3d(w[n]), _as3d(m[n]), _as3d(v[n]), _as3d(small[n])[None], 1)
        out_g[n], out_d[n], out_m[n], out_v[n] = (a.reshape(shp) for a in (g, d, nm, nv))

    return (loss, dx[None], *[out_g[n] for n in _WEIGHTS], *[out_d[n] for n in _WEIGHTS],
            *[out_m[n] for n in _WEIGHTS], *[out_v[n] for n in _WEIGHTS])
```

```python
import functools

import numpy as np
import jax
import jax.numpy as jnp
from jax import lax
from jax.experimental import pallas as pl
from jax.experimental.pallas import tpu as pltpu

F32 = jnp.float32
MXU_DTYPE = jnp.bfloat16

D_MODEL = 1024
BW = 256
NH = 4
DH = 64
DEPTH = 4
NBR = 4
PLE = 256
CHUNK = 64
SCHUNK = 128
EPS = 1e-6
MASK_VALUE = -1e30
NLEV = 6

ADAM_LR, ADAM_B1, ADAM_B2, ADAM_EPS, ADAM_WD, ADAM_STEP = 0.001, 0.9, 0.999, 1e-08, 0.01, 10

ZM, ZA, ZB, ZC, ZD, ZF = 0, 4096, 5120, 6144, 7168, 8192
ZCOLS = 8320
_OA, _OB, _OF, _OC, _OD, _OM, _OEND = 0, 1024, 2048, 2052, 3076, 3844, 7940

VMEM_LIMIT = 56 * 1024 * 1024
ROW_TILE = 512
FLASH_TILE = 512
MERGE_TILE = 256
MESH = pl.DeviceIdType.MESH
_ANY = pl.BlockSpec(memory_space=pl.ANY)


def _tile(n, pref):
    t = min(n, pref)
    assert n % t == 0, (n, t)
    return t


def _call(body, *, name, grid, in_specs, out_specs, out_shape, scratch=(), sem=None, aliases=None):
    return pl.pallas_call(
        functools.partial(body), name=name, grid=grid, in_specs=in_specs, out_specs=out_specs,
        out_shape=out_shape, scratch_shapes=list(scratch), input_output_aliases=aliases or {},
        compiler_params=pltpu.CompilerParams(dimension_semantics=sem, vmem_limit_bytes=VMEM_LIMIT))


def _mx(x):
    return x.astype(MXU_DTYPE)


def _dot(a, b):
    return jnp.dot(_mx(a), _mx(b), preferred_element_type=F32)


def _dot_nt(a, b):
    return lax.dot_general(_mx(a), _mx(b), (((1,), (1,)), ((), ())), preferred_element_type=F32)


def _dot_tn(a, b):
    return lax.dot_general(_mx(a), _mx(b), (((0,), (0,)), ((), ())), preferred_element_type=F32)


def _top16(x):
    u = lax.bitcast_convert_type(x, jnp.uint32) & jnp.uint32(0xFFFF0000)
    return lax.bitcast_convert_type(u, F32)


def _split3(x):
    hi = _top16(x)
    r1 = x - hi
    mid = _top16(r1)
    return _mx(hi), _mx(mid), _mx(r1 - mid)


def _dot_exact_lhs01(t, x):
    hi, mid, lo = _split3(x)
    t = _mx(t)
    return (jnp.dot(t, hi, preferred_element_type=F32) + jnp.dot(t, mid, preferred_element_type=F32)
            + jnp.dot(t, lo, preferred_element_type=F32))


def _sigmoid(x):
    return jax.nn.sigmoid(x)


def _silu(x):
    return x * _sigmoid(x)


def _dsilu(x):
    s = _sigmoid(x)
    return s * (1.0 + x * (1.0 - s))


def _log_sigmoid(x):
    return jnp.minimum(x, 0.0) - jnp.log(1.0 + jnp.exp(-jnp.abs(x)))


def _rms_fwd(x, g):
    r = lax.rsqrt(jnp.mean(x * x, axis=-1, keepdims=True) + EPS)
    return x * r * g


def _rms_bwd(x, g, dy):
    r = lax.rsqrt(jnp.mean(x * x, axis=-1, keepdims=True) + EPS)
    n = x * r
    t = dy * g
    dx = r * (t - n * jnp.mean(t * n, axis=-1, keepdims=True))
    return dx, dy * n


def _tri(n, upper=False):
    r = lax.broadcasted_iota(jnp.int32, (n, n), 0)
    c = lax.broadcasted_iota(jnp.int32, (n, n), 1)
    return jnp.where((c >= r) if upper else (r >= c), 1.0, 0.0).astype(F32)


def _hs(h):
    return slice(h * DH, (h + 1) * DH)


def _mm_nn(a, b, out_dtype, *, name, tn=None, tk=None, b_transposed=False):
    S, K = a.shape
    N = b.shape[0] if b_transposed else b.shape[1]
    ts = _tile(S, ROW_TILE)
    tn = _tile(N, tn or N)
    tk = _tile(K, tk or K)
    nk = K // tk
    b_spec = (pl.BlockSpec((tn, tk), lambda j, i, k: (j, k)) if b_transposed
              else pl.BlockSpec((tk, tn), lambda j, i, k: (k, j)))

    def body(a_ref, b_ref, o_ref, acc_ref):
        k = pl.program_id(2)
        part = _dot_nt(a_ref[...], b_ref[...]) if b_transposed else jnp.dot(a_ref[...], b_ref[...],
                                                                          preferred_element_type=F32)
        if nk == 1:
            o_ref[...] = part.astype(o_ref.dtype)
        else:
            @pl.when(k == 0)
            def _():
                acc_ref[...] = part

            @pl.when(k > 0)
            def _():
                acc_ref[...] += part

            @pl.when(k == nk - 1)
            def _():
                o_ref[...] = acc_ref[...].astype(o_ref.dtype)

    return _call(
        body, name=name, grid=(N // tn, S // ts, nk),
        in_specs=[pl.BlockSpec((ts, tk), lambda j, i, k: (i, k)), b_spec],
        out_specs=pl.BlockSpec((ts, tn), lambda j, i, k: (i, j)),
        out_shape=jax.ShapeDtypeStruct((S, N), out_dtype),
        scratch=[pltpu.VMEM((ts, tn) if nk > 1 else (8, 128), F32)],
        sem=("parallel", "parallel", "arbitrary"))(a, b)


def _mm_tn(x, y, *, name, tn=None, ycol=0, n=None):
    S, M = x.shape
    n = n or y.shape[1]
    ts = _tile(S, ROW_TILE)
    tn = _tile(n, tn or n)
    nj = n // tn

    def body(x_ref, y_ref, o_ref):
        @pl.when(pl.program_id(1) == 0)
        def _():
            o_ref[...] = jnp.zeros_like(o_ref)

        o_ref[...] += lax.dot_general(x_ref[...], y_ref[...], (((0,), (0,)), ((), ())), preferred_element_type=F32)

    return _call(
        body, name=name, grid=(nj, S // ts),
        in_specs=[pl.BlockSpec((ts, M), lambda j, s: (s, 0)), pl.BlockSpec((ts, tn), lambda j, s: (s, ycol * nj + j))],
        out_specs=pl.BlockSpec((M, tn), lambda j, s: (0, j)),
        out_shape=jax.ShapeDtypeStruct((M, n), F32),
        sem=("parallel", "arbitrary"))(x, y)


def _norm_fwd(x, g):
    S = x.shape[0]
    ts = _tile(S, ROW_TILE)

    def body(x_ref, g_ref, h_ref):
        h_ref[...] = _mx(_rms_fwd(x_ref[...], g_ref[...]))

    return _call(
        body, name="norm_fwd", grid=(S // ts,),
        in_specs=[pl.BlockSpec((ts, D_MODEL), lambda i: (i, 0)), pl.BlockSpec((1, D_MODEL), lambda i: (0, 0))],
        out_specs=pl.BlockSpec((ts, D_MODEL), lambda i: (i, 0)),
        out_shape=jax.ShapeDtypeStruct((S, D_MODEL), MXU_DTYPE), sem=("parallel",))(x, g)


def _norm_bwd(x, g, dh, dres):
    S = x.shape[0]
    ts = _tile(S, ROW_TILE)

    def body(x_ref, g_ref, dh_ref, dr_ref, dx_ref, st_ref):
        @pl.when(pl.program_id(0) == 0)
        def _():
            st_ref[...] = jnp.zeros_like(st_ref)

        dx, dgr = _rms_bwd(x_ref[...], g_ref[...], dh_ref[...])
        dx_ref[...] = dr_ref[...] + dx
        st_ref[0:1, :] += jnp.sum(dgr, axis=0, keepdims=True)

    row = pl.BlockSpec((ts, D_MODEL), lambda i: (i, 0))
    return _call(
        body, name="norm_bwd", grid=(S // ts,),
        in_specs=[row, pl.BlockSpec((1, D_MODEL), lambda i: (0, 0)), row, row],
        out_specs=[row, pl.BlockSpec((8, D_MODEL), lambda i: (0, 0))],
        out_shape=[jax.ShapeDtypeStruct((S, D_MODEL), F32), jax.ShapeDtypeStruct((8, D_MODEL), F32)],
        sem=("arbitrary",))(x, g, dh, dres)


def _loss_fwd_bwd(y, target):
    S = y.shape[0]
    ts = _tile(S, ROW_TILE)

    def body(y_ref, t_ref, l_ref, dy_ref):
        @pl.when(pl.program_id(0) == 0)
        def _():
            l_ref[...] = jnp.zeros_like(l_ref)

        err = y_ref[...] - t_ref[...]
        dy_ref[...] = err * (1.0 / D_MODEL)
        rowloss = jnp.mean(err * err, axis=-1, keepdims=True)
        l_ref[...] += 0.5 * jnp.sum(rowloss, axis=0, keepdims=True)

    row = pl.BlockSpec((ts, D_MODEL), lambda i: (i, 0))
    return _call(
        body, name="loss", grid=(S // ts,), in_specs=[row, row],
        out_specs=[pl.BlockSpec((8, 128), lambda i: (0, 0)), row],
        out_shape=[jax.ShapeDtypeStruct((8, 128), F32), jax.ShapeDtypeStruct((S, D_MODEL), F32)],
        sem=("arbitrary",))(y, target)


def _shift_down(x, k, halo, rows):
    y = pltpu.roll(x, k, 0)
    for j in range(k):
        y = jnp.where(rows == j, halo[8 - k + j:8 - k + j + 1, :], y)
    return y


def _shift_up(x, k, halo, rows, n):
    y = pltpu.roll(x, n - k, 0)
    for j in range(k):
        y = jnp.where(rows == n - k + j, halo[j:j + 1, :], y)
    return y


def _conv_parts(za, zh, cw, first, rows):
    ax, ab, ac, ag = za[:, 0:BW], za[:, BW:2 * BW], za[:, 2 * BW:3 * BW], za[:, 3 * BW:4 * BW]
    zz = ac * ax
    hz = jnp.where(first, 0.0, zh[:, 2 * BW:3 * BW] * zh[:, 0:BW])
    zz1 = _shift_down(zz, 1, hz, rows)
    zz2 = _shift_down(zz, 2, hz, rows)
    conv = zz2 * cw[0:1, :] + zz1 * cw[1:2, :] + zz * cw[2:3, :] + cw[3:4, :]
    return ax, ab, ac, ag, zz, zz1, zz2, conv


def _conv_fwd(z, cw):
    S = z.shape[0]
    ts = _tile(S, ROW_TILE)
    hb = ts // 8

    def body(za_ref, zh_ref, cw_ref, y_ref):
        i = pl.program_id(0)
        rows = lax.broadcasted_iota(jnp.int32, (ts, BW), 0)
        ax, ab, ac, ag, zz, zz1, zz2, conv = _conv_parts(za_ref[...], zh_ref[...], cw_ref[...], i == 0, rows)
        y_ref[...] = _mx(ab * conv * _silu(ag))

    return _call(
        body, name="conv_fwd", grid=(S // ts,),
        in_specs=[pl.BlockSpec((ts, 1024), lambda i: (i, ZA // 1024)),
                  pl.BlockSpec((8, 1024), lambda i: (jnp.maximum(i * hb - 1, 0), ZA // 1024)),
                  pl.BlockSpec((8, BW), lambda i: (0, 0))],
        out_specs=pl.BlockSpec((ts, BW), lambda i: (i, 0)),
        out_shape=jax.ShapeDtypeStruct((S, BW), MXU_DTYPE), sem=("parallel",))(z, z, cw)


def _conv_bwd(z, cw, dy, dz):
    S = z.shape[0]
    ts = _tile(S, ROW_TILE)
    hb = ts // 8
    nt = S // ts

    def body(za_ref, zh_ref, zn_ref, cw_ref, dy_ref, dyn_ref, dz_in, dz_ref, st_ref):
        i = pl.program_id(0)

        @pl.when(i == 0)
        def _():
            st_ref[...] = jnp.zeros_like(st_ref)

        cw = cw_ref[...]
        rows = lax.broadcasted_iota(jnp.int32, (ts, BW), 0)
        ax, ab, ac, ag, zz, zz1, zz2, conv = _conv_parts(za_ref[...], zh_ref[...], cw, i == 0, rows)
        dy = dy_ref[...]
        sg = _silu(ag)
        dc = dy * ab * sg
        zn = zn_ref[...]
        dcn = jnp.where(i == nt - 1, 0.0, dyn_ref[...] * zn[:, BW:2 * BW] * _silu(zn[:, 3 * BW:4 * BW]))
        dc1 = _shift_up(dc, 1, dcn, rows, ts)
        dc2 = _shift_up(dc, 2, dcn, rows, ts)
        dzz = dc * cw[2:3, :] + dc1 * cw[1:2, :] + dc2 * cw[0:1, :]
        dz_ref[:, 0:BW] = _mx(dzz * ac)
        dz_ref[:, BW:2 * BW] = _mx(dy * conv * sg)
        dz_ref[:, 2 * BW:3 * BW] = _mx(dzz * ax)
        dz_ref[:, 3 * BW:4 * BW] = _mx(dy * ab * conv * _dsilu(ag))
        st_ref[0:1, :] += jnp.sum(dc * zz2, axis=0, keepdims=True)
        st_ref[1:2, :] += jnp.sum(dc * zz1, axis=0, keepdims=True)
        st_ref[2:3, :] += jnp.sum(dc * zz, axis=0, keepdims=True)
        st_ref[3:4, :] += jnp.sum(dc, axis=0, keepdims=True)

    return _call(
        body, name="conv_bwd", grid=(nt,),
        in_specs=[pl.BlockSpec((ts, 1024), lambda i: (i, ZA // 1024)),
                  pl.BlockSpec((8, 1024), lambda i: (jnp.maximum(i * hb - 1, 0), ZA // 1024)),
                  pl.BlockSpec((8, 1024), lambda i: (jnp.minimum((i + 1) * hb, S // 8 - 1), ZA // 1024)),
                  pl.BlockSpec((8, BW), lambda i: (0, 0)),
                  pl.BlockSpec((ts, BW), lambda i: (i, 0)),
                  pl.BlockSpec((8, BW), lambda i: (jnp.minimum((i + 1) * hb, S // 8 - 1), 0)), _ANY],
        out_specs=[pl.BlockSpec((ts, 1024), lambda i: (i, ZA // 1024)), pl.BlockSpec((8, BW), lambda i: (0, 0))],
        out_shape=[jax.ShapeDtypeStruct((S, ZCOLS), MXU_DTYPE), jax.ShapeDtypeStruct((8, BW), F32)],
        sem=("arbitrary",), aliases={6: 0})(z, z, z, cw, dy, dy, dz)


def _attn_prep_fwd(z, fb, gq, gk):
    S = z.shape[0]
    ts = _tile(S, ROW_TILE)

    def body(zb_ref, zf_ref, fb_ref, gq_ref, gk_ref, q_ref, k_ref, v_ref, cc_ref, cr_ref, carry):
        @pl.when(pl.program_id(0) == 0)
        def _():
            carry[...] = jnp.zeros_like(carry)

        zb = zb_ref[...]
        gqv, gkv = gq_ref[...], gk_ref[...]
        for h in range(NH):
            q_ref[:, _hs(h)] = _mx(_rms_fwd(zb[:, _hs(h)], gqv) * (DH ** -0.5))
            k_ref[:, _hs(h)] = _mx(_rms_fwd(zb[:, BW + h * DH:BW + (h + 1) * DH], gkv))
        v_ref[...] = _mx(zb[:, 2 * BW:3 * BW])
        lf = _log_sigmoid(zf_ref[...] + fb_ref[...])
        cum = _dot_exact_lhs01(_tri(ts), lf) + carry[...]
        carry[...] = cum[ts - 1:ts, :]
        cc_ref[...] = cum
        cr_ref[...] = cum.T[0:8, :]

    row = lambda w: pl.BlockSpec((ts, w), lambda i: (i, 0))
    return _call(
        body, name="attn_prep_fwd", grid=(S // ts,),
        in_specs=[pl.BlockSpec((ts, 1024), lambda i: (i, ZB // 1024)), pl.BlockSpec((ts, 128), lambda i: (i, ZF // 128)),
                  pl.BlockSpec((1, 128), lambda i: (0, 0)), pl.BlockSpec((1, DH), lambda i: (0, 0)),
                  pl.BlockSpec((1, DH), lambda i: (0, 0))],
        out_specs=[row(BW), row(BW), row(BW), row(128), pl.BlockSpec((8, ts), lambda i: (0, i))],
        out_shape=[jax.ShapeDtypeStruct((S, BW), MXU_DTYPE)] * 3
        + [jax.ShapeDtypeStruct((S, 128), F32), jax.ShapeDtypeStruct((8, S), F32)],
        scratch=[pltpu.VMEM((1, 128), F32)], sem=("arbitrary",))(z, z, fb, gq, gk)


def _attn_scores(q, k, cc, cr, h, i, j, t, masked):
    s = _dot_nt(q[:, _hs(h)], k[:, _hs(h)]) + (cc[:, h:h + 1] - cr[h:h + 1, :])
    if not masked:
        return s, None
    qpos = i * t + lax.broadcasted_iota(jnp.int32, (t, t), 0)
    kpos = j * t + lax.broadcasted_iota(jnp.int32, (t, t), 1)
    return s, qpos >= kpos


ROW_CHUNK = 256


def _fuse_comm(core, n_in, n_out, comm, n):
    nc = 0 if comm is None else len(comm[1])

    def body(*refs):
        cin, xin = refs[:n_in], refs[n_in:n_in + nc]
        a = n_in + nc
        cout, xout = refs[a:a + n_out], refs[a + n_out:a + n_out + nc]
        rest = refs[a + n_out + nc:]
        if nc == 0:
            core(*cin, *cout, *rest)
            return
        cscr, sems = rest[:-3], rest[-3:]
        i, j = pl.program_id(0), pl.program_id(1)
        phase = _COMM_PHASES[comm[0]]

        @pl.when((i == 0) & (j == 0))
        def _():
            phase(0, xin, xout, sems)

        core(*cin, *cout, *cscr)

        @pl.when((i == max(n - 2, 0)) & (j == 0))
        def _():
            phase(1, xin, xout, sems)

        @pl.when((i == n - 1) & (j == n - 1))
        def _():
            phase(2, xin, xout, sems)

    return body


def _comm_specs(comm):
    if comm is None:
        return [], [], [], []
    hbm = pl.BlockSpec(memory_space=pl.ANY)
    nc = len(comm[1])
    return [hbm] * nc, [hbm] * nc, _COMM_OUT[comm[0]](comm[1]), _comm_sems(comm[0], nc)


def _flash_fwd(q, k, v, cc, cr, z, comm=None):
    S = q.shape[0]
    t = _tile(S, FLASH_TILE)
    n = S // t

    rch = _tile(t, ROW_CHUNK)

    def core(q_ref, k_ref, v_ref, cc_ref, cr_ref, zb_ref, o_ref, lse_ref, y_ref, m_sc, l_sc, acc):
        i, j = pl.program_id(0), pl.program_id(1)

        @pl.when(j == 0)
        def _():
            m_sc[...] = jnp.full_like(m_sc, MASK_VALUE)
            l_sc[...] = jnp.zeros_like(l_sc)
            acc[...] = jnp.zeros_like(acc)

        def block(masked):
            for h in range(NH):
                for rc in range(t // rch):
                    rows = slice(rc * rch, (rc + 1) * rch)
                    s = (_dot_nt(q_ref[rows, _hs(h)], k_ref[:, _hs(h)])
                         + (cc_ref[rows, h:h + 1] - cr_ref[h:h + 1, :]))
                    if masked:
                        qpos = rc * rch + lax.broadcasted_iota(jnp.int32, (rch, t), 0)
                        s = jnp.where(qpos >= lax.broadcasted_iota(jnp.int32, (rch, t), 1), s, MASK_VALUE)
                    m_old = m_sc[h, rows, :]
                    m_new = jnp.maximum(m_old, jnp.max(s, axis=-1, keepdims=True))
                    p = jnp.exp(s - m_new)
                    alpha = jnp.exp(m_old - m_new)
                    l_sc[h, rows, :] = alpha * l_sc[h, rows, :] + jnp.sum(p, axis=-1, keepdims=True)
                    acc[rows, _hs(h)] = alpha * acc[rows, _hs(h)] + _dot(p, v_ref[:, _hs(h)])
                    m_sc[h, rows, :] = m_new

        @pl.when(j < i)
        def _():
            block(False)

        @pl.when(j == i)
        def _():
            block(True)
            lse_ref[...] = jnp.zeros_like(lse_ref)
            for h in range(NH):
                o_ref[:, _hs(h)] = acc[:, _hs(h)] / l_sc[h]
                lse_ref[:, h:h + 1] = m_sc[h] + jnp.log(l_sc[h])
            y_ref[...] = _mx(o_ref[...] * _silu(zb_ref[:, 3 * BW:4 * BW]))

    qspec = lambda w: pl.BlockSpec((t, w), lambda i, j: (i, 0))
    kspec = lambda w: pl.BlockSpec((t, w), lambda i, j: (jnp.minimum(j, i), 0))
    xin, xout, xshape, xsem = _comm_specs(comm)
    res = _call(
        _fuse_comm(core, 6, 3, comm, n), name="flash_fwd", grid=(n, n),
        in_specs=[qspec(BW), kspec(BW), kspec(BW), qspec(128), pl.BlockSpec((8, t), lambda i, j: (0, jnp.minimum(j, i))),
                  pl.BlockSpec((t, 1024), lambda i, j: (i, ZB // 1024))] + xin,
        out_specs=[qspec(BW), qspec(128), qspec(BW)] + xout,
        out_shape=[jax.ShapeDtypeStruct((S, BW), F32), jax.ShapeDtypeStruct((S, 128), F32),
                   jax.ShapeDtypeStruct((S, BW), MXU_DTYPE)] + xshape,
        scratch=[pltpu.VMEM((NH, t, 1), F32), pltpu.VMEM((NH, t, 1), F32), pltpu.VMEM((t, BW), F32)] + xsem,
        sem=("arbitrary", "arbitrary"))(q, k, v, cc, cr, z, *(comm[1] if comm else []))
    return res[0], res[1], res[2], list(res[3:])


def _attn_gate_bwd(dy, o, z):
    S = dy.shape[0]
    ts = _tile(S, ROW_TILE)

    def body(dy_ref, o_ref, zb_ref, do_ref, dg_ref):
        g = zb_ref[:, 3 * BW:4 * BW]
        dy = dy_ref[...]
        do_ref[...] = dy * _silu(g)
        dg_ref[...] = _mx(dy * o_ref[...] * _dsilu(g))

    row = pl.BlockSpec((ts, BW), lambda i: (i, 0))
    return _call(
        body, name="attn_gate_bwd", grid=(S // ts,),
        in_specs=[pl.BlockSpec((ts, BW), lambda i: (i, 1)), row, pl.BlockSpec((ts, 1024), lambda i: (i, ZB // 1024))],
        out_specs=[row, row],
        out_shape=[jax.ShapeDtypeStruct((S, BW), F32), jax.ShapeDtypeStruct((S, BW), MXU_DTYPE)],
        sem=("parallel",))(dy, o, z)


def _flash_probs(refs, h, rows, masked):
    q_ref, k_ref, v_ref, cc_ref, cr_ref, lse_ref, o_ref, do_ref = refs
    hs = _hs(h)
    s = _dot_nt(q_ref[rows, hs], k_ref[:, hs]) + (cc_ref[rows, h:h + 1] - cr_ref[h:h + 1, :])
    p = jnp.exp(s - lse_ref[rows, h:h + 1])
    if masked:
        qpos = rows.start + lax.broadcasted_iota(jnp.int32, s.shape, 0)
        p = jnp.where(qpos >= lax.broadcasted_iota(jnp.int32, s.shape, 1), p, 0.0)
    doh = do_ref[rows, hs]
    dp = _dot_nt(doh, v_ref[:, hs])
    delta = jnp.sum(doh * o_ref[rows, hs], axis=-1, keepdims=True)
    return p, p * (dp - delta), doh


def _flash_bwd_dq(q, k, v, cc, cr, lse, o, do, comm=None):
    S = q.shape[0]
    t = _tile(S, FLASH_TILE)
    n = S // t
    rch = _tile(t, ROW_CHUNK)

    def core(q_ref, k_ref, v_ref, cc_ref, cr_ref, lse_ref, o_ref, do_ref, dq_ref, dr_ref):
        i, j = pl.program_id(0), pl.program_id(1)

        @pl.when(j == 0)
        def _():
            dq_ref[...] = jnp.zeros_like(dq_ref)
            dr_ref[...] = jnp.zeros_like(dr_ref)

        def block(masked):
            refs = (q_ref, k_ref, v_ref, cc_ref, cr_ref, lse_ref, o_ref, do_ref)
            for h in range(NH):
                for rc in range(t // rch):
                    rows = slice(rc * rch, (rc + 1) * rch)
                    p, ds, doh = _flash_probs(refs, h, rows, masked)
                    dq_ref[rows, _hs(h)] += _dot(ds, k_ref[:, _hs(h)])
                    dr_ref[rows, h:h + 1] += jnp.sum(ds, axis=-1, keepdims=True)

        @pl.when(j < i)
        def _():
            block(False)

        @pl.when(j == i)
        def _():
            block(True)
            for h in range(NH):
                dr_ref[:, h:h + 1] += jnp.sum(do_ref[:, _hs(h)] * o_ref[:, _hs(h)], axis=-1, keepdims=True)

    qspec = lambda w: pl.BlockSpec((t, w), lambda i, j: (i, 0))
    kspec = lambda w: pl.BlockSpec((t, w), lambda i, j: (jnp.minimum(j, i), 0))
    xin, xout, xshape, xsem = _comm_specs(comm)
    res = _call(
        _fuse_comm(core, 8, 2, comm, n), name="flash_bwd_dq", grid=(n, n),
        in_specs=[qspec(BW), kspec(BW), kspec(BW), qspec(128), pl.BlockSpec((8, t), lambda i, j: (0, jnp.minimum(j, i))),
                  qspec(128), qspec(BW), qspec(BW)] + xin,
        out_specs=[qspec(BW), qspec(128)] + xout,
        out_shape=[jax.ShapeDtypeStruct((S, BW), F32), jax.ShapeDtypeStruct((S, 128), F32)] + xshape,
        scratch=xsem,
        sem=("arbitrary", "arbitrary"))(q, k, v, cc, cr, lse, o, do, *(comm[1] if comm else []))
    return res[0], res[1], list(res[2:])


def _flash_bwd_dkv(q, k, v, cc, cr, lse_rows, do, delta_rows):
    S = q.shape[0]
    t = _tile(S, FLASH_TILE)
    n = S // t

    def body(q_ref, k_ref, v_ref, cck_ref, crq_ref, lse_ref, do_ref, dl_ref, dk_ref, dv_ref, dc_ref):
        j, i = pl.program_id(0), pl.program_id(1)

        @pl.when(i == 0)
        def _():
            dk_ref[...] = jnp.zeros_like(dk_ref)
            dv_ref[...] = jnp.zeros_like(dv_ref)
            dc_ref[...] = jnp.zeros_like(dc_ref)

        def block(masked):
            for h in range(NH):
                hs = _hs(h)
                qh, doh = q_ref[:, hs], _mx(do_ref[:, hs])
                st = _dot_nt(k_ref[:, hs], qh) + (crq_ref[h:h + 1, :] - cck_ref[:, h:h + 1])
                pt = jnp.exp(st - lse_ref[h:h + 1, :])
                if masked:
                    kpos = lax.broadcasted_iota(jnp.int32, (t, t), 0)
                    pt = jnp.where(kpos <= lax.broadcasted_iota(jnp.int32, (t, t), 1), pt, 0.0)
                dst = pt * (_dot_nt(v_ref[:, hs], doh) - dl_ref[h:h + 1, :])
                dv_ref[:, hs] += _dot(pt, doh)
                dk_ref[:, hs] += _dot(dst, qh)
                dc_ref[:, h:h + 1] += -jnp.sum(dst, axis=-1, keepdims=True)

        @pl.when(i > j)
        def _():
            block(False)

        @pl.when(i == j)
        def _():
            block(True)

    qspec = lambda w: pl.BlockSpec((t, w), lambda j, i: (jnp.maximum(i, j), 0))
    kspec = lambda w: pl.BlockSpec((t, w), lambda j, i: (j, 0))
    qrows = pl.BlockSpec((8, t), lambda j, i: (0, jnp.maximum(i, j)))
    return _call(
        body, name="flash_bwd_dkv", grid=(n, n),
        in_specs=[qspec(BW), kspec(BW), kspec(BW), kspec(128), qrows, qrows, qspec(BW), qrows],
        out_specs=[kspec(BW), kspec(BW), kspec(128)],
        out_shape=[jax.ShapeDtypeStruct((S, BW), F32), jax.ShapeDtypeStruct((S, BW), F32),
                   jax.ShapeDtypeStruct((S, 128), F32)],
        sem=("parallel", "arbitrary"))(q, k, v, cc, cr, lse_rows, do, delta_rows)


def _attn_prep_bwd(z, fb, gq, gk, dq, dk, dv, dgate, dcc, dz):
    S = z.shape[0]
    ts = _tile(S, ROW_TILE)
    nt = S // ts

    def body(zb_ref, zf_ref, fb_ref, gq_ref, gk_ref, dq_ref, dk_ref, dv_ref, dg_ref, dcc_ref, dz_in, dzb_ref, dzf_ref,
             st_ref, carry):
        @pl.when(pl.program_id(0) == 0)
        def _():
            carry[...] = jnp.zeros_like(carry)
            st_ref[...] = jnp.zeros_like(st_ref)

        zb = zb_ref[...]
        gqv, gkv = gq_ref[...], gk_ref[...]
        dqv, dkv = dq_ref[...], dk_ref[...]
        sq = jnp.zeros((1, DH), F32)
        sk = jnp.zeros((1, DH), F32)
        for h in range(NH):
            dx, dgr = _rms_bwd(zb[:, _hs(h)], gqv, dqv[:, _hs(h)] * (DH ** -0.5))
            dzb_ref[:, _hs(h)] = _mx(dx)
            sq = sq + jnp.sum(dgr, axis=0, keepdims=True)
            ks = slice(BW + h * DH, BW + (h + 1) * DH)
            dx, dgr = _rms_bwd(zb[:, ks], gkv, dkv[:, _hs(h)])
            dzb_ref[:, ks] = _mx(dx)
            sk = sk + jnp.sum(dgr, axis=0, keepdims=True)
        dzb_ref[:, 2 * BW:3 * BW] = _mx(dv_ref[...])
        dzb_ref[:, 3 * BW:4 * BW] = dg_ref[...]
        dc = dcc_ref[...]
        dlf = _dot_exact_lhs01(_tri(ts, upper=True), dc) + carry[...]
        carry[...] = dlf[0:1, :]
        dfz = dlf * _sigmoid(-(zf_ref[...] + fb_ref[...]))
        dzf_ref[...] = _mx(dfz)
        st_ref[0:1, 0:DH] += sq
        st_ref[1:2, 0:DH] += sk
        st_ref[2:3, :] += jnp.sum(dfz, axis=0, keepdims=True)

    rev = lambda w, c=0: pl.BlockSpec((ts, w), lambda i: (nt - 1 - i, c))
    one = lambda w: pl.BlockSpec((1, w), lambda i: (0, 0))
    return _call(
        body, name="attn_prep_bwd", grid=(nt,),
        in_specs=[rev(1024, ZB // 1024), rev(128, ZF // 128), one(128), one(DH), one(DH),
                  rev(BW), rev(BW), rev(BW), rev(BW), rev(128), _ANY],
        out_specs=[rev(1024, ZB // 1024), rev(128), pl.BlockSpec((8, 128), lambda i: (0, 0))],
        out_shape=[jax.ShapeDtypeStruct((S, ZCOLS), MXU_DTYPE), jax.ShapeDtypeStruct((S, 128), MXU_DTYPE),
                   jax.ShapeDtypeStruct((8, 128), F32)],
        scratch=[pltpu.VMEM((1, 128), F32)], sem=("arbitrary",), aliases={10: 0})(z, z, fb, gq, gk, dq, dk, dv, dgate, dcc, dz)


def _put_fgate_cols(dzf, dz):
    S = dzf.shape[0]
    ts = _tile(S, ROW_TILE)

    def body(f_ref, dz_in, o_ref):
        o_ref[...] = f_ref[...]

    return _call(
        body, name="put_fgate_cols", grid=(S // ts,),
        in_specs=[pl.BlockSpec((ts, 128), lambda i: (i, 0)), _ANY],
        out_specs=pl.BlockSpec((ts, 128), lambda i: (i, ZF // 128)),
        out_shape=jax.ShapeDtypeStruct((S, ZCOLS), MXU_DTYPE), sem=("parallel",), aliases={1: 0})(dzf, dz)


def _hgrn_consts():
    C = CHUNK
    t = np.arange(C)[:, None]
    j = np.arange(C)[None, :]
    mats = [(j <= t)]
    tq, tk, masks = [], [], []
    for lev in range(NLEV):
        m = C >> (lev + 1)
        blk, pos = t // (2 * m), t % (2 * m)
        mid = blk * 2 * m + m - 1
        tq.append((pos >= m) & (j > mid) & (j <= t))
        tk.append((pos < m) & (j > t) & (j <= mid))
        sblk, spos = j // (2 * m), j % (2 * m)
        masks.append((blk == sblk) & (pos >= m) & (spos < m))
    masks.append(t == j)
    tmat = np.concatenate(mats + tq + tk, axis=0).astype(np.float32)
    return tmat, np.stack(masks).astype(np.float32)


def _hgrn_chunk_fwd(zc, lb, tmat):
    cq, cf, ci = zc[:, 0:BW], zc[:, BW:2 * BW], zc[:, 2 * BW:3 * BW]
    q = _silu(cq)
    sg = _sigmoid(cf)
    g = lb + (1.0 - lb) * sg
    lg = jnp.log(g)
    kf = (1.0 - lb) * _sigmoid(-cf)
    e = _dot_exact_lhs01(tmat, lg)
    b = e[0:CHUNK]
    blast = b[CHUNK - 1:CHUNK, :]
    return cq, cf, q, sg, g, kf, ci, e, b, blast


def _round_mx(x):
    if MXU_DTYPE != jnp.bfloat16:
        return x
    u = lax.bitcast_convert_type(x, jnp.uint32)
    u = (u + jnp.uint32(0x7FFF) + ((u >> 16) & jnp.uint32(1))) & jnp.uint32(0xFFFF0000)
    return lax.bitcast_convert_type(u, F32)


def _hgrn_scores(q, kf, e, masks, h):
    qh, kh = q[:, _hs(h)], kf[:, _hs(h)]
    ql, kl = [], []
    a = None
    for lev in range(NLEV + 1):
        if lev < NLEV:
            eq = jnp.exp(e[(1 + lev) * CHUNK:(2 + lev) * CHUNK, _hs(h)])
            ek = jnp.exp(e[(1 + NLEV + lev) * CHUNK:(2 + NLEV + lev) * CHUNK, _hs(h)])
            ql.append((_round_mx(qh * eq), eq))
            kl.append((_round_mx(kh * ek), ek))
        else:
            ql.append((_round_mx(qh), None))
            kl.append((_round_mx(kh), None))
        term = masks[lev] * _dot_nt(ql[-1][0], kl[-1][0])
        a = term if a is None else a + term
    return a, ql, kl


def _hgrn_fwd(z, lb, gain, tmat, masks):
    S = z.shape[0]
    ts = _tile(S, ROW_TILE)
    nc = ts // CHUNK

    def body(zc_ref, lb_ref, gn_ref, tm_ref, mk_ref, o_ref, st_ref, y_ref, state):
        @pl.when(pl.program_id(0) == 0)
        def _():
            state[...] = jnp.zeros_like(state)

        lbv, gn, tm, mk = lb_ref[...], gn_ref[...], tm_ref[...], mk_ref[...]

        def chunk(c, carry):
            r0 = pl.multiple_of(c * CHUNK, CHUNK)
            zc = zc_ref[pl.ds(r0, CHUNK), :]
            cq, cf, q, sg, g, kf, v, e, b, blast = _hgrn_chunk_fwd(zc, lbv, tm)
            qe = q * jnp.exp(b)
            kd = kf * jnp.exp(blast - b)
            st_ref[pl.ds(r0, CHUNK), :] = state[...]
            for h in range(NH):
                sth = state[:, _hs(h)]
                a, _, _ = _hgrn_scores(q, kf, e, mk, h)
                oh = _dot_nt(qe[:, _hs(h)], sth) + _dot(a, v[:, _hs(h)])
                state[:, _hs(h)] = sth * jnp.exp(blast[:, _hs(h)]) + _dot_tn(v[:, _hs(h)], kd[:, _hs(h)])
                o_ref[pl.ds(r0, CHUNK), _hs(h)] = oh
                yn = _rms_fwd(oh, gn[:, _hs(h)])
                y_ref[pl.ds(r0, CHUNK), _hs(h)] = _mx(yn * _silu(zc[:, 3 * BW + h * DH:3 * BW + (h + 1) * DH]))
            return carry

        lax.fori_loop(0, nc, chunk, 0)

    row = pl.BlockSpec((ts, BW), lambda i: (i, 0))
    one = pl.BlockSpec((1, BW), lambda i: (0, 0))
    return _call(
        body, name="hgrn_fwd", grid=(S // ts,),
        in_specs=[pl.BlockSpec((ts, 1024), lambda i: (i, ZC // 1024)), one, one,
                  pl.BlockSpec(tmat.shape, lambda i: (0, 0)), pl.BlockSpec(masks.shape, lambda i: (0, 0, 0))],
        out_specs=[row, row, row],
        out_shape=[jax.ShapeDtypeStruct((S, BW), F32), jax.ShapeDtypeStruct((S, BW), F32),
                   jax.ShapeDtypeStruct((S, BW), MXU_DTYPE)],
        scratch=[pltpu.VMEM((CHUNK, BW), F32)], sem=("arbitrary",))(z, lb, gain, tmat, masks)


def _hgrn_bwd(z, lb, gain, tmat, masks, o_pre, states, dy, dz):
    S = z.shape[0]
    ts = _tile(S, ROW_TILE)
    nt = S // ts
    nc = ts // CHUNK

    def body(zc_ref, lb_ref, gn_ref, tm_ref, mk_ref, o_ref, st_ref, dy_ref, dz_in, dz_ref, stat_ref, dstate):
        @pl.when(pl.program_id(0) == 0)
        def _():
            dstate[...] = jnp.zeros_like(dstate)
            stat_ref[...] = jnp.zeros_like(stat_ref)

        lbv, gn, tm, mk = lb_ref[...], gn_ref[...], tm_ref[...], mk_ref[...]
        upper = _tri(CHUNK, upper=True)
        lower_strict = 1.0 - upper

        def chunk(cc, carry):
            c = nc - 1 - cc
            r0 = pl.multiple_of(c * CHUNK, CHUNK)
            zc = zc_ref[pl.ds(r0, CHUNK), :]
            cq, cf, q, sg, g, kf, v, e, b, blast = _hgrn_chunk_fwd(zc, lbv, tm)
            eb = jnp.exp(b)
            ebl = jnp.exp(blast - b)
            qe = q * eb
            kd = kf * ebl
            o = o_ref[pl.ds(r0, CHUNK), :]
            dyv = dy_ref[pl.ds(r0, CHUNK), :]
            stp = st_ref[pl.ds(r0, CHUNK), :]
            cg = zc[:, 3 * BW:4 * BW]
            sgate = _silu(cg)
            dq_parts, dk_parts, dv_parts, dcg_parts = [], [], [], []
            dgain, up_parts, lo_parts, const_parts = [], [], [], []
            for h in range(NH):
                hs = _hs(h)
                oh = o[:, hs]
                r = lax.rsqrt(jnp.mean(oh * oh, axis=-1, keepdims=True) + EPS)
                nrm = oh * r
                dyn = dyv[:, hs] * sgate[:, hs]
                dcg_parts.append(dyv[:, hs] * nrm * gn[:, hs] * _dsilu(cg[:, hs]))
                dgain.append(jnp.sum(dyn * nrm, axis=0, keepdims=True))
                tt = dyn * gn[:, hs]
                doh = r * (tt - nrm * jnp.mean(tt * nrm, axis=-1, keepdims=True))
                a, ql, kl = _hgrn_scores(q, kf, e, mk, h)
                da = _dot_nt(doh, v[:, hs])
                dsth = dstate[:, hs]
                ebh = jnp.exp(blast[:, hs])
                dv_parts.append(_dot_tn(a, doh) + _dot_nt(kd[:, hs], dsth))
                dq_inter = eb[:, hs] * _dot(doh, stp[:, hs])
                dk_state = ebl[:, hs] * _dot(v[:, hs], dsth)
                dqh, dkh, gh = dq_inter, dk_state, None
                for lev in range(NLEV + 1):
                    dal = mk[lev] * da
                    xq = _dot(dal, kl[lev][0])
                    yk = _dot_tn(dal, ql[lev][0])
                    gterm = ql[lev][0] * xq - kl[lev][0] * yk
                    gh = gterm if gh is None else gh + gterm
                    dqh = dqh + (xq if lev == NLEV else ql[lev][1] * xq)
                    dkh = dkh + (yk if lev == NLEV else kl[lev][1] * yk)
                up_parts.append(gh + q[:, hs] * dq_inter)
                lo_parts.append(kf[:, hs] * dk_state)
                const_parts.append(jnp.sum(dsth * stp[:, hs], axis=0, keepdims=True) * ebh)
                dstate[:, hs] = dsth * ebh + _dot_tn(doh, qe[:, hs])
                dq_parts.append(dqh)
                dk_parts.append(dkh)
            dq = jnp.concatenate(dq_parts, axis=1)
            dk = jnp.concatenate(dk_parts, axis=1)
            dlg = (_dot_exact_lhs01(upper, jnp.concatenate(up_parts, axis=1))
                   + _dot_exact_lhs01(lower_strict, jnp.concatenate(lo_parts, axis=1))
                   + jnp.concatenate(const_parts, axis=1))
            dsg = sg * (1.0 - sg)
            dz_ref[pl.ds(r0, CHUNK), 0:BW] = _mx(dq * _dsilu(cq))
            dz_ref[pl.ds(r0, CHUNK), BW:2 * BW] = _mx((dlg / g - dk) * (1.0 - lbv) * dsg)
            dz_ref[pl.ds(r0, CHUNK), 2 * BW:3 * BW] = _mx(jnp.concatenate(dv_parts, axis=1))
            dz_ref[pl.ds(r0, CHUNK), 3 * BW:4 * BW] = _mx(jnp.concatenate(dcg_parts, axis=1))
            stat_ref[0:1, :] += jnp.concatenate(dgain, axis=1)
            stat_ref[1:2, :] += jnp.sum((dlg / g - dk) * (1.0 - sg), axis=0, keepdims=True)
            return carry

        lax.fori_loop(0, nc, chunk, 0)

    rev = lambda w, c=0: pl.BlockSpec((ts, w), lambda i: (nt - 1 - i, c))
    one = pl.BlockSpec((1, BW), lambda i: (0, 0))
    return _call(
        body, name="hgrn_bwd", grid=(nt,),
        in_specs=[rev(1024, ZC // 1024), one, one, pl.BlockSpec(tmat.shape, lambda i: (0, 0)),
                  pl.BlockSpec(masks.shape, lambda i: (0, 0, 0)), rev(BW), rev(BW), rev(BW, 2), _ANY],
        out_specs=[rev(1024, ZC // 1024), pl.BlockSpec((8, BW), lambda i: (0, 0))],
        out_shape=[jax.ShapeDtypeStruct((S, ZCOLS), MXU_DTYPE), jax.ShapeDtypeStruct((8, BW), F32)],
        scratch=[pltpu.VMEM((CHUNK, BW), F32)],
        sem=("arbitrary",), aliases={8: 0})(z, lb, gain, tmat, masks, o_pre, states, dy, dz)


def _lower_bounds_fwd(lb_logits):
    def body(l_ref, o_ref):
        l = l_ref[...]
        m = jnp.max(l, axis=0, keepdims=True)
        ex = jnp.exp(l - m)
        p = ex / jnp.sum(ex, axis=0, keepdims=True)
        cs = p[0:1, :]
        o_ref[0:1, :] = jnp.clip(cs - p[0:1, :], 0.0, 1.0)
        for d in range(1, DEPTH):
            cs = cs + p[d:d + 1, :]
            o_ref[d:d + 1, :] = jnp.clip(cs - p[0:1, :], 0.0, 1.0)

    full = pl.BlockSpec((DEPTH, BW), lambda: (0, 0))
    return _call(body, name="lower_bounds_fwd", grid=(), in_specs=[full], out_specs=full,
                 out_shape=jax.ShapeDtypeStruct((DEPTH, BW), F32))(lb_logits)


def _lower_bounds_bwd(lb_logits, dlow):
    def body(l_ref, d_ref, o_ref):
        l = l_ref[...]
        m = jnp.max(l, axis=0, keepdims=True)
        ex = jnp.exp(l - m)
        p = ex / jnp.sum(ex, axis=0, keepdims=True)
        dl = d_ref[...]
        cs = p[0:1, :]
        dcs = []
        for d in range(DEPTH):
            if d > 0:
                cs = cs + p[d:d + 1, :]
            val = cs - p[0:1, :]
            dcs.append(jnp.where((val > 0.0) & (val < 1.0), dl[d:d + 1, :], 0.0))
        total = dcs[0] + dcs[1] + dcs[2] + dcs[3]
        dp = []
        for j in range(DEPTH):
            s = dcs[j]
            for d in range(j + 1, DEPTH):
                s = s + dcs[d]
            dp.append(s - total if j == 0 else s)
        inner = p[0:1, :] * dp[0]
        for j in range(1, DEPTH):
            inner = inner + p[j:j + 1, :] * dp[j]
        for j in range(DEPTH):
            o_ref[j:j + 1, :] = p[j:j + 1, :] * (dp[j] - inner)

    full = pl.BlockSpec((DEPTH, BW), lambda: (0, 0))
    return _call(body, name="lower_bounds_bwd", grid=(), in_specs=[full, full], out_specs=full,
                 out_shape=jax.ShapeDtypeStruct((DEPTH, BW), F32))(lb_logits, dlow)


def _sgu_fwd(z, gv, ws, bs):
    S = z.shape[0]
    ts = _tile(S, ROW_TILE)
    nc = ts // SCHUNK

    def body(zd_ref, gv_ref, ws_ref, bs_ref, y_ref):
        gvv, bsv = gv_ref[...], bs_ref[...]
        tril = _tri(SCHUNK)
        for c in range(nc):
            rs = slice(c * SCHUNK, (c + 1) * SCHUNK)
            zd = zd_ref[rs, :]
            for h in range(NH):
                vn = _rms_fwd(zd[:, BW + h * DH:BW + (h + 1) * DH], gvv[:, _hs(h)])
                s = _dot(ws_ref[h] * tril, vn) + bsv[:, h:h + 1]
                y_ref[rs, _hs(h)] = _mx(zd[:, _hs(h)] * s * _silu(zd[:, 2 * BW + h * DH:2 * BW + (h + 1) * DH]))

    return _call(
        body, name="sgu_fwd", grid=(S // ts,),
        in_specs=[pl.BlockSpec((ts, 1024), lambda i: (i, ZD // 1024)), pl.BlockSpec((1, BW), lambda i: (0, 0)),
                  pl.BlockSpec((NH, SCHUNK, SCHUNK), lambda i: (0, 0, 0)), pl.BlockSpec((SCHUNK, 128), lambda i: (0, 0))],
        out_specs=pl.BlockSpec((ts, BW), lambda i: (i, 0)),
        out_shape=jax.ShapeDtypeStruct((S, BW), MXU_DTYPE), sem=("parallel",))(z, gv, ws, bs)


def _sgu_bwd(z, gv, ws, bs, dy, dz):
    S = z.shape[0]
    ts = _tile(S, ROW_TILE)
    nc = ts // SCHUNK

    def body(zd_ref, gv_ref, ws_ref, bs_ref, dy_ref, dz_in, dz_ref, dws_ref, dbs_ref, st_ref):
        @pl.when(pl.program_id(0) == 0)
        def _():
            dws_ref[...] = jnp.zeros_like(dws_ref)
            dbs_ref[...] = jnp.zeros_like(dbs_ref)
            st_ref[...] = jnp.zeros_like(st_ref)

        gvv, bsv = gv_ref[...], bs_ref[...]
        tril = _tri(SCHUNK)
        dz_ref[:, 3 * BW:4 * BW] = jnp.zeros((ts, BW), MXU_DTYPE)
        for c in range(nc):
            rs = slice(c * SCHUNK, (c + 1) * SCHUNK)
            zd = zd_ref[rs, :]
            dyv = dy_ref[rs, :]
            for h in range(NH):
                hs = _hs(h)
                u = zd[:, hs]
                vraw = zd[:, BW + h * DH:BW + (h + 1) * DH]
                gt = zd[:, 2 * BW + h * DH:2 * BW + (h + 1) * DH]
                gvh = gvv[:, hs]
                vn = _rms_fwd(vraw, gvh)
                wm = ws_ref[h] * tril
                s = _dot(wm, vn) + bsv[:, h:h + 1]
                sil = _silu(gt)
                d = dyv[:, hs]
                ds = d * u * sil
                dz_ref[rs, hs] = _mx(d * s * sil)
                dz_ref[rs, 2 * BW + h * DH:2 * BW + (h + 1) * DH] = _mx(d * u * s * _dsilu(gt))
                dws_ref[h] += tril * _dot_nt(ds, vn)
                dbs_ref[:, h:h + 1] += jnp.sum(ds, axis=-1, keepdims=True)
                dvn = _dot_tn(wm, ds)
                dx, dgr = _rms_bwd(vraw, gvh, dvn)
                dz_ref[rs, BW + h * DH:BW + (h + 1) * DH] = _mx(dx)
                st_ref[0:1, hs] += jnp.sum(dgr, axis=0, keepdims=True)

    return _call(
        body, name="sgu_bwd", grid=(S // ts,),
        in_specs=[pl.BlockSpec((ts, 1024), lambda i: (i, ZD // 1024)), pl.BlockSpec((1, BW), lambda i: (0, 0)),
                  pl.BlockSpec((NH, SCHUNK, SCHUNK), lambda i: (0, 0, 0)), pl.BlockSpec((SCHUNK, 128), lambda i: (0, 0)),
                  pl.BlockSpec((ts, BW), lambda i: (i, 3)), _ANY],
        out_specs=[pl.BlockSpec((ts, 1024), lambda i: (i, ZD // 1024)), pl.BlockSpec((NH, SCHUNK, SCHUNK), lambda i: (0, 0, 0)),
                   pl.BlockSpec((SCHUNK, 128), lambda i: (0, 0)), pl.BlockSpec((8, BW), lambda i: (0, 0))],
        out_shape=[jax.ShapeDtypeStruct((S, ZCOLS), MXU_DTYPE), jax.ShapeDtypeStruct((NH, SCHUNK, SCHUNK), F32),
                   jax.ShapeDtypeStruct((SCHUNK, 128), F32), jax.ShapeDtypeStruct((8, BW), F32)],
        sem=("arbitrary",), aliases={5: 0})(z, gv, ws, bs, dy, dz)


def _merge_fwd(x, z, ys, wup, mb, wo):
    S = x.shape[0]
    ts = _tile(S, MERGE_TILE)

    def body(x_ref, zm_ref, ya_ref, yb_ref, yc_ref, yd_ref, wup_ref, mb_ref, wo_ref, x1_ref, mg_ref):
        yrefs = (ya_ref, yb_ref, yc_ref, yd_ref)
        mbv = mb_ref[...]
        merged = None
        for b in range(NBR):
            cs = slice(b * D_MODEL, (b + 1) * D_MODEL)
            term = _sigmoid(zm_ref[:, cs] + mbv[b:b + 1, :]) * jnp.dot(yrefs[b][...], wup_ref[b],
                                                                      preferred_element_type=F32)
            merged = term if merged is None else merged + term
        mg = _mx(merged)
        mg_ref[...] = mg
        x1_ref[...] = x_ref[...] + jnp.dot(mg, wo_ref[...], preferred_element_type=F32)

    row = lambda w: pl.BlockSpec((ts, w), lambda i: (i, 0))
    return _call(
        body, name="merge_fwd", grid=(S // ts,),
        in_specs=[row(D_MODEL), pl.BlockSpec((ts, 4096), lambda i: (i, 0)), row(BW), row(BW), row(BW), row(BW),
                  pl.BlockSpec((NBR, BW, D_MODEL), lambda i: (0, 0, 0)), pl.BlockSpec((NBR, D_MODEL), lambda i: (0, 0)),
                  pl.BlockSpec((D_MODEL, D_MODEL), lambda i: (0, 0))],
        out_specs=[row(D_MODEL), row(D_MODEL)],
        out_shape=[jax.ShapeDtypeStruct((S, D_MODEL), F32), jax.ShapeDtypeStruct((S, D_MODEL), MXU_DTYPE)],
        sem=("parallel",))(x, z, *ys, wup, mb, wo)


def _merge_bwd(dx1, z, ys, wup, mb, wo):
    S = dx1.shape[0]
    ts = _tile(S, MERGE_TILE)

    def body(dx_ref, zm_ref, ya_ref, yb_ref, yc_ref, yd_ref, wup_ref, mb_ref, wo_ref,
             dzm_ref, du_ref, dxb_ref, dy_ref, st_ref):
        @pl.when(pl.program_id(0) == 0)
        def _():
            st_ref[...] = jnp.zeros_like(st_ref)

        yrefs = (ya_ref, yb_ref, yc_ref, yd_ref)
        mbv = mb_ref[...]
        dxb = _mx(dx_ref[...])
        dxb_ref[...] = dxb
        dmerged = _dot_nt(dxb, wo_ref[...])
        for b in range(NBR):
            cs = slice(b * D_MODEL, (b + 1) * D_MODEL)
            u = jnp.dot(yrefs[b][...], wup_ref[b], preferred_element_type=F32)
            sg = _sigmoid(zm_ref[:, cs] + mbv[b:b + 1, :])
            du = _mx(dmerged * sg)
            du_ref[:, cs] = du
            dzm = dmerged * u * sg * (1.0 - sg)
            dzm_ref[:, cs] = _mx(dzm)
            st_ref[b:b + 1, :] += jnp.sum(dzm, axis=0, keepdims=True)
            dy_ref[:, b * BW:(b + 1) * BW] = _dot_nt(du, wup_ref[b])

    row = lambda w: pl.BlockSpec((ts, w), lambda i: (i, 0))
    return _call(
        body, name="merge_bwd", grid=(S // ts,),
        in_specs=[row(D_MODEL), pl.BlockSpec((ts, 4096), lambda i: (i, 0)), row(BW), row(BW), row(BW), row(BW),
                  pl.BlockSpec((NBR, BW, D_MODEL), lambda i: (0, 0, 0)),
                  pl.BlockSpec((NBR, D_MODEL), lambda i: (0, 0)), pl.BlockSpec((D_MODEL, D_MODEL), lambda i: (0, 0))],
        out_specs=[row(4096), row(4096), row(D_MODEL), row(D_MODEL), pl.BlockSpec((8, D_MODEL), lambda i: (0, 0))],
        out_shape=[jax.ShapeDtypeStruct((S, ZCOLS), MXU_DTYPE), jax.ShapeDtypeStruct((S, 4096), MXU_DTYPE),
                   jax.ShapeDtypeStruct((S, D_MODEL), MXU_DTYPE), jax.ShapeDtypeStruct((S, D_MODEL), F32),
                   jax.ShapeDtypeStruct((8, D_MODEL), F32)],
        sem=("arbitrary",))(dx1, z, *ys, wup, mb, wo)


def _ple_fwd(x1, p, g, wg, wp):
    S = x1.shape[0]
    ts = _tile(S, ROW_TILE)

    def body(x_ref, p_ref, g_ref, wg_ref, wp_ref, o_ref):
        x = x_ref[...]
        hp = _mx(_rms_fwd(x, g_ref[...]))
        gate = _sigmoid(jnp.dot(hp, wg_ref[...], preferred_element_type=F32))
        pp = jnp.dot(_mx(p_ref[...]), wp_ref[...], preferred_element_type=F32)
        o_ref[...] = x + gate * pp

    row = lambda w: pl.BlockSpec((ts, w), lambda i: (i, 0))
    return _call(
        body, name="ple_fwd", grid=(S // ts,),
        in_specs=[row(D_MODEL), row(PLE), pl.BlockSpec((1, D_MODEL), lambda i: (0, 0)),
                  pl.BlockSpec((D_MODEL, D_MODEL), lambda i: (0, 0)), pl.BlockSpec((PLE, D_MODEL), lambda i: (0, 0))],
        out_specs=row(D_MODEL), out_shape=jax.ShapeDtypeStruct((S, D_MODEL), F32),
        sem=("parallel",))(x1, p, g, wg, wp)


def _ple_bwd(x1, p, dx2, g, wg, wp):
    S = x1.shape[0]
    ts = _tile(S, ROW_TILE)

    def body(x_ref, p_ref, dx_ref, g_ref, wg_ref, wp_ref, dx1_ref, hp_ref, dgl_ref, dpp_ref, pb_ref, st_ref):
        @pl.when(pl.program_id(0) == 0)
        def _():
            st_ref[...] = jnp.zeros_like(st_ref)

        x, gv, dx2 = x_ref[...], g_ref[...], dx_ref[...]
        hp = _mx(_rms_fwd(x, gv))
        hp_ref[...] = hp
        gate = _sigmoid(jnp.dot(hp, wg_ref[...], preferred_element_type=F32))
        pb = _mx(p_ref[...])
        pb_ref[...] = pb
        pp = jnp.dot(pb, wp_ref[...], preferred_element_type=F32)
        dgl = _mx(dx2 * pp * gate * (1.0 - gate))
        dgl_ref[...] = dgl
        dpp_ref[...] = _mx(dx2 * gate)
        dhp = _dot_nt(dgl, wg_ref[...])
        dxn, dgr = _rms_bwd(x, gv, dhp)
        dx1_ref[...] = dx2 + dxn
        st_ref[0:1, :] += jnp.sum(dgr, axis=0, keepdims=True)

    row = lambda w: pl.BlockSpec((ts, w), lambda i: (i, 0))
    sq = pl.BlockSpec((D_MODEL, D_MODEL), lambda i: (0, 0))
    return _call(
        body, name="ple_bwd", grid=(S // ts,),
        in_specs=[row(D_MODEL), row(PLE), row(D_MODEL), pl.BlockSpec((1, D_MODEL), lambda i: (0, 0)), sq,
                  pl.BlockSpec((PLE, D_MODEL), lambda i: (0, 0))],
        out_specs=[row(D_MODEL), row(D_MODEL), row(D_MODEL), row(D_MODEL), row(PLE),
                   pl.BlockSpec((8, D_MODEL), lambda i: (0, 0))],
        out_shape=[jax.ShapeDtypeStruct((S, D_MODEL), F32)] + [jax.ShapeDtypeStruct((S, D_MODEL), MXU_DTYPE)] * 3
        + [jax.ShapeDtypeStruct((S, PLE), MXU_DTYPE), jax.ShapeDtypeStruct((8, D_MODEL), F32)],
        sem=("arbitrary",))(x1, p, dx2, g, wg, wp)


def _pad_rows(a, rows=8):
    return jnp.concatenate([a, jnp.zeros((rows - a.shape[0],) + a.shape[1:], a.dtype)], axis=0)


def _pad_lanes(a, lanes=128):
    return jnp.concatenate([a, jnp.zeros(a.shape[:-1] + (lanes - a.shape[-1],), a.dtype)], axis=-1)


def _wz_from_w_in(w):
    zeros = lambda n: jnp.zeros((w.shape[0], n), w.dtype)
    return jnp.concatenate([w[:, _OM:_OEND], w[:, _OA:_OB], w[:, _OB:_OF], w[:, _OC:_OD], w[:, _OD:_OM], zeros(256),
                            w[:, _OF:_OC], zeros(124)], axis=1)


def _w_in_from_wz(g):
    return jnp.concatenate([g[:, ZA:ZB], g[:, ZB:ZC], g[:, ZF:ZF + 4], g[:, ZC:ZD], g[:, ZD:ZD + 768], g[:, ZM:ZA]], axis=1)


_W_IN_GROUPS = [(_OA, _OB, ZA), (_OB, _OF, ZB), (_OF, _OC, ZF), (_OC, _OD, ZC), (_OD, _OM, ZD), (_OM, _OEND, ZM)]


def _wz_from_shards(g):
    n = g.shape[-1]
    pieces, pos = [], 0
    for a, b, zs in sorted(_W_IN_GROUPS, key=lambda grp: grp[2]):
        if zs > pos:
            pieces.append(jnp.zeros((g.shape[1], zs - pos), g.dtype))
        for k in range(4):
            lo, hi = max(a, k * n), min(b, (k + 1) * n)
            if lo < hi:
                pieces.append(g[k][:, lo - k * n:hi - k * n])
        pos = zs + (b - a)
    pieces.append(jnp.zeros((g.shape[1], ZCOLS - pos), g.dtype))
    return jnp.concatenate(pieces, axis=1)


def _w_in_slabs_from_wz(g):
    n = _OEND // 4
    slabs = []
    for k in range(4):
        pieces = []
        for a, b, zs in _W_IN_GROUPS:
            lo, hi = max(a, k * n), min(b, (k + 1) * n)
            if lo < hi:
                pieces.append(g[:, zs + lo - a:zs + hi - a])
        slabs.append(jnp.concatenate(pieces, axis=1))
    return jnp.stack(slabs)


def _local_step(x, p, target, wts, dist=None):
    tmat_np, masks_np = _hgrn_consts()
    tmat = jnp.asarray(tmat_np, MXU_DTYPE)
    masks = jnp.asarray(masks_np, F32)
    lower = _lower_bounds_fwd(wts["lb_logits"])
    saved = []
    for li in range(DEPTH):
        big = dist.weights(li) if dist else {n: wts[n][li] for n in _GATHERED}
        wz = _wz_from_shards(big["w_in"]) if big["w_in"].ndim == 3 else _wz_from_w_in(big["w_in"])
        g_mix = wts["norm_mix"][li][None, :]
        h = _norm_fwd(x, g_mix)
        z = _mm_nn(h, wz, F32, name="mm_z", tn=1664)
        cw = _pad_rows(jnp.concatenate([big["conv_w"], wts["conv_b"][li][None, :]], axis=0))
        ya = _conv_fwd(z, cw)
        fb = _pad_lanes(wts["fgate_bias"][li][None, :])
        gq, gk = wts["q_norm"][li][None, :], wts["k_norm"][li][None, :]
        qs, kn, vb, cc, cr = _attn_prep_fwd(z, fb, gq, gk)
        o, lse, yb, got = _flash_fwd(qs, kn, vb, cc, cr, z, comm=dist.fwd_comm(li) if dist else None)
        if dist:
            dist.fwd_done(li, got)
        lb = lower[li][None, :]
        gh = wts["hgrn_norm"][li][None, :]
        o_pre, states, yc = _hgrn_fwd(z, lb, gh, tmat, masks)
        gv = wts["sgu_norm"][li][None, :]
        ws = wts["spatial_w"][li]
        bs = _pad_lanes(wts["spatial_b"][li].T)
        yd = _sgu_fwd(z, gv, ws, bs)
        ys = (ya, yb, yc, yd)
        x1, merged = _merge_fwd(x, z, ys, big["w_up"], big["merge_b"], big["w_o"])
        g_ple = wts["norm_ple"][li][None, :]
        x2 = _ple_fwd(x1, p[li], g_ple, big["w_ple_gate"], big["w_ple_proj"])
        saved.append(dict(x=x, h=h, z=z, wz=wz, cw=cw, fb=fb, gq=gq, gk=gk, qs=qs, kn=kn, vb=vb, cc=cc, cr=cr, o=o, lse=lse,
                          lb=lb, gh=gh, o_pre=o_pre, states=states, gv=gv, ws=ws, bs=bs, ys=ys, x1=x1, merged=merged,
                          g_mix=g_mix, g_ple=g_ple, big=big))
        x = x2

    loss, dx = _loss_fwd_bwd(x, target)

    names = ["norm_mix", "w_in", "conv_w", "conv_b", "fgate_bias", "q_norm", "k_norm", "lb", "hgrn_norm", "sgu_norm",
             "spatial_w", "spatial_b", "w_up", "merge_b", "w_o", "norm_ple", "w_ple_gate", "w_ple_proj"]
    gl = {n: [None] * DEPTH for n in names}
    for li in reversed(range(DEPTH)):
        s = saved[li]
        z, big = s["z"], s["big"]
        wg, wp = big["w_ple_gate"], big["w_ple_proj"]
        dx1, hp, dgl, dpp, pb, st = _ple_bwd(s["x1"], p[li], dx, s["g_ple"], wg, wp)
        gl["norm_ple"][li] = st[0]
        gl["w_ple_gate"][li] = _mm_tn(hp, dgl, name="mm_dwg")
        gl["w_ple_proj"][li] = _mm_tn(pb, dpp, name="mm_dwp")
        wup, wo = big["w_up"], big["w_o"]
        dz, du, dxb, dy, st = _merge_bwd(dx1, z, s["ys"], wup, big["merge_b"], wo)
        gl["merge_b"][li] = st[0:NBR]
        gl["w_o"][li] = _mm_tn(s["merged"], dxb, name="mm_dwo")
        gl["w_up"][li] = jnp.stack([_mm_tn(s["ys"][b], du, name="mm_dwup", ycol=b, n=D_MODEL) for b in range(NBR)])
        dz, st = _conv_bwd(z, s["cw"], dy, dz)
        gl["conv_w"][li] = st[0:3]
        gl["conv_b"][li] = st[3]
        do, dgate = _attn_gate_bwd(dy, s["o"], z)
        fa = (s["qs"], s["kn"], s["vb"], s["cc"], s["cr"], s["lse"], s["o"], do)
        dq, drow, got = _flash_bwd_dq(*fa, comm=dist.bwd_comm() if dist else None)
        if dist:
            dist.bwd_done(got)
        dk, dv, dcc = _flash_bwd_dkv(s["qs"], s["kn"], s["vb"], s["cc"], s["cr"], s["lse"][:, 0:8].T, do, drow[:, 0:8].T)
        dz, dzf, st = _attn_prep_bwd(z, s["fb"], s["gq"], s["gk"], dq, dk, dv, dgate, dcc, dz)
        dz = _put_fgate_cols(dzf, dz)
        gl["q_norm"][li] = st[0, 0:DH]
        gl["k_norm"][li] = st[1, 0:DH]
        gl["fgate_bias"][li] = st[2, 0:NH]
        dz, st = _hgrn_bwd(z, s["lb"], s["gh"], tmat, masks, s["o_pre"], s["states"], dy, dz)
        gl["hgrn_norm"][li] = st[0]
        gl["lb"][li] = st[1]
        dz, dws, dbs, st = _sgu_bwd(z, s["gv"], s["ws"], s["bs"], dy, dz)
        gl["sgu_norm"][li] = st[0]
        gl["spatial_w"][li] = dws
        gl["spatial_b"][li] = dbs[:, 0:NH].T
        dwz = _mm_tn(s["h"], dz, name="mm_dwz", tn=1664)
        gl["w_in"][li] = dwz if dist else _w_in_from_wz(dwz)
        dh = _mm_nn(dz, s["wz"], F32, name="mm_dh", tk=1664, b_transposed=True)
        dx, st = _norm_bwd(s["x"], s["g_mix"], dh, dx1)
        gl["norm_mix"][li] = st[0]
        if dist:
            dist.push(li, {n: gl[n][li] for n in _BIG})

    if dist:
        dist.finish()
    grads = {n: jnp.stack(v) for n, v in gl.items() if not (dist and n in _BIG)}
    grads["lb_logits"] = _lower_bounds_bwd(wts["lb_logits"], grads.pop("lb"))
    return loss, dx, grads


def _my_pos():
    return lax.axis_index("x"), lax.axis_index("y"), lax.axis_index("c")


def _gather_phase(phase, ins, outs, sems):
    if phase == 1:
        return
    send, recv, lsem = sems
    x, y, c = _my_pos()
    me = 2 * x + y
    peers = [(1 - x, y), (x, 1 - y), (1 - x, 1 - y)]
    copies = []
    for t in range(len(ins)):
        copies.append(pltpu.make_async_copy(ins[t], outs[t].at[me], lsem.at[t]))
        for j, (px, py) in enumerate(peers):
            copies.append(pltpu.make_async_remote_copy(
                src_ref=ins[t], dst_ref=outs[t].at[me], send_sem=send.at[t, j], recv_sem=recv.at[t, j],
                device_id=(px, py, c), device_id_type=MESH))
    for cp in copies:
        if phase == 0:
            cp.start()
        else:
            cp.wait()


def _exchange_phase(phase, ins, outs, sems):
    send, recv, lsem = sems
    x, y, c = _my_pos()
    me = 2 * x + y
    sib = (x, y, 1 - c)
    chips = [(1 - x, y), (x, 1 - y), (1 - x, 1 - y)]

    def rc(t, k, src, dst, dev):
        return pltpu.make_async_remote_copy(src_ref=src, dst_ref=dst, send_sem=send.at[t, k], recv_sem=recv.at[t, k],
                                            device_id=dev, device_id_type=MESH)

    for t in range(len(ins)):
        local = pltpu.make_async_copy(ins[t].at[me], outs[t].at[2 * me + c], lsem.at[t])
        first = [rc(t, 0, ins[t].at[me], outs[t].at[2 * me + c], sib)]
        first += [rc(t, 1 + j, ins[t].at[2 * px + py], outs[t].at[2 * me + c], (px, py, c))
                  for j, (px, py) in enumerate(chips)]
        landed = [outs[t].at[2 * (2 * px + py) + c] for px, py in chips]
        passed = [rc(t, 4 + j, slot, slot, sib) for j, slot in enumerate(landed)]
        if phase == 0:
            local.start()
            for cp in first:
                cp.start()
        elif phase == 1:
            for j, slot in enumerate(landed):
                rc(t, 1 + j, slot, slot, (x, y, c)).wait_recv()
                passed[j].start()
        else:
            s0 = outs[t].at[2 * me + (1 - c)]
            rc(t, 0, s0, s0, (x, y, c)).wait_recv()
            for j, (px, py) in enumerate(chips):
                slot = outs[t].at[2 * (2 * px + py) + (1 - c)]
                rc(t, 4 + j, slot, slot, (x, y, c)).wait_recv()
            for cp in first + passed:
                cp.wait_send()
            local.wait()


_COMM_PHASES = {"gather": _gather_phase, "exchange": _exchange_phase}
_COMM_OUT = {"gather": lambda arrays: [jax.ShapeDtypeStruct((4,) + a.shape, a.dtype) for a in arrays],
             "exchange": lambda arrays: [jax.ShapeDtypeStruct((8,) + a.shape[1:], a.dtype) for a in arrays]}


def _comm_sems(kind, nc):
    k = 3 if kind == "gather" else 7
    return [pltpu.SemaphoreType.DMA((nc, k)), pltpu.SemaphoreType.DMA((nc, k)), pltpu.SemaphoreType.DMA((nc,))]


def _comm_alone(kind, arrays, name):
    nc = len(arrays)

    def body(*refs):
        for phase in range(3):
            _COMM_PHASES[kind](phase, refs[:nc], refs[nc:2 * nc], refs[2 * nc:])

    hbm = pl.BlockSpec(memory_space=pl.ANY)
    return pl.pallas_call(
        functools.partial(body), name=name, in_specs=[hbm] * nc, out_specs=[hbm] * nc,
        out_shape=_COMM_OUT[kind](arrays), scratch_shapes=_comm_sems(kind, nc))(*arrays)


def _allreduce_small(v):
    R = v.shape[0]

    def body(v_ref, o_ref, buf, send, recv):
        x, y, c = _my_pos()
        me = 4 * x + 2 * y + c
        buf[me] = v_ref[...]
        copies = []
        k = 0
        for dx in range(2):
            for dy in range(2):
                for dc in range(2):
                    if dx == 0 and dy == 0 and dc == 0:
                        continue
                    cp = pltpu.make_async_remote_copy(
                        src_ref=v_ref, dst_ref=buf.at[me], send_sem=send.at[k - 0], recv_sem=recv.at[k],
                        device_id=(jnp.bitwise_xor(x, dx), jnp.bitwise_xor(y, dy), jnp.bitwise_xor(c, dc)),
                        device_id_type=MESH)
                    cp.start()
                    copies.append(cp)
                    k += 1
        for cp in copies:
            cp.wait()
        acc = buf[0]
        for d in range(1, 8):
            acc = acc + buf[d]
        o_ref[...] = acc

    vm = pl.BlockSpec(memory_space=pltpu.VMEM)
    return pl.pallas_call(
        functools.partial(body), name="allreduce_small", in_specs=[vm], out_specs=vm,
        out_shape=jax.ShapeDtypeStruct((R, 128), F32),
        scratch_shapes=[pltpu.VMEM((8, R, 128), F32), pltpu.SemaphoreType.DMA((7,)), pltpu.SemaphoreType.DMA((7,))],
        compiler_params=pltpu.CompilerParams(vmem_limit_bytes=VMEM_LIMIT))(v)


def _adamw(w, m, v, parts, nparts):
    A, R, C = w.shape
    per_layer = isinstance(parts, (list, tuple))
    tr = _tile(R, 64) if per_layer else (R if R <= 128 else _tile(R, 128))
    nparr = len(parts) if per_layer else 1

    def body(*refs):
        w_ref, m_ref, v_ref = refs[:3]
        p_refs = refs[3:3 + nparr]
        g_ref, d_ref, nm_ref, nv_ref = refs[3 + nparr:]

        def update(read):
            g = read(0)
            for k in range(1, nparts):
                g = g + read(k)
            mm = ADAM_B1 * m_ref[0] + (1.0 - ADAM_B1) * g
            vv = ADAM_B2 * v_ref[0] + (1.0 - ADAM_B2) * jnp.square(g)
            m_hat = mm / (1.0 - ADAM_B1 ** ADAM_STEP)
            v_hat = vv / (1.0 - ADAM_B2 ** ADAM_STEP)
            g_ref[0] = g
            d_ref[0] = -ADAM_LR * (m_hat / (jnp.sqrt(v_hat) + ADAM_EPS) + ADAM_WD * w_ref[0])
            nm_ref[0] = mm
            nv_ref[0] = vv

        if not per_layer:
            update(lambda k: p_refs[0][k, 0])
        else:
            for a in range(A):
                @pl.when(pl.program_id(0) == a)
                def _(a=a):
                    update(lambda k: p_refs[a][k])

    blk = pl.BlockSpec((1, tr, C), lambda a, r: (a, r, 0))
    if per_layer:
        pspecs = [pl.BlockSpec((nparts, tr, C), functools.partial(lambda a, r, l: (0, jnp.where(a == l, r, 0), 0), l=l))
                  for l in range(A)]
        pargs = list(parts)
    else:
        pspecs = [pl.BlockSpec((nparts, 1, tr, C), lambda a, r: (0, a, r, 0))]
        pargs = [parts]
    return _call(
        body, name="adamw", grid=(A, R // tr), in_specs=[blk, blk, blk] + pspecs,
        out_specs=[blk] * 4, out_shape=[jax.ShapeDtypeStruct((A, R, C), F32)] * 4,
        sem=("arbitrary", "arbitrary"))(w, m, v, *pargs)


def _as3d(a):
    if a.ndim == 2:
        return a[None]
    if a.ndim == 3:
        return a
    return a.reshape((-1,) + a.shape[-2:])


_WEIGHTS = ["norm_mix", "w_in", "conv_w", "conv_b", "fgate_bias", "q_norm", "k_norm", "lb_logits", "hgrn_norm", "sgu_norm",
            "spatial_w", "spatial_b", "w_up", "merge_b", "w_o", "norm_ple", "w_ple_gate", "w_ple_proj"]
_BIG = ["w_in", "w_up", "w_o", "w_ple_gate", "w_ple_proj"]
_GATHERED = _BIG + ["conv_w", "merge_b"]
_SHARD_AXIS = {"w_in": 2, "w_up": 3, "w_o": 1, "w_ple_gate": 1, "w_ple_proj": 2, "conv_w": 2, "merge_b": 2}
_SMALL = [n for n in _WEIGHTS if n not in _BIG]


class _Dist:
    def __init__(self, w):
        self.w = w
        self.full = {0: self._unpack(_comm_alone("gather", self._shards(0), "gather_weights"))}
        self.contrib = {}
        self.pending = None

    def _shards(self, l):
        return [_mx(self.w[n][l]) for n in _BIG] + [self.w["conv_w"][l], self.w["merge_b"][l]]

    def _unpack(self, gathered):
        return {n: g if n == "w_in" else jnp.concatenate([g[k] for k in range(4)], axis=_SHARD_AXIS[n] - 1)
                for n, g in zip(_GATHERED, gathered)}

    def weights(self, l):
        return self.full[l]

    def fwd_comm(self, l):
        return ("gather", self._shards(l + 1)) if l + 1 < DEPTH else None

    def fwd_done(self, l, got):
        if got:
            self.full[l + 1] = self._unpack(got)

    def push(self, l, grads):
        self.pending = (l, [_w_in_slabs_from_wz(grads[n]) if n == "w_in"
                            else jnp.stack(jnp.split(grads[n], 4, axis=_SHARD_AXIS[n] - 1)) for n in _BIG])

    def bwd_comm(self):
        return ("exchange", self.pending[1]) if self.pending else None

    def bwd_done(self, got):
        if got:
            self.contrib[self.pending[0]] = got
            self.pending = None

    def finish(self):
        if self.pending:
            self.contrib[self.pending[0]] = _comm_alone("exchange", self.pending[1], "exchange_grads")
            self.pending = None


def kernel(x, p, norm_mix, w_in, conv_w, conv_b, fgate_bias, q_norm, k_norm, lb_logits, hgrn_norm, sgu_norm, spatial_w, spatial_b, w_up, merge_b, w_o, norm_ple, w_ple_gate, w_ple_proj, loss_target, m_norm_mix, m_w_in, m_conv_w, m_conv_b, m_fgate_bias, m_q_norm, m_k_norm, m_lb_logits, m_hgrn_norm, m_sgu_norm, m_spatial_w, m_spatial_b, m_w_up, m_merge_b, m_w_o, m_norm_ple, m_w_ple_gate, m_w_ple_proj, v_norm_mix, v_w_in, v_conv_w, v_conv_b, v_fgate_bias, v_q_norm, v_k_norm, v_lb_logits, v_hgrn_norm, v_sgu_norm, v_spatial_w, v_spatial_b, v_w_up, v_merge_b, v_w_o, v_norm_ple, v_w_ple_gate, v_w_ple_proj):
    loc = dict(locals())
    w = {n: loc[n] for n in _WEIGHTS}
    m = {n: loc["m_" + n] for n in _WEIGHTS}
    v = {n: loc["v_" + n] for n in _WEIGHTS}
    chip = 2 * lax.axis_index("x") + lax.axis_index("y")

    dist = _Dist(w)
    loss_blk, dx, grads = _local_step(x[0], p[:, 0], loss_target[0], w, dist)
    loss = lax.psum(loss_blk[0, 0], ("x", "y", "c"))

    flat = jnp.concatenate([grads[n].reshape(-1) for n in _SMALL])
    npad = (-flat.shape[0]) % 1024
    packed = jnp.concatenate([flat, jnp.zeros((npad,), F32)]).reshape(-1, 128)
    red = _allreduce_small(packed).reshape(-1)
    small, off = {}, 0
    for n in _SMALL:
        sz = int(np.prod(grads[n].shape))
        small[n] = red[off:off + sz].reshape(grads[n].shape)
        off += sz
    for n in ("conv_w", "merge_b"):
        ax = _SHARD_AXIS[n]
        width = small[n].shape[ax] // 4
        small[n] = lax.dynamic_slice_in_dim(small[n], chip * width, width, axis=ax)

    out_g, out_d, out_m, out_v = {}, {}, {}, {}
    for n in _WEIGHTS:
        shp = w[n].shape
        if n in _BIG:
            w3, m3, v3 = (a.reshape((DEPTH, -1, shp[-1])) for a in (w[n], m[n], v[n]))
            parts = [dist.contrib[l][_BIG.index(n)].reshape((8,) + w3.shape[1:]) for l in range(DEPTH)]
            g, d, nm, nv = _adamw(w3, m3, v3, parts, 8)
        else:
            g, d, nm, nv = _adamw(_as3d(w[n]), _as3d(m[n]), _as3d(v[n]), _as3d(small[n])[None], 1)
        out_g[n], out_d[n], out_m[n], out_v[n] = (a.reshape(shp) for a in (g, d, nm, nv))

    return (loss, dx[None], *[out_g[n] for n in _WEIGHTS], *[out_d[n] for n in _WEIGHTS],
            *[out_m[n] for n in _WEIGHTS], *[out_v[n] for n in _WEIGHTS])
```

```python
import functools

import numpy as np
import jax
import jax.numpy as jnp
from jax import lax
from jax.experimental import pallas as pl
from jax.experimental.pallas import tpu as pltpu

F32 = jnp.float32
MXU_DTYPE = jnp.bfloat16

D_MODEL = 1024
BW = 256
NH = 4
DH = 64
DEPTH = 4
NBR = 4
PLE = 256
CHUNK = 64
SCHUNK = 128
EPS = 1e-6
MASK_VALUE = -1e30
NLEV = 6

ADAM_LR, ADAM_B1, ADAM_B2, ADAM_EPS, ADAM_WD, ADAM_STEP = 0.001, 0.9, 0.999, 1e-08, 0.01, 10

ZM, ZA, ZB, ZC, ZD, ZF = 0, 4096, 5120, 6144, 7168, 8192
ZCOLS = 8320
_OA, _OB, _OF, _OC, _OD, _OM, _OEND = 0, 1024, 2048, 2052, 3076, 3844, 7940

VMEM_LIMIT = 56 * 1024 * 1024
ROW_TILE = 512
FLASH_TILE = 1024
GRAD_WIRE_DTYPE = jnp.bfloat16
MERGE_TILE = 256
MESH = pl.DeviceIdType.MESH
_ANY = pl.BlockSpec(memory_space=pl.ANY)


def _tile(n, pref):
    t = min(n, pref)
    assert n % t == 0, (n, t)
    return t


def _call(body, *, name, grid, in_specs, out_specs, out_shape, scratch=(), sem=None, aliases=None):
    return pl.pallas_call(
        functools.partial(body), name=name, grid=grid, in_specs=in_specs, out_specs=out_specs,
        out_shape=out_shape, scratch_shapes=list(scratch), input_output_aliases=aliases or {},
        compiler_params=pltpu.CompilerParams(dimension_semantics=sem, vmem_limit_bytes=VMEM_LIMIT))


def _mx(x):
    return x.astype(MXU_DTYPE)


def _dot(a, b):
    return jnp.dot(_mx(a), _mx(b), preferred_element_type=F32)


def _dot_nt(a, b):
    return lax.dot_general(_mx(a), _mx(b), (((1,), (1,)), ((), ())), preferred_element_type=F32)


def _dot_tn(a, b):
    return lax.dot_general(_mx(a), _mx(b), (((0,), (0,)), ((), ())), preferred_element_type=F32)


def _top16(x):
    u = lax.bitcast_convert_type(x, jnp.uint32) & jnp.uint32(0xFFFF0000)
    return lax.bitcast_convert_type(u, F32)


def _split3(x):
    hi = _top16(x)
    r1 = x - hi
    mid = _top16(r1)
    return _mx(hi), _mx(mid), _mx(r1 - mid)


def _dot_exact_lhs01(t, x):
    hi, mid, lo = _split3(x)
    t = _mx(t)
    return (jnp.dot(t, hi, preferred_element_type=F32) + jnp.dot(t, mid, preferred_element_type=F32)
            + jnp.dot(t, lo, preferred_element_type=F32))


def _sigmoid(x):
    return jax.nn.sigmoid(x)


def _silu(x):
    return x * _sigmoid(x)


def _dsilu(x):
    s = _sigmoid(x)
    return s * (1.0 + x * (1.0 - s))


def _log_sigmoid(x):
    return jnp.minimum(x, 0.0) - jnp.log(1.0 + jnp.exp(-jnp.abs(x)))


def _rms_fwd(x, g):
    r = lax.rsqrt(jnp.mean(x * x, axis=-1, keepdims=True) + EPS)
    return x * r * g


def _rms_bwd(x, g, dy):
    r = lax.rsqrt(jnp.mean(x * x, axis=-1, keepdims=True) + EPS)
    n = x * r
    t = dy * g
    dx = r * (t - n * jnp.mean(t * n, axis=-1, keepdims=True))
    return dx, dy * n


def _tri(n, upper=False):
    r = lax.broadcasted_iota(jnp.int32, (n, n), 0)
    c = lax.broadcasted_iota(jnp.int32, (n, n), 1)
    return jnp.where((c >= r) if upper else (r >= c), 1.0, 0.0).astype(F32)


def _hs(h):
    return slice(h * DH, (h + 1) * DH)


def _mm_nn(a, b, out_dtype, *, name, tn=None, tk=None, b_transposed=False):
    S, K = a.shape
    N = b.shape[0] if b_transposed else b.shape[1]
    ts = _tile(S, ROW_TILE)
    tn = _tile(N, tn or N)
    tk = _tile(K, tk or K)
    nk = K // tk
    b_spec = (pl.BlockSpec((tn, tk), lambda j, i, k: (j, k)) if b_transposed
              else pl.BlockSpec((tk, tn), lambda j, i, k: (k, j)))

    def body(a_ref, b_ref, o_ref, acc_ref):
        k = pl.program_id(2)
        part = _dot_nt(a_ref[...], b_ref[...]) if b_transposed else jnp.dot(a_ref[...], b_ref[...],
                                                                          preferred_element_type=F32)
        if nk == 1:
            o_ref[...] = part.astype(o_ref.dtype)
        else:
            @pl.when(k == 0)
            def _():
                acc_ref[...] = part

            @pl.when(k > 0)
            def _():
                acc_ref[...] += part

            @pl.when(k == nk - 1)
            def _():
                o_ref[...] = acc_ref[...].astype(o_ref.dtype)

    return _call(
        body, name=name, grid=(N // tn, S // ts, nk),
        in_specs=[pl.BlockSpec((ts, tk), lambda j, i, k: (i, k)), b_spec],
        out_specs=pl.BlockSpec((ts, tn), lambda j, i, k: (i, j)),
        out_shape=jax.ShapeDtypeStruct((S, N), out_dtype),
        scratch=[pltpu.VMEM((ts, tn) if nk > 1 else (8, 128), F32)],
        sem=("parallel", "parallel", "arbitrary"))(a, b)


def _mm_tn(x, y, *, name, tn=None, ycol=0, n=None):
    S, M = x.shape
    n = n or y.shape[1]
    ts = _tile(S, ROW_TILE)
    tn = _tile(n, tn or n)
    nj = n // tn

    def body(x_ref, y_ref, o_ref):
        @pl.when(pl.program_id(1) == 0)
        def _():
            o_ref[...] = jnp.zeros_like(o_ref)

        o_ref[...] += lax.dot_general(x_ref[...], y_ref[...], (((0,), (0,)), ((), ())), preferred_element_type=F32)

    return _call(
        body, name=name, grid=(nj, S // ts),
        in_specs=[pl.BlockSpec((ts, M), lambda j, s: (s, 0)), pl.BlockSpec((ts, tn), lambda j, s: (s, ycol * nj + j))],
        out_specs=pl.BlockSpec((M, tn), lambda j, s: (0, j)),
        out_shape=jax.ShapeDtypeStruct((M, n), F32),
        sem=("parallel", "arbitrary"))(x, y)


def _norm_fwd(x, g):
    S = x.shape[0]
    ts = _tile(S, ROW_TILE)

    def body(x_ref, g_ref, h_ref):
        h_ref[...] = _mx(_rms_fwd(x_ref[...], g_ref[...]))

    return _call(
        body, name="norm_fwd", grid=(S // ts,),
        in_specs=[pl.BlockSpec((ts, D_MODEL), lambda i: (i, 0)), pl.BlockSpec((1, D_MODEL), lambda i: (0, 0))],
        out_specs=pl.BlockSpec((ts, D_MODEL), lambda i: (i, 0)),
        out_shape=jax.ShapeDtypeStruct((S, D_MODEL), MXU_DTYPE), sem=("parallel",))(x, g)


def _norm_bwd(x, g, dh, dres):
    S = x.shape[0]
    ts = _tile(S, ROW_TILE)

    def body(x_ref, g_ref, dh_ref, dr_ref, dx_ref, st_ref):
        @pl.when(pl.program_id(0) == 0)
        def _():
            st_ref[...] = jnp.zeros_like(st_ref)

        dx, dgr = _rms_bwd(x_ref[...], g_ref[...], dh_ref[...])
        dx_ref[...] = dr_ref[...] + dx
        st_ref[0:1, :] += jnp.sum(dgr, axis=0, keepdims=True)

    row = pl.BlockSpec((ts, D_MODEL), lambda i: (i, 0))
    return _call(
        body, name="norm_bwd", grid=(S // ts,),
        in_specs=[row, pl.BlockSpec((1, D_MODEL), lambda i: (0, 0)), row, row],
        out_specs=[row, pl.BlockSpec((8, D_MODEL), lambda i: (0, 0))],
        out_shape=[jax.ShapeDtypeStruct((S, D_MODEL), F32), jax.ShapeDtypeStruct((8, D_MODEL), F32)],
        sem=("arbitrary",))(x, g, dh, dres)


def _loss_fwd_bwd(y, target):
    S = y.shape[0]
    ts = _tile(S, ROW_TILE)

    def body(y_ref, t_ref, l_ref, dy_ref):
        @pl.when(pl.program_id(0) == 0)
        def _():
            l_ref[...] = jnp.zeros_like(l_ref)

        err = y_ref[...] - t_ref[...]
        dy_ref[...] = err * (1.0 / D_MODEL)
        rowloss = jnp.mean(err * err, axis=-1, keepdims=True)
        l_ref[...] += 0.5 * jnp.sum(rowloss, axis=0, keepdims=True)

    row = pl.BlockSpec((ts, D_MODEL), lambda i: (i, 0))
    return _call(
        body, name="loss", grid=(S // ts,), in_specs=[row, row],
        out_specs=[pl.BlockSpec((8, 128), lambda i: (0, 0)), row],
        out_shape=[jax.ShapeDtypeStruct((8, 128), F32), jax.ShapeDtypeStruct((S, D_MODEL), F32)],
        sem=("arbitrary",))(y, target)


def _shift_down(x, k, halo, rows):
    y = pltpu.roll(x, k, 0)
    for j in range(k):
        y = jnp.where(rows == j, halo[8 - k + j:8 - k + j + 1, :], y)
    return y


def _shift_up(x, k, halo, rows, n):
    y = pltpu.roll(x, n - k, 0)
    for j in range(k):
        y = jnp.where(rows == n - k + j, halo[j:j + 1, :], y)
    return y


def _conv_parts(za, zh, cw, first, rows):
    ax, ab, ac, ag = za[:, 0:BW], za[:, BW:2 * BW], za[:, 2 * BW:3 * BW], za[:, 3 * BW:4 * BW]
    zz = ac * ax
    hz = jnp.where(first, 0.0, zh[:, 2 * BW:3 * BW] * zh[:, 0:BW])
    zz1 = _shift_down(zz, 1, hz, rows)
    zz2 = _shift_down(zz, 2, hz, rows)
    conv = zz2 * cw[0:1, :] + zz1 * cw[1:2, :] + zz * cw[2:3, :] + cw[3:4, :]
    return ax, ab, ac, ag, zz, zz1, zz2, conv


def _conv_fwd(z, cw):
    S = z.shape[0]
    ts = _tile(S, ROW_TILE)
    hb = ts // 8

    def body(za_ref, zh_ref, cw_ref, y_ref):
        i = pl.program_id(0)
        rows = lax.broadcasted_iota(jnp.int32, (ts, BW), 0)
        ax, ab, ac, ag, zz, zz1, zz2, conv = _conv_parts(za_ref[...], zh_ref[...], cw_ref[...], i == 0, rows)
        y_ref[...] = _mx(ab * conv * _silu(ag))

    return _call(
        body, name="conv_fwd", grid=(S // ts,),
        in_specs=[pl.BlockSpec((ts, 1024), lambda i: (i, ZA // 1024)),
                  pl.BlockSpec((8, 1024), lambda i: (jnp.maximum(i * hb - 1, 0), ZA // 1024)),
                  pl.BlockSpec((8, BW), lambda i: (0, 0))],
        out_specs=pl.BlockSpec((ts, BW), lambda i: (i, 0)),
        out_shape=jax.ShapeDtypeStruct((S, BW), MXU_DTYPE), sem=("parallel",))(z, z, cw)


def _conv_bwd(z, cw, dy, dz):
    S = z.shape[0]
    ts = _tile(S, ROW_TILE)
    hb = ts // 8
    nt = S // ts

    def body(za_ref, zh_ref, zn_ref, cw_ref, dy_ref, dyn_ref, dz_in, dz_ref, st_ref):
        i = pl.program_id(0)

        @pl.when(i == 0)
        def _():
            st_ref[...] = jnp.zeros_like(st_ref)

        cw = cw_ref[...]
        rows = lax.broadcasted_iota(jnp.int32, (ts, BW), 0)
        ax, ab, ac, ag, zz, zz1, zz2, conv = _conv_parts(za_ref[...], zh_ref[...], cw, i == 0, rows)
        dy = dy_ref[...]
        sg = _silu(ag)
        dc = dy * ab * sg
        zn = zn_ref[...]
        dcn = jnp.where(i == nt - 1, 0.0, dyn_ref[...] * zn[:, BW:2 * BW] * _silu(zn[:, 3 * BW:4 * BW]))
        dc1 = _shift_up(dc, 1, dcn, rows, ts)
        dc2 = _shift_up(dc, 2, dcn, rows, ts)
        dzz = dc * cw[2:3, :] + dc1 * cw[1:2, :] + dc2 * cw[0:1, :]
        dz_ref[:, 0:BW] = _mx(dzz * ac)
        dz_ref[:, BW:2 * BW] = _mx(dy * conv * sg)
        dz_ref[:, 2 * BW:3 * BW] = _mx(dzz * ax)
        dz_ref[:, 3 * BW:4 * BW] = _mx(dy * ab * conv * _dsilu(ag))
        st_ref[0:1, :] += jnp.sum(dc * zz2, axis=0, keepdims=True)
        st_ref[1:2, :] += jnp.sum(dc * zz1, axis=0, keepdims=True)
        st_ref[2:3, :] += jnp.sum(dc * zz, axis=0, keepdims=True)
        st_ref[3:4, :] += jnp.sum(dc, axis=0, keepdims=True)

    return _call(
        body, name="conv_bwd", grid=(nt,),
        in_specs=[pl.BlockSpec((ts, 1024), lambda i: (i, ZA // 1024)),
                  pl.BlockSpec((8, 1024), lambda i: (jnp.maximum(i * hb - 1, 0), ZA // 1024)),
                  pl.BlockSpec((8, 1024), lambda i: (jnp.minimum((i + 1) * hb, S // 8 - 1), ZA // 1024)),
                  pl.BlockSpec((8, BW), lambda i: (0, 0)),
                  pl.BlockSpec((ts, BW), lambda i: (i, 0)),
                  pl.BlockSpec((8, BW), lambda i: (jnp.minimum((i + 1) * hb, S // 8 - 1), 0)), _ANY],
        out_specs=[pl.BlockSpec((ts, 1024), lambda i: (i, ZA // 1024)), pl.BlockSpec((8, BW), lambda i: (0, 0))],
        out_shape=[jax.ShapeDtypeStruct((S, ZCOLS), MXU_DTYPE), jax.ShapeDtypeStruct((8, BW), F32)],
        sem=("arbitrary",), aliases={6: 0})(z, z, z, cw, dy, dy, dz)


def _attn_prep_fwd(z, fb, gq, gk):
    S = z.shape[0]
    ts = _tile(S, ROW_TILE)

    def body(zb_ref, zf_ref, fb_ref, gq_ref, gk_ref, q_ref, k_ref, v_ref, cc_ref, cr_ref, carry):
        @pl.when(pl.program_id(0) == 0)
        def _():
            carry[...] = jnp.zeros_like(carry)

        zb = zb_ref[...]
        gqv, gkv = gq_ref[...], gk_ref[...]
        for h in range(NH):
            q_ref[:, _hs(h)] = _mx(_rms_fwd(zb[:, _hs(h)], gqv) * (DH ** -0.5))
            k_ref[:, _hs(h)] = _mx(_rms_fwd(zb[:, BW + h * DH:BW + (h + 1) * DH], gkv))
        v_ref[...] = _mx(zb[:, 2 * BW:3 * BW])
        lf = _log_sigmoid(zf_ref[...] + fb_ref[...])
        cum = _dot_exact_lhs01(_tri(ts), lf) + carry[...]
        carry[...] = cum[ts - 1:ts, :]
        cc_ref[...] = cum
        cr_ref[...] = cum.T[0:8, :]

    row = lambda w: pl.BlockSpec((ts, w), lambda i: (i, 0))
    return _call(
        body, name="attn_prep_fwd", grid=(S // ts,),
        in_specs=[pl.BlockSpec((ts, 1024), lambda i: (i, ZB // 1024)), pl.BlockSpec((ts, 128), lambda i: (i, ZF // 128)),
                  pl.BlockSpec((1, 128), lambda i: (0, 0)), pl.BlockSpec((1, DH), lambda i: (0, 0)),
                  pl.BlockSpec((1, DH), lambda i: (0, 0))],
        out_specs=[row(BW), row(BW), row(BW), row(128), pl.BlockSpec((8, ts), lambda i: (0, i))],
        out_shape=[jax.ShapeDtypeStruct((S, BW), MXU_DTYPE)] * 3
        + [jax.ShapeDtypeStruct((S, 128), F32), jax.ShapeDtypeStruct((8, S), F32)],
        scratch=[pltpu.VMEM((1, 128), F32)], sem=("arbitrary",))(z, z, fb, gq, gk)


ROW_CHUNK = 256


def _fuse_comm(core, n_in, n_out, comm, n):
    nc = 0 if comm is None else len(comm[1])

    def body(*refs):
        cin, xin = refs[:n_in], refs[n_in:n_in + nc]
        a = n_in + nc
        cout, xout = refs[a:a + n_out], refs[a + n_out:a + n_out + nc]
        rest = refs[a + n_out + nc:]
        if nc == 0:
            core(*cin, *cout, *rest)
            return
        cscr, sems = rest[:-3], rest[-3:]
        i, j = pl.program_id(0), pl.program_id(1)
        phase = _COMM_PHASES[comm[0]]

        @pl.when((i == 0) & (j == 0))
        def _():
            phase(0, xin, xout, sems)

        core(*cin, *cout, *cscr)

        @pl.when((i == n - 1) & (j == 0))
        def _():
            phase(1, xin, xout, sems)

        @pl.when((i == n - 1) & (j == n - 1))
        def _():
            phase(2, xin, xout, sems)

    return body


def _comm_specs(comm):
    if comm is None:
        return [], [], [], []
    hbm = pl.BlockSpec(memory_space=pl.ANY)
    nc = len(comm[1])
    return [hbm] * nc, [hbm] * nc, _COMM_OUT[comm[0]](comm[1]), _comm_sems(comm[0], nc)


def _flash_fwd(q, k, v, cc, cr, z, comm=None):
    S = q.shape[0]
    t = _tile(S, FLASH_TILE)
    n = S // t

    rch = _tile(t, ROW_CHUNK)

    def core(q_ref, k_ref, v_ref, cc_ref, cr_ref, zb_ref, o_ref, lse_ref, y_ref, m_sc, l_sc, acc):
        i, j = pl.program_id(0), pl.program_id(1)

        @pl.when(j == 0)
        def _():
            m_sc[...] = jnp.full_like(m_sc, MASK_VALUE)
            l_sc[...] = jnp.zeros_like(l_sc)
            acc[...] = jnp.zeros_like(acc)

        def block(masked):
            for h in range(NH):
                for rc in range(t // rch):
                    rows = slice(rc * rch, (rc + 1) * rch)
                    s = (_dot_nt(q_ref[rows, _hs(h)], k_ref[:, _hs(h)])
                         + (cc_ref[rows, h:h + 1] - cr_ref[h:h + 1, :]))
                    if masked:
                        qpos = rc * rch + lax.broadcasted_iota(jnp.int32, (rch, t), 0)
                        s = jnp.where(qpos >= lax.broadcasted_iota(jnp.int32, (rch, t), 1), s, MASK_VALUE)
                    m_old = m_sc[h, rows, :]
                    m_new = jnp.maximum(m_old, jnp.max(s, axis=-1, keepdims=True))
                    p = jnp.exp(s - m_new)
                    alpha = jnp.exp(m_old - m_new)
                    l_sc[h, rows, :] = alpha * l_sc[h, rows, :] + jnp.sum(p, axis=-1, keepdims=True)
                    acc[rows, _hs(h)] = alpha * acc[rows, _hs(h)] + _dot(p, v_ref[:, _hs(h)])
                    m_sc[h, rows, :] = m_new

        @pl.when(j < i)
        def _():
            block(False)

        @pl.when(j == i)
        def _():
            block(True)
            lse_ref[...] = jnp.zeros_like(lse_ref)
            for h in range(NH):
                o_ref[:, _hs(h)] = acc[:, _hs(h)] / l_sc[h]
                lse_ref[:, h:h + 1] = m_sc[h] + jnp.log(l_sc[h])
            y_ref[...] = _mx(o_ref[...] * _silu(zb_ref[:, 3 * BW:4 * BW]))

    qspec = lambda w: pl.BlockSpec((t, w), lambda i, j: (i, 0))
    kspec = lambda w: pl.BlockSpec((t, w), lambda i, j: (jnp.minimum(j, i), 0))
    xin, xout, xshape, xsem = _comm_specs(comm)
    res = _call(
        _fuse_comm(core, 6, 3, comm, n), name="flash_fwd", grid=(n, n),
        in_specs=[qspec(BW), kspec(BW), kspec(BW), qspec(128), pl.BlockSpec((8, t), lambda i, j: (0, jnp.minimum(j, i))),
                  pl.BlockSpec((t, 1024), lambda i, j: (i, ZB // 1024))] + xin,
        out_specs=[qspec(BW), qspec(128), qspec(BW)] + xout,
        out_shape=[jax.ShapeDtypeStruct((S, BW), F32), jax.ShapeDtypeStruct((S, 128), F32),
                   jax.ShapeDtypeStruct((S, BW), MXU_DTYPE)] + xshape,
        scratch=[pltpu.VMEM((NH, t, 1), F32), pltpu.VMEM((NH, t, 1), F32), pltpu.VMEM((t, BW), F32)] + xsem,
        sem=("arbitrary", "arbitrary"))(q, k, v, cc, cr, z, *(comm[1] if comm else []))
    return res[0], res[1], res[2], list(res[3:])


def _attn_gate_bwd(dy, o, z):
    S = dy.shape[0]
    ts = _tile(S, ROW_TILE)

    def body(dy_ref, o_ref, zb_ref, do_ref, dg_ref):
        g = zb_ref[:, 3 * BW:4 * BW]
        dy = dy_ref[...]
        do_ref[...] = dy * _silu(g)
        dg_ref[...] = _mx(dy * o_ref[...] * _dsilu(g))

    row = pl.BlockSpec((ts, BW), lambda i: (i, 0))
    return _call(
        body, name="attn_gate_bwd", grid=(S // ts,),
        in_specs=[pl.BlockSpec((ts, BW), lambda i: (i, 1)), row, pl.BlockSpec((ts, 1024), lambda i: (i, ZB // 1024))],
        out_specs=[row, row],
        out_shape=[jax.ShapeDtypeStruct((S, BW), F32), jax.ShapeDtypeStruct((S, BW), MXU_DTYPE)],
        sem=("parallel",))(dy, o, z)


def _flash_probs(refs, h, rows, masked):
    q_ref, k_ref, v_ref, cc_ref, cr_ref, lse_ref, o_ref, do_ref = refs
    hs = _hs(h)
    s = _dot_nt(q_ref[rows, hs], k_ref[:, hs]) + (cc_ref[rows, h:h + 1] - cr_ref[h:h + 1, :])
    p = jnp.exp(s - lse_ref[rows, h:h + 1])
    if masked:
        qpos = rows.start + lax.broadcasted_iota(jnp.int32, s.shape, 0)
        p = jnp.where(qpos >= lax.broadcasted_iota(jnp.int32, s.shape, 1), p, 0.0)
    doh = do_ref[rows, hs]
    dp = _dot_nt(doh, v_ref[:, hs])
    delta = jnp.sum(doh * o_ref[rows, hs], axis=-1, keepdims=True)
    return p, p * (dp - delta), doh


def _flash_bwd_dq(q, k, v, cc, cr, lse, o, do, comm=None):
    S = q.shape[0]
    t = _tile(S, FLASH_TILE)
    n = S // t
    rch = _tile(t, ROW_CHUNK)

    def core(q_ref, k_ref, v_ref, cc_ref, cr_ref, lse_ref, o_ref, do_ref, dq_ref, dr_ref):
        i, j = pl.program_id(0), pl.program_id(1)

        @pl.when(j == 0)
        def _():
            dq_ref[...] = jnp.zeros_like(dq_ref)
            dr_ref[...] = jnp.zeros_like(dr_ref)

        def block(masked):
            refs = (q_ref, k_ref, v_ref, cc_ref, cr_ref, lse_ref, o_ref, do_ref)
            for h in range(NH):
                for rc in range(t // rch):
                    rows = slice(rc * rch, (rc + 1) * rch)
                    p, ds, doh = _flash_probs(refs, h, rows, masked)
                    dq_ref[rows, _hs(h)] += _dot(ds, k_ref[:, _hs(h)])
                    dr_ref[rows, h:h + 1] += jnp.sum(ds, axis=-1, keepdims=True)

        @pl.when(j < i)
        def _():
            block(False)

        @pl.when(j == i)
        def _():
            block(True)
            for h in range(NH):
                dr_ref[:, h:h + 1] += jnp.sum(do_ref[:, _hs(h)] * o_ref[:, _hs(h)], axis=-1, keepdims=True)

    qspec = lambda w: pl.BlockSpec((t, w), lambda i, j: (i, 0))
    kspec = lambda w: pl.BlockSpec((t, w), lambda i, j: (jnp.minimum(j, i), 0))
    xin, xout, xshape, xsem = _comm_specs(comm)
    res = _call(
        _fuse_comm(core, 8, 2, comm, n), name="flash_bwd_dq", grid=(n, n),
        in_specs=[qspec(BW), kspec(BW), kspec(BW), qspec(128), pl.BlockSpec((8, t), lambda i, j: (0, jnp.minimum(j, i))),
                  qspec(128), qspec(BW), qspec(BW)] + xin,
        out_specs=[qspec(BW), qspec(128)] + xout,
        out_shape=[jax.ShapeDtypeStruct((S, BW), F32), jax.ShapeDtypeStruct((S, 128), F32)] + xshape,
        scratch=xsem,
        sem=("arbitrary", "arbitrary"))(q, k, v, cc, cr, lse, o, do, *(comm[1] if comm else []))
    return res[0], res[1], list(res[2:])


def _flash_bwd_dkv(q, k, v, cc, cr, lse_rows, do, delta_rows):
    S = q.shape[0]
    t = _tile(S, FLASH_TILE)
    n = S // t

    def body(q_ref, k_ref, v_ref, cck_ref, crq_ref, lse_ref, do_ref, dl_ref, dk_ref, dv_ref, dc_ref):
        j, i = pl.program_id(0), pl.program_id(1)

        @pl.when(i == 0)
        def _():
            dk_ref[...] = jnp.zeros_like(dk_ref)
            dv_ref[...] = jnp.zeros_like(dv_ref)
            dc_ref[...] = jnp.zeros_like(dc_ref)

        def block(masked):
            for h in range(NH):
                hs = _hs(h)
                qh, doh = q_ref[:, hs], _mx(do_ref[:, hs])
                st = _dot_nt(k_ref[:, hs], qh) + (crq_ref[h:h + 1, :] - cck_ref[:, h:h + 1])
                pt = jnp.exp(st - lse_ref[h:h + 1, :])
                if masked:
                    kpos = lax.broadcasted_iota(jnp.int32, (t, t), 0)
                    pt = jnp.where(kpos <= lax.broadcasted_iota(jnp.int32, (t, t), 1), pt, 0.0)
                dst = pt * (_dot_nt(v_ref[:, hs], doh) - dl_ref[h:h + 1, :])
                dv_ref[:, hs] += _dot(pt, doh)
                dk_ref[:, hs] += _dot(dst, qh)
                dc_ref[:, h:h + 1] += -jnp.sum(dst, axis=-1, keepdims=True)

        @pl.when(i > j)
        def _():
            block(False)

        @pl.when(i == j)
        def _():
            block(True)

    qspec = lambda w: pl.BlockSpec((t, w), lambda j, i: (jnp.maximum(i, j), 0))
    kspec = lambda w: pl.BlockSpec((t, w), lambda j, i: (j, 0))
    qrows = pl.BlockSpec((8, t), lambda j, i: (0, jnp.maximum(i, j)))
    return _call(
        body, name="flash_bwd_dkv", grid=(n, n),
        in_specs=[qspec(BW), kspec(BW), kspec(BW), kspec(128), qrows, qrows, qspec(BW), qrows],
        out_specs=[kspec(BW), kspec(BW), kspec(128)],
        out_shape=[jax.ShapeDtypeStruct((S, BW), F32), jax.ShapeDtypeStruct((S, BW), F32),
                   jax.ShapeDtypeStruct((S, 128), F32)],
        sem=("parallel", "arbitrary"))(q, k, v, cc, cr, lse_rows, do, delta_rows)


def _attn_prep_bwd(z, fb, gq, gk, dq, dk, dv, dgate, dcc, dz):
    S = z.shape[0]
    ts = _tile(S, ROW_TILE)
    nt = S // ts

    def body(zb_ref, zf_ref, fb_ref, gq_ref, gk_ref, dq_ref, dk_ref, dv_ref, dg_ref, dcc_ref, dz_in, dzb_ref, dzf_ref,
             st_ref, carry):
        @pl.when(pl.program_id(0) == 0)
        def _():
            carry[...] = jnp.zeros_like(carry)
            st_ref[...] = jnp.zeros_like(st_ref)

        zb = zb_ref[...]
        gqv, gkv = gq_ref[...], gk_ref[...]
        dqv, dkv = dq_ref[...], dk_ref[...]
        sq = jnp.zeros((1, DH), F32)
        sk = jnp.zeros((1, DH), F32)
        for h in range(NH):
            dx, dgr = _rms_bwd(zb[:, _hs(h)], gqv, dqv[:, _hs(h)] * (DH ** -0.5))
            dzb_ref[:, _hs(h)] = _mx(dx)
            sq = sq + jnp.sum(dgr, axis=0, keepdims=True)
            ks = slice(BW + h * DH, BW + (h + 1) * DH)
            dx, dgr = _rms_bwd(zb[:, ks], gkv, dkv[:, _hs(h)])
            dzb_ref[:, ks] = _mx(dx)
            sk = sk + jnp.sum(dgr, axis=0, keepdims=True)
        dzb_ref[:, 2 * BW:3 * BW] = _mx(dv_ref[...])
        dzb_ref[:, 3 * BW:4 * BW] = dg_ref[...]
        dc = dcc_ref[...]
        dlf = _dot_exact_lhs01(_tri(ts, upper=True), dc) + carry[...]
        carry[...] = dlf[0:1, :]
        dfz = dlf * _sigmoid(-(zf_ref[...] + fb_ref[...]))
        dzf_ref[...] = _mx(dfz)
        st_ref[0:1, 0:DH] += sq
        st_ref[1:2, 0:DH] += sk
        st_ref[2:3, :] += jnp.sum(dfz, axis=0, keepdims=True)

    rev = lambda w, c=0: pl.BlockSpec((ts, w), lambda i: (nt - 1 - i, c))
    one = lambda w: pl.BlockSpec((1, w), lambda i: (0, 0))
    return _call(
        body, name="attn_prep_bwd", grid=(nt,),
        in_specs=[rev(1024, ZB // 1024), rev(128, ZF // 128), one(128), one(DH), one(DH),
                  rev(BW), rev(BW), rev(BW), rev(BW), rev(128), _ANY],
        out_specs=[rev(1024, ZB // 1024), rev(128), pl.BlockSpec((8, 128), lambda i: (0, 0))],
        out_shape=[jax.ShapeDtypeStruct((S, ZCOLS), MXU_DTYPE), jax.ShapeDtypeStruct((S, 128), MXU_DTYPE),
                   jax.ShapeDtypeStruct((8, 128), F32)],
        scratch=[pltpu.VMEM((1, 128), F32)], sem=("arbitrary",), aliases={10: 0})(z, z, fb, gq, gk, dq, dk, dv, dgate, dcc, dz)


def _put_fgate_cols(dzf, dz):
    S = dzf.shape[0]
    ts = _tile(S, ROW_TILE)

    def body(f_ref, dz_in, o_ref):
        o_ref[...] = f_ref[...]

    return _call(
        body, name="put_fgate_cols", grid=(S // ts,),
        in_specs=[pl.BlockSpec((ts, 128), lambda i: (i, 0)), _ANY],
        out_specs=pl.BlockSpec((ts, 128), lambda i: (i, ZF // 128)),
        out_shape=jax.ShapeDtypeStruct((S, ZCOLS), MXU_DTYPE), sem=("parallel",), aliases={1: 0})(dzf, dz)


def _hgrn_consts():
    C = CHUNK
    t = np.arange(C)[:, None]
    j = np.arange(C)[None, :]
    mats = [(j <= t)]
    tq, tk, masks = [], [], []
    for lev in range(NLEV):
        m = C >> (lev + 1)
        blk, pos = t // (2 * m), t % (2 * m)
        mid = blk * 2 * m + m - 1
        tq.append((pos >= m) & (j > mid) & (j <= t))
        tk.append((pos < m) & (j > t) & (j <= mid))
        sblk, spos = j // (2 * m), j % (2 * m)
        masks.append((blk == sblk) & (pos >= m) & (spos < m))
    masks.append(t == j)
    tmat = np.concatenate(mats + tq + tk, axis=0).astype(np.float32)
    return tmat, np.stack(masks).astype(np.float32)


def _hgrn_chunk_fwd(zc, lb, tmat):
    cq, cf, ci = zc[:, 0:BW], zc[:, BW:2 * BW], zc[:, 2 * BW:3 * BW]
    q = _silu(cq)
    sg = _sigmoid(cf)
    g = lb + (1.0 - lb) * sg
    lg = jnp.log(g)
    kf = (1.0 - lb) * _sigmoid(-cf)
    e = _dot_exact_lhs01(tmat, lg)
    b = e[0:CHUNK]
    blast = b[CHUNK - 1:CHUNK, :]
    return cq, cf, q, sg, g, kf, ci, e, b, blast


def _round_mx(x):
    if MXU_DTYPE != jnp.bfloat16:
        return x
    u = lax.bitcast_convert_type(x, jnp.uint32)
    u = (u + jnp.uint32(0x7FFF) + ((u >> 16) & jnp.uint32(1))) & jnp.uint32(0xFFFF0000)
    return lax.bitcast_convert_type(u, F32)


def _hgrn_scores(q, kf, e, masks, h):
    qh, kh = q[:, _hs(h)], kf[:, _hs(h)]
    ql, kl = [], []
    a = None
    for lev in range(NLEV + 1):
        if lev < NLEV:
            eq = jnp.exp(e[(1 + lev) * CHUNK:(2 + lev) * CHUNK, _hs(h)])
            ek = jnp.exp(e[(1 + NLEV + lev) * CHUNK:(2 + NLEV + lev) * CHUNK, _hs(h)])
            ql.append((_round_mx(qh * eq), eq))
            kl.append((_round_mx(kh * ek), ek))
        else:
            ql.append((_round_mx(qh), None))
            kl.append((_round_mx(kh), None))
        term = masks[lev] * _dot_nt(ql[-1][0], kl[-1][0])
        a = term if a is None else a + term
    return a, ql, kl


def _hgrn_fwd(z, lb, gain, tmat, masks):
    S = z.shape[0]
    ts = _tile(S, ROW_TILE)
    nc = ts // CHUNK

    def body(zc_ref, lb_ref, gn_ref, tm_ref, mk_ref, o_ref, st_ref, y_ref, state):
        @pl.when(pl.program_id(0) == 0)
        def _():
            state[...] = jnp.zeros_like(state)

        lbv, gn, tm, mk = lb_ref[...], gn_ref[...], tm_ref[...], mk_ref[...]

        def chunk(c, carry):
            r0 = pl.multiple_of(c * CHUNK, CHUNK)
            zc = zc_ref[pl.ds(r0, CHUNK), :]
            cq, cf, q, sg, g, kf, v, e, b, blast = _hgrn_chunk_fwd(zc, lbv, tm)
            qe = q * jnp.exp(b)
            kd = kf * jnp.exp(blast - b)
            st_ref[pl.ds(r0, CHUNK), :] = state[...]
            for h in range(NH):
                sth = state[:, _hs(h)]
                a, _, _ = _hgrn_scores(q, kf, e, mk, h)
                oh = _dot_nt(qe[:, _hs(h)], sth) + _dot(a, v[:, _hs(h)])
                state[:, _hs(h)] = sth * jnp.exp(blast[:, _hs(h)]) + _dot_tn(v[:, _hs(h)], kd[:, _hs(h)])
                o_ref[pl.ds(r0, CHUNK), _hs(h)] = oh
                yn = _rms_fwd(oh, gn[:, _hs(h)])
                y_ref[pl.ds(r0, CHUNK), _hs(h)] = _mx(yn * _silu(zc[:, 3 * BW + h * DH:3 * BW + (h + 1) * DH]))
            return carry

        lax.fori_loop(0, nc, chunk, 0, unroll=2)

    row = pl.BlockSpec((ts, BW), lambda i: (i, 0))
    one = pl.BlockSpec((1, BW), lambda i: (0, 0))
    return _call(
        body, name="hgrn_fwd", grid=(S // ts,),
        in_specs=[pl.BlockSpec((ts, 1024), lambda i: (i, ZC // 1024)), one, one,
                  pl.BlockSpec(tmat.shape, lambda i: (0, 0)), pl.BlockSpec(masks.shape, lambda i: (0, 0, 0))],
        out_specs=[row, row, row],
        out_shape=[jax.ShapeDtypeStruct((S, BW), F32), jax.ShapeDtypeStruct((S, BW), F32),
                   jax.ShapeDtypeStruct((S, BW), MXU_DTYPE)],
        scratch=[pltpu.VMEM((CHUNK, BW), F32)], sem=("arbitrary",))(z, lb, gain, tmat, masks)


def _hgrn_bwd(z, lb, gain, tmat, masks, o_pre, states, dy, dz):
    S = z.shape[0]
    ts = _tile(S, ROW_TILE)
    nt = S // ts
    nc = ts // CHUNK

    def body(zc_ref, lb_ref, gn_ref, tm_ref, mk_ref, o_ref, st_ref, dy_ref, dz_in, dz_ref, stat_ref, dstate):
        @pl.when(pl.program_id(0) == 0)
        def _():
            dstate[...] = jnp.zeros_like(dstate)
            stat_ref[...] = jnp.zeros_like(stat_ref)

        lbv, gn, tm, mk = lb_ref[...], gn_ref[...], tm_ref[...], mk_ref[...]
        upper = _tri(CHUNK, upper=True)
        lower_strict = 1.0 - upper

        def chunk(cc, carry):
            c = nc - 1 - cc
            r0 = pl.multiple_of(c * CHUNK, CHUNK)
            zc = zc_ref[pl.ds(r0, CHUNK), :]
            cq, cf, q, sg, g, kf, v, e, b, blast = _hgrn_chunk_fwd(zc, lbv, tm)
            eb = jnp.exp(b)
            ebl = jnp.exp(blast - b)
            qe = q * eb
            kd = kf * ebl
            o = o_ref[pl.ds(r0, CHUNK), :]
            dyv = dy_ref[pl.ds(r0, CHUNK), :]
            stp = st_ref[pl.ds(r0, CHUNK), :]
            cg = zc[:, 3 * BW:4 * BW]
            sgate = _silu(cg)
            dq_parts, dk_parts, dv_parts, dcg_parts = [], [], [], []
            dgain, up_parts, lo_parts, const_parts = [], [], [], []
            for h in range(NH):
                hs = _hs(h)
                oh = o[:, hs]
                r = lax.rsqrt(jnp.mean(oh * oh, axis=-1, keepdims=True) + EPS)
                nrm = oh * r
                dyn = dyv[:, hs] * sgate[:, hs]
                dcg_parts.append(dyv[:, hs] * nrm * gn[:, hs] * _dsilu(cg[:, hs]))
                dgain.append(jnp.sum(dyn * nrm, axis=0, keepdims=True))
                tt = dyn * gn[:, hs]
                doh = r * (tt - nrm * jnp.mean(tt * nrm, axis=-1, keepdims=True))
                a, ql, kl = _hgrn_scores(q, kf, e, mk, h)
                da = _dot_nt(doh, v[:, hs])
                dsth = dstate[:, hs]
                ebh = jnp.exp(blast[:, hs])
                dv_parts.append(_dot_tn(a, doh) + _dot_nt(kd[:, hs], dsth))
                dq_inter = eb[:, hs] * _dot(doh, stp[:, hs])
                dk_state = ebl[:, hs] * _dot(v[:, hs], dsth)
                dqh, dkh, gh = dq_inter, dk_state, None
                for lev in range(NLEV + 1):
                    dal = mk[lev] * da
                    xq = _dot(dal, kl[lev][0])
                    yk = _dot_tn(dal, ql[lev][0])
                    gterm = ql[lev][0] * xq - kl[lev][0] * yk
                    gh = gterm if gh is None else gh + gterm
                    dqh = dqh + (xq if lev == NLEV else ql[lev][1] * xq)
                    dkh = dkh + (yk if lev == NLEV else kl[lev][1] * yk)
                up_parts.append(gh + q[:, hs] * dq_inter)
                lo_parts.append(kf[:, hs] * dk_state)
                const_parts.append(jnp.sum(dsth * stp[:, hs], axis=0, keepdims=True) * ebh)
                dstate[:, hs] = dsth * ebh + _dot_tn(doh, qe[:, hs])
                dq_parts.append(dqh)
                dk_parts.append(dkh)
            dq = jnp.concatenate(dq_parts, axis=1)
            dk = jnp.concatenate(dk_parts, axis=1)
            dlg = (_dot_exact_lhs01(upper, jnp.concatenate(up_parts, axis=1))
                   + _dot_exact_lhs01(lower_strict, jnp.concatenate(lo_parts, axis=1))
                   + jnp.concatenate(const_parts, axis=1))
            dsg = sg * (1.0 - sg)
            dz_ref[pl.ds(r0, CHUNK), 0:BW] = _mx(dq * _dsilu(cq))
            dz_ref[pl.ds(r0, CHUNK), BW:2 * BW] = _mx((dlg / g - dk) * (1.0 - lbv) * dsg)
            dz_ref[pl.ds(r0, CHUNK), 2 * BW:3 * BW] = _mx(jnp.concatenate(dv_parts, axis=1))
            dz_ref[pl.ds(r0, CHUNK), 3 * BW:4 * BW] = _mx(jnp.concatenate(dcg_parts, axis=1))
            stat_ref[0:1, :] += jnp.concatenate(dgain, axis=1)
            stat_ref[1:2, :] += jnp.sum((dlg / g - dk) * (1.0 - sg), axis=0, keepdims=True)
            return carry

        lax.fori_loop(0, nc, chunk, 0, unroll=2)

    rev = lambda w, c=0: pl.BlockSpec((ts, w), lambda i: (nt - 1 - i, c))
    one = pl.BlockSpec((1, BW), lambda i: (0, 0))
    return _call(
        body, name="hgrn_bwd", grid=(nt,),
        in_specs=[rev(1024, ZC // 1024), one, one, pl.BlockSpec(tmat.shape, lambda i: (0, 0)),
                  pl.BlockSpec(masks.shape, lambda i: (0, 0, 0)), rev(BW), rev(BW), rev(BW, 2), _ANY],
        out_specs=[rev(1024, ZC // 1024), pl.BlockSpec((8, BW), lambda i: (0, 0))],
        out_shape=[jax.ShapeDtypeStruct((S, ZCOLS), MXU_DTYPE), jax.ShapeDtypeStruct((8, BW), F32)],
        scratch=[pltpu.VMEM((CHUNK, BW), F32)],
        sem=("arbitrary",), aliases={8: 0})(z, lb, gain, tmat, masks, o_pre, states, dy, dz)


def _lower_bounds_fwd(lb_logits):
    def body(l_ref, o_ref):
        l = l_ref[...]
        m = jnp.max(l, axis=0, keepdims=True)
        ex = jnp.exp(l - m)
        p = ex / jnp.sum(ex, axis=0, keepdims=True)
        cs = p[0:1, :]
        o_ref[0:1, :] = jnp.clip(cs - p[0:1, :], 0.0, 1.0)
        for d in range(1, DEPTH):
            cs = cs + p[d:d + 1, :]
            o_ref[d:d + 1, :] = jnp.clip(cs - p[0:1, :], 0.0, 1.0)

    full = pl.BlockSpec((DEPTH, BW), lambda: (0, 0))
    return _call(body, name="lower_bounds_fwd", grid=(), in_specs=[full], out_specs=full,
                 out_shape=jax.ShapeDtypeStruct((DEPTH, BW), F32))(lb_logits)


def _lower_bounds_bwd(lb_logits, dlow):
    def body(l_ref, d_ref, o_ref):
        l = l_ref[...]
        m = jnp.max(l, axis=0, keepdims=True)
        ex = jnp.exp(l - m)
        p = ex / jnp.sum(ex, axis=0, keepdims=True)
        dl = d_ref[...]
        cs = p[0:1, :]
        dcs = []
        for d in range(DEPTH):
            if d > 0:
                cs = cs + p[d:d + 1, :]
            val = cs - p[0:1, :]
            dcs.append(jnp.where((val > 0.0) & (val < 1.0), dl[d:d + 1, :], 0.0))
        total = dcs[0] + dcs[1] + dcs[2] + dcs[3]
        dp = []
        for j in range(DEPTH):
            s = dcs[j]
            for d in range(j + 1, DEPTH):
                s = s + dcs[d]
            dp.append(s - total if j == 0 else s)
        inner = p[0:1, :] * dp[0]
        for j in range(1, DEPTH):
            inner = inner + p[j:j + 1, :] * dp[j]
        for j in range(DEPTH):
            o_ref[j:j + 1, :] = p[j:j + 1, :] * (dp[j] - inner)

    full = pl.BlockSpec((DEPTH, BW), lambda: (0, 0))
    return _call(body, name="lower_bounds_bwd", grid=(), in_specs=[full, full], out_specs=full,
                 out_shape=jax.ShapeDtypeStruct((DEPTH, BW), F32))(lb_logits, dlow)


def _sgu_fwd(z, gv, ws, bs):
    S = z.shape[0]
    ts = _tile(S, ROW_TILE)
    nc = ts // SCHUNK

    def body(zd_ref, gv_ref, ws_ref, bs_ref, y_ref):
        gvv, bsv = gv_ref[...], bs_ref[...]
        tril = _tri(SCHUNK)
        for c in range(nc):
            rs = slice(c * SCHUNK, (c + 1) * SCHUNK)
            zd = zd_ref[rs, :]
            for h in range(NH):
                vn = _rms_fwd(zd[:, BW + h * DH:BW + (h + 1) * DH], gvv[:, _hs(h)])
                s = _dot(ws_ref[h] * tril, vn) + bsv[:, h:h + 1]
                y_ref[rs, _hs(h)] = _mx(zd[:, _hs(h)] * s * _silu(zd[:, 2 * BW + h * DH:2 * BW + (h + 1) * DH]))

    return _call(
        body, name="sgu_fwd", grid=(S // ts,),
        in_specs=[pl.BlockSpec((ts, 1024), lambda i: (i, ZD // 1024)), pl.BlockSpec((1, BW), lambda i: (0, 0)),
                  pl.BlockSpec((NH, SCHUNK, SCHUNK), lambda i: (0, 0, 0)), pl.BlockSpec((SCHUNK, 128), lambda i: (0, 0))],
        out_specs=pl.BlockSpec((ts, BW), lambda i: (i, 0)),
        out_shape=jax.ShapeDtypeStruct((S, BW), MXU_DTYPE), sem=("parallel",))(z, gv, ws, bs)


def _sgu_bwd(z, gv, ws, bs, dy, dz):
    S = z.shape[0]
    ts = _tile(S, ROW_TILE)
    nc = ts // SCHUNK

    def body(zd_ref, gv_ref, ws_ref, bs_ref, dy_ref, dz_in, dz_ref, dws_ref, dbs_ref, st_ref):
        @pl.when(pl.program_id(0) == 0)
        def _():
            dws_ref[...] = jnp.zeros_like(dws_ref)
            dbs_ref[...] = jnp.zeros_like(dbs_ref)
            st_ref[...] = jnp.zeros_like(st_ref)

        gvv, bsv = gv_ref[...], bs_ref[...]
        tril = _tri(SCHUNK)
        dz_ref[:, 3 * BW:4 * BW] = jnp.zeros((ts, BW), MXU_DTYPE)
        for c in range(nc):
            rs = slice(c * SCHUNK, (c + 1) * SCHUNK)
            zd = zd_ref[rs, :]
            dyv = dy_ref[rs, :]
            for h in range(NH):
                hs = _hs(h)
                u = zd[:, hs]
                vraw = zd[:, BW + h * DH:BW + (h + 1) * DH]
                gt = zd[:, 2 * BW + h * DH:2 * BW + (h + 1) * DH]
                gvh = gvv[:, hs]
                vn = _rms_fwd(vraw, gvh)
                wm = ws_ref[h] * tril
                s = _dot(wm, vn) + bsv[:, h:h + 1]
                sil = _silu(gt)
                d = dyv[:, hs]
                ds = d * u * sil
                dz_ref[rs, hs] = _mx(d * s * sil)
                dz_ref[rs, 2 * BW + h * DH:2 * BW + (h + 1) * DH] = _mx(d * u * s * _dsilu(gt))
                dws_ref[h] += tril * _dot_nt(ds, vn)
                dbs_ref[:, h:h + 1] += jnp.sum(ds, axis=-1, keepdims=True)
                dvn = _dot_tn(wm, ds)
                dx, dgr = _rms_bwd(vraw, gvh, dvn)
                dz_ref[rs, BW + h * DH:BW + (h + 1) * DH] = _mx(dx)
                st_ref[0:1, hs] += jnp.sum(dgr, axis=0, keepdims=True)

    return _call(
        body, name="sgu_bwd", grid=(S // ts,),
        in_specs=[pl.BlockSpec((ts, 1024), lambda i: (i, ZD // 1024)), pl.BlockSpec((1, BW), lambda i: (0, 0)),
                  pl.BlockSpec((NH, SCHUNK, SCHUNK), lambda i: (0, 0, 0)), pl.BlockSpec((SCHUNK, 128), lambda i: (0, 0)),
                  pl.BlockSpec((ts, BW), lambda i: (i, 3)), _ANY],
        out_specs=[pl.BlockSpec((ts, 1024), lambda i: (i, ZD // 1024)), pl.BlockSpec((NH, SCHUNK, SCHUNK), lambda i: (0, 0, 0)),
                   pl.BlockSpec((SCHUNK, 128), lambda i: (0, 0)), pl.BlockSpec((8, BW), lambda i: (0, 0))],
        out_shape=[jax.ShapeDtypeStruct((S, ZCOLS), MXU_DTYPE), jax.ShapeDtypeStruct((NH, SCHUNK, SCHUNK), F32),
                   jax.ShapeDtypeStruct((SCHUNK, 128), F32), jax.ShapeDtypeStruct((8, BW), F32)],
        sem=("arbitrary",), aliases={5: 0})(z, gv, ws, bs, dy, dz)


def _merge_fwd(x, z, ys, wup, mb, wo):
    S = x.shape[0]
    ts = _tile(S, MERGE_TILE)

    def body(x_ref, zm_ref, ya_ref, yb_ref, yc_ref, yd_ref, wup_ref, mb_ref, wo_ref, x1_ref, mg_ref):
        yrefs = (ya_ref, yb_ref, yc_ref, yd_ref)
        mbv = mb_ref[...]
        merged = None
        for b in range(NBR):
            cs = slice(b * D_MODEL, (b + 1) * D_MODEL)
            term = _sigmoid(zm_ref[:, cs] + mbv[b:b + 1, :]) * jnp.dot(yrefs[b][...], wup_ref[b],
                                                                      preferred_element_type=F32)
            merged = term if merged is None else merged + term
        mg = _mx(merged)
        mg_ref[...] = mg
        x1_ref[...] = x_ref[...] + jnp.dot(mg, wo_ref[...], preferred_element_type=F32)

    row = lambda w: pl.BlockSpec((ts, w), lambda i: (i, 0))
    return _call(
        body, name="merge_fwd", grid=(S // ts,),
        in_specs=[row(D_MODEL), pl.BlockSpec((ts, 4096), lambda i: (i, 0)), row(BW), row(BW), row(BW), row(BW),
                  pl.BlockSpec((NBR, BW, D_MODEL), lambda i: (0, 0, 0)), pl.BlockSpec((NBR, D_MODEL), lambda i: (0, 0)),
                  pl.BlockSpec((D_MODEL, D_MODEL), lambda i: (0, 0))],
        out_specs=[row(D_MODEL), row(D_MODEL)],
        out_shape=[jax.ShapeDtypeStruct((S, D_MODEL), F32), jax.ShapeDtypeStruct((S, D_MODEL), MXU_DTYPE)],
        sem=("parallel",))(x, z, *ys, wup, mb, wo)


def _merge_bwd(dx1, z, ys, wup, mb, wo):
    S = dx1.shape[0]
    ts = _tile(S, MERGE_TILE)

    def body(dx_ref, zm_ref, ya_ref, yb_ref, yc_ref, yd_ref, wup_ref, mb_ref, wo_ref,
             dzm_ref, du_ref, dxb_ref, dy_ref, st_ref):
        @pl.when(pl.program_id(0) == 0)
        def _():
            st_ref[...] = jnp.zeros_like(st_ref)

        yrefs = (ya_ref, yb_ref, yc_ref, yd_ref)
        mbv = mb_ref[...]
        dxb = _mx(dx_ref[...])
        dxb_ref[...] = dxb
        dmerged = _dot_nt(dxb, wo_ref[...])
        for b in range(NBR):
            cs = slice(b * D_MODEL, (b + 1) * D_MODEL)
            u = jnp.dot(yrefs[b][...], wup_ref[b], preferred_element_type=F32)
            sg = _sigmoid(zm_ref[:, cs] + mbv[b:b + 1, :])
            du = _mx(dmerged * sg)
            du_ref[:, cs] = du
            dzm = dmerged * u * sg * (1.0 - sg)
            dzm_ref[:, cs] = _mx(dzm)
            st_ref[b:b + 1, :] += jnp.sum(dzm, axis=0, keepdims=True)
            dy_ref[:, b * BW:(b + 1) * BW] = _dot_nt(du, wup_ref[b])

    row = lambda w: pl.BlockSpec((ts, w), lambda i: (i, 0))
    return _call(
        body, name="merge_bwd", grid=(S // ts,),
        in_specs=[row(D_MODEL), pl.BlockSpec((ts, 4096), lambda i: (i, 0)), row(BW), row(BW), row(BW), row(BW),
                  pl.BlockSpec((NBR, BW, D_MODEL), lambda i: (0, 0, 0)),
                  pl.BlockSpec((NBR, D_MODEL), lambda i: (0, 0)), pl.BlockSpec((D_MODEL, D_MODEL), lambda i: (0, 0))],
        out_specs=[row(4096), row(4096), row(D_MODEL), row(D_MODEL), pl.BlockSpec((8, D_MODEL), lambda i: (0, 0))],
        out_shape=[jax.ShapeDtypeStruct((S, ZCOLS), MXU_DTYPE), jax.ShapeDtypeStruct((S, 4096), MXU_DTYPE),
                   jax.ShapeDtypeStruct((S, D_MODEL), MXU_DTYPE), jax.ShapeDtypeStruct((S, D_MODEL), F32),
                   jax.ShapeDtypeStruct((8, D_MODEL), F32)],
        sem=("arbitrary",))(dx1, z, *ys, wup, mb, wo)


def _ple_fwd(x1, p, g, wg, wp):
    S = x1.shape[0]
    ts = _tile(S, ROW_TILE)

    def body(x_ref, p_ref, g_ref, wg_ref, wp_ref, o_ref):
        x = x_ref[...]
        hp = _mx(_rms_fwd(x, g_ref[...]))
        gate = _sigmoid(jnp.dot(hp, wg_ref[...], preferred_element_type=F32))
        pp = jnp.dot(_mx(p_ref[...]), wp_ref[...], preferred_element_type=F32)
        o_ref[...] = x + gate * pp

    row = lambda w: pl.BlockSpec((ts, w), lambda i: (i, 0))
    return _call(
        body, name="ple_fwd", grid=(S // ts,),
        in_specs=[row(D_MODEL), row(PLE), pl.BlockSpec((1, D_MODEL), lambda i: (0, 0)),
                  pl.BlockSpec((D_MODEL, D_MODEL), lambda i: (0, 0)), pl.BlockSpec((PLE, D_MODEL), lambda i: (0, 0))],
        out_specs=row(D_MODEL), out_shape=jax.ShapeDtypeStruct((S, D_MODEL), F32),
        sem=("parallel",))(x1, p, g, wg, wp)


def _ple_bwd(x1, p, dx2, g, wg, wp):
    S = x1.shape[0]
    ts = _tile(S, ROW_TILE)

    def body(x_ref, p_ref, dx_ref, g_ref, wg_ref, wp_ref, dx1_ref, hp_ref, dgl_ref, dpp_ref, pb_ref, st_ref):
        @pl.when(pl.program_id(0) == 0)
        def _():
            st_ref[...] = jnp.zeros_like(st_ref)

        x, gv, dx2 = x_ref[...], g_ref[...], dx_ref[...]
        hp = _mx(_rms_fwd(x, gv))
        hp_ref[...] = hp
        gate = _sigmoid(jnp.dot(hp, wg_ref[...], preferred_element_type=F32))
        pb = _mx(p_ref[...])
        pb_ref[...] = pb
        pp = jnp.dot(pb, wp_ref[...], preferred_element_type=F32)
        dgl = _mx(dx2 * pp * gate * (1.0 - gate))
        dgl_ref[...] = dgl
        dpp_ref[...] = _mx(dx2 * gate)
        dhp = _dot_nt(dgl, wg_ref[...])
        dxn, dgr = _rms_bwd(x, gv, dhp)
        dx1_ref[...] = dx2 + dxn
        st_ref[0:1, :] += jnp.sum(dgr, axis=0, keepdims=True)

    row = lambda w: pl.BlockSpec((ts, w), lambda i: (i, 0))
    sq = pl.BlockSpec((D_MODEL, D_MODEL), lambda i: (0, 0))
    return _call(
        body, name="ple_bwd", grid=(S // ts,),
        in_specs=[row(D_MODEL), row(PLE), row(D_MODEL), pl.BlockSpec((1, D_MODEL), lambda i: (0, 0)), sq,
                  pl.BlockSpec((PLE, D_MODEL), lambda i: (0, 0))],
        out_specs=[row(D_MODEL), row(D_MODEL), row(D_MODEL), row(D_MODEL), row(PLE),
                   pl.BlockSpec((8, D_MODEL), lambda i: (0, 0))],
        out_shape=[jax.ShapeDtypeStruct((S, D_MODEL), F32)] + [jax.ShapeDtypeStruct((S, D_MODEL), MXU_DTYPE)] * 3
        + [jax.ShapeDtypeStruct((S, PLE), MXU_DTYPE), jax.ShapeDtypeStruct((8, D_MODEL), F32)],
        sem=("arbitrary",))(x1, p, dx2, g, wg, wp)


def _pad_rows(a, rows=8):
    return jnp.concatenate([a, jnp.zeros((rows - a.shape[0],) + a.shape[1:], a.dtype)], axis=0)


def _pad_lanes(a, lanes=128):
    return jnp.concatenate([a, jnp.zeros(a.shape[:-1] + (lanes - a.shape[-1],), a.dtype)], axis=-1)


def _wz_from_w_in(w):
    zeros = lambda n: jnp.zeros((w.shape[0], n), w.dtype)
    return jnp.concatenate([w[:, _OM:_OEND], w[:, _OA:_OB], w[:, _OB:_OF], w[:, _OC:_OD], w[:, _OD:_OM], zeros(256),
                            w[:, _OF:_OC], zeros(124)], axis=1)


def _w_in_from_wz(g):
    return jnp.concatenate([g[:, ZA:ZB], g[:, ZB:ZC], g[:, ZF:ZF + 4], g[:, ZC:ZD], g[:, ZD:ZD + 768], g[:, ZM:ZA]], axis=1)


_W_IN_GROUPS = [(_OA, _OB, ZA), (_OB, _OF, ZB), (_OF, _OC, ZF), (_OC, _OD, ZC), (_OD, _OM, ZD), (_OM, _OEND, ZM)]


def _wz_from_shards(g):
    n = g.shape[-1]
    pieces, pos = [], 0
    for a, b, zs in sorted(_W_IN_GROUPS, key=lambda grp: grp[2]):
        if zs > pos:
            pieces.append(jnp.zeros((g.shape[1], zs - pos), g.dtype))
        for k in range(4):
            lo, hi = max(a, k * n), min(b, (k + 1) * n)
            if lo < hi:
                pieces.append(g[k][:, lo - k * n:hi - k * n])
        pos = zs + (b - a)
    pieces.append(jnp.zeros((g.shape[1], ZCOLS - pos), g.dtype))
    return jnp.concatenate(pieces, axis=1)


def _w_in_slabs_from_wz(g):
    n = _OEND // 4
    slabs = []
    for k in range(4):
        pieces = []
        for a, b, zs in _W_IN_GROUPS:
            lo, hi = max(a, k * n), min(b, (k + 1) * n)
            if lo < hi:
                pieces.append(g[:, zs + lo - a:zs + hi - a])
        slabs.append(jnp.concatenate(pieces, axis=1))
    return jnp.stack(slabs)


def _local_step(x, p, target, wts, dist=None):
    tmat_np, masks_np = _hgrn_consts()
    tmat = jnp.asarray(tmat_np, MXU_DTYPE)
    masks = jnp.asarray(masks_np, F32)
    lower = _lower_bounds_fwd(wts["lb_logits"])
    saved = []
    for li in range(DEPTH):
        big = dist.weights(li) if dist else {n: wts[n][li] for n in _GATHERED}
        wz = _wz_from_shards(big["w_in"]) if big["w_in"].ndim == 3 else _wz_from_w_in(big["w_in"])
        g_mix = wts["norm_mix"][li][None, :]
        h = _norm_fwd(x, g_mix)
        z = _mm_nn(h, wz, F32, name="mm_z", tn=1664)
        cw = _pad_rows(jnp.concatenate([big["conv_w"], wts["conv_b"][li][None, :]], axis=0))
        ya = _conv_fwd(z, cw)
        fb = _pad_lanes(wts["fgate_bias"][li][None, :])
        gq, gk = wts["q_norm"][li][None, :], wts["k_norm"][li][None, :]
        qs, kn, vb, cc, cr = _attn_prep_fwd(z, fb, gq, gk)
        o, lse, yb, got = _flash_fwd(qs, kn, vb, cc, cr, z, comm=dist.fwd_comm(li) if dist else None)
        if dist:
            dist.fwd_done(li, got)
        lb = lower[li][None, :]
        gh = wts["hgrn_norm"][li][None, :]
        o_pre, states, yc = _hgrn_fwd(z, lb, gh, tmat, masks)
        gv = wts["sgu_norm"][li][None, :]
        ws = wts["spatial_w"][li]
        bs = _pad_lanes(wts["spatial_b"][li].T)
        yd = _sgu_fwd(z, gv, ws, bs)
        ys = (ya, yb, yc, yd)
        x1, merged = _merge_fwd(x, z, ys, big["w_up"], big["merge_b"], big["w_o"])
        g_ple = wts["norm_ple"][li][None, :]
        x2 = _ple_fwd(x1, p[li], g_ple, big["w_ple_gate"], big["w_ple_proj"])
        saved.append(dict(x=x, h=h, z=z, wz=wz, cw=cw, fb=fb, gq=gq, gk=gk, qs=qs, kn=kn, vb=vb, cc=cc, cr=cr, o=o, lse=lse,
                          lb=lb, gh=gh, o_pre=o_pre, states=states, gv=gv, ws=ws, bs=bs, ys=ys, x1=x1, merged=merged,
                          g_mix=g_mix, g_ple=g_ple, big=big))
        x = x2

    loss, dx = _loss_fwd_bwd(x, target)

    names = ["norm_mix", "w_in", "conv_w", "conv_b", "fgate_bias", "q_norm", "k_norm", "lb", "hgrn_norm", "sgu_norm",
             "spatial_w", "spatial_b", "w_up", "merge_b", "w_o", "norm_ple", "w_ple_gate", "w_ple_proj"]
    gl = {n: [None] * DEPTH for n in names}
    for li in reversed(range(DEPTH)):
        s = saved[li]
        z, big = s["z"], s["big"]
        wg, wp = big["w_ple_gate"], big["w_ple_proj"]
        dx1, hp, dgl, dpp, pb, st = _ple_bwd(s["x1"], p[li], dx, s["g_ple"], wg, wp)
        gl["norm_ple"][li] = st[0]
        gl["w_ple_gate"][li] = _mm_tn(hp, dgl, name="mm_dwg")
        gl["w_ple_proj"][li] = _mm_tn(pb, dpp, name="mm_dwp")
        wup, wo = big["w_up"], big["w_o"]
        dz, du, dxb, dy, st = _merge_bwd(dx1, z, s["ys"], wup, big["merge_b"], wo)
        gl["merge_b"][li] = st[0:NBR]
        gl["w_o"][li] = _mm_tn(s["merged"], dxb, name="mm_dwo")
        gl["w_up"][li] = jnp.stack([_mm_tn(s["ys"][b], du, name="mm_dwup", ycol=b, n=D_MODEL) for b in range(NBR)])
        dz, st = _conv_bwd(z, s["cw"], dy, dz)
        gl["conv_w"][li] = st[0:3]
        gl["conv_b"][li] = st[3]
        do, dgate = _attn_gate_bwd(dy, s["o"], z)
        fa = (s["qs"], s["kn"], s["vb"], s["cc"], s["cr"], s["lse"], s["o"], do)
        dq, drow, got = _flash_bwd_dq(*fa, comm=dist.bwd_comm() if dist else None)
        if dist:
            dist.bwd_done(got)
        dk, dv, dcc = _flash_bwd_dkv(s["qs"], s["kn"], s["vb"], s["cc"], s["cr"], s["lse"][:, 0:8].T, do, drow[:, 0:8].T)
        dz, dzf, st = _attn_prep_bwd(z, s["fb"], s["gq"], s["gk"], dq, dk, dv, dgate, dcc, dz)
        dz = _put_fgate_cols(dzf, dz)
        gl["q_norm"][li] = st[0, 0:DH]
        gl["k_norm"][li] = st[1, 0:DH]
        gl["fgate_bias"][li] = st[2, 0:NH]
        dz, st = _hgrn_bwd(z, s["lb"], s["gh"], tmat, masks, s["o_pre"], s["states"], dy, dz)
        gl["hgrn_norm"][li] = st[0]
        gl["lb"][li] = st[1]
        dz, dws, dbs, st = _sgu_bwd(z, s["gv"], s["ws"], s["bs"], dy, dz)
        gl["sgu_norm"][li] = st[0]
        gl["spatial_w"][li] = dws
        gl["spatial_b"][li] = dbs[:, 0:NH].T
        dwz = _mm_tn(s["h"], dz, name="mm_dwz", tn=1664)
        gl["w_in"][li] = dwz if dist else _w_in_from_wz(dwz)
        dh = _mm_nn(dz, s["wz"], F32, name="mm_dh", tk=1664, b_transposed=True)
        dx, st = _norm_bwd(s["x"], s["g_mix"], dh, dx1)
        gl["norm_mix"][li] = st[0]
        if dist:
            dist.push(li, {n: gl[n][li] for n in _BIG})

    if dist:
        dist.finish()
    grads = {n: jnp.stack(v) for n, v in gl.items() if not (dist and n in _BIG)}
    grads["lb_logits"] = _lower_bounds_bwd(wts["lb_logits"], grads.pop("lb"))
    return loss, dx, grads


def _my_pos():
    return lax.axis_index("x"), lax.axis_index("y"), lax.axis_index("c")


def _gather_phase(phase, ins, outs, sems):
    if phase == 1:
        return
    send, recv, lsem = sems
    x, y, c = _my_pos()
    me = 2 * x + y
    peers = [(1 - x, y), (x, 1 - y), (1 - x, 1 - y)]
    copies = []
    for t in range(len(ins)):
        copies.append(pltpu.make_async_copy(ins[t], outs[t].at[me], lsem.at[t]))
        for j, (px, py) in enumerate(peers):
            copies.append(pltpu.make_async_remote_copy(
                src_ref=ins[t], dst_ref=outs[t].at[me], send_sem=send.at[t, j], recv_sem=recv.at[t, j],
                device_id=(px, py, c), device_id_type=MESH))
    for cp in copies:
        if phase == 0:
            cp.start()
        else:
            cp.wait()


def _exchange_phase(phase, ins, outs, sems):
    send, recv, lsem = sems
    x, y, c = _my_pos()
    me = 2 * x + y
    sib = (x, y, 1 - c)
    chips = [(1 - x, y), (x, 1 - y), (1 - x, 1 - y)]

    def rc(t, k, src, dst, dev):
        return pltpu.make_async_remote_copy(src_ref=src, dst_ref=dst, send_sem=send.at[t, k], recv_sem=recv.at[t, k],
                                            device_id=dev, device_id_type=MESH)

    for t in range(len(ins)):
        local = pltpu.make_async_copy(ins[t].at[me], outs[t].at[2 * me + c], lsem.at[t])
        first = [rc(t, 0, ins[t].at[me], outs[t].at[2 * me + c], sib)]
        first += [rc(t, 1 + j, ins[t].at[2 * px + py], outs[t].at[2 * me + c], (px, py, c))
                  for j, (px, py) in enumerate(chips)]
        landed = [outs[t].at[2 * (2 * px + py) + c] for px, py in chips]
        passed = [rc(t, 4 + j, slot, slot, sib) for j, slot in enumerate(landed)]
        if phase == 0:
            local.start()
            for cp in first:
                cp.start()
        elif phase == 1:
            for j, slot in enumerate(landed):
                rc(t, 1 + j, slot, slot, (x, y, c)).wait_recv()
                passed[j].start()
        else:
            s0 = outs[t].at[2 * me + (1 - c)]
            rc(t, 0, s0, s0, (x, y, c)).wait_recv()
            for j, (px, py) in enumerate(chips):
                slot = outs[t].at[2 * (2 * px + py) + (1 - c)]
                rc(t, 4 + j, slot, slot, (x, y, c)).wait_recv()
            for cp in first + passed:
                cp.wait_send()
            local.wait()


_COMM_PHASES = {"gather": _gather_phase, "exchange": _exchange_phase}
_COMM_OUT = {"gather": lambda arrays: [jax.ShapeDtypeStruct((4,) + a.shape, a.dtype) for a in arrays],
             "exchange": lambda arrays: [jax.ShapeDtypeStruct((8,) + a.shape[1:], a.dtype) for a in arrays]}


def _comm_sems(kind, nc):
    k = 3 if kind == "gather" else 7
    return [pltpu.SemaphoreType.DMA((nc, k)), pltpu.SemaphoreType.DMA((nc, k)), pltpu.SemaphoreType.DMA((nc,))]


def _comm_alone(kind, arrays, name):
    nc = len(arrays)

    def body(*refs):
        for phase in range(3):
            _COMM_PHASES[kind](phase, refs[:nc], refs[nc:2 * nc], refs[2 * nc:])

    hbm = pl.BlockSpec(memory_space=pl.ANY)
    return pl.pallas_call(
        functools.partial(body), name=name, in_specs=[hbm] * nc, out_specs=[hbm] * nc,
        out_shape=_COMM_OUT[kind](arrays), scratch_shapes=_comm_sems(kind, nc))(*arrays)


def _allreduce_small(v):
    R = v.shape[0]

    def body(v_ref, o_ref, buf, send, recv):
        x, y, c = _my_pos()
        me = 4 * x + 2 * y + c
        buf[me] = v_ref[...]
        copies = []
        k = 0
        for dx in range(2):
            for dy in range(2):
                for dc in range(2):
                    if dx == 0 and dy == 0 and dc == 0:
                        continue
                    cp = pltpu.make_async_remote_copy(
                        src_ref=v_ref, dst_ref=buf.at[me], send_sem=send.at[k - 0], recv_sem=recv.at[k],
                        device_id=(jnp.bitwise_xor(x, dx), jnp.bitwise_xor(y, dy), jnp.bitwise_xor(c, dc)),
                        device_id_type=MESH)
                    cp.start()
                    copies.append(cp)
                    k += 1
        for cp in copies:
            cp.wait()
        acc = buf[0]
        for d in range(1, 8):
            acc = acc + buf[d]
        o_ref[...] = acc

    vm = pl.BlockSpec(memory_space=pltpu.VMEM)
    return pl.pallas_call(
        functools.partial(body), name="allreduce_small", in_specs=[vm], out_specs=vm,
        out_shape=jax.ShapeDtypeStruct((R, 128), F32),
        scratch_shapes=[pltpu.VMEM((8, R, 128), F32), pltpu.SemaphoreType.DMA((7,)), pltpu.SemaphoreType.DMA((7,))],
        compiler_params=pltpu.CompilerParams(vmem_limit_bytes=VMEM_LIMIT))(v)


def _adamw(w, m, v, parts, nparts):
    A, R, C = w.shape
    per_layer = isinstance(parts, (list, tuple))
    tr = _tile(R, 64) if per_layer else (R if R <= 128 else _tile(R, 128))
    nparr = len(parts) if per_layer else 1

    def body(*refs):
        w_ref, m_ref, v_ref = refs[:3]
        p_refs = refs[3:3 + nparr]
        g_ref, d_ref, nm_ref, nv_ref = refs[3 + nparr:]

        def update(read):
            g = read(0).astype(F32)
            for k in range(1, nparts):
                g = g + read(k).astype(F32)
            mm = ADAM_B1 * m_ref[0] + (1.0 - ADAM_B1) * g
            vv = ADAM_B2 * v_ref[0] + (1.0 - ADAM_B2) * jnp.square(g)
            m_hat = mm / (1.0 - ADAM_B1 ** ADAM_STEP)
            v_hat = vv / (1.0 - ADAM_B2 ** ADAM_STEP)
            g_ref[0] = g
            d_ref[0] = -ADAM_LR * (m_hat / (jnp.sqrt(v_hat) + ADAM_EPS) + ADAM_WD * w_ref[0])
            nm_ref[0] = mm
            nv_ref[0] = vv

        if not per_layer:
            update(lambda k: p_refs[0][k, 0])
        else:
            for a in range(A):
                @pl.when(pl.program_id(0) == a)
                def _(a=a):
                    update(lambda k: p_refs[a][k])

    blk = pl.BlockSpec((1, tr, C), lambda a, r: (a, r, 0))
    if per_layer:
        pspecs = [pl.BlockSpec((nparts, tr, C), functools.partial(lambda a, r, l: (0, jnp.where(a == l, r, 0), 0), l=l))
                  for l in range(A)]
        pargs = list(parts)
    else:
        pspecs = [pl.BlockSpec((nparts, 1, tr, C), lambda a, r: (0, a, r, 0))]
        pargs = [parts]
    return _call(
        body, name="adamw", grid=(A, R // tr), in_specs=[blk, blk, blk] + pspecs,
        out_specs=[blk] * 4, out_shape=[jax.ShapeDtypeStruct((A, R, C), F32)] * 4,
        sem=("arbitrary", "arbitrary"))(w, m, v, *pargs)


def _as3d(a):
    if a.ndim == 2:
        return a[None]
    if a.ndim == 3:
        return a
    return a.reshape((-1,) + a.shape[-2:])


_WEIGHTS = ["norm_mix", "w_in", "conv_w", "conv_b", "fgate_bias", "q_norm", "k_norm", "lb_logits", "hgrn_norm", "sgu_norm",
            "spatial_w", "spatial_b", "w_up", "merge_b", "w_o", "norm_ple", "w_ple_gate", "w_ple_proj"]
_BIG = ["w_in", "w_up", "w_o", "w_ple_gate", "w_ple_proj"]
_GATHERED = _BIG + ["conv_w", "merge_b"]
_SHARD_AXIS = {"w_in": 2, "w_up": 3, "w_o": 1, "w_ple_gate": 1, "w_ple_proj": 2, "conv_w": 2, "merge_b": 2}
_SMALL = [n for n in _WEIGHTS if n not in _BIG]


class _Dist:
    def __init__(self, w):
        self.w = w
        self.full = {0: self._unpack(_comm_alone("gather", self._shards(0), "gather_weights"))}
        self.contrib = {}
        self.pending = None

    def _shards(self, l):
        return [_mx(self.w[n][l]) for n in _BIG] + [self.w["conv_w"][l], self.w["merge_b"][l]]

    def _unpack(self, gathered):
        return {n: g if n == "w_in" else jnp.concatenate([g[k] for k in range(4)], axis=_SHARD_AXIS[n] - 1)
                for n, g in zip(_GATHERED, gathered)}

    def weights(self, l):
        return self.full[l]

    def fwd_comm(self, l):
        return ("gather", self._shards(l + 1)) if l + 1 < DEPTH else None

    def fwd_done(self, l, got):
        if got:
            self.full[l + 1] = self._unpack(got)

    def push(self, l, grads):
        self.pending = (l, [(_w_in_slabs_from_wz(grads[n]) if n == "w_in" else
                             jnp.stack(jnp.split(grads[n], 4, axis=_SHARD_AXIS[n] - 1))).astype(GRAD_WIRE_DTYPE)
                            for n in _BIG])

    def bwd_comm(self):
        return ("exchange", self.pending[1]) if self.pending else None

    def bwd_done(self, got):
        if got:
            self.contrib[self.pending[0]] = got
            self.pending = None

    def finish(self):
        if self.pending:
            self.contrib[self.pending[0]] = _comm_alone("exchange", self.pending[1], "exchange_grads")
            self.pending = None


def kernel(x, p, norm_mix, w_in, conv_w, conv_b, fgate_bias, q_norm, k_norm, lb_logits, hgrn_norm, sgu_norm, spatial_w, spatial_b, w_up, merge_b, w_o, norm_ple, w_ple_gate, w_ple_proj, loss_target, m_norm_mix, m_w_in, m_conv_w, m_conv_b, m_fgate_bias, m_q_norm, m_k_norm, m_lb_logits, m_hgrn_norm, m_sgu_norm, m_spatial_w, m_spatial_b, m_w_up, m_merge_b, m_w_o, m_norm_ple, m_w_ple_gate, m_w_ple_proj, v_norm_mix, v_w_in, v_conv_w, v_conv_b, v_fgate_bias, v_q_norm, v_k_norm, v_lb_logits, v_hgrn_norm, v_sgu_norm, v_spatial_w, v_spatial_b, v_w_up, v_merge_b, v_w_o, v_norm_ple, v_w_ple_gate, v_w_ple_proj):
    loc = dict(locals())
    w = {n: loc[n] for n in _WEIGHTS}
    m = {n: loc["m_" + n] for n in _WEIGHTS}
    v = {n: loc["v_" + n] for n in _WEIGHTS}
    chip = 2 * lax.axis_index("x") + lax.axis_index("y")

    dist = _Dist(w)
    loss_blk, dx, grads = _local_step(x[0], p[:, 0], loss_target[0], w, dist)
    loss = lax.psum(loss_blk[0, 0], ("x", "y", "c"))

    flat = jnp.concatenate([grads[n].reshape(-1) for n in _SMALL])
    npad = (-flat.shape[0]) % 1024
    packed = jnp.concatenate([flat, jnp.zeros((npad,), F32)]).reshape(-1, 128)
    red = _allreduce_small(packed).reshape(-1)
    small, off = {}, 0
    for n in _SMALL:
        sz = int(np.prod(grads[n].shape))
        small[n] = red[off:off + sz].reshape(grads[n].shape)
        off += sz
    for n in ("conv_w", "merge_b"):
        ax = _SHARD_AXIS[n]
        width = small[n].shape[ax] // 4
        small[n] = lax.dynamic_slice_in_dim(small[n], chip * width, width, axis=ax)

    out_g, out_d, out_m, out_v = {}, {}, {}, {}
    for n in _WEIGHTS:
        shp = w[n].shape
        if n in _BIG:
            w3, m3, v3 = (a.reshape((DEPTH, -1, shp[-1])) for a in (w[n], m[n], v[n]))
            parts = [dist.contrib[l][_BIG.index(n)].reshape((8,) + w3.shape[1:]) for l in range(DEPTH)]
            g, d, nm, nv = _adamw(w3, m3, v3, parts, 8)
        else:
            g, d, nm, nv = _adamw(_as3d(w[n]), _as3d(m[n]), _as3d(v[n]), _as3d(small[n])[None], 1)
        out_g[n], out_d[n], out_m[n], out_v[n] = (a.reshape(shp) for a in (g, d, nm, nv))

    return (loss, dx[None], *[out_g[n] for n in _WEIGHTS], *[out_d[n] for n in _WEIGHTS],
            *[out_m[n] for n in _WEIGHTS], *[out_v[n] for n in _WEIGHTS])
```

```python
import functools

import numpy as np
import jax
import jax.numpy as jnp
from jax import lax
from jax.experimental import pallas as pl
from jax.experimental.pallas import tpu as pltpu

F32 = jnp.float32
MXU_DTYPE = jnp.bfloat16

D_MODEL = 1024
BW = 256
NH = 4
DH = 64
DEPTH = 4
NBR = 4
PLE = 256
CHUNK = 64
SCHUNK = 128
EPS = 1e-6
MASK_VALUE = -1e30
NLEV = 6

ADAM_LR, ADAM_B1, ADAM_B2, ADAM_EPS, ADAM_WD, ADAM_STEP = 0.001, 0.9, 0.999, 1e-08, 0.01, 10

ZM, ZA, ZB, ZC, ZD, ZF = 0, 4096, 5120, 6144, 7168, 8192
ZCOLS = 8320
_OA, _OB, _OF, _OC, _OD, _OM, _OEND = 0, 1024, 2048, 2052, 3076, 3844, 7940

VMEM_LIMIT = 56 * 1024 * 1024
ROW_TILE = 512
FLASH_TILE = 1024
GRAD_WIRE_DTYPE = jnp.bfloat16
MERGE_TILE = 256
MESH = pl.DeviceIdType.MESH
_ANY = pl.BlockSpec(memory_space=pl.ANY)


def _tile(n, pref):
    t = min(n, pref)
    assert n % t == 0, (n, t)
    return t


def _call(body, *, name, grid, in_specs, out_specs, out_shape, scratch=(), sem=None, aliases=None):
    return pl.pallas_call(
        functools.partial(body), name=name, grid=grid, in_specs=in_specs, out_specs=out_specs,
        out_shape=out_shape, scratch_shapes=list(scratch), input_output_aliases=aliases or {},
        compiler_params=pltpu.CompilerParams(dimension_semantics=sem, vmem_limit_bytes=VMEM_LIMIT))


def _mx(x):
    return x.astype(MXU_DTYPE)


def _dot(a, b):
    return jnp.dot(_mx(a), _mx(b), preferred_element_type=F32)


def _dot_nt(a, b):
    return lax.dot_general(_mx(a), _mx(b), (((1,), (1,)), ((), ())), preferred_element_type=F32)


def _dot_tn(a, b):
    return lax.dot_general(_mx(a), _mx(b), (((0,), (0,)), ((), ())), preferred_element_type=F32)


def _top16(x):
    u = lax.bitcast_convert_type(x, jnp.uint32) & jnp.uint32(0xFFFF0000)
    return lax.bitcast_convert_type(u, F32)


def _split3(x):
    hi = _top16(x)
    r1 = x - hi
    mid = _top16(r1)
    return _mx(hi), _mx(mid), _mx(r1 - mid)


def _dot_exact_lhs01(t, x):
    hi, mid, lo = _split3(x)
    t = _mx(t)
    return (jnp.dot(t, hi, preferred_element_type=F32) + jnp.dot(t, mid, preferred_element_type=F32)
            + jnp.dot(t, lo, preferred_element_type=F32))


def _sigmoid(x):
    return jax.nn.sigmoid(x)


def _silu(x):
    return x * _sigmoid(x)


def _dsilu(x):
    s = _sigmoid(x)
    return s * (1.0 + x * (1.0 - s))


def _log_sigmoid(x):
    return jnp.minimum(x, 0.0) - jnp.log(1.0 + jnp.exp(-jnp.abs(x)))


def _rms_fwd(x, g):
    r = lax.rsqrt(jnp.mean(x * x, axis=-1, keepdims=True) + EPS)
    return x * r * g


def _rms_bwd(x, g, dy):
    r = lax.rsqrt(jnp.mean(x * x, axis=-1, keepdims=True) + EPS)
    n = x * r
    t = dy * g
    dx = r * (t - n * jnp.mean(t * n, axis=-1, keepdims=True))
    return dx, dy * n


def _tri(n, upper=False):
    r = lax.broadcasted_iota(jnp.int32, (n, n), 0)
    c = lax.broadcasted_iota(jnp.int32, (n, n), 1)
    return jnp.where((c >= r) if upper else (r >= c), 1.0, 0.0).astype(F32)


def _hs(h):
    return slice(h * DH, (h + 1) * DH)


def _mm_nn(a, b, out_dtype, *, name, tn=None, tk=None, b_transposed=False):
    S, K = a.shape
    N = b.shape[0] if b_transposed else b.shape[1]
    ts = _tile(S, ROW_TILE)
    tn = _tile(N, tn or N)
    tk = _tile(K, tk or K)
    nk = K // tk
    b_spec = (pl.BlockSpec((tn, tk), lambda j, i, k: (j, k)) if b_transposed
              else pl.BlockSpec((tk, tn), lambda j, i, k: (k, j)))

    def body(a_ref, b_ref, o_ref, acc_ref):
        k = pl.program_id(2)
        part = _dot_nt(a_ref[...], b_ref[...]) if b_transposed else jnp.dot(a_ref[...], b_ref[...],
                                                                          preferred_element_type=F32)
        if nk == 1:
            o_ref[...] = part.astype(o_ref.dtype)
        else:
            @pl.when(k == 0)
            def _():
                acc_ref[...] = part

            @pl.when(k > 0)
            def _():
                acc_ref[...] += part

            @pl.when(k == nk - 1)
            def _():
                o_ref[...] = acc_ref[...].astype(o_ref.dtype)

    return _call(
        body, name=name, grid=(N // tn, S // ts, nk),
        in_specs=[pl.BlockSpec((ts, tk), lambda j, i, k: (i, k)), b_spec],
        out_specs=pl.BlockSpec((ts, tn), lambda j, i, k: (i, j)),
        out_shape=jax.ShapeDtypeStruct((S, N), out_dtype),
        scratch=[pltpu.VMEM((ts, tn) if nk > 1 else (8, 128), F32)],
        sem=("parallel", "parallel", "arbitrary"))(a, b)


def _mm_tn(x, y, *, name, tn=None, ycol=0, n=None):
    S, M = x.shape
    n = n or y.shape[1]
    ts = _tile(S, ROW_TILE)
    tn = _tile(n, tn or n)
    nj = n // tn

    def body(x_ref, y_ref, o_ref):
        @pl.when(pl.program_id(1) == 0)
        def _():
            o_ref[...] = jnp.zeros_like(o_ref)

        o_ref[...] += lax.dot_general(x_ref[...], y_ref[...], (((0,), (0,)), ((), ())), preferred_element_type=F32)

    return _call(
        body, name=name, grid=(nj, S // ts),
        in_specs=[pl.BlockSpec((ts, M), lambda j, s: (s, 0)), pl.BlockSpec((ts, tn), lambda j, s: (s, ycol * nj + j))],
        out_specs=pl.BlockSpec((M, tn), lambda j, s: (0, j)),
        out_shape=jax.ShapeDtypeStruct((M, n), F32),
        sem=("parallel", "arbitrary"))(x, y)


def _norm_fwd(x, g):
    S = x.shape[0]
    ts = _tile(S, ROW_TILE)

    def body(x_ref, g_ref, h_ref):
        h_ref[...] = _mx(_rms_fwd(x_ref[...], g_ref[...]))

    return _call(
        body, name="norm_fwd", grid=(S // ts,),
        in_specs=[pl.BlockSpec((ts, D_MODEL), lambda i: (i, 0)), pl.BlockSpec((1, D_MODEL), lambda i: (0, 0))],
        out_specs=pl.BlockSpec((ts, D_MODEL), lambda i: (i, 0)),
        out_shape=jax.ShapeDtypeStruct((S, D_MODEL), MXU_DTYPE), sem=("parallel",))(x, g)


def _norm_bwd(x, g, dh, dres):
    S = x.shape[0]
    ts = _tile(S, ROW_TILE)

    def body(x_ref, g_ref, dh_ref, dr_ref, dx_ref, st_ref):
        @pl.when(pl.program_id(0) == 0)
        def _():
            st_ref[...] = jnp.zeros_like(st_ref)

        dx, dgr = _rms_bwd(x_ref[...], g_ref[...], dh_ref[...])
        dx_ref[...] = dr_ref[...] + dx
        st_ref[0:1, :] += jnp.sum(dgr, axis=0, keepdims=True)

    row = pl.BlockSpec((ts, D_MODEL), lambda i: (i, 0))
    return _call(
        body, name="norm_bwd", grid=(S // ts,),
        in_specs=[row, pl.BlockSpec((1, D_MODEL), lambda i: (0, 0)), row, row],
        out_specs=[row, pl.BlockSpec((8, D_MODEL), lambda i: (0, 0))],
        out_shape=[jax.ShapeDtypeStruct((S, D_MODEL), F32), jax.ShapeDtypeStruct((8, D_MODEL), F32)],
        sem=("arbitrary",))(x, g, dh, dres)


def _loss_fwd_bwd(y, target):
    S = y.shape[0]
    ts = _tile(S, ROW_TILE)

    def body(y_ref, t_ref, l_ref, dy_ref):
        @pl.when(pl.program_id(0) == 0)
        def _():
            l_ref[...] = jnp.zeros_like(l_ref)

        err = y_ref[...] - t_ref[...]
        dy_ref[...] = err * (1.0 / D_MODEL)
        rowloss = jnp.mean(err * err, axis=-1, keepdims=True)
        l_ref[...] += 0.5 * jnp.sum(rowloss, axis=0, keepdims=True)

    row = pl.BlockSpec((ts, D_MODEL), lambda i: (i, 0))
    return _call(
        body, name="loss", grid=(S // ts,), in_specs=[row, row],
        out_specs=[pl.BlockSpec((8, 128), lambda i: (0, 0)), row],
        out_shape=[jax.ShapeDtypeStruct((8, 128), F32), jax.ShapeDtypeStruct((S, D_MODEL), F32)],
        sem=("arbitrary",))(y, target)


def _shift_down(x, k, halo, rows):
    y = pltpu.roll(x, k, 0)
    for j in range(k):
        y = jnp.where(rows == j, halo[8 - k + j:8 - k + j + 1, :], y)
    return y


def _shift_up(x, k, halo, rows, n):
    y = pltpu.roll(x, n - k, 0)
    for j in range(k):
        y = jnp.where(rows == n - k + j, halo[j:j + 1, :], y)
    return y


def _conv_parts(za, zh, cw, first, rows):
    ax, ab, ac, ag = za[:, 0:BW], za[:, BW:2 * BW], za[:, 2 * BW:3 * BW], za[:, 3 * BW:4 * BW]
    zz = ac * ax
    hz = jnp.where(first, 0.0, zh[:, 2 * BW:3 * BW] * zh[:, 0:BW])
    zz1 = _shift_down(zz, 1, hz, rows)
    zz2 = _shift_down(zz, 2, hz, rows)
    conv = zz2 * cw[0:1, :] + zz1 * cw[1:2, :] + zz * cw[2:3, :] + cw[3:4, :]
    return ax, ab, ac, ag, zz, zz1, zz2, conv


def _conv_fwd(z, cw):
    S = z.shape[0]
    ts = _tile(S, ROW_TILE)
    hb = ts // 8

    def body(za_ref, zh_ref, cw_ref, y_ref):
        i = pl.program_id(0)
        rows = lax.broadcasted_iota(jnp.int32, (ts, BW), 0)
        ax, ab, ac, ag, zz, zz1, zz2, conv = _conv_parts(za_ref[...], zh_ref[...], cw_ref[...], i == 0, rows)
        y_ref[...] = _mx(ab * conv * _silu(ag))

    return _call(
        body, name="conv_fwd", grid=(S // ts,),
        in_specs=[pl.BlockSpec((ts, 1024), lambda i: (i, ZA // 1024)),
                  pl.BlockSpec((8, 1024), lambda i: (jnp.maximum(i * hb - 1, 0), ZA // 1024)),
                  pl.BlockSpec((8, BW), lambda i: (0, 0))],
        out_specs=pl.BlockSpec((ts, BW), lambda i: (i, 0)),
        out_shape=jax.ShapeDtypeStruct((S, BW), MXU_DTYPE), sem=("parallel",))(z, z, cw)


def _conv_bwd(z, cw, dy, dz):
    S = z.shape[0]
    ts = _tile(S, ROW_TILE)
    hb = ts // 8
    nt = S // ts

    def body(za_ref, zh_ref, zn_ref, cw_ref, dy_ref, dyn_ref, dz_in, dz_ref, st_ref):
        i = pl.program_id(0)

        @pl.when(i == 0)
        def _():
            st_ref[...] = jnp.zeros_like(st_ref)

        cw = cw_ref[...]
        rows = lax.broadcasted_iota(jnp.int32, (ts, BW), 0)
        ax, ab, ac, ag, zz, zz1, zz2, conv = _conv_parts(za_ref[...], zh_ref[...], cw, i == 0, rows)
        dy = dy_ref[...]
        sg = _silu(ag)
        dc = dy * ab * sg
        zn = zn_ref[...]
        dcn = jnp.where(i == nt - 1, 0.0, dyn_ref[...] * zn[:, BW:2 * BW] * _silu(zn[:, 3 * BW:4 * BW]))
        dc1 = _shift_up(dc, 1, dcn, rows, ts)
        dc2 = _shift_up(dc, 2, dcn, rows, ts)
        dzz = dc * cw[2:3, :] + dc1 * cw[1:2, :] + dc2 * cw[0:1, :]
        dz_ref[:, 0:BW] = _mx(dzz * ac)
        dz_ref[:, BW:2 * BW] = _mx(dy * conv * sg)
        dz_ref[:, 2 * BW:3 * BW] = _mx(dzz * ax)
        dz_ref[:, 3 * BW:4 * BW] = _mx(dy * ab * conv * _dsilu(ag))
        st_ref[0:1, :] += jnp.sum(dc * zz2, axis=0, keepdims=True)
        st_ref[1:2, :] += jnp.sum(dc * zz1, axis=0, keepdims=True)
        st_ref[2:3, :] += jnp.sum(dc * zz, axis=0, keepdims=True)
        st_ref[3:4, :] += jnp.sum(dc, axis=0, keepdims=True)

    return _call(
        body, name="conv_bwd", grid=(nt,),
        in_specs=[pl.BlockSpec((ts, 1024), lambda i: (i, ZA // 1024)),
                  pl.BlockSpec((8, 1024), lambda i: (jnp.maximum(i * hb - 1, 0), ZA // 1024)),
                  pl.BlockSpec((8, 1024), lambda i: (jnp.minimum((i + 1) * hb, S // 8 - 1), ZA // 1024)),
                  pl.BlockSpec((8, BW), lambda i: (0, 0)),
                  pl.BlockSpec((ts, BW), lambda i: (i, 0)),
                  pl.BlockSpec((8, BW), lambda i: (jnp.minimum((i + 1) * hb, S // 8 - 1), 0)), _ANY],
        out_specs=[pl.BlockSpec((ts, 1024), lambda i: (i, ZA // 1024)), pl.BlockSpec((8, BW), lambda i: (0, 0))],
        out_shape=[jax.ShapeDtypeStruct((S, ZCOLS), MXU_DTYPE), jax.ShapeDtypeStruct((8, BW), F32)],
        sem=("arbitrary",), aliases={6: 0})(z, z, z, cw, dy, dy, dz)


AW = 128
_AUG = DH


def _split3_f32(x):
    hi = _top16(x)
    r1 = x - hi
    mid = _top16(r1)
    return hi, mid, r1 - mid


def _put_aug(ref, h, col, parts=None, const=None):
    base = h * AW + _AUG + col
    if parts is None:
        ref[:, base:base + 3] = jnp.full((ref.shape[0], 3), const, ref.dtype)
    else:
        for e, part in enumerate(parts):
            ref[:, base + e:base + e + 1] = part.astype(ref.dtype)


def _attn_prep_fwd(z, fb, gq, gk):
    S = z.shape[0]
    ts = _tile(S, ROW_TILE)

    def body(zb_ref, zf_ref, fb_ref, gq_ref, gk_ref, qa_ref, ka_ref, va_ref, cc_ref, carry):
        @pl.when(pl.program_id(0) == 0)
        def _():
            carry[...] = jnp.zeros_like(carry)

        zb = zb_ref[...]
        gqv, gkv = gq_ref[...], gk_ref[...]
        lf = _log_sigmoid(zf_ref[...] + fb_ref[...])
        cum = _dot_exact_lhs01(_tri(ts), lf) + carry[...]
        carry[...] = cum[ts - 1:ts, :]
        cc_ref[...] = cum
        pieces = _split3_f32(cum)
        for ref in (qa_ref, ka_ref, va_ref):
            ref[...] = jnp.zeros_like(ref)
        for h in range(NH):
            qa_ref[:, h * AW:h * AW + DH] = _mx(_rms_fwd(zb[:, _hs(h)], gqv) * (DH ** -0.5))
            ka_ref[:, h * AW:h * AW + DH] = _mx(_rms_fwd(zb[:, BW + h * DH:BW + (h + 1) * DH], gkv))
            va_ref[:, h * AW:h * AW + DH] = _mx(zb[:, 2 * BW + h * DH:2 * BW + (h + 1) * DH])
            _put_aug(qa_ref, h, 0, parts=[pc[:, h:h + 1] for pc in pieces])
            _put_aug(qa_ref, h, 3, const=1.0)
            _put_aug(ka_ref, h, 0, const=1.0)
            _put_aug(ka_ref, h, 3, parts=[-pc[:, h:h + 1] for pc in pieces])
            _put_aug(va_ref, h, 0, const=-1.0)

    row = lambda w: pl.BlockSpec((ts, w), lambda i: (i, 0))
    return _call(
        body, name="attn_prep_fwd", grid=(S // ts,),
        in_specs=[pl.BlockSpec((ts, 1024), lambda i: (i, ZB // 1024)), pl.BlockSpec((ts, 128), lambda i: (i, ZF // 128)),
                  pl.BlockSpec((1, 128), lambda i: (0, 0)), pl.BlockSpec((1, DH), lambda i: (0, 0)),
                  pl.BlockSpec((1, DH), lambda i: (0, 0))],
        out_specs=[row(NH * AW), row(NH * AW), row(NH * AW), row(128)],
        out_shape=[jax.ShapeDtypeStruct((S, NH * AW), MXU_DTYPE)] * 3 + [jax.ShapeDtypeStruct((S, 128), F32)],
        scratch=[pltpu.VMEM((1, 128), F32)], sem=("arbitrary",))(z, z, fb, gq, gk)


ROW_CHUNK = 256


def _fuse_comm(core, n_in, n_out, comm, n):
    nc = 0 if comm is None else len(comm[1])

    def body(*refs):
        cin, xin = refs[:n_in], refs[n_in:n_in + nc]
        a = n_in + nc
        cout, xout = refs[a:a + n_out], refs[a + n_out:a + n_out + nc]
        rest = refs[a + n_out + nc:]
        if nc == 0:
            core(*cin, *cout, *rest)
            return
        cscr, sems = rest[:-3], rest[-3:]
        i, j = pl.program_id(0), pl.program_id(1)
        phase = _COMM_PHASES[comm[0]]

        @pl.when((i == 0) & (j == 0))
        def _():
            phase(0, xin, xout, sems)

        core(*cin, *cout, *cscr)

        @pl.when((i == n - 1) & (j == 0))
        def _():
            phase(1, xin, xout, sems)

        @pl.when((i == n - 1) & (j == n - 1))
        def _():
            phase(2, xin, xout, sems)

    return body


def _comm_specs(comm):
    if comm is None:
        return [], [], [], []
    hbm = pl.BlockSpec(memory_space=pl.ANY)
    nc = len(comm[1])
    return [hbm] * nc, [hbm] * nc, _COMM_OUT[comm[0]](comm[1]), _comm_sems(comm[0], nc)


def _ah(h):
    return slice(h * AW, (h + 1) * AW)


def _ahd(h):
    return slice(h * AW, h * AW + DH)


def _causal(shape, row0, transposed=False):
    r = row0 + lax.broadcasted_iota(jnp.int32, shape, 0)
    c = lax.broadcasted_iota(jnp.int32, shape, 1)
    return (r <= c) if transposed else (r >= c)


def _flash_fwd(qa, ka, va, z, comm=None):
    S = qa.shape[0]
    t = _tile(S, FLASH_TILE)
    n = S // t
    rch = _tile(t, ROW_CHUNK)

    def core(q_ref, k_ref, v_ref, zb_ref, o_ref, lse_ref, y_ref, m_sc, l_sc, acc):
        i, j = pl.program_id(0), pl.program_id(1)

        @pl.when(j == 0)
        def _():
            m_sc[...] = jnp.full_like(m_sc, MASK_VALUE)
            l_sc[...] = jnp.zeros_like(l_sc)
            acc[...] = jnp.zeros_like(acc)

        def block(masked):
            for h in range(NH):
                for rc in range(t // rch):
                    rows = slice(rc * rch, (rc + 1) * rch)
                    s = _dot_nt(q_ref[rows, _ah(h)], k_ref[:, _ah(h)])
                    if masked:
                        s = jnp.where(_causal(s.shape, rc * rch), s, MASK_VALUE)
                    m_old = m_sc[h, rows, :]
                    m_new = jnp.maximum(m_old, jnp.max(s, axis=-1, keepdims=True))
                    p = jnp.exp(s - m_new)
                    alpha = jnp.exp(m_old - m_new)
                    l_sc[h, rows, :] = alpha * l_sc[h, rows, :] + jnp.sum(p, axis=-1, keepdims=True)
                    acc[rows, _hs(h)] = alpha * acc[rows, _hs(h)] + _dot(p, v_ref[:, _ahd(h)])
                    m_sc[h, rows, :] = m_new

        @pl.when(j < i)
        def _():
            block(False)

        @pl.when(j == i)
        def _():
            block(True)
            lse_ref[...] = jnp.zeros_like(lse_ref)
            for h in range(NH):
                o_ref[:, _hs(h)] = acc[:, _hs(h)] / l_sc[h]
                lse_ref[:, h:h + 1] = m_sc[h] + jnp.log(l_sc[h])
            y_ref[...] = _mx(o_ref[...] * _silu(zb_ref[:, 3 * BW:4 * BW]))

    qspec = lambda w: pl.BlockSpec((t, w), lambda i, j: (i, 0))
    kspec = lambda w: pl.BlockSpec((t, w), lambda i, j: (jnp.minimum(j, i), 0))
    xin, xout, xshape, xsem = _comm_specs(comm)
    res = _call(
        _fuse_comm(core, 4, 3, comm, n), name="flash_fwd", grid=(n, n),
        in_specs=[qspec(NH * AW), kspec(NH * AW), kspec(NH * AW),
                  pl.BlockSpec((t, 1024), lambda i, j: (i, ZB // 1024))] + xin,
        out_specs=[qspec(BW), qspec(128), qspec(BW)] + xout,
        out_shape=[jax.ShapeDtypeStruct((S, BW), F32), jax.ShapeDtypeStruct((S, 128), F32),
                   jax.ShapeDtypeStruct((S, BW), MXU_DTYPE)] + xshape,
        scratch=[pltpu.VMEM((NH, t, 1), F32), pltpu.VMEM((NH, t, 1), F32), pltpu.VMEM((t, BW), F32)] + xsem,
        sem=("arbitrary", "arbitrary"))(qa, ka, va, z, *(comm[1] if comm else []))
    return res[0], res[1], res[2], list(res[3:])


def _attn_gate_bwd(dy, o, z, qa, cc, lse):
    S = dy.shape[0]
    ts = _tile(S, ROW_TILE)

    def body(dy_ref, o_ref, zb_ref, qa_ref, cc_ref, lse_ref, dg_ref, qb_ref, doa_ref):
        g = zb_ref[:, 3 * BW:4 * BW]
        dy, o = dy_ref[...], o_ref[...]
        do = dy * _silu(g)
        dg_ref[...] = _mx(dy * o * _dsilu(g))
        qb_ref[...] = qa_ref[...]
        doa_ref[...] = jnp.zeros_like(doa_ref)
        shifted = _split3_f32(cc_ref[...] - lse_ref[...])
        for h in range(NH):
            doh = do[:, _hs(h)]
            doa_ref[:, _ahd(h)] = _mx(doh)
            delta = jnp.sum(doh * o[:, _hs(h)], axis=-1, keepdims=True)
            _put_aug(doa_ref, h, 0, parts=_split3_f32(delta))
            _put_aug(qb_ref, h, 0, parts=[pc[:, h:h + 1] for pc in shifted])

    row = lambda w: pl.BlockSpec((ts, w), lambda i: (i, 0))
    return _call(
        body, name="attn_gate_bwd", grid=(S // ts,),
        in_specs=[pl.BlockSpec((ts, BW), lambda i: (i, 1)), row(BW), pl.BlockSpec((ts, 1024), lambda i: (i, ZB // 1024)),
                  row(NH * AW), row(128), row(128)],
        out_specs=[row(BW), row(NH * AW), row(NH * AW)],
        out_shape=[jax.ShapeDtypeStruct((S, BW), MXU_DTYPE), jax.ShapeDtypeStruct((S, NH * AW), MXU_DTYPE),
                   jax.ShapeDtypeStruct((S, NH * AW), MXU_DTYPE)],
        sem=("parallel",))(dy, o, z, qa, cc, lse)


def _aug_value(ref, h, rows=slice(None)):
    base = h * AW + _AUG
    x = ref[rows, base:base + 3].astype(F32)
    return x[:, 0:1] + x[:, 1:2] + x[:, 2:3]


def _flash_bwd_dq(qb, ka, va, doa, comm=None):
    S = qb.shape[0]
    t = _tile(S, FLASH_TILE)
    n = S // t
    rch = _tile(t, ROW_CHUNK)

    def core(q_ref, k_ref, v_ref, do_ref, dq_ref, do2_ref, dr_sc):
        i, j = pl.program_id(0), pl.program_id(1)

        @pl.when(j == 0)
        def _():
            dq_ref[...] = jnp.zeros_like(dq_ref)
            dr_sc[...] = jnp.zeros_like(dr_sc)

        def block(masked):
            for h in range(NH):
                for rc in range(t // rch):
                    rows = slice(rc * rch, (rc + 1) * rch)
                    p = jnp.exp(_dot_nt(q_ref[rows, _ah(h)], k_ref[:, _ah(h)]))
                    if masked:
                        p = jnp.where(_causal(p.shape, rc * rch), p, 0.0)
                    ds = p * _dot_nt(do_ref[rows, _ah(h)], v_ref[:, _ah(h)])
                    dq_ref[rows, _hs(h)] += _dot(ds, k_ref[:, _ahd(h)])
                    dr_sc[h, rows, :] += jnp.sum(ds, axis=-1, keepdims=True)

        @pl.when(j < i)
        def _():
            block(False)

        @pl.when(j == i)
        def _():
            block(True)
            do2_ref[...] = do_ref[...]
            for h in range(NH):
                _put_aug(do2_ref, h, 0, parts=_split3_f32(_aug_value(do_ref, h) + dr_sc[h]))

    qspec = lambda w: pl.BlockSpec((t, w), lambda i, j: (i, 0))
    kspec = lambda w: pl.BlockSpec((t, w), lambda i, j: (jnp.minimum(j, i), 0))
    xin, xout, xshape, xsem = _comm_specs(comm)
    res = _call(
        _fuse_comm(core, 4, 2, comm, n), name="flash_bwd_dq", grid=(n, n),
        in_specs=[qspec(NH * AW), kspec(NH * AW), kspec(NH * AW), qspec(NH * AW)] + xin,
        out_specs=[qspec(BW), qspec(NH * AW)] + xout,
        out_shape=[jax.ShapeDtypeStruct((S, BW), F32), jax.ShapeDtypeStruct((S, NH * AW), MXU_DTYPE)] + xshape,
        scratch=[pltpu.VMEM((NH, t, 1), F32)] + xsem,
        sem=("arbitrary", "arbitrary"))(qb, ka, va, doa, *(comm[1] if comm else []))
    return res[0], res[1], list(res[2:])


def _flash_bwd_dkv(qb, ka, va, doa):
    S = qb.shape[0]
    t = _tile(S, FLASH_TILE)
    n = S // t

    def body(q_ref, k_ref, v_ref, do_ref, dk_ref, dv_ref, dc_ref):
        j, i = pl.program_id(0), pl.program_id(1)

        @pl.when(i == 0)
        def _():
            dk_ref[...] = jnp.zeros_like(dk_ref)
            dv_ref[...] = jnp.zeros_like(dv_ref)
            dc_ref[...] = jnp.zeros_like(dc_ref)

        def block(masked):
            for h in range(NH):
                pt = jnp.exp(_dot_nt(k_ref[:, _ah(h)], q_ref[:, _ah(h)]))
                if masked:
                    pt = jnp.where(_causal(pt.shape, 0, transposed=True), pt, 0.0)
                dst = pt * _dot_nt(v_ref[:, _ah(h)], do_ref[:, _ah(h)])
                dv_ref[:, _hs(h)] += _dot(pt, do_ref[:, _ahd(h)])
                dk_ref[:, _hs(h)] += _dot(dst, q_ref[:, _ahd(h)])
                dc_ref[:, h:h + 1] += -jnp.sum(dst, axis=-1, keepdims=True)

        @pl.when(i > j)
        def _():
            block(False)

        @pl.when(i == j)
        def _():
            block(True)

    qspec = lambda w: pl.BlockSpec((t, w), lambda j, i: (jnp.maximum(i, j), 0))
    kspec = lambda w: pl.BlockSpec((t, w), lambda j, i: (j, 0))
    return _call(
        body, name="flash_bwd_dkv", grid=(n, n),
        in_specs=[qspec(NH * AW), kspec(NH * AW), kspec(NH * AW), qspec(NH * AW)],
        out_specs=[kspec(BW), kspec(BW), kspec(128)],
        out_shape=[jax.ShapeDtypeStruct((S, BW), F32), jax.ShapeDtypeStruct((S, BW), F32),
                   jax.ShapeDtypeStruct((S, 128), F32)],
        sem=("parallel", "arbitrary"))(qb, ka, va, doa)


def _attn_prep_bwd(z, fb, gq, gk, dq, dk, dv, dgate, dcc, dz):
    S = z.shape[0]
    ts = _tile(S, ROW_TILE)
    nt = S // ts

    def body(zb_ref, zf_ref, fb_ref, gq_ref, gk_ref, dq_ref, dk_ref, dv_ref, dg_ref, dcc_ref, dz_in, dzb_ref, dzf_ref,
             st_ref, carry):
        @pl.when(pl.program_id(0) == 0)
        def _():
            carry[...] = jnp.zeros_like(carry)
            st_ref[...] = jnp.zeros_like(st_ref)

        zb = zb_ref[...]
        gqv, gkv = gq_ref[...], gk_ref[...]
        dqv, dkv = dq_ref[...], dk_ref[...]
        sq = jnp.zeros((1, DH), F32)
        sk = jnp.zeros((1, DH), F32)
        for h in range(NH):
            dx, dgr = _rms_bwd(zb[:, _hs(h)], gqv, dqv[:, _hs(h)] * (DH ** -0.5))
            dzb_ref[:, _hs(h)] = _mx(dx)
            sq = sq + jnp.sum(dgr, axis=0, keepdims=True)
            ks = slice(BW + h * DH, BW + (h + 1) * DH)
            dx, dgr = _rms_bwd(zb[:, ks], gkv, dkv[:, _hs(h)])
            dzb_ref[:, ks] = _mx(dx)
            sk = sk + jnp.sum(dgr, axis=0, keepdims=True)
        dzb_ref[:, 2 * BW:3 * BW] = _mx(dv_ref[...])
        dzb_ref[:, 3 * BW:4 * BW] = dg_ref[...]
        dc = dcc_ref[...]
        dlf = _dot_exact_lhs01(_tri(ts, upper=True), dc) + carry[...]
        carry[...] = dlf[0:1, :]
        dfz = dlf * _sigmoid(-(zf_ref[...] + fb_ref[...]))
        dzf_ref[...] = _mx(dfz)
        st_ref[0:1, 0:DH] += sq
        st_ref[1:2, 0:DH] += sk
        st_ref[2:3, :] += jnp.sum(dfz, axis=0, keepdims=True)

    rev = lambda w, c=0: pl.BlockSpec((ts, w), lambda i: (nt - 1 - i, c))
    one = lambda w: pl.BlockSpec((1, w), lambda i: (0, 0))
    return _call(
        body, name="attn_prep_bwd", grid=(nt,),
        in_specs=[rev(1024, ZB // 1024), rev(128, ZF // 128), one(128), one(DH), one(DH),
                  rev(BW), rev(BW), rev(BW), rev(BW), rev(128), _ANY],
        out_specs=[rev(1024, ZB // 1024), rev(128), pl.BlockSpec((8, 128), lambda i: (0, 0))],
        out_shape=[jax.ShapeDtypeStruct((S, ZCOLS), MXU_DTYPE), jax.ShapeDtypeStruct((S, 128), MXU_DTYPE),
                   jax.ShapeDtypeStruct((8, 128), F32)],
        scratch=[pltpu.VMEM((1, 128), F32)], sem=("arbitrary",), aliases={10: 0})(z, z, fb, gq, gk, dq, dk, dv, dgate, dcc, dz)


def _put_fgate_cols(dzf, dz):
    S = dzf.shape[0]
    ts = _tile(S, ROW_TILE)

    def body(f_ref, dz_in, o_ref):
        o_ref[...] = f_ref[...]

    return _call(
        body, name="put_fgate_cols", grid=(S // ts,),
        in_specs=[pl.BlockSpec((ts, 128), lambda i: (i, 0)), _ANY],
        out_specs=pl.BlockSpec((ts, 128), lambda i: (i, ZF // 128)),
        out_shape=jax.ShapeDtypeStruct((S, ZCOLS), MXU_DTYPE), sem=("parallel",), aliases={1: 0})(dzf, dz)


def _hgrn_consts():
    C = CHUNK
    t = np.arange(C)[:, None]
    j = np.arange(C)[None, :]
    mats = [(j <= t)]
    tq, tk, masks = [], [], []
    for lev in range(NLEV):
        m = C >> (lev + 1)
        blk, pos = t // (2 * m), t % (2 * m)
        mid = blk * 2 * m + m - 1
        tq.append((pos >= m) & (j > mid) & (j <= t))
        tk.append((pos < m) & (j > t) & (j <= mid))
        sblk, spos = j // (2 * m), j % (2 * m)
        masks.append((blk == sblk) & (pos >= m) & (spos < m))
    masks.append(t == j)
    tmat = np.concatenate(mats + tq + tk, axis=0).astype(np.float32)
    return tmat, np.stack(masks).astype(np.float32)


def _hgrn_chunk_fwd(zc, lb, tmat):
    cq, cf, ci = zc[:, 0:BW], zc[:, BW:2 * BW], zc[:, 2 * BW:3 * BW]
    q = _silu(cq)
    sg = _sigmoid(cf)
    g = lb + (1.0 - lb) * sg
    lg = jnp.log(g)
    kf = (1.0 - lb) * _sigmoid(-cf)
    e = _dot_exact_lhs01(tmat, lg)
    b = e[0:CHUNK]
    blast = b[CHUNK - 1:CHUNK, :]
    return cq, cf, q, sg, g, kf, ci, e, b, blast


def _round_mx(x):
    if MXU_DTYPE != jnp.bfloat16:
        return x
    u = lax.bitcast_convert_type(x, jnp.uint32)
    u = (u + jnp.uint32(0x7FFF) + ((u >> 16) & jnp.uint32(1))) & jnp.uint32(0xFFFF0000)
    return lax.bitcast_convert_type(u, F32)


def _hgrn_scores(q, kf, e, masks, h):
    qh, kh = q[:, _hs(h)], kf[:, _hs(h)]
    ql, kl = [], []
    a = None
    for lev in range(NLEV + 1):
        if lev < NLEV:
            eq = jnp.exp(e[(1 + lev) * CHUNK:(2 + lev) * CHUNK, _hs(h)])
            ek = jnp.exp(e[(1 + NLEV + lev) * CHUNK:(2 + NLEV + lev) * CHUNK, _hs(h)])
            ql.append((_round_mx(qh * eq), eq))
            kl.append((_round_mx(kh * ek), ek))
        else:
            ql.append((_round_mx(qh), None))
            kl.append((_round_mx(kh), None))
        term = masks[lev] * _dot_nt(ql[-1][0], kl[-1][0])
        a = term if a is None else a + term
    return a, ql, kl


def _hgrn_fwd(z, lb, gain, tmat, masks):
    S = z.shape[0]
    ts = _tile(S, ROW_TILE)
    nc = ts // CHUNK

    def body(zc_ref, lb_ref, gn_ref, tm_ref, mk_ref, o_ref, st_ref, y_ref, state):
        @pl.when(pl.program_id(0) == 0)
        def _():
            state[...] = jnp.zeros_like(state)

        lbv, gn, tm, mk = lb_ref[...], gn_ref[...], tm_ref[...], mk_ref[...]

        def chunk(c, carry):
            r0 = pl.multiple_of(c * CHUNK, CHUNK)
            zc = zc_ref[pl.ds(r0, CHUNK), :]
            cq, cf, q, sg, g, kf, v, e, b, blast = _hgrn_chunk_fwd(zc, lbv, tm)
            qe = q * jnp.exp(b)
            kd = kf * jnp.exp(blast - b)
            st_ref[pl.ds(r0, CHUNK), :] = state[...]
            for h in range(NH):
                sth = state[:, _hs(h)]
                a, _, _ = _hgrn_scores(q, kf, e, mk, h)
                oh = _dot_nt(qe[:, _hs(h)], sth) + _dot(a, v[:, _hs(h)])
                state[:, _hs(h)] = sth * jnp.exp(blast[:, _hs(h)]) + _dot_tn(v[:, _hs(h)], kd[:, _hs(h)])
                o_ref[pl.ds(r0, CHUNK), _hs(h)] = oh
                yn = _rms_fwd(oh, gn[:, _hs(h)])
                y_ref[pl.ds(r0, CHUNK), _hs(h)] = _mx(yn * _silu(zc[:, 3 * BW + h * DH:3 * BW + (h + 1) * DH]))
            return carry

        lax.fori_loop(0, nc, chunk, 0, unroll=2)

    row = pl.BlockSpec((ts, BW), lambda i: (i, 0))
    one = pl.BlockSpec((1, BW), lambda i: (0, 0))
    return _call(
        body, name="hgrn_fwd", grid=(S // ts,),
        in_specs=[pl.BlockSpec((ts, 1024), lambda i: (i, ZC // 1024)), one, one,
                  pl.BlockSpec(tmat.shape, lambda i: (0, 0)), pl.BlockSpec(masks.shape, lambda i: (0, 0, 0))],
        out_specs=[row, row, row],
        out_shape=[jax.ShapeDtypeStruct((S, BW), F32), jax.ShapeDtypeStruct((S, BW), F32),
                   jax.ShapeDtypeStruct((S, BW), MXU_DTYPE)],
        scratch=[pltpu.VMEM((CHUNK, BW), F32)], sem=("arbitrary",))(z, lb, gain, tmat, masks)


def _hgrn_bwd(z, lb, gain, tmat, masks, o_pre, states, dy, dz):
    S = z.shape[0]
    ts = _tile(S, ROW_TILE)
    nt = S // ts
    nc = ts // CHUNK

    def body(zc_ref, lb_ref, gn_ref, tm_ref, mk_ref, o_ref, st_ref, dy_ref, dz_in, dz_ref, stat_ref, dstate):
        @pl.when(pl.program_id(0) == 0)
        def _():
            dstate[...] = jnp.zeros_like(dstate)
            stat_ref[...] = jnp.zeros_like(stat_ref)

        lbv, gn, tm, mk = lb_ref[...], gn_ref[...], tm_ref[...], mk_ref[...]
        upper = _tri(CHUNK, upper=True)
        lower_strict = 1.0 - upper

        def chunk(cc, carry):
            c = nc - 1 - cc
            r0 = pl.multiple_of(c * CHUNK, CHUNK)
            zc = zc_ref[pl.ds(r0, CHUNK), :]
            cq, cf, q, sg, g, kf, v, e, b, blast = _hgrn_chunk_fwd(zc, lbv, tm)
            eb = jnp.exp(b)
            ebl = jnp.exp(blast - b)
            qe = q * eb
            kd = kf * ebl
            o = o_ref[pl.ds(r0, CHUNK), :]
            dyv = dy_ref[pl.ds(r0, CHUNK), :]
            stp = st_ref[pl.ds(r0, CHUNK), :]
            cg = zc[:, 3 * BW:4 * BW]
            sgate = _silu(cg)
            dq_parts, dk_parts, dv_parts, dcg_parts = [], [], [], []
            dgain, up_parts, lo_parts, const_parts = [], [], [], []
            for h in range(NH):
                hs = _hs(h)
                oh = o[:, hs]
                r = lax.rsqrt(jnp.mean(oh * oh, axis=-1, keepdims=True) + EPS)
                nrm = oh * r
                dyn = dyv[:, hs] * sgate[:, hs]
                dcg_parts.append(dyv[:, hs] * nrm * gn[:, hs] * _dsilu(cg[:, hs]))
                dgain.append(jnp.sum(dyn * nrm, axis=0, keepdims=True))
                tt = dyn * gn[:, hs]
                doh = r * (tt - nrm * jnp.mean(tt * nrm, axis=-1, keepdims=True))
                a, ql, kl = _hgrn_scores(q, kf, e, mk, h)
                da = _dot_nt(doh, v[:, hs])
                dsth = dstate[:, hs]
                ebh = jnp.exp(blast[:, hs])
                dv_parts.append(_dot_tn(a, doh) + _dot_nt(kd[:, hs], dsth))
                dq_inter = eb[:, hs] * _dot(doh, stp[:, hs])
                dk_state = ebl[:, hs] * _dot(v[:, hs], dsth)
                dqh, dkh, gh = dq_inter, dk_state, None
                for lev in range(NLEV + 1):
                    dal = mk[lev] * da
                    xq = _dot(dal, kl[lev][0])
                    yk = _dot_tn(dal, ql[lev][0])
                    gterm = ql[lev][0] * xq - kl[lev][0] * yk
                    gh = gterm if gh is None else gh + gterm
                    dqh = dqh + (xq if lev == NLEV else ql[lev][1] * xq)
                    dkh = dkh + (yk if lev == NLEV else kl[lev][1] * yk)
                up_parts.append(gh + q[:, hs] * dq_inter)
                lo_parts.append(kf[:, hs] * dk_state)
                const_parts.append(jnp.sum(dsth * stp[:, hs], axis=0, keepdims=True) * ebh)
                dstate[:, hs] = dsth * ebh + _dot_tn(doh, qe[:, hs])
                dq_parts.append(dqh)
                dk_parts.append(dkh)
            dq = jnp.concatenate(dq_parts, axis=1)
            dk = jnp.concatenate(dk_parts, axis=1)
            dlg = (_dot_exact_lhs01(upper, jnp.concatenate(up_parts, axis=1))
                   + _dot_exact_lhs01(lower_strict, jnp.concatenate(lo_parts, axis=1))
                   + jnp.concatenate(const_parts, axis=1))
            dsg = sg * (1.0 - sg)
            dz_ref[pl.ds(r0, CHUNK), 0:BW] = _mx(dq * _dsilu(cq))
            dz_ref[pl.ds(r0, CHUNK), BW:2 * BW] = _mx((dlg / g - dk) * (1.0 - lbv) * dsg)
            dz_ref[pl.ds(r0, CHUNK), 2 * BW:3 * BW] = _mx(jnp.concatenate(dv_parts, axis=1))
            dz_ref[pl.ds(r0, CHUNK), 3 * BW:4 * BW] = _mx(jnp.concatenate(dcg_parts, axis=1))
            stat_ref[0:1, :] += jnp.concatenate(dgain, axis=1)
            stat_ref[1:2, :] += jnp.sum((dlg / g - dk) * (1.0 - sg), axis=0, keepdims=True)
            return carry

        lax.fori_loop(0, nc, chunk, 0, unroll=2)

    rev = lambda w, c=0: pl.BlockSpec((ts, w), lambda i: (nt - 1 - i, c))
    one = pl.BlockSpec((1, BW), lambda i: (0, 0))
    return _call(
        body, name="hgrn_bwd", grid=(nt,),
        in_specs=[rev(1024, ZC // 1024), one, one, pl.BlockSpec(tmat.shape, lambda i: (0, 0)),
                  pl.BlockSpec(masks.shape, lambda i: (0, 0, 0)), rev(BW), rev(BW), rev(BW, 2), _ANY],
        out_specs=[rev(1024, ZC // 1024), pl.BlockSpec((8, BW), lambda i: (0, 0))],
        out_shape=[jax.ShapeDtypeStruct((S, ZCOLS), MXU_DTYPE), jax.ShapeDtypeStruct((8, BW), F32)],
        scratch=[pltpu.VMEM((CHUNK, BW), F32)],
        sem=("arbitrary",), aliases={8: 0})(z, lb, gain, tmat, masks, o_pre, states, dy, dz)


def _lower_bounds_fwd(lb_logits):
    def body(l_ref, o_ref):
        l = l_ref[...]
        m = jnp.max(l, axis=0, keepdims=True)
        ex = jnp.exp(l - m)
        p = ex / jnp.sum(ex, axis=0, keepdims=True)
        cs = p[0:1, :]
        o_ref[0:1, :] = jnp.clip(cs - p[0:1, :], 0.0, 1.0)
        for d in range(1, DEPTH):
            cs = cs + p[d:d + 1, :]
            o_ref[d:d + 1, :] = jnp.clip(cs - p[0:1, :], 0.0, 1.0)

    full = pl.BlockSpec((DEPTH, BW), lambda: (0, 0))
    return _call(body, name="lower_bounds_fwd", grid=(), in_specs=[full], out_specs=full,
                 out_shape=jax.ShapeDtypeStruct((DEPTH, BW), F32))(lb_logits)


def _lower_bounds_bwd(lb_logits, dlow):
    def body(l_ref, d_ref, o_ref):
        l = l_ref[...]
        m = jnp.max(l, axis=0, keepdims=True)
        ex = jnp.exp(l - m)
        p = ex / jnp.sum(ex, axis=0, keepdims=True)
        dl = d_ref[...]
        cs = p[0:1, :]
        dcs = []
        for d in range(DEPTH):
            if d > 0:
                cs = cs + p[d:d + 1, :]
            val = cs - p[0:1, :]
            dcs.append(jnp.where((val > 0.0) & (val < 1.0), dl[d:d + 1, :], 0.0))
        total = dcs[0] + dcs[1] + dcs[2] + dcs[3]
        dp = []
        for j in range(DEPTH):
            s = dcs[j]
            for d in range(j + 1, DEPTH):
                s = s + dcs[d]
            dp.append(s - total if j == 0 else s)
        inner = p[0:1, :] * dp[0]
        for j in range(1, DEPTH):
            inner = inner + p[j:j + 1, :] * dp[j]
        for j in range(DEPTH):
            o_ref[j:j + 1, :] = p[j:j + 1, :] * (dp[j] - inner)

    full = pl.BlockSpec((DEPTH, BW), lambda: (0, 0))
    return _call(body, name="lower_bounds_bwd", grid=(), in_specs=[full, full], out_specs=full,
                 out_shape=jax.ShapeDtypeStruct((DEPTH, BW), F32))(lb_logits, dlow)


def _sgu_fwd(z, gv, ws, bs):
    S = z.shape[0]
    ts = _tile(S, ROW_TILE)
    nc = ts // SCHUNK

    def body(zd_ref, gv_ref, ws_ref, bs_ref, y_ref):
        gvv, bsv = gv_ref[...], bs_ref[...]
        tril = _tri(SCHUNK)
        for c in range(nc):
            rs = slice(c * SCHUNK, (c + 1) * SCHUNK)
            zd = zd_ref[rs, :]
            for h in range(NH):
                vn = _rms_fwd(zd[:, BW + h * DH:BW + (h + 1) * DH], gvv[:, _hs(h)])
                s = _dot(ws_ref[h] * tril, vn) + bsv[:, h:h + 1]
                y_ref[rs, _hs(h)] = _mx(zd[:, _hs(h)] * s * _silu(zd[:, 2 * BW + h * DH:2 * BW + (h + 1) * DH]))

    return _call(
        body, name="sgu_fwd", grid=(S // ts,),
        in_specs=[pl.BlockSpec((ts, 1024), lambda i: (i, ZD // 1024)), pl.BlockSpec((1, BW), lambda i: (0, 0)),
                  pl.BlockSpec((NH, SCHUNK, SCHUNK), lambda i: (0, 0, 0)), pl.BlockSpec((SCHUNK, 128), lambda i: (0, 0))],
        out_specs=pl.BlockSpec((ts, BW), lambda i: (i, 0)),
        out_shape=jax.ShapeDtypeStruct((S, BW), MXU_DTYPE), sem=("parallel",))(z, gv, ws, bs)


def _sgu_bwd(z, gv, ws, bs, dy, dz):
    S = z.shape[0]
    ts = _tile(S, ROW_TILE)
    nc = ts // SCHUNK

    def body(zd_ref, gv_ref, ws_ref, bs_ref, dy_ref, dz_in, dz_ref, dws_ref, dbs_ref, st_ref):
        @pl.when(pl.program_id(0) == 0)
        def _():
            dws_ref[...] = jnp.zeros_like(dws_ref)
            dbs_ref[...] = jnp.zeros_like(dbs_ref)
            st_ref[...] = jnp.zeros_like(st_ref)

        gvv, bsv = gv_ref[...], bs_ref[...]
        tril = _tri(SCHUNK)
        dz_ref[:, 3 * BW:4 * BW] = jnp.zeros((ts, BW), MXU_DTYPE)
        for c in range(nc):
            rs = slice(c * SCHUNK, (c + 1) * SCHUNK)
            zd = zd_ref[rs, :]
            dyv = dy_ref[rs, :]
            for h in range(NH):
                hs = _hs(h)
                u = zd[:, hs]
                vraw = zd[:, BW + h * DH:BW + (h + 1) * DH]
                gt = zd[:, 2 * BW + h * DH:2 * BW + (h + 1) * DH]
                gvh = gvv[:, hs]
                vn = _rms_fwd(vraw, gvh)
                wm = ws_ref[h] * tril
                s = _dot(wm, vn) + bsv[:, h:h + 1]
                sil = _silu(gt)
                d = dyv[:, hs]
                ds = d * u * sil
                dz_ref[rs, hs] = _mx(d * s * sil)
                dz_ref[rs, 2 * BW + h * DH:2 * BW + (h + 1) * DH] = _mx(d * u * s * _dsilu(gt))
                dws_ref[h] += tril * _dot_nt(ds, vn)
                dbs_ref[:, h:h + 1] += jnp.sum(ds, axis=-1, keepdims=True)
                dvn = _dot_tn(wm, ds)
                dx, dgr = _rms_bwd(vraw, gvh, dvn)
                dz_ref[rs, BW + h * DH:BW + (h + 1) * DH] = _mx(dx)
                st_ref[0:1, hs] += jnp.sum(dgr, axis=0, keepdims=True)

    return _call(
        body, name="sgu_bwd", grid=(S // ts,),
        in_specs=[pl.BlockSpec((ts, 1024), lambda i: (i, ZD // 1024)), pl.BlockSpec((1, BW), lambda i: (0, 0)),
                  pl.BlockSpec((NH, SCHUNK, SCHUNK), lambda i: (0, 0, 0)), pl.BlockSpec((SCHUNK, 128), lambda i: (0, 0)),
                  pl.BlockSpec((ts, BW), lambda i: (i, 3)), _ANY],
        out_specs=[pl.BlockSpec((ts, 1024), lambda i: (i, ZD // 1024)), pl.BlockSpec((NH, SCHUNK, SCHUNK), lambda i: (0, 0, 0)),
                   pl.BlockSpec((SCHUNK, 128), lambda i: (0, 0)), pl.BlockSpec((8, BW), lambda i: (0, 0))],
        out_shape=[jax.ShapeDtypeStruct((S, ZCOLS), MXU_DTYPE), jax.ShapeDtypeStruct((NH, SCHUNK, SCHUNK), F32),
                   jax.ShapeDtypeStruct((SCHUNK, 128), F32), jax.ShapeDtypeStruct((8, BW), F32)],
        sem=("arbitrary",), aliases={5: 0})(z, gv, ws, bs, dy, dz)


def _merge_fwd(x, z, ys, wup, mb, wo):
    S = x.shape[0]
    ts = _tile(S, MERGE_TILE)

    def body(x_ref, zm_ref, ya_ref, yb_ref, yc_ref, yd_ref, wup_ref, mb_ref, wo_ref, x1_ref, mg_ref):
        yrefs = (ya_ref, yb_ref, yc_ref, yd_ref)
        mbv = mb_ref[...]
        merged = None
        for b in range(NBR):
            cs = slice(b * D_MODEL, (b + 1) * D_MODEL)
            term = _sigmoid(zm_ref[:, cs] + mbv[b:b + 1, :]) * jnp.dot(yrefs[b][...], wup_ref[b],
                                                                      preferred_element_type=F32)
            merged = term if merged is None else merged + term
        mg = _mx(merged)
        mg_ref[...] = mg
        x1_ref[...] = x_ref[...] + jnp.dot(mg, wo_ref[...], preferred_element_type=F32)

    row = lambda w: pl.BlockSpec((ts, w), lambda i: (i, 0))
    return _call(
        body, name="merge_fwd", grid=(S // ts,),
        in_specs=[row(D_MODEL), pl.BlockSpec((ts, 4096), lambda i: (i, 0)), row(BW), row(BW), row(BW), row(BW),
                  pl.BlockSpec((NBR, BW, D_MODEL), lambda i: (0, 0, 0)), pl.BlockSpec((NBR, D_MODEL), lambda i: (0, 0)),
                  pl.BlockSpec((D_MODEL, D_MODEL), lambda i: (0, 0))],
        out_specs=[row(D_MODEL), row(D_MODEL)],
        out_shape=[jax.ShapeDtypeStruct((S, D_MODEL), F32), jax.ShapeDtypeStruct((S, D_MODEL), MXU_DTYPE)],
        sem=("parallel",))(x, z, *ys, wup, mb, wo)


def _merge_bwd(dx1, z, ys, wup, mb, wo):
    S = dx1.shape[0]
    ts = _tile(S, MERGE_TILE)

    def body(dx_ref, zm_ref, ya_ref, yb_ref, yc_ref, yd_ref, wup_ref, mb_ref, wo_ref,
             dzm_ref, du_ref, dxb_ref, dy_ref, st_ref):
        @pl.when(pl.program_id(0) == 0)
        def _():
            st_ref[...] = jnp.zeros_like(st_ref)

        yrefs = (ya_ref, yb_ref, yc_ref, yd_ref)
        mbv = mb_ref[...]
        dxb = _mx(dx_ref[...])
        dxb_ref[...] = dxb
        dmerged = _dot_nt(dxb, wo_ref[...])
        for b in range(NBR):
            cs = slice(b * D_MODEL, (b + 1) * D_MODEL)
            u = jnp.dot(yrefs[b][...], wup_ref[b], preferred_element_type=F32)
            sg = _sigmoid(zm_ref[:, cs] + mbv[b:b + 1, :])
            du = _mx(dmerged * sg)
            du_ref[:, cs] = du
            dzm = dmerged * u * sg * (1.0 - sg)
            dzm_ref[:, cs] = _mx(dzm)
            st_ref[b:b + 1, :] += jnp.sum(dzm, axis=0, keepdims=True)
            dy_ref[:, b * BW:(b + 1) * BW] = _dot_nt(du, wup_ref[b])

    row = lambda w: pl.BlockSpec((ts, w), lambda i: (i, 0))
    return _call(
        body, name="merge_bwd", grid=(S // ts,),
        in_specs=[row(D_MODEL), pl.BlockSpec((ts, 4096), lambda i: (i, 0)), row(BW), row(BW), row(BW), row(BW),
                  pl.BlockSpec((NBR, BW, D_MODEL), lambda i: (0, 0, 0)),
                  pl.BlockSpec((NBR, D_MODEL), lambda i: (0, 0)), pl.BlockSpec((D_MODEL, D_MODEL), lambda i: (0, 0))],
        out_specs=[row(4096), row(4096), row(D_MODEL), row(D_MODEL), pl.BlockSpec((8, D_MODEL), lambda i: (0, 0))],
        out_shape=[jax.ShapeDtypeStruct((S, ZCOLS), MXU_DTYPE), jax.ShapeDtypeStruct((S, 4096), MXU_DTYPE),
                   jax.ShapeDtypeStruct((S, D_MODEL), MXU_DTYPE), jax.ShapeDtypeStruct((S, D_MODEL), F32),
                   jax.ShapeDtypeStruct((8, D_MODEL), F32)],
        sem=("arbitrary",))(dx1, z, *ys, wup, mb, wo)


def _ple_fwd(x1, p, g, wg, wp):
    S = x1.shape[0]
    ts = _tile(S, ROW_TILE)

    def body(x_ref, p_ref, g_ref, wg_ref, wp_ref, o_ref):
        x = x_ref[...]
        hp = _mx(_rms_fwd(x, g_ref[...]))
        gate = _sigmoid(jnp.dot(hp, wg_ref[...], preferred_element_type=F32))
        pp = jnp.dot(_mx(p_ref[...]), wp_ref[...], preferred_element_type=F32)
        o_ref[...] = x + gate * pp

    row = lambda w: pl.BlockSpec((ts, w), lambda i: (i, 0))
    return _call(
        body, name="ple_fwd", grid=(S // ts,),
        in_specs=[row(D_MODEL), row(PLE), pl.BlockSpec((1, D_MODEL), lambda i: (0, 0)),
                  pl.BlockSpec((D_MODEL, D_MODEL), lambda i: (0, 0)), pl.BlockSpec((PLE, D_MODEL), lambda i: (0, 0))],
        out_specs=row(D_MODEL), out_shape=jax.ShapeDtypeStruct((S, D_MODEL), F32),
        sem=("parallel",))(x1, p, g, wg, wp)


def _ple_bwd(x1, p, dx2, g, wg, wp):
    S = x1.shape[0]
    ts = _tile(S, ROW_TILE)

    def body(x_ref, p_ref, dx_ref, g_ref, wg_ref, wp_ref, dx1_ref, hp_ref, dgl_ref, dpp_ref, pb_ref, st_ref):
        @pl.when(pl.program_id(0) == 0)
        def _():
            st_ref[...] = jnp.zeros_like(st_ref)

        x, gv, dx2 = x_ref[...], g_ref[...], dx_ref[...]
        hp = _mx(_rms_fwd(x, gv))
        hp_ref[...] = hp
        gate = _sigmoid(jnp.dot(hp, wg_ref[...], preferred_element_type=F32))
        pb = _mx(p_ref[...])
        pb_ref[...] = pb
        pp = jnp.dot(pb, wp_ref[...], preferred_element_type=F32)
        dgl = _mx(dx2 * pp * gate * (1.0 - gate))
        dgl_ref[...] = dgl
        dpp_ref[...] = _mx(dx2 * gate)
        dhp = _dot_nt(dgl, wg_ref[...])
        dxn, dgr = _rms_bwd(x, gv, dhp)
        dx1_ref[...] = dx2 + dxn
        st_ref[0:1, :] += jnp.sum(dgr, axis=0, keepdims=True)

    row = lambda w: pl.BlockSpec((ts, w), lambda i: (i, 0))
    sq = pl.BlockSpec((D_MODEL, D_MODEL), lambda i: (0, 0))
    return _call(
        body, name="ple_bwd", grid=(S // ts,),
        in_specs=[row(D_MODEL), row(PLE), row(D_MODEL), pl.BlockSpec((1, D_MODEL), lambda i: (0, 0)), sq,
                  pl.BlockSpec((PLE, D_MODEL), lambda i: (0, 0))],
        out_specs=[row(D_MODEL), row(D_MODEL), row(D_MODEL), row(D_MODEL), row(PLE),
                   pl.BlockSpec((8, D_MODEL), lambda i: (0, 0))],
        out_shape=[jax.ShapeDtypeStruct((S, D_MODEL), F32)] + [jax.ShapeDtypeStruct((S, D_MODEL), MXU_DTYPE)] * 3
        + [jax.ShapeDtypeStruct((S, PLE), MXU_DTYPE), jax.ShapeDtypeStruct((8, D_MODEL), F32)],
        sem=("arbitrary",))(x1, p, dx2, g, wg, wp)


def _pad_rows(a, rows=8):
    return jnp.concatenate([a, jnp.zeros((rows - a.shape[0],) + a.shape[1:], a.dtype)], axis=0)


def _pad_lanes(a, lanes=128):
    return jnp.concatenate([a, jnp.zeros(a.shape[:-1] + (lanes - a.shape[-1],), a.dtype)], axis=-1)


def _wz_from_w_in(w):
    zeros = lambda n: jnp.zeros((w.shape[0], n), w.dtype)
    return jnp.concatenate([w[:, _OM:_OEND], w[:, _OA:_OB], w[:, _OB:_OF], w[:, _OC:_OD], w[:, _OD:_OM], zeros(256),
                            w[:, _OF:_OC], zeros(124)], axis=1)


def _w_in_from_wz(g):
    return jnp.concatenate([g[:, ZA:ZB], g[:, ZB:ZC], g[:, ZF:ZF + 4], g[:, ZC:ZD], g[:, ZD:ZD + 768], g[:, ZM:ZA]], axis=1)


_W_IN_GROUPS = [(_OA, _OB, ZA), (_OB, _OF, ZB), (_OF, _OC, ZF), (_OC, _OD, ZC), (_OD, _OM, ZD), (_OM, _OEND, ZM)]


def _wz_from_shards(g):
    n = g.shape[-1]
    pieces, pos = [], 0
    for a, b, zs in sorted(_W_IN_GROUPS, key=lambda grp: grp[2]):
        if zs > pos:
            pieces.append(jnp.zeros((g.shape[1], zs - pos), g.dtype))
        for k in range(4):
            lo, hi = max(a, k * n), min(b, (k + 1) * n)
            if lo < hi:
                pieces.append(g[k][:, lo - k * n:hi - k * n])
        pos = zs + (b - a)
    pieces.append(jnp.zeros((g.shape[1], ZCOLS - pos), g.dtype))
    return jnp.concatenate(pieces, axis=1)


def _w_in_slabs_from_wz(g):
    n = _OEND // 4
    slabs = []
    for k in range(4):
        pieces = []
        for a, b, zs in _W_IN_GROUPS:
            lo, hi = max(a, k * n), min(b, (k + 1) * n)
            if lo < hi:
                pieces.append(g[:, zs + lo - a:zs + hi - a])
        slabs.append(jnp.concatenate(pieces, axis=1))
    return jnp.stack(slabs)


def _local_step(x, p, target, wts, dist=None):
    tmat_np, masks_np = _hgrn_consts()
    tmat = jnp.asarray(tmat_np, MXU_DTYPE)
    masks = jnp.asarray(masks_np, F32)
    lower = _lower_bounds_fwd(wts["lb_logits"])
    saved = []
    for li in range(DEPTH):
        big = dist.weights(li) if dist else {n: wts[n][li] for n in _GATHERED}
        wz = _wz_from_shards(big["w_in"]) if big["w_in"].ndim == 3 else _wz_from_w_in(big["w_in"])
        g_mix = wts["norm_mix"][li][None, :]
        h = _norm_fwd(x, g_mix)
        z = _mm_nn(h, wz, F32, name="mm_z", tn=1664)
        cw = _pad_rows(jnp.concatenate([big["conv_w"], wts["conv_b"][li][None, :]], axis=0))
        ya = _conv_fwd(z, cw)
        fb = _pad_lanes(wts["fgate_bias"][li][None, :])
        gq, gk = wts["q_norm"][li][None, :], wts["k_norm"][li][None, :]
        qa, ka, va, cc = _attn_prep_fwd(z, fb, gq, gk)
        o, lse, yb, got = _flash_fwd(qa, ka, va, z, comm=dist.fwd_comm(li) if dist else None)
        if dist:
            dist.fwd_done(li, got)
        lb = lower[li][None, :]
        gh = wts["hgrn_norm"][li][None, :]
        o_pre, states, yc = _hgrn_fwd(z, lb, gh, tmat, masks)
        gv = wts["sgu_norm"][li][None, :]
        ws = wts["spatial_w"][li]
        bs = _pad_lanes(wts["spatial_b"][li].T)
        yd = _sgu_fwd(z, gv, ws, bs)
        ys = (ya, yb, yc, yd)
        x1, merged = _merge_fwd(x, z, ys, big["w_up"], big["merge_b"], big["w_o"])
        g_ple = wts["norm_ple"][li][None, :]
        x2 = _ple_fwd(x1, p[li], g_ple, big["w_ple_gate"], big["w_ple_proj"])
        saved.append(dict(x=x, h=h, z=z, wz=wz, cw=cw, fb=fb, gq=gq, gk=gk, qa=qa, ka=ka, va=va, cc=cc, o=o, lse=lse,
                          lb=lb, gh=gh, o_pre=o_pre, states=states, gv=gv, ws=ws, bs=bs, ys=ys, x1=x1, merged=merged,
                          g_mix=g_mix, g_ple=g_ple, big=big))
        x = x2

    loss, dx = _loss_fwd_bwd(x, target)

    names = ["norm_mix", "w_in", "conv_w", "conv_b", "fgate_bias", "q_norm", "k_norm", "lb", "hgrn_norm", "sgu_norm",
             "spatial_w", "spatial_b", "w_up", "merge_b", "w_o", "norm_ple", "w_ple_gate", "w_ple_proj"]
    gl = {n: [None] * DEPTH for n in names}
    for li in reversed(range(DEPTH)):
        s = saved[li]
        z, big = s["z"], s["big"]
        wg, wp = big["w_ple_gate"], big["w_ple_proj"]
        dx1, hp, dgl, dpp, pb, st = _ple_bwd(s["x1"], p[li], dx, s["g_ple"], wg, wp)
        gl["norm_ple"][li] = st[0]
        gl["w_ple_gate"][li] = _mm_tn(hp, dgl, name="mm_dwg")
        gl["w_ple_proj"][li] = _mm_tn(pb, dpp, name="mm_dwp")
        wup, wo = big["w_up"], big["w_o"]
        dz, du, dxb, dy, st = _merge_bwd(dx1, z, s["ys"], wup, big["merge_b"], wo)
        gl["merge_b"][li] = st[0:NBR]
        gl["w_o"][li] = _mm_tn(s["merged"], dxb, name="mm_dwo")
        gl["w_up"][li] = jnp.stack([_mm_tn(s["ys"][b], du, name="mm_dwup", ycol=b, n=D_MODEL) for b in range(NBR)])
        dz, st = _conv_bwd(z, s["cw"], dy, dz)
        gl["conv_w"][li] = st[0:3]
        gl["conv_b"][li] = st[3]
        dgate, qb, doa = _attn_gate_bwd(dy, s["o"], z, s["qa"], s["cc"], s["lse"])
        dq, doa, got = _flash_bwd_dq(qb, s["ka"], s["va"], doa, comm=dist.bwd_comm() if dist else None)
        if dist:
            dist.bwd_done(got)
        dk, dv, dcc = _flash_bwd_dkv(qb, s["ka"], s["va"], doa)
        dz, dzf, st = _attn_prep_bwd(z, s["fb"], s["gq"], s["gk"], dq, dk, dv, dgate, dcc, dz)
        dz = _put_fgate_cols(dzf, dz)
        gl["q_norm"][li] = st[0, 0:DH]
        gl["k_norm"][li] = st[1, 0:DH]
        gl["fgate_bias"][li] = st[2, 0:NH]
        dz, st = _hgrn_bwd(z, s["lb"], s["gh"], tmat, masks, s["o_pre"], s["states"], dy, dz)
        gl["hgrn_norm"][li] = st[0]
        gl["lb"][li] = st[1]
        dz, dws, dbs, st = _sgu_bwd(z, s["gv"], s["ws"], s["bs"], dy, dz)
        gl["sgu_norm"][li] = st[0]
        gl["spatial_w"][li] = dws
        gl["spatial_b"][li] = dbs[:, 0:NH].T
        dwz = _mm_tn(s["h"], dz, name="mm_dwz", tn=1664)
        gl["w_in"][li] = dwz if dist else _w_in_from_wz(dwz)
        dh = _mm_nn(dz, s["wz"], F32, name="mm_dh", tk=1664, b_transposed=True)
        dx, st = _norm_bwd(s["x"], s["g_mix"], dh, dx1)
        gl["norm_mix"][li] = st[0]
        if dist:
            dist.push(li, {n: gl[n][li] for n in _BIG})

    if dist:
        dist.finish()
    grads = {n: jnp.stack(v) for n, v in gl.items() if not (dist and n in _BIG)}
    grads["lb_logits"] = _lower_bounds_bwd(wts["lb_logits"], grads.pop("lb"))
    return loss, dx, grads


def _my_pos():
    return lax.axis_index("x"), lax.axis_index("y"), lax.axis_index("c")


def _gather_phase(phase, ins, outs, sems):
    if phase == 1:
        return
    send, recv, lsem = sems
    x, y, c = _my_pos()
    me = 2 * x + y
    peers = [(1 - x, y), (x, 1 - y), (1 - x, 1 - y)]
    copies = []
    for t in range(len(ins)):
        copies.append(pltpu.make_async_copy(ins[t], outs[t].at[me], lsem.at[t]))
        for j, (px, py) in enumerate(peers):
            copies.append(pltpu.make_async_remote_copy(
                src_ref=ins[t], dst_ref=outs[t].at[me], send_sem=send.at[t, j], recv_sem=recv.at[t, j],
                device_id=(px, py, c), device_id_type=MESH))
    for cp in copies:
        if phase == 0:
            cp.start()
        else:
            cp.wait()


def _exchange_phase(phase, ins, outs, sems):
    send, recv, lsem = sems
    x, y, c = _my_pos()
    me = 2 * x + y
    sib = (x, y, 1 - c)
    chips = [(1 - x, y), (x, 1 - y), (1 - x, 1 - y)]

    def rc(t, k, src, dst, dev):
        return pltpu.make_async_remote_copy(src_ref=src, dst_ref=dst, send_sem=send.at[t, k], recv_sem=recv.at[t, k],
                                            device_id=dev, device_id_type=MESH)

    for t in range(len(ins)):
        local = pltpu.make_async_copy(ins[t].at[me], outs[t].at[2 * me + c], lsem.at[t])
        first = [rc(t, 0, ins[t].at[me], outs[t].at[2 * me + c], sib)]
        first += [rc(t, 1 + j, ins[t].at[2 * px + py], outs[t].at[2 * me + c], (px, py, c))
                  for j, (px, py) in enumerate(chips)]
        landed = [outs[t].at[2 * (2 * px + py) + c] for px, py in chips]
        passed = [rc(t, 4 + j, slot, slot, sib) for j, slot in enumerate(landed)]
        if phase == 0:
            local.start()
            for cp in first:
                cp.start()
        elif phase == 1:
            for j, slot in enumerate(landed):
                rc(t, 1 + j, slot, slot, (x, y, c)).wait_recv()
                passed[j].start()
        else:
            s0 = outs[t].at[2 * me + (1 - c)]
            rc(t, 0, s0, s0, (x, y, c)).wait_recv()
            for j, (px, py) in enumerate(chips):
                slot = outs[t].at[2 * (2 * px + py) + (1 - c)]
                rc(t, 4 + j, slot, slot, (x, y, c)).wait_recv()
            for cp in first + passed:
                cp.wait_send()
            local.wait()


_COMM_PHASES = {"gather": _gather_phase, "exchange": _exchange_phase}
_COMM_OUT = {"gather": lambda arrays: [jax.ShapeDtypeStruct((4,) + a.shape, a.dtype) for a in arrays],
             "exchange": lambda arrays: [jax.ShapeDtypeStruct((8,) + a.shape[1:], a.dtype) for a in arrays]}


def _comm_sems(kind, nc):
    k = 3 if kind == "gather" else 7
    return [pltpu.SemaphoreType.DMA((nc, k)), pltpu.SemaphoreType.DMA((nc, k)), pltpu.SemaphoreType.DMA((nc,))]


def _comm_alone(kind, arrays, name):
    nc = len(arrays)

    def body(*refs):
        for phase in range(3):
            _COMM_PHASES[kind](phase, refs[:nc], refs[nc:2 * nc], refs[2 * nc:])

    hbm = pl.BlockSpec(memory_space=pl.ANY)
    return pl.pallas_call(
        functools.partial(body), name=name, in_specs=[hbm] * nc, out_specs=[hbm] * nc,
        out_shape=_COMM_OUT[kind](arrays), scratch_shapes=_comm_sems(kind, nc))(*arrays)


def _allreduce_small(v):
    R = v.shape[0]

    def body(v_ref, o_ref, buf, send, recv):
        x, y, c = _my_pos()
        me = 4 * x + 2 * y + c
        buf[me] = v_ref[...]
        copies = []
        k = 0
        for dx in range(2):
            for dy in range(2):
                for dc in range(2):
                    if dx == 0 and dy == 0 and dc == 0:
                        continue
                    cp = pltpu.make_async_remote_copy(
                        src_ref=v_ref, dst_ref=buf.at[me], send_sem=send.at[k - 0], recv_sem=recv.at[k],
                        device_id=(jnp.bitwise_xor(x, dx), jnp.bitwise_xor(y, dy), jnp.bitwise_xor(c, dc)),
                        device_id_type=MESH)
                    cp.start()
                    copies.append(cp)
                    k += 1
        for cp in copies:
            cp.wait()
        acc = buf[0]
        for d in range(1, 8):
            acc = acc + buf[d]
        o_ref[...] = acc

    vm = pl.BlockSpec(memory_space=pltpu.VMEM)
    return pl.pallas_call(
        functools.partial(body), name="allreduce_small", in_specs=[vm], out_specs=vm,
        out_shape=jax.ShapeDtypeStruct((R, 128), F32),
        scratch_shapes=[pltpu.VMEM((8, R, 128), F32), pltpu.SemaphoreType.DMA((7,)), pltpu.SemaphoreType.DMA((7,))],
        compiler_params=pltpu.CompilerParams(vmem_limit_bytes=VMEM_LIMIT))(v)


def _adamw(w, m, v, parts, nparts):
    A, R, C = w.shape
    per_layer = isinstance(parts, (list, tuple))
    tr = _tile(R, 64) if per_layer else (R if R <= 128 else _tile(R, 128))
    nparr = len(parts) if per_layer else 1

    def body(*refs):
        w_ref, m_ref, v_ref = refs[:3]
        p_refs = refs[3:3 + nparr]
        g_ref, d_ref, nm_ref, nv_ref = refs[3 + nparr:]

        def update(read):
            g = read(0).astype(F32)
            for k in range(1, nparts):
                g = g + read(k).astype(F32)
            mm = ADAM_B1 * m_ref[0] + (1.0 - ADAM_B1) * g
            vv = ADAM_B2 * v_ref[0] + (1.0 - ADAM_B2) * jnp.square(g)
            m_hat = mm / (1.0 - ADAM_B1 ** ADAM_STEP)
            v_hat = vv / (1.0 - ADAM_B2 ** ADAM_STEP)
            g_ref[0] = g
            d_ref[0] = -ADAM_LR * (m_hat / (jnp.sqrt(v_hat) + ADAM_EPS) + ADAM_WD * w_ref[0])
            nm_ref[0] = mm
            nv_ref[0] = vv

        if not per_layer:
            update(lambda k: p_refs[0][k, 0])
        else:
            for a in range(A):
                @pl.when(pl.program_id(0) == a)
                def _(a=a):
                    update(lambda k: p_refs[a][k])

    blk = pl.BlockSpec((1, tr, C), lambda a, r: (a, r, 0))
    if per_layer:
        pspecs = [pl.BlockSpec((nparts, tr, C), functools.partial(lambda a, r, l: (0, jnp.where(a == l, r, 0), 0), l=l))
                  for l in range(A)]
        pargs = list(parts)
    else:
        pspecs = [pl.BlockSpec((nparts, 1, tr, C), lambda a, r: (0, a, r, 0))]
        pargs = [parts]
    return _call(
        body, name="adamw", grid=(A, R // tr), in_specs=[blk, blk, blk] + pspecs,
        out_specs=[blk] * 4, out_shape=[jax.ShapeDtypeStruct((A, R, C), F32)] * 4,
        sem=("arbitrary", "arbitrary"))(w, m, v, *pargs)


def _as3d(a):
    if a.ndim == 2:
        return a[None]
    if a.ndim == 3:
        return a
    return a.reshape((-1,) + a.shape[-2:])


_WEIGHTS = ["norm_mix", "w_in", "conv_w", "conv_b", "fgate_bias", "q_norm", "k_norm", "lb_logits", "hgrn_norm", "sgu_norm",
            "spatial_w", "spatial_b", "w_up", "merge_b", "w_o", "norm_ple", "w_ple_gate", "w_ple_proj"]
_BIG = ["w_in", "w_up", "w_o", "w_ple_gate", "w_ple_proj"]
_GATHERED = _BIG + ["conv_w", "merge_b"]
_SHARD_AXIS = {"w_in": 2, "w_up": 3, "w_o": 1, "w_ple_gate": 1, "w_ple_proj": 2, "conv_w": 2, "merge_b": 2}
_SMALL = [n for n in _WEIGHTS if n not in _BIG]


class _Dist:
    def __init__(self, w):
        self.w = w
        self.full = {0: self._unpack(_comm_alone("gather", self._shards(0), "gather_weights"))}
        self.contrib = {}
        self.pending = None

    def _shards(self, l):
        return [_mx(self.w[n][l]) for n in _BIG] + [self.w["conv_w"][l], self.w["merge_b"][l]]

    def _unpack(self, gathered):
        return {n: g if n == "w_in" else jnp.concatenate([g[k] for k in range(4)], axis=_SHARD_AXIS[n] - 1)
                for n, g in zip(_GATHERED, gathered)}

    def weights(self, l):
        return self.full[l]

    def fwd_comm(self, l):
        return ("gather", self._shards(l + 1)) if l + 1 < DEPTH else None

    def fwd_done(self, l, got):
        if got:
            self.full[l + 1] = self._unpack(got)

    def push(self, l, grads):
        self.pending = (l, [(_w_in_slabs_from_wz(grads[n]) if n == "w_in" else
                             jnp.stack(jnp.split(grads[n], 4, axis=_SHARD_AXIS[n] - 1))).astype(GRAD_WIRE_DTYPE)
                            for n in _BIG])

    def bwd_comm(self):
        return ("exchange", self.pending[1]) if self.pending else None

    def bwd_done(self, got):
        if got:
            self.contrib[self.pending[0]] = got
            self.pending = None

    def finish(self):
        if self.pending:
            self.contrib[self.pending[0]] = _comm_alone("exchange", self.pending[1], "exchange_grads")
            self.pending = None


def kernel(x, p, norm_mix, w_in, conv_w, conv_b, fgate_bias, q_norm, k_norm, lb_logits, hgrn_norm, sgu_norm, spatial_w, spatial_b, w_up, merge_b, w_o, norm_ple, w_ple_gate, w_ple_proj, loss_target, m_norm_mix, m_w_in, m_conv_w, m_conv_b, m_fgate_bias, m_q_norm, m_k_norm, m_lb_logits, m_hgrn_norm, m_sgu_norm, m_spatial_w, m_spatial_b, m_w_up, m_merge_b, m_w_o, m_norm_ple, m_w_ple_gate, m_w_ple_proj, v_norm_mix, v_w_in, v_conv_w, v_conv_b, v_fgate_bias, v_q_norm, v_k_norm, v_lb_logits, v_hgrn_norm, v_sgu_norm, v_spatial_w, v_spatial_b, v_w_up, v_merge_b, v_w_o, v_norm_ple, v_w_ple_gate, v_w_ple_proj):
    loc = dict(locals())
    w = {n: loc[n] for n in _WEIGHTS}
    m = {n: loc["m_" + n] for n in _WEIGHTS}
    v = {n: loc["v_" + n] for n in _WEIGHTS}
    chip = 2 * lax.axis_index("x") + lax.axis_index("y")

    dist = _Dist(w)
    loss_blk, dx, grads = _local_step(x[0], p[:, 0], loss_target[0], w, dist)
    loss = lax.psum(loss_blk[0, 0], ("x", "y", "c"))

    flat = jnp.concatenate([grads[n].reshape(-1) for n in _SMALL])
    npad = (-flat.shape[0]) % 1024
    packed = jnp.concatenate([flat, jnp.zeros((npad,), F32)]).reshape(-1, 128)
    red = _allreduce_small(packed).reshape(-1)
    small, off = {}, 0
    for n in _SMALL:
        sz = int(np.prod(grads[n].shape))
        small[n] = red[off:off + sz].reshape(grads[n].shape)
        off += sz
    for n in ("conv_w", "merge_b"):
        ax = _SHARD_AXIS[n]
        width = small[n].shape[ax] // 4
        small[n] = lax.dynamic_slice_in_dim(small[n], chip * width, width, axis=ax)

    out_g, out_d, out_m, out_v = {}, {}, {}, {}
    for n in _WEIGHTS:
        shp = w[n].shape
        if n in _BIG:
            w3, m3, v3 = (a.reshape((DEPTH, -1, shp[-1])) for a in (w[n], m[n], v[n]))
            parts = [dist.contrib[l][_BIG.index(n)].reshape((8,) + w3.shape[1:]) for l in range(DEPTH)]
            g, d, nm, nv = _adamw(w3, m3, v3, parts, 8)
        else:
            g, d, nm, nv = _adamw(_as3d(w[n]), _as3d(m[n]), _as3d(v[n]), _as3d(small[n])[None], 1)
        out_g[n], out_d[n], out_m[n], out_v[n] = (a.reshape(shp) for a in (g, d, nm, nv))

    return (loss, dx[None], *[out_g[n] for n in _WEIGHTS], *[out_d[n] for n in _WEIGHTS],
            *[out_m[n] for n in _WEIGHTS], *[out_v[n] for n in _WEIGHTS])
```

```python
import functools

import numpy as np
import jax
import jax.numpy as jnp
from jax import lax
from jax.experimental import pallas as pl
from jax.experimental.pallas import tpu as pltpu

F32 = jnp.float32
MXU_DTYPE = jnp.bfloat16

D_MODEL = 1024
BW = 256
NH = 4
DH = 64
DEPTH = 4
NBR = 4
PLE = 256
CHUNK = 64
SCHUNK = 128
EPS = 1e-6
MASK_VALUE = -1e30
NLEV = 6

ADAM_LR, ADAM_B1, ADAM_B2, ADAM_EPS, ADAM_WD, ADAM_STEP = 0.001, 0.9, 0.999, 1e-08, 0.01, 10

ZM, ZA, ZB, ZC, ZD, ZF = 0, 4096, 5120, 6144, 7168, 8192
ZCOLS = 8320
_OA, _OB, _OF, _OC, _OD, _OM, _OEND = 0, 1024, 2048, 2052, 3076, 3844, 7940

VMEM_LIMIT = 56 * 1024 * 1024
ROW_TILE = 512
FLASH_TILE = 1024
GRAD_WIRE_DTYPE = jnp.bfloat16
MERGE_TILE = 256
MESH = pl.DeviceIdType.MESH
_ANY = pl.BlockSpec(memory_space=pl.ANY)


def _tile(n, pref):
    t = min(n, pref)
    assert n % t == 0, (n, t)
    return t


def _call(body, *, name, grid, in_specs, out_specs, out_shape, scratch=(), sem=None, aliases=None):
    return pl.pallas_call(
        functools.partial(body), name=name, grid=grid, in_specs=in_specs, out_specs=out_specs,
        out_shape=out_shape, scratch_shapes=list(scratch), input_output_aliases=aliases or {},
        compiler_params=pltpu.CompilerParams(dimension_semantics=sem, vmem_limit_bytes=VMEM_LIMIT))


def _mx(x):
    return x.astype(MXU_DTYPE)


def _dot(a, b):
    return jnp.dot(_mx(a), _mx(b), preferred_element_type=F32)


def _dot_nt(a, b):
    return lax.dot_general(_mx(a), _mx(b), (((1,), (1,)), ((), ())), preferred_element_type=F32)


def _dot_tn(a, b):
    return lax.dot_general(_mx(a), _mx(b), (((0,), (0,)), ((), ())), preferred_element_type=F32)


def _top16(x):
    u = lax.bitcast_convert_type(x, jnp.uint32) & jnp.uint32(0xFFFF0000)
    return lax.bitcast_convert_type(u, F32)


def _split3(x):
    hi = _top16(x)
    r1 = x - hi
    mid = _top16(r1)
    return _mx(hi), _mx(mid), _mx(r1 - mid)


def _dot_exact_lhs01(t, x):
    hi, mid, lo = _split3(x)
    t = _mx(t)
    return (jnp.dot(t, hi, preferred_element_type=F32) + jnp.dot(t, mid, preferred_element_type=F32)
            + jnp.dot(t, lo, preferred_element_type=F32))


def _sigmoid(x):
    return jax.nn.sigmoid(x)


def _silu(x):
    return x * _sigmoid(x)


def _dsilu(x):
    s = _sigmoid(x)
    return s * (1.0 + x * (1.0 - s))


def _log_sigmoid(x):
    return jnp.minimum(x, 0.0) - jnp.log(1.0 + jnp.exp(-jnp.abs(x)))


def _rms_fwd(x, g):
    r = lax.rsqrt(jnp.mean(x * x, axis=-1, keepdims=True) + EPS)
    return x * r * g


def _rms_bwd(x, g, dy):
    r = lax.rsqrt(jnp.mean(x * x, axis=-1, keepdims=True) + EPS)
    n = x * r
    t = dy * g
    dx = r * (t - n * jnp.mean(t * n, axis=-1, keepdims=True))
    return dx, dy * n


def _tri(n, upper=False):
    r = lax.broadcasted_iota(jnp.int32, (n, n), 0)
    c = lax.broadcasted_iota(jnp.int32, (n, n), 1)
    return jnp.where((c >= r) if upper else (r >= c), 1.0, 0.0).astype(F32)


def _hs(h):
    return slice(h * DH, (h + 1) * DH)


def _mm_nn(a, b, out_dtype, *, name, ts=None, tn=None, tk=None, b_transposed=False):
    S, K = a.shape
    N = b.shape[0] if b_transposed else b.shape[1]
    ts = _tile(S, ts or ROW_TILE)
    tn = _tile(N, tn or N)
    tk = _tile(K, tk or K)
    nk = K // tk
    b_spec = (pl.BlockSpec((tn, tk), lambda j, i, k: (j, k)) if b_transposed
              else pl.BlockSpec((tk, tn), lambda j, i, k: (k, j)))

    def body(a_ref, b_ref, o_ref, acc_ref):
        k = pl.program_id(2)
        part = _dot_nt(a_ref[...], b_ref[...]) if b_transposed else jnp.dot(a_ref[...], b_ref[...],
                                                                          preferred_element_type=F32)
        if nk == 1:
            o_ref[...] = part.astype(o_ref.dtype)
        else:
            @pl.when(k == 0)
            def _():
                acc_ref[...] = part

            @pl.when(k > 0)
            def _():
                acc_ref[...] += part

            @pl.when(k == nk - 1)
            def _():
                o_ref[...] = acc_ref[...].astype(o_ref.dtype)

    return _call(
        body, name=name, grid=(N // tn, S // ts, nk),
        in_specs=[pl.BlockSpec((ts, tk), lambda j, i, k: (i, k)), b_spec],
        out_specs=pl.BlockSpec((ts, tn), lambda j, i, k: (i, j)),
        out_shape=jax.ShapeDtypeStruct((S, N), out_dtype),
        scratch=[pltpu.VMEM((ts, tn) if nk > 1 else (8, 128), F32)],
        sem=("parallel", "parallel", "arbitrary"))(a, b)


def _mm_tn(x, y, *, name, tn=None, ycol=0, n=None):
    S, M = x.shape
    n = n or y.shape[1]
    ts = _tile(S, 4 * ROW_TILE)
    tn = _tile(n, tn or n)
    nj = n // tn

    def body(x_ref, y_ref, o_ref):
        @pl.when(pl.program_id(1) == 0)
        def _():
            o_ref[...] = jnp.zeros_like(o_ref)

        o_ref[...] += lax.dot_general(x_ref[...], y_ref[...], (((0,), (0,)), ((), ())), preferred_element_type=F32)

    return _call(
        body, name=name, grid=(nj, S // ts),
        in_specs=[pl.BlockSpec((ts, M), lambda j, s: (s, 0)), pl.BlockSpec((ts, tn), lambda j, s: (s, ycol * nj + j))],
        out_specs=pl.BlockSpec((M, tn), lambda j, s: (0, j)),
        out_shape=jax.ShapeDtypeStruct((M, n), F32),
        sem=("parallel", "arbitrary"))(x, y)


def _norm_fwd(x, g):
    S = x.shape[0]
    ts = _tile(S, ROW_TILE)

    def body(x_ref, g_ref, h_ref):
        h_ref[...] = _mx(_rms_fwd(x_ref[...], g_ref[...]))

    return _call(
        body, name="norm_fwd", grid=(S // ts,),
        in_specs=[pl.BlockSpec((ts, D_MODEL), lambda i: (i, 0)), pl.BlockSpec((1, D_MODEL), lambda i: (0, 0))],
        out_specs=pl.BlockSpec((ts, D_MODEL), lambda i: (i, 0)),
        out_shape=jax.ShapeDtypeStruct((S, D_MODEL), MXU_DTYPE), sem=("parallel",))(x, g)


def _norm_bwd(x, g, dh, dres):
    S = x.shape[0]
    ts = _tile(S, ROW_TILE)

    def body(x_ref, g_ref, dh_ref, dr_ref, dx_ref, st_ref):
        @pl.when(pl.program_id(0) == 0)
        def _():
            st_ref[...] = jnp.zeros_like(st_ref)

        dx, dgr = _rms_bwd(x_ref[...], g_ref[...], dh_ref[...])
        dx_ref[...] = dr_ref[...] + dx
        st_ref[0:1, :] += jnp.sum(dgr, axis=0, keepdims=True)

    row = pl.BlockSpec((ts, D_MODEL), lambda i: (i, 0))
    return _call(
        body, name="norm_bwd", grid=(S // ts,),
        in_specs=[row, pl.BlockSpec((1, D_MODEL), lambda i: (0, 0)), row, row],
        out_specs=[row, pl.BlockSpec((8, D_MODEL), lambda i: (0, 0))],
        out_shape=[jax.ShapeDtypeStruct((S, D_MODEL), F32), jax.ShapeDtypeStruct((8, D_MODEL), F32)],
        sem=("arbitrary",))(x, g, dh, dres)


def _loss_fwd_bwd(y, target):
    S = y.shape[0]
    ts = _tile(S, ROW_TILE)

    def body(y_ref, t_ref, l_ref, dy_ref):
        @pl.when(pl.program_id(0) == 0)
        def _():
            l_ref[...] = jnp.zeros_like(l_ref)

        err = y_ref[...] - t_ref[...]
        dy_ref[...] = err * (1.0 / D_MODEL)
        rowloss = jnp.mean(err * err, axis=-1, keepdims=True)
        l_ref[...] += 0.5 * jnp.sum(rowloss, axis=0, keepdims=True)

    row = pl.BlockSpec((ts, D_MODEL), lambda i: (i, 0))
    return _call(
        body, name="loss", grid=(S // ts,), in_specs=[row, row],
        out_specs=[pl.BlockSpec((8, 128), lambda i: (0, 0)), row],
        out_shape=[jax.ShapeDtypeStruct((8, 128), F32), jax.ShapeDtypeStruct((S, D_MODEL), F32)],
        sem=("arbitrary",))(y, target)


def _shift_down(x, k, halo, rows):
    y = pltpu.roll(x, k, 0)
    for j in range(k):
        y = jnp.where(rows == j, halo[8 - k + j:8 - k + j + 1, :], y)
    return y


def _shift_up(x, k, halo, rows, n):
    y = pltpu.roll(x, n - k, 0)
    for j in range(k):
        y = jnp.where(rows == n - k + j, halo[j:j + 1, :], y)
    return y


def _conv_parts(za, zh, cw, first, rows):
    ax, ab, ac, ag = za[:, 0:BW], za[:, BW:2 * BW], za[:, 2 * BW:3 * BW], za[:, 3 * BW:4 * BW]
    zz = ac * ax
    hz = jnp.where(first, 0.0, zh[:, 2 * BW:3 * BW] * zh[:, 0:BW])
    zz1 = _shift_down(zz, 1, hz, rows)
    zz2 = _shift_down(zz, 2, hz, rows)
    conv = zz2 * cw[0:1, :] + zz1 * cw[1:2, :] + zz * cw[2:3, :] + cw[3:4, :]
    return ax, ab, ac, ag, zz, zz1, zz2, conv


def _conv_fwd(z, cw):
    S = z.shape[0]
    ts = _tile(S, ROW_TILE)
    hb = ts // 8

    def body(za_ref, zh_ref, cw_ref, y_ref):
        i = pl.program_id(0)
        rows = lax.broadcasted_iota(jnp.int32, (ts, BW), 0)
        ax, ab, ac, ag, zz, zz1, zz2, conv = _conv_parts(za_ref[...], zh_ref[...], cw_ref[...], i == 0, rows)
        y_ref[...] = _mx(ab * conv * _silu(ag))

    return _call(
        body, name="conv_fwd", grid=(S // ts,),
        in_specs=[pl.BlockSpec((ts, 1024), lambda i: (i, ZA // 1024)),
                  pl.BlockSpec((8, 1024), lambda i: (jnp.maximum(i * hb - 1, 0), ZA // 1024)),
                  pl.BlockSpec((8, BW), lambda i: (0, 0))],
        out_specs=pl.BlockSpec((ts, BW), lambda i: (i, 0)),
        out_shape=jax.ShapeDtypeStruct((S, BW), MXU_DTYPE), sem=("parallel",))(z, z, cw)


def _conv_bwd(z, cw, dy, dz):
    S = z.shape[0]
    ts = _tile(S, ROW_TILE)
    hb = ts // 8
    nt = S // ts

    def body(za_ref, zh_ref, zn_ref, cw_ref, dy_ref, dyn_ref, dz_in, dz_ref, st_ref):
        i = pl.program_id(0)

        @pl.when(i == 0)
        def _():
            st_ref[...] = jnp.zeros_like(st_ref)

        cw = cw_ref[...]
        rows = lax.broadcasted_iota(jnp.int32, (ts, BW), 0)
        ax, ab, ac, ag, zz, zz1, zz2, conv = _conv_parts(za_ref[...], zh_ref[...], cw, i == 0, rows)
        dy = dy_ref[...]
        sg = _silu(ag)
        dc = dy * ab * sg
        zn = zn_ref[...]
        dcn = jnp.where(i == nt - 1, 0.0, dyn_ref[...] * zn[:, BW:2 * BW] * _silu(zn[:, 3 * BW:4 * BW]))
        dc1 = _shift_up(dc, 1, dcn, rows, ts)
        dc2 = _shift_up(dc, 2, dcn, rows, ts)
        dzz = dc * cw[2:3, :] + dc1 * cw[1:2, :] + dc2 * cw[0:1, :]
        dz_ref[:, 0:BW] = _mx(dzz * ac)
        dz_ref[:, BW:2 * BW] = _mx(dy * conv * sg)
        dz_ref[:, 2 * BW:3 * BW] = _mx(dzz * ax)
        dz_ref[:, 3 * BW:4 * BW] = _mx(dy * ab * conv * _dsilu(ag))
        st_ref[0:1, :] += jnp.sum(dc * zz2, axis=0, keepdims=True)
        st_ref[1:2, :] += jnp.sum(dc * zz1, axis=0, keepdims=True)
        st_ref[2:3, :] += jnp.sum(dc * zz, axis=0, keepdims=True)
        st_ref[3:4, :] += jnp.sum(dc, axis=0, keepdims=True)

    return _call(
        body, name="conv_bwd", grid=(nt,),
        in_specs=[pl.BlockSpec((ts, 1024), lambda i: (i, ZA // 1024)),
                  pl.BlockSpec((8, 1024), lambda i: (jnp.maximum(i * hb - 1, 0), ZA // 1024)),
                  pl.BlockSpec((8, 1024), lambda i: (jnp.minimum((i + 1) * hb, S // 8 - 1), ZA // 1024)),
                  pl.BlockSpec((8, BW), lambda i: (0, 0)),
                  pl.BlockSpec((ts, BW), lambda i: (i, 0)),
                  pl.BlockSpec((8, BW), lambda i: (jnp.minimum((i + 1) * hb, S // 8 - 1), 0)), _ANY],
        out_specs=[pl.BlockSpec((ts, 1024), lambda i: (i, ZA // 1024)), pl.BlockSpec((8, BW), lambda i: (0, 0))],
        out_shape=[jax.ShapeDtypeStruct((S, ZCOLS), MXU_DTYPE), jax.ShapeDtypeStruct((8, BW), F32)],
        sem=("arbitrary",), aliases={6: 0})(z, z, z, cw, dy, dy, dz)


AW = 128
_AUG = DH


def _split3_f32(x):
    hi = _top16(x)
    r1 = x - hi
    mid = _top16(r1)
    return hi, mid, r1 - mid


def _aug_lanes(n, cols):
    lane = lax.broadcasted_iota(jnp.int32, (n, DH), 1)
    out = jnp.zeros((n, DH), F32)
    for e, c in enumerate(cols):
        out = jnp.where(lane == e, c, out)
    return out


def _put_aug(ref, h, col, parts=None, const=None):
    base = h * AW + _AUG + col
    if parts is None:
        ref[:, base:base + 3] = jnp.full((ref.shape[0], 3), const, ref.dtype)
    else:
        for e, part in enumerate(parts):
            ref[:, base + e:base + e + 1] = part.astype(ref.dtype)


def _attn_prep_fwd(z, fb, gq, gk):
    S = z.shape[0]
    ts = _tile(S, ROW_TILE)

    def body(zb_ref, zf_ref, fb_ref, gq_ref, gk_ref, qa_ref, ka_ref, va_ref, cc_ref, carry):
        @pl.when(pl.program_id(0) == 0)
        def _():
            carry[...] = jnp.zeros_like(carry)

        zb = zb_ref[...]
        gqv, gkv = gq_ref[...], gk_ref[...]
        lf = _log_sigmoid(zf_ref[...] + fb_ref[...])
        cum = _dot_exact_lhs01(_tri(ts), lf) + carry[...]
        carry[...] = cum[ts - 1:ts, :]
        cc_ref[...] = cum
        pieces = _split3_f32(cum)
        for h in range(NH):
            ph = [pc[:, h:h + 1] for pc in pieces]
            qh = _rms_fwd(zb[:, _hs(h)], gqv) * (DH ** -0.5)
            kh = _rms_fwd(zb[:, BW + h * DH:BW + (h + 1) * DH], gkv)
            vh = zb[:, 2 * BW + h * DH:2 * BW + (h + 1) * DH]
            qa_ref[:, _ah(h)] = _mx(jnp.concatenate([qh, _aug_lanes(ts, ph + [1.0] * 3)], axis=1))
            ka_ref[:, _ah(h)] = _mx(jnp.concatenate([kh, _aug_lanes(ts, [1.0] * 3 + [-x for x in ph])], axis=1))
            va_ref[:, _ah(h)] = _mx(jnp.concatenate([vh, _aug_lanes(ts, [-1.0] * 3)], axis=1))

    row = lambda w: pl.BlockSpec((ts, w), lambda i: (i, 0))
    return _call(
        body, name="attn_prep_fwd", grid=(S // ts,),
        in_specs=[pl.BlockSpec((ts, 1024), lambda i: (i, ZB // 1024)), pl.BlockSpec((ts, 128), lambda i: (i, ZF // 128)),
                  pl.BlockSpec((1, 128), lambda i: (0, 0)), pl.BlockSpec((1, DH), lambda i: (0, 0)),
                  pl.BlockSpec((1, DH), lambda i: (0, 0))],
        out_specs=[row(NH * AW), row(NH * AW), row(NH * AW), row(128)],
        out_shape=[jax.ShapeDtypeStruct((S, NH * AW), MXU_DTYPE)] * 3 + [jax.ShapeDtypeStruct((S, 128), F32)],
        scratch=[pltpu.VMEM((1, 128), F32)], sem=("arbitrary",))(z, z, fb, gq, gk)


ROW_CHUNK = 256


def _fuse_comm(core, n_in, n_out, comm, n):
    nc = 0 if comm is None else len(comm[1])

    def body(*refs):
        cin, xin = refs[:n_in], refs[n_in:n_in + nc]
        a = n_in + nc
        cout, xout = refs[a:a + n_out], refs[a + n_out:a + n_out + nc]
        rest = refs[a + n_out + nc:]
        if nc == 0:
            core(*cin, *cout, *rest)
            return
        cscr, sems = rest[:-3], rest[-3:]
        i, j = pl.program_id(0), pl.program_id(1)
        phase = _COMM_PHASES[comm[0]]

        @pl.when((i == 0) & (j == 0))
        def _():
            phase(0, xin, xout, sems)

        core(*cin, *cout, *cscr)

        @pl.when((i == n - 1) & (j == 0))
        def _():
            phase(1, xin, xout, sems)

        @pl.when((i == n - 1) & (j == n - 1))
        def _():
            phase(2, xin, xout, sems)

    return body


def _comm_specs(comm):
    if comm is None:
        return [], [], [], []
    hbm = pl.BlockSpec(memory_space=pl.ANY)
    nc = len(comm[1])
    return [hbm] * nc, [hbm] * nc, _COMM_OUT[comm[0]](comm[1]), _comm_sems(comm[0], nc)


def _ah(h):
    return slice(h * AW, (h + 1) * AW)


def _ahd(h):
    return slice(h * AW, h * AW + DH)


def _causal(shape, row0, transposed=False):
    r = row0 + lax.broadcasted_iota(jnp.int32, shape, 0)
    c = lax.broadcasted_iota(jnp.int32, shape, 1)
    return (r <= c) if transposed else (r >= c)


def _flash_fwd(qa, ka, va, z, comm=None):
    S = qa.shape[0]
    t = _tile(S, FLASH_TILE)
    n = S // t
    rch = _tile(t, ROW_CHUNK)

    def core(q_ref, k_ref, v_ref, zb_ref, o_ref, lse_ref, y_ref, m_sc, l_sc, acc):
        i, j = pl.program_id(0), pl.program_id(1)

        @pl.when(j == 0)
        def _():
            m_sc[...] = jnp.full_like(m_sc, MASK_VALUE)
            l_sc[...] = jnp.zeros_like(l_sc)
            acc[...] = jnp.zeros_like(acc)

        def block(masked):
            for h in range(NH):
                for rc in range(t // rch):
                    rows = slice(rc * rch, (rc + 1) * rch)
                    s = _dot_nt(q_ref[rows, _ah(h)], k_ref[:, _ah(h)])
                    if masked:
                        s = jnp.where(_causal(s.shape, rc * rch), s, MASK_VALUE)
                    m_old = m_sc[h, rows, :]
                    m_new = jnp.maximum(m_old, jnp.max(s, axis=-1, keepdims=True))
                    p = jnp.exp(s - m_new)
                    alpha = jnp.exp(m_old - m_new)
                    l_sc[h, rows, :] = alpha * l_sc[h, rows, :] + jnp.sum(p, axis=-1, keepdims=True)
                    acc[rows, _hs(h)] = alpha * acc[rows, _hs(h)] + _dot(p, v_ref[:, _ahd(h)])
                    m_sc[h, rows, :] = m_new

        @pl.when(j < i)
        def _():
            block(False)

        @pl.when(j == i)
        def _():
            block(True)
            lse_ref[...] = jnp.zeros_like(lse_ref)
            for h in range(NH):
                o_ref[:, _hs(h)] = acc[:, _hs(h)] / l_sc[h]
                lse_ref[:, h:h + 1] = m_sc[h] + jnp.log(l_sc[h])
            y_ref[...] = _mx(o_ref[...] * _silu(zb_ref[:, 3 * BW:4 * BW]))

    qspec = lambda w: pl.BlockSpec((t, w), lambda i, j: (i, 0))
    kspec = lambda w: pl.BlockSpec((t, w), lambda i, j: (jnp.minimum(j, i), 0))
    xin, xout, xshape, xsem = _comm_specs(comm)
    res = _call(
        _fuse_comm(core, 4, 3, comm, n), name="flash_fwd", grid=(n, n),
        in_specs=[qspec(NH * AW), kspec(NH * AW), kspec(NH * AW),
                  pl.BlockSpec((t, 1024), lambda i, j: (i, ZB // 1024))] + xin,
        out_specs=[qspec(BW), qspec(128), qspec(BW)] + xout,
        out_shape=[jax.ShapeDtypeStruct((S, BW), F32), jax.ShapeDtypeStruct((S, 128), F32),
                   jax.ShapeDtypeStruct((S, BW), MXU_DTYPE)] + xshape,
        scratch=[pltpu.VMEM((NH, t, 1), F32), pltpu.VMEM((NH, t, 1), F32), pltpu.VMEM((t, BW), F32)] + xsem,
        sem=("arbitrary", "arbitrary"))(qa, ka, va, z, *(comm[1] if comm else []))
    return res[0], res[1], res[2], list(res[3:])


def _attn_gate_bwd(dy, o, z, qa, cc, lse):
    S = dy.shape[0]
    ts = _tile(S, ROW_TILE)

    def body(dy_ref, o_ref, zb_ref, qa_ref, cc_ref, lse_ref, dg_ref, qb_ref, doa_ref):
        g = zb_ref[:, 3 * BW:4 * BW]
        dy, o = dy_ref[...], o_ref[...]
        do = dy * _silu(g)
        dg_ref[...] = _mx(dy * o * _dsilu(g))
        qb_ref[...] = qa_ref[...]
        doa_ref[...] = jnp.zeros_like(doa_ref)
        shifted = _split3_f32(cc_ref[...] - lse_ref[...])
        for h in range(NH):
            doh = do[:, _hs(h)]
            doa_ref[:, _ahd(h)] = _mx(doh)
            delta = jnp.sum(doh * o[:, _hs(h)], axis=-1, keepdims=True)
            _put_aug(doa_ref, h, 0, parts=_split3_f32(delta))
            _put_aug(qb_ref, h, 0, parts=[pc[:, h:h + 1] for pc in shifted])

    row = lambda w: pl.BlockSpec((ts, w), lambda i: (i, 0))
    return _call(
        body, name="attn_gate_bwd", grid=(S // ts,),
        in_specs=[pl.BlockSpec((ts, BW), lambda i: (i, 1)), row(BW), pl.BlockSpec((ts, 1024), lambda i: (i, ZB // 1024)),
                  row(NH * AW), row(128), row(128)],
        out_specs=[row(BW), row(NH * AW), row(NH * AW)],
        out_shape=[jax.ShapeDtypeStruct((S, BW), MXU_DTYPE), jax.ShapeDtypeStruct((S, NH * AW), MXU_DTYPE),
                   jax.ShapeDtypeStruct((S, NH * AW), MXU_DTYPE)],
        sem=("parallel",))(dy, o, z, qa, cc, lse)


def _aug_value(ref, h, rows=slice(None)):
    base = h * AW + _AUG
    x = ref[rows, base:base + 3].astype(F32)
    return x[:, 0:1] + x[:, 1:2] + x[:, 2:3]


def _flash_bwd_dq(qb, ka, va, doa, comm=None):
    S = qb.shape[0]
    t = _tile(S, FLASH_TILE)
    n = S // t
    rch = _tile(t, ROW_CHUNK)

    def core(q_ref, k_ref, v_ref, do_ref, dq_ref, do2_ref, dr_sc):
        i, j = pl.program_id(0), pl.program_id(1)

        @pl.when(j == 0)
        def _():
            dq_ref[...] = jnp.zeros_like(dq_ref)
            dr_sc[...] = jnp.zeros_like(dr_sc)

        def block(masked):
            for h in range(NH):
                for rc in range(t // rch):
                    rows = slice(rc * rch, (rc + 1) * rch)
                    p = jnp.exp(_dot_nt(q_ref[rows, _ah(h)], k_ref[:, _ah(h)]))
                    if masked:
                        p = jnp.where(_causal(p.shape, rc * rch), p, 0.0)
                    ds = p * _dot_nt(do_ref[rows, _ah(h)], v_ref[:, _ah(h)])
                    dq_ref[rows, _hs(h)] += _dot(ds, k_ref[:, _ahd(h)])
                    dr_sc[h, rows, :] += jnp.sum(ds, axis=-1, keepdims=True)

        @pl.when(j < i)
        def _():
            block(False)

        @pl.when(j == i)
        def _():
            block(True)
            do2_ref[...] = do_ref[...]
            for h in range(NH):
                _put_aug(do2_ref, h, 0, parts=_split3_f32(_aug_value(do_ref, h) + dr_sc[h]))

    qspec = lambda w: pl.BlockSpec((t, w), lambda i, j: (i, 0))
    kspec = lambda w: pl.BlockSpec((t, w), lambda i, j: (jnp.minimum(j, i), 0))
    xin, xout, xshape, xsem = _comm_specs(comm)
    res = _call(
        _fuse_comm(core, 4, 2, comm, n), name="flash_bwd_dq", grid=(n, n),
        in_specs=[qspec(NH * AW), kspec(NH * AW), kspec(NH * AW), qspec(NH * AW)] + xin,
        out_specs=[qspec(BW), qspec(NH * AW)] + xout,
        out_shape=[jax.ShapeDtypeStruct((S, BW), F32), jax.ShapeDtypeStruct((S, NH * AW), MXU_DTYPE)] + xshape,
        scratch=[pltpu.VMEM((NH, t, 1), F32)] + xsem,
        sem=("arbitrary", "arbitrary"))(qb, ka, va, doa, *(comm[1] if comm else []))
    return res[0], res[1], list(res[2:])


def _flash_bwd_dkv(qb, ka, va, doa):
    S = qb.shape[0]
    t = _tile(S, FLASH_TILE)
    n = S // t

    def body(q_ref, k_ref, v_ref, do_ref, dk_ref, dv_ref, dc_ref):
        j, i = pl.program_id(0), pl.program_id(1)

        @pl.when(i == 0)
        def _():
            dk_ref[...] = jnp.zeros_like(dk_ref)
            dv_ref[...] = jnp.zeros_like(dv_ref)
            dc_ref[...] = jnp.zeros_like(dc_ref)

        def block(masked):
            for h in range(NH):
                pt = jnp.exp(_dot_nt(k_ref[:, _ah(h)], q_ref[:, _ah(h)]))
                if masked:
                    pt = jnp.where(_causal(pt.shape, 0, transposed=True), pt, 0.0)
                dst = pt * _dot_nt(v_ref[:, _ah(h)], do_ref[:, _ah(h)])
                dv_ref[:, _hs(h)] += _dot(pt, do_ref[:, _ahd(h)])
                dk_ref[:, _hs(h)] += _dot(dst, q_ref[:, _ahd(h)])
                dc_ref[:, h:h + 1] += -jnp.sum(dst, axis=-1, keepdims=True)

        @pl.when(i > j)
        def _():
            block(False)

        @pl.when(i == j)
        def _():
            block(True)

    qspec = lambda w: pl.BlockSpec((t, w), lambda j, i: (jnp.maximum(i, j), 0))
    kspec = lambda w: pl.BlockSpec((t, w), lambda j, i: (j, 0))
    return _call(
        body, name="flash_bwd_dkv", grid=(n, n),
        in_specs=[qspec(NH * AW), kspec(NH * AW), kspec(NH * AW), qspec(NH * AW)],
        out_specs=[kspec(BW), kspec(BW), kspec(128)],
        out_shape=[jax.ShapeDtypeStruct((S, BW), F32), jax.ShapeDtypeStruct((S, BW), F32),
                   jax.ShapeDtypeStruct((S, 128), F32)],
        sem=("parallel", "arbitrary"))(qb, ka, va, doa)


def _attn_prep_bwd(z, fb, gq, gk, dq, dk, dv, dgate, dcc, dz):
    S = z.shape[0]
    ts = _tile(S, ROW_TILE)
    nt = S // ts

    def body(zb_ref, zf_ref, fb_ref, gq_ref, gk_ref, dq_ref, dk_ref, dv_ref, dg_ref, dcc_ref, dz_in, dzb_ref, dzf_ref,
             st_ref, carry):
        @pl.when(pl.program_id(0) == 0)
        def _():
            carry[...] = jnp.zeros_like(carry)
            st_ref[...] = jnp.zeros_like(st_ref)

        zb = zb_ref[...]
        gqv, gkv = gq_ref[...], gk_ref[...]
        dqv, dkv = dq_ref[...], dk_ref[...]
        sq = jnp.zeros((1, DH), F32)
        sk = jnp.zeros((1, DH), F32)
        for h in range(NH):
            dx, dgr = _rms_bwd(zb[:, _hs(h)], gqv, dqv[:, _hs(h)] * (DH ** -0.5))
            dzb_ref[:, _hs(h)] = _mx(dx)
            sq = sq + jnp.sum(dgr, axis=0, keepdims=True)
            ks = slice(BW + h * DH, BW + (h + 1) * DH)
            dx, dgr = _rms_bwd(zb[:, ks], gkv, dkv[:, _hs(h)])
            dzb_ref[:, ks] = _mx(dx)
            sk = sk + jnp.sum(dgr, axis=0, keepdims=True)
        dzb_ref[:, 2 * BW:3 * BW] = _mx(dv_ref[...])
        dzb_ref[:, 3 * BW:4 * BW] = dg_ref[...]
        dc = dcc_ref[...]
        dlf = _dot_exact_lhs01(_tri(ts, upper=True), dc) + carry[...]
        carry[...] = dlf[0:1, :]
        dfz = dlf * _sigmoid(-(zf_ref[...] + fb_ref[...]))
        dzf_ref[...] = _mx(dfz)
        st_ref[0:1, 0:DH] += sq
        st_ref[1:2, 0:DH] += sk
        st_ref[2:3, :] += jnp.sum(dfz, axis=0, keepdims=True)

    rev = lambda w, c=0: pl.BlockSpec((ts, w), lambda i: (nt - 1 - i, c))
    one = lambda w: pl.BlockSpec((1, w), lambda i: (0, 0))
    return _call(
        body, name="attn_prep_bwd", grid=(nt,),
        in_specs=[rev(1024, ZB // 1024), rev(128, ZF // 128), one(128), one(DH), one(DH),
                  rev(BW), rev(BW), rev(BW), rev(BW), rev(128), _ANY],
        out_specs=[rev(1024, ZB // 1024), rev(128), pl.BlockSpec((8, 128), lambda i: (0, 0))],
        out_shape=[jax.ShapeDtypeStruct((S, ZCOLS), MXU_DTYPE), jax.ShapeDtypeStruct((S, 128), MXU_DTYPE),
                   jax.ShapeDtypeStruct((8, 128), F32)],
        scratch=[pltpu.VMEM((1, 128), F32)], sem=("arbitrary",), aliases={10: 0})(z, z, fb, gq, gk, dq, dk, dv, dgate, dcc, dz)


def _put_fgate_cols(dzf, dz):
    S = dzf.shape[0]
    ts = _tile(S, ROW_TILE)

    def body(f_ref, dz_in, o_ref):
        o_ref[...] = f_ref[...]

    return _call(
        body, name="put_fgate_cols", grid=(S // ts,),
        in_specs=[pl.BlockSpec((ts, 128), lambda i: (i, 0)), _ANY],
        out_specs=pl.BlockSpec((ts, 128), lambda i: (i, ZF // 128)),
        out_shape=jax.ShapeDtypeStruct((S, ZCOLS), MXU_DTYPE), sem=("parallel",), aliases={1: 0})(dzf, dz)


def _hgrn_consts():
    C = CHUNK
    t = np.arange(C)[:, None]
    j = np.arange(C)[None, :]
    masks = []
    for lev in range(NLEV):
        m = C >> (lev + 1)
        blk, pos = t // (2 * m), t % (2 * m)
        sblk, spos = j // (2 * m), j % (2 * m)
        masks.append((blk == sblk) & (pos >= m) & (spos < m))
    masks.append(t == j)
    return (j <= t).astype(np.float32), np.stack(masks).astype(np.float32)


def _level_exponents(b, lg):
    row = lax.broadcasted_iota(jnp.int32, (CHUNK, 1), 0)
    eqs, eks = [], []
    for lev in range(NLEV):
        m = CHUNK >> (lev + 1)
        pos = jnp.bitwise_and(row, 2 * m - 1)
        if 2 * m >= 8:
            b3 = b.reshape(CHUNK // (2 * m), 2 * m, b.shape[1])
            mid = jnp.broadcast_to(b3[:, m - 1:m, :], b3.shape).reshape(b.shape)
            eqs.append(jnp.where(pos >= m, b - mid, 0.0))
            eks.append(jnp.where(pos < m, mid - b, 0.0))
        elif m == 2:
            eqs.append(jnp.where(pos == 2, lg, jnp.where(pos == 3, lg + pltpu.roll(lg, 1, 0), 0.0)))
            eks.append(jnp.where(pos == 0, pltpu.roll(lg, CHUNK - 1, 0), 0.0))
        else:
            eqs.append(jnp.where(pos == 1, lg, 0.0))
            eks.append(jnp.zeros_like(lg))
    return jnp.concatenate([b] + eqs + eks, axis=0)


def _hgrn_chunk_fwd(zc, lb, tmat):
    cq, cf, ci = zc[:, 0:BW], zc[:, BW:2 * BW], zc[:, 2 * BW:3 * BW]
    q = _silu(cq)
    sg = _sigmoid(cf)
    g = lb + (1.0 - lb) * sg
    lg = jnp.log(g)
    kf = (1.0 - lb) * _sigmoid(-cf)
    b = _dot_exact_lhs01(tmat, lg)
    e = _level_exponents(b, lg)
    blast = b[CHUNK - 1:CHUNK, :]
    return cq, cf, q, sg, g, kf, ci, e, b, blast


def _round_mx(x):
    if MXU_DTYPE != jnp.bfloat16:
        return x
    u = lax.bitcast_convert_type(x, jnp.uint32)
    u = (u + jnp.uint32(0x7FFF) + ((u >> 16) & jnp.uint32(1))) & jnp.uint32(0xFFFF0000)
    return lax.bitcast_convert_type(u, F32)


def _hgrn_scores(q, kf, e, masks, h):
    qh, kh = q[:, _hs(h)], kf[:, _hs(h)]
    ql, kl = [], []
    a = None
    for lev in range(NLEV + 1):
        if lev < NLEV:
            eq = jnp.exp(e[(1 + lev) * CHUNK:(2 + lev) * CHUNK, _hs(h)])
            ek = jnp.exp(e[(1 + NLEV + lev) * CHUNK:(2 + NLEV + lev) * CHUNK, _hs(h)])
            ql.append((_round_mx(qh * eq), eq))
            kl.append((_round_mx(kh * ek), ek))
        else:
            ql.append((_round_mx(qh), None))
            kl.append((_round_mx(kh), None))
        term = masks[lev] * _dot_nt(ql[-1][0], kl[-1][0])
        a = term if a is None else a + term
    return a, ql, kl


def _hgrn_fwd(z, lb, gain, tmat, masks):
    S = z.shape[0]
    ts = _tile(S, ROW_TILE)
    nc = ts // CHUNK

    def body(zc_ref, lb_ref, gn_ref, tm_ref, mk_ref, o_ref, st_ref, y_ref, state):
        @pl.when(pl.program_id(0) == 0)
        def _():
            state[...] = jnp.zeros_like(state)

        lbv, gn, tm, mk = lb_ref[...], gn_ref[...], tm_ref[...], mk_ref[...]

        def chunk(c, carry):
            r0 = pl.multiple_of(c * CHUNK, CHUNK)
            zc = zc_ref[pl.ds(r0, CHUNK), :]
            cq, cf, q, sg, g, kf, v, e, b, blast = _hgrn_chunk_fwd(zc, lbv, tm)
            qe = q * jnp.exp(b)
            kd = kf * jnp.exp(blast - b)
            st_ref[pl.ds(r0, CHUNK), :] = state[...]
            for h in range(NH):
                sth = state[:, _hs(h)]
                a, _, _ = _hgrn_scores(q, kf, e, mk, h)
                oh = _dot_nt(qe[:, _hs(h)], sth) + _dot(a, v[:, _hs(h)])
                state[:, _hs(h)] = sth * jnp.exp(blast[:, _hs(h)]) + _dot_tn(v[:, _hs(h)], kd[:, _hs(h)])
                o_ref[pl.ds(r0, CHUNK), _hs(h)] = oh
                yn = _rms_fwd(oh, gn[:, _hs(h)])
                y_ref[pl.ds(r0, CHUNK), _hs(h)] = _mx(yn * _silu(zc[:, 3 * BW + h * DH:3 * BW + (h + 1) * DH]))
            return carry

        lax.fori_loop(0, nc, chunk, 0, unroll=2)

    row = pl.BlockSpec((ts, BW), lambda i: (i, 0))
    one = pl.BlockSpec((1, BW), lambda i: (0, 0))
    return _call(
        body, name="hgrn_fwd", grid=(S // ts,),
        in_specs=[pl.BlockSpec((ts, 1024), lambda i: (i, ZC // 1024)), one, one,
                  pl.BlockSpec(tmat.shape, lambda i: (0, 0)), pl.BlockSpec(masks.shape, lambda i: (0, 0, 0))],
        out_specs=[row, row, row],
        out_shape=[jax.ShapeDtypeStruct((S, BW), F32), jax.ShapeDtypeStruct((S, BW), F32),
                   jax.ShapeDtypeStruct((S, BW), MXU_DTYPE)],
        scratch=[pltpu.VMEM((CHUNK, BW), F32)], sem=("arbitrary",))(z, lb, gain, tmat, masks)


def _hgrn_bwd(z, lb, gain, tmat, masks, o_pre, states, dy, dz):
    S = z.shape[0]
    ts = _tile(S, ROW_TILE)
    nt = S // ts
    nc = ts // CHUNK

    def body(zc_ref, lb_ref, gn_ref, tm_ref, mk_ref, o_ref, st_ref, dy_ref, dz_in, dz_ref, stat_ref, dstate):
        @pl.when(pl.program_id(0) == 0)
        def _():
            dstate[...] = jnp.zeros_like(dstate)
            stat_ref[...] = jnp.zeros_like(stat_ref)

        lbv, gn, tm, mk = lb_ref[...], gn_ref[...], tm_ref[...], mk_ref[...]
        upper = _tri(CHUNK, upper=True)
        lower_strict = 1.0 - upper

        def chunk(cc, carry):
            c = nc - 1 - cc
            r0 = pl.multiple_of(c * CHUNK, CHUNK)
            zc = zc_ref[pl.ds(r0, CHUNK), :]
            cq, cf, q, sg, g, kf, v, e, b, blast = _hgrn_chunk_fwd(zc, lbv, tm)
            eb = jnp.exp(b)
            ebl = jnp.exp(blast - b)
            qe = q * eb
            kd = kf * ebl
            o = o_ref[pl.ds(r0, CHUNK), :]
            dyv = dy_ref[pl.ds(r0, CHUNK), :]
            stp = st_ref[pl.ds(r0, CHUNK), :]
            cg = zc[:, 3 * BW:4 * BW]
            sgate = _silu(cg)
            dq_parts, dk_parts, dv_parts, dcg_parts = [], [], [], []
            dgain, up_parts, lo_parts, const_parts = [], [], [], []
            for h in range(NH):
                hs = _hs(h)
                oh = o[:, hs]
                r = lax.rsqrt(jnp.mean(oh * oh, axis=-1, keepdims=True) + EPS)
                nrm = oh * r
                dyn = dyv[:, hs] * sgate[:, hs]
                dcg_parts.append(dyv[:, hs] * nrm * gn[:, hs] * _dsilu(cg[:, hs]))
                dgain.append(jnp.sum(dyn * nrm, axis=0, keepdims=True))
                tt = dyn * gn[:, hs]
                doh = r * (tt - nrm * jnp.mean(tt * nrm, axis=-1, keepdims=True))
                a, ql, kl = _hgrn_scores(q, kf, e, mk, h)
                da = _dot_nt(doh, v[:, hs])
                dsth = dstate[:, hs]
                ebh = jnp.exp(blast[:, hs])
                dv_parts.append(_dot_tn(a, doh) + _dot_nt(kd[:, hs], dsth))
                dq_inter = eb[:, hs] * _dot(doh, stp[:, hs])
                dk_state = ebl[:, hs] * _dot(v[:, hs], dsth)
                dqh, dkh, gh = dq_inter, dk_state, None
                for lev in range(NLEV + 1):
                    dal = mk[lev] * da
                    xq = _dot(dal, kl[lev][0])
                    yk = _dot_tn(dal, ql[lev][0])
                    gterm = ql[lev][0] * xq - kl[lev][0] * yk
                    gh = gterm if gh is None else gh + gterm
                    dqh = dqh + (xq if lev == NLEV else ql[lev][1] * xq)
                    dkh = dkh + (yk if lev == NLEV else kl[lev][1] * yk)
                up_parts.append(gh + q[:, hs] * dq_inter)
                lo_parts.append(kf[:, hs] * dk_state)
                const_parts.append(jnp.sum(dsth * stp[:, hs], axis=0, keepdims=True) * ebh)
                dstate[:, hs] = dsth * ebh + _dot_tn(doh, qe[:, hs])
                dq_parts.append(dqh)
                dk_parts.append(dkh)
            dq = jnp.concatenate(dq_parts, axis=1)
            dk = jnp.concatenate(dk_parts, axis=1)
            dlg = (_dot_exact_lhs01(upper, jnp.concatenate(up_parts, axis=1))
                   + _dot_exact_lhs01(lower_strict, jnp.concatenate(lo_parts, axis=1))
                   + jnp.concatenate(const_parts, axis=1))
            dsg = sg * (1.0 - sg)
            dz_ref[pl.ds(r0, CHUNK), 0:BW] = _mx(dq * _dsilu(cq))
            dz_ref[pl.ds(r0, CHUNK), BW:2 * BW] = _mx((dlg / g - dk) * (1.0 - lbv) * dsg)
            dz_ref[pl.ds(r0, CHUNK), 2 * BW:3 * BW] = _mx(jnp.concatenate(dv_parts, axis=1))
            dz_ref[pl.ds(r0, CHUNK), 3 * BW:4 * BW] = _mx(jnp.concatenate(dcg_parts, axis=1))
            stat_ref[0:1, :] += jnp.concatenate(dgain, axis=1)
            stat_ref[1:2, :] += jnp.sum((dlg / g - dk) * (1.0 - sg), axis=0, keepdims=True)
            return carry

        lax.fori_loop(0, nc, chunk, 0, unroll=2)

    rev = lambda w, c=0: pl.BlockSpec((ts, w), lambda i: (nt - 1 - i, c))
    one = pl.BlockSpec((1, BW), lambda i: (0, 0))
    return _call(
        body, name="hgrn_bwd", grid=(nt,),
        in_specs=[rev(1024, ZC // 1024), one, one, pl.BlockSpec(tmat.shape, lambda i: (0, 0)),
                  pl.BlockSpec(masks.shape, lambda i: (0, 0, 0)), rev(BW), rev(BW), rev(BW, 2), _ANY],
        out_specs=[rev(1024, ZC // 1024), pl.BlockSpec((8, BW), lambda i: (0, 0))],
        out_shape=[jax.ShapeDtypeStruct((S, ZCOLS), MXU_DTYPE), jax.ShapeDtypeStruct((8, BW), F32)],
        scratch=[pltpu.VMEM((CHUNK, BW), F32)],
        sem=("arbitrary",), aliases={8: 0})(z, lb, gain, tmat, masks, o_pre, states, dy, dz)


def _lower_bounds_fwd(lb_logits):
    def body(l_ref, o_ref):
        l = l_ref[...]
        m = jnp.max(l, axis=0, keepdims=True)
        ex = jnp.exp(l - m)
        p = ex / jnp.sum(ex, axis=0, keepdims=True)
        cs = p[0:1, :]
        o_ref[0:1, :] = jnp.clip(cs - p[0:1, :], 0.0, 1.0)
        for d in range(1, DEPTH):
            cs = cs + p[d:d + 1, :]
            o_ref[d:d + 1, :] = jnp.clip(cs - p[0:1, :], 0.0, 1.0)

    full = pl.BlockSpec((DEPTH, BW), lambda: (0, 0))
    return _call(body, name="lower_bounds_fwd", grid=(), in_specs=[full], out_specs=full,
                 out_shape=jax.ShapeDtypeStruct((DEPTH, BW), F32))(lb_logits)


def _lower_bounds_bwd(lb_logits, dlow):
    def body(l_ref, d_ref, o_ref):
        l = l_ref[...]
        m = jnp.max(l, axis=0, keepdims=True)
        ex = jnp.exp(l - m)
        p = ex / jnp.sum(ex, axis=0, keepdims=True)
        dl = d_ref[...]
        cs = p[0:1, :]
        dcs = []
        for d in range(DEPTH):
            if d > 0:
                cs = cs + p[d:d + 1, :]
            val = cs - p[0:1, :]
            dcs.append(jnp.where((val > 0.0) & (val < 1.0), dl[d:d + 1, :], 0.0))
        total = dcs[0] + dcs[1] + dcs[2] + dcs[3]
        dp = []
        for j in range(DEPTH):
            s = dcs[j]
            for d in range(j + 1, DEPTH):
                s = s + dcs[d]
            dp.append(s - total if j == 0 else s)
        inner = p[0:1, :] * dp[0]
        for j in range(1, DEPTH):
            inner = inner + p[j:j + 1, :] * dp[j]
        for j in range(DEPTH):
            o_ref[j:j + 1, :] = p[j:j + 1, :] * (dp[j] - inner)

    full = pl.BlockSpec((DEPTH, BW), lambda: (0, 0))
    return _call(body, name="lower_bounds_bwd", grid=(), in_specs=[full, full], out_specs=full,
                 out_shape=jax.ShapeDtypeStruct((DEPTH, BW), F32))(lb_logits, dlow)


def _sgu_fwd(z, gv, ws, bs):
    S = z.shape[0]
    ts = _tile(S, ROW_TILE)
    nc = ts // SCHUNK

    def body(zd_ref, gv_ref, ws_ref, bs_ref, y_ref):
        gvv, bsv = gv_ref[...], bs_ref[...]
        tril = _tri(SCHUNK)
        for c in range(nc):
            rs = slice(c * SCHUNK, (c + 1) * SCHUNK)
            zd = zd_ref[rs, :]
            for h in range(NH):
                vn = _rms_fwd(zd[:, BW + h * DH:BW + (h + 1) * DH], gvv[:, _hs(h)])
                s = _dot(ws_ref[h] * tril, vn) + bsv[:, h:h + 1]
                y_ref[rs, _hs(h)] = _mx(zd[:, _hs(h)] * s * _silu(zd[:, 2 * BW + h * DH:2 * BW + (h + 1) * DH]))

    return _call(
        body, name="sgu_fwd", grid=(S // ts,),
        in_specs=[pl.BlockSpec((ts, 1024), lambda i: (i, ZD // 1024)), pl.BlockSpec((1, BW), lambda i: (0, 0)),
                  pl.BlockSpec((NH, SCHUNK, SCHUNK), lambda i: (0, 0, 0)), pl.BlockSpec((SCHUNK, 128), lambda i: (0, 0))],
        out_specs=pl.BlockSpec((ts, BW), lambda i: (i, 0)),
        out_shape=jax.ShapeDtypeStruct((S, BW), MXU_DTYPE), sem=("parallel",))(z, gv, ws, bs)


def _sgu_bwd(z, gv, ws, bs, dy, dz):
    S = z.shape[0]
    ts = _tile(S, ROW_TILE)
    nc = ts // SCHUNK

    def body(zd_ref, gv_ref, ws_ref, bs_ref, dy_ref, dz_in, dz_ref, dws_ref, dbs_ref, st_ref):
        @pl.when(pl.program_id(0) == 0)
        def _():
            dws_ref[...] = jnp.zeros_like(dws_ref)
            dbs_ref[...] = jnp.zeros_like(dbs_ref)
            st_ref[...] = jnp.zeros_like(st_ref)

        gvv, bsv = gv_ref[...], bs_ref[...]
        tril = _tri(SCHUNK)
        dz_ref[:, 3 * BW:4 * BW] = jnp.zeros((ts, BW), MXU_DTYPE)
        for c in range(nc):
            rs = slice(c * SCHUNK, (c + 1) * SCHUNK)
            zd = zd_ref[rs, :]
            dyv = dy_ref[rs, :]
            for h in range(NH):
                hs = _hs(h)
                u = zd[:, hs]
                vraw = zd[:, BW + h * DH:BW + (h + 1) * DH]
                gt = zd[:, 2 * BW + h * DH:2 * BW + (h + 1) * DH]
                gvh = gvv[:, hs]
                vn = _rms_fwd(vraw, gvh)
                wm = ws_ref[h] * tril
                s = _dot(wm, vn) + bsv[:, h:h + 1]
                sil = _silu(gt)
                d = dyv[:, hs]
                ds = d * u * sil
                dz_ref[rs, hs] = _mx(d * s * sil)
                dz_ref[rs, 2 * BW + h * DH:2 * BW + (h + 1) * DH] = _mx(d * u * s * _dsilu(gt))
                dws_ref[h] += tril * _dot_nt(ds, vn)
                dbs_ref[:, h:h + 1] += jnp.sum(ds, axis=-1, keepdims=True)
                dvn = _dot_tn(wm, ds)
                dx, dgr = _rms_bwd(vraw, gvh, dvn)
                dz_ref[rs, BW + h * DH:BW + (h + 1) * DH] = _mx(dx)
                st_ref[0:1, hs] += jnp.sum(dgr, axis=0, keepdims=True)

    return _call(
        body, name="sgu_bwd", grid=(S // ts,),
        in_specs=[pl.BlockSpec((ts, 1024), lambda i: (i, ZD // 1024)), pl.BlockSpec((1, BW), lambda i: (0, 0)),
                  pl.BlockSpec((NH, SCHUNK, SCHUNK), lambda i: (0, 0, 0)), pl.BlockSpec((SCHUNK, 128), lambda i: (0, 0)),
                  pl.BlockSpec((ts, BW), lambda i: (i, 3)), _ANY],
        out_specs=[pl.BlockSpec((ts, 1024), lambda i: (i, ZD // 1024)), pl.BlockSpec((NH, SCHUNK, SCHUNK), lambda i: (0, 0, 0)),
                   pl.BlockSpec((SCHUNK, 128), lambda i: (0, 0)), pl.BlockSpec((8, BW), lambda i: (0, 0))],
        out_shape=[jax.ShapeDtypeStruct((S, ZCOLS), MXU_DTYPE), jax.ShapeDtypeStruct((NH, SCHUNK, SCHUNK), F32),
                   jax.ShapeDtypeStruct((SCHUNK, 128), F32), jax.ShapeDtypeStruct((8, BW), F32)],
        sem=("arbitrary",), aliases={5: 0})(z, gv, ws, bs, dy, dz)


def _merge_fwd(x, z, ys, wup, mb, wo):
    S = x.shape[0]
    ts = _tile(S, MERGE_TILE)

    def body(x_ref, zm_ref, ya_ref, yb_ref, yc_ref, yd_ref, wup_ref, mb_ref, wo_ref, x1_ref, mg_ref):
        yrefs = (ya_ref, yb_ref, yc_ref, yd_ref)
        mbv = mb_ref[...]
        merged = None
        for b in range(NBR):
            cs = slice(b * D_MODEL, (b + 1) * D_MODEL)
            term = _sigmoid(zm_ref[:, cs] + mbv[b:b + 1, :]) * jnp.dot(yrefs[b][...], wup_ref[b],
                                                                      preferred_element_type=F32)
            merged = term if merged is None else merged + term
        mg = _mx(merged)
        mg_ref[...] = mg
        x1_ref[...] = x_ref[...] + jnp.dot(mg, wo_ref[...], preferred_element_type=F32)

    row = lambda w: pl.BlockSpec((ts, w), lambda i: (i, 0))
    return _call(
        body, name="merge_fwd", grid=(S // ts,),
        in_specs=[row(D_MODEL), pl.BlockSpec((ts, 4096), lambda i: (i, 0)), row(BW), row(BW), row(BW), row(BW),
                  pl.BlockSpec((NBR, BW, D_MODEL), lambda i: (0, 0, 0)), pl.BlockSpec((NBR, D_MODEL), lambda i: (0, 0)),
                  pl.BlockSpec((D_MODEL, D_MODEL), lambda i: (0, 0))],
        out_specs=[row(D_MODEL), row(D_MODEL)],
        out_shape=[jax.ShapeDtypeStruct((S, D_MODEL), F32), jax.ShapeDtypeStruct((S, D_MODEL), MXU_DTYPE)],
        sem=("parallel",))(x, z, *ys, wup, mb, wo)


def _merge_bwd(dx1, z, ys, wup, mb, wo):
    S = dx1.shape[0]
    ts = _tile(S, MERGE_TILE)

    def body(dx_ref, zm_ref, ya_ref, yb_ref, yc_ref, yd_ref, wup_ref, mb_ref, wo_ref,
             dzm_ref, du_ref, dxb_ref, dy_ref, st_ref):
        @pl.when(pl.program_id(0) == 0)
        def _():
            st_ref[...] = jnp.zeros_like(st_ref)

        yrefs = (ya_ref, yb_ref, yc_ref, yd_ref)
        mbv = mb_ref[...]
        dxb = _mx(dx_ref[...])
        dxb_ref[...] = dxb
        dmerged = _dot_nt(dxb, wo_ref[...])
        for b in range(NBR):
            cs = slice(b * D_MODEL, (b + 1) * D_MODEL)
            u = jnp.dot(yrefs[b][...], wup_ref[b], preferred_element_type=F32)
            sg = _sigmoid(zm_ref[:, cs] + mbv[b:b + 1, :])
            du = _mx(dmerged * sg)
            du_ref[:, cs] = du
            dzm = dmerged * u * sg * (1.0 - sg)
            dzm_ref[:, cs] = _mx(dzm)
            st_ref[b:b + 1, :] += jnp.sum(dzm, axis=0, keepdims=True)
            dy_ref[:, b * BW:(b + 1) * BW] = _dot_nt(du, wup_ref[b])

    row = lambda w: pl.BlockSpec((ts, w), lambda i: (i, 0))
    return _call(
        body, name="merge_bwd", grid=(S // ts,),
        in_specs=[row(D_MODEL), pl.BlockSpec((ts, 4096), lambda i: (i, 0)), row(BW), row(BW), row(BW), row(BW),
                  pl.BlockSpec((NBR, BW, D_MODEL), lambda i: (0, 0, 0)),
                  pl.BlockSpec((NBR, D_MODEL), lambda i: (0, 0)), pl.BlockSpec((D_MODEL, D_MODEL), lambda i: (0, 0))],
        out_specs=[row(4096), row(4096), row(D_MODEL), row(D_MODEL), pl.BlockSpec((8, D_MODEL), lambda i: (0, 0))],
        out_shape=[jax.ShapeDtypeStruct((S, ZCOLS), MXU_DTYPE), jax.ShapeDtypeStruct((S, 4096), MXU_DTYPE),
                   jax.ShapeDtypeStruct((S, D_MODEL), MXU_DTYPE), jax.ShapeDtypeStruct((S, D_MODEL), F32),
                   jax.ShapeDtypeStruct((8, D_MODEL), F32)],
        sem=("arbitrary",))(dx1, z, *ys, wup, mb, wo)


def _ple_fwd(x1, p, g, wg, wp):
    S = x1.shape[0]
    ts = _tile(S, ROW_TILE)

    def body(x_ref, p_ref, g_ref, wg_ref, wp_ref, o_ref):
        x = x_ref[...]
        hp = _mx(_rms_fwd(x, g_ref[...]))
        gate = _sigmoid(jnp.dot(hp, wg_ref[...], preferred_element_type=F32))
        pp = jnp.dot(_mx(p_ref[...]), wp_ref[...], preferred_element_type=F32)
        o_ref[...] = x + gate * pp

    row = lambda w: pl.BlockSpec((ts, w), lambda i: (i, 0))
    return _call(
        body, name="ple_fwd", grid=(S // ts,),
        in_specs=[row(D_MODEL), row(PLE), pl.BlockSpec((1, D_MODEL), lambda i: (0, 0)),
                  pl.BlockSpec((D_MODEL, D_MODEL), lambda i: (0, 0)), pl.BlockSpec((PLE, D_MODEL), lambda i: (0, 0))],
        out_specs=row(D_MODEL), out_shape=jax.ShapeDtypeStruct((S, D_MODEL), F32),
        sem=("parallel",))(x1, p, g, wg, wp)


def _ple_bwd(x1, p, dx2, g, wg, wp):
    S = x1.shape[0]
    ts = _tile(S, ROW_TILE)

    def body(x_ref, p_ref, dx_ref, g_ref, wg_ref, wp_ref, dx1_ref, hp_ref, dgl_ref, dpp_ref, pb_ref, st_ref):
        @pl.when(pl.program_id(0) == 0)
        def _():
            st_ref[...] = jnp.zeros_like(st_ref)

        x, gv, dx2 = x_ref[...], g_ref[...], dx_ref[...]
        hp = _mx(_rms_fwd(x, gv))
        hp_ref[...] = hp
        gate = _sigmoid(jnp.dot(hp, wg_ref[...], preferred_element_type=F32))
        pb = _mx(p_ref[...])
        pb_ref[...] = pb
        pp = jnp.dot(pb, wp_ref[...], preferred_element_type=F32)
        dgl = _mx(dx2 * pp * gate * (1.0 - gate))
        dgl_ref[...] = dgl
        dpp_ref[...] = _mx(dx2 * gate)
        dhp = _dot_nt(dgl, wg_ref[...])
        dxn, dgr = _rms_bwd(x, gv, dhp)
        dx1_ref[...] = dx2 + dxn
        st_ref[0:1, :] += jnp.sum(dgr, axis=0, keepdims=True)

    row = lambda w: pl.BlockSpec((ts, w), lambda i: (i, 0))
    sq = pl.BlockSpec((D_MODEL, D_MODEL), lambda i: (0, 0))
    return _call(
        body, name="ple_bwd", grid=(S // ts,),
        in_specs=[row(D_MODEL), row(PLE), row(D_MODEL), pl.BlockSpec((1, D_MODEL), lambda i: (0, 0)), sq,
                  pl.BlockSpec((PLE, D_MODEL), lambda i: (0, 0))],
        out_specs=[row(D_MODEL), row(D_MODEL), row(D_MODEL), row(D_MODEL), row(PLE),
                   pl.BlockSpec((8, D_MODEL), lambda i: (0, 0))],
        out_shape=[jax.ShapeDtypeStruct((S, D_MODEL), F32)] + [jax.ShapeDtypeStruct((S, D_MODEL), MXU_DTYPE)] * 3
        + [jax.ShapeDtypeStruct((S, PLE), MXU_DTYPE), jax.ShapeDtypeStruct((8, D_MODEL), F32)],
        sem=("arbitrary",))(x1, p, dx2, g, wg, wp)


def _pad_rows(a, rows=8):
    return jnp.concatenate([a, jnp.zeros((rows - a.shape[0],) + a.shape[1:], a.dtype)], axis=0)


def _pad_lanes(a, lanes=128):
    return jnp.concatenate([a, jnp.zeros(a.shape[:-1] + (lanes - a.shape[-1],), a.dtype)], axis=-1)


def _wz_from_w_in(w):
    zeros = lambda n: jnp.zeros((w.shape[0], n), w.dtype)
    return jnp.concatenate([w[:, _OM:_OEND], w[:, _OA:_OB], w[:, _OB:_OF], w[:, _OC:_OD], w[:, _OD:_OM], zeros(256),
                            w[:, _OF:_OC], zeros(124)], axis=1)


def _w_in_from_wz(g):
    return jnp.concatenate([g[:, ZA:ZB], g[:, ZB:ZC], g[:, ZF:ZF + 4], g[:, ZC:ZD], g[:, ZD:ZD + 768], g[:, ZM:ZA]], axis=1)


_W_IN_GROUPS = [(_OA, _OB, ZA), (_OB, _OF, ZB), (_OF, _OC, ZF), (_OC, _OD, ZC), (_OD, _OM, ZD), (_OM, _OEND, ZM)]


def _wz_from_shards(g):
    n = g.shape[-1]
    pieces, pos = [], 0
    for a, b, zs in sorted(_W_IN_GROUPS, key=lambda grp: grp[2]):
        if zs > pos:
            pieces.append(jnp.zeros((g.shape[1], zs - pos), g.dtype))
        for k in range(4):
            lo, hi = max(a, k * n), min(b, (k + 1) * n)
            if lo < hi:
                pieces.append(g[k][:, lo - k * n:hi - k * n])
        pos = zs + (b - a)
    pieces.append(jnp.zeros((g.shape[1], ZCOLS - pos), g.dtype))
    return jnp.concatenate(pieces, axis=1)


def _w_in_slabs_from_wz(g):
    n = _OEND // 4
    slabs = []
    for k in range(4):
        pieces = []
        for a, b, zs in _W_IN_GROUPS:
            lo, hi = max(a, k * n), min(b, (k + 1) * n)
            if lo < hi:
                pieces.append(g[:, zs + lo - a:zs + hi - a])
        slabs.append(jnp.concatenate(pieces, axis=1))
    return jnp.stack(slabs)


def _local_step(x, p, target, wts, dist=None):
    tmat_np, masks_np = _hgrn_consts()
    tmat = jnp.asarray(tmat_np, MXU_DTYPE)
    masks = jnp.asarray(masks_np, F32)
    lower = _lower_bounds_fwd(wts["lb_logits"])
    saved = []
    for li in range(DEPTH):
        big = dist.weights(li) if dist else {n: wts[n][li] for n in _GATHERED}
        wz = _wz_from_shards(big["w_in"]) if big["w_in"].ndim == 3 else _wz_from_w_in(big["w_in"])
        g_mix = wts["norm_mix"][li][None, :]
        h = _norm_fwd(x, g_mix)
        z = _mm_nn(h, wz, F32, name="mm_z", tn=1664)
        cw = _pad_rows(jnp.concatenate([big["conv_w"], wts["conv_b"][li][None, :]], axis=0))
        ya = _conv_fwd(z, cw)
        fb = _pad_lanes(wts["fgate_bias"][li][None, :])
        gq, gk = wts["q_norm"][li][None, :], wts["k_norm"][li][None, :]
        qa, ka, va, cc = _attn_prep_fwd(z, fb, gq, gk)
        o, lse, yb, got = _flash_fwd(qa, ka, va, z, comm=dist.fwd_comm(li) if dist else None)
        if dist:
            dist.fwd_done(li, got)
        lb = lower[li][None, :]
        gh = wts["hgrn_norm"][li][None, :]
        o_pre, states, yc = _hgrn_fwd(z, lb, gh, tmat, masks)
        gv = wts["sgu_norm"][li][None, :]
        ws = wts["spatial_w"][li]
        bs = _pad_lanes(wts["spatial_b"][li].T)
        yd = _sgu_fwd(z, gv, ws, bs)
        ys = (ya, yb, yc, yd)
        x1, merged = _merge_fwd(x, z, ys, big["w_up"], big["merge_b"], big["w_o"])
        g_ple = wts["norm_ple"][li][None, :]
        x2 = _ple_fwd(x1, p[li], g_ple, big["w_ple_gate"], big["w_ple_proj"])
        saved.append(dict(x=x, h=h, z=z, wz=wz, cw=cw, fb=fb, gq=gq, gk=gk, qa=qa, ka=ka, va=va, cc=cc, o=o, lse=lse,
                          lb=lb, gh=gh, o_pre=o_pre, states=states, gv=gv, ws=ws, bs=bs, ys=ys, x1=x1, merged=merged,
                          g_mix=g_mix, g_ple=g_ple, big=big))
        x = x2

    loss, dx = _loss_fwd_bwd(x, target)

    names = ["norm_mix", "w_in", "conv_w", "conv_b", "fgate_bias", "q_norm", "k_norm", "lb", "hgrn_norm", "sgu_norm",
             "spatial_w", "spatial_b", "w_up", "merge_b", "w_o", "norm_ple", "w_ple_gate", "w_ple_proj"]
    gl = {n: [None] * DEPTH for n in names}
    for li in reversed(range(DEPTH)):
        s = saved[li]
        z, big = s["z"], s["big"]
        wg, wp = big["w_ple_gate"], big["w_ple_proj"]
        dx1, hp, dgl, dpp, pb, st = _ple_bwd(s["x1"], p[li], dx, s["g_ple"], wg, wp)
        gl["norm_ple"][li] = st[0]
        gl["w_ple_gate"][li] = _mm_tn(hp, dgl, name="mm_dwg")
        gl["w_ple_proj"][li] = _mm_tn(pb, dpp, name="mm_dwp")
        wup, wo = big["w_up"], big["w_o"]
        dz, du, dxb, dy, st = _merge_bwd(dx1, z, s["ys"], wup, big["merge_b"], wo)
        gl["merge_b"][li] = st[0:NBR]
        gl["w_o"][li] = _mm_tn(s["merged"], dxb, name="mm_dwo")
        gl["w_up"][li] = jnp.stack([_mm_tn(s["ys"][b], du, name="mm_dwup", ycol=b, n=D_MODEL) for b in range(NBR)])
        dz, st = _conv_bwd(z, s["cw"], dy, dz)
        gl["conv_w"][li] = st[0:3]
        gl["conv_b"][li] = st[3]
        dgate, qb, doa = _attn_gate_bwd(dy, s["o"], z, s["qa"], s["cc"], s["lse"])
        dq, doa, got = _flash_bwd_dq(qb, s["ka"], s["va"], doa, comm=dist.bwd_comm() if dist else None)
        if dist:
            dist.bwd_done(got)
        dk, dv, dcc = _flash_bwd_dkv(qb, s["ka"], s["va"], doa)
        dz, dzf, st = _attn_prep_bwd(z, s["fb"], s["gq"], s["gk"], dq, dk, dv, dgate, dcc, dz)
        dz = _put_fgate_cols(dzf, dz)
        gl["q_norm"][li] = st[0, 0:DH]
        gl["k_norm"][li] = st[1, 0:DH]
        gl["fgate_bias"][li] = st[2, 0:NH]
        dz, st = _hgrn_bwd(z, s["lb"], s["gh"], tmat, masks, s["o_pre"], s["states"], dy, dz)
        gl["hgrn_norm"][li] = st[0]
        gl["lb"][li] = st[1]
        dz, dws, dbs, st = _sgu_bwd(z, s["gv"], s["ws"], s["bs"], dy, dz)
        gl["sgu_norm"][li] = st[0]
        gl["spatial_w"][li] = dws
        gl["spatial_b"][li] = dbs[:, 0:NH].T
        dwz = _mm_tn(s["h"], dz, name="mm_dwz", tn=1664)
        gl["w_in"][li] = dwz if dist else _w_in_from_wz(dwz)
        dh = _mm_nn(dz, s["wz"], F32, name="mm_dh", ts=ROW_TILE // 2, b_transposed=True)
        dx, st = _norm_bwd(s["x"], s["g_mix"], dh, dx1)
        gl["norm_mix"][li] = st[0]
        if dist:
            dist.push(li, {n: gl[n][li] for n in _BIG})

    if dist:
        dist.finish()
    grads = {n: jnp.stack(v) for n, v in gl.items() if not (dist and n in _BIG)}
    grads["lb_logits"] = _lower_bounds_bwd(wts["lb_logits"], grads.pop("lb"))
    return loss, dx, grads


def _my_pos():
    return lax.axis_index("x"), lax.axis_index("y"), lax.axis_index("c")


def _gather_phase(phase, ins, outs, sems):
    if phase == 1:
        return
    send, recv, lsem = sems
    x, y, c = _my_pos()
    me = 2 * x + y
    peers = [(1 - x, y), (x, 1 - y), (1 - x, 1 - y)]
    copies = []
    for t in range(len(ins)):
        copies.append(pltpu.make_async_copy(ins[t], outs[t].at[me], lsem.at[t]))
        for j, (px, py) in enumerate(peers):
            copies.append(pltpu.make_async_remote_copy(
                src_ref=ins[t], dst_ref=outs[t].at[me], send_sem=send.at[t, j], recv_sem=recv.at[t, j],
                device_id=(px, py, c), device_id_type=MESH))
    for cp in copies:
        if phase == 0:
            cp.start()
        else:
            cp.wait()


def _exchange_phase(phase, ins, outs, sems):
    send, recv, lsem = sems
    x, y, c = _my_pos()
    me = 2 * x + y
    sib = (x, y, 1 - c)
    chips = [(1 - x, y), (x, 1 - y), (1 - x, 1 - y)]

    def rc(t, k, src, dst, dev):
        return pltpu.make_async_remote_copy(src_ref=src, dst_ref=dst, send_sem=send.at[t, k], recv_sem=recv.at[t, k],
                                            device_id=dev, device_id_type=MESH)

    for t in range(len(ins)):
        local = pltpu.make_async_copy(ins[t].at[me], outs[t].at[2 * me + c], lsem.at[t])
        first = [rc(t, 0, ins[t].at[me], outs[t].at[2 * me + c], sib)]
        first += [rc(t, 1 + j, ins[t].at[2 * px + py], outs[t].at[2 * me + c], (px, py, c))
                  for j, (px, py) in enumerate(chips)]
        landed = [outs[t].at[2 * (2 * px + py) + c] for px, py in chips]
        passed = [rc(t, 4 + j, slot, slot, sib) for j, slot in enumerate(landed)]
        if phase == 0:
            local.start()
            for cp in first:
                cp.start()
        elif phase == 1:
            for j, slot in enumerate(landed):
                rc(t, 1 + j, slot, slot, (x, y, c)).wait_recv()
                passed[j].start()
        else:
            s0 = outs[t].at[2 * me + (1 - c)]
            rc(t, 0, s0, s0, (x, y, c)).wait_recv()
            for j, (px, py) in enumerate(chips):
                slot = outs[t].at[2 * (2 * px + py) + (1 - c)]
                rc(t, 4 + j, slot, slot, (x, y, c)).wait_recv()
            for cp in first + passed:
                cp.wait_send()
            local.wait()


_COMM_PHASES = {"gather": _gather_phase, "exchange": _exchange_phase}
_COMM_OUT = {"gather": lambda arrays: [jax.ShapeDtypeStruct((4,) + a.shape, a.dtype) for a in arrays],
             "exchange": lambda arrays: [jax.ShapeDtypeStruct((8,) + a.shape[1:], a.dtype) for a in arrays]}


def _comm_sems(kind, nc):
    k = 3 if kind == "gather" else 7
    return [pltpu.SemaphoreType.DMA((nc, k)), pltpu.SemaphoreType.DMA((nc, k)), pltpu.SemaphoreType.DMA((nc,))]


def _comm_alone(kind, arrays, name):
    nc = len(arrays)

    def body(*refs):
        for phase in range(3):
            _COMM_PHASES[kind](phase, refs[:nc], refs[nc:2 * nc], refs[2 * nc:])

    hbm = pl.BlockSpec(memory_space=pl.ANY)
    return pl.pallas_call(
        functools.partial(body), name=name, in_specs=[hbm] * nc, out_specs=[hbm] * nc,
        out_shape=_COMM_OUT[kind](arrays), scratch_shapes=_comm_sems(kind, nc))(*arrays)


def _allreduce_small(v):
    R = v.shape[0]

    def body(v_ref, o_ref, buf, send, recv):
        x, y, c = _my_pos()
        me = 4 * x + 2 * y + c
        buf[me] = v_ref[...]
        copies = []
        k = 0
        for dx in range(2):
            for dy in range(2):
                for dc in range(2):
                    if dx == 0 and dy == 0 and dc == 0:
                        continue
                    cp = pltpu.make_async_remote_copy(
                        src_ref=v_ref, dst_ref=buf.at[me], send_sem=send.at[k - 0], recv_sem=recv.at[k],
                        device_id=(jnp.bitwise_xor(x, dx), jnp.bitwise_xor(y, dy), jnp.bitwise_xor(c, dc)),
                        device_id_type=MESH)
                    cp.start()
                    copies.append(cp)
                    k += 1
        for cp in copies:
            cp.wait()
        acc = buf[0]
        for d in range(1, 8):
            acc = acc + buf[d]
        o_ref[...] = acc

    vm = pl.BlockSpec(memory_space=pltpu.VMEM)
    return pl.pallas_call(
        functools.partial(body), name="allreduce_small", in_specs=[vm], out_specs=vm,
        out_shape=jax.ShapeDtypeStruct((R, 128), F32),
        scratch_shapes=[pltpu.VMEM((8, R, 128), F32), pltpu.SemaphoreType.DMA((7,)), pltpu.SemaphoreType.DMA((7,))],
        compiler_params=pltpu.CompilerParams(vmem_limit_bytes=VMEM_LIMIT))(v)


def _adamw(w, m, v, parts, nparts):
    A, R, C = w.shape
    per_layer = isinstance(parts, (list, tuple))
    tr = _tile(R, 64) if per_layer else (R if R <= 128 else _tile(R, 128))
    nparr = len(parts) if per_layer else 1

    def body(*refs):
        w_ref, m_ref, v_ref = refs[:3]
        p_refs = refs[3:3 + nparr]
        g_ref, d_ref, nm_ref, nv_ref = refs[3 + nparr:]

        def update(read):
            g = read(0).astype(F32)
            for k in range(1, nparts):
                g = g + read(k).astype(F32)
            mm = ADAM_B1 * m_ref[0] + (1.0 - ADAM_B1) * g
            vv = ADAM_B2 * v_ref[0] + (1.0 - ADAM_B2) * jnp.square(g)
            m_hat = mm / (1.0 - ADAM_B1 ** ADAM_STEP)
            v_hat = vv / (1.0 - ADAM_B2 ** ADAM_STEP)
            g_ref[0] = g
            d_ref[0] = -ADAM_LR * (m_hat / (jnp.sqrt(v_hat) + ADAM_EPS) + ADAM_WD * w_ref[0])
            nm_ref[0] = mm
            nv_ref[0] = vv

        if not per_layer:
            update(lambda k: p_refs[0][k, 0])
        else:
            for a in range(A):
                @pl.when(pl.program_id(0) == a)
                def _(a=a):
                    update(lambda k: p_refs[a][k])

    blk = pl.BlockSpec((1, tr, C), lambda a, r: (a, r, 0))
    if per_layer:
        pspecs = [pl.BlockSpec((nparts, tr, C), functools.partial(lambda a, r, l: (0, jnp.where(a == l, r, 0), 0), l=l))
                  for l in range(A)]
        pargs = list(parts)
    else:
        pspecs = [pl.BlockSpec((nparts, 1, tr, C), lambda a, r: (0, a, r, 0))]
        pargs = [parts]
    return _call(
        body, name="adamw", grid=(A, R // tr), in_specs=[blk, blk, blk] + pspecs,
        out_specs=[blk] * 4, out_shape=[jax.ShapeDtypeStruct((A, R, C), F32)] * 4,
        sem=("arbitrary", "arbitrary"))(w, m, v, *pargs)


def _rows128(a):
    width = a.shape[-1]
    a2 = a.reshape(-1, width)
    k = -(-width // 128)
    if width % 128:
        a2 = jnp.pad(a2, ((0, 0), (0, k * 128 - width)))
    return a2.reshape(-1, 128)


def _from_rows128(r, shape):
    width = shape[-1]
    k = -(-width // 128)
    return r.reshape(-1, k * 128)[:, :width].reshape(shape)


def _as3d(a):
    if a.ndim == 2:
        return a[None]
    if a.ndim == 3:
        return a
    return a.reshape((-1,) + a.shape[-2:])


_WEIGHTS = ["norm_mix", "w_in", "conv_w", "conv_b", "fgate_bias", "q_norm", "k_norm", "lb_logits", "hgrn_norm", "sgu_norm",
            "spatial_w", "spatial_b", "w_up", "merge_b", "w_o", "norm_ple", "w_ple_gate", "w_ple_proj"]
_BIG = ["w_in", "w_up", "w_o", "w_ple_gate", "w_ple_proj"]
_GATHERED = _BIG + ["conv_w", "merge_b"]
_SHARD_AXIS = {"w_in": 2, "w_up": 3, "w_o": 1, "w_ple_gate": 1, "w_ple_proj": 2, "conv_w": 2, "merge_b": 2}
_SMALL = [n for n in _WEIGHTS if n not in _BIG]


class _Dist:
    def __init__(self, w):
        self.w = w
        self.full = {0: self._unpack(_comm_alone("gather", self._shards(0), "gather_weights"))}
        self.contrib = {}
        self.pending = None

    def _shards(self, l):
        return [_mx(self.w[n][l]) for n in _BIG] + [self.w["conv_w"][l], self.w["merge_b"][l]]

    def _unpack(self, gathered):
        return {n: g if n == "w_in" else jnp.concatenate([g[k] for k in range(4)], axis=_SHARD_AXIS[n] - 1)
                for n, g in zip(_GATHERED, gathered)}

    def weights(self, l):
        return self.full[l]

    def fwd_comm(self, l):
        return ("gather", self._shards(l + 1)) if l + 1 < DEPTH else None

    def fwd_done(self, l, got):
        if got:
            self.full[l + 1] = self._unpack(got)

    def push(self, l, grads):
        self.pending = (l, [(_w_in_slabs_from_wz(grads[n]) if n == "w_in" else
                             jnp.stack(jnp.split(grads[n], 4, axis=_SHARD_AXIS[n] - 1))).astype(GRAD_WIRE_DTYPE)
                            for n in _BIG])

    def bwd_comm(self):
        return ("exchange", self.pending[1]) if self.pending else None

    def bwd_done(self, got):
        if got:
            self.contrib[self.pending[0]] = got
            self.pending = None

    def finish(self):
        if self.pending:
            self.contrib[self.pending[0]] = _comm_alone("exchange", self.pending[1], "exchange_grads")
            self.pending = None


def kernel(x, p, norm_mix, w_in, conv_w, conv_b, fgate_bias, q_norm, k_norm, lb_logits, hgrn_norm, sgu_norm, spatial_w, spatial_b, w_up, merge_b, w_o, norm_ple, w_ple_gate, w_ple_proj, loss_target, m_norm_mix, m_w_in, m_conv_w, m_conv_b, m_fgate_bias, m_q_norm, m_k_norm, m_lb_logits, m_hgrn_norm, m_sgu_norm, m_spatial_w, m_spatial_b, m_w_up, m_merge_b, m_w_o, m_norm_ple, m_w_ple_gate, m_w_ple_proj, v_norm_mix, v_w_in, v_conv_w, v_conv_b, v_fgate_bias, v_q_norm, v_k_norm, v_lb_logits, v_hgrn_norm, v_sgu_norm, v_spatial_w, v_spatial_b, v_w_up, v_merge_b, v_w_o, v_norm_ple, v_w_ple_gate, v_w_ple_proj):
    loc = dict(locals())
    w = {n: loc[n] for n in _WEIGHTS}
    m = {n: loc["m_" + n] for n in _WEIGHTS}
    v = {n: loc["v_" + n] for n in _WEIGHTS}
    chip = 2 * lax.axis_index("x") + lax.axis_index("y")

    dist = _Dist(w)
    loss_blk, dx, grads = _local_step(x[0], p[:, 0], loss_target[0], w, dist)
    loss = lax.psum(loss_blk[0, 0], ("x", "y", "c"))

    blocks = [_rows128(grads[n]) for n in _SMALL]
    nrows = sum(b.shape[0] for b in blocks)
    packed = jnp.concatenate(blocks + [jnp.zeros(((-nrows) % 8, 128), F32)], axis=0)
    red = _allreduce_small(packed)
    small, off = {}, 0
    for n, b in zip(_SMALL, blocks):
        small[n] = _from_rows128(red[off:off + b.shape[0]], grads[n].shape)
        off += b.shape[0]
    for n in ("conv_w", "merge_b"):
        ax = _SHARD_AXIS[n]
        width = small[n].shape[ax] // 4
        small[n] = lax.dynamic_slice_in_dim(small[n], chip * width, width, axis=ax)

    out_g, out_d, out_m, out_v = {}, {}, {}, {}
    for n in _WEIGHTS:
        shp = w[n].shape
        if n in _BIG:
            w3, m3, v3 = (a.reshape((DEPTH, -1, shp[-1])) for a in (w[n], m[n], v[n]))
            parts = [dist.contrib[l][_BIG.index(n)].reshape((8,) + w3.shape[1:]) for l in range(DEPTH)]
            g, d, nm, nv = _adamw(w3, m3, v3, parts, 8)
        else:
            g, d, nm, nv = _adamw(_as3d(w[n]), _as3d(m[n]), _as3d(v[n]), _as3d(small[n])[None], 1)
        out_g[n], out_d[n], out_m[n], out_v[n] = (a.reshape(shp) for a in (g, d, nm, nv))

    return (loss, dx[None], *[out_g[n] for n in _WEIGHTS], *[out_d[n] for n in _WEIGHTS],
            *[out_m[n] for n in _WEIGHTS], *[out_v[n] for n in _WEIGHTS])
```

```python
import functools

import numpy as np
import jax
import jax.numpy as jnp
from jax import lax
from jax.experimental import pallas as pl
from jax.experimental.pallas import tpu as pltpu

F32 = jnp.float32
MXU_DTYPE = jnp.bfloat16

D_MODEL = 1024
BW = 256
NH = 4
DH = 64
DEPTH = 4
NBR = 4
PLE = 256
CHUNK = 64
SCHUNK = 128
EPS = 1e-6
MASK_VALUE = -1e30
NLEV = 6

ADAM_LR, ADAM_B1, ADAM_B2, ADAM_EPS, ADAM_WD, ADAM_STEP = 0.001, 0.9, 0.999, 1e-08, 0.01, 10

ZM, ZA, ZB, ZC, ZD, ZF = 0, 4096, 5120, 6144, 7168, 8192
ZCOLS = 8320
_OA, _OB, _OF, _OC, _OD, _OM, _OEND = 0, 1024, 2048, 2052, 3076, 3844, 7940

VMEM_LIMIT = 56 * 1024 * 1024
ROW_TILE = 512
FLASH_TILE = 1024
GRAD_WIRE_DTYPE = jnp.bfloat16
MERGE_TILE = 256
MESH = pl.DeviceIdType.MESH
_ANY = pl.BlockSpec(memory_space=pl.ANY)


def _tile(n, pref):
    t = min(n, pref)
    assert n % t == 0, (n, t)
    return t


def _call(body, *, name, grid, in_specs, out_specs, out_shape, scratch=(), sem=None, aliases=None):
    return pl.pallas_call(
        functools.partial(body), name=name, grid=grid, in_specs=in_specs, out_specs=out_specs,
        out_shape=out_shape, scratch_shapes=list(scratch), input_output_aliases=aliases or {},
        compiler_params=pltpu.CompilerParams(dimension_semantics=sem, vmem_limit_bytes=VMEM_LIMIT))


def _mx(x):
    return x.astype(MXU_DTYPE)


def _dot(a, b):
    return jnp.dot(_mx(a), _mx(b), preferred_element_type=F32)


def _dot_nt(a, b):
    return lax.dot_general(_mx(a), _mx(b), (((1,), (1,)), ((), ())), preferred_element_type=F32)


def _dot_tn(a, b):
    return lax.dot_general(_mx(a), _mx(b), (((0,), (0,)), ((), ())), preferred_element_type=F32)


def _top16(x):
    u = lax.bitcast_convert_type(x, jnp.uint32) & jnp.uint32(0xFFFF0000)
    return lax.bitcast_convert_type(u, F32)


def _split3(x):
    hi = _top16(x)
    r1 = x - hi
    mid = _top16(r1)
    return _mx(hi), _mx(mid), _mx(r1 - mid)


def _dot_exact_lhs01(t, x):
    hi, mid, lo = _split3(x)
    t = _mx(t)
    return (jnp.dot(t, hi, preferred_element_type=F32) + jnp.dot(t, mid, preferred_element_type=F32)
            + jnp.dot(t, lo, preferred_element_type=F32))


def _sigmoid(x):
    return jax.nn.sigmoid(x)


def _silu(x):
    return x * _sigmoid(x)


def _dsilu(x):
    s = _sigmoid(x)
    return s * (1.0 + x * (1.0 - s))


def _log_sigmoid(x):
    return jnp.minimum(x, 0.0) - jnp.log(1.0 + jnp.exp(-jnp.abs(x)))


def _rms_fwd(x, g):
    r = lax.rsqrt(jnp.mean(x * x, axis=-1, keepdims=True) + EPS)
    return x * r * g


def _rms_bwd(x, g, dy):
    r = lax.rsqrt(jnp.mean(x * x, axis=-1, keepdims=True) + EPS)
    n = x * r
    t = dy * g
    dx = r * (t - n * jnp.mean(t * n, axis=-1, keepdims=True))
    return dx, dy * n


def _tri(n, upper=False):
    r = lax.broadcasted_iota(jnp.int32, (n, n), 0)
    c = lax.broadcasted_iota(jnp.int32, (n, n), 1)
    return jnp.where((c >= r) if upper else (r >= c), 1.0, 0.0).astype(F32)


def _hs(h):
    return slice(h * DH, (h + 1) * DH)


def _mm_nn(a, b, out_dtype, *, name, ts=None, tn=None, tk=None, b_transposed=False, comm=None):
    S, K = a.shape
    N = b.shape[0] if b_transposed else b.shape[1]
    ts = _tile(S, ts or ROW_TILE)
    tn = _tile(N, tn or N)
    tk = _tile(K, tk or K)
    nk = K // tk
    b_spec = (pl.BlockSpec((tn, tk), lambda j, i, k: (j, k)) if b_transposed
              else pl.BlockSpec((tk, tn), lambda j, i, k: (k, j)))

    def body(a_ref, b_ref, o_ref, acc_ref):
        k = pl.program_id(2)
        part = _dot_nt(a_ref[...], b_ref[...]) if b_transposed else jnp.dot(a_ref[...], b_ref[...],
                                                                          preferred_element_type=F32)
        if nk == 1:
            o_ref[...] = part.astype(o_ref.dtype)
        else:
            @pl.when(k == 0)
            def _():
                acc_ref[...] = part

            @pl.when(k > 0)
            def _():
                acc_ref[...] += part

            @pl.when(k == nk - 1)
            def _():
                o_ref[...] = acc_ref[...].astype(o_ref.dtype)

    grid = (N // tn, S // ts, nk)

    def when():
        at = [pl.program_id(d) for d in range(3)]
        first = (at[0] == 0) & (at[1] == 0) & (at[2] == 0)
        last = (at[0] == grid[0] - 1) & (at[1] == grid[1] - 1) & (at[2] == grid[2] - 1)
        return first, last, last

    xin, xout, xshape, xsem = _comm_specs(comm)
    res = _call(
        _fuse_comm(body, 2, 1, comm, when), name=name, grid=grid,
        in_specs=[pl.BlockSpec((ts, tk), lambda j, i, k: (i, k)), b_spec] + xin,
        out_specs=[pl.BlockSpec((ts, tn), lambda j, i, k: (i, j))] + xout,
        out_shape=[jax.ShapeDtypeStruct((S, N), out_dtype)] + xshape,
        scratch=[pltpu.VMEM((ts, tn) if nk > 1 else (8, 128), F32)] + xsem,
        sem=("arbitrary", "arbitrary", "arbitrary") if comm else ("parallel", "parallel", "arbitrary"),
        )(a, b, *(comm[1] if comm else []))
    return (res[0], list(res[1:])) if comm else res[0]


def _mm_tn(x, y, *, name, tn=None, ycol=0, n=None):
    S, M = x.shape
    n = n or y.shape[1]
    ts = _tile(S, 4 * ROW_TILE)
    tn = _tile(n, tn or n)
    nj = n // tn

    def body(x_ref, y_ref, o_ref):
        @pl.when(pl.program_id(1) == 0)
        def _():
            o_ref[...] = jnp.zeros_like(o_ref)

        o_ref[...] += lax.dot_general(x_ref[...], y_ref[...], (((0,), (0,)), ((), ())), preferred_element_type=F32)

    return _call(
        body, name=name, grid=(nj, S // ts),
        in_specs=[pl.BlockSpec((ts, M), lambda j, s: (s, 0)), pl.BlockSpec((ts, tn), lambda j, s: (s, ycol * nj + j))],
        out_specs=pl.BlockSpec((M, tn), lambda j, s: (0, j)),
        out_shape=jax.ShapeDtypeStruct((M, n), F32),
        sem=("parallel", "arbitrary"))(x, y)


def _norm_fwd(x, g):
    S = x.shape[0]
    ts = _tile(S, ROW_TILE)

    def body(x_ref, g_ref, h_ref):
        h_ref[...] = _mx(_rms_fwd(x_ref[...], g_ref[...]))

    return _call(
        body, name="norm_fwd", grid=(S // ts,),
        in_specs=[pl.BlockSpec((ts, D_MODEL), lambda i: (i, 0)), pl.BlockSpec((1, D_MODEL), lambda i: (0, 0))],
        out_specs=pl.BlockSpec((ts, D_MODEL), lambda i: (i, 0)),
        out_shape=jax.ShapeDtypeStruct((S, D_MODEL), MXU_DTYPE), sem=("parallel",))(x, g)


def _norm_bwd(x, g, dh, dres):
    S = x.shape[0]
    ts = _tile(S, ROW_TILE)

    def body(x_ref, g_ref, dh_ref, dr_ref, dx_ref, st_ref):
        @pl.when(pl.program_id(0) == 0)
        def _():
            st_ref[...] = jnp.zeros_like(st_ref)

        dx, dgr = _rms_bwd(x_ref[...], g_ref[...], dh_ref[...])
        dx_ref[...] = dr_ref[...] + dx
        st_ref[0:1, :] += jnp.sum(dgr, axis=0, keepdims=True)

    row = pl.BlockSpec((ts, D_MODEL), lambda i: (i, 0))
    return _call(
        body, name="norm_bwd", grid=(S // ts,),
        in_specs=[row, pl.BlockSpec((1, D_MODEL), lambda i: (0, 0)), row, row],
        out_specs=[row, pl.BlockSpec((8, D_MODEL), lambda i: (0, 0))],
        out_shape=[jax.ShapeDtypeStruct((S, D_MODEL), F32), jax.ShapeDtypeStruct((8, D_MODEL), F32)],
        sem=("arbitrary",))(x, g, dh, dres)


def _loss_fwd_bwd(y, target):
    S = y.shape[0]
    ts = _tile(S, ROW_TILE)

    def body(y_ref, t_ref, l_ref, dy_ref):
        @pl.when(pl.program_id(0) == 0)
        def _():
            l_ref[...] = jnp.zeros_like(l_ref)

        err = y_ref[...] - t_ref[...]
        dy_ref[...] = err * (1.0 / D_MODEL)
        rowloss = jnp.mean(err * err, axis=-1, keepdims=True)
        l_ref[...] += 0.5 * jnp.sum(rowloss, axis=0, keepdims=True)

    row = pl.BlockSpec((ts, D_MODEL), lambda i: (i, 0))
    return _call(
        body, name="loss", grid=(S // ts,), in_specs=[row, row],
        out_specs=[pl.BlockSpec((8, 128), lambda i: (0, 0)), row],
        out_shape=[jax.ShapeDtypeStruct((8, 128), F32), jax.ShapeDtypeStruct((S, D_MODEL), F32)],
        sem=("arbitrary",))(y, target)


def _shift_down(x, k, halo, rows):
    y = pltpu.roll(x, k, 0)
    for j in range(k):
        y = jnp.where(rows == j, halo[8 - k + j:8 - k + j + 1, :], y)
    return y


def _shift_up(x, k, halo, rows, n):
    y = pltpu.roll(x, n - k, 0)
    for j in range(k):
        y = jnp.where(rows == n - k + j, halo[j:j + 1, :], y)
    return y


def _conv_parts(za, zh, cw, first, rows):
    ax, ab, ac, ag = za[:, 0:BW], za[:, BW:2 * BW], za[:, 2 * BW:3 * BW], za[:, 3 * BW:4 * BW]
    zz = ac * ax
    hz = jnp.where(first, 0.0, zh[:, 2 * BW:3 * BW] * zh[:, 0:BW])
    zz1 = _shift_down(zz, 1, hz, rows)
    zz2 = _shift_down(zz, 2, hz, rows)
    conv = zz2 * cw[0:1, :] + zz1 * cw[1:2, :] + zz * cw[2:3, :] + cw[3:4, :]
    return ax, ab, ac, ag, zz, zz1, zz2, conv


def _conv_fwd(z, cw):
    S = z.shape[0]
    ts = _tile(S, ROW_TILE)
    hb = ts // 8

    def body(za_ref, zh_ref, cw_ref, y_ref):
        i = pl.program_id(0)
        rows = lax.broadcasted_iota(jnp.int32, (ts, BW), 0)
        ax, ab, ac, ag, zz, zz1, zz2, conv = _conv_parts(za_ref[...], zh_ref[...], cw_ref[...], i == 0, rows)
        y_ref[...] = _mx(ab * conv * _silu(ag))

    return _call(
        body, name="conv_fwd", grid=(S // ts,),
        in_specs=[pl.BlockSpec((ts, 1024), lambda i: (i, ZA // 1024)),
                  pl.BlockSpec((8, 1024), lambda i: (jnp.maximum(i * hb - 1, 0), ZA // 1024)),
                  pl.BlockSpec((8, BW), lambda i: (0, 0))],
        out_specs=pl.BlockSpec((ts, BW), lambda i: (i, 0)),
        out_shape=jax.ShapeDtypeStruct((S, BW), MXU_DTYPE), sem=("parallel",))(z, z, cw)


def _conv_bwd(z, cw, dy, dz):
    S = z.shape[0]
    ts = _tile(S, ROW_TILE)
    hb = ts // 8
    nt = S // ts

    def body(za_ref, zh_ref, zn_ref, cw_ref, dy_ref, dyn_ref, dz_in, dz_ref, st_ref):
        i = pl.program_id(0)

        @pl.when(i == 0)
        def _():
            st_ref[...] = jnp.zeros_like(st_ref)

        cw = cw_ref[...]
        rows = lax.broadcasted_iota(jnp.int32, (ts, BW), 0)
        ax, ab, ac, ag, zz, zz1, zz2, conv = _conv_parts(za_ref[...], zh_ref[...], cw, i == 0, rows)
        dy = dy_ref[...]
        sg = _silu(ag)
        dc = dy * ab * sg
        zn = zn_ref[...]
        dcn = jnp.where(i == nt - 1, 0.0, dyn_ref[...] * zn[:, BW:2 * BW] * _silu(zn[:, 3 * BW:4 * BW]))
        dc1 = _shift_up(dc, 1, dcn, rows, ts)
        dc2 = _shift_up(dc, 2, dcn, rows, ts)
        dzz = dc * cw[2:3, :] + dc1 * cw[1:2, :] + dc2 * cw[0:1, :]
        dz_ref[:, 0:BW] = _mx(dzz * ac)
        dz_ref[:, BW:2 * BW] = _mx(dy * conv * sg)
        dz_ref[:, 2 * BW:3 * BW] = _mx(dzz * ax)
        dz_ref[:, 3 * BW:4 * BW] = _mx(dy * ab * conv * _dsilu(ag))
        st_ref[0:1, :] += jnp.sum(dc * zz2, axis=0, keepdims=True)
        st_ref[1:2, :] += jnp.sum(dc * zz1, axis=0, keepdims=True)
        st_ref[2:3, :] += jnp.sum(dc * zz, axis=0, keepdims=True)
        st_ref[3:4, :] += jnp.sum(dc, axis=0, keepdims=True)

    return _call(
        body, name="conv_bwd", grid=(nt,),
        in_specs=[pl.BlockSpec((ts, 1024), lambda i: (i, ZA // 1024)),
                  pl.BlockSpec((8, 1024), lambda i: (jnp.maximum(i * hb - 1, 0), ZA // 1024)),
                  pl.BlockSpec((8, 1024), lambda i: (jnp.minimum((i + 1) * hb, S // 8 - 1), ZA // 1024)),
                  pl.BlockSpec((8, BW), lambda i: (0, 0)),
                  pl.BlockSpec((ts, BW), lambda i: (i, 0)),
                  pl.BlockSpec((8, BW), lambda i: (jnp.minimum((i + 1) * hb, S // 8 - 1), 0)), _ANY],
        out_specs=[pl.BlockSpec((ts, 1024), lambda i: (i, ZA // 1024)), pl.BlockSpec((8, BW), lambda i: (0, 0))],
        out_shape=[jax.ShapeDtypeStruct((S, ZCOLS), MXU_DTYPE), jax.ShapeDtypeStruct((8, BW), F32)],
        sem=("arbitrary",), aliases={6: 0})(z, z, z, cw, dy, dy, dz)


AW = 128
_AUG = DH


def _split3_f32(x):
    hi = _top16(x)
    r1 = x - hi
    mid = _top16(r1)
    return hi, mid, r1 - mid


def _aug_lanes(n, cols):
    lane = lax.broadcasted_iota(jnp.int32, (n, DH), 1)
    out = jnp.zeros((n, DH), F32)
    for e, c in enumerate(cols):
        out = jnp.where(lane == e, c, out)
    return out


def _put_aug(ref, h, col, parts=None, const=None):
    base = h * AW + _AUG + col
    if parts is None:
        ref[:, base:base + 3] = jnp.full((ref.shape[0], 3), const, ref.dtype)
    else:
        for e, part in enumerate(parts):
            ref[:, base + e:base + e + 1] = part.astype(ref.dtype)


def _attn_prep_fwd(z, fb, gq, gk):
    S = z.shape[0]
    ts = _tile(S, ROW_TILE)

    def body(zb_ref, zf_ref, fb_ref, gq_ref, gk_ref, qa_ref, ka_ref, va_ref, cc_ref, carry):
        @pl.when(pl.program_id(0) == 0)
        def _():
            carry[...] = jnp.zeros_like(carry)

        zb = zb_ref[...]
        gqv, gkv = gq_ref[...], gk_ref[...]
        lf = _log_sigmoid(zf_ref[...] + fb_ref[...])
        cum = _dot_exact_lhs01(_tri(ts), lf) + carry[...]
        carry[...] = cum[ts - 1:ts, :]
        cc_ref[...] = cum
        pieces = _split3_f32(cum)
        for h in range(NH):
            ph = [pc[:, h:h + 1] for pc in pieces]
            qh = _rms_fwd(zb[:, _hs(h)], gqv) * (DH ** -0.5)
            kh = _rms_fwd(zb[:, BW + h * DH:BW + (h + 1) * DH], gkv)
            vh = zb[:, 2 * BW + h * DH:2 * BW + (h + 1) * DH]
            qa_ref[:, _ah(h)] = _mx(jnp.concatenate([qh, _aug_lanes(ts, ph + [1.0] * 3)], axis=1))
            ka_ref[:, _ah(h)] = _mx(jnp.concatenate([kh, _aug_lanes(ts, [1.0] * 3 + [-x for x in ph])], axis=1))
            va_ref[:, _ah(h)] = _mx(jnp.concatenate([vh, _aug_lanes(ts, [-1.0] * 3)], axis=1))

    row = lambda w: pl.BlockSpec((ts, w), lambda i: (i, 0))
    return _call(
        body, name="attn_prep_fwd", grid=(S // ts,),
        in_specs=[pl.BlockSpec((ts, 1024), lambda i: (i, ZB // 1024)), pl.BlockSpec((ts, 128), lambda i: (i, ZF // 128)),
                  pl.BlockSpec((1, 128), lambda i: (0, 0)), pl.BlockSpec((1, DH), lambda i: (0, 0)),
                  pl.BlockSpec((1, DH), lambda i: (0, 0))],
        out_specs=[row(NH * AW), row(NH * AW), row(NH * AW), row(128)],
        out_shape=[jax.ShapeDtypeStruct((S, NH * AW), MXU_DTYPE)] * 3 + [jax.ShapeDtypeStruct((S, 128), F32)],
        scratch=[pltpu.VMEM((1, 128), F32)], sem=("arbitrary",))(z, z, fb, gq, gk)


ROW_CHUNK = 256


def _square_steps(n):
    def when():
        i, j = pl.program_id(0), pl.program_id(1)
        return (i == 0) & (j == 0), (i == n - 1) & (j == 0), (i == n - 1) & (j == n - 1)
    return when


def _fuse_comm(core, n_in, n_out, comm, when):
    nc = 0 if comm is None else len(comm[1])

    def body(*refs):
        cin, xin = refs[:n_in], refs[n_in:n_in + nc]
        a = n_in + nc
        cout, xout = refs[a:a + n_out], refs[a + n_out:a + n_out + nc]
        rest = refs[a + n_out + nc:]
        if nc == 0:
            core(*cin, *cout, *rest)
            return
        cscr, sems = rest[:-3], rest[-3:]
        first, middle, last = when()
        phase = _COMM_PHASES[comm[0]]

        @pl.when(first)
        def _():
            phase(0, xin, xout, sems)

        core(*cin, *cout, *cscr)

        @pl.when(middle)
        def _():
            phase(1, xin, xout, sems)

        @pl.when(last)
        def _():
            phase(2, xin, xout, sems)

    return body


def _comm_specs(comm):
    if comm is None:
        return [], [], [], []
    hbm = pl.BlockSpec(memory_space=pl.ANY)
    nc = len(comm[1])
    return [hbm] * nc, [hbm] * nc, _COMM_OUT[comm[0]](comm[1]), _comm_sems(comm[0], nc)


def _ah(h):
    return slice(h * AW, (h + 1) * AW)


def _ahd(h):
    return slice(h * AW, h * AW + DH)


def _causal(shape, row0, transposed=False):
    r = row0 + lax.broadcasted_iota(jnp.int32, shape, 0)
    c = lax.broadcasted_iota(jnp.int32, shape, 1)
    return (r <= c) if transposed else (r >= c)


def _flash_fwd(qa, ka, va, z, comm=None):
    S = qa.shape[0]
    t = _tile(S, FLASH_TILE)
    n = S // t
    rch = _tile(t, ROW_CHUNK)

    def core(q_ref, k_ref, v_ref, zb_ref, o_ref, lse_ref, y_ref, m_sc, l_sc, acc):
        i, j = pl.program_id(0), pl.program_id(1)

        @pl.when(j == 0)
        def _():
            m_sc[...] = jnp.full_like(m_sc, MASK_VALUE)
            l_sc[...] = jnp.zeros_like(l_sc)
            acc[...] = jnp.zeros_like(acc)

        def block(masked):
            for h in range(NH):
                for rc in range(t // rch):
                    rows = slice(rc * rch, (rc + 1) * rch)
                    s = _dot_nt(q_ref[rows, _ah(h)], k_ref[:, _ah(h)])
                    if masked:
                        s = jnp.where(_causal(s.shape, rc * rch), s, MASK_VALUE)
                    m_old = m_sc[h, rows, :]
                    m_new = jnp.maximum(m_old, jnp.max(s, axis=-1, keepdims=True))
                    p = jnp.exp(s - m_new)
                    alpha = jnp.exp(m_old - m_new)
                    l_sc[h, rows, :] = alpha * l_sc[h, rows, :] + jnp.sum(p, axis=-1, keepdims=True)
                    acc[rows, _hs(h)] = alpha * acc[rows, _hs(h)] + _dot(p, v_ref[:, _ahd(h)])
                    m_sc[h, rows, :] = m_new

        @pl.when(j < i)
        def _():
            block(False)

        @pl.when(j == i)
        def _():
            block(True)
            lse_ref[...] = jnp.zeros_like(lse_ref)
            for h in range(NH):
                o_ref[:, _hs(h)] = acc[:, _hs(h)] / l_sc[h]
                lse_ref[:, h:h + 1] = m_sc[h] + jnp.log(l_sc[h])
            y_ref[...] = _mx(o_ref[...] * _silu(zb_ref[:, 3 * BW:4 * BW]))

    qspec = lambda w: pl.BlockSpec((t, w), lambda i, j: (i, 0))
    kspec = lambda w: pl.BlockSpec((t, w), lambda i, j: (jnp.minimum(j, i), 0))
    xin, xout, xshape, xsem = _comm_specs(comm)
    res = _call(
        _fuse_comm(core, 4, 3, comm, _square_steps(n)), name="flash_fwd", grid=(n, n),
        in_specs=[qspec(NH * AW), kspec(NH * AW), kspec(NH * AW),
                  pl.BlockSpec((t, 1024), lambda i, j: (i, ZB // 1024))] + xin,
        out_specs=[qspec(BW), qspec(128), qspec(BW)] + xout,
        out_shape=[jax.ShapeDtypeStruct((S, BW), F32), jax.ShapeDtypeStruct((S, 128), F32),
                   jax.ShapeDtypeStruct((S, BW), MXU_DTYPE)] + xshape,
        scratch=[pltpu.VMEM((NH, t, 1), F32), pltpu.VMEM((NH, t, 1), F32), pltpu.VMEM((t, BW), F32)] + xsem,
        sem=("arbitrary", "arbitrary"))(qa, ka, va, z, *(comm[1] if comm else []))
    return res[0], res[1], res[2], list(res[3:])


def _attn_gate_bwd(dy, o, z, qa, cc, lse):
    S = dy.shape[0]
    ts = _tile(S, ROW_TILE)

    def body(dy_ref, o_ref, zb_ref, qa_ref, cc_ref, lse_ref, dg_ref, qb_ref, doa_ref):
        g = zb_ref[:, 3 * BW:4 * BW]
        dy, o = dy_ref[...], o_ref[...]
        do = dy * _silu(g)
        dg_ref[...] = _mx(dy * o * _dsilu(g))
        qb_ref[...] = qa_ref[...]
        doa_ref[...] = jnp.zeros_like(doa_ref)
        shifted = _split3_f32(cc_ref[...] - lse_ref[...])
        for h in range(NH):
            doh = do[:, _hs(h)]
            doa_ref[:, _ahd(h)] = _mx(doh)
            delta = jnp.sum(doh * o[:, _hs(h)], axis=-1, keepdims=True)
            _put_aug(doa_ref, h, 0, parts=_split3_f32(delta))
            _put_aug(qb_ref, h, 0, parts=[pc[:, h:h + 1] for pc in shifted])

    row = lambda w: pl.BlockSpec((ts, w), lambda i: (i, 0))
    return _call(
        body, name="attn_gate_bwd", grid=(S // ts,),
        in_specs=[pl.BlockSpec((ts, BW), lambda i: (i, 1)), row(BW), pl.BlockSpec((ts, 1024), lambda i: (i, ZB // 1024)),
                  row(NH * AW), row(128), row(128)],
        out_specs=[row(BW), row(NH * AW), row(NH * AW)],
        out_shape=[jax.ShapeDtypeStruct((S, BW), MXU_DTYPE), jax.ShapeDtypeStruct((S, NH * AW), MXU_DTYPE),
                   jax.ShapeDtypeStruct((S, NH * AW), MXU_DTYPE)],
        sem=("parallel",))(dy, o, z, qa, cc, lse)


def _aug_value(ref, h, rows=slice(None)):
    base = h * AW + _AUG
    x = ref[rows, base:base + 3].astype(F32)
    return x[:, 0:1] + x[:, 1:2] + x[:, 2:3]


def _flash_bwd_dq(qb, ka, va, doa, comm=None):
    S = qb.shape[0]
    t = _tile(S, FLASH_TILE)
    n = S // t
    rch = _tile(t, ROW_CHUNK)

    def core(q_ref, k_ref, v_ref, do_ref, dq_ref, do2_ref, dr_sc):
        i, j = pl.program_id(0), pl.program_id(1)

        @pl.when(j == 0)
        def _():
            dq_ref[...] = jnp.zeros_like(dq_ref)
            dr_sc[...] = jnp.zeros_like(dr_sc)

        def block(masked):
            for h in range(NH):
                for rc in range(t // rch):
                    rows = slice(rc * rch, (rc + 1) * rch)
                    p = jnp.exp(_dot_nt(q_ref[rows, _ah(h)], k_ref[:, _ah(h)]))
                    if masked:
                        p = jnp.where(_causal(p.shape, rc * rch), p, 0.0)
                    ds = p * _dot_nt(do_ref[rows, _ah(h)], v_ref[:, _ah(h)])
                    dq_ref[rows, _hs(h)] += _dot(ds, k_ref[:, _ahd(h)])
                    dr_sc[h, rows, :] += jnp.sum(ds, axis=-1, keepdims=True)

        @pl.when(j < i)
        def _():
            block(False)

        @pl.when(j == i)
        def _():
            block(True)
            do2_ref[...] = do_ref[...]
            for h in range(NH):
                _put_aug(do2_ref, h, 0, parts=_split3_f32(_aug_value(do_ref, h) + dr_sc[h]))

    qspec = lambda w: pl.BlockSpec((t, w), lambda i, j: (i, 0))
    kspec = lambda w: pl.BlockSpec((t, w), lambda i, j: (jnp.minimum(j, i), 0))
    xin, xout, xshape, xsem = _comm_specs(comm)
    res = _call(
        _fuse_comm(core, 4, 2, comm, _square_steps(n)), name="flash_bwd_dq", grid=(n, n),
        in_specs=[qspec(NH * AW), kspec(NH * AW), kspec(NH * AW), qspec(NH * AW)] + xin,
        out_specs=[qspec(BW), qspec(NH * AW)] + xout,
        out_shape=[jax.ShapeDtypeStruct((S, BW), F32), jax.ShapeDtypeStruct((S, NH * AW), MXU_DTYPE)] + xshape,
        scratch=[pltpu.VMEM((NH, t, 1), F32)] + xsem,
        sem=("arbitrary", "arbitrary"))(qb, ka, va, doa, *(comm[1] if comm else []))
    return res[0], res[1], list(res[2:])


def _flash_bwd_dkv(qb, ka, va, doa):
    S = qb.shape[0]
    t = _tile(S, FLASH_TILE)
    n = S // t

    def body(q_ref, k_ref, v_ref, do_ref, dk_ref, dv_ref, dc_ref):
        j, i = pl.program_id(0), pl.program_id(1)

        @pl.when(i == 0)
        def _():
            dk_ref[...] = jnp.zeros_like(dk_ref)
            dv_ref[...] = jnp.zeros_like(dv_ref)
            dc_ref[...] = jnp.zeros_like(dc_ref)

        def block(masked):
            for h in range(NH):
                pt = jnp.exp(_dot_nt(k_ref[:, _ah(h)], q_ref[:, _ah(h)]))
                if masked:
                    pt = jnp.where(_causal(pt.shape, 0, transposed=True), pt, 0.0)
                dst = pt * _dot_nt(v_ref[:, _ah(h)], do_ref[:, _ah(h)])
                dv_ref[:, _hs(h)] += _dot(pt, do_ref[:, _ahd(h)])
                dk_ref[:, _hs(h)] += _dot(dst, q_ref[:, _ahd(h)])
                dc_ref[:, h:h + 1] += -jnp.sum(dst, axis=-1, keepdims=True)

        @pl.when(i > j)
        def _():
            block(False)

        @pl.when(i == j)
        def _():
            block(True)

    qspec = lambda w: pl.BlockSpec((t, w), lambda j, i: (jnp.maximum(i, j), 0))
    kspec = lambda w: pl.BlockSpec((t, w), lambda j, i: (j, 0))
    return _call(
        body, name="flash_bwd_dkv", grid=(n, n),
        in_specs=[qspec(NH * AW), kspec(NH * AW), kspec(NH * AW), qspec(NH * AW)],
        out_specs=[kspec(BW), kspec(BW), kspec(128)],
        out_shape=[jax.ShapeDtypeStruct((S, BW), F32), jax.ShapeDtypeStruct((S, BW), F32),
                   jax.ShapeDtypeStruct((S, 128), F32)],
        sem=("parallel", "arbitrary"))(qb, ka, va, doa)


def _attn_prep_bwd(z, fb, gq, gk, dq, dk, dv, dgate, dcc, dz):
    S = z.shape[0]
    ts = _tile(S, ROW_TILE)
    nt = S // ts

    def body(zb_ref, zf_ref, fb_ref, gq_ref, gk_ref, dq_ref, dk_ref, dv_ref, dg_ref, dcc_ref, dz_in, dzb_ref, dzf_ref,
             st_ref, carry):
        @pl.when(pl.program_id(0) == 0)
        def _():
            carry[...] = jnp.zeros_like(carry)
            st_ref[...] = jnp.zeros_like(st_ref)

        zb = zb_ref[...]
        gqv, gkv = gq_ref[...], gk_ref[...]
        dqv, dkv = dq_ref[...], dk_ref[...]
        sq = jnp.zeros((1, DH), F32)
        sk = jnp.zeros((1, DH), F32)
        for h in range(NH):
            dx, dgr = _rms_bwd(zb[:, _hs(h)], gqv, dqv[:, _hs(h)] * (DH ** -0.5))
            dzb_ref[:, _hs(h)] = _mx(dx)
            sq = sq + jnp.sum(dgr, axis=0, keepdims=True)
            ks = slice(BW + h * DH, BW + (h + 1) * DH)
            dx, dgr = _rms_bwd(zb[:, ks], gkv, dkv[:, _hs(h)])
            dzb_ref[:, ks] = _mx(dx)
            sk = sk + jnp.sum(dgr, axis=0, keepdims=True)
        dzb_ref[:, 2 * BW:3 * BW] = _mx(dv_ref[...])
        dzb_ref[:, 3 * BW:4 * BW] = dg_ref[...]
        dc = dcc_ref[...]
        dlf = _dot_exact_lhs01(_tri(ts, upper=True), dc) + carry[...]
        carry[...] = dlf[0:1, :]
        dfz = dlf * _sigmoid(-(zf_ref[...] + fb_ref[...]))
        dzf_ref[...] = _mx(dfz)
        st_ref[0:1, 0:DH] += sq
        st_ref[1:2, 0:DH] += sk
        st_ref[2:3, :] += jnp.sum(dfz, axis=0, keepdims=True)

    rev = lambda w, c=0: pl.BlockSpec((ts, w), lambda i: (nt - 1 - i, c))
    one = lambda w: pl.BlockSpec((1, w), lambda i: (0, 0))
    return _call(
        body, name="attn_prep_bwd", grid=(nt,),
        in_specs=[rev(1024, ZB // 1024), rev(128, ZF // 128), one(128), one(DH), one(DH),
                  rev(BW), rev(BW), rev(BW), rev(BW), rev(128), _ANY],
        out_specs=[rev(1024, ZB // 1024), rev(128), pl.BlockSpec((8, 128), lambda i: (0, 0))],
        out_shape=[jax.ShapeDtypeStruct((S, ZCOLS), MXU_DTYPE), jax.ShapeDtypeStruct((S, 128), MXU_DTYPE),
                   jax.ShapeDtypeStruct((8, 128), F32)],
        scratch=[pltpu.VMEM((1, 128), F32)], sem=("arbitrary",), aliases={10: 0})(z, z, fb, gq, gk, dq, dk, dv, dgate, dcc, dz)


def _put_fgate_cols(dzf, dz):
    S = dzf.shape[0]
    ts = _tile(S, ROW_TILE)

    def body(f_ref, dz_in, o_ref):
        o_ref[...] = f_ref[...]

    return _call(
        body, name="put_fgate_cols", grid=(S // ts,),
        in_specs=[pl.BlockSpec((ts, 128), lambda i: (i, 0)), _ANY],
        out_specs=pl.BlockSpec((ts, 128), lambda i: (i, ZF // 128)),
        out_shape=jax.ShapeDtypeStruct((S, ZCOLS), MXU_DTYPE), sem=("parallel",), aliases={1: 0})(dzf, dz)


def _hgrn_consts():
    C = CHUNK
    t = np.arange(C)[:, None]
    j = np.arange(C)[None, :]
    masks = []
    for lev in range(NLEV):
        m = C >> (lev + 1)
        blk, pos = t // (2 * m), t % (2 * m)
        sblk, spos = j // (2 * m), j % (2 * m)
        masks.append((blk == sblk) & (pos >= m) & (spos < m))
    masks.append(t == j)
    return (j <= t).astype(np.float32), np.stack(masks).astype(np.float32)


def _level_exponents(b, lg):
    row = lax.broadcasted_iota(jnp.int32, (CHUNK, 1), 0)
    eqs, eks = [], []
    for lev in range(NLEV):
        m = CHUNK >> (lev + 1)
        pos = jnp.bitwise_and(row, 2 * m - 1)
        if 2 * m >= 8:
            b3 = b.reshape(CHUNK // (2 * m), 2 * m, b.shape[1])
            mid = jnp.broadcast_to(b3[:, m - 1:m, :], b3.shape).reshape(b.shape)
            eqs.append(jnp.where(pos >= m, b - mid, 0.0))
            eks.append(jnp.where(pos < m, mid - b, 0.0))
        elif m == 2:
            eqs.append(jnp.where(pos == 2, lg, jnp.where(pos == 3, lg + pltpu.roll(lg, 1, 0), 0.0)))
            eks.append(jnp.where(pos == 0, pltpu.roll(lg, CHUNK - 1, 0), 0.0))
        else:
            eqs.append(jnp.where(pos == 1, lg, 0.0))
            eks.append(jnp.zeros_like(lg))
    return jnp.concatenate([b] + eqs + eks, axis=0)


def _hgrn_chunk_fwd(zc, lb, tmat):
    cq, cf, ci = zc[:, 0:BW], zc[:, BW:2 * BW], zc[:, 2 * BW:3 * BW]
    q = _silu(cq)
    sg = _sigmoid(cf)
    g = lb + (1.0 - lb) * sg
    lg = jnp.log(g)
    kf = (1.0 - lb) * _sigmoid(-cf)
    b = _dot_exact_lhs01(tmat, lg)
    e = _level_exponents(b, lg)
    blast = b[CHUNK - 1:CHUNK, :]
    return cq, cf, q, sg, g, kf, ci, e, b, blast


def _round_mx(x):
    if MXU_DTYPE != jnp.bfloat16:
        return x
    u = lax.bitcast_convert_type(x, jnp.uint32)
    u = (u + jnp.uint32(0x7FFF) + ((u >> 16) & jnp.uint32(1))) & jnp.uint32(0xFFFF0000)
    return lax.bitcast_convert_type(u, F32)


def _hgrn_scores(q, kf, e, masks, h):
    qh, kh = q[:, _hs(h)], kf[:, _hs(h)]
    ql, kl = [], []
    a = None
    for lev in range(NLEV + 1):
        if lev < NLEV:
            eq = jnp.exp(e[(1 + lev) * CHUNK:(2 + lev) * CHUNK, _hs(h)])
            ek = jnp.exp(e[(1 + NLEV + lev) * CHUNK:(2 + NLEV + lev) * CHUNK, _hs(h)])
            ql.append((_round_mx(qh * eq), eq))
            kl.append((_round_mx(kh * ek), ek))
        else:
            ql.append((_round_mx(qh), None))
            kl.append((_round_mx(kh), None))
        term = masks[lev] * _dot_nt(ql[-1][0], kl[-1][0])
        a = term if a is None else a + term
    return a, ql, kl


def _hgrn_fwd(z, lb, gain, tmat, masks):
    S = z.shape[0]
    ts = _tile(S, ROW_TILE)
    nc = ts // CHUNK

    def body(zc_ref, lb_ref, gn_ref, tm_ref, mk_ref, o_ref, st_ref, y_ref, state):
        @pl.when(pl.program_id(0) == 0)
        def _():
            state[...] = jnp.zeros_like(state)

        lbv, gn, tm, mk = lb_ref[...], gn_ref[...], tm_ref[...], mk_ref[...]

        def chunk(c, carry):
            r0 = pl.multiple_of(c * CHUNK, CHUNK)
            zc = zc_ref[pl.ds(r0, CHUNK), :]
            cq, cf, q, sg, g, kf, v, e, b, blast = _hgrn_chunk_fwd(zc, lbv, tm)
            qe = q * jnp.exp(b)
            kd = kf * jnp.exp(blast - b)
            st_ref[pl.ds(r0, CHUNK), :] = state[...]
            for h in range(NH):
                sth = state[:, _hs(h)]
                a, _, _ = _hgrn_scores(q, kf, e, mk, h)
                oh = _dot_nt(qe[:, _hs(h)], sth) + _dot(a, v[:, _hs(h)])
                state[:, _hs(h)] = sth * jnp.exp(blast[:, _hs(h)]) + _dot_tn(v[:, _hs(h)], kd[:, _hs(h)])
                o_ref[pl.ds(r0, CHUNK), _hs(h)] = oh
                yn = _rms_fwd(oh, gn[:, _hs(h)])
                y_ref[pl.ds(r0, CHUNK), _hs(h)] = _mx(yn * _silu(zc[:, 3 * BW + h * DH:3 * BW + (h + 1) * DH]))
            return carry

        lax.fori_loop(0, nc, chunk, 0, unroll=2)

    row = pl.BlockSpec((ts, BW), lambda i: (i, 0))
    one = pl.BlockSpec((1, BW), lambda i: (0, 0))
    return _call(
        body, name="hgrn_fwd", grid=(S // ts,),
        in_specs=[pl.BlockSpec((ts, 1024), lambda i: (i, ZC // 1024)), one, one,
                  pl.BlockSpec(tmat.shape, lambda i: (0, 0)), pl.BlockSpec(masks.shape, lambda i: (0, 0, 0))],
        out_specs=[row, row, row],
        out_shape=[jax.ShapeDtypeStruct((S, BW), F32), jax.ShapeDtypeStruct((S, BW), F32),
                   jax.ShapeDtypeStruct((S, BW), MXU_DTYPE)],
        scratch=[pltpu.VMEM((CHUNK, BW), F32)], sem=("arbitrary",))(z, lb, gain, tmat, masks)


def _hgrn_bwd(z, lb, gain, tmat, masks, o_pre, states, dy, dz):
    S = z.shape[0]
    ts = _tile(S, ROW_TILE)
    nt = S // ts
    nc = ts // CHUNK

    def body(zc_ref, lb_ref, gn_ref, tm_ref, mk_ref, o_ref, st_ref, dy_ref, dz_in, dz_ref, stat_ref, dstate):
        @pl.when(pl.program_id(0) == 0)
        def _():
            dstate[...] = jnp.zeros_like(dstate)
            stat_ref[...] = jnp.zeros_like(stat_ref)

        lbv, gn, tm, mk = lb_ref[...], gn_ref[...], tm_ref[...], mk_ref[...]
        upper = _tri(CHUNK, upper=True)
        lower_strict = 1.0 - upper

        def chunk(cc, carry):
            c = nc - 1 - cc
            r0 = pl.multiple_of(c * CHUNK, CHUNK)
            zc = zc_ref[pl.ds(r0, CHUNK), :]
            cq, cf, q, sg, g, kf, v, e, b, blast = _hgrn_chunk_fwd(zc, lbv, tm)
            eb = jnp.exp(b)
            ebl = jnp.exp(blast - b)
            qe = q * eb
            kd = kf * ebl
            o = o_ref[pl.ds(r0, CHUNK), :]
            dyv = dy_ref[pl.ds(r0, CHUNK), :]
            stp = st_ref[pl.ds(r0, CHUNK), :]
            cg = zc[:, 3 * BW:4 * BW]
            sgate = _silu(cg)
            dq_parts, dk_parts, dv_parts, dcg_parts = [], [], [], []
            dgain, up_parts, lo_parts, const_parts = [], [], [], []
            for h in range(NH):
                hs = _hs(h)
                oh = o[:, hs]
                r = lax.rsqrt(jnp.mean(oh * oh, axis=-1, keepdims=True) + EPS)
                nrm = oh * r
                dyn = dyv[:, hs] * sgate[:, hs]
                dcg_parts.append(dyv[:, hs] * nrm * gn[:, hs] * _dsilu(cg[:, hs]))
                dgain.append(jnp.sum(dyn * nrm, axis=0, keepdims=True))
                tt = dyn * gn[:, hs]
                doh = r * (tt - nrm * jnp.mean(tt * nrm, axis=-1, keepdims=True))
                a, ql, kl = _hgrn_scores(q, kf, e, mk, h)
                da = _dot_nt(doh, v[:, hs])
                dsth = dstate[:, hs]
                ebh = jnp.exp(blast[:, hs])
                dv_parts.append(_dot_tn(a, doh) + _dot_nt(kd[:, hs], dsth))
                dq_inter = eb[:, hs] * _dot(doh, stp[:, hs])
                dk_state = ebl[:, hs] * _dot(v[:, hs], dsth)
                dqh, dkh, gh = dq_inter, dk_state, None
                for lev in range(NLEV + 1):
                    dal = mk[lev] * da
                    xq = _dot(dal, kl[lev][0])
                    yk = _dot_tn(dal, ql[lev][0])
                    gterm = ql[lev][0] * xq - kl[lev][0] * yk
                    gh = gterm if gh is None else gh + gterm
                    dqh = dqh + (xq if lev == NLEV else ql[lev][1] * xq)
                    dkh = dkh + (yk if lev == NLEV else kl[lev][1] * yk)
                up_parts.append(gh + q[:, hs] * dq_inter)
                lo_parts.append(kf[:, hs] * dk_state)
                const_parts.append(jnp.sum(dsth * stp[:, hs], axis=0, keepdims=True) * ebh)
                dstate[:, hs] = dsth * ebh + _dot_tn(doh, qe[:, hs])
                dq_parts.append(dqh)
                dk_parts.append(dkh)
            dq = jnp.concatenate(dq_parts, axis=1)
            dk = jnp.concatenate(dk_parts, axis=1)
            dlg = (_dot_exact_lhs01(upper, jnp.concatenate(up_parts, axis=1))
                   + _dot_exact_lhs01(lower_strict, jnp.concatenate(lo_parts, axis=1))
                   + jnp.concatenate(const_parts, axis=1))
            dsg = sg * (1.0 - sg)
            dz_ref[pl.ds(r0, CHUNK), 0:BW] = _mx(dq * _dsilu(cq))
            dz_ref[pl.ds(r0, CHUNK), BW:2 * BW] = _mx((dlg / g - dk) * (1.0 - lbv) * dsg)
            dz_ref[pl.ds(r0, CHUNK), 2 * BW:3 * BW] = _mx(jnp.concatenate(dv_parts, axis=1))
            dz_ref[pl.ds(r0, CHUNK), 3 * BW:4 * BW] = _mx(jnp.concatenate(dcg_parts, axis=1))
            stat_ref[0:1, :] += jnp.concatenate(dgain, axis=1)
            stat_ref[1:2, :] += jnp.sum((dlg / g - dk) * (1.0 - sg), axis=0, keepdims=True)
            return carry

        lax.fori_loop(0, nc, chunk, 0, unroll=2)

    rev = lambda w, c=0: pl.BlockSpec((ts, w), lambda i: (nt - 1 - i, c))
    one = pl.BlockSpec((1, BW), lambda i: (0, 0))
    return _call(
        body, name="hgrn_bwd", grid=(nt,),
        in_specs=[rev(1024, ZC // 1024), one, one, pl.BlockSpec(tmat.shape, lambda i: (0, 0)),
                  pl.BlockSpec(masks.shape, lambda i: (0, 0, 0)), rev(BW), rev(BW), rev(BW, 2), _ANY],
        out_specs=[rev(1024, ZC // 1024), pl.BlockSpec((8, BW), lambda i: (0, 0))],
        out_shape=[jax.ShapeDtypeStruct((S, ZCOLS), MXU_DTYPE), jax.ShapeDtypeStruct((8, BW), F32)],
        scratch=[pltpu.VMEM((CHUNK, BW), F32)],
        sem=("arbitrary",), aliases={8: 0})(z, lb, gain, tmat, masks, o_pre, states, dy, dz)


def _lower_bounds_fwd(lb_logits):
    def body(l_ref, o_ref):
        l = l_ref[...]
        m = jnp.max(l, axis=0, keepdims=True)
        ex = jnp.exp(l - m)
        p = ex / jnp.sum(ex, axis=0, keepdims=True)
        cs = p[0:1, :]
        o_ref[0:1, :] = jnp.clip(cs - p[0:1, :], 0.0, 1.0)
        for d in range(1, DEPTH):
            cs = cs + p[d:d + 1, :]
            o_ref[d:d + 1, :] = jnp.clip(cs - p[0:1, :], 0.0, 1.0)

    full = pl.BlockSpec((DEPTH, BW), lambda: (0, 0))
    return _call(body, name="lower_bounds_fwd", grid=(), in_specs=[full], out_specs=full,
                 out_shape=jax.ShapeDtypeStruct((DEPTH, BW), F32))(lb_logits)


def _lower_bounds_bwd(lb_logits, dlow):
    def body(l_ref, d_ref, o_ref):
        l = l_ref[...]
        m = jnp.max(l, axis=0, keepdims=True)
        ex = jnp.exp(l - m)
        p = ex / jnp.sum(ex, axis=0, keepdims=True)
        dl = d_ref[...]
        cs = p[0:1, :]
        dcs = []
        for d in range(DEPTH):
            if d > 0:
                cs = cs + p[d:d + 1, :]
            val = cs - p[0:1, :]
            dcs.append(jnp.where((val > 0.0) & (val < 1.0), dl[d:d + 1, :], 0.0))
        total = dcs[0] + dcs[1] + dcs[2] + dcs[3]
        dp = []
        for j in range(DEPTH):
            s = dcs[j]
            for d in range(j + 1, DEPTH):
                s = s + dcs[d]
            dp.append(s - total if j == 0 else s)
        inner = p[0:1, :] * dp[0]
        for j in range(1, DEPTH):
            inner = inner + p[j:j + 1, :] * dp[j]
        for j in range(DEPTH):
            o_ref[j:j + 1, :] = p[j:j + 1, :] * (dp[j] - inner)

    full = pl.BlockSpec((DEPTH, BW), lambda: (0, 0))
    return _call(body, name="lower_bounds_bwd", grid=(), in_specs=[full, full], out_specs=full,
                 out_shape=jax.ShapeDtypeStruct((DEPTH, BW), F32))(lb_logits, dlow)


def _sgu_fwd(z, gv, ws, bs):
    S = z.shape[0]
    ts = _tile(S, ROW_TILE)
    nc = ts // SCHUNK

    def body(zd_ref, gv_ref, ws_ref, bs_ref, y_ref):
        gvv, bsv = gv_ref[...], bs_ref[...]
        tril = _tri(SCHUNK)
        for c in range(nc):
            rs = slice(c * SCHUNK, (c + 1) * SCHUNK)
            zd = zd_ref[rs, :]
            for h in range(NH):
                vn = _rms_fwd(zd[:, BW + h * DH:BW + (h + 1) * DH], gvv[:, _hs(h)])
                s = _dot(ws_ref[h] * tril, vn) + bsv[:, h:h + 1]
                y_ref[rs, _hs(h)] = _mx(zd[:, _hs(h)] * s * _silu(zd[:, 2 * BW + h * DH:2 * BW + (h + 1) * DH]))

    return _call(
        body, name="sgu_fwd", grid=(S // ts,),
        in_specs=[pl.BlockSpec((ts, 1024), lambda i: (i, ZD // 1024)), pl.BlockSpec((1, BW), lambda i: (0, 0)),
                  pl.BlockSpec((NH, SCHUNK, SCHUNK), lambda i: (0, 0, 0)), pl.BlockSpec((SCHUNK, 128), lambda i: (0, 0))],
        out_specs=pl.BlockSpec((ts, BW), lambda i: (i, 0)),
        out_shape=jax.ShapeDtypeStruct((S, BW), MXU_DTYPE), sem=("parallel",))(z, gv, ws, bs)


def _sgu_bwd(z, gv, ws, bs, dy, dz):
    S = z.shape[0]
    ts = _tile(S, ROW_TILE)
    nc = ts // SCHUNK

    def body(zd_ref, gv_ref, ws_ref, bs_ref, dy_ref, dz_in, dz_ref, dws_ref, dbs_ref, st_ref):
        @pl.when(pl.program_id(0) == 0)
        def _():
            dws_ref[...] = jnp.zeros_like(dws_ref)
            dbs_ref[...] = jnp.zeros_like(dbs_ref)
            st_ref[...] = jnp.zeros_like(st_ref)

        gvv, bsv = gv_ref[...], bs_ref[...]
        tril = _tri(SCHUNK)
        dz_ref[:, 3 * BW:4 * BW] = jnp.zeros((ts, BW), MXU_DTYPE)
        for c in range(nc):
            rs = slice(c * SCHUNK, (c + 1) * SCHUNK)
            zd = zd_ref[rs, :]
            dyv = dy_ref[rs, :]
            for h in range(NH):
                hs = _hs(h)
                u = zd[:, hs]
                vraw = zd[:, BW + h * DH:BW + (h + 1) * DH]
                gt = zd[:, 2 * BW + h * DH:2 * BW + (h + 1) * DH]
                gvh = gvv[:, hs]
                vn = _rms_fwd(vraw, gvh)
                wm = ws_ref[h] * tril
                s = _dot(wm, vn) + bsv[:, h:h + 1]
                sil = _silu(gt)
                d = dyv[:, hs]
                ds = d * u * sil
                dz_ref[rs, hs] = _mx(d * s * sil)
                dz_ref[rs, 2 * BW + h * DH:2 * BW + (h + 1) * DH] = _mx(d * u * s * _dsilu(gt))
                dws_ref[h] += tril * _dot_nt(ds, vn)
                dbs_ref[:, h:h + 1] += jnp.sum(ds, axis=-1, keepdims=True)
                dvn = _dot_tn(wm, ds)
                dx, dgr = _rms_bwd(vraw, gvh, dvn)
                dz_ref[rs, BW + h * DH:BW + (h + 1) * DH] = _mx(dx)
                st_ref[0:1, hs] += jnp.sum(dgr, axis=0, keepdims=True)

    return _call(
        body, name="sgu_bwd", grid=(S // ts,),
        in_specs=[pl.BlockSpec((ts, 1024), lambda i: (i, ZD // 1024)), pl.BlockSpec((1, BW), lambda i: (0, 0)),
                  pl.BlockSpec((NH, SCHUNK, SCHUNK), lambda i: (0, 0, 0)), pl.BlockSpec((SCHUNK, 128), lambda i: (0, 0)),
                  pl.BlockSpec((ts, BW), lambda i: (i, 3)), _ANY],
        out_specs=[pl.BlockSpec((ts, 1024), lambda i: (i, ZD // 1024)), pl.BlockSpec((NH, SCHUNK, SCHUNK), lambda i: (0, 0, 0)),
                   pl.BlockSpec((SCHUNK, 128), lambda i: (0, 0)), pl.BlockSpec((8, BW), lambda i: (0, 0))],
        out_shape=[jax.ShapeDtypeStruct((S, ZCOLS), MXU_DTYPE), jax.ShapeDtypeStruct((NH, SCHUNK, SCHUNK), F32),
                   jax.ShapeDtypeStruct((SCHUNK, 128), F32), jax.ShapeDtypeStruct((8, BW), F32)],
        sem=("arbitrary",), aliases={5: 0})(z, gv, ws, bs, dy, dz)


def _merge_fwd(x, z, ys, wup, mb, wo):
    S = x.shape[0]
    ts = _tile(S, MERGE_TILE)

    def body(x_ref, zm_ref, ya_ref, yb_ref, yc_ref, yd_ref, wup_ref, mb_ref, wo_ref, x1_ref, mg_ref):
        yrefs = (ya_ref, yb_ref, yc_ref, yd_ref)
        mbv = mb_ref[...]
        merged = None
        for b in range(NBR):
            cs = slice(b * D_MODEL, (b + 1) * D_MODEL)
            term = _sigmoid(zm_ref[:, cs] + mbv[b:b + 1, :]) * jnp.dot(yrefs[b][...], wup_ref[b],
                                                                      preferred_element_type=F32)
            merged = term if merged is None else merged + term
        mg = _mx(merged)
        mg_ref[...] = mg
        x1_ref[...] = x_ref[...] + jnp.dot(mg, wo_ref[...], preferred_element_type=F32)

    row = lambda w: pl.BlockSpec((ts, w), lambda i: (i, 0))
    return _call(
        body, name="merge_fwd", grid=(S // ts,),
        in_specs=[row(D_MODEL), pl.BlockSpec((ts, 4096), lambda i: (i, 0)), row(BW), row(BW), row(BW), row(BW),
                  pl.BlockSpec((NBR, BW, D_MODEL), lambda i: (0, 0, 0)), pl.BlockSpec((NBR, D_MODEL), lambda i: (0, 0)),
                  pl.BlockSpec((D_MODEL, D_MODEL), lambda i: (0, 0))],
        out_specs=[row(D_MODEL), row(D_MODEL)],
        out_shape=[jax.ShapeDtypeStruct((S, D_MODEL), F32), jax.ShapeDtypeStruct((S, D_MODEL), MXU_DTYPE)],
        sem=("parallel",))(x, z, *ys, wup, mb, wo)


def _merge_bwd(dx1, z, ys, wup, mb, wo):
    S = dx1.shape[0]
    ts = _tile(S, MERGE_TILE)

    def body(dx_ref, zm_ref, ya_ref, yb_ref, yc_ref, yd_ref, wup_ref, mb_ref, wo_ref,
             dzm_ref, du_ref, dxb_ref, dy_ref, st_ref):
        @pl.when(pl.program_id(0) == 0)
        def _():
            st_ref[...] = jnp.zeros_like(st_ref)

        yrefs = (ya_ref, yb_ref, yc_ref, yd_ref)
        mbv = mb_ref[...]
        dxb = _mx(dx_ref[...])
        dxb_ref[...] = dxb
        dmerged = _dot_nt(dxb, wo_ref[...])
        for b in range(NBR):
            cs = slice(b * D_MODEL, (b + 1) * D_MODEL)
            u = jnp.dot(yrefs[b][...], wup_ref[b], preferred_element_type=F32)
            sg = _sigmoid(zm_ref[:, cs] + mbv[b:b + 1, :])
            du = _mx(dmerged * sg)
            du_ref[:, cs] = du
            dzm = dmerged * u * sg * (1.0 - sg)
            dzm_ref[:, cs] = _mx(dzm)
            st_ref[b:b + 1, :] += jnp.sum(dzm, axis=0, keepdims=True)
            dy_ref[:, b * BW:(b + 1) * BW] = _dot_nt(du, wup_ref[b])

    row = lambda w: pl.BlockSpec((ts, w), lambda i: (i, 0))
    return _call(
        body, name="merge_bwd", grid=(S // ts,),
        in_specs=[row(D_MODEL), pl.BlockSpec((ts, 4096), lambda i: (i, 0)), row(BW), row(BW), row(BW), row(BW),
                  pl.BlockSpec((NBR, BW, D_MODEL), lambda i: (0, 0, 0)),
                  pl.BlockSpec((NBR, D_MODEL), lambda i: (0, 0)), pl.BlockSpec((D_MODEL, D_MODEL), lambda i: (0, 0))],
        out_specs=[row(4096), row(4096), row(D_MODEL), row(D_MODEL), pl.BlockSpec((8, D_MODEL), lambda i: (0, 0))],
        out_shape=[jax.ShapeDtypeStruct((S, ZCOLS), MXU_DTYPE), jax.ShapeDtypeStruct((S, 4096), MXU_DTYPE),
                   jax.ShapeDtypeStruct((S, D_MODEL), MXU_DTYPE), jax.ShapeDtypeStruct((S, D_MODEL), F32),
                   jax.ShapeDtypeStruct((8, D_MODEL), F32)],
        sem=("arbitrary",))(dx1, z, *ys, wup, mb, wo)


def _ple_fwd(x1, p, g, wg, wp):
    S = x1.shape[0]
    ts = _tile(S, ROW_TILE)

    def body(x_ref, p_ref, g_ref, wg_ref, wp_ref, o_ref):
        x = x_ref[...]
        hp = _mx(_rms_fwd(x, g_ref[...]))
        gate = _sigmoid(jnp.dot(hp, wg_ref[...], preferred_element_type=F32))
        pp = jnp.dot(_mx(p_ref[...]), wp_ref[...], preferred_element_type=F32)
        o_ref[...] = x + gate * pp

    row = lambda w: pl.BlockSpec((ts, w), lambda i: (i, 0))
    return _call(
        body, name="ple_fwd", grid=(S // ts,),
        in_specs=[row(D_MODEL), row(PLE), pl.BlockSpec((1, D_MODEL), lambda i: (0, 0)),
                  pl.BlockSpec((D_MODEL, D_MODEL), lambda i: (0, 0)), pl.BlockSpec((PLE, D_MODEL), lambda i: (0, 0))],
        out_specs=row(D_MODEL), out_shape=jax.ShapeDtypeStruct((S, D_MODEL), F32),
        sem=("parallel",))(x1, p, g, wg, wp)


def _ple_bwd(x1, p, dx2, g, wg, wp):
    S = x1.shape[0]
    ts = _tile(S, ROW_TILE)

    def body(x_ref, p_ref, dx_ref, g_ref, wg_ref, wp_ref, dx1_ref, hp_ref, dgl_ref, dpp_ref, pb_ref, st_ref):
        @pl.when(pl.program_id(0) == 0)
        def _():
            st_ref[...] = jnp.zeros_like(st_ref)

        x, gv, dx2 = x_ref[...], g_ref[...], dx_ref[...]
        hp = _mx(_rms_fwd(x, gv))
        hp_ref[...] = hp
        gate = _sigmoid(jnp.dot(hp, wg_ref[...], preferred_element_type=F32))
        pb = _mx(p_ref[...])
        pb_ref[...] = pb
        pp = jnp.dot(pb, wp_ref[...], preferred_element_type=F32)
        dgl = _mx(dx2 * pp * gate * (1.0 - gate))
        dgl_ref[...] = dgl
        dpp_ref[...] = _mx(dx2 * gate)
        dhp = _dot_nt(dgl, wg_ref[...])
        dxn, dgr = _rms_bwd(x, gv, dhp)
        dx1_ref[...] = dx2 + dxn
        st_ref[0:1, :] += jnp.sum(dgr, axis=0, keepdims=True)

    row = lambda w: pl.BlockSpec((ts, w), lambda i: (i, 0))
    sq = pl.BlockSpec((D_MODEL, D_MODEL), lambda i: (0, 0))
    return _call(
        body, name="ple_bwd", grid=(S // ts,),
        in_specs=[row(D_MODEL), row(PLE), row(D_MODEL), pl.BlockSpec((1, D_MODEL), lambda i: (0, 0)), sq,
                  pl.BlockSpec((PLE, D_MODEL), lambda i: (0, 0))],
        out_specs=[row(D_MODEL), row(D_MODEL), row(D_MODEL), row(D_MODEL), row(PLE),
                   pl.BlockSpec((8, D_MODEL), lambda i: (0, 0))],
        out_shape=[jax.ShapeDtypeStruct((S, D_MODEL), F32)] + [jax.ShapeDtypeStruct((S, D_MODEL), MXU_DTYPE)] * 3
        + [jax.ShapeDtypeStruct((S, PLE), MXU_DTYPE), jax.ShapeDtypeStruct((8, D_MODEL), F32)],
        sem=("arbitrary",))(x1, p, dx2, g, wg, wp)


def _pad_rows(a, rows=8):
    return jnp.concatenate([a, jnp.zeros((rows - a.shape[0],) + a.shape[1:], a.dtype)], axis=0)


def _pad_lanes(a, lanes=128):
    return jnp.concatenate([a, jnp.zeros(a.shape[:-1] + (lanes - a.shape[-1],), a.dtype)], axis=-1)


def _wz_from_w_in(w):
    zeros = lambda n: jnp.zeros((w.shape[0], n), w.dtype)
    return jnp.concatenate([w[:, _OM:_OEND], w[:, _OA:_OB], w[:, _OB:_OF], w[:, _OC:_OD], w[:, _OD:_OM], zeros(256),
                            w[:, _OF:_OC], zeros(124)], axis=1)


def _w_in_from_wz(g):
    return jnp.concatenate([g[:, ZA:ZB], g[:, ZB:ZC], g[:, ZF:ZF + 4], g[:, ZC:ZD], g[:, ZD:ZD + 768], g[:, ZM:ZA]], axis=1)


_W_IN_GROUPS = [(_OA, _OB, ZA), (_OB, _OF, ZB), (_OF, _OC, ZF), (_OC, _OD, ZC), (_OD, _OM, ZD), (_OM, _OEND, ZM)]


def _wz_from_shards(g):
    n = g.shape[-1]
    pieces, pos = [], 0
    for a, b, zs in sorted(_W_IN_GROUPS, key=lambda grp: grp[2]):
        if zs > pos:
            pieces.append(jnp.zeros((g.shape[1], zs - pos), g.dtype))
        for k in range(4):
            lo, hi = max(a, k * n), min(b, (k + 1) * n)
            if lo < hi:
                pieces.append(g[k][:, lo - k * n:hi - k * n])
        pos = zs + (b - a)
    pieces.append(jnp.zeros((g.shape[1], ZCOLS - pos), g.dtype))
    return jnp.concatenate(pieces, axis=1)


def _w_in_slabs_from_wz(g):
    n = _OEND // 4
    slabs = []
    for k in range(4):
        pieces = []
        for a, b, zs in _W_IN_GROUPS:
            lo, hi = max(a, k * n), min(b, (k + 1) * n)
            if lo < hi:
                pieces.append(g[:, zs + lo - a:zs + hi - a])
        slabs.append(jnp.concatenate(pieces, axis=1))
    return jnp.stack(slabs)


def _local_step(x, p, target, wts, dist=None):
    tmat_np, masks_np = _hgrn_consts()
    tmat = jnp.asarray(tmat_np, MXU_DTYPE)
    masks = jnp.asarray(masks_np, F32)
    lower = _lower_bounds_fwd(wts["lb_logits"])
    saved = []
    for li in range(DEPTH):
        big = dist.weights(li) if dist else {n: wts[n][li] for n in _GATHERED}
        wz = _wz_from_shards(big["w_in"]) if big["w_in"].ndim == 3 else _wz_from_w_in(big["w_in"])
        g_mix = wts["norm_mix"][li][None, :]
        h = _norm_fwd(x, g_mix)
        z = _mm_nn(h, wz, F32, name="mm_z", tn=1664)
        cw = _pad_rows(jnp.concatenate([big["conv_w"], wts["conv_b"][li][None, :]], axis=0))
        ya = _conv_fwd(z, cw)
        fb = _pad_lanes(wts["fgate_bias"][li][None, :])
        gq, gk = wts["q_norm"][li][None, :], wts["k_norm"][li][None, :]
        qa, ka, va, cc = _attn_prep_fwd(z, fb, gq, gk)
        o, lse, yb, got = _flash_fwd(qa, ka, va, z, comm=dist.fwd_comm(li) if dist else None)
        if dist:
            dist.fwd_done(li, got)
        lb = lower[li][None, :]
        gh = wts["hgrn_norm"][li][None, :]
        o_pre, states, yc = _hgrn_fwd(z, lb, gh, tmat, masks)
        gv = wts["sgu_norm"][li][None, :]
        ws = wts["spatial_w"][li]
        bs = _pad_lanes(wts["spatial_b"][li].T)
        yd = _sgu_fwd(z, gv, ws, bs)
        ys = (ya, yb, yc, yd)
        x1, merged = _merge_fwd(x, z, ys, big["w_up"], big["merge_b"], big["w_o"])
        g_ple = wts["norm_ple"][li][None, :]
        x2 = _ple_fwd(x1, p[li], g_ple, big["w_ple_gate"], big["w_ple_proj"])
        saved.append(dict(x=x, h=h, z=z, wz=wz, cw=cw, fb=fb, gq=gq, gk=gk, qa=qa, ka=ka, va=va, cc=cc, o=o, lse=lse,
                          lb=lb, gh=gh, o_pre=o_pre, states=states, gv=gv, ws=ws, bs=bs, ys=ys, x1=x1, merged=merged,
                          g_mix=g_mix, g_ple=g_ple, big=big))
        x = x2

    loss, dx = _loss_fwd_bwd(x, target)

    names = ["norm_mix", "w_in", "conv_w", "conv_b", "fgate_bias", "q_norm", "k_norm", "lb", "hgrn_norm", "sgu_norm",
             "spatial_w", "spatial_b", "w_up", "merge_b", "w_o", "norm_ple", "w_ple_gate", "w_ple_proj"]
    gl = {n: [None] * DEPTH for n in names}
    for li in reversed(range(DEPTH)):
        s = saved[li]
        z, big = s["z"], s["big"]
        wg, wp = big["w_ple_gate"], big["w_ple_proj"]
        dx1, hp, dgl, dpp, pb, st = _ple_bwd(s["x1"], p[li], dx, s["g_ple"], wg, wp)
        gl["norm_ple"][li] = st[0]
        gl["w_ple_gate"][li] = _mm_tn(hp, dgl, name="mm_dwg")
        gl["w_ple_proj"][li] = _mm_tn(pb, dpp, name="mm_dwp")
        wup, wo = big["w_up"], big["w_o"]
        dz, du, dxb, dy, st = _merge_bwd(dx1, z, s["ys"], wup, big["merge_b"], wo)
        gl["merge_b"][li] = st[0:NBR]
        gl["w_o"][li] = _mm_tn(s["merged"], dxb, name="mm_dwo")
        gl["w_up"][li] = jnp.stack([_mm_tn(s["ys"][b], du, name="mm_dwup", ycol=b, n=D_MODEL) for b in range(NBR)])
        dz, st = _conv_bwd(z, s["cw"], dy, dz)
        gl["conv_w"][li] = st[0:3]
        gl["conv_b"][li] = st[3]
        dgate, qb, doa = _attn_gate_bwd(dy, s["o"], z, s["qa"], s["cc"], s["lse"])
        dq, doa, got = _flash_bwd_dq(qb, s["ka"], s["va"], doa, comm=dist.bwd_comm() if dist else None)
        if dist:
            dist.bwd_done(got)
        dk, dv, dcc = _flash_bwd_dkv(qb, s["ka"], s["va"], doa)
        dz, dzf, st = _attn_prep_bwd(z, s["fb"], s["gq"], s["gk"], dq, dk, dv, dgate, dcc, dz)
        dz = _put_fgate_cols(dzf, dz)
        gl["q_norm"][li] = st[0, 0:DH]
        gl["k_norm"][li] = st[1, 0:DH]
        gl["fgate_bias"][li] = st[2, 0:NH]
        dz, st = _hgrn_bwd(z, s["lb"], s["gh"], tmat, masks, s["o_pre"], s["states"], dy, dz)
        gl["hgrn_norm"][li] = st[0]
        gl["lb"][li] = st[1]
        dz, dws, dbs, st = _sgu_bwd(z, s["gv"], s["ws"], s["bs"], dy, dz)
        gl["sgu_norm"][li] = st[0]
        gl["spatial_w"][li] = dws
        gl["spatial_b"][li] = dbs[:, 0:NH].T
        dwz = _mm_tn(s["h"], dz, name="mm_dwz", tn=1664)
        gl["w_in"][li] = dwz if dist else _w_in_from_wz(dwz)
        if dist:
            dist.push(li, {n: gl[n][li] for n in _BIG})
        if dist and li == 0:
            dh, got = _mm_nn(dz, s["wz"], F32, name="mm_dh", ts=ROW_TILE // 2, b_transposed=True, comm=dist.bwd_comm())
            dist.bwd_done(got)
        else:
            dh = _mm_nn(dz, s["wz"], F32, name="mm_dh", ts=ROW_TILE // 2, b_transposed=True)
        dx, st = _norm_bwd(s["x"], s["g_mix"], dh, dx1)
        gl["norm_mix"][li] = st[0]

    if dist:
        dist.finish()
    grads = {n: jnp.stack(v) for n, v in gl.items() if not (dist and n in _BIG)}
    grads["lb_logits"] = _lower_bounds_bwd(wts["lb_logits"], grads.pop("lb"))
    return loss, dx, grads


def _my_pos():
    return lax.axis_index("x"), lax.axis_index("y"), lax.axis_index("c")


def _gather_phase(phase, ins, outs, sems):
    if phase == 1:
        return
    send, recv, lsem = sems
    x, y, c = _my_pos()
    me = 2 * x + y
    peers = [(1 - x, y), (x, 1 - y), (1 - x, 1 - y)]
    copies = []
    for t in range(len(ins)):
        copies.append(pltpu.make_async_copy(ins[t], outs[t].at[me], lsem.at[t]))
        for j, (px, py) in enumerate(peers):
            copies.append(pltpu.make_async_remote_copy(
                src_ref=ins[t], dst_ref=outs[t].at[me], send_sem=send.at[t, j], recv_sem=recv.at[t, j],
                device_id=(px, py, c), device_id_type=MESH))
    for cp in copies:
        if phase == 0:
            cp.start()
        else:
            cp.wait()


def _exchange_phase(phase, ins, outs, sems):
    send, recv, lsem = sems
    x, y, c = _my_pos()
    me = 2 * x + y
    sib = (x, y, 1 - c)
    chips = [(1 - x, y), (x, 1 - y), (1 - x, 1 - y)]

    def rc(t, k, src, dst, dev):
        return pltpu.make_async_remote_copy(src_ref=src, dst_ref=dst, send_sem=send.at[t, k], recv_sem=recv.at[t, k],
                                            device_id=dev, device_id_type=MESH)

    for t in range(len(ins)):
        local = pltpu.make_async_copy(ins[t].at[me], outs[t].at[2 * me + c], lsem.at[t])
        first = [rc(t, 0, ins[t].at[me], outs[t].at[2 * me + c], sib)]
        first += [rc(t, 1 + j, ins[t].at[2 * px + py], outs[t].at[2 * me + c], (px, py, c))
                  for j, (px, py) in enumerate(chips)]
        landed = [outs[t].at[2 * (2 * px + py) + c] for px, py in chips]
        passed = [rc(t, 4 + j, slot, slot, sib) for j, slot in enumerate(landed)]
        if phase == 0:
            local.start()
            for cp in first:
                cp.start()
        elif phase == 1:
            for j, slot in enumerate(landed):
                rc(t, 1 + j, slot, slot, (x, y, c)).wait_recv()
                passed[j].start()
        else:
            s0 = outs[t].at[2 * me + (1 - c)]
            rc(t, 0, s0, s0, (x, y, c)).wait_recv()
            for j, (px, py) in enumerate(chips):
                slot = outs[t].at[2 * (2 * px + py) + (1 - c)]
                rc(t, 4 + j, slot, slot, (x, y, c)).wait_recv()
            for cp in first + passed:
                cp.wait_send()
            local.wait()


_COMM_PHASES = {"gather": _gather_phase, "exchange": _exchange_phase}
_COMM_OUT = {"gather": lambda arrays: [jax.ShapeDtypeStruct((4,) + a.shape, a.dtype) for a in arrays],
             "exchange": lambda arrays: [jax.ShapeDtypeStruct((8,) + a.shape[1:], a.dtype) for a in arrays]}


def _comm_sems(kind, nc):
    k = 3 if kind == "gather" else 7
    return [pltpu.SemaphoreType.DMA((nc, k)), pltpu.SemaphoreType.DMA((nc, k)), pltpu.SemaphoreType.DMA((nc,))]


def _comm_alone(kind, arrays, name):
    nc = len(arrays)

    def body(*refs):
        for phase in range(3):
            _COMM_PHASES[kind](phase, refs[:nc], refs[nc:2 * nc], refs[2 * nc:])

    hbm = pl.BlockSpec(memory_space=pl.ANY)
    return pl.pallas_call(
        functools.partial(body), name=name, in_specs=[hbm] * nc, out_specs=[hbm] * nc,
        out_shape=_COMM_OUT[kind](arrays), scratch_shapes=_comm_sems(kind, nc))(*arrays)


def _allreduce_small(v):
    R = v.shape[0]

    def body(v_ref, o_ref, pair, chips, send, recv):
        x, y, c = _my_pos()
        me = 2 * x + y
        pair[c] = v_ref[...]
        swap = pltpu.make_async_remote_copy(src_ref=v_ref, dst_ref=pair.at[c], send_sem=send.at[0], recv_sem=recv.at[0],
                                            device_id=(x, y, 1 - c), device_id_type=MESH)
        swap.start()
        swap.wait()
        chips[me] = pair[0] + pair[1]
        copies = [pltpu.make_async_remote_copy(src_ref=chips.at[me], dst_ref=chips.at[me], send_sem=send.at[1 + j],
                                               recv_sem=recv.at[1 + j], device_id=(px, py, c), device_id_type=MESH)
                  for j, (px, py) in enumerate([(1 - x, y), (x, 1 - y), (1 - x, 1 - y)])]
        for cp in copies:
            cp.start()
        for cp in copies:
            cp.wait()
        o_ref[...] = (chips[0] + chips[1]) + (chips[2] + chips[3])

    vm = pl.BlockSpec(memory_space=pltpu.VMEM)
    return pl.pallas_call(
        functools.partial(body), name="allreduce_small", in_specs=[vm], out_specs=vm,
        out_shape=jax.ShapeDtypeStruct((R, 128), F32),
        scratch_shapes=[pltpu.VMEM((2, R, 128), F32), pltpu.VMEM((4, R, 128), F32),
                        pltpu.SemaphoreType.DMA((4,)), pltpu.SemaphoreType.DMA((4,))],
        compiler_params=pltpu.CompilerParams(vmem_limit_bytes=VMEM_LIMIT))(v)


def _adamw(w, m, v, parts, nparts):
    A, R, C = w.shape
    per_layer = isinstance(parts, (list, tuple))
    tr = _tile(R, 64) if per_layer else (R if R <= 128 else _tile(R, 128))
    nparr = len(parts) if per_layer else 1

    def body(*refs):
        w_ref, m_ref, v_ref = refs[:3]
        p_refs = refs[3:3 + nparr]
        g_ref, d_ref, nm_ref, nv_ref = refs[3 + nparr:]

        def update(read):
            g = read(0).astype(F32)
            for k in range(1, nparts):
                g = g + read(k).astype(F32)
            mm = ADAM_B1 * m_ref[0] + (1.0 - ADAM_B1) * g
            vv = ADAM_B2 * v_ref[0] + (1.0 - ADAM_B2) * jnp.square(g)
            m_hat = mm / (1.0 - ADAM_B1 ** ADAM_STEP)
            v_hat = vv / (1.0 - ADAM_B2 ** ADAM_STEP)
            g_ref[0] = g
            d_ref[0] = -ADAM_LR * (m_hat / (jnp.sqrt(v_hat) + ADAM_EPS) + ADAM_WD * w_ref[0])
            nm_ref[0] = mm
            nv_ref[0] = vv

        if not per_layer:
            update(lambda k: p_refs[0][k, 0])
        else:
            for a in range(A):
                @pl.when(pl.program_id(0) == a)
                def _(a=a):
                    update(lambda k: p_refs[a][k])

    blk = pl.BlockSpec((1, tr, C), lambda a, r: (a, r, 0))
    if per_layer:
        pspecs = [pl.BlockSpec((nparts, tr, C), functools.partial(lambda a, r, l: (0, jnp.where(a == l, r, 0), 0), l=l))
                  for l in range(A)]
        pargs = list(parts)
    else:
        pspecs = [pl.BlockSpec((nparts, 1, tr, C), lambda a, r: (0, a, r, 0))]
        pargs = [parts]
    return _call(
        body, name="adamw", grid=(A, R // tr), in_specs=[blk, blk, blk] + pspecs,
        out_specs=[blk] * 4, out_shape=[jax.ShapeDtypeStruct((A, R, C), F32)] * 4,
        sem=("arbitrary", "arbitrary"))(w, m, v, *pargs)


def _rows128(a):
    width = a.shape[-1]
    a2 = a.reshape(-1, width)
    k = -(-width // 128)
    if width % 128:
        a2 = jnp.pad(a2, ((0, 0), (0, k * 128 - width)))
    return a2.reshape(-1, 128)


def _from_rows128(r, shape):
    width = shape[-1]
    k = -(-width // 128)
    return r.reshape(-1, k * 128)[:, :width].reshape(shape)


def _as3d(a):
    if a.ndim == 2:
        return a[None]
    if a.ndim == 3:
        return a
    return a.reshape((-1,) + a.shape[-2:])


_WEIGHTS = ["norm_mix", "w_in", "conv_w", "conv_b", "fgate_bias", "q_norm", "k_norm", "lb_logits", "hgrn_norm", "sgu_norm",
            "spatial_w", "spatial_b", "w_up", "merge_b", "w_o", "norm_ple", "w_ple_gate", "w_ple_proj"]
_BIG = ["w_in", "w_up", "w_o", "w_ple_gate", "w_ple_proj"]
_GATHERED = _BIG + ["conv_w", "merge_b"]
_SHARD_AXIS = {"w_in": 2, "w_up": 3, "w_o": 1, "w_ple_gate": 1, "w_ple_proj": 2, "conv_w": 2, "merge_b": 2}
_SMALL = [n for n in _WEIGHTS if n not in _BIG]


class _Dist:
    def __init__(self, w):
        self.w = w
        self.full = {0: self._unpack(_comm_alone("gather", self._shards(0), "gather_weights"))}
        self.contrib = {}
        self.pending = None

    def _shards(self, l):
        return [_mx(self.w[n][l]) for n in _BIG] + [self.w["conv_w"][l], self.w["merge_b"][l]]

    def _unpack(self, gathered):
        return {n: g if n == "w_in" else jnp.concatenate([g[k] for k in range(4)], axis=_SHARD_AXIS[n] - 1)
                for n, g in zip(_GATHERED, gathered)}

    def weights(self, l):
        return self.full[l]

    def fwd_comm(self, l):
        return ("gather", self._shards(l + 1)) if l + 1 < DEPTH else None

    def fwd_done(self, l, got):
        if got:
            self.full[l + 1] = self._unpack(got)

    def push(self, l, grads):
        self.pending = (l, [(_w_in_slabs_from_wz(grads[n]) if n == "w_in" else
                             jnp.stack(jnp.split(grads[n], 4, axis=_SHARD_AXIS[n] - 1))).astype(GRAD_WIRE_DTYPE)
                            for n in _BIG])

    def bwd_comm(self):
        return ("exchange", self.pending[1]) if self.pending else None

    def bwd_done(self, got):
        if got:
            self.contrib[self.pending[0]] = got
            self.pending = None

    def finish(self):
        if self.pending:
            self.contrib[self.pending[0]] = _comm_alone("exchange", self.pending[1], "exchange_grads")
            self.pending = None


def kernel(x, p, norm_mix, w_in, conv_w, conv_b, fgate_bias, q_norm, k_norm, lb_logits, hgrn_norm, sgu_norm, spatial_w, spatial_b, w_up, merge_b, w_o, norm_ple, w_ple_gate, w_ple_proj, loss_target, m_norm_mix, m_w_in, m_conv_w, m_conv_b, m_fgate_bias, m_q_norm, m_k_norm, m_lb_logits, m_hgrn_norm, m_sgu_norm, m_spatial_w, m_spatial_b, m_w_up, m_merge_b, m_w_o, m_norm_ple, m_w_ple_gate, m_w_ple_proj, v_norm_mix, v_w_in, v_conv_w, v_conv_b, v_fgate_bias, v_q_norm, v_k_norm, v_lb_logits, v_hgrn_norm, v_sgu_norm, v_spatial_w, v_spatial_b, v_w_up, v_merge_b, v_w_o, v_norm_ple, v_w_ple_gate, v_w_ple_proj):
    loc = dict(locals())
    w = {n: loc[n] for n in _WEIGHTS}
    m = {n: loc["m_" + n] for n in _WEIGHTS}
    v = {n: loc["v_" + n] for n in _WEIGHTS}
    chip = 2 * lax.axis_index("x") + lax.axis_index("y")

    dist = _Dist(w)
    loss_blk, dx, grads = _local_step(x[0], p[:, 0], loss_target[0], w, dist)
    loss = lax.psum(loss_blk[0, 0], ("x", "y", "c"))

    blocks = [_rows128(grads[n]) for n in _SMALL]
    nrows = sum(b.shape[0] for b in blocks)
    packed = jnp.concatenate(blocks + [jnp.zeros(((-nrows) % 8, 128), F32)], axis=0)
    red = _allreduce_small(packed)
    small, off = {}, 0
    for n, b in zip(_SMALL, blocks):
        small[n] = _from_rows128(red[off:off + b.shape[0]], grads[n].shape)
        off += b.shape[0]
    for n in ("conv_w", "merge_b"):
        ax = _SHARD_AXIS[n]
        width = small[n].shape[ax] // 4
        small[n] = lax.dynamic_slice_in_dim(small[n], chip * width, width, axis=ax)

    out_g, out_d, out_m, out_v = {}, {}, {}, {}
    for n in _WEIGHTS:
        shp = w[n].shape
        if n in _BIG:
            w3, m3, v3 = (a.reshape((DEPTH, -1, shp[-1])) for a in (w[n], m[n], v[n]))
            parts = [dist.contrib[l][_BIG.index(n)].reshape((8,) + w3.shape[1:]) for l in range(DEPTH)]
            g, d, nm, nv = _adamw(w3, m3, v3, parts, 8)
        else:
            g, d, nm, nv = _adamw(_as3d(w[n]), _as3d(m[n]), _as3d(v[n]), _as3d(small[n])[None], 1)
        out_g[n], out_d[n], out_m[n], out_v[n] = (a.reshape(shp) for a in (g, d, nm, nv))

    return (loss, dx[None], *[out_g[n] for n in _WEIGHTS], *[out_d[n] for n in _WEIGHTS],
            *[out_m[n] for n in _WEIGHTS], *[out_v[n] for n in _WEIGHTS])
```

```python
import functools

import numpy as np
import jax
import jax.numpy as jnp
from jax import lax
from jax.experimental import pallas as pl
from jax.experimental.pallas import tpu as pltpu

F32 = jnp.float32
MXU_DTYPE = jnp.bfloat16

D_MODEL = 1024
BW = 256
NH = 4
DH = 64
DEPTH = 4
NBR = 4
PLE = 256
CHUNK = 64
SCHUNK = 128
EPS = 1e-6
MASK_VALUE = -1e30
NLEV = 6

ADAM_LR, ADAM_B1, ADAM_B2, ADAM_EPS, ADAM_WD, ADAM_STEP = 0.001, 0.9, 0.999, 1e-08, 0.01, 10

ZM, ZA, ZB, ZC, ZD, ZF = 0, 4096, 5120, 6144, 7168, 8192
ZCOLS = 8320
_OA, _OB, _OF, _OC, _OD, _OM, _OEND = 0, 1024, 2048, 2052, 3076, 3844, 7940

VMEM_LIMIT = 56 * 1024 * 1024
ROW_TILE = 512
FLASH_TILE = 1024
GRAD_WIRE_DTYPE = jnp.bfloat16
MERGE_TILE = 256
MESH = pl.DeviceIdType.MESH
_ANY = pl.BlockSpec(memory_space=pl.ANY)


def _tile(n, pref):
    t = min(n, pref)
    assert n % t == 0, (n, t)
    return t


def _call(body, *, name, grid, in_specs, out_specs, out_shape, scratch=(), sem=None, aliases=None):
    return pl.pallas_call(
        functools.partial(body), name=name, grid=grid, in_specs=in_specs, out_specs=out_specs,
        out_shape=out_shape, scratch_shapes=list(scratch), input_output_aliases=aliases or {},
        compiler_params=pltpu.CompilerParams(dimension_semantics=sem, vmem_limit_bytes=VMEM_LIMIT))


def _mx(x):
    return x.astype(MXU_DTYPE)


def _dot(a, b):
    return jnp.dot(_mx(a), _mx(b), preferred_element_type=F32)


def _dot_nt(a, b):
    return lax.dot_general(_mx(a), _mx(b), (((1,), (1,)), ((), ())), preferred_element_type=F32)


def _dot_tn(a, b):
    return lax.dot_general(_mx(a), _mx(b), (((0,), (0,)), ((), ())), preferred_element_type=F32)


def _top16(x):
    u = lax.bitcast_convert_type(x, jnp.uint32) & jnp.uint32(0xFFFF0000)
    return lax.bitcast_convert_type(u, F32)


def _split3(x):
    hi = _top16(x)
    r1 = x - hi
    mid = _top16(r1)
    return _mx(hi), _mx(mid), _mx(r1 - mid)


def _dot_exact_lhs01(t, x):
    hi, mid, lo = _split3(x)
    t = _mx(t)
    return (jnp.dot(t, hi, preferred_element_type=F32) + jnp.dot(t, mid, preferred_element_type=F32)
            + jnp.dot(t, lo, preferred_element_type=F32))


def _sigmoid(x):
    return jax.nn.sigmoid(x)


def _silu(x):
    return x * _sigmoid(x)


def _dsilu(x):
    s = _sigmoid(x)
    return s * (1.0 + x * (1.0 - s))


def _log_sigmoid(x):
    return jnp.minimum(x, 0.0) - jnp.log(1.0 + jnp.exp(-jnp.abs(x)))


def _rms_fwd(x, g):
    r = lax.rsqrt(jnp.mean(x * x, axis=-1, keepdims=True) + EPS)
    return x * r * g


def _rms_bwd(x, g, dy):
    r = lax.rsqrt(jnp.mean(x * x, axis=-1, keepdims=True) + EPS)
    n = x * r
    t = dy * g
    dx = r * (t - n * jnp.mean(t * n, axis=-1, keepdims=True))
    return dx, dy * n


def _tri(n, upper=False):
    r = lax.broadcasted_iota(jnp.int32, (n, n), 0)
    c = lax.broadcasted_iota(jnp.int32, (n, n), 1)
    return jnp.where((c >= r) if upper else (r >= c), 1.0, 0.0).astype(F32)


def _hs(h):
    return slice(h * DH, (h + 1) * DH)


def _mm_nn(a, b, out_dtype, *, name, ts=None, tn=None, tk=None, b_transposed=False, comm=None):
    S, K = a.shape
    N = b.shape[0] if b_transposed else b.shape[1]
    ts = _tile(S, ts or ROW_TILE)
    tn = _tile(N, tn or N)
    tk = _tile(K, tk or K)
    nk = K // tk
    b_spec = (pl.BlockSpec((tn, tk), lambda j, i, k: (j, k)) if b_transposed
              else pl.BlockSpec((tk, tn), lambda j, i, k: (k, j)))

    def body(a_ref, b_ref, o_ref, acc_ref):
        k = pl.program_id(2)
        part = _dot_nt(a_ref[...], b_ref[...]) if b_transposed else jnp.dot(a_ref[...], b_ref[...],
                                                                          preferred_element_type=F32)
        if nk == 1:
            o_ref[...] = part.astype(o_ref.dtype)
        else:
            @pl.when(k == 0)
            def _():
                acc_ref[...] = part

            @pl.when(k > 0)
            def _():
                acc_ref[...] += part

            @pl.when(k == nk - 1)
            def _():
                o_ref[...] = acc_ref[...].astype(o_ref.dtype)

    grid = (N // tn, S // ts, nk)

    def when():
        at = [pl.program_id(d) for d in range(3)]
        first = (at[0] == 0) & (at[1] == 0) & (at[2] == 0)
        last = (at[0] == grid[0] - 1) & (at[1] == grid[1] - 1) & (at[2] == grid[2] - 1)
        return first, last, last

    xin, xout, xshape, xsem = _comm_specs(comm)
    res = _call(
        _fuse_comm(body, 2, 1, comm, when), name=name, grid=grid,
        in_specs=[pl.BlockSpec((ts, tk), lambda j, i, k: (i, k)), b_spec] + xin,
        out_specs=[pl.BlockSpec((ts, tn), lambda j, i, k: (i, j))] + xout,
        out_shape=[jax.ShapeDtypeStruct((S, N), out_dtype)] + xshape,
        scratch=[pltpu.VMEM((ts, tn) if nk > 1 else (8, 128), F32)] + xsem,
        sem=("arbitrary", "arbitrary", "arbitrary") if comm else ("parallel", "parallel", "arbitrary"),
        )(a, b, *(comm[1] if comm else []))
    return (res[0], list(res[1:])) if comm else res[0]


def _mm_tn(x, y, *, name, tn=None, ycol=0, n=None):
    S, M = x.shape
    n = n or y.shape[1]
    ts = _tile(S, 4 * ROW_TILE)
    tn = _tile(n, tn or n)
    nj = n // tn

    def body(x_ref, y_ref, o_ref):
        @pl.when(pl.program_id(1) == 0)
        def _():
            o_ref[...] = jnp.zeros_like(o_ref)

        o_ref[...] += lax.dot_general(x_ref[...], y_ref[...], (((0,), (0,)), ((), ())), preferred_element_type=F32)

    return _call(
        body, name=name, grid=(nj, S // ts),
        in_specs=[pl.BlockSpec((ts, M), lambda j, s: (s, 0)), pl.BlockSpec((ts, tn), lambda j, s: (s, ycol * nj + j))],
        out_specs=pl.BlockSpec((M, tn), lambda j, s: (0, j)),
        out_shape=jax.ShapeDtypeStruct((M, n), F32),
        sem=("parallel", "arbitrary"))(x, y)


def _norm_fwd(x, g):
    S = x.shape[0]
    ts = _tile(S, ROW_TILE)

    def body(x_ref, g_ref, h_ref):
        h_ref[...] = _mx(_rms_fwd(x_ref[...], g_ref[...]))

    return _call(
        body, name="norm_fwd", grid=(S // ts,),
        in_specs=[pl.BlockSpec((ts, D_MODEL), lambda i: (i, 0)), pl.BlockSpec((1, D_MODEL), lambda i: (0, 0))],
        out_specs=pl.BlockSpec((ts, D_MODEL), lambda i: (i, 0)),
        out_shape=jax.ShapeDtypeStruct((S, D_MODEL), MXU_DTYPE), sem=("parallel",))(x, g)


def _norm_bwd(x, g, dh, dres):
    S = x.shape[0]
    ts = _tile(S, ROW_TILE)

    def body(x_ref, g_ref, dh_ref, dr_ref, dx_ref, st_ref):
        @pl.when(pl.program_id(0) == 0)
        def _():
            st_ref[...] = jnp.zeros_like(st_ref)

        dx, dgr = _rms_bwd(x_ref[...], g_ref[...], dh_ref[...])
        dx_ref[...] = dr_ref[...] + dx
        st_ref[0:1, :] += jnp.sum(dgr, axis=0, keepdims=True)

    row = pl.BlockSpec((ts, D_MODEL), lambda i: (i, 0))
    return _call(
        body, name="norm_bwd", grid=(S // ts,),
        in_specs=[row, pl.BlockSpec((1, D_MODEL), lambda i: (0, 0)), row, row],
        out_specs=[row, pl.BlockSpec((8, D_MODEL), lambda i: (0, 0))],
        out_shape=[jax.ShapeDtypeStruct((S, D_MODEL), F32), jax.ShapeDtypeStruct((8, D_MODEL), F32)],
        sem=("arbitrary",))(x, g, dh, dres)


def _loss_fwd_bwd(y, target):
    S = y.shape[0]
    ts = _tile(S, ROW_TILE)

    def body(y_ref, t_ref, l_ref, dy_ref):
        @pl.when(pl.program_id(0) == 0)
        def _():
            l_ref[...] = jnp.zeros_like(l_ref)

        err = y_ref[...] - t_ref[...]
        dy_ref[...] = err * (1.0 / D_MODEL)
        rowloss = jnp.mean(err * err, axis=-1, keepdims=True)
        l_ref[...] += 0.5 * jnp.sum(rowloss, axis=0, keepdims=True)

    row = pl.BlockSpec((ts, D_MODEL), lambda i: (i, 0))
    return _call(
        body, name="loss", grid=(S // ts,), in_specs=[row, row],
        out_specs=[pl.BlockSpec((8, 128), lambda i: (0, 0)), row],
        out_shape=[jax.ShapeDtypeStruct((8, 128), F32), jax.ShapeDtypeStruct((S, D_MODEL), F32)],
        sem=("arbitrary",))(y, target)


def _shift_down(x, k, halo, rows):
    y = pltpu.roll(x, k, 0)
    for j in range(k):
        y = jnp.where(rows == j, halo[8 - k + j:8 - k + j + 1, :], y)
    return y


def _shift_up(x, k, halo, rows, n):
    y = pltpu.roll(x, n - k, 0)
    for j in range(k):
        y = jnp.where(rows == n - k + j, halo[j:j + 1, :], y)
    return y


def _conv_parts(za, zh, cw, first, rows):
    ax, ab, ac, ag = za[:, 0:BW], za[:, BW:2 * BW], za[:, 2 * BW:3 * BW], za[:, 3 * BW:4 * BW]
    zz = ac * ax
    hz = jnp.where(first, 0.0, zh[:, 2 * BW:3 * BW] * zh[:, 0:BW])
    zz1 = _shift_down(zz, 1, hz, rows)
    zz2 = _shift_down(zz, 2, hz, rows)
    conv = zz2 * cw[0:1, :] + zz1 * cw[1:2, :] + zz * cw[2:3, :] + cw[3:4, :]
    return ax, ab, ac, ag, zz, zz1, zz2, conv


def _conv_fwd(z, cw):
    S = z.shape[0]
    ts = _tile(S, ROW_TILE)
    hb = ts // 8

    def body(za_ref, zh_ref, cw_ref, y_ref):
        i = pl.program_id(0)
        rows = lax.broadcasted_iota(jnp.int32, (ts, BW), 0)
        ax, ab, ac, ag, zz, zz1, zz2, conv = _conv_parts(za_ref[...], zh_ref[...], cw_ref[...], i == 0, rows)
        y_ref[...] = _mx(ab * conv * _silu(ag))

    return _call(
        body, name="conv_fwd", grid=(S // ts,),
        in_specs=[pl.BlockSpec((ts, 1024), lambda i: (i, ZA // 1024)),
                  pl.BlockSpec((8, 1024), lambda i: (jnp.maximum(i * hb - 1, 0), ZA // 1024)),
                  pl.BlockSpec((8, BW), lambda i: (0, 0))],
        out_specs=pl.BlockSpec((ts, BW), lambda i: (i, 0)),
        out_shape=jax.ShapeDtypeStruct((S, BW), MXU_DTYPE), sem=("parallel",))(z, z, cw)


def _conv_bwd(z, cw, dy, dz):
    S = z.shape[0]
    ts = _tile(S, ROW_TILE)
    hb = ts // 8
    nt = S // ts

    def body(za_ref, zh_ref, zn_ref, cw_ref, dy_ref, dyn_ref, dz_in, dz_ref, st_ref):
        i = pl.program_id(0)

        @pl.when(i == 0)
        def _():
            st_ref[...] = jnp.zeros_like(st_ref)

        cw = cw_ref[...]
        rows = lax.broadcasted_iota(jnp.int32, (ts, BW), 0)
        ax, ab, ac, ag, zz, zz1, zz2, conv = _conv_parts(za_ref[...], zh_ref[...], cw, i == 0, rows)
        dy = dy_ref[...]
        sg = _silu(ag)
        dc = dy * ab * sg
        zn = zn_ref[...]
        dcn = jnp.where(i == nt - 1, 0.0, dyn_ref[...] * zn[:, BW:2 * BW] * _silu(zn[:, 3 * BW:4 * BW]))
        dc1 = _shift_up(dc, 1, dcn, rows, ts)
        dc2 = _shift_up(dc, 2, dcn, rows, ts)
        dzz = dc * cw[2:3, :] + dc1 * cw[1:2, :] + dc2 * cw[0:1, :]
        dz_ref[:, 0:BW] = _mx(dzz * ac)
        dz_ref[:, BW:2 * BW] = _mx(dy * conv * sg)
        dz_ref[:, 2 * BW:3 * BW] = _mx(dzz * ax)
        dz_ref[:, 3 * BW:4 * BW] = _mx(dy * ab * conv * _dsilu(ag))
        st_ref[0:1, :] += jnp.sum(dc * zz2, axis=0, keepdims=True)
        st_ref[1:2, :] += jnp.sum(dc * zz1, axis=0, keepdims=True)
        st_ref[2:3, :] += jnp.sum(dc * zz, axis=0, keepdims=True)
        st_ref[3:4, :] += jnp.sum(dc, axis=0, keepdims=True)

    return _call(
        body, name="conv_bwd", grid=(nt,),
        in_specs=[pl.BlockSpec((ts, 1024), lambda i: (i, ZA // 1024)),
                  pl.BlockSpec((8, 1024), lambda i: (jnp.maximum(i * hb - 1, 0), ZA // 1024)),
                  pl.BlockSpec((8, 1024), lambda i: (jnp.minimum((i + 1) * hb, S // 8 - 1), ZA // 1024)),
                  pl.BlockSpec((8, BW), lambda i: (0, 0)),
                  pl.BlockSpec((ts, BW), lambda i: (i, 0)),
                  pl.BlockSpec((8, BW), lambda i: (jnp.minimum((i + 1) * hb, S // 8 - 1), 0)), _ANY],
        out_specs=[pl.BlockSpec((ts, 1024), lambda i: (i, ZA // 1024)), pl.BlockSpec((8, BW), lambda i: (0, 0))],
        out_shape=[jax.ShapeDtypeStruct((S, ZCOLS), MXU_DTYPE), jax.ShapeDtypeStruct((8, BW), F32)],
        sem=("arbitrary",), aliases={6: 0})(z, z, z, cw, dy, dy, dz)


AW = 128
_AUG = DH


def _split3_f32(x):
    hi = _top16(x)
    r1 = x - hi
    mid = _top16(r1)
    return hi, mid, r1 - mid


def _aug_lanes(n, cols):
    lane = lax.broadcasted_iota(jnp.int32, (n, DH), 1)
    out = jnp.zeros((n, DH), F32)
    for e, c in enumerate(cols):
        out = jnp.where(lane == e, c, out)
    return out


def _put_aug(ref, h, col, parts=None, const=None):
    base = h * AW + _AUG + col
    if parts is None:
        ref[:, base:base + 3] = jnp.full((ref.shape[0], 3), const, ref.dtype)
    else:
        for e, part in enumerate(parts):
            ref[:, base + e:base + e + 1] = part.astype(ref.dtype)


def _attn_prep_fwd(z, fb, gq, gk):
    S = z.shape[0]
    ts = _tile(S, ROW_TILE)

    def body(zb_ref, zf_ref, fb_ref, gq_ref, gk_ref, qa_ref, ka_ref, va_ref, cc_ref, carry):
        @pl.when(pl.program_id(0) == 0)
        def _():
            carry[...] = jnp.zeros_like(carry)

        zb = zb_ref[...]
        gqv, gkv = gq_ref[...], gk_ref[...]
        lf = _log_sigmoid(zf_ref[...] + fb_ref[...])
        cum = _dot_exact_lhs01(_tri(ts), lf) + carry[...]
        carry[...] = cum[ts - 1:ts, :]
        cc_ref[...] = cum
        pieces = _split3_f32(cum)
        for h in range(NH):
            ph = [pc[:, h:h + 1] for pc in pieces]
            qh = _rms_fwd(zb[:, _hs(h)], gqv) * (DH ** -0.5)
            kh = _rms_fwd(zb[:, BW + h * DH:BW + (h + 1) * DH], gkv)
            vh = zb[:, 2 * BW + h * DH:2 * BW + (h + 1) * DH]
            qa_ref[:, _ah(h)] = _mx(jnp.concatenate([qh, _aug_lanes(ts, ph + [1.0] * 3)], axis=1))
            ka_ref[:, _ah(h)] = _mx(jnp.concatenate([kh, _aug_lanes(ts, [1.0] * 3 + [-x for x in ph])], axis=1))
            va_ref[:, _ah(h)] = _mx(jnp.concatenate([vh, _aug_lanes(ts, [-1.0] * 3)], axis=1))

    row = lambda w: pl.BlockSpec((ts, w), lambda i: (i, 0))
    return _call(
        body, name="attn_prep_fwd", grid=(S // ts,),
        in_specs=[pl.BlockSpec((ts, 1024), lambda i: (i, ZB // 1024)), pl.BlockSpec((ts, 128), lambda i: (i, ZF // 128)),
                  pl.BlockSpec((1, 128), lambda i: (0, 0)), pl.BlockSpec((1, DH), lambda i: (0, 0)),
                  pl.BlockSpec((1, DH), lambda i: (0, 0))],
        out_specs=[row(NH * AW), row(NH * AW), row(NH * AW), row(128)],
        out_shape=[jax.ShapeDtypeStruct((S, NH * AW), MXU_DTYPE)] * 3 + [jax.ShapeDtypeStruct((S, 128), F32)],
        scratch=[pltpu.VMEM((1, 128), F32)], sem=("arbitrary",))(z, z, fb, gq, gk)


ROW_CHUNK = 256


def _square_steps(n):
    def when():
        i, j = pl.program_id(0), pl.program_id(1)
        return (i == 0) & (j == 0), (i == n - 1) & (j == 0), (i == n - 1) & (j == n - 1)
    return when


def _fuse_comm(core, n_in, n_out, comm, when):
    nc = 0 if comm is None else len(comm[1])

    def body(*refs):
        cin, xin = refs[:n_in], refs[n_in:n_in + nc]
        a = n_in + nc
        cout, xout = refs[a:a + n_out], refs[a + n_out:a + n_out + nc]
        rest = refs[a + n_out + nc:]
        if nc == 0:
            core(*cin, *cout, *rest)
            return
        cscr, sems = rest[:-3], rest[-3:]
        first, middle, last = when()
        phase = _COMM_PHASES[comm[0]]

        @pl.when(first)
        def _():
            phase(0, xin, xout, sems)

        core(*cin, *cout, *cscr)

        @pl.when(middle)
        def _():
            phase(1, xin, xout, sems)

        @pl.when(last)
        def _():
            phase(2, xin, xout, sems)

    return body


def _comm_specs(comm):
    if comm is None:
        return [], [], [], []
    hbm = pl.BlockSpec(memory_space=pl.ANY)
    nc = len(comm[1])
    return [hbm] * nc, [hbm] * nc, _COMM_OUT[comm[0]](comm[1]), _comm_sems(comm[0], nc)


def _ah(h):
    return slice(h * AW, (h + 1) * AW)


def _ahd(h):
    return slice(h * AW, h * AW + DH)


def _causal(shape, row0, transposed=False):
    r = row0 + lax.broadcasted_iota(jnp.int32, shape, 0)
    c = lax.broadcasted_iota(jnp.int32, shape, 1)
    return (r <= c) if transposed else (r >= c)


def _flash_fwd(qa, ka, va, z, comm=None):
    S = qa.shape[0]
    t = _tile(S, FLASH_TILE)
    n = S // t
    rch = _tile(t, ROW_CHUNK)

    def core(q_ref, k_ref, v_ref, zb_ref, o_ref, lse_ref, y_ref, m_sc, l_sc, acc):
        i, j = pl.program_id(0), pl.program_id(1)

        @pl.when(j == 0)
        def _():
            m_sc[...] = jnp.full_like(m_sc, MASK_VALUE)
            l_sc[...] = jnp.zeros_like(l_sc)
            acc[...] = jnp.zeros_like(acc)

        def block(masked):
            for h in range(NH):
                for rc in range(t // rch):
                    rows = slice(rc * rch, (rc + 1) * rch)
                    s = _dot_nt(q_ref[rows, _ah(h)], k_ref[:, _ah(h)])
                    if masked:
                        s = jnp.where(_causal(s.shape, rc * rch), s, MASK_VALUE)
                    m_old = m_sc[h, rows, :]
                    m_new = jnp.maximum(m_old, jnp.max(s, axis=-1, keepdims=True))
                    p = jnp.exp(s - m_new)
                    alpha = jnp.exp(m_old - m_new)
                    l_sc[h, rows, :] = alpha * l_sc[h, rows, :] + jnp.sum(p, axis=-1, keepdims=True)
                    acc[rows, _hs(h)] = alpha * acc[rows, _hs(h)] + _dot(p, v_ref[:, _ahd(h)])
                    m_sc[h, rows, :] = m_new

        @pl.when(j < i)
        def _():
            block(False)

        @pl.when(j == i)
        def _():
            block(True)
            lse_ref[...] = jnp.zeros_like(lse_ref)
            for h in range(NH):
                o_ref[:, _hs(h)] = acc[:, _hs(h)] / l_sc[h]
                lse_ref[:, h:h + 1] = m_sc[h] + jnp.log(l_sc[h])
            y_ref[...] = _mx(o_ref[...] * _silu(zb_ref[:, 3 * BW:4 * BW]))

    qspec = lambda w: pl.BlockSpec((t, w), lambda i, j: (i, 0))
    kspec = lambda w: pl.BlockSpec((t, w), lambda i, j: (jnp.minimum(j, i), 0))
    xin, xout, xshape, xsem = _comm_specs(comm)
    res = _call(
        _fuse_comm(core, 4, 3, comm, _square_steps(n)), name="flash_fwd", grid=(n, n),
        in_specs=[qspec(NH * AW), kspec(NH * AW), kspec(NH * AW),
                  pl.BlockSpec((t, 1024), lambda i, j: (i, ZB // 1024))] + xin,
        out_specs=[qspec(BW), qspec(128), qspec(BW)] + xout,
        out_shape=[jax.ShapeDtypeStruct((S, BW), F32), jax.ShapeDtypeStruct((S, 128), F32),
                   jax.ShapeDtypeStruct((S, BW), MXU_DTYPE)] + xshape,
        scratch=[pltpu.VMEM((NH, t, 1), F32), pltpu.VMEM((NH, t, 1), F32), pltpu.VMEM((t, BW), F32)] + xsem,
        sem=("arbitrary", "arbitrary"))(qa, ka, va, z, *(comm[1] if comm else []))
    return res[0], res[1], res[2], list(res[3:])


def _attn_gate_bwd(dy, o, z, qa, cc, lse):
    S = dy.shape[0]
    ts = _tile(S, ROW_TILE)

    def body(dy_ref, o_ref, zb_ref, qa_ref, cc_ref, lse_ref, dg_ref, qb_ref, doa_ref):
        g = zb_ref[:, 3 * BW:4 * BW]
        dy, o = dy_ref[...], o_ref[...]
        do = dy * _silu(g)
        dg_ref[...] = _mx(dy * o * _dsilu(g))
        qb_ref[...] = qa_ref[...]
        doa_ref[...] = jnp.zeros_like(doa_ref)
        shifted = _split3_f32(cc_ref[...] - lse_ref[...])
        for h in range(NH):
            doh = do[:, _hs(h)]
            doa_ref[:, _ahd(h)] = _mx(doh)
            delta = jnp.sum(doh * o[:, _hs(h)], axis=-1, keepdims=True)
            _put_aug(doa_ref, h, 0, parts=_split3_f32(delta))
            _put_aug(qb_ref, h, 0, parts=[pc[:, h:h + 1] for pc in shifted])

    row = lambda w: pl.BlockSpec((ts, w), lambda i: (i, 0))
    return _call(
        body, name="attn_gate_bwd", grid=(S // ts,),
        in_specs=[pl.BlockSpec((ts, BW), lambda i: (i, 1)), row(BW), pl.BlockSpec((ts, 1024), lambda i: (i, ZB // 1024)),
                  row(NH * AW), row(128), row(128)],
        out_specs=[row(BW), row(NH * AW), row(NH * AW)],
        out_shape=[jax.ShapeDtypeStruct((S, BW), MXU_DTYPE), jax.ShapeDtypeStruct((S, NH * AW), MXU_DTYPE),
                   jax.ShapeDtypeStruct((S, NH * AW), MXU_DTYPE)],
        sem=("parallel",))(dy, o, z, qa, cc, lse)


def _aug_value(ref, h, rows=slice(None)):
    base = h * AW + _AUG
    x = ref[rows, base:base + 3].astype(F32)
    return x[:, 0:1] + x[:, 1:2] + x[:, 2:3]


def _flash_bwd_dq(qb, ka, va, doa, comm=None):
    S = qb.shape[0]
    t = _tile(S, FLASH_TILE)
    n = S // t
    rch = _tile(t, ROW_CHUNK)

    def core(q_ref, k_ref, v_ref, do_ref, dq_ref, do2_ref, dr_sc):
        i, j = pl.program_id(0), pl.program_id(1)

        @pl.when(j == 0)
        def _():
            dq_ref[...] = jnp.zeros_like(dq_ref)
            dr_sc[...] = jnp.zeros_like(dr_sc)

        def block(masked):
            for h in range(NH):
                for rc in range(t // rch):
                    rows = slice(rc * rch, (rc + 1) * rch)
                    p = jnp.exp(_dot_nt(q_ref[rows, _ah(h)], k_ref[:, _ah(h)]))
                    if masked:
                        p = jnp.where(_causal(p.shape, rc * rch), p, 0.0)
                    ds = p * _dot_nt(do_ref[rows, _ah(h)], v_ref[:, _ah(h)])
                    dq_ref[rows, _hs(h)] += _dot(ds, k_ref[:, _ahd(h)])
                    dr_sc[h, rows, :] += jnp.sum(ds, axis=-1, keepdims=True)

        @pl.when(j < i)
        def _():
            block(False)

        @pl.when(j == i)
        def _():
            block(True)
            do2_ref[...] = do_ref[...]
            for h in range(NH):
                _put_aug(do2_ref, h, 0, parts=_split3_f32(_aug_value(do_ref, h) + dr_sc[h]))

    qspec = lambda w: pl.BlockSpec((t, w), lambda i, j: (i, 0))
    kspec = lambda w: pl.BlockSpec((t, w), lambda i, j: (jnp.minimum(j, i), 0))
    xin, xout, xshape, xsem = _comm_specs(comm)
    res = _call(
        _fuse_comm(core, 4, 2, comm, _square_steps(n)), name="flash_bwd_dq", grid=(n, n),
        in_specs=[qspec(NH * AW), kspec(NH * AW), kspec(NH * AW), qspec(NH * AW)] + xin,
        out_specs=[qspec(BW), qspec(NH * AW)] + xout,
        out_shape=[jax.ShapeDtypeStruct((S, BW), F32), jax.ShapeDtypeStruct((S, NH * AW), MXU_DTYPE)] + xshape,
        scratch=[pltpu.VMEM((NH, t, 1), F32)] + xsem,
        sem=("arbitrary", "arbitrary"))(qb, ka, va, doa, *(comm[1] if comm else []))
    return res[0], res[1], list(res[2:])


def _flash_bwd_dkv(qb, ka, va, doa):
    S = qb.shape[0]
    t = _tile(S, FLASH_TILE)
    n = S // t

    def body(q_ref, k_ref, v_ref, do_ref, dk_ref, dv_ref, dc_ref):
        j, i = pl.program_id(0), pl.program_id(1)

        @pl.when(i == 0)
        def _():
            dk_ref[...] = jnp.zeros_like(dk_ref)
            dv_ref[...] = jnp.zeros_like(dv_ref)
            dc_ref[...] = jnp.zeros_like(dc_ref)

        def block(masked):
            for h in range(NH):
                pt = jnp.exp(_dot_nt(k_ref[:, _ah(h)], q_ref[:, _ah(h)]))
                if masked:
                    pt = jnp.where(_causal(pt.shape, 0, transposed=True), pt, 0.0)
                dst = pt * _dot_nt(v_ref[:, _ah(h)], do_ref[:, _ah(h)])
                dv_ref[:, _hs(h)] += _dot(pt, do_ref[:, _ahd(h)])
                dk_ref[:, _hs(h)] += _dot(dst, q_ref[:, _ahd(h)])
                dc_ref[:, h:h + 1] += -jnp.sum(dst, axis=-1, keepdims=True)

        @pl.when(i > j)
        def _():
            block(False)

        @pl.when(i == j)
        def _():
            block(True)

    qspec = lambda w: pl.BlockSpec((t, w), lambda j, i: (jnp.maximum(i, j), 0))
    kspec = lambda w: pl.BlockSpec((t, w), lambda j, i: (j, 0))
    return _call(
        body, name="flash_bwd_dkv", grid=(n, n),
        in_specs=[qspec(NH * AW), kspec(NH * AW), kspec(NH * AW), qspec(NH * AW)],
        out_specs=[kspec(BW), kspec(BW), kspec(128)],
        out_shape=[jax.ShapeDtypeStruct((S, BW), F32), jax.ShapeDtypeStruct((S, BW), F32),
                   jax.ShapeDtypeStruct((S, 128), F32)],
        sem=("parallel", "arbitrary"))(qb, ka, va, doa)


def _attn_prep_bwd(z, fb, gq, gk, dq, dk, dv, dgate, dcc, dz):
    S = z.shape[0]
    ts = _tile(S, ROW_TILE)
    nt = S // ts

    def body(zb_ref, zf_ref, fb_ref, gq_ref, gk_ref, dq_ref, dk_ref, dv_ref, dg_ref, dcc_ref, dz_in, dzb_ref, dzf_ref,
             st_ref, carry):
        @pl.when(pl.program_id(0) == 0)
        def _():
            carry[...] = jnp.zeros_like(carry)
            st_ref[...] = jnp.zeros_like(st_ref)

        zb = zb_ref[...]
        gqv, gkv = gq_ref[...], gk_ref[...]
        dqv, dkv = dq_ref[...], dk_ref[...]
        sq = jnp.zeros((1, DH), F32)
        sk = jnp.zeros((1, DH), F32)
        for h in range(NH):
            dx, dgr = _rms_bwd(zb[:, _hs(h)], gqv, dqv[:, _hs(h)] * (DH ** -0.5))
            dzb_ref[:, _hs(h)] = _mx(dx)
            sq = sq + jnp.sum(dgr, axis=0, keepdims=True)
            ks = slice(BW + h * DH, BW + (h + 1) * DH)
            dx, dgr = _rms_bwd(zb[:, ks], gkv, dkv[:, _hs(h)])
            dzb_ref[:, ks] = _mx(dx)
            sk = sk + jnp.sum(dgr, axis=0, keepdims=True)
        dzb_ref[:, 2 * BW:3 * BW] = _mx(dv_ref[...])
        dzb_ref[:, 3 * BW:4 * BW] = dg_ref[...]
        dc = dcc_ref[...]
        dlf = _dot_exact_lhs01(_tri(ts, upper=True), dc) + carry[...]
        carry[...] = dlf[0:1, :]
        dfz = dlf * _sigmoid(-(zf_ref[...] + fb_ref[...]))
        dzf_ref[...] = _mx(dfz)
        st_ref[0:1, 0:DH] += sq
        st_ref[1:2, 0:DH] += sk
        st_ref[2:3, :] += jnp.sum(dfz, axis=0, keepdims=True)

    rev = lambda w, c=0: pl.BlockSpec((ts, w), lambda i: (nt - 1 - i, c))
    one = lambda w: pl.BlockSpec((1, w), lambda i: (0, 0))
    return _call(
        body, name="attn_prep_bwd", grid=(nt,),
        in_specs=[rev(1024, ZB // 1024), rev(128, ZF // 128), one(128), one(DH), one(DH),
                  rev(BW), rev(BW), rev(BW), rev(BW), rev(128), _ANY],
        out_specs=[rev(1024, ZB // 1024), rev(128), pl.BlockSpec((8, 128), lambda i: (0, 0))],
        out_shape=[jax.ShapeDtypeStruct((S, ZCOLS), MXU_DTYPE), jax.ShapeDtypeStruct((S, 128), MXU_DTYPE),
                   jax.ShapeDtypeStruct((8, 128), F32)],
        scratch=[pltpu.VMEM((1, 128), F32)], sem=("arbitrary",), aliases={10: 0})(z, z, fb, gq, gk, dq, dk, dv, dgate, dcc, dz)


def _put_fgate_cols(dzf, dz):
    S = dzf.shape[0]
    ts = _tile(S, ROW_TILE)

    def body(f_ref, dz_in, o_ref):
        o_ref[...] = f_ref[...]

    return _call(
        body, name="put_fgate_cols", grid=(S // ts,),
        in_specs=[pl.BlockSpec((ts, 128), lambda i: (i, 0)), _ANY],
        out_specs=pl.BlockSpec((ts, 128), lambda i: (i, ZF // 128)),
        out_shape=jax.ShapeDtypeStruct((S, ZCOLS), MXU_DTYPE), sem=("parallel",), aliases={1: 0})(dzf, dz)


def _hgrn_consts():
    C = CHUNK
    t = np.arange(C)[:, None]
    j = np.arange(C)[None, :]
    masks = []
    for lev in range(NLEV):
        m = C >> (lev + 1)
        blk, pos = t // (2 * m), t % (2 * m)
        sblk, spos = j // (2 * m), j % (2 * m)
        masks.append((blk == sblk) & (pos >= m) & (spos < m))
    masks.append(t == j)
    return (j <= t).astype(np.float32), np.stack(masks).astype(np.float32)


def _level_exponents(b, lg):
    row = lax.broadcasted_iota(jnp.int32, (CHUNK, 1), 0)
    eqs, eks = [], []
    for lev in range(NLEV):
        m = CHUNK >> (lev + 1)
        pos = jnp.bitwise_and(row, 2 * m - 1)
        if 2 * m >= 8:
            b3 = b.reshape(CHUNK // (2 * m), 2 * m, b.shape[1])
            mid = jnp.broadcast_to(b3[:, m - 1:m, :], b3.shape).reshape(b.shape)
            eqs.append(jnp.where(pos >= m, b - mid, 0.0))
            eks.append(jnp.where(pos < m, mid - b, 0.0))
        elif m == 2:
            eqs.append(jnp.where(pos == 2, lg, jnp.where(pos == 3, lg + pltpu.roll(lg, 1, 0), 0.0)))
            eks.append(jnp.where(pos == 0, pltpu.roll(lg, CHUNK - 1, 0), 0.0))
        else:
            eqs.append(jnp.where(pos == 1, lg, 0.0))
            eks.append(jnp.zeros_like(lg))
    return jnp.concatenate([b] + eqs + eks, axis=0)


def _hgrn_chunk_fwd(zc, lb, tmat):
    cq, cf, ci = zc[:, 0:BW], zc[:, BW:2 * BW], zc[:, 2 * BW:3 * BW]
    q = _silu(cq)
    sg = _sigmoid(cf)
    g = lb + (1.0 - lb) * sg
    lg = jnp.log(g)
    kf = (1.0 - lb) * _sigmoid(-cf)
    b = _dot_exact_lhs01(tmat, lg)
    e = _level_exponents(b, lg)
    blast = b[CHUNK - 1:CHUNK, :]
    return cq, cf, q, sg, g, kf, ci, e, b, blast


def _round_mx(x):
    if MXU_DTYPE != jnp.bfloat16:
        return x
    u = lax.bitcast_convert_type(x, jnp.uint32)
    u = (u + jnp.uint32(0x7FFF) + ((u >> 16) & jnp.uint32(1))) & jnp.uint32(0xFFFF0000)
    return lax.bitcast_convert_type(u, F32)


def _hgrn_scores(q, kf, e, masks, h, scores=True):
    qh, kh = q[:, _hs(h)], kf[:, _hs(h)]
    ql, kl = [], []
    a = None
    for lev in range(NLEV + 1):
        if lev < NLEV:
            eq = jnp.exp(e[(1 + lev) * CHUNK:(2 + lev) * CHUNK, _hs(h)])
            ek = jnp.exp(e[(1 + NLEV + lev) * CHUNK:(2 + NLEV + lev) * CHUNK, _hs(h)])
            ql.append((_round_mx(qh * eq), eq))
            kl.append((_round_mx(kh * ek), ek))
        else:
            ql.append((_round_mx(qh), None))
            kl.append((_round_mx(kh), None))
        if scores:
            term = masks[lev] * _dot_nt(ql[-1][0], kl[-1][0])
            a = term if a is None else a + term
    return a, ql, kl


def _hgrn_fwd(z, lb, gain, tmat, masks):
    S = z.shape[0]
    ts = _tile(S, ROW_TILE)
    nc = ts // CHUNK

    def body(zc_ref, lb_ref, gn_ref, tm_ref, mk_ref, o_ref, st_ref, a_ref, y_ref, state):
        @pl.when(pl.program_id(0) == 0)
        def _():
            state[...] = jnp.zeros_like(state)

        lbv, gn, tm, mk = lb_ref[...], gn_ref[...], tm_ref[...], mk_ref[...]

        def chunk(c, carry):
            r0 = pl.multiple_of(c * CHUNK, CHUNK)
            zc = zc_ref[pl.ds(r0, CHUNK), :]
            cq, cf, q, sg, g, kf, v, e, b, blast = _hgrn_chunk_fwd(zc, lbv, tm)
            qe = q * jnp.exp(b)
            kd = kf * jnp.exp(blast - b)
            st_ref[pl.ds(r0, CHUNK), :] = state[...]
            for h in range(NH):
                sth = state[:, _hs(h)]
                a = _mx(_hgrn_scores(q, kf, e, mk, h)[0])
                a_ref[pl.ds(r0, CHUNK), _hs(h)] = a
                oh = _dot_nt(qe[:, _hs(h)], sth) + _dot(a, v[:, _hs(h)])
                state[:, _hs(h)] = sth * jnp.exp(blast[:, _hs(h)]) + _dot_tn(v[:, _hs(h)], kd[:, _hs(h)])
                o_ref[pl.ds(r0, CHUNK), _hs(h)] = oh
                yn = _rms_fwd(oh, gn[:, _hs(h)])
                y_ref[pl.ds(r0, CHUNK), _hs(h)] = _mx(yn * _silu(zc[:, 3 * BW + h * DH:3 * BW + (h + 1) * DH]))
            return carry

        lax.fori_loop(0, nc, chunk, 0, unroll=2)

    row = pl.BlockSpec((ts, BW), lambda i: (i, 0))
    one = pl.BlockSpec((1, BW), lambda i: (0, 0))
    return _call(
        body, name="hgrn_fwd", grid=(S // ts,),
        in_specs=[pl.BlockSpec((ts, 1024), lambda i: (i, ZC // 1024)), one, one,
                  pl.BlockSpec(tmat.shape, lambda i: (0, 0)), pl.BlockSpec(masks.shape, lambda i: (0, 0, 0))],
        out_specs=[row, row, row, row],
        out_shape=[jax.ShapeDtypeStruct((S, BW), F32), jax.ShapeDtypeStruct((S, BW), F32),
                   jax.ShapeDtypeStruct((S, BW), MXU_DTYPE), jax.ShapeDtypeStruct((S, BW), MXU_DTYPE)],
        scratch=[pltpu.VMEM((CHUNK, BW), F32)], sem=("arbitrary",))(z, lb, gain, tmat, masks)


def _hgrn_bwd(z, lb, gain, tmat, masks, o_pre, states, scores, dy, dz):
    S = z.shape[0]
    ts = _tile(S, ROW_TILE)
    nt = S // ts
    nc = ts // CHUNK

    def body(zc_ref, lb_ref, gn_ref, tm_ref, mk_ref, o_ref, st_ref, a_ref, dy_ref, dz_in, dz_ref, stat_ref, dstate):
        @pl.when(pl.program_id(0) == 0)
        def _():
            dstate[...] = jnp.zeros_like(dstate)
            stat_ref[...] = jnp.zeros_like(stat_ref)

        lbv, gn, tm, mk = lb_ref[...], gn_ref[...], tm_ref[...], mk_ref[...]
        upper = _tri(CHUNK, upper=True)
        lower_strict = 1.0 - upper

        def chunk(cc, carry):
            c = nc - 1 - cc
            r0 = pl.multiple_of(c * CHUNK, CHUNK)
            zc = zc_ref[pl.ds(r0, CHUNK), :]
            cq, cf, q, sg, g, kf, v, e, b, blast = _hgrn_chunk_fwd(zc, lbv, tm)
            eb = jnp.exp(b)
            ebl = jnp.exp(blast - b)
            qe = q * eb
            kd = kf * ebl
            o = o_ref[pl.ds(r0, CHUNK), :]
            dyv = dy_ref[pl.ds(r0, CHUNK), :]
            stp = st_ref[pl.ds(r0, CHUNK), :]
            cg = zc[:, 3 * BW:4 * BW]
            sgate = _silu(cg)
            dq_parts, dk_parts, dv_parts, dcg_parts = [], [], [], []
            dgain, up_parts, lo_parts, const_parts = [], [], [], []
            for h in range(NH):
                hs = _hs(h)
                oh = o[:, hs]
                r = lax.rsqrt(jnp.mean(oh * oh, axis=-1, keepdims=True) + EPS)
                nrm = oh * r
                dyn = dyv[:, hs] * sgate[:, hs]
                dcg_parts.append(dyv[:, hs] * nrm * gn[:, hs] * _dsilu(cg[:, hs]))
                dgain.append(jnp.sum(dyn * nrm, axis=0, keepdims=True))
                tt = dyn * gn[:, hs]
                doh = r * (tt - nrm * jnp.mean(tt * nrm, axis=-1, keepdims=True))
                _, ql, kl = _hgrn_scores(q, kf, e, mk, h, scores=False)
                a = a_ref[pl.ds(r0, CHUNK), hs]
                da = _dot_nt(doh, v[:, hs])
                dsth = dstate[:, hs]
                ebh = jnp.exp(blast[:, hs])
                dv_parts.append(_dot_tn(a, doh) + _dot_nt(kd[:, hs], dsth))
                dq_inter = eb[:, hs] * _dot(doh, stp[:, hs])
                dk_state = ebl[:, hs] * _dot(v[:, hs], dsth)
                dqh, dkh, gh = dq_inter, dk_state, None
                for lev in range(NLEV + 1):
                    dal = mk[lev] * da
                    xq = _dot(dal, kl[lev][0])
                    yk = _dot_tn(dal, ql[lev][0])
                    gterm = ql[lev][0] * xq - kl[lev][0] * yk
                    gh = gterm if gh is None else gh + gterm
                    dqh = dqh + (xq if lev == NLEV else ql[lev][1] * xq)
                    dkh = dkh + (yk if lev == NLEV else kl[lev][1] * yk)
                up_parts.append(gh + q[:, hs] * dq_inter)
                lo_parts.append(kf[:, hs] * dk_state)
                const_parts.append(jnp.sum(dsth * stp[:, hs], axis=0, keepdims=True) * ebh)
                dstate[:, hs] = dsth * ebh + _dot_tn(doh, qe[:, hs])
                dq_parts.append(dqh)
                dk_parts.append(dkh)
            dq = jnp.concatenate(dq_parts, axis=1)
            dk = jnp.concatenate(dk_parts, axis=1)
            dlg = (_dot_exact_lhs01(upper, jnp.concatenate(up_parts, axis=1))
                   + _dot_exact_lhs01(lower_strict, jnp.concatenate(lo_parts, axis=1))
                   + jnp.concatenate(const_parts, axis=1))
            dsg = sg * (1.0 - sg)
            dz_ref[pl.ds(r0, CHUNK), 0:BW] = _mx(dq * _dsilu(cq))
            dz_ref[pl.ds(r0, CHUNK), BW:2 * BW] = _mx((dlg / g - dk) * (1.0 - lbv) * dsg)
            dz_ref[pl.ds(r0, CHUNK), 2 * BW:3 * BW] = _mx(jnp.concatenate(dv_parts, axis=1))
            dz_ref[pl.ds(r0, CHUNK), 3 * BW:4 * BW] = _mx(jnp.concatenate(dcg_parts, axis=1))
            stat_ref[0:1, :] += jnp.concatenate(dgain, axis=1)
            stat_ref[1:2, :] += jnp.sum((dlg / g - dk) * (1.0 - sg), axis=0, keepdims=True)
            return carry

        lax.fori_loop(0, nc, chunk, 0, unroll=2)

    rev = lambda w, c=0: pl.BlockSpec((ts, w), lambda i: (nt - 1 - i, c))
    one = pl.BlockSpec((1, BW), lambda i: (0, 0))
    return _call(
        body, name="hgrn_bwd", grid=(nt,),
        in_specs=[rev(1024, ZC // 1024), one, one, pl.BlockSpec(tmat.shape, lambda i: (0, 0)),
                  pl.BlockSpec(masks.shape, lambda i: (0, 0, 0)), rev(BW), rev(BW), rev(BW), rev(BW, 2), _ANY],
        out_specs=[rev(1024, ZC // 1024), pl.BlockSpec((8, BW), lambda i: (0, 0))],
        out_shape=[jax.ShapeDtypeStruct((S, ZCOLS), MXU_DTYPE), jax.ShapeDtypeStruct((8, BW), F32)],
        scratch=[pltpu.VMEM((CHUNK, BW), F32)],
        sem=("arbitrary",), aliases={9: 0})(z, lb, gain, tmat, masks, o_pre, states, scores, dy, dz)


def _lower_bounds_fwd(lb_logits):
    def body(l_ref, o_ref):
        l = l_ref[...]
        m = jnp.max(l, axis=0, keepdims=True)
        ex = jnp.exp(l - m)
        p = ex / jnp.sum(ex, axis=0, keepdims=True)
        cs = p[0:1, :]
        o_ref[0:1, :] = jnp.clip(cs - p[0:1, :], 0.0, 1.0)
        for d in range(1, DEPTH):
            cs = cs + p[d:d + 1, :]
            o_ref[d:d + 1, :] = jnp.clip(cs - p[0:1, :], 0.0, 1.0)

    full = pl.BlockSpec((DEPTH, BW), lambda: (0, 0))
    return _call(body, name="lower_bounds_fwd", grid=(), in_specs=[full], out_specs=full,
                 out_shape=jax.ShapeDtypeStruct((DEPTH, BW), F32))(lb_logits)


def _lower_bounds_bwd(lb_logits, dlow):
    def body(l_ref, d_ref, o_ref):
        l = l_ref[...]
        m = jnp.max(l, axis=0, keepdims=True)
        ex = jnp.exp(l - m)
        p = ex / jnp.sum(ex, axis=0, keepdims=True)
        dl = d_ref[...]
        cs = p[0:1, :]
        dcs = []
        for d in range(DEPTH):
            if d > 0:
                cs = cs + p[d:d + 1, :]
            val = cs - p[0:1, :]
            dcs.append(jnp.where((val > 0.0) & (val < 1.0), dl[d:d + 1, :], 0.0))
        total = dcs[0] + dcs[1] + dcs[2] + dcs[3]
        dp = []
        for j in range(DEPTH):
            s = dcs[j]
            for d in range(j + 1, DEPTH):
                s = s + dcs[d]
            dp.append(s - total if j == 0 else s)
        inner = p[0:1, :] * dp[0]
        for j in range(1, DEPTH):
            inner = inner + p[j:j + 1, :] * dp[j]
        for j in range(DEPTH):
            o_ref[j:j + 1, :] = p[j:j + 1, :] * (dp[j] - inner)

    full = pl.BlockSpec((DEPTH, BW), lambda: (0, 0))
    return _call(body, name="lower_bounds_bwd", grid=(), in_specs=[full, full], out_specs=full,
                 out_shape=jax.ShapeDtypeStruct((DEPTH, BW), F32))(lb_logits, dlow)


def _sgu_fwd(z, gv, ws, bs):
    S = z.shape[0]
    ts = _tile(S, ROW_TILE)
    nc = ts // SCHUNK

    def body(zd_ref, gv_ref, ws_ref, bs_ref, y_ref):
        gvv, bsv = gv_ref[...], bs_ref[...]
        tril = _tri(SCHUNK)
        for c in range(nc):
            rs = slice(c * SCHUNK, (c + 1) * SCHUNK)
            zd = zd_ref[rs, :]
            for h in range(NH):
                vn = _rms_fwd(zd[:, BW + h * DH:BW + (h + 1) * DH], gvv[:, _hs(h)])
                s = _dot(ws_ref[h] * tril, vn) + bsv[:, h:h + 1]
                y_ref[rs, _hs(h)] = _mx(zd[:, _hs(h)] * s * _silu(zd[:, 2 * BW + h * DH:2 * BW + (h + 1) * DH]))

    return _call(
        body, name="sgu_fwd", grid=(S // ts,),
        in_specs=[pl.BlockSpec((ts, 1024), lambda i: (i, ZD // 1024)), pl.BlockSpec((1, BW), lambda i: (0, 0)),
                  pl.BlockSpec((NH, SCHUNK, SCHUNK), lambda i: (0, 0, 0)), pl.BlockSpec((SCHUNK, 128), lambda i: (0, 0))],
        out_specs=pl.BlockSpec((ts, BW), lambda i: (i, 0)),
        out_shape=jax.ShapeDtypeStruct((S, BW), MXU_DTYPE), sem=("parallel",))(z, gv, ws, bs)


def _sgu_bwd(z, gv, ws, bs, dy, dz):
    S = z.shape[0]
    ts = _tile(S, ROW_TILE)
    nc = ts // SCHUNK

    def body(zd_ref, gv_ref, ws_ref, bs_ref, dy_ref, dz_in, dz_ref, dws_ref, dbs_ref, st_ref):
        @pl.when(pl.program_id(0) == 0)
        def _():
            dws_ref[...] = jnp.zeros_like(dws_ref)
            dbs_ref[...] = jnp.zeros_like(dbs_ref)
            st_ref[...] = jnp.zeros_like(st_ref)

        gvv, bsv = gv_ref[...], bs_ref[...]
        tril = _tri(SCHUNK)
        dz_ref[:, 3 * BW:4 * BW] = jnp.zeros((ts, BW), MXU_DTYPE)
        for c in range(nc):
            rs = slice(c * SCHUNK, (c + 1) * SCHUNK)
            zd = zd_ref[rs, :]
            dyv = dy_ref[rs, :]
            for h in range(NH):
                hs = _hs(h)
                u = zd[:, hs]
                vraw = zd[:, BW + h * DH:BW + (h + 1) * DH]
                gt = zd[:, 2 * BW + h * DH:2 * BW + (h + 1) * DH]
                gvh = gvv[:, hs]
                vn = _rms_fwd(vraw, gvh)
                wm = ws_ref[h] * tril
                s = _dot(wm, vn) + bsv[:, h:h + 1]
                sil = _silu(gt)
                d = dyv[:, hs]
                ds = d * u * sil
                dz_ref[rs, hs] = _mx(d * s * sil)
                dz_ref[rs, 2 * BW + h * DH:2 * BW + (h + 1) * DH] = _mx(d * u * s * _dsilu(gt))
                dws_ref[h] += tril * _dot_nt(ds, vn)
                dbs_ref[:, h:h + 1] += jnp.sum(ds, axis=-1, keepdims=True)
                dvn = _dot_tn(wm, ds)
                dx, dgr = _rms_bwd(vraw, gvh, dvn)
                dz_ref[rs, BW + h * DH:BW + (h + 1) * DH] = _mx(dx)
                st_ref[0:1, hs] += jnp.sum(dgr, axis=0, keepdims=True)

    return _call(
        body, name="sgu_bwd", grid=(S // ts,),
        in_specs=[pl.BlockSpec((ts, 1024), lambda i: (i, ZD // 1024)), pl.BlockSpec((1, BW), lambda i: (0, 0)),
                  pl.BlockSpec((NH, SCHUNK, SCHUNK), lambda i: (0, 0, 0)), pl.BlockSpec((SCHUNK, 128), lambda i: (0, 0)),
                  pl.BlockSpec((ts, BW), lambda i: (i, 3)), _ANY],
        out_specs=[pl.BlockSpec((ts, 1024), lambda i: (i, ZD // 1024)), pl.BlockSpec((NH, SCHUNK, SCHUNK), lambda i: (0, 0, 0)),
                   pl.BlockSpec((SCHUNK, 128), lambda i: (0, 0)), pl.BlockSpec((8, BW), lambda i: (0, 0))],
        out_shape=[jax.ShapeDtypeStruct((S, ZCOLS), MXU_DTYPE), jax.ShapeDtypeStruct((NH, SCHUNK, SCHUNK), F32),
                   jax.ShapeDtypeStruct((SCHUNK, 128), F32), jax.ShapeDtypeStruct((8, BW), F32)],
        sem=("arbitrary",), aliases={5: 0})(z, gv, ws, bs, dy, dz)


def _merge_fwd(x, z, ys, wup, mb, wo):
    S = x.shape[0]
    ts = _tile(S, MERGE_TILE)

    def body(x_ref, zm_ref, ya_ref, yb_ref, yc_ref, yd_ref, wup_ref, mb_ref, wo_ref, x1_ref, mg_ref):
        yrefs = (ya_ref, yb_ref, yc_ref, yd_ref)
        mbv = mb_ref[...]
        merged = None
        for b in range(NBR):
            cs = slice(b * D_MODEL, (b + 1) * D_MODEL)
            term = _sigmoid(zm_ref[:, cs] + mbv[b:b + 1, :]) * jnp.dot(yrefs[b][...], wup_ref[b],
                                                                      preferred_element_type=F32)
            merged = term if merged is None else merged + term
        mg = _mx(merged)
        mg_ref[...] = mg
        x1_ref[...] = x_ref[...] + jnp.dot(mg, wo_ref[...], preferred_element_type=F32)

    row = lambda w: pl.BlockSpec((ts, w), lambda i: (i, 0))
    return _call(
        body, name="merge_fwd", grid=(S // ts,),
        in_specs=[row(D_MODEL), pl.BlockSpec((ts, 4096), lambda i: (i, 0)), row(BW), row(BW), row(BW), row(BW),
                  pl.BlockSpec((NBR, BW, D_MODEL), lambda i: (0, 0, 0)), pl.BlockSpec((NBR, D_MODEL), lambda i: (0, 0)),
                  pl.BlockSpec((D_MODEL, D_MODEL), lambda i: (0, 0))],
        out_specs=[row(D_MODEL), row(D_MODEL)],
        out_shape=[jax.ShapeDtypeStruct((S, D_MODEL), F32), jax.ShapeDtypeStruct((S, D_MODEL), MXU_DTYPE)],
        sem=("parallel",))(x, z, *ys, wup, mb, wo)


def _merge_bwd(dx1, z, ys, wup, mb, wo):
    S = dx1.shape[0]
    ts = _tile(S, MERGE_TILE)

    def body(dx_ref, zm_ref, ya_ref, yb_ref, yc_ref, yd_ref, wup_ref, mb_ref, wo_ref,
             dzm_ref, du_ref, dxb_ref, dy_ref, st_ref):
        @pl.when(pl.program_id(0) == 0)
        def _():
            st_ref[...] = jnp.zeros_like(st_ref)

        yrefs = (ya_ref, yb_ref, yc_ref, yd_ref)
        mbv = mb_ref[...]
        dxb = _mx(dx_ref[...])
        dxb_ref[...] = dxb
        dmerged = _dot_nt(dxb, wo_ref[...])
        for b in range(NBR):
            cs = slice(b * D_MODEL, (b + 1) * D_MODEL)
            u = jnp.dot(yrefs[b][...], wup_ref[b], preferred_element_type=F32)
            sg = _sigmoid(zm_ref[:, cs] + mbv[b:b + 1, :])
            du = _mx(dmerged * sg)
            du_ref[:, cs] = du
            dzm = dmerged * u * sg * (1.0 - sg)
            dzm_ref[:, cs] = _mx(dzm)
            st_ref[b:b + 1, :] += jnp.sum(dzm, axis=0, keepdims=True)
            dy_ref[:, b * BW:(b + 1) * BW] = _dot_nt(du, wup_ref[b])

    row = lambda w: pl.BlockSpec((ts, w), lambda i: (i, 0))
    return _call(
        body, name="merge_bwd", grid=(S // ts,),
        in_specs=[row(D_MODEL), pl.BlockSpec((ts, 4096), lambda i: (i, 0)), row(BW), row(BW), row(BW), row(BW),
                  pl.BlockSpec((NBR, BW, D_MODEL), lambda i: (0, 0, 0)),
                  pl.BlockSpec((NBR, D_MODEL), lambda i: (0, 0)), pl.BlockSpec((D_MODEL, D_MODEL), lambda i: (0, 0))],
        out_specs=[row(4096), row(4096), row(D_MODEL), row(D_MODEL), pl.BlockSpec((8, D_MODEL), lambda i: (0, 0))],
        out_shape=[jax.ShapeDtypeStruct((S, ZCOLS), MXU_DTYPE), jax.ShapeDtypeStruct((S, 4096), MXU_DTYPE),
                   jax.ShapeDtypeStruct((S, D_MODEL), MXU_DTYPE), jax.ShapeDtypeStruct((S, D_MODEL), F32),
                   jax.ShapeDtypeStruct((8, D_MODEL), F32)],
        sem=("arbitrary",))(dx1, z, *ys, wup, mb, wo)


def _ple_fwd(x1, p, g, wg, wp):
    S = x1.shape[0]
    ts = _tile(S, ROW_TILE)

    def body(x_ref, p_ref, g_ref, wg_ref, wp_ref, o_ref):
        x = x_ref[...]
        hp = _mx(_rms_fwd(x, g_ref[...]))
        gate = _sigmoid(jnp.dot(hp, wg_ref[...], preferred_element_type=F32))
        pp = jnp.dot(_mx(p_ref[...]), wp_ref[...], preferred_element_type=F32)
        o_ref[...] = x + gate * pp

    row = lambda w: pl.BlockSpec((ts, w), lambda i: (i, 0))
    return _call(
        body, name="ple_fwd", grid=(S // ts,),
        in_specs=[row(D_MODEL), row(PLE), pl.BlockSpec((1, D_MODEL), lambda i: (0, 0)),
                  pl.BlockSpec((D_MODEL, D_MODEL), lambda i: (0, 0)), pl.BlockSpec((PLE, D_MODEL), lambda i: (0, 0))],
        out_specs=row(D_MODEL), out_shape=jax.ShapeDtypeStruct((S, D_MODEL), F32),
        sem=("parallel",))(x1, p, g, wg, wp)


def _ple_bwd(x1, p, dx2, g, wg, wp):
    S = x1.shape[0]
    ts = _tile(S, ROW_TILE)

    def body(x_ref, p_ref, dx_ref, g_ref, wg_ref, wp_ref, dx1_ref, hp_ref, dgl_ref, dpp_ref, pb_ref, st_ref):
        @pl.when(pl.program_id(0) == 0)
        def _():
            st_ref[...] = jnp.zeros_like(st_ref)

        x, gv, dx2 = x_ref[...], g_ref[...], dx_ref[...]
        hp = _mx(_rms_fwd(x, gv))
        hp_ref[...] = hp
        gate = _sigmoid(jnp.dot(hp, wg_ref[...], preferred_element_type=F32))
        pb = _mx(p_ref[...])
        pb_ref[...] = pb
        pp = jnp.dot(pb, wp_ref[...], preferred_element_type=F32)
        dgl = _mx(dx2 * pp * gate * (1.0 - gate))
        dgl_ref[...] = dgl
        dpp_ref[...] = _mx(dx2 * gate)
        dhp = _dot_nt(dgl, wg_ref[...])
        dxn, dgr = _rms_bwd(x, gv, dhp)
        dx1_ref[...] = dx2 + dxn
        st_ref[0:1, :] += jnp.sum(dgr, axis=0, keepdims=True)

    row = lambda w: pl.BlockSpec((ts, w), lambda i: (i, 0))
    sq = pl.BlockSpec((D_MODEL, D_MODEL), lambda i: (0, 0))
    return _call(
        body, name="ple_bwd", grid=(S // ts,),
        in_specs=[row(D_MODEL), row(PLE), row(D_MODEL), pl.BlockSpec((1, D_MODEL), lambda i: (0, 0)), sq,
                  pl.BlockSpec((PLE, D_MODEL), lambda i: (0, 0))],
        out_specs=[row(D_MODEL), row(D_MODEL), row(D_MODEL), row(D_MODEL), row(PLE),
                   pl.BlockSpec((8, D_MODEL), lambda i: (0, 0))],
        out_shape=[jax.ShapeDtypeStruct((S, D_MODEL), F32)] + [jax.ShapeDtypeStruct((S, D_MODEL), MXU_DTYPE)] * 3
        + [jax.ShapeDtypeStruct((S, PLE), MXU_DTYPE), jax.ShapeDtypeStruct((8, D_MODEL), F32)],
        sem=("arbitrary",))(x1, p, dx2, g, wg, wp)


def _pad_rows(a, rows=8):
    return jnp.concatenate([a, jnp.zeros((rows - a.shape[0],) + a.shape[1:], a.dtype)], axis=0)


def _pad_lanes(a, lanes=128):
    return jnp.concatenate([a, jnp.zeros(a.shape[:-1] + (lanes - a.shape[-1],), a.dtype)], axis=-1)


def _wz_from_w_in(w):
    zeros = lambda n: jnp.zeros((w.shape[0], n), w.dtype)
    return jnp.concatenate([w[:, _OM:_OEND], w[:, _OA:_OB], w[:, _OB:_OF], w[:, _OC:_OD], w[:, _OD:_OM], zeros(256),
                            w[:, _OF:_OC], zeros(124)], axis=1)


def _w_in_from_wz(g):
    return jnp.concatenate([g[:, ZA:ZB], g[:, ZB:ZC], g[:, ZF:ZF + 4], g[:, ZC:ZD], g[:, ZD:ZD + 768], g[:, ZM:ZA]], axis=1)


_W_IN_GROUPS = [(_OA, _OB, ZA), (_OB, _OF, ZB), (_OF, _OC, ZF), (_OC, _OD, ZC), (_OD, _OM, ZD), (_OM, _OEND, ZM)]


def _wz_from_shards(g):
    n = g.shape[-1]
    pieces, pos = [], 0
    for a, b, zs in sorted(_W_IN_GROUPS, key=lambda grp: grp[2]):
        if zs > pos:
            pieces.append(jnp.zeros((g.shape[1], zs - pos), g.dtype))
        for k in range(4):
            lo, hi = max(a, k * n), min(b, (k + 1) * n)
            if lo < hi:
                pieces.append(g[k][:, lo - k * n:hi - k * n])
        pos = zs + (b - a)
    pieces.append(jnp.zeros((g.shape[1], ZCOLS - pos), g.dtype))
    return jnp.concatenate(pieces, axis=1)


def _w_in_slabs_from_wz(g):
    n = _OEND // 4
    slabs = []
    for k in range(4):
        pieces = []
        for a, b, zs in _W_IN_GROUPS:
            lo, hi = max(a, k * n), min(b, (k + 1) * n)
            if lo < hi:
                pieces.append(g[:, zs + lo - a:zs + hi - a])
        slabs.append(jnp.concatenate(pieces, axis=1))
    return jnp.stack(slabs)


def _local_step(x, p, target, wts, dist=None):
    tmat_np, masks_np = _hgrn_consts()
    tmat = jnp.asarray(tmat_np, MXU_DTYPE)
    masks = jnp.asarray(masks_np, F32)
    lower = _lower_bounds_fwd(wts["lb_logits"])
    saved = []
    for li in range(DEPTH):
        big = dist.weights(li) if dist else {n: wts[n][li] for n in _GATHERED}
        wz = _wz_from_shards(big["w_in"]) if big["w_in"].ndim == 3 else _wz_from_w_in(big["w_in"])
        g_mix = wts["norm_mix"][li][None, :]
        h = _norm_fwd(x, g_mix)
        z = _mm_nn(h, wz, F32, name="mm_z", tn=1664)
        cw = _pad_rows(jnp.concatenate([big["conv_w"], wts["conv_b"][li][None, :]], axis=0))
        ya = _conv_fwd(z, cw)
        fb = _pad_lanes(wts["fgate_bias"][li][None, :])
        gq, gk = wts["q_norm"][li][None, :], wts["k_norm"][li][None, :]
        qa, ka, va, cc = _attn_prep_fwd(z, fb, gq, gk)
        o, lse, yb, got = _flash_fwd(qa, ka, va, z, comm=dist.fwd_comm(li) if dist else None)
        if dist:
            dist.fwd_done(li, got)
        lb = lower[li][None, :]
        gh = wts["hgrn_norm"][li][None, :]
        o_pre, states, hscores, yc = _hgrn_fwd(z, lb, gh, tmat, masks)
        gv = wts["sgu_norm"][li][None, :]
        ws = wts["spatial_w"][li]
        bs = _pad_lanes(wts["spatial_b"][li].T)
        yd = _sgu_fwd(z, gv, ws, bs)
        ys = (ya, yb, yc, yd)
        x1, merged = _merge_fwd(x, z, ys, big["w_up"], big["merge_b"], big["w_o"])
        g_ple = wts["norm_ple"][li][None, :]
        x2 = _ple_fwd(x1, p[li], g_ple, big["w_ple_gate"], big["w_ple_proj"])
        saved.append(dict(x=x, h=h, z=z, wz=wz, cw=cw, fb=fb, gq=gq, gk=gk, qa=qa, ka=ka, va=va, cc=cc, o=o, lse=lse,
                          lb=lb, gh=gh, o_pre=o_pre, states=states, hscores=hscores, gv=gv, ws=ws, bs=bs, ys=ys, x1=x1, merged=merged,
                          g_mix=g_mix, g_ple=g_ple, big=big))
        x = x2

    loss, dx = _loss_fwd_bwd(x, target)

    names = ["norm_mix", "w_in", "conv_w", "conv_b", "fgate_bias", "q_norm", "k_norm", "lb", "hgrn_norm", "sgu_norm",
             "spatial_w", "spatial_b", "w_up", "merge_b", "w_o", "norm_ple", "w_ple_gate", "w_ple_proj"]
    gl = {n: [None] * DEPTH for n in names}
    for li in reversed(range(DEPTH)):
        s = saved[li]
        z, big = s["z"], s["big"]
        wg, wp = big["w_ple_gate"], big["w_ple_proj"]
        dx1, hp, dgl, dpp, pb, st = _ple_bwd(s["x1"], p[li], dx, s["g_ple"], wg, wp)
        gl["norm_ple"][li] = st[0]
        gl["w_ple_gate"][li] = _mm_tn(hp, dgl, name="mm_dwg")
        gl["w_ple_proj"][li] = _mm_tn(pb, dpp, name="mm_dwp")
        wup, wo = big["w_up"], big["w_o"]
        dz, du, dxb, dy, st = _merge_bwd(dx1, z, s["ys"], wup, big["merge_b"], wo)
        gl["merge_b"][li] = st[0:NBR]
        gl["w_o"][li] = _mm_tn(s["merged"], dxb, name="mm_dwo")
        gl["w_up"][li] = jnp.stack([_mm_tn(s["ys"][b], du, name="mm_dwup", ycol=b, n=D_MODEL) for b in range(NBR)])
        dz, st = _conv_bwd(z, s["cw"], dy, dz)
        gl["conv_w"][li] = st[0:3]
        gl["conv_b"][li] = st[3]
        dgate, qb, doa = _attn_gate_bwd(dy, s["o"], z, s["qa"], s["cc"], s["lse"])
        dq, doa, got = _flash_bwd_dq(qb, s["ka"], s["va"], doa, comm=dist.bwd_comm() if dist else None)
        if dist:
            dist.bwd_done(got)
        dk, dv, dcc = _flash_bwd_dkv(qb, s["ka"], s["va"], doa)
        dz, dzf, st = _attn_prep_bwd(z, s["fb"], s["gq"], s["gk"], dq, dk, dv, dgate, dcc, dz)
        dz = _put_fgate_cols(dzf, dz)
        gl["q_norm"][li] = st[0, 0:DH]
        gl["k_norm"][li] = st[1, 0:DH]
        gl["fgate_bias"][li] = st[2, 0:NH]
        dz, st = _hgrn_bwd(z, s["lb"], s["gh"], tmat, masks, s["o_pre"], s["states"], s["hscores"], dy, dz)
        gl["hgrn_norm"][li] = st[0]
        gl["lb"][li] = st[1]
        dz, dws, dbs, st = _sgu_bwd(z, s["gv"], s["ws"], s["bs"], dy, dz)
        gl["sgu_norm"][li] = st[0]
        gl["spatial_w"][li] = dws
        gl["spatial_b"][li] = dbs[:, 0:NH].T
        dwz = _mm_tn(s["h"], dz, name="mm_dwz", tn=1664)
        gl["w_in"][li] = dwz if dist else _w_in_from_wz(dwz)
        if dist:
            dist.push(li, {n: gl[n][li] for n in _BIG})
        if dist and li == 0:
            dh, got = _mm_nn(dz, s["wz"], F32, name="mm_dh", ts=ROW_TILE // 2, b_transposed=True, comm=dist.bwd_comm())
            dist.bwd_done(got)
        else:
            dh = _mm_nn(dz, s["wz"], F32, name="mm_dh", ts=ROW_TILE // 2, b_transposed=True)
        dx, st = _norm_bwd(s["x"], s["g_mix"], dh, dx1)
        gl["norm_mix"][li] = st[0]

    if dist:
        dist.finish()
    grads = {n: jnp.stack(v) for n, v in gl.items() if not (dist and n in _BIG)}
    grads["lb_logits"] = _lower_bounds_bwd(wts["lb_logits"], grads.pop("lb"))
    return loss, dx, grads


def _my_pos():
    return lax.axis_index("x"), lax.axis_index("y"), lax.axis_index("c")


def _gather_phase(phase, ins, outs, sems):
    if phase == 1:
        return
    send, recv, lsem = sems
    x, y, c = _my_pos()
    me = 2 * x + y
    peers = [(1 - x, y), (x, 1 - y), (1 - x, 1 - y)]
    copies = []
    for t in range(len(ins)):
        copies.append(pltpu.make_async_copy(ins[t], outs[t].at[me], lsem.at[t]))
        for j, (px, py) in enumerate(peers):
            copies.append(pltpu.make_async_remote_copy(
                src_ref=ins[t], dst_ref=outs[t].at[me], send_sem=send.at[t, j], recv_sem=recv.at[t, j],
                device_id=(px, py, c), device_id_type=MESH))
    for cp in copies:
        if phase == 0:
            cp.start()
        else:
            cp.wait()


N_HALVED = 5


def _gather_halves_phase(phase, ins, outs, sems):
    send, recv, lsem = sems
    x, y, c = _my_pos()
    me = 2 * x + y
    sib = (x, y, 1 - c)
    chips = [(1 - x, y), (x, 1 - y), (1 - x, 1 - y)]

    def rc(t, k, src, dst, dev):
        return pltpu.make_async_remote_copy(src_ref=src, dst_ref=dst, send_sem=send.at[t, k], recv_sem=recv.at[t, k],
                                            device_id=dev, device_id_type=MESH)

    for t in range(len(ins)):
        local = pltpu.make_async_copy(ins[t], outs[t].at[me], lsem.at[t])
        if t >= N_HALVED:
            whole = [rc(t, j, ins[t], outs[t].at[me], (px, py, c)) for j, (px, py) in enumerate(chips)]
            for cp in [local] + whole:
                if phase == 0:
                    cp.start()
                elif phase == 2:
                    cp.wait()
            continue
        hr = ins[t].shape[0] // 2
        mine, other = pl.ds(c * hr, hr), pl.ds((1 - c) * hr, hr)
        first = [rc(t, j, ins[t].at[mine], outs[t].at[me, mine], (px, py, c)) for j, (px, py) in enumerate(chips)]
        landed = [outs[t].at[2 * px + py, mine] for px, py in chips]
        passed = [rc(t, 3 + j, slot, slot, sib) for j, slot in enumerate(landed)]
        if phase == 0:
            local.start()
            for cp in first:
                cp.start()
        elif phase == 1:
            for j, slot in enumerate(landed):
                rc(t, j, slot, slot, (x, y, c)).wait_recv()
                passed[j].start()
        else:
            for j, (px, py) in enumerate(chips):
                slot = outs[t].at[2 * px + py, other]
                rc(t, 3 + j, slot, slot, (x, y, c)).wait_recv()
            for cp in first + passed:
                cp.wait_send()
            local.wait()


def _exchange_phase(phase, ins, outs, sems):
    send, recv, lsem = sems
    x, y, c = _my_pos()
    me = 2 * x + y
    sib = (x, y, 1 - c)
    chips = [(1 - x, y), (x, 1 - y), (1 - x, 1 - y)]

    def rc(t, k, src, dst, dev):
        return pltpu.make_async_remote_copy(src_ref=src, dst_ref=dst, send_sem=send.at[t, k], recv_sem=recv.at[t, k],
                                            device_id=dev, device_id_type=MESH)

    for t in range(len(ins)):
        local = pltpu.make_async_copy(ins[t].at[me], outs[t].at[2 * me + c], lsem.at[t])
        first = [rc(t, 0, ins[t].at[me], outs[t].at[2 * me + c], sib)]
        first += [rc(t, 1 + j, ins[t].at[2 * px + py], outs[t].at[2 * me + c], (px, py, c))
                  for j, (px, py) in enumerate(chips)]
        landed = [outs[t].at[2 * (2 * px + py) + c] for px, py in chips]
        passed = [rc(t, 4 + j, slot, slot, sib) for j, slot in enumerate(landed)]
        if phase == 0:
            local.start()
            for cp in first:
                cp.start()
        elif phase == 1:
            for j, slot in enumerate(landed):
                rc(t, 1 + j, slot, slot, (x, y, c)).wait_recv()
                passed[j].start()
        else:
            s0 = outs[t].at[2 * me + (1 - c)]
            rc(t, 0, s0, s0, (x, y, c)).wait_recv()
            for j, (px, py) in enumerate(chips):
                slot = outs[t].at[2 * (2 * px + py) + (1 - c)]
                rc(t, 4 + j, slot, slot, (x, y, c)).wait_recv()
            for cp in first + passed:
                cp.wait_send()
            local.wait()


_COMM_PHASES = {"gather": _gather_phase, "gather_halves": _gather_halves_phase, "exchange": _exchange_phase}
_gathered_shapes = lambda arrays: [jax.ShapeDtypeStruct((4,) + a.shape, a.dtype) for a in arrays]
_COMM_OUT = {"gather": _gathered_shapes, "gather_halves": _gathered_shapes,
             "exchange": lambda arrays: [jax.ShapeDtypeStruct((8,) + a.shape[1:], a.dtype) for a in arrays]}


def _comm_sems(kind, nc):
    k = {"gather": 3, "gather_halves": 6, "exchange": 7}[kind]
    return [pltpu.SemaphoreType.DMA((nc, k)), pltpu.SemaphoreType.DMA((nc, k)), pltpu.SemaphoreType.DMA((nc,))]


def _comm_alone(kind, arrays, name):
    nc = len(arrays)

    def body(*refs):
        for phase in range(3):
            _COMM_PHASES[kind](phase, refs[:nc], refs[nc:2 * nc], refs[2 * nc:])

    hbm = pl.BlockSpec(memory_space=pl.ANY)
    return pl.pallas_call(
        functools.partial(body), name=name, in_specs=[hbm] * nc, out_specs=[hbm] * nc,
        out_shape=_COMM_OUT[kind](arrays), scratch_shapes=_comm_sems(kind, nc))(*arrays)


def _allreduce_small(v):
    R = v.shape[0]

    def body(v_ref, o_ref, pair, chips, send, recv):
        x, y, c = _my_pos()
        me = 2 * x + y
        pair[c] = v_ref[...]
        swap = pltpu.make_async_remote_copy(src_ref=v_ref, dst_ref=pair.at[c], send_sem=send.at[0], recv_sem=recv.at[0],
                                            device_id=(x, y, 1 - c), device_id_type=MESH)
        swap.start()
        swap.wait()
        chips[me] = pair[0] + pair[1]
        copies = [pltpu.make_async_remote_copy(src_ref=chips.at[me], dst_ref=chips.at[me], send_sem=send.at[1 + j],
                                               recv_sem=recv.at[1 + j], device_id=(px, py, c), device_id_type=MESH)
                  for j, (px, py) in enumerate([(1 - x, y), (x, 1 - y), (1 - x, 1 - y)])]
        for cp in copies:
            cp.start()
        for cp in copies:
            cp.wait()
        o_ref[...] = (chips[0] + chips[1]) + (chips[2] + chips[3])

    vm = pl.BlockSpec(memory_space=pltpu.VMEM)
    return pl.pallas_call(
        functools.partial(body), name="allreduce_small", in_specs=[vm], out_specs=vm,
        out_shape=jax.ShapeDtypeStruct((R, 128), F32),
        scratch_shapes=[pltpu.VMEM((2, R, 128), F32), pltpu.VMEM((4, R, 128), F32),
                        pltpu.SemaphoreType.DMA((4,)), pltpu.SemaphoreType.DMA((4,))],
        compiler_params=pltpu.CompilerParams(vmem_limit_bytes=VMEM_LIMIT))(v)


def _adamw(w, m, v, parts, nparts):
    A, R, C = w.shape
    per_layer = isinstance(parts, (list, tuple))
    tr = _tile(R, 64) if per_layer else (R if R <= 128 else _tile(R, 128))
    nparr = len(parts) if per_layer else 1

    def body(*refs):
        w_ref, m_ref, v_ref = refs[:3]
        p_refs = refs[3:3 + nparr]
        g_ref, d_ref, nm_ref, nv_ref = refs[3 + nparr:]

        def update(read):
            g = read(0).astype(F32)
            for k in range(1, nparts):
                g = g + read(k).astype(F32)
            mm = ADAM_B1 * m_ref[0] + (1.0 - ADAM_B1) * g
            vv = ADAM_B2 * v_ref[0] + (1.0 - ADAM_B2) * jnp.square(g)
            m_hat = mm / (1.0 - ADAM_B1 ** ADAM_STEP)
            v_hat = vv / (1.0 - ADAM_B2 ** ADAM_STEP)
            g_ref[0] = g
            d_ref[0] = -ADAM_LR * (m_hat / (jnp.sqrt(v_hat) + ADAM_EPS) + ADAM_WD * w_ref[0])
            nm_ref[0] = mm
            nv_ref[0] = vv

        if not per_layer:
            update(lambda k: p_refs[0][k, 0])
        else:
            for a in range(A):
                @pl.when(pl.program_id(0) == a)
                def _(a=a):
                    update(lambda k: p_refs[a][k])

    blk = pl.BlockSpec((1, tr, C), lambda a, r: (a, r, 0))
    if per_layer:
        pspecs = [pl.BlockSpec((nparts, tr, C), functools.partial(lambda a, r, l: (0, jnp.where(a == l, r, 0), 0), l=l))
                  for l in range(A)]
        pargs = list(parts)
    else:
        pspecs = [pl.BlockSpec((nparts, 1, tr, C), lambda a, r: (0, a, r, 0))]
        pargs = [parts]
    return _call(
        body, name="adamw", grid=(A, R // tr), in_specs=[blk, blk, blk] + pspecs,
        out_specs=[blk] * 4, out_shape=[jax.ShapeDtypeStruct((A, R, C), F32)] * 4,
        sem=("arbitrary", "arbitrary"))(w, m, v, *pargs)


def _rows128(a):
    width = a.shape[-1]
    a2 = a.reshape(-1, width)
    k = -(-width // 128)
    if width % 128:
        a2 = jnp.pad(a2, ((0, 0), (0, k * 128 - width)))
    return a2.reshape(-1, 128)


def _from_rows128(r, shape):
    width = shape[-1]
    k = -(-width // 128)
    return r.reshape(-1, k * 128)[:, :width].reshape(shape)


def _as3d(a):
    if a.ndim == 2:
        return a[None]
    if a.ndim == 3:
        return a
    return a.reshape((-1,) + a.shape[-2:])


_WEIGHTS = ["norm_mix", "w_in", "conv_w", "conv_b", "fgate_bias", "q_norm", "k_norm", "lb_logits", "hgrn_norm", "sgu_norm",
            "spatial_w", "spatial_b", "w_up", "merge_b", "w_o", "norm_ple", "w_ple_gate", "w_ple_proj"]
_BIG = ["w_in", "w_up", "w_o", "w_ple_gate", "w_ple_proj"]
_GATHERED = _BIG + ["conv_w", "merge_b"]
_SHARD_AXIS = {"w_in": 2, "w_up": 3, "w_o": 1, "w_ple_gate": 1, "w_ple_proj": 2, "conv_w": 2, "merge_b": 2}
_SMALL = [n for n in _WEIGHTS if n not in _BIG]


class _Dist:
    def __init__(self, w):
        self.w = w
        self.full = {0: self._unpack(_comm_alone("gather_halves", self._shards(0), "gather_weights"))}
        self.contrib = {}
        self.pending = None

    def _shards(self, l):
        return [_mx(self.w[n][l]) for n in _BIG] + [self.w["conv_w"][l], self.w["merge_b"][l]]

    def _unpack(self, gathered):
        return {n: g if n == "w_in" else jnp.concatenate([g[k] for k in range(4)], axis=_SHARD_AXIS[n] - 1)
                for n, g in zip(_GATHERED, gathered)}

    def weights(self, l):
        return self.full[l]

    def fwd_comm(self, l):
        return ("gather", self._shards(l + 1)) if l + 1 < DEPTH else None

    def fwd_done(self, l, got):
        if got:
            self.full[l + 1] = self._unpack(got)

    def push(self, l, grads):
        self.pending = (l, [(_w_in_slabs_from_wz(grads[n]) if n == "w_in" else
                             jnp.stack(jnp.split(grads[n], 4, axis=_SHARD_AXIS[n] - 1))).astype(GRAD_WIRE_DTYPE)
                            for n in _BIG])

    def bwd_comm(self):
        return ("exchange", self.pending[1]) if self.pending else None

    def bwd_done(self, got):
        if got:
            self.contrib[self.pending[0]] = got
            self.pending = None

    def finish(self):
        if self.pending:
            self.contrib[self.pending[0]] = _comm_alone("exchange", self.pending[1], "exchange_grads")
            self.pending = None


def kernel(x, p, norm_mix, w_in, conv_w, conv_b, fgate_bias, q_norm, k_norm, lb_logits, hgrn_norm, sgu_norm, spatial_w, spatial_b, w_up, merge_b, w_o, norm_ple, w_ple_gate, w_ple_proj, loss_target, m_norm_mix, m_w_in, m_conv_w, m_conv_b, m_fgate_bias, m_q_norm, m_k_norm, m_lb_logits, m_hgrn_norm, m_sgu_norm, m_spatial_w, m_spatial_b, m_w_up, m_merge_b, m_w_o, m_norm_ple, m_w_ple_gate, m_w_ple_proj, v_norm_mix, v_w_in, v_conv_w, v_conv_b, v_fgate_bias, v_q_norm, v_k_norm, v_lb_logits, v_hgrn_norm, v_sgu_norm, v_spatial_w, v_spatial_b, v_w_up, v_merge_b, v_w_o, v_norm_ple, v_w_ple_gate, v_w_ple_proj):
    loc = dict(locals())
    w = {n: loc[n] for n in _WEIGHTS}
    m = {n: loc["m_" + n] for n in _WEIGHTS}
    v = {n: loc["v_" + n] for n in _WEIGHTS}
    chip = 2 * lax.axis_index("x") + lax.axis_index("y")

    dist = _Dist(w)
    loss_blk, dx, grads = _local_step(x[0], p[:, 0], loss_target[0], w, dist)
    loss = lax.psum(loss_blk[0, 0], ("x", "y", "c"))

    blocks = [_rows128(grads[n]) for n in _SMALL]
    nrows = sum(b.shape[0] for b in blocks)
    packed = jnp.concatenate(blocks + [jnp.zeros(((-nrows) % 8, 128), F32)], axis=0)
    red = _allreduce_small(packed)
    small, off = {}, 0
    for n, b in zip(_SMALL, blocks):
        small[n] = _from_rows128(red[off:off + b.shape[0]], grads[n].shape)
        off += b.shape[0]
    for n in ("conv_w", "merge_b"):
        ax = _SHARD_AXIS[n]
        width = small[n].shape[ax] // 4
        small[n] = lax.dynamic_slice_in_dim(small[n], chip * width, width, axis=ax)

    out_g, out_d, out_m, out_v = {}, {}, {}, {}
    for n in _WEIGHTS:
        shp = w[n].shape
        if n in _BIG:
            w3, m3, v3 = (a.reshape((DEPTH, -1, shp[-1])) for a in (w[n], m[n], v[n]))
            parts = [dist.contrib[l][_BIG.index(n)].reshape((8,) + w3.shape[1:]) for l in range(DEPTH)]
            g, d, nm, nv = _adamw(w3, m3, v3, parts, 8)
        else:
            g, d, nm, nv = _adamw(_as3d(w[n]), _as3d(m[n]), _as3d(v[n]), _as3d(small[n])[None], 1)
        out_g[n], out_d[n], out_m[n], out_v[n] = (a.reshape(shp) for a in (g, d, nm, nv))

    return (loss, dx[None], *[out_g[n] for n in _WEIGHTS], *[out_d[n] for n in _WEIGHTS],
            *[out_m[n] for n in _WEIGHTS], *[out_v[n] for n in _WEIGHTS])
```

```python
import functools

import numpy as np
import jax
import jax.numpy as jnp
from jax import lax
from jax.experimental import pallas as pl
from jax.experimental.pallas import tpu as pltpu

F32 = jnp.float32
MXU_DTYPE = jnp.bfloat16

D_MODEL = 1024
BW = 256
NH = 4
DH = 64
DEPTH = 4
NBR = 4
PLE = 256
CHUNK = 64
SCHUNK = 128
EPS = 1e-6
MASK_VALUE = -1e30
NLEV = 6

ADAM_LR, ADAM_B1, ADAM_B2, ADAM_EPS, ADAM_WD, ADAM_STEP = 0.001, 0.9, 0.999, 1e-08, 0.01, 10

ZM, ZA, ZB, ZC, ZD, ZF = 0, 4096, 5120, 6144, 7168, 8192
ZCOLS = 8320
ZB_GATE = ZB + 3 * BW
_OA, _OB, _OF, _OC, _OD, _OM, _OEND = 0, 1024, 2048, 2052, 3076, 3844, 7940

VMEM_LIMIT = 56 * 1024 * 1024
ROW_TILE = 512
FLASH_TILE = 1024
GRAD_WIRE_DTYPE = jnp.bfloat16
MERGE_TILE = 512
MESH = pl.DeviceIdType.MESH
_ANY = pl.BlockSpec(memory_space=pl.ANY)


def _tile(n, pref):
    t = min(n, pref)
    assert n % t == 0, (n, t)
    return t


def _call(body, *, name, grid, in_specs, out_specs, out_shape, scratch=(), sem=None, aliases=None):
    return pl.pallas_call(
        functools.partial(body), name=name, grid=grid, in_specs=in_specs, out_specs=out_specs,
        out_shape=out_shape, scratch_shapes=list(scratch), input_output_aliases=aliases or {},
        compiler_params=pltpu.CompilerParams(dimension_semantics=sem, vmem_limit_bytes=VMEM_LIMIT))


def _mx(x):
    return x.astype(MXU_DTYPE)


def _dot(a, b):
    return jnp.dot(_mx(a), _mx(b), preferred_element_type=F32)


def _dot_nt(a, b):
    return lax.dot_general(_mx(a), _mx(b), (((1,), (1,)), ((), ())), preferred_element_type=F32)


def _dot_tn(a, b):
    return lax.dot_general(_mx(a), _mx(b), (((0,), (0,)), ((), ())), preferred_element_type=F32)


def _top16(x):
    u = lax.bitcast_convert_type(x, jnp.uint32) & jnp.uint32(0xFFFF0000)
    return lax.bitcast_convert_type(u, F32)


def _split3(x):
    hi = _top16(x)
    r1 = x - hi
    mid = _top16(r1)
    return _mx(hi), _mx(mid), _mx(r1 - mid)


def _dot_exact_lhs01(t, x):
    hi, mid, lo = _split3(x)
    t = _mx(t)
    return (jnp.dot(t, hi, preferred_element_type=F32) + jnp.dot(t, mid, preferred_element_type=F32)
            + jnp.dot(t, lo, preferred_element_type=F32))


def _sigmoid(x):
    return jax.nn.sigmoid(x)


def _silu(x):
    return x * _sigmoid(x)


def _dsilu(x):
    s = _sigmoid(x)
    return s * (1.0 + x * (1.0 - s))


def _log_sigmoid(x):
    return jnp.minimum(x, 0.0) - jnp.log(1.0 + jnp.exp(-jnp.abs(x)))


def _rms_fwd(x, g):
    r = lax.rsqrt(jnp.mean(x * x, axis=-1, keepdims=True) + EPS)
    return x * r * g


def _rms_bwd(x, g, dy):
    r = lax.rsqrt(jnp.mean(x * x, axis=-1, keepdims=True) + EPS)
    n = x * r
    t = dy * g
    dx = r * (t - n * jnp.mean(t * n, axis=-1, keepdims=True))
    return dx, dy * n


def _tri(n, upper=False):
    r = lax.broadcasted_iota(jnp.int32, (n, n), 0)
    c = lax.broadcasted_iota(jnp.int32, (n, n), 1)
    return jnp.where((c >= r) if upper else (r >= c), 1.0, 0.0).astype(F32)


def _hs(h):
    return slice(h * DH, (h + 1) * DH)


def _mm_nn(a, b, out_dtype, *, name, ts=None, tn=None, tk=None, b_transposed=False, comm=None):
    S, K = a.shape
    N = b.shape[0] if b_transposed else b.shape[1]
    ts = _tile(S, ts or ROW_TILE)
    tn = _tile(N, tn or N)
    tk = _tile(K, tk or K)
    nk = K // tk
    b_spec = (pl.BlockSpec((tn, tk), lambda j, i, k: (j, k)) if b_transposed
              else pl.BlockSpec((tk, tn), lambda j, i, k: (k, j)))

    def body(a_ref, b_ref, o_ref, acc_ref):
        k = pl.program_id(2)
        part = _dot_nt(a_ref[...], b_ref[...]) if b_transposed else jnp.dot(a_ref[...], b_ref[...],
                                                                          preferred_element_type=F32)
        if nk == 1:
            o_ref[...] = part.astype(o_ref.dtype)
        else:
            @pl.when(k == 0)
            def _():
                acc_ref[...] = part

            @pl.when(k > 0)
            def _():
                acc_ref[...] += part

            @pl.when(k == nk - 1)
            def _():
                o_ref[...] = acc_ref[...].astype(o_ref.dtype)

    grid = (N // tn, S // ts, nk)

    def when():
        at = [pl.program_id(d) for d in range(3)]
        first = (at[0] == 0) & (at[1] == 0) & (at[2] == 0)
        last = (at[0] == grid[0] - 1) & (at[1] == grid[1] - 1) & (at[2] == grid[2] - 1)
        return first, last, last

    xin, xout, xshape, xsem = _comm_specs(comm)
    res = _call(
        _fuse_comm(body, 2, 1, comm, when), name=name, grid=grid,
        in_specs=[pl.BlockSpec((ts, tk), lambda j, i, k: (i, k)), b_spec] + xin,
        out_specs=[pl.BlockSpec((ts, tn), lambda j, i, k: (i, j))] + xout,
        out_shape=[jax.ShapeDtypeStruct((S, N), out_dtype)] + xshape,
        scratch=[pltpu.VMEM((ts, tn) if nk > 1 else (8, 128), F32)] + xsem,
        sem=("arbitrary", "arbitrary", "arbitrary") if comm else ("parallel", "parallel", "arbitrary"),
        )(a, b, *(comm[1] if comm else []))
    return (res[0], list(res[1:])) if comm else res[0]


def _mm_tn(x, y, *, name, tn=None, ycol=0, n=None):
    S, M = x.shape
    n = n or y.shape[1]
    ts = _tile(S, 4 * ROW_TILE)
    tn = _tile(n, tn or n)
    nj = n // tn

    def body(x_ref, y_ref, o_ref):
        @pl.when(pl.program_id(1) == 0)
        def _():
            o_ref[...] = jnp.zeros_like(o_ref)

        o_ref[...] += lax.dot_general(x_ref[...], y_ref[...], (((0,), (0,)), ((), ())), preferred_element_type=F32)

    return _call(
        body, name=name, grid=(nj, S // ts),
        in_specs=[pl.BlockSpec((ts, M), lambda j, s: (s, 0)), pl.BlockSpec((ts, tn), lambda j, s: (s, ycol * nj + j))],
        out_specs=pl.BlockSpec((M, tn), lambda j, s: (0, j)),
        out_shape=jax.ShapeDtypeStruct((M, n), F32),
        sem=("parallel", "arbitrary"))(x, y)


def _norm_fwd(x, g):
    S = x.shape[0]
    ts = _tile(S, ROW_TILE)

    def body(x_ref, g_ref, h_ref):
        h_ref[...] = _mx(_rms_fwd(x_ref[...], g_ref[...]))

    return _call(
        body, name="norm_fwd", grid=(S // ts,),
        in_specs=[pl.BlockSpec((ts, D_MODEL), lambda i: (i, 0)), pl.BlockSpec((1, D_MODEL), lambda i: (0, 0))],
        out_specs=pl.BlockSpec((ts, D_MODEL), lambda i: (i, 0)),
        out_shape=jax.ShapeDtypeStruct((S, D_MODEL), MXU_DTYPE), sem=("parallel",))(x, g)


def _norm_bwd(x, g, dh, dres):
    S = x.shape[0]
    ts = _tile(S, ROW_TILE)

    def body(x_ref, g_ref, dh_ref, dr_ref, dx_ref, st_ref):
        @pl.when(pl.program_id(0) == 0)
        def _():
            st_ref[...] = jnp.zeros_like(st_ref)

        dx, dgr = _rms_bwd(x_ref[...], g_ref[...], dh_ref[...])
        dx_ref[...] = dr_ref[...] + dx
        st_ref[0:1, :] += jnp.sum(dgr, axis=0, keepdims=True)

    row = pl.BlockSpec((ts, D_MODEL), lambda i: (i, 0))
    return _call(
        body, name="norm_bwd", grid=(S // ts,),
        in_specs=[row, pl.BlockSpec((1, D_MODEL), lambda i: (0, 0)), row, row],
        out_specs=[row, pl.BlockSpec((8, D_MODEL), lambda i: (0, 0))],
        out_shape=[jax.ShapeDtypeStruct((S, D_MODEL), F32), jax.ShapeDtypeStruct((8, D_MODEL), F32)],
        sem=("arbitrary",))(x, g, dh, dres)


def _loss_fwd_bwd(y, target):
    S = y.shape[0]
    ts = _tile(S, ROW_TILE)

    def body(y_ref, t_ref, l_ref, dy_ref):
        @pl.when(pl.program_id(0) == 0)
        def _():
            l_ref[...] = jnp.zeros_like(l_ref)

        err = y_ref[...] - t_ref[...]
        dy_ref[...] = err * (1.0 / D_MODEL)
        rowloss = jnp.mean(err * err, axis=-1, keepdims=True)
        l_ref[...] += 0.5 * jnp.sum(rowloss, axis=0, keepdims=True)

    row = pl.BlockSpec((ts, D_MODEL), lambda i: (i, 0))
    return _call(
        body, name="loss", grid=(S // ts,), in_specs=[row, row],
        out_specs=[pl.BlockSpec((8, 128), lambda i: (0, 0)), row],
        out_shape=[jax.ShapeDtypeStruct((8, 128), F32), jax.ShapeDtypeStruct((S, D_MODEL), F32)],
        sem=("arbitrary",))(y, target)


def _shift_down(x, k, halo, rows):
    y = pltpu.roll(x, k, 0)
    for j in range(k):
        y = jnp.where(rows == j, halo[8 - k + j:8 - k + j + 1, :], y)
    return y


def _shift_up(x, k, halo, rows, n):
    y = pltpu.roll(x, n - k, 0)
    for j in range(k):
        y = jnp.where(rows == n - k + j, halo[j:j + 1, :], y)
    return y


def _conv_parts(za, zh, cw, first, rows):
    ax, ab, ac, ag = za[:, 0:BW], za[:, BW:2 * BW], za[:, 2 * BW:3 * BW], za[:, 3 * BW:4 * BW]
    zz = ac * ax
    hz = jnp.where(first, 0.0, zh[:, 2 * BW:3 * BW] * zh[:, 0:BW])
    zz1 = _shift_down(zz, 1, hz, rows)
    zz2 = _shift_down(zz, 2, hz, rows)
    conv = zz2 * cw[0:1, :] + zz1 * cw[1:2, :] + zz * cw[2:3, :] + cw[3:4, :]
    return ax, ab, ac, ag, zz, zz1, zz2, conv


def _conv_fwd(z, cw):
    S = z.shape[0]
    ts = _tile(S, ROW_TILE)
    hb = ts // 8

    def body(za_ref, zh_ref, cw_ref, y_ref):
        i = pl.program_id(0)
        rows = lax.broadcasted_iota(jnp.int32, (ts, BW), 0)
        ax, ab, ac, ag, zz, zz1, zz2, conv = _conv_parts(za_ref[...], zh_ref[...], cw_ref[...], i == 0, rows)
        y_ref[...] = _mx(ab * conv * _silu(ag))

    return _call(
        body, name="conv_fwd", grid=(S // ts,),
        in_specs=[pl.BlockSpec((ts, 1024), lambda i: (i, ZA // 1024)),
                  pl.BlockSpec((8, 1024), lambda i: (jnp.maximum(i * hb - 1, 0), ZA // 1024)),
                  pl.BlockSpec((8, BW), lambda i: (0, 0))],
        out_specs=pl.BlockSpec((ts, BW), lambda i: (i, 0)),
        out_shape=jax.ShapeDtypeStruct((S, BW), MXU_DTYPE), sem=("parallel",))(z, z, cw)


def _conv_bwd(z, cw, dy, dz):
    S = z.shape[0]
    ts = _tile(S, ROW_TILE)
    hb = ts // 8
    nt = S // ts

    def body(za_ref, zh_ref, zn_ref, cw_ref, dy_ref, dyn_ref, dz_in, dz_ref, st_ref):
        i = pl.program_id(0)

        @pl.when(i == 0)
        def _():
            st_ref[...] = jnp.zeros_like(st_ref)

        cw = cw_ref[...]
        rows = lax.broadcasted_iota(jnp.int32, (ts, BW), 0)
        ax, ab, ac, ag, zz, zz1, zz2, conv = _conv_parts(za_ref[...], zh_ref[...], cw, i == 0, rows)
        dy = dy_ref[...]
        sg = _silu(ag)
        dc = dy * ab * sg
        zn = zn_ref[...]
        dcn = jnp.where(i == nt - 1, 0.0, dyn_ref[...] * zn[:, BW:2 * BW] * _silu(zn[:, 3 * BW:4 * BW]))
        dc1 = _shift_up(dc, 1, dcn, rows, ts)
        dc2 = _shift_up(dc, 2, dcn, rows, ts)
        dzz = dc * cw[2:3, :] + dc1 * cw[1:2, :] + dc2 * cw[0:1, :]
        dz_ref[:, 0:BW] = _mx(dzz * ac)
        dz_ref[:, BW:2 * BW] = _mx(dy * conv * sg)
        dz_ref[:, 2 * BW:3 * BW] = _mx(dzz * ax)
        dz_ref[:, 3 * BW:4 * BW] = _mx(dy * ab * conv * _dsilu(ag))
        st_ref[0:1, :] += jnp.sum(dc * zz2, axis=0, keepdims=True)
        st_ref[1:2, :] += jnp.sum(dc * zz1, axis=0, keepdims=True)
        st_ref[2:3, :] += jnp.sum(dc * zz, axis=0, keepdims=True)
        st_ref[3:4, :] += jnp.sum(dc, axis=0, keepdims=True)

    return _call(
        body, name="conv_bwd", grid=(nt,),
        in_specs=[pl.BlockSpec((ts, 1024), lambda i: (i, ZA // 1024)),
                  pl.BlockSpec((8, 1024), lambda i: (jnp.maximum(i * hb - 1, 0), ZA // 1024)),
                  pl.BlockSpec((8, 1024), lambda i: (jnp.minimum((i + 1) * hb, S // 8 - 1), ZA // 1024)),
                  pl.BlockSpec((8, BW), lambda i: (0, 0)),
                  pl.BlockSpec((ts, BW), lambda i: (i, 0)),
                  pl.BlockSpec((8, BW), lambda i: (jnp.minimum((i + 1) * hb, S // 8 - 1), 0)), _ANY],
        out_specs=[pl.BlockSpec((ts, 1024), lambda i: (i, ZA // 1024)), pl.BlockSpec((8, BW), lambda i: (0, 0))],
        out_shape=[jax.ShapeDtypeStruct((S, ZCOLS), MXU_DTYPE), jax.ShapeDtypeStruct((8, BW), F32)],
        sem=("arbitrary",), aliases={6: 0})(z, z, z, cw, dy, dy, dz)


AW = 128
_AUG = DH


def _split3_f32(x):
    hi = _top16(x)
    r1 = x - hi
    mid = _top16(r1)
    return hi, mid, r1 - mid


def _aug_lanes(n, cols):
    lane = lax.broadcasted_iota(jnp.int32, (n, DH), 1)
    out = jnp.zeros((n, DH), F32)
    for e, c in enumerate(cols):
        out = jnp.where(lane == e, c, out)
    return out


def _put_aug(ref, h, col, parts=None, const=None):
    base = h * AW + _AUG + col
    if parts is None:
        ref[:, base:base + 3] = jnp.full((ref.shape[0], 3), const, ref.dtype)
    else:
        for e, part in enumerate(parts):
            ref[:, base + e:base + e + 1] = part.astype(ref.dtype)


def _attn_prep_fwd(z, fb, gq, gk):
    S = z.shape[0]
    ts = _tile(S, ROW_TILE)

    def body(zb_ref, zf_ref, fb_ref, gq_ref, gk_ref, qa_ref, ka_ref, va_ref, cc_ref, carry):
        @pl.when(pl.program_id(0) == 0)
        def _():
            carry[...] = jnp.zeros_like(carry)

        zb = zb_ref[...]
        gqv, gkv = gq_ref[...], gk_ref[...]
        lf = _log_sigmoid(zf_ref[...] + fb_ref[...])
        cum = _dot_exact_lhs01(_tri(ts), lf) + carry[...]
        carry[...] = cum[ts - 1:ts, :]
        cc_ref[...] = cum
        pieces = _split3_f32(cum)
        for h in range(NH):
            ph = [pc[:, h:h + 1] for pc in pieces]
            qh = _rms_fwd(zb[:, _hs(h)], gqv) * (DH ** -0.5)
            kh = _rms_fwd(zb[:, BW + h * DH:BW + (h + 1) * DH], gkv)
            vh = zb[:, 2 * BW + h * DH:2 * BW + (h + 1) * DH]
            qa_ref[:, _ah(h)] = _mx(jnp.concatenate([qh, _aug_lanes(ts, ph + [1.0] * 3)], axis=1))
            ka_ref[:, _ah(h)] = _mx(jnp.concatenate([kh, _aug_lanes(ts, [1.0] * 3 + [-x for x in ph])], axis=1))
            va_ref[:, _ah(h)] = _mx(jnp.concatenate([vh, _aug_lanes(ts, [-1.0] * 3)], axis=1))

    row = lambda w: pl.BlockSpec((ts, w), lambda i: (i, 0))
    return _call(
        body, name="attn_prep_fwd", grid=(S // ts,),
        in_specs=[pl.BlockSpec((ts, 1024), lambda i: (i, ZB // 1024)), pl.BlockSpec((ts, 128), lambda i: (i, ZF // 128)),
                  pl.BlockSpec((1, 128), lambda i: (0, 0)), pl.BlockSpec((1, DH), lambda i: (0, 0)),
                  pl.BlockSpec((1, DH), lambda i: (0, 0))],
        out_specs=[row(NH * AW), row(NH * AW), row(NH * AW), row(128)],
        out_shape=[jax.ShapeDtypeStruct((S, NH * AW), MXU_DTYPE)] * 3 + [jax.ShapeDtypeStruct((S, 128), F32)],
        scratch=[pltpu.VMEM((1, 128), F32)], sem=("arbitrary",))(z, z, fb, gq, gk)


ROW_CHUNK = 256


def _square_steps(n):
    def when():
        i, j = pl.program_id(0), pl.program_id(1)
        return (i == 0) & (j == 0), (i == n - 1) & (j == 0), (i == n - 1) & (j == n - 1)
    return when


def _fuse_comm(core, n_in, n_out, comm, when):
    nc = 0 if comm is None else len(comm[1])

    def body(*refs):
        cin, xin = refs[:n_in], refs[n_in:n_in + nc]
        a = n_in + nc
        cout, xout = refs[a:a + n_out], refs[a + n_out:a + n_out + nc]
        rest = refs[a + n_out + nc:]
        if nc == 0:
            core(*cin, *cout, *rest)
            return
        cscr, sems = rest[:-3], rest[-3:]
        first, middle, last = when()
        phase = _COMM_PHASES[comm[0]]

        @pl.when(first)
        def _():
            phase(0, xin, xout, sems)

        core(*cin, *cout, *cscr)

        @pl.when(middle)
        def _():
            phase(1, xin, xout, sems)

        @pl.when(last)
        def _():
            phase(2, xin, xout, sems)

    return body


def _comm_specs(comm):
    if comm is None:
        return [], [], [], []
    hbm = pl.BlockSpec(memory_space=pl.ANY)
    nc = len(comm[1])
    return [hbm] * nc, [hbm] * nc, _COMM_OUT[comm[0]](comm[1]), _comm_sems(comm[0], nc)


def _ah(h):
    return slice(h * AW, (h + 1) * AW)


def _ahd(h):
    return slice(h * AW, h * AW + DH)


def _causal(shape, row0, transposed=False):
    r = row0 + lax.broadcasted_iota(jnp.int32, shape, 0)
    c = lax.broadcasted_iota(jnp.int32, shape, 1)
    return (r <= c) if transposed else (r >= c)


def _flash_fwd(qa, ka, va, z, comm=None):
    S = qa.shape[0]
    t = _tile(S, FLASH_TILE)
    n = S // t
    rch = _tile(t, ROW_CHUNK)

    def core(q_ref, k_ref, v_ref, zb_ref, o_ref, lse_ref, y_ref, m_sc, l_sc, acc):
        i, j = pl.program_id(0), pl.program_id(1)

        @pl.when(j == 0)
        def _():
            m_sc[...] = jnp.full_like(m_sc, MASK_VALUE)
            l_sc[...] = jnp.zeros_like(l_sc)
            acc[...] = jnp.zeros_like(acc)

        def block(masked):
            for h in range(NH):
                for rc in range(t // rch):
                    rows = slice(rc * rch, (rc + 1) * rch)
                    s = _dot_nt(q_ref[rows, _ah(h)], k_ref[:, _ah(h)])
                    if masked:
                        s = jnp.where(_causal(s.shape, rc * rch), s, MASK_VALUE)
                    m_old = m_sc[h, rows, :]
                    m_new = jnp.maximum(m_old, jnp.max(s, axis=-1, keepdims=True))
                    p = jnp.exp(s - m_new)
                    alpha = jnp.exp(m_old - m_new)
                    l_sc[h, rows, :] = alpha * l_sc[h, rows, :] + jnp.sum(p, axis=-1, keepdims=True)
                    acc[rows, _hs(h)] = alpha * acc[rows, _hs(h)] + _dot(p, v_ref[:, _ahd(h)])
                    m_sc[h, rows, :] = m_new

        @pl.when(j < i)
        def _():
            block(False)

        @pl.when(j == i)
        def _():
            block(True)
            lse_ref[...] = jnp.zeros_like(lse_ref)
            for h in range(NH):
                o_ref[:, _hs(h)] = acc[:, _hs(h)] / l_sc[h]
                lse_ref[:, h:h + 1] = m_sc[h] + jnp.log(l_sc[h])
            y_ref[...] = _mx(o_ref[...] * _silu(zb_ref[...]))

    qspec = lambda w: pl.BlockSpec((t, w), lambda i, j: (i, 0))
    kspec = lambda w: pl.BlockSpec((t, w), lambda i, j: (jnp.minimum(j, i), 0))
    xin, xout, xshape, xsem = _comm_specs(comm)
    res = _call(
        _fuse_comm(core, 4, 3, comm, _square_steps(n)), name="flash_fwd", grid=(n, n),
        in_specs=[qspec(NH * AW), kspec(NH * AW), kspec(NH * AW),
                  pl.BlockSpec((t, BW), lambda i, j: (i, ZB_GATE // BW))] + xin,
        out_specs=[qspec(BW), qspec(128), qspec(BW)] + xout,
        out_shape=[jax.ShapeDtypeStruct((S, BW), F32), jax.ShapeDtypeStruct((S, 128), F32),
                   jax.ShapeDtypeStruct((S, BW), MXU_DTYPE)] + xshape,
        scratch=[pltpu.VMEM((NH, t, 1), F32), pltpu.VMEM((NH, t, 1), F32), pltpu.VMEM((t, BW), F32)] + xsem,
        sem=("arbitrary", "arbitrary"))(qa, ka, va, z, *(comm[1] if comm else []))
    return res[0], res[1], res[2], list(res[3:])


def _attn_gate_bwd(dy, o, z, qa, cc, lse):
    S = dy.shape[0]
    ts = _tile(S, ROW_TILE)

    def body(dy_ref, o_ref, zb_ref, qa_ref, cc_ref, lse_ref, dg_ref, qb_ref, doa_ref):
        g = zb_ref[...]
        dy, o = dy_ref[...], o_ref[...]
        do = dy * _silu(g)
        dg_ref[...] = _mx(dy * o * _dsilu(g))
        qb_ref[...] = qa_ref[...]
        doa_ref[...] = jnp.zeros_like(doa_ref)
        shifted = _split3_f32(cc_ref[...] - lse_ref[...])
        for h in range(NH):
            doh = do[:, _hs(h)]
            doa_ref[:, _ahd(h)] = _mx(doh)
            delta = jnp.sum(doh * o[:, _hs(h)], axis=-1, keepdims=True)
            _put_aug(doa_ref, h, 0, parts=_split3_f32(delta))
            _put_aug(qb_ref, h, 0, parts=[pc[:, h:h + 1] for pc in shifted])

    row = lambda w: pl.BlockSpec((ts, w), lambda i: (i, 0))
    return _call(
        body, name="attn_gate_bwd", grid=(S // ts,),
        in_specs=[pl.BlockSpec((ts, BW), lambda i: (i, 1)), row(BW), pl.BlockSpec((ts, BW), lambda i: (i, ZB_GATE // BW)),
                  row(NH * AW), row(128), row(128)],
        out_specs=[row(BW), row(NH * AW), row(NH * AW)],
        out_shape=[jax.ShapeDtypeStruct((S, BW), MXU_DTYPE), jax.ShapeDtypeStruct((S, NH * AW), MXU_DTYPE),
                   jax.ShapeDtypeStruct((S, NH * AW), MXU_DTYPE)],
        sem=("parallel",))(dy, o, z, qa, cc, lse)


def _aug_value(ref, h, rows=slice(None)):
    base = h * AW + _AUG
    x = ref[rows, base:base + 3].astype(F32)
    return x[:, 0:1] + x[:, 1:2] + x[:, 2:3]


def _flash_bwd_dq(qb, ka, va, doa, comm=None):
    S = qb.shape[0]
    t = _tile(S, FLASH_TILE)
    n = S // t
    rch = _tile(t, ROW_CHUNK)

    def core(q_ref, k_ref, v_ref, do_ref, dq_ref, do2_ref, dr_sc):
        i, j = pl.program_id(0), pl.program_id(1)

        @pl.when(j == 0)
        def _():
            dq_ref[...] = jnp.zeros_like(dq_ref)
            dr_sc[...] = jnp.zeros_like(dr_sc)

        def block(masked):
            for h in range(NH):
                for rc in range(t // rch):
                    rows = slice(rc * rch, (rc + 1) * rch)
                    p = jnp.exp(_dot_nt(q_ref[rows, _ah(h)], k_ref[:, _ah(h)]))
                    if masked:
                        p = jnp.where(_causal(p.shape, rc * rch), p, 0.0)
                    ds = p * _dot_nt(do_ref[rows, _ah(h)], v_ref[:, _ah(h)])
                    dq_ref[rows, _hs(h)] += _dot(ds, k_ref[:, _ahd(h)])
                    dr_sc[h, rows, :] += jnp.sum(ds, axis=-1, keepdims=True)

        @pl.when(j < i)
        def _():
            block(False)

        @pl.when(j == i)
        def _():
            block(True)
            do2_ref[...] = do_ref[...]
            for h in range(NH):
                _put_aug(do2_ref, h, 0, parts=_split3_f32(_aug_value(do_ref, h) + dr_sc[h]))

    qspec = lambda w: pl.BlockSpec((t, w), lambda i, j: (i, 0))
    kspec = lambda w: pl.BlockSpec((t, w), lambda i, j: (jnp.minimum(j, i), 0))
    xin, xout, xshape, xsem = _comm_specs(comm)
    res = _call(
        _fuse_comm(core, 4, 2, comm, _square_steps(n)), name="flash_bwd_dq", grid=(n, n),
        in_specs=[qspec(NH * AW), kspec(NH * AW), kspec(NH * AW), qspec(NH * AW)] + xin,
        out_specs=[qspec(BW), qspec(NH * AW)] + xout,
        out_shape=[jax.ShapeDtypeStruct((S, BW), F32), jax.ShapeDtypeStruct((S, NH * AW), MXU_DTYPE)] + xshape,
        scratch=[pltpu.VMEM((NH, t, 1), F32)] + xsem,
        sem=("arbitrary", "arbitrary"))(qb, ka, va, doa, *(comm[1] if comm else []))
    return res[0], res[1], list(res[2:])


def _flash_bwd_dkv(qb, ka, va, doa):
    S = qb.shape[0]
    t = _tile(S, FLASH_TILE)
    n = S // t

    def body(q_ref, k_ref, v_ref, do_ref, dk_ref, dv_ref, dc_ref):
        j, i = pl.program_id(0), pl.program_id(1)

        @pl.when(i == 0)
        def _():
            dk_ref[...] = jnp.zeros_like(dk_ref)
            dv_ref[...] = jnp.zeros_like(dv_ref)
            dc_ref[...] = jnp.zeros_like(dc_ref)

        def block(masked):
            for h in range(NH):
                pt = jnp.exp(_dot_nt(k_ref[:, _ah(h)], q_ref[:, _ah(h)]))
                if masked:
                    pt = jnp.where(_causal(pt.shape, 0, transposed=True), pt, 0.0)
                dst = pt * _dot_nt(v_ref[:, _ah(h)], do_ref[:, _ah(h)])
                dv_ref[:, _hs(h)] += _dot(pt, do_ref[:, _ahd(h)])
                dk_ref[:, _hs(h)] += _dot(dst, q_ref[:, _ahd(h)])
                dc_ref[:, h:h + 1] += -jnp.sum(dst, axis=-1, keepdims=True)

        @pl.when(i > j)
        def _():
            block(False)

        @pl.when(i == j)
        def _():
            block(True)

    qspec = lambda w: pl.BlockSpec((t, w), lambda j, i: (jnp.maximum(i, j), 0))
    kspec = lambda w: pl.BlockSpec((t, w), lambda j, i: (j, 0))
    return _call(
        body, name="flash_bwd_dkv", grid=(n, n),
        in_specs=[qspec(NH * AW), kspec(NH * AW), kspec(NH * AW), qspec(NH * AW)],
        out_specs=[kspec(BW), kspec(BW), kspec(128)],
        out_shape=[jax.ShapeDtypeStruct((S, BW), F32), jax.ShapeDtypeStruct((S, BW), F32),
                   jax.ShapeDtypeStruct((S, 128), F32)],
        sem=("parallel", "arbitrary"))(qb, ka, va, doa)


def _attn_prep_bwd(z, fb, gq, gk, dq, dk, dv, dgate, dcc, dz):
    S = z.shape[0]
    ts = _tile(S, ROW_TILE)
    nt = S // ts

    def body(zb_ref, zf_ref, fb_ref, gq_ref, gk_ref, dq_ref, dk_ref, dv_ref, dg_ref, dcc_ref, dz_in, dzb_ref, dzf_ref,
             st_ref, carry):
        @pl.when(pl.program_id(0) == 0)
        def _():
            carry[...] = jnp.zeros_like(carry)
            st_ref[...] = jnp.zeros_like(st_ref)

        zb = zb_ref[...]
        gqv, gkv = gq_ref[...], gk_ref[...]
        dqv, dkv = dq_ref[...], dk_ref[...]
        sq = jnp.zeros((1, DH), F32)
        sk = jnp.zeros((1, DH), F32)
        for h in range(NH):
            dx, dgr = _rms_bwd(zb[:, _hs(h)], gqv, dqv[:, _hs(h)] * (DH ** -0.5))
            dzb_ref[:, _hs(h)] = _mx(dx)
            sq = sq + jnp.sum(dgr, axis=0, keepdims=True)
            ks = slice(BW + h * DH, BW + (h + 1) * DH)
            dx, dgr = _rms_bwd(zb[:, ks], gkv, dkv[:, _hs(h)])
            dzb_ref[:, ks] = _mx(dx)
            sk = sk + jnp.sum(dgr, axis=0, keepdims=True)
        dzb_ref[:, 2 * BW:3 * BW] = _mx(dv_ref[...])
        dzb_ref[:, 3 * BW:4 * BW] = dg_ref[...]
        dc = dcc_ref[...]
        dlf = _dot_exact_lhs01(_tri(ts, upper=True), dc) + carry[...]
        carry[...] = dlf[0:1, :]
        dfz = dlf * _sigmoid(-(zf_ref[...] + fb_ref[...]))
        dzf_ref[...] = _mx(dfz)
        st_ref[0:1, 0:DH] += sq
        st_ref[1:2, 0:DH] += sk
        st_ref[2:3, :] += jnp.sum(dfz, axis=0, keepdims=True)

    rev = lambda w, c=0: pl.BlockSpec((ts, w), lambda i: (nt - 1 - i, c))
    one = lambda w: pl.BlockSpec((1, w), lambda i: (0, 0))
    return _call(
        body, name="attn_prep_bwd", grid=(nt,),
        in_specs=[rev(1024, ZB // 1024), rev(128, ZF // 128), one(128), one(DH), one(DH),
                  rev(BW), rev(BW), rev(BW), rev(BW), rev(128), _ANY],
        out_specs=[rev(1024, ZB // 1024), rev(128), pl.BlockSpec((8, 128), lambda i: (0, 0))],
        out_shape=[jax.ShapeDtypeStruct((S, ZCOLS), MXU_DTYPE), jax.ShapeDtypeStruct((S, 128), MXU_DTYPE),
                   jax.ShapeDtypeStruct((8, 128), F32)],
        scratch=[pltpu.VMEM((1, 128), F32)], sem=("arbitrary",), aliases={10: 0})(z, z, fb, gq, gk, dq, dk, dv, dgate, dcc, dz)


def _put_fgate_cols(dzf, dz):
    S = dzf.shape[0]
    ts = _tile(S, ROW_TILE)

    def body(f_ref, dz_in, o_ref):
        o_ref[...] = f_ref[...]

    return _call(
        body, name="put_fgate_cols", grid=(S // ts,),
        in_specs=[pl.BlockSpec((ts, 128), lambda i: (i, 0)), _ANY],
        out_specs=pl.BlockSpec((ts, 128), lambda i: (i, ZF // 128)),
        out_shape=jax.ShapeDtypeStruct((S, ZCOLS), MXU_DTYPE), sem=("parallel",), aliases={1: 0})(dzf, dz)


def _hgrn_consts():
    C = CHUNK
    t = np.arange(C)[:, None]
    j = np.arange(C)[None, :]
    masks = []
    for lev in range(NLEV):
        m = C >> (lev + 1)
        blk, pos = t // (2 * m), t % (2 * m)
        sblk, spos = j // (2 * m), j % (2 * m)
        masks.append((blk == sblk) & (pos >= m) & (spos < m))
    masks.append(t == j)
    return (j <= t).astype(np.float32), np.stack(masks).astype(np.float32)


def _level_exponents(b, lg):
    row = lax.broadcasted_iota(jnp.int32, (CHUNK, 1), 0)
    eqs, eks = [], []
    for lev in range(NLEV):
        m = CHUNK >> (lev + 1)
        pos = jnp.bitwise_and(row, 2 * m - 1)
        if 2 * m >= 8:
            b3 = b.reshape(CHUNK // (2 * m), 2 * m, b.shape[1])
            mid = jnp.broadcast_to(b3[:, m - 1:m, :], b3.shape).reshape(b.shape)
            eqs.append(jnp.where(pos >= m, b - mid, 0.0))
            eks.append(jnp.where(pos < m, mid - b, 0.0))
        elif m == 2:
            eqs.append(jnp.where(pos == 2, lg, jnp.where(pos == 3, lg + pltpu.roll(lg, 1, 0), 0.0)))
            eks.append(jnp.where(pos == 0, pltpu.roll(lg, CHUNK - 1, 0), 0.0))
        else:
            eqs.append(jnp.where(pos == 1, lg, 0.0))
            eks.append(jnp.zeros_like(lg))
    return jnp.concatenate([b] + eqs + eks, axis=0)


def _hgrn_chunk_fwd(zc, lb, tmat):
    cq, cf, ci = zc[:, 0:BW], zc[:, BW:2 * BW], zc[:, 2 * BW:3 * BW]
    q = _silu(cq)
    sg = _sigmoid(cf)
    g = lb + (1.0 - lb) * sg
    lg = jnp.log(g)
    kf = (1.0 - lb) * _sigmoid(-cf)
    b = _dot_exact_lhs01(tmat, lg)
    e = _level_exponents(b, lg)
    blast = b[CHUNK - 1:CHUNK, :]
    return cq, cf, q, sg, g, kf, ci, e, b, blast


def _round_mx(x):
    if MXU_DTYPE != jnp.bfloat16:
        return x
    u = lax.bitcast_convert_type(x, jnp.uint32)
    u = (u + jnp.uint32(0x7FFF) + ((u >> 16) & jnp.uint32(1))) & jnp.uint32(0xFFFF0000)
    return lax.bitcast_convert_type(u, F32)


def _hgrn_scores(q, kf, e, masks, h, scores=True):
    qh, kh = q[:, _hs(h)], kf[:, _hs(h)]
    ql, kl = [], []
    a = None
    for lev in range(NLEV + 1):
        if lev < NLEV:
            eq = jnp.exp(e[(1 + lev) * CHUNK:(2 + lev) * CHUNK, _hs(h)])
            ek = jnp.exp(e[(1 + NLEV + lev) * CHUNK:(2 + NLEV + lev) * CHUNK, _hs(h)])
            ql.append((_round_mx(qh * eq), eq))
            kl.append((_round_mx(kh * ek), ek))
        else:
            ql.append((_round_mx(qh), None))
            kl.append((_round_mx(kh), None))
        if scores:
            term = masks[lev] * _dot_nt(ql[-1][0], kl[-1][0])
            a = term if a is None else a + term
    return a, ql, kl


def _hgrn_fwd(z, lb, gain, tmat, masks):
    S = z.shape[0]
    ts = _tile(S, ROW_TILE)
    nc = ts // CHUNK

    def body(zc_ref, lb_ref, gn_ref, tm_ref, mk_ref, o_ref, st_ref, a_ref, y_ref, state):
        @pl.when(pl.program_id(0) == 0)
        def _():
            state[...] = jnp.zeros_like(state)

        lbv, gn, tm, mk = lb_ref[...], gn_ref[...], tm_ref[...], mk_ref[...]

        def chunk(c, carry):
            r0 = pl.multiple_of(c * CHUNK, CHUNK)
            zc = zc_ref[pl.ds(r0, CHUNK), :]
            cq, cf, q, sg, g, kf, v, e, b, blast = _hgrn_chunk_fwd(zc, lbv, tm)
            qe = q * jnp.exp(b)
            kd = kf * jnp.exp(blast - b)
            st_ref[pl.ds(r0, CHUNK), :] = state[...]
            for h in range(NH):
                sth = state[:, _hs(h)]
                a = _mx(_hgrn_scores(q, kf, e, mk, h)[0])
                a_ref[pl.ds(r0, CHUNK), _hs(h)] = a
                oh = _dot_nt(qe[:, _hs(h)], sth) + _dot(a, v[:, _hs(h)])
                state[:, _hs(h)] = sth * jnp.exp(blast[:, _hs(h)]) + _dot_tn(v[:, _hs(h)], kd[:, _hs(h)])
                o_ref[pl.ds(r0, CHUNK), _hs(h)] = oh
                yn = _rms_fwd(oh, gn[:, _hs(h)])
                y_ref[pl.ds(r0, CHUNK), _hs(h)] = _mx(yn * _silu(zc[:, 3 * BW + h * DH:3 * BW + (h + 1) * DH]))
            return carry

        lax.fori_loop(0, nc, chunk, 0, unroll=2)

    row = pl.BlockSpec((ts, BW), lambda i: (i, 0))
    one = pl.BlockSpec((1, BW), lambda i: (0, 0))
    return _call(
        body, name="hgrn_fwd", grid=(S // ts,),
        in_specs=[pl.BlockSpec((ts, 1024), lambda i: (i, ZC // 1024)), one, one,
                  pl.BlockSpec(tmat.shape, lambda i: (0, 0)), pl.BlockSpec(masks.shape, lambda i: (0, 0, 0))],
        out_specs=[row, row, row, row],
        out_shape=[jax.ShapeDtypeStruct((S, BW), F32), jax.ShapeDtypeStruct((S, BW), F32),
                   jax.ShapeDtypeStruct((S, BW), MXU_DTYPE), jax.ShapeDtypeStruct((S, BW), MXU_DTYPE)],
        scratch=[pltpu.VMEM((CHUNK, BW), F32)], sem=("arbitrary",))(z, lb, gain, tmat, masks)


def _hgrn_bwd(z, lb, gain, tmat, masks, o_pre, states, scores, dy, dz):
    S = z.shape[0]
    ts = _tile(S, ROW_TILE)
    nt = S // ts
    nc = ts // CHUNK

    def body(zc_ref, lb_ref, gn_ref, tm_ref, mk_ref, o_ref, st_ref, a_ref, dy_ref, dz_in, dz_ref, stat_ref, dstate):
        @pl.when(pl.program_id(0) == 0)
        def _():
            dstate[...] = jnp.zeros_like(dstate)
            stat_ref[...] = jnp.zeros_like(stat_ref)

        lbv, gn, tm, mk = lb_ref[...], gn_ref[...], tm_ref[...], mk_ref[...]
        upper = _tri(CHUNK, upper=True)
        lower_strict = 1.0 - upper

        def chunk(cc, carry):
            c = nc - 1 - cc
            r0 = pl.multiple_of(c * CHUNK, CHUNK)
            zc = zc_ref[pl.ds(r0, CHUNK), :]
            cq, cf, q, sg, g, kf, v, e, b, blast = _hgrn_chunk_fwd(zc, lbv, tm)
            eb = jnp.exp(b)
            ebl = jnp.exp(blast - b)
            qe = q * eb
            kd = kf * ebl
            o = o_ref[pl.ds(r0, CHUNK), :]
            dyv = dy_ref[pl.ds(r0, CHUNK), :]
            stp = st_ref[pl.ds(r0, CHUNK), :]
            cg = zc[:, 3 * BW:4 * BW]
            sgate = _silu(cg)
            dq_parts, dk_parts, dv_parts, dcg_parts = [], [], [], []
            dgain, up_parts, lo_parts, const_parts = [], [], [], []
            for h in range(NH):
                hs = _hs(h)
                oh = o[:, hs]
                r = lax.rsqrt(jnp.mean(oh * oh, axis=-1, keepdims=True) + EPS)
                nrm = oh * r
                dyn = dyv[:, hs] * sgate[:, hs]
                dcg_parts.append(dyv[:, hs] * nrm * gn[:, hs] * _dsilu(cg[:, hs]))
                dgain.append(jnp.sum(dyn * nrm, axis=0, keepdims=True))
                tt = dyn * gn[:, hs]
                doh = r * (tt - nrm * jnp.mean(tt * nrm, axis=-1, keepdims=True))
                _, ql, kl = _hgrn_scores(q, kf, e, mk, h, scores=False)
                a = a_ref[pl.ds(r0, CHUNK), hs]
                da = _dot_nt(doh, v[:, hs])
                dsth = dstate[:, hs]
                ebh = jnp.exp(blast[:, hs])
                dv_parts.append(_dot_tn(a, doh) + _dot_nt(kd[:, hs], dsth))
                dq_inter = eb[:, hs] * _dot(doh, stp[:, hs])
                dk_state = ebl[:, hs] * _dot(v[:, hs], dsth)
                dqh, dkh, gh = dq_inter, dk_state, None
                for lev in range(NLEV + 1):
                    dal = mk[lev] * da
                    xq = _dot(dal, kl[lev][0])
                    yk = _dot_tn(dal, ql[lev][0])
                    gterm = ql[lev][0] * xq - kl[lev][0] * yk
                    gh = gterm if gh is None else gh + gterm
                    dqh = dqh + (xq if lev == NLEV else ql[lev][1] * xq)
                    dkh = dkh + (yk if lev == NLEV else kl[lev][1] * yk)
                up_parts.append(gh + q[:, hs] * dq_inter)
                lo_parts.append(kf[:, hs] * dk_state)
                const_parts.append(jnp.sum(dsth * stp[:, hs], axis=0, keepdims=True) * ebh)
                dstate[:, hs] = dsth * ebh + _dot_tn(doh, qe[:, hs])
                dq_parts.append(dqh)
                dk_parts.append(dkh)
            dq = jnp.concatenate(dq_parts, axis=1)
            dk = jnp.concatenate(dk_parts, axis=1)
            dlg = (_dot_exact_lhs01(upper, jnp.concatenate(up_parts, axis=1))
                   + _dot_exact_lhs01(lower_strict, jnp.concatenate(lo_parts, axis=1))
                   + jnp.concatenate(const_parts, axis=1))
            dsg = sg * (1.0 - sg)
            dz_ref[pl.ds(r0, CHUNK), 0:BW] = _mx(dq * _dsilu(cq))
            dz_ref[pl.ds(r0, CHUNK), BW:2 * BW] = _mx((dlg / g - dk) * (1.0 - lbv) * dsg)
            dz_ref[pl.ds(r0, CHUNK), 2 * BW:3 * BW] = _mx(jnp.concatenate(dv_parts, axis=1))
            dz_ref[pl.ds(r0, CHUNK), 3 * BW:4 * BW] = _mx(jnp.concatenate(dcg_parts, axis=1))
            stat_ref[0:1, :] += jnp.concatenate(dgain, axis=1)
            stat_ref[1:2, :] += jnp.sum((dlg / g - dk) * (1.0 - sg), axis=0, keepdims=True)
            return carry

        lax.fori_loop(0, nc, chunk, 0, unroll=2)

    rev = lambda w, c=0: pl.BlockSpec((ts, w), lambda i: (nt - 1 - i, c))
    one = pl.BlockSpec((1, BW), lambda i: (0, 0))
    return _call(
        body, name="hgrn_bwd", grid=(nt,),
        in_specs=[rev(1024, ZC // 1024), one, one, pl.BlockSpec(tmat.shape, lambda i: (0, 0)),
                  pl.BlockSpec(masks.shape, lambda i: (0, 0, 0)), rev(BW), rev(BW), rev(BW), rev(BW, 2), _ANY],
        out_specs=[rev(1024, ZC // 1024), pl.BlockSpec((8, BW), lambda i: (0, 0))],
        out_shape=[jax.ShapeDtypeStruct((S, ZCOLS), MXU_DTYPE), jax.ShapeDtypeStruct((8, BW), F32)],
        scratch=[pltpu.VMEM((CHUNK, BW), F32)],
        sem=("arbitrary",), aliases={9: 0})(z, lb, gain, tmat, masks, o_pre, states, scores, dy, dz)


def _lower_bounds_fwd(lb_logits):
    def body(l_ref, o_ref):
        l = l_ref[...]
        m = jnp.max(l, axis=0, keepdims=True)
        ex = jnp.exp(l - m)
        p = ex / jnp.sum(ex, axis=0, keepdims=True)
        cs = p[0:1, :]
        o_ref[0:1, :] = jnp.clip(cs - p[0:1, :], 0.0, 1.0)
        for d in range(1, DEPTH):
            cs = cs + p[d:d + 1, :]
            o_ref[d:d + 1, :] = jnp.clip(cs - p[0:1, :], 0.0, 1.0)

    full = pl.BlockSpec((DEPTH, BW), lambda: (0, 0))
    return _call(body, name="lower_bounds_fwd", grid=(), in_specs=[full], out_specs=full,
                 out_shape=jax.ShapeDtypeStruct((DEPTH, BW), F32))(lb_logits)


def _lower_bounds_bwd(lb_logits, dlow):
    def body(l_ref, d_ref, o_ref):
        l = l_ref[...]
        m = jnp.max(l, axis=0, keepdims=True)
        ex = jnp.exp(l - m)
        p = ex / jnp.sum(ex, axis=0, keepdims=True)
        dl = d_ref[...]
        cs = p[0:1, :]
        dcs = []
        for d in range(DEPTH):
            if d > 0:
                cs = cs + p[d:d + 1, :]
            val = cs - p[0:1, :]
            dcs.append(jnp.where((val > 0.0) & (val < 1.0), dl[d:d + 1, :], 0.0))
        total = dcs[0] + dcs[1] + dcs[2] + dcs[3]
        dp = []
        for j in range(DEPTH):
            s = dcs[j]
            for d in range(j + 1, DEPTH):
                s = s + dcs[d]
            dp.append(s - total if j == 0 else s)
        inner = p[0:1, :] * dp[0]
        for j in range(1, DEPTH):
            inner = inner + p[j:j + 1, :] * dp[j]
        for j in range(DEPTH):
            o_ref[j:j + 1, :] = p[j:j + 1, :] * (dp[j] - inner)

    full = pl.BlockSpec((DEPTH, BW), lambda: (0, 0))
    return _call(body, name="lower_bounds_bwd", grid=(), in_specs=[full, full], out_specs=full,
                 out_shape=jax.ShapeDtypeStruct((DEPTH, BW), F32))(lb_logits, dlow)


def _sgu_fwd(z, gv, ws, bs):
    S = z.shape[0]
    ts = _tile(S, ROW_TILE)
    nc = ts // SCHUNK

    def body(zd_ref, gv_ref, ws_ref, bs_ref, y_ref):
        gvv, bsv = gv_ref[...], bs_ref[...]
        tril = _tri(SCHUNK)
        for c in range(nc):
            rs = slice(c * SCHUNK, (c + 1) * SCHUNK)
            zd = zd_ref[rs, :]
            for h in range(NH):
                vn = _rms_fwd(zd[:, BW + h * DH:BW + (h + 1) * DH], gvv[:, _hs(h)])
                s = _dot(ws_ref[h] * tril, vn) + bsv[:, h:h + 1]
                y_ref[rs, _hs(h)] = _mx(zd[:, _hs(h)] * s * _silu(zd[:, 2 * BW + h * DH:2 * BW + (h + 1) * DH]))

    return _call(
        body, name="sgu_fwd", grid=(S // ts,),
        in_specs=[pl.BlockSpec((ts, 1024), lambda i: (i, ZD // 1024)), pl.BlockSpec((1, BW), lambda i: (0, 0)),
                  pl.BlockSpec((NH, SCHUNK, SCHUNK), lambda i: (0, 0, 0)), pl.BlockSpec((SCHUNK, 128), lambda i: (0, 0))],
        out_specs=pl.BlockSpec((ts, BW), lambda i: (i, 0)),
        out_shape=jax.ShapeDtypeStruct((S, BW), MXU_DTYPE), sem=("parallel",))(z, gv, ws, bs)


def _sgu_bwd(z, gv, ws, bs, dy, dz):
    S = z.shape[0]
    ts = _tile(S, ROW_TILE)
    nc = ts // SCHUNK

    def body(zd_ref, gv_ref, ws_ref, bs_ref, dy_ref, dz_in, dz_ref, dws_ref, dbs_ref, st_ref):
        @pl.when(pl.program_id(0) == 0)
        def _():
            dws_ref[...] = jnp.zeros_like(dws_ref)
            dbs_ref[...] = jnp.zeros_like(dbs_ref)
            st_ref[...] = jnp.zeros_like(st_ref)

        gvv, bsv = gv_ref[...], bs_ref[...]
        tril = _tri(SCHUNK)
        dz_ref[:, 3 * BW:4 * BW] = jnp.zeros((ts, BW), MXU_DTYPE)
        for c in range(nc):
            rs = slice(c * SCHUNK, (c + 1) * SCHUNK)
            zd = zd_ref[rs, :]
            dyv = dy_ref[rs, :]
            for h in range(NH):
                hs = _hs(h)
                u = zd[:, hs]
                vraw = zd[:, BW + h * DH:BW + (h + 1) * DH]
                gt = zd[:, 2 * BW + h * DH:2 * BW + (h + 1) * DH]
                gvh = gvv[:, hs]
                vn = _rms_fwd(vraw, gvh)
                wm = ws_ref[h] * tril
                s = _dot(wm, vn) + bsv[:, h:h + 1]
                sil = _silu(gt)
                d = dyv[:, hs]
                ds = d * u * sil
                dz_ref[rs, hs] = _mx(d * s * sil)
                dz_ref[rs, 2 * BW + h * DH:2 * BW + (h + 1) * DH] = _mx(d * u * s * _dsilu(gt))
                dws_ref[h] += tril * _dot_nt(ds, vn)
                dbs_ref[:, h:h + 1] += jnp.sum(ds, axis=-1, keepdims=True)
                dvn = _dot_tn(wm, ds)
                dx, dgr = _rms_bwd(vraw, gvh, dvn)
                dz_ref[rs, BW + h * DH:BW + (h + 1) * DH] = _mx(dx)
                st_ref[0:1, hs] += jnp.sum(dgr, axis=0, keepdims=True)

    return _call(
        body, name="sgu_bwd", grid=(S // ts,),
        in_specs=[pl.BlockSpec((ts, 1024), lambda i: (i, ZD // 1024)), pl.BlockSpec((1, BW), lambda i: (0, 0)),
                  pl.BlockSpec((NH, SCHUNK, SCHUNK), lambda i: (0, 0, 0)), pl.BlockSpec((SCHUNK, 128), lambda i: (0, 0)),
                  pl.BlockSpec((ts, BW), lambda i: (i, 3)), _ANY],
        out_specs=[pl.BlockSpec((ts, 1024), lambda i: (i, ZD // 1024)), pl.BlockSpec((NH, SCHUNK, SCHUNK), lambda i: (0, 0, 0)),
                   pl.BlockSpec((SCHUNK, 128), lambda i: (0, 0)), pl.BlockSpec((8, BW), lambda i: (0, 0))],
        out_shape=[jax.ShapeDtypeStruct((S, ZCOLS), MXU_DTYPE), jax.ShapeDtypeStruct((NH, SCHUNK, SCHUNK), F32),
                   jax.ShapeDtypeStruct((SCHUNK, 128), F32), jax.ShapeDtypeStruct((8, BW), F32)],
        sem=("arbitrary",), aliases={5: 0})(z, gv, ws, bs, dy, dz)


def _merge_fwd(x, z, ys, wup, mb, wo):
    S = x.shape[0]
    ts = _tile(S, MERGE_TILE)

    def body(x_ref, zm_ref, ya_ref, yb_ref, yc_ref, yd_ref, wup_ref, mb_ref, wo_ref, x1_ref, mg_ref):
        yrefs = (ya_ref, yb_ref, yc_ref, yd_ref)
        mbv = mb_ref[...]
        merged = None
        for b in range(NBR):
            cs = slice(b * D_MODEL, (b + 1) * D_MODEL)
            term = _sigmoid(zm_ref[:, cs] + mbv[b:b + 1, :]) * jnp.dot(yrefs[b][...], wup_ref[b],
                                                                      preferred_element_type=F32)
            merged = term if merged is None else merged + term
        mg = _mx(merged)
        mg_ref[...] = mg
        x1_ref[...] = x_ref[...] + jnp.dot(mg, wo_ref[...], preferred_element_type=F32)

    row = lambda w: pl.BlockSpec((ts, w), lambda i: (i, 0))
    return _call(
        body, name="merge_fwd", grid=(S // ts,),
        in_specs=[row(D_MODEL), pl.BlockSpec((ts, 4096), lambda i: (i, 0)), row(BW), row(BW), row(BW), row(BW),
                  pl.BlockSpec((NBR, BW, D_MODEL), lambda i: (0, 0, 0)), pl.BlockSpec((NBR, D_MODEL), lambda i: (0, 0)),
                  pl.BlockSpec((D_MODEL, D_MODEL), lambda i: (0, 0))],
        out_specs=[row(D_MODEL), row(D_MODEL)],
        out_shape=[jax.ShapeDtypeStruct((S, D_MODEL), F32), jax.ShapeDtypeStruct((S, D_MODEL), MXU_DTYPE)],
        sem=("parallel",))(x, z, *ys, wup, mb, wo)


def _merge_bwd(dx1, z, ys, wup, mb, wo):
    S = dx1.shape[0]
    ts = _tile(S, MERGE_TILE)

    def body(dx_ref, zm_ref, ya_ref, yb_ref, yc_ref, yd_ref, wup_ref, mb_ref, wo_ref,
             dzm_ref, du_ref, dxb_ref, dy_ref, st_ref):
        @pl.when(pl.program_id(0) == 0)
        def _():
            st_ref[...] = jnp.zeros_like(st_ref)

        yrefs = (ya_ref, yb_ref, yc_ref, yd_ref)
        mbv = mb_ref[...]
        dxb = _mx(dx_ref[...])
        dxb_ref[...] = dxb
        dmerged = _dot_nt(dxb, wo_ref[...])
        for b in range(NBR):
            cs = slice(b * D_MODEL, (b + 1) * D_MODEL)
            u = jnp.dot(yrefs[b][...], wup_ref[b], preferred_element_type=F32)
            sg = _sigmoid(zm_ref[:, cs] + mbv[b:b + 1, :])
            du = _mx(dmerged * sg)
            du_ref[:, cs] = du
            dzm = dmerged * u * sg * (1.0 - sg)
            dzm_ref[:, cs] = _mx(dzm)
            st_ref[b:b + 1, :] += jnp.sum(dzm, axis=0, keepdims=True)
            dy_ref[:, b * BW:(b + 1) * BW] = _dot_nt(du, wup_ref[b])

    row = lambda w: pl.BlockSpec((ts, w), lambda i: (i, 0))
    return _call(
        body, name="merge_bwd", grid=(S // ts,),
        in_specs=[row(D_MODEL), pl.BlockSpec((ts, 4096), lambda i: (i, 0)), row(BW), row(BW), row(BW), row(BW),
                  pl.BlockSpec((NBR, BW, D_MODEL), lambda i: (0, 0, 0)),
                  pl.BlockSpec((NBR, D_MODEL), lambda i: (0, 0)), pl.BlockSpec((D_MODEL, D_MODEL), lambda i: (0, 0))],
        out_specs=[row(4096), row(4096), row(D_MODEL), row(D_MODEL), pl.BlockSpec((8, D_MODEL), lambda i: (0, 0))],
        out_shape=[jax.ShapeDtypeStruct((S, ZCOLS), MXU_DTYPE), jax.ShapeDtypeStruct((S, 4096), MXU_DTYPE),
                   jax.ShapeDtypeStruct((S, D_MODEL), MXU_DTYPE), jax.ShapeDtypeStruct((S, D_MODEL), F32),
                   jax.ShapeDtypeStruct((8, D_MODEL), F32)],
        sem=("arbitrary",))(dx1, z, *ys, wup, mb, wo)


def _ple_fwd(x1, p, g, wg, wp):
    S = x1.shape[0]
    ts = _tile(S, ROW_TILE)

    def body(x_ref, p_ref, g_ref, wg_ref, wp_ref, o_ref):
        x = x_ref[...]
        hp = _mx(_rms_fwd(x, g_ref[...]))
        gate = _sigmoid(jnp.dot(hp, wg_ref[...], preferred_element_type=F32))
        pp = jnp.dot(_mx(p_ref[...]), wp_ref[...], preferred_element_type=F32)
        o_ref[...] = x + gate * pp

    row = lambda w: pl.BlockSpec((ts, w), lambda i: (i, 0))
    return _call(
        body, name="ple_fwd", grid=(S // ts,),
        in_specs=[row(D_MODEL), row(PLE), pl.BlockSpec((1, D_MODEL), lambda i: (0, 0)),
                  pl.BlockSpec((D_MODEL, D_MODEL), lambda i: (0, 0)), pl.BlockSpec((PLE, D_MODEL), lambda i: (0, 0))],
        out_specs=row(D_MODEL), out_shape=jax.ShapeDtypeStruct((S, D_MODEL), F32),
        sem=("parallel",))(x1, p, g, wg, wp)


def _ple_bwd(x1, p, dx2, g, wg, wp):
    S = x1.shape[0]
    ts = _tile(S, ROW_TILE)

    def body(x_ref, p_ref, dx_ref, g_ref, wg_ref, wp_ref, dx1_ref, hp_ref, dgl_ref, dpp_ref, pb_ref, st_ref):
        @pl.when(pl.program_id(0) == 0)
        def _():
            st_ref[...] = jnp.zeros_like(st_ref)

        x, gv, dx2 = x_ref[...], g_ref[...], dx_ref[...]
        hp = _mx(_rms_fwd(x, gv))
        hp_ref[...] = hp
        gate = _sigmoid(jnp.dot(hp, wg_ref[...], preferred_element_type=F32))
        pb = _mx(p_ref[...])
        pb_ref[...] = pb
        pp = jnp.dot(pb, wp_ref[...], preferred_element_type=F32)
        dgl = _mx(dx2 * pp * gate * (1.0 - gate))
        dgl_ref[...] = dgl
        dpp_ref[...] = _mx(dx2 * gate)
        dhp = _dot_nt(dgl, wg_ref[...])
        dxn, dgr = _rms_bwd(x, gv, dhp)
        dx1_ref[...] = dx2 + dxn
        st_ref[0:1, :] += jnp.sum(dgr, axis=0, keepdims=True)

    row = lambda w: pl.BlockSpec((ts, w), lambda i: (i, 0))
    sq = pl.BlockSpec((D_MODEL, D_MODEL), lambda i: (0, 0))
    return _call(
        body, name="ple_bwd", grid=(S // ts,),
        in_specs=[row(D_MODEL), row(PLE), row(D_MODEL), pl.BlockSpec((1, D_MODEL), lambda i: (0, 0)), sq,
                  pl.BlockSpec((PLE, D_MODEL), lambda i: (0, 0))],
        out_specs=[row(D_MODEL), row(D_MODEL), row(D_MODEL), row(D_MODEL), row(PLE),
                   pl.BlockSpec((8, D_MODEL), lambda i: (0, 0))],
        out_shape=[jax.ShapeDtypeStruct((S, D_MODEL), F32)] + [jax.ShapeDtypeStruct((S, D_MODEL), MXU_DTYPE)] * 3
        + [jax.ShapeDtypeStruct((S, PLE), MXU_DTYPE), jax.ShapeDtypeStruct((8, D_MODEL), F32)],
        sem=("arbitrary",))(x1, p, dx2, g, wg, wp)


def _pad_rows(a, rows=8):
    return jnp.concatenate([a, jnp.zeros((rows - a.shape[0],) + a.shape[1:], a.dtype)], axis=0)


def _pad_lanes(a, lanes=128):
    return jnp.concatenate([a, jnp.zeros(a.shape[:-1] + (lanes - a.shape[-1],), a.dtype)], axis=-1)


def _wz_from_w_in(w):
    zeros = lambda n: jnp.zeros((w.shape[0], n), w.dtype)
    return jnp.concatenate([w[:, _OM:_OEND], w[:, _OA:_OB], w[:, _OB:_OF], w[:, _OC:_OD], w[:, _OD:_OM], zeros(256),
                            w[:, _OF:_OC], zeros(124)], axis=1)


def _w_in_from_wz(g):
    return jnp.concatenate([g[:, ZA:ZB], g[:, ZB:ZC], g[:, ZF:ZF + 4], g[:, ZC:ZD], g[:, ZD:ZD + 768], g[:, ZM:ZA]], axis=1)


_W_IN_GROUPS = [(_OA, _OB, ZA), (_OB, _OF, ZB), (_OF, _OC, ZF), (_OC, _OD, ZC), (_OD, _OM, ZD), (_OM, _OEND, ZM)]


def _wz_from_shards(g):
    n = g.shape[-1]
    pieces, pos = [], 0
    for a, b, zs in sorted(_W_IN_GROUPS, key=lambda grp: grp[2]):
        if zs > pos:
            pieces.append(jnp.zeros((g.shape[1], zs - pos), g.dtype))
        for k in range(4):
            lo, hi = max(a, k * n), min(b, (k + 1) * n)
            if lo < hi:
                pieces.append(g[k][:, lo - k * n:hi - k * n])
        pos = zs + (b - a)
    pieces.append(jnp.zeros((g.shape[1], ZCOLS - pos), g.dtype))
    return jnp.concatenate(pieces, axis=1)


def _w_in_slabs_from_wz(g):
    n = _OEND // 4
    slabs = []
    for k in range(4):
        pieces = []
        for a, b, zs in _W_IN_GROUPS:
            lo, hi = max(a, k * n), min(b, (k + 1) * n)
            if lo < hi:
                pieces.append(g[:, zs + lo - a:zs + hi - a])
        slabs.append(jnp.concatenate(pieces, axis=1))
    return jnp.stack(slabs)


def _local_step(x, p, target, wts, dist=None):
    tmat_np, masks_np = _hgrn_consts()
    tmat = jnp.asarray(tmat_np, MXU_DTYPE)
    masks = jnp.asarray(masks_np, F32)
    lower = _lower_bounds_fwd(wts["lb_logits"])
    saved = []
    for li in range(DEPTH):
        big = dist.weights(li) if dist else {n: wts[n][li] for n in _GATHERED}
        wz = _wz_from_shards(big["w_in"]) if big["w_in"].ndim == 3 else _wz_from_w_in(big["w_in"])
        g_mix = wts["norm_mix"][li][None, :]
        h = _norm_fwd(x, g_mix)
        z = _mm_nn(h, wz, F32, name="mm_z", tn=1664)
        cw = _pad_rows(jnp.concatenate([big["conv_w"], wts["conv_b"][li][None, :]], axis=0))
        ya = _conv_fwd(z, cw)
        fb = _pad_lanes(wts["fgate_bias"][li][None, :])
        gq, gk = wts["q_norm"][li][None, :], wts["k_norm"][li][None, :]
        qa, ka, va, cc = _attn_prep_fwd(z, fb, gq, gk)
        o, lse, yb, got = _flash_fwd(qa, ka, va, z, comm=dist.fwd_comm(li) if dist else None)
        if dist:
            dist.fwd_done(li, got)
        lb = lower[li][None, :]
        gh = wts["hgrn_norm"][li][None, :]
        o_pre, states, hscores, yc = _hgrn_fwd(z, lb, gh, tmat, masks)
        gv = wts["sgu_norm"][li][None, :]
        ws = wts["spatial_w"][li]
        bs = _pad_lanes(wts["spatial_b"][li].T)
        yd = _sgu_fwd(z, gv, ws, bs)
        ys = (ya, yb, yc, yd)
        x1, merged = _merge_fwd(x, z, ys, big["w_up"], big["merge_b"], big["w_o"])
        g_ple = wts["norm_ple"][li][None, :]
        x2 = _ple_fwd(x1, p[li], g_ple, big["w_ple_gate"], big["w_ple_proj"])
        saved.append(dict(x=x, h=h, z=z, wz=wz, cw=cw, fb=fb, gq=gq, gk=gk, qa=qa, ka=ka, va=va, cc=cc, o=o, lse=lse,
                          lb=lb, gh=gh, o_pre=o_pre, states=states, hscores=hscores, gv=gv, ws=ws, bs=bs, ys=ys, x1=x1, merged=merged,
                          g_mix=g_mix, g_ple=g_ple, big=big))
        x = x2

    loss, dx = _loss_fwd_bwd(x, target)

    names = ["norm_mix", "w_in", "conv_w", "conv_b", "fgate_bias", "q_norm", "k_norm", "lb", "hgrn_norm", "sgu_norm",
             "spatial_w", "spatial_b", "w_up", "merge_b", "w_o", "norm_ple", "w_ple_gate", "w_ple_proj"]
    gl = {n: [None] * DEPTH for n in names}
    for li in reversed(range(DEPTH)):
        s = saved[li]
        z, big = s["z"], s["big"]
        wg, wp = big["w_ple_gate"], big["w_ple_proj"]
        dx1, hp, dgl, dpp, pb, st = _ple_bwd(s["x1"], p[li], dx, s["g_ple"], wg, wp)
        gl["norm_ple"][li] = st[0]
        gl["w_ple_gate"][li] = _mm_tn(hp, dgl, name="mm_dwg")
        gl["w_ple_proj"][li] = _mm_tn(pb, dpp, name="mm_dwp")
        wup, wo = big["w_up"], big["w_o"]
        dz, du, dxb, dy, st = _merge_bwd(dx1, z, s["ys"], wup, big["merge_b"], wo)
        gl["merge_b"][li] = st[0:NBR]
        gl["w_o"][li] = _mm_tn(s["merged"], dxb, name="mm_dwo")
        gl["w_up"][li] = jnp.stack([_mm_tn(s["ys"][b], du, name="mm_dwup", ycol=b, n=D_MODEL) for b in range(NBR)])
        dz, st = _conv_bwd(z, s["cw"], dy, dz)
        gl["conv_w"][li] = st[0:3]
        gl["conv_b"][li] = st[3]
        dgate, qb, doa = _attn_gate_bwd(dy, s["o"], z, s["qa"], s["cc"], s["lse"])
        dq, doa, got = _flash_bwd_dq(qb, s["ka"], s["va"], doa, comm=dist.bwd_comm() if dist else None)
        if dist:
            dist.bwd_done(got)
        dk, dv, dcc = _flash_bwd_dkv(qb, s["ka"], s["va"], doa)
        dz, dzf, st = _attn_prep_bwd(z, s["fb"], s["gq"], s["gk"], dq, dk, dv, dgate, dcc, dz)
        dz = _put_fgate_cols(dzf, dz)
        gl["q_norm"][li] = st[0, 0:DH]
        gl["k_norm"][li] = st[1, 0:DH]
        gl["fgate_bias"][li] = st[2, 0:NH]
        dz, st = _hgrn_bwd(z, s["lb"], s["gh"], tmat, masks, s["o_pre"], s["states"], s["hscores"], dy, dz)
        gl["hgrn_norm"][li] = st[0]
        gl["lb"][li] = st[1]
        dz, dws, dbs, st = _sgu_bwd(z, s["gv"], s["ws"], s["bs"], dy, dz)
        gl["sgu_norm"][li] = st[0]
        gl["spatial_w"][li] = dws
        gl["spatial_b"][li] = dbs[:, 0:NH].T
        dwz = _mm_tn(s["h"], dz, name="mm_dwz", tn=1664)
        gl["w_in"][li] = dwz if dist else _w_in_from_wz(dwz)
        if dist:
            dist.push(li, {n: gl[n][li] for n in _BIG})
        if dist and li == 0:
            dh, got = _mm_nn(dz, s["wz"], F32, name="mm_dh", ts=ROW_TILE // 2, b_transposed=True, comm=dist.bwd_comm())
            dist.bwd_done(got)
        else:
            dh = _mm_nn(dz, s["wz"], F32, name="mm_dh", ts=ROW_TILE // 2, b_transposed=True)
        dx, st = _norm_bwd(s["x"], s["g_mix"], dh, dx1)
        gl["norm_mix"][li] = st[0]

    if dist:
        dist.finish()
    grads = {n: jnp.stack(v) for n, v in gl.items() if not (dist and n in _BIG)}
    grads["lb_logits"] = _lower_bounds_bwd(wts["lb_logits"], grads.pop("lb"))
    return loss, dx, grads


def _my_pos():
    return lax.axis_index("x"), lax.axis_index("y"), lax.axis_index("c")


def _gather_phase(phase, ins, outs, sems):
    if phase == 1:
        return
    send, recv, lsem = sems
    x, y, c = _my_pos()
    me = 2 * x + y
    peers = [(1 - x, y), (x, 1 - y), (1 - x, 1 - y)]
    copies = []
    for t in range(len(ins)):
        copies.append(pltpu.make_async_copy(ins[t], outs[t].at[me], lsem.at[t]))
        for j, (px, py) in enumerate(peers):
            copies.append(pltpu.make_async_remote_copy(
                src_ref=ins[t], dst_ref=outs[t].at[me], send_sem=send.at[t, j], recv_sem=recv.at[t, j],
                device_id=(px, py, c), device_id_type=MESH))
    for cp in copies:
        if phase == 0:
            cp.start()
        else:
            cp.wait()


N_HALVED = 5


def _gather_halves_phase(phase, ins, outs, sems):
    send, recv, lsem = sems
    x, y, c = _my_pos()
    me = 2 * x + y
    sib = (x, y, 1 - c)
    chips = [(1 - x, y), (x, 1 - y), (1 - x, 1 - y)]

    def rc(t, k, src, dst, dev):
        return pltpu.make_async_remote_copy(src_ref=src, dst_ref=dst, send_sem=send.at[t, k], recv_sem=recv.at[t, k],
                                            device_id=dev, device_id_type=MESH)

    for t in range(len(ins)):
        local = pltpu.make_async_copy(ins[t], outs[t].at[me], lsem.at[t])
        if t >= N_HALVED:
            whole = [rc(t, j, ins[t], outs[t].at[me], (px, py, c)) for j, (px, py) in enumerate(chips)]
            for cp in [local] + whole:
                if phase == 0:
                    cp.start()
                elif phase == 2:
                    cp.wait()
            continue
        hr = ins[t].shape[0] // 2
        mine, other = pl.ds(c * hr, hr), pl.ds((1 - c) * hr, hr)
        first = [rc(t, j, ins[t].at[mine], outs[t].at[me, mine], (px, py, c)) for j, (px, py) in enumerate(chips)]
        landed = [outs[t].at[2 * px + py, mine] for px, py in chips]
        passed = [rc(t, 3 + j, slot, slot, sib) for j, slot in enumerate(landed)]
        if phase == 0:
            local.start()
            for cp in first:
                cp.start()
        elif phase == 1:
            for j, slot in enumerate(landed):
                rc(t, j, slot, slot, (x, y, c)).wait_recv()
                passed[j].start()
        else:
            for j, (px, py) in enumerate(chips):
                slot = outs[t].at[2 * px + py, other]
                rc(t, 3 + j, slot, slot, (x, y, c)).wait_recv()
            for cp in first + passed:
                cp.wait_send()
            local.wait()


def _exchange_phase(phase, ins, outs, sems):
    send, recv, lsem = sems
    x, y, c = _my_pos()
    me = 2 * x + y
    sib = (x, y, 1 - c)
    chips = [(1 - x, y), (x, 1 - y), (1 - x, 1 - y)]

    def rc(t, k, src, dst, dev):
        return pltpu.make_async_remote_copy(src_ref=src, dst_ref=dst, send_sem=send.at[t, k], recv_sem=recv.at[t, k],
                                            device_id=dev, device_id_type=MESH)

    for t in range(len(ins)):
        local = pltpu.make_async_copy(ins[t].at[me], outs[t].at[2 * me + c], lsem.at[t])
        first = [rc(t, 0, ins[t].at[me], outs[t].at[2 * me + c], sib)]
        first += [rc(t, 1 + j, ins[t].at[2 * px + py], outs[t].at[2 * me + c], (px, py, c))
                  for j, (px, py) in enumerate(chips)]
        landed = [outs[t].at[2 * (2 * px + py) + c] for px, py in chips]
        passed = [rc(t, 4 + j, slot, slot, sib) for j, slot in enumerate(landed)]
        if phase == 0:
            local.start()
            for cp in first:
                cp.start()
        elif phase == 1:
            for j, slot in enumerate(landed):
                rc(t, 1 + j, slot, slot, (x, y, c)).wait_recv()
                passed[j].start()
        else:
            s0 = outs[t].at[2 * me + (1 - c)]
            rc(t, 0, s0, s0, (x, y, c)).wait_recv()
            for j, (px, py) in enumerate(chips):
                slot = outs[t].at[2 * (2 * px + py) + (1 - c)]
                rc(t, 4 + j, slot, slot, (x, y, c)).wait_recv()
            for cp in first + passed:
                cp.wait_send()
            local.wait()


_COMM_PHASES = {"gather": _gather_phase, "gather_halves": _gather_halves_phase, "exchange": _exchange_phase}
_gathered_shapes = lambda arrays: [jax.ShapeDtypeStruct((4,) + a.shape, a.dtype) for a in arrays]
_COMM_OUT = {"gather": _gathered_shapes, "gather_halves": _gathered_shapes,
             "exchange": lambda arrays: [jax.ShapeDtypeStruct((8,) + a.shape[1:], a.dtype) for a in arrays]}


def _comm_sems(kind, nc):
    k = {"gather": 3, "gather_halves": 6, "exchange": 7}[kind]
    return [pltpu.SemaphoreType.DMA((nc, k)), pltpu.SemaphoreType.DMA((nc, k)), pltpu.SemaphoreType.DMA((nc,))]


def _comm_alone(kind, arrays, name):
    nc = len(arrays)

    def body(*refs):
        for phase in range(3):
            _COMM_PHASES[kind](phase, refs[:nc], refs[nc:2 * nc], refs[2 * nc:])

    hbm = pl.BlockSpec(memory_space=pl.ANY)
    return pl.pallas_call(
        functools.partial(body), name=name, in_specs=[hbm] * nc, out_specs=[hbm] * nc,
        out_shape=_COMM_OUT[kind](arrays), scratch_shapes=_comm_sems(kind, nc))(*arrays)


def _allreduce_small(v):
    R = v.shape[0]

    def body(v_ref, o_ref, pair, chips, send, recv):
        x, y, c = _my_pos()
        me = 2 * x + y
        pair[c] = v_ref[...]
        swap = pltpu.make_async_remote_copy(src_ref=v_ref, dst_ref=pair.at[c], send_sem=send.at[0], recv_sem=recv.at[0],
                                            device_id=(x, y, 1 - c), device_id_type=MESH)
        swap.start()
        swap.wait()
        chips[me] = pair[0] + pair[1]
        copies = [pltpu.make_async_remote_copy(src_ref=chips.at[me], dst_ref=chips.at[me], send_sem=send.at[1 + j],
                                               recv_sem=recv.at[1 + j], device_id=(px, py, c), device_id_type=MESH)
                  for j, (px, py) in enumerate([(1 - x, y), (x, 1 - y), (1 - x, 1 - y)])]
        for cp in copies:
            cp.start()
        for cp in copies:
            cp.wait()
        o_ref[...] = (chips[0] + chips[1]) + (chips[2] + chips[3])

    vm = pl.BlockSpec(memory_space=pltpu.VMEM)
    return pl.pallas_call(
        functools.partial(body), name="allreduce_small", in_specs=[vm], out_specs=vm,
        out_shape=jax.ShapeDtypeStruct((R, 128), F32),
        scratch_shapes=[pltpu.VMEM((2, R, 128), F32), pltpu.VMEM((4, R, 128), F32),
                        pltpu.SemaphoreType.DMA((4,)), pltpu.SemaphoreType.DMA((4,))],
        compiler_params=pltpu.CompilerParams(vmem_limit_bytes=VMEM_LIMIT))(v)


def _adamw(w, m, v, parts, nparts):
    A, R, C = w.shape
    per_layer = isinstance(parts, (list, tuple))
    tr = _tile(R, 64) if per_layer else (R if R <= 128 else _tile(R, 128))
    nparr = len(parts) if per_layer else 1

    def body(*refs):
        w_ref, m_ref, v_ref = refs[:3]
        p_refs = refs[3:3 + nparr]
        g_ref, d_ref, nm_ref, nv_ref = refs[3 + nparr:]

        def update(read):
            g = read(0).astype(F32)
            for k in range(1, nparts):
                g = g + read(k).astype(F32)
            mm = ADAM_B1 * m_ref[0] + (1.0 - ADAM_B1) * g
            vv = ADAM_B2 * v_ref[0] + (1.0 - ADAM_B2) * jnp.square(g)
            m_hat = mm / (1.0 - ADAM_B1 ** ADAM_STEP)
            v_hat = vv / (1.0 - ADAM_B2 ** ADAM_STEP)
            g_ref[0] = g
            d_ref[0] = -ADAM_LR * (m_hat / (jnp.sqrt(v_hat) + ADAM_EPS) + ADAM_WD * w_ref[0])
            nm_ref[0] = mm
            nv_ref[0] = vv

        if not per_layer:
            update(lambda k: p_refs[0][k, 0])
        else:
            for a in range(A):
                @pl.when(pl.program_id(0) == a)
                def _(a=a):
                    update(lambda k: p_refs[a][k])

    blk = pl.BlockSpec((1, tr, C), lambda a, r: (a, r, 0))
    if per_layer:
        pspecs = [pl.BlockSpec((nparts, tr, C), functools.partial(lambda a, r, l: (0, jnp.where(a == l, r, 0), 0), l=l))
                  for l in range(A)]
        pargs = list(parts)
    else:
        pspecs = [pl.BlockSpec((nparts, 1, tr, C), lambda a, r: (0, a, r, 0))]
        pargs = [parts]
    return _call(
        body, name="adamw", grid=(A, R // tr), in_specs=[blk, blk, blk] + pspecs,
        out_specs=[blk] * 4, out_shape=[jax.ShapeDtypeStruct((A, R, C), F32)] * 4,
        sem=("arbitrary", "arbitrary"))(w, m, v, *pargs)


def _rows128(a):
    width = a.shape[-1]
    a2 = a.reshape(-1, width)
    k = -(-width // 128)
    if width % 128:
        a2 = jnp.pad(a2, ((0, 0), (0, k * 128 - width)))
    return a2.reshape(-1, 128)


def _from_rows128(r, shape):
    width = shape[-1]
    k = -(-width // 128)
    return r.reshape(-1, k * 128)[:, :width].reshape(shape)


def _as3d(a):
    if a.ndim == 2:
        return a[None]
    if a.ndim == 3:
        return a
    return a.reshape((-1,) + a.shape[-2:])


_WEIGHTS = ["norm_mix", "w_in", "conv_w", "conv_b", "fgate_bias", "q_norm", "k_norm", "lb_logits", "hgrn_norm", "sgu_norm",
            "spatial_w", "spatial_b", "w_up", "merge_b", "w_o", "norm_ple", "w_ple_gate", "w_ple_proj"]
_BIG = ["w_in", "w_up", "w_o", "w_ple_gate", "w_ple_proj"]
_GATHERED = _BIG + ["conv_w", "merge_b"]
_SHARD_AXIS = {"w_in": 2, "w_up": 3, "w_o": 1, "w_ple_gate": 1, "w_ple_proj": 2, "conv_w": 2, "merge_b": 2}
_SMALL = [n for n in _WEIGHTS if n not in _BIG]


class _Dist:
    def __init__(self, w):
        self.w = w
        self.full = {0: self._unpack(_comm_alone("gather_halves", self._shards(0), "gather_weights"))}
        self.contrib = {}
        self.pending = None

    def _shards(self, l):
        return [_mx(self.w[n][l]) for n in _BIG] + [self.w["conv_w"][l], self.w["merge_b"][l]]

    def _unpack(self, gathered):
        return {n: g if n == "w_in" else jnp.concatenate([g[k] for k in range(4)], axis=_SHARD_AXIS[n] - 1)
                for n, g in zip(_GATHERED, gathered)}

    def weights(self, l):
        return self.full[l]

    def fwd_comm(self, l):
        return ("gather", self._shards(l + 1)) if l + 1 < DEPTH else None

    def fwd_done(self, l, got):
        if got:
            self.full[l + 1] = self._unpack(got)

    def push(self, l, grads):
        self.pending = (l, [(_w_in_slabs_from_wz(grads[n]) if n == "w_in" else
                             jnp.stack(jnp.split(grads[n], 4, axis=_SHARD_AXIS[n] - 1))).astype(GRAD_WIRE_DTYPE)
                            for n in _BIG])

    def bwd_comm(self):
        return ("exchange", self.pending[1]) if self.pending else None

    def bwd_done(self, got):
        if got:
            self.contrib[self.pending[0]] = got
            self.pending = None

    def finish(self):
        if self.pending:
            self.contrib[self.pending[0]] = _comm_alone("exchange", self.pending[1], "exchange_grads")
            self.pending = None


def kernel(x, p, norm_mix, w_in, conv_w, conv_b, fgate_bias, q_norm, k_norm, lb_logits, hgrn_norm, sgu_norm, spatial_w, spatial_b, w_up, merge_b, w_o, norm_ple, w_ple_gate, w_ple_proj, loss_target, m_norm_mix, m_w_in, m_conv_w, m_conv_b, m_fgate_bias, m_q_norm, m_k_norm, m_lb_logits, m_hgrn_norm, m_sgu_norm, m_spatial_w, m_spatial_b, m_w_up, m_merge_b, m_w_o, m_norm_ple, m_w_ple_gate, m_w_ple_proj, v_norm_mix, v_w_in, v_conv_w, v_conv_b, v_fgate_bias, v_q_norm, v_k_norm, v_lb_logits, v_hgrn_norm, v_sgu_norm, v_spatial_w, v_spatial_b, v_w_up, v_merge_b, v_w_o, v_norm_ple, v_w_ple_gate, v_w_ple_proj):
    loc = dict(locals())
    w = {n: loc[n] for n in _WEIGHTS}
    m = {n: loc["m_" + n] for n in _WEIGHTS}
    v = {n: loc["v_" + n] for n in _WEIGHTS}
    chip = 2 * lax.axis_index("x") + lax.axis_index("y")

    dist = _Dist(w)
    loss_blk, dx, grads = _local_step(x[0], p[:, 0], loss_target[0], w, dist)
    loss = lax.psum(loss_blk[0, 0], ("x", "y", "c"))

    blocks = [_rows128(grads[n]) for n in _SMALL]
    nrows = sum(b.shape[0] for b in blocks)
    packed = jnp.concatenate(blocks + [jnp.zeros(((-nrows) % 8, 128), F32)], axis=0)
    red = _allreduce_small(packed)
    small, off = {}, 0
    for n, b in zip(_SMALL, blocks):
        small[n] = _from_rows128(red[off:off + b.shape[0]], grads[n].shape)
        off += b.shape[0]
    for n in ("conv_w", "merge_b"):
        ax = _SHARD_AXIS[n]
        width = small[n].shape[ax] // 4
        small[n] = lax.dynamic_slice_in_dim(small[n], chip * width, width, axis=ax)

    out_g, out_d, out_m, out_v = {}, {}, {}, {}
    for n in _WEIGHTS:
        shp = w[n].shape
        if n in _BIG:
            w3, m3, v3 = (a.reshape((DEPTH, -1, shp[-1])) for a in (w[n], m[n], v[n]))
            parts = [dist.contrib[l][_BIG.index(n)].reshape((8,) + w3.shape[1:]) for l in range(DEPTH)]
            g, d, nm, nv = _adamw(w3, m3, v3, parts, 8)
        else:
            g, d, nm, nv = _adamw(_as3d(w[n]), _as3d(m[n]), _as3d(v[n]), _as3d(small[n])[None], 1)
        out_g[n], out_d[n], out_m[n], out_v[n] = (a.reshape(shp) for a in (g, d, nm, nv))

    return (loss, dx[None], *[out_g[n] for n in _WEIGHTS], *[out_d[n] for n in _WEIGHTS],
            *[out_m[n] for n in _WEIGHTS], *[out_v[n] for n in _WEIGHTS])
```

```python
import functools

import numpy as np
import jax
import jax.numpy as jnp
from jax import lax
from jax.experimental import pallas as pl
from jax.experimental.pallas import tpu as pltpu

F32 = jnp.float32
MXU_DTYPE = jnp.bfloat16

D_MODEL = 1024
BW = 256
NH = 4
DH = 64
DEPTH = 4
NBR = 4
PLE = 256
CHUNK = 64
SCHUNK = 128
EPS = 1e-6
MASK_VALUE = -1e30
NLEV = 6

ADAM_LR, ADAM_B1, ADAM_B2, ADAM_EPS, ADAM_WD, ADAM_STEP = 0.001, 0.9, 0.999, 1e-08, 0.01, 10

ZM, ZA, ZB, ZC, ZD, ZF = 0, 4096, 5120, 6144, 7168, 8192
ZCOLS = 8320
ZB_GATE = ZB + 3 * BW
_OA, _OB, _OF, _OC, _OD, _OM, _OEND = 0, 1024, 2048, 2052, 3076, 3844, 7940

VMEM_LIMIT = 56 * 1024 * 1024
ROW_TILE = 512
FLASH_TILE = 1024
GRAD_WIRE_DTYPE = jnp.bfloat16
MERGE_TILE = 512
MESH = pl.DeviceIdType.MESH
_ANY = pl.BlockSpec(memory_space=pl.ANY)


def _tile(n, pref):
    t = min(n, pref)
    assert n % t == 0, (n, t)
    return t


def _call(body, *, name, grid, in_specs, out_specs, out_shape, scratch=(), sem=None, aliases=None):
    return pl.pallas_call(
        functools.partial(body), name=name, grid=grid, in_specs=in_specs, out_specs=out_specs,
        out_shape=out_shape, scratch_shapes=list(scratch), input_output_aliases=aliases or {},
        compiler_params=pltpu.CompilerParams(dimension_semantics=sem, vmem_limit_bytes=VMEM_LIMIT))


def _mx(x):
    return x.astype(MXU_DTYPE)


def _dot(a, b):
    return jnp.dot(_mx(a), _mx(b), preferred_element_type=F32)


def _dot_nt(a, b):
    return lax.dot_general(_mx(a), _mx(b), (((1,), (1,)), ((), ())), preferred_element_type=F32)


def _dot_tn(a, b):
    return lax.dot_general(_mx(a), _mx(b), (((0,), (0,)), ((), ())), preferred_element_type=F32)


def _top16(x):
    u = lax.bitcast_convert_type(x, jnp.uint32) & jnp.uint32(0xFFFF0000)
    return lax.bitcast_convert_type(u, F32)


def _split3(x):
    hi = _top16(x)
    r1 = x - hi
    mid = _top16(r1)
    return _mx(hi), _mx(mid), _mx(r1 - mid)


def _dot_exact_lhs01(t, x):
    hi, mid, lo = _split3(x)
    t = _mx(t)
    return (jnp.dot(t, hi, preferred_element_type=F32) + jnp.dot(t, mid, preferred_element_type=F32)
            + jnp.dot(t, lo, preferred_element_type=F32))


def _sigmoid(x):
    return jax.nn.sigmoid(x)


def _silu(x):
    return x * _sigmoid(x)


def _dsilu(x):
    s = _sigmoid(x)
    return s * (1.0 + x * (1.0 - s))


def _log_sigmoid(x):
    return jnp.minimum(x, 0.0) - jnp.log(1.0 + jnp.exp(-jnp.abs(x)))


def _rms_fwd(x, g):
    r = lax.rsqrt(jnp.mean(x * x, axis=-1, keepdims=True) + EPS)
    return x * r * g


def _rms_bwd(x, g, dy):
    r = lax.rsqrt(jnp.mean(x * x, axis=-1, keepdims=True) + EPS)
    n = x * r
    t = dy * g
    dx = r * (t - n * jnp.mean(t * n, axis=-1, keepdims=True))
    return dx, dy * n


def _tri(n, upper=False):
    r = lax.broadcasted_iota(jnp.int32, (n, n), 0)
    c = lax.broadcasted_iota(jnp.int32, (n, n), 1)
    return jnp.where((c >= r) if upper else (r >= c), 1.0, 0.0).astype(F32)


def _hs(h):
    return slice(h * DH, (h + 1) * DH)


def _mm_nn(a, b, out_dtype, *, name, ts=None, tn=None, tk=None, b_transposed=False, comm=None):
    S, K = a.shape
    N = b.shape[0] if b_transposed else b.shape[1]
    ts = _tile(S, ts or ROW_TILE)
    tn = _tile(N, tn or N)
    tk = _tile(K, tk or K)
    nk = K // tk
    b_spec = (pl.BlockSpec((tn, tk), lambda j, i, k: (j, k)) if b_transposed
              else pl.BlockSpec((tk, tn), lambda j, i, k: (k, j)))

    def body(a_ref, b_ref, o_ref, acc_ref):
        k = pl.program_id(2)
        part = _dot_nt(a_ref[...], b_ref[...]) if b_transposed else jnp.dot(a_ref[...], b_ref[...],
                                                                          preferred_element_type=F32)
        if nk == 1:
            o_ref[...] = part.astype(o_ref.dtype)
        else:
            @pl.when(k == 0)
            def _():
                acc_ref[...] = part

            @pl.when(k > 0)
            def _():
                acc_ref[...] += part

            @pl.when(k == nk - 1)
            def _():
                o_ref[...] = acc_ref[...].astype(o_ref.dtype)

    grid = (N // tn, S // ts, nk)

    def when():
        at = [pl.program_id(d) for d in range(3)]
        first = (at[0] == 0) & (at[1] == 0) & (at[2] == 0)
        last = (at[0] == grid[0] - 1) & (at[1] == grid[1] - 1) & (at[2] == grid[2] - 1)
        return first, last, last

    xin, xout, xshape, xsem = _comm_specs(comm)
    res = _call(
        _fuse_comm(body, 2, 1, comm, when), name=name, grid=grid,
        in_specs=[pl.BlockSpec((ts, tk), lambda j, i, k: (i, k)), b_spec] + xin,
        out_specs=[pl.BlockSpec((ts, tn), lambda j, i, k: (i, j))] + xout,
        out_shape=[jax.ShapeDtypeStruct((S, N), out_dtype)] + xshape,
        scratch=[pltpu.VMEM((ts, tn) if nk > 1 else (8, 128), F32)] + xsem,
        sem=("arbitrary", "arbitrary", "arbitrary") if comm else ("parallel", "parallel", "arbitrary"),
        )(a, b, *(comm[1] if comm else []))
    return (res[0], list(res[1:])) if comm else res[0]


def _mm_tn(x, y, *, name, tn=None, ycol=0, n=None):
    S, M = x.shape
    n = n or y.shape[1]
    ts = _tile(S, 4 * ROW_TILE)
    tn = _tile(n, tn or n)
    nj = n // tn

    def body(x_ref, y_ref, o_ref):
        @pl.when(pl.program_id(1) == 0)
        def _():
            o_ref[...] = jnp.zeros_like(o_ref)

        o_ref[...] += lax.dot_general(x_ref[...], y_ref[...], (((0,), (0,)), ((), ())), preferred_element_type=F32)

    return _call(
        body, name=name, grid=(nj, S // ts),
        in_specs=[pl.BlockSpec((ts, M), lambda j, s: (s, 0)), pl.BlockSpec((ts, tn), lambda j, s: (s, ycol * nj + j))],
        out_specs=pl.BlockSpec((M, tn), lambda j, s: (0, j)),
        out_shape=jax.ShapeDtypeStruct((M, n), F32),
        sem=("parallel", "arbitrary"))(x, y)


def _norm_fwd(x, g):
    S = x.shape[0]
    ts = _tile(S, ROW_TILE)

    def body(x_ref, g_ref, h_ref):
        h_ref[...] = _mx(_rms_fwd(x_ref[...], g_ref[...]))

    return _call(
        body, name="norm_fwd", grid=(S // ts,),
        in_specs=[pl.BlockSpec((ts, D_MODEL), lambda i: (i, 0)), pl.BlockSpec((1, D_MODEL), lambda i: (0, 0))],
        out_specs=pl.BlockSpec((ts, D_MODEL), lambda i: (i, 0)),
        out_shape=jax.ShapeDtypeStruct((S, D_MODEL), MXU_DTYPE), sem=("parallel",))(x, g)


def _norm_bwd(x, g, dh, dres):
    S = x.shape[0]
    ts = _tile(S, ROW_TILE)

    def body(x_ref, g_ref, dh_ref, dr_ref, dx_ref, st_ref):
        @pl.when(pl.program_id(0) == 0)
        def _():
            st_ref[...] = jnp.zeros_like(st_ref)

        dx, dgr = _rms_bwd(x_ref[...], g_ref[...], dh_ref[...])
        dx_ref[...] = dr_ref[...] + dx
        st_ref[0:1, :] += jnp.sum(dgr, axis=0, keepdims=True)

    row = pl.BlockSpec((ts, D_MODEL), lambda i: (i, 0))
    return _call(
        body, name="norm_bwd", grid=(S // ts,),
        in_specs=[row, pl.BlockSpec((1, D_MODEL), lambda i: (0, 0)), row, row],
        out_specs=[row, pl.BlockSpec((8, D_MODEL), lambda i: (0, 0))],
        out_shape=[jax.ShapeDtypeStruct((S, D_MODEL), F32), jax.ShapeDtypeStruct((8, D_MODEL), F32)],
        sem=("arbitrary",))(x, g, dh, dres)


def _loss_fwd_bwd(y, target):
    S = y.shape[0]
    ts = _tile(S, ROW_TILE)

    def body(y_ref, t_ref, l_ref, dy_ref):
        @pl.when(pl.program_id(0) == 0)
        def _():
            l_ref[...] = jnp.zeros_like(l_ref)

        err = y_ref[...] - t_ref[...]
        dy_ref[...] = err * (1.0 / D_MODEL)
        rowloss = jnp.mean(err * err, axis=-1, keepdims=True)
        l_ref[...] += 0.5 * jnp.sum(rowloss, axis=0, keepdims=True)

    row = pl.BlockSpec((ts, D_MODEL), lambda i: (i, 0))
    return _call(
        body, name="loss", grid=(S // ts,), in_specs=[row, row],
        out_specs=[pl.BlockSpec((8, 128), lambda i: (0, 0)), row],
        out_shape=[jax.ShapeDtypeStruct((8, 128), F32), jax.ShapeDtypeStruct((S, D_MODEL), F32)],
        sem=("arbitrary",))(y, target)


def _shift_down(x, k, halo, rows):
    y = pltpu.roll(x, k, 0)
    for j in range(k):
        y = jnp.where(rows == j, halo[8 - k + j:8 - k + j + 1, :], y)
    return y


def _shift_up(x, k, halo, rows, n):
    y = pltpu.roll(x, n - k, 0)
    for j in range(k):
        y = jnp.where(rows == n - k + j, halo[j:j + 1, :], y)
    return y


def _conv_parts(za, zh, cw, first, rows):
    ax, ab, ac, ag = za[:, 0:BW], za[:, BW:2 * BW], za[:, 2 * BW:3 * BW], za[:, 3 * BW:4 * BW]
    zz = ac * ax
    hz = jnp.where(first, 0.0, zh[:, 2 * BW:3 * BW] * zh[:, 0:BW])
    zz1 = _shift_down(zz, 1, hz, rows)
    zz2 = _shift_down(zz, 2, hz, rows)
    conv = zz2 * cw[0:1, :] + zz1 * cw[1:2, :] + zz * cw[2:3, :] + cw[3:4, :]
    return ax, ab, ac, ag, zz, zz1, zz2, conv


def _conv_fwd(z, cw):
    S = z.shape[0]
    ts = _tile(S, ROW_TILE)
    hb = ts // 8

    def body(za_ref, zh_ref, cw_ref, y_ref):
        i = pl.program_id(0)
        rows = lax.broadcasted_iota(jnp.int32, (ts, BW), 0)
        ax, ab, ac, ag, zz, zz1, zz2, conv = _conv_parts(za_ref[...], zh_ref[...], cw_ref[...], i == 0, rows)
        y_ref[...] = _mx(ab * conv * _silu(ag))

    return _call(
        body, name="conv_fwd", grid=(S // ts,),
        in_specs=[pl.BlockSpec((ts, 1024), lambda i: (i, ZA // 1024)),
                  pl.BlockSpec((8, 1024), lambda i: (jnp.maximum(i * hb - 1, 0), ZA // 1024)),
                  pl.BlockSpec((8, BW), lambda i: (0, 0))],
        out_specs=pl.BlockSpec((ts, BW), lambda i: (i, 0)),
        out_shape=jax.ShapeDtypeStruct((S, BW), MXU_DTYPE), sem=("parallel",))(z, z, cw)


def _conv_bwd(z, cw, dy, dz):
    S = z.shape[0]
    ts = _tile(S, ROW_TILE)
    hb = ts // 8
    nt = S // ts

    def body(za_ref, zh_ref, zn_ref, cw_ref, dy_ref, dyn_ref, dz_in, dz_ref, st_ref):
        i = pl.program_id(0)

        @pl.when(i == 0)
        def _():
            st_ref[...] = jnp.zeros_like(st_ref)

        cw = cw_ref[...]
        rows = lax.broadcasted_iota(jnp.int32, (ts, BW), 0)
        ax, ab, ac, ag, zz, zz1, zz2, conv = _conv_parts(za_ref[...], zh_ref[...], cw, i == 0, rows)
        dy = dy_ref[...]
        sg = _silu(ag)
        dc = dy * ab * sg
        zn = zn_ref[...]
        dcn = jnp.where(i == nt - 1, 0.0, dyn_ref[...] * zn[:, BW:2 * BW] * _silu(zn[:, 3 * BW:4 * BW]))
        dc1 = _shift_up(dc, 1, dcn, rows, ts)
        dc2 = _shift_up(dc, 2, dcn, rows, ts)
        dzz = dc * cw[2:3, :] + dc1 * cw[1:2, :] + dc2 * cw[0:1, :]
        dz_ref[:, 0:BW] = _mx(dzz * ac)
        dz_ref[:, BW:2 * BW] = _mx(dy * conv * sg)
        dz_ref[:, 2 * BW:3 * BW] = _mx(dzz * ax)
        dz_ref[:, 3 * BW:4 * BW] = _mx(dy * ab * conv * _dsilu(ag))
        st_ref[0:1, :] += jnp.sum(dc * zz2, axis=0, keepdims=True)
        st_ref[1:2, :] += jnp.sum(dc * zz1, axis=0, keepdims=True)
        st_ref[2:3, :] += jnp.sum(dc * zz, axis=0, keepdims=True)
        st_ref[3:4, :] += jnp.sum(dc, axis=0, keepdims=True)

    return _call(
        body, name="conv_bwd", grid=(nt,),
        in_specs=[pl.BlockSpec((ts, 1024), lambda i: (i, ZA // 1024)),
                  pl.BlockSpec((8, 1024), lambda i: (jnp.maximum(i * hb - 1, 0), ZA // 1024)),
                  pl.BlockSpec((8, 1024), lambda i: (jnp.minimum((i + 1) * hb, S // 8 - 1), ZA // 1024)),
                  pl.BlockSpec((8, BW), lambda i: (0, 0)),
                  pl.BlockSpec((ts, BW), lambda i: (i, 0)),
                  pl.BlockSpec((8, BW), lambda i: (jnp.minimum((i + 1) * hb, S // 8 - 1), 0)), _ANY],
        out_specs=[pl.BlockSpec((ts, 1024), lambda i: (i, ZA // 1024)), pl.BlockSpec((8, BW), lambda i: (0, 0))],
        out_shape=[jax.ShapeDtypeStruct((S, ZCOLS), MXU_DTYPE), jax.ShapeDtypeStruct((8, BW), F32)],
        sem=("arbitrary",), aliases={6: 0})(z, z, z, cw, dy, dy, dz)


AW = 128
_AUG = DH


def _split3_f32(x):
    hi = _top16(x)
    r1 = x - hi
    mid = _top16(r1)
    return hi, mid, r1 - mid


def _aug_lanes(n, cols):
    lane = lax.broadcasted_iota(jnp.int32, (n, DH), 1)
    out = jnp.zeros((n, DH), F32)
    for e, c in enumerate(cols):
        out = jnp.where(lane == e, c, out)
    return out


def _put_aug(ref, h, col, parts=None, const=None):
    base = h * AW + _AUG + col
    if parts is None:
        ref[:, base:base + 3] = jnp.full((ref.shape[0], 3), const, ref.dtype)
    else:
        for e, part in enumerate(parts):
            ref[:, base + e:base + e + 1] = part.astype(ref.dtype)


def _attn_prep_fwd(z, fb, gq, gk):
    S = z.shape[0]
    ts = _tile(S, ROW_TILE)

    def body(zb_ref, zf_ref, fb_ref, gq_ref, gk_ref, qa_ref, ka_ref, va_ref, cc_ref, carry):
        @pl.when(pl.program_id(0) == 0)
        def _():
            carry[...] = jnp.zeros_like(carry)

        zb = zb_ref[...]
        gqv, gkv = gq_ref[...], gk_ref[...]
        lf = _log_sigmoid(zf_ref[...] + fb_ref[...])
        cum = _dot_exact_lhs01(_tri(ts), lf) + carry[...]
        carry[...] = cum[ts - 1:ts, :]
        cc_ref[...] = cum
        pieces = _split3_f32(cum)
        for h in range(NH):
            ph = [pc[:, h:h + 1] for pc in pieces]
            qh = _rms_fwd(zb[:, _hs(h)], gqv) * (DH ** -0.5)
            kh = _rms_fwd(zb[:, BW + h * DH:BW + (h + 1) * DH], gkv)
            vh = zb[:, 2 * BW + h * DH:2 * BW + (h + 1) * DH]
            qa_ref[:, _ah(h)] = _mx(jnp.concatenate([qh, _aug_lanes(ts, ph + [1.0] * 3)], axis=1))
            ka_ref[:, _ah(h)] = _mx(jnp.concatenate([kh, _aug_lanes(ts, [1.0] * 3 + [-x for x in ph])], axis=1))
            va_ref[:, _ah(h)] = _mx(jnp.concatenate([vh, _aug_lanes(ts, [-1.0] * 3)], axis=1))

    row = lambda w: pl.BlockSpec((ts, w), lambda i: (i, 0))
    return _call(
        body, name="attn_prep_fwd", grid=(S // ts,),
        in_specs=[pl.BlockSpec((ts, 1024), lambda i: (i, ZB // 1024)), pl.BlockSpec((ts, 128), lambda i: (i, ZF // 128)),
                  pl.BlockSpec((1, 128), lambda i: (0, 0)), pl.BlockSpec((1, DH), lambda i: (0, 0)),
                  pl.BlockSpec((1, DH), lambda i: (0, 0))],
        out_specs=[row(NH * AW), row(NH * AW), row(NH * AW), row(128)],
        out_shape=[jax.ShapeDtypeStruct((S, NH * AW), MXU_DTYPE)] * 3 + [jax.ShapeDtypeStruct((S, 128), F32)],
        scratch=[pltpu.VMEM((1, 128), F32)], sem=("arbitrary",))(z, z, fb, gq, gk)


ROW_CHUNK = 256


def _square_steps(n):
    def when():
        i, j = pl.program_id(0), pl.program_id(1)
        return (i == 0) & (j == 0), (i == n - 1) & (j == 0), (i == n - 1) & (j == n - 1)
    return when


def _fuse_comm(core, n_in, n_out, comm, when):
    nc = 0 if comm is None else len(comm[1])

    def body(*refs):
        cin, xin = refs[:n_in], refs[n_in:n_in + nc]
        a = n_in + nc
        cout, xout = refs[a:a + n_out], refs[a + n_out:a + n_out + nc]
        rest = refs[a + n_out + nc:]
        if nc == 0:
            core(*cin, *cout, *rest)
            return
        cscr, sems = rest[:-3], rest[-3:]
        first, middle, last = when()
        phase = _COMM_PHASES[comm[0]]

        @pl.when(first)
        def _():
            phase(0, xin, xout, sems)

        core(*cin, *cout, *cscr)

        @pl.when(middle)
        def _():
            phase(1, xin, xout, sems)

        @pl.when(last)
        def _():
            phase(2, xin, xout, sems)

    return body


def _comm_specs(comm):
    if comm is None:
        return [], [], [], []
    hbm = pl.BlockSpec(memory_space=pl.ANY)
    nc = len(comm[1])
    return [hbm] * nc, [hbm] * nc, _COMM_OUT[comm[0]](comm[1]), _comm_sems(comm[0], nc)


def _ah(h):
    return slice(h * AW, (h + 1) * AW)


def _ahd(h):
    return slice(h * AW, h * AW + DH)


def _causal(shape, row0, transposed=False):
    r = row0 + lax.broadcasted_iota(jnp.int32, shape, 0)
    c = lax.broadcasted_iota(jnp.int32, shape, 1)
    return (r <= c) if transposed else (r >= c)


def _flash_fwd(qa, ka, va, z, comm=None):
    S = qa.shape[0]
    t = _tile(S, FLASH_TILE)
    n = S // t
    rch = _tile(t, ROW_CHUNK)

    def core(q_ref, k_ref, v_ref, zb_ref, o_ref, lse_ref, y_ref, m_sc, l_sc, acc):
        i, j = pl.program_id(0), pl.program_id(1)

        @pl.when(j == 0)
        def _():
            m_sc[...] = jnp.full_like(m_sc, MASK_VALUE)
            l_sc[...] = jnp.zeros_like(l_sc)
            acc[...] = jnp.zeros_like(acc)

        def block(masked):
            for h in range(NH):
                for rc in range(t // rch):
                    rows = slice(rc * rch, (rc + 1) * rch)
                    s = _dot_nt(q_ref[rows, _ah(h)], k_ref[:, _ah(h)])
                    if masked:
                        s = jnp.where(_causal(s.shape, rc * rch), s, MASK_VALUE)
                    m_old = m_sc[h, rows, :]
                    m_new = jnp.maximum(m_old, jnp.max(s, axis=-1, keepdims=True))
                    p = jnp.exp(s - m_new)
                    alpha = jnp.exp(m_old - m_new)
                    l_sc[h, rows, :] = alpha * l_sc[h, rows, :] + jnp.sum(p, axis=-1, keepdims=True)
                    acc[rows, _hs(h)] = alpha * acc[rows, _hs(h)] + _dot(p, v_ref[:, _ahd(h)])
                    m_sc[h, rows, :] = m_new

        @pl.when(j < i)
        def _():
            block(False)

        @pl.when(j == i)
        def _():
            block(True)
            lse_ref[...] = jnp.zeros_like(lse_ref)
            for h in range(NH):
                o_ref[:, _hs(h)] = acc[:, _hs(h)] / l_sc[h]
                lse_ref[:, h:h + 1] = m_sc[h] + jnp.log(l_sc[h])
            y_ref[...] = _mx(o_ref[...] * _silu(zb_ref[...]))

    qspec = lambda w: pl.BlockSpec((t, w), lambda i, j: (i, 0))
    kspec = lambda w: pl.BlockSpec((t, w), lambda i, j: (jnp.minimum(j, i), 0))
    xin, xout, xshape, xsem = _comm_specs(comm)
    res = _call(
        _fuse_comm(core, 4, 3, comm, _square_steps(n)), name="flash_fwd", grid=(n, n),
        in_specs=[qspec(NH * AW), kspec(NH * AW), kspec(NH * AW),
                  pl.BlockSpec((t, BW), lambda i, j: (i, ZB_GATE // BW))] + xin,
        out_specs=[qspec(BW), qspec(128), qspec(BW)] + xout,
        out_shape=[jax.ShapeDtypeStruct((S, BW), F32), jax.ShapeDtypeStruct((S, 128), F32),
                   jax.ShapeDtypeStruct((S, BW), MXU_DTYPE)] + xshape,
        scratch=[pltpu.VMEM((NH, t, 1), F32), pltpu.VMEM((NH, t, 1), F32), pltpu.VMEM((t, BW), F32)] + xsem,
        sem=("arbitrary", "arbitrary"))(qa, ka, va, z, *(comm[1] if comm else []))
    return res[0], res[1], res[2], list(res[3:])


def _attn_gate_bwd(dy, o, z, qa, cc, lse):
    S = dy.shape[0]
    ts = _tile(S, ROW_TILE)

    def body(dy_ref, o_ref, zb_ref, qa_ref, cc_ref, lse_ref, dg_ref, qb_ref, doa_ref):
        g = zb_ref[...]
        dy, o = dy_ref[...], o_ref[...]
        do = dy * _silu(g)
        dg_ref[...] = _mx(dy * o * _dsilu(g))
        qb_ref[...] = qa_ref[...]
        doa_ref[...] = jnp.zeros_like(doa_ref)
        shifted = _split3_f32(cc_ref[...] - lse_ref[...])
        for h in range(NH):
            doh = do[:, _hs(h)]
            doa_ref[:, _ahd(h)] = _mx(doh)
            delta = jnp.sum(doh * o[:, _hs(h)], axis=-1, keepdims=True)
            _put_aug(doa_ref, h, 0, parts=_split3_f32(delta))
            _put_aug(qb_ref, h, 0, parts=[pc[:, h:h + 1] for pc in shifted])

    row = lambda w: pl.BlockSpec((ts, w), lambda i: (i, 0))
    return _call(
        body, name="attn_gate_bwd", grid=(S // ts,),
        in_specs=[pl.BlockSpec((ts, BW), lambda i: (i, 1)), row(BW), pl.BlockSpec((ts, BW), lambda i: (i, ZB_GATE // BW)),
                  row(NH * AW), row(128), row(128)],
        out_specs=[row(BW), row(NH * AW), row(NH * AW)],
        out_shape=[jax.ShapeDtypeStruct((S, BW), MXU_DTYPE), jax.ShapeDtypeStruct((S, NH * AW), MXU_DTYPE),
                   jax.ShapeDtypeStruct((S, NH * AW), MXU_DTYPE)],
        sem=("parallel",))(dy, o, z, qa, cc, lse)


def _aug_value(ref, h, rows=slice(None)):
    base = h * AW + _AUG
    x = ref[rows, base:base + 3].astype(F32)
    return x[:, 0:1] + x[:, 1:2] + x[:, 2:3]


def _flash_bwd_dq(qb, ka, va, doa, comm=None):
    S = qb.shape[0]
    t = _tile(S, FLASH_TILE)
    n = S // t
    rch = _tile(t, ROW_CHUNK)

    def core(q_ref, k_ref, v_ref, do_ref, dq_ref, do2_ref, dr_sc):
        i, j = pl.program_id(0), pl.program_id(1)

        @pl.when(j == 0)
        def _():
            dq_ref[...] = jnp.zeros_like(dq_ref)
            dr_sc[...] = jnp.zeros_like(dr_sc)

        def block(masked):
            for h in range(NH):
                for rc in range(t // rch):
                    rows = slice(rc * rch, (rc + 1) * rch)
                    p = jnp.exp(_dot_nt(q_ref[rows, _ah(h)], k_ref[:, _ah(h)]))
                    if masked:
                        p = jnp.where(_causal(p.shape, rc * rch), p, 0.0)
                    ds = p * _dot_nt(do_ref[rows, _ah(h)], v_ref[:, _ah(h)])
                    dq_ref[rows, _hs(h)] += _dot(ds, k_ref[:, _ahd(h)])
                    dr_sc[h, rows, :] += jnp.sum(ds, axis=-1, keepdims=True)

        @pl.when(j < i)
        def _():
            block(False)

        @pl.when(j == i)
        def _():
            block(True)
            do2_ref[...] = do_ref[...]
            for h in range(NH):
                _put_aug(do2_ref, h, 0, parts=_split3_f32(_aug_value(do_ref, h) + dr_sc[h]))

    qspec = lambda w: pl.BlockSpec((t, w), lambda i, j: (i, 0))
    kspec = lambda w: pl.BlockSpec((t, w), lambda i, j: (jnp.minimum(j, i), 0))
    xin, xout, xshape, xsem = _comm_specs(comm)
    res = _call(
        _fuse_comm(core, 4, 2, comm, _square_steps(n)), name="flash_bwd_dq", grid=(n, n),
        in_specs=[qspec(NH * AW), kspec(NH * AW), kspec(NH * AW), qspec(NH * AW)] + xin,
        out_specs=[qspec(BW), qspec(NH * AW)] + xout,
        out_shape=[jax.ShapeDtypeStruct((S, BW), F32), jax.ShapeDtypeStruct((S, NH * AW), MXU_DTYPE)] + xshape,
        scratch=[pltpu.VMEM((NH, t, 1), F32)] + xsem,
        sem=("arbitrary", "arbitrary"))(qb, ka, va, doa, *(comm[1] if comm else []))
    return res[0], res[1], list(res[2:])


def _flash_bwd_dkv(qb, ka, va, doa):
    S = qb.shape[0]
    t = _tile(S, FLASH_TILE)
    n = S // t

    def body(q_ref, k_ref, v_ref, do_ref, dk_ref, dv_ref, dc_ref):
        j, i = pl.program_id(0), pl.program_id(1)

        @pl.when(i == 0)
        def _():
            dk_ref[...] = jnp.zeros_like(dk_ref)
            dv_ref[...] = jnp.zeros_like(dv_ref)
            dc_ref[...] = jnp.zeros_like(dc_ref)

        def block(masked):
            for h in range(NH):
                pt = jnp.exp(_dot_nt(k_ref[:, _ah(h)], q_ref[:, _ah(h)]))
                if masked:
                    pt = jnp.where(_causal(pt.shape, 0, transposed=True), pt, 0.0)
                dst = pt * _dot_nt(v_ref[:, _ah(h)], do_ref[:, _ah(h)])
                dv_ref[:, _hs(h)] += _dot(pt, do_ref[:, _ahd(h)])
                dk_ref[:, _hs(h)] += _dot(dst, q_ref[:, _ahd(h)])
                dc_ref[:, h:h + 1] += -jnp.sum(dst, axis=-1, keepdims=True)

        @pl.when(i > j)
        def _():
            block(False)

        @pl.when(i == j)
        def _():
            block(True)

    qspec = lambda w: pl.BlockSpec((t, w), lambda j, i: (jnp.maximum(i, j), 0))
    kspec = lambda w: pl.BlockSpec((t, w), lambda j, i: (j, 0))
    return _call(
        body, name="flash_bwd_dkv", grid=(n, n),
        in_specs=[qspec(NH * AW), kspec(NH * AW), kspec(NH * AW), qspec(NH * AW)],
        out_specs=[kspec(BW), kspec(BW), kspec(128)],
        out_shape=[jax.ShapeDtypeStruct((S, BW), F32), jax.ShapeDtypeStruct((S, BW), F32),
                   jax.ShapeDtypeStruct((S, 128), F32)],
        sem=("parallel", "arbitrary"))(qb, ka, va, doa)


def _attn_prep_bwd(z, fb, gq, gk, dq, dk, dv, dgate, dcc, dz):
    S = z.shape[0]
    ts = _tile(S, ROW_TILE)
    nt = S // ts

    def body(zb_ref, zf_ref, fb_ref, gq_ref, gk_ref, dq_ref, dk_ref, dv_ref, dg_ref, dcc_ref, dz_in, dzb_ref, dzf_ref,
             st_ref, carry):
        @pl.when(pl.program_id(0) == 0)
        def _():
            carry[...] = jnp.zeros_like(carry)
            st_ref[...] = jnp.zeros_like(st_ref)

        zb = zb_ref[...]
        gqv, gkv = gq_ref[...], gk_ref[...]
        dqv, dkv = dq_ref[...], dk_ref[...]
        sq = jnp.zeros((1, DH), F32)
        sk = jnp.zeros((1, DH), F32)
        for h in range(NH):
            dx, dgr = _rms_bwd(zb[:, _hs(h)], gqv, dqv[:, _hs(h)] * (DH ** -0.5))
            dzb_ref[:, _hs(h)] = _mx(dx)
            sq = sq + jnp.sum(dgr, axis=0, keepdims=True)
            ks = slice(BW + h * DH, BW + (h + 1) * DH)
            dx, dgr = _rms_bwd(zb[:, ks], gkv, dkv[:, _hs(h)])
            dzb_ref[:, ks] = _mx(dx)
            sk = sk + jnp.sum(dgr, axis=0, keepdims=True)
        dzb_ref[:, 2 * BW:3 * BW] = _mx(dv_ref[...])
        dzb_ref[:, 3 * BW:4 * BW] = dg_ref[...]
        dc = dcc_ref[...]
        dlf = _dot_exact_lhs01(_tri(ts, upper=True), dc) + carry[...]
        carry[...] = dlf[0:1, :]
        dfz = dlf * _sigmoid(-(zf_ref[...] + fb_ref[...]))
        dzf_ref[...] = _mx(dfz)
        st_ref[0:1, 0:DH] += sq
        st_ref[1:2, 0:DH] += sk
        st_ref[2:3, :] += jnp.sum(dfz, axis=0, keepdims=True)

    rev = lambda w, c=0: pl.BlockSpec((ts, w), lambda i: (nt - 1 - i, c))
    one = lambda w: pl.BlockSpec((1, w), lambda i: (0, 0))
    return _call(
        body, name="attn_prep_bwd", grid=(nt,),
        in_specs=[rev(1024, ZB // 1024), rev(128, ZF // 128), one(128), one(DH), one(DH),
                  rev(BW), rev(BW), rev(BW), rev(BW), rev(128), _ANY],
        out_specs=[rev(1024, ZB // 1024), rev(128), pl.BlockSpec((8, 128), lambda i: (0, 0))],
        out_shape=[jax.ShapeDtypeStruct((S, ZCOLS), MXU_DTYPE), jax.ShapeDtypeStruct((S, 128), MXU_DTYPE),
                   jax.ShapeDtypeStruct((8, 128), F32)],
        scratch=[pltpu.VMEM((1, 128), F32)], sem=("arbitrary",), aliases={10: 0})(z, z, fb, gq, gk, dq, dk, dv, dgate, dcc, dz)


def _put_fgate_cols(dzf, dz):
    S = dzf.shape[0]
    ts = _tile(S, ROW_TILE)

    def body(f_ref, dz_in, o_ref):
        o_ref[...] = f_ref[...]

    return _call(
        body, name="put_fgate_cols", grid=(S // ts,),
        in_specs=[pl.BlockSpec((ts, 128), lambda i: (i, 0)), _ANY],
        out_specs=pl.BlockSpec((ts, 128), lambda i: (i, ZF // 128)),
        out_shape=jax.ShapeDtypeStruct((S, ZCOLS), MXU_DTYPE), sem=("parallel",), aliases={1: 0})(dzf, dz)


def _hgrn_consts():
    C = CHUNK
    t = np.arange(C)[:, None]
    j = np.arange(C)[None, :]
    masks = []
    for lev in range(NLEV):
        m = C >> (lev + 1)
        blk, pos = t // (2 * m), t % (2 * m)
        sblk, spos = j // (2 * m), j % (2 * m)
        masks.append((blk == sblk) & (pos >= m) & (spos < m))
    masks.append(t == j)
    return (j <= t).astype(np.float32), np.stack(masks).astype(np.float32)


def _level_exponents(b, lg):
    row = lax.broadcasted_iota(jnp.int32, (CHUNK, 1), 0)
    eqs, eks = [], []
    for lev in range(NLEV):
        m = CHUNK >> (lev + 1)
        pos = jnp.bitwise_and(row, 2 * m - 1)
        if 2 * m >= 8:
            b3 = b.reshape(CHUNK // (2 * m), 2 * m, b.shape[1])
            mid = jnp.broadcast_to(b3[:, m - 1:m, :], b3.shape).reshape(b.shape)
            eqs.append(jnp.where(pos >= m, b - mid, 0.0))
            eks.append(jnp.where(pos < m, mid - b, 0.0))
        elif m == 2:
            eqs.append(jnp.where(pos == 2, lg, jnp.where(pos == 3, lg + pltpu.roll(lg, 1, 0), 0.0)))
            eks.append(jnp.where(pos == 0, pltpu.roll(lg, CHUNK - 1, 0), 0.0))
        else:
            eqs.append(jnp.where(pos == 1, lg, 0.0))
            eks.append(jnp.zeros_like(lg))
    return jnp.concatenate([b] + eqs + eks, axis=0)


def _hgrn_chunk_fwd(zc, lb, tmat):
    cq, cf, ci = zc[:, 0:BW], zc[:, BW:2 * BW], zc[:, 2 * BW:3 * BW]
    q = _silu(cq)
    sg = _sigmoid(cf)
    g = lb + (1.0 - lb) * sg
    lg = jnp.log(g)
    kf = (1.0 - lb) * _sigmoid(-cf)
    b = _dot_exact_lhs01(tmat, lg)
    e = _level_exponents(b, lg)
    blast = b[CHUNK - 1:CHUNK, :]
    return cq, cf, q, sg, g, kf, ci, e, b, blast


def _round_mx(x):
    if MXU_DTYPE != jnp.bfloat16:
        return x
    u = lax.bitcast_convert_type(x, jnp.uint32)
    u = (u + jnp.uint32(0x7FFF) + ((u >> 16) & jnp.uint32(1))) & jnp.uint32(0xFFFF0000)
    return lax.bitcast_convert_type(u, F32)


def _hgrn_scores(q, kf, e, masks, h, scores=True):
    qh, kh = q[:, _hs(h)], kf[:, _hs(h)]
    ql, kl = [], []
    a = None
    for lev in range(NLEV + 1):
        if lev < NLEV:
            eq = jnp.exp(e[(1 + lev) * CHUNK:(2 + lev) * CHUNK, _hs(h)])
            ek = jnp.exp(e[(1 + NLEV + lev) * CHUNK:(2 + NLEV + lev) * CHUNK, _hs(h)])
            ql.append((_round_mx(qh * eq), eq))
            kl.append((_round_mx(kh * ek), ek))
        else:
            ql.append((_round_mx(qh), None))
            kl.append((_round_mx(kh), None))
        if scores:
            term = masks[lev] * _dot_nt(ql[-1][0], kl[-1][0])
            a = term if a is None else a + term
    return a, ql, kl


def _hgrn_fwd(z, lb, gain, tmat, masks):
    S = z.shape[0]
    ts = _tile(S, ROW_TILE)
    nc = ts // CHUNK

    def body(zc_ref, lb_ref, gn_ref, tm_ref, mk_ref, o_ref, st_ref, a_ref, y_ref, state):
        @pl.when(pl.program_id(0) == 0)
        def _():
            state[...] = jnp.zeros_like(state)

        lbv, gn, tm, mk = lb_ref[...], gn_ref[...], tm_ref[...], mk_ref[...]

        def chunk(c, carry):
            r0 = pl.multiple_of(c * CHUNK, CHUNK)
            zc = zc_ref[pl.ds(r0, CHUNK), :]
            cq, cf, q, sg, g, kf, v, e, b, blast = _hgrn_chunk_fwd(zc, lbv, tm)
            qe = q * jnp.exp(b)
            kd = kf * jnp.exp(blast - b)
            st_ref[pl.ds(r0, CHUNK), :] = state[...]
            for h in range(NH):
                sth = state[:, _hs(h)]
                a = _mx(_hgrn_scores(q, kf, e, mk, h)[0])
                a_ref[pl.ds(r0, CHUNK), _hs(h)] = a
                oh = _dot_nt(qe[:, _hs(h)], sth) + _dot(a, v[:, _hs(h)])
                state[:, _hs(h)] = sth * jnp.exp(blast[:, _hs(h)]) + _dot_tn(v[:, _hs(h)], kd[:, _hs(h)])
                o_ref[pl.ds(r0, CHUNK), _hs(h)] = oh
                yn = _rms_fwd(oh, gn[:, _hs(h)])
                y_ref[pl.ds(r0, CHUNK), _hs(h)] = _mx(yn * _silu(zc[:, 3 * BW + h * DH:3 * BW + (h + 1) * DH]))
            return carry

        lax.fori_loop(0, nc, chunk, 0, unroll=min(4, nc))

    row = pl.BlockSpec((ts, BW), lambda i: (i, 0))
    one = pl.BlockSpec((1, BW), lambda i: (0, 0))
    return _call(
        body, name="hgrn_fwd", grid=(S // ts,),
        in_specs=[pl.BlockSpec((ts, 1024), lambda i: (i, ZC // 1024)), one, one,
                  pl.BlockSpec(tmat.shape, lambda i: (0, 0)), pl.BlockSpec(masks.shape, lambda i: (0, 0, 0))],
        out_specs=[row, row, row, row],
        out_shape=[jax.ShapeDtypeStruct((S, BW), F32), jax.ShapeDtypeStruct((S, BW), F32),
                   jax.ShapeDtypeStruct((S, BW), MXU_DTYPE), jax.ShapeDtypeStruct((S, BW), MXU_DTYPE)],
        scratch=[pltpu.VMEM((CHUNK, BW), F32)], sem=("arbitrary",))(z, lb, gain, tmat, masks)


def _hgrn_bwd(z, lb, gain, tmat, masks, o_pre, states, scores, dy, dz):
    S = z.shape[0]
    ts = _tile(S, ROW_TILE)
    nt = S // ts
    nc = ts // CHUNK

    def body(zc_ref, lb_ref, gn_ref, tm_ref, mk_ref, o_ref, st_ref, a_ref, dy_ref, dz_in, dz_ref, stat_ref, dstate):
        @pl.when(pl.program_id(0) == 0)
        def _():
            dstate[...] = jnp.zeros_like(dstate)
            stat_ref[...] = jnp.zeros_like(stat_ref)

        lbv, gn, tm, mk = lb_ref[...], gn_ref[...], tm_ref[...], mk_ref[...]
        upper = _tri(CHUNK, upper=True)
        lower_strict = 1.0 - upper

        def chunk(cc, carry):
            c = nc - 1 - cc
            r0 = pl.multiple_of(c * CHUNK, CHUNK)
            zc = zc_ref[pl.ds(r0, CHUNK), :]
            cq, cf, q, sg, g, kf, v, e, b, blast = _hgrn_chunk_fwd(zc, lbv, tm)
            eb = jnp.exp(b)
            ebl = jnp.exp(blast - b)
            qe = q * eb
            kd = kf * ebl
            o = o_ref[pl.ds(r0, CHUNK), :]
            dyv = dy_ref[pl.ds(r0, CHUNK), :]
            stp = st_ref[pl.ds(r0, CHUNK), :]
            cg = zc[:, 3 * BW:4 * BW]
            sgate = _silu(cg)
            dq_parts, dk_parts, dv_parts, dcg_parts = [], [], [], []
            dgain, up_parts, lo_parts, const_parts = [], [], [], []
            for h in range(NH):
                hs = _hs(h)
                oh = o[:, hs]
                r = lax.rsqrt(jnp.mean(oh * oh, axis=-1, keepdims=True) + EPS)
                nrm = oh * r
                dyn = dyv[:, hs] * sgate[:, hs]
                dcg_parts.append(dyv[:, hs] * nrm * gn[:, hs] * _dsilu(cg[:, hs]))
                dgain.append(jnp.sum(dyn * nrm, axis=0, keepdims=True))
                tt = dyn * gn[:, hs]
                doh = r * (tt - nrm * jnp.mean(tt * nrm, axis=-1, keepdims=True))
                _, ql, kl = _hgrn_scores(q, kf, e, mk, h, scores=False)
                a = a_ref[pl.ds(r0, CHUNK), hs]
                da = _dot_nt(doh, v[:, hs])
                dsth = dstate[:, hs]
                ebh = jnp.exp(blast[:, hs])
                dv_parts.append(_dot_tn(a, doh) + _dot_nt(kd[:, hs], dsth))
                dq_inter = eb[:, hs] * _dot(doh, stp[:, hs])
                dk_state = ebl[:, hs] * _dot(v[:, hs], dsth)
                dqh, dkh, gh = dq_inter, dk_state, None
                for lev in range(NLEV + 1):
                    dal = mk[lev] * da
                    xq = _dot(dal, kl[lev][0])
                    yk = _dot_tn(dal, ql[lev][0])
                    gterm = ql[lev][0] * xq - kl[lev][0] * yk
                    gh = gterm if gh is None else gh + gterm
                    dqh = dqh + (xq if lev == NLEV else ql[lev][1] * xq)
                    dkh = dkh + (yk if lev == NLEV else kl[lev][1] * yk)
                up_parts.append(gh + q[:, hs] * dq_inter)
                lo_parts.append(kf[:, hs] * dk_state)
                const_parts.append(jnp.sum(dsth * stp[:, hs], axis=0, keepdims=True) * ebh)
                dstate[:, hs] = dsth * ebh + _dot_tn(doh, qe[:, hs])
                dq_parts.append(dqh)
                dk_parts.append(dkh)
            dq = jnp.concatenate(dq_parts, axis=1)
            dk = jnp.concatenate(dk_parts, axis=1)
            dlg = (_dot_exact_lhs01(upper, jnp.concatenate(up_parts, axis=1))
                   + _dot_exact_lhs01(lower_strict, jnp.concatenate(lo_parts, axis=1))
                   + jnp.concatenate(const_parts, axis=1))
            dsg = sg * (1.0 - sg)
            dz_ref[pl.ds(r0, CHUNK), 0:BW] = _mx(dq * _dsilu(cq))
            dz_ref[pl.ds(r0, CHUNK), BW:2 * BW] = _mx((dlg / g - dk) * (1.0 - lbv) * dsg)
            dz_ref[pl.ds(r0, CHUNK), 2 * BW:3 * BW] = _mx(jnp.concatenate(dv_parts, axis=1))
            dz_ref[pl.ds(r0, CHUNK), 3 * BW:4 * BW] = _mx(jnp.concatenate(dcg_parts, axis=1))
            stat_ref[0:1, :] += jnp.concatenate(dgain, axis=1)
            stat_ref[1:2, :] += jnp.sum((dlg / g - dk) * (1.0 - sg), axis=0, keepdims=True)
            return carry

        lax.fori_loop(0, nc, chunk, 0, unroll=min(4, nc))

    rev = lambda w, c=0: pl.BlockSpec((ts, w), lambda i: (nt - 1 - i, c))
    one = pl.BlockSpec((1, BW), lambda i: (0, 0))
    return _call(
        body, name="hgrn_bwd", grid=(nt,),
        in_specs=[rev(1024, ZC // 1024), one, one, pl.BlockSpec(tmat.shape, lambda i: (0, 0)),
                  pl.BlockSpec(masks.shape, lambda i: (0, 0, 0)), rev(BW), rev(BW), rev(BW), rev(BW, 2), _ANY],
        out_specs=[rev(1024, ZC // 1024), pl.BlockSpec((8, BW), lambda i: (0, 0))],
        out_shape=[jax.ShapeDtypeStruct((S, ZCOLS), MXU_DTYPE), jax.ShapeDtypeStruct((8, BW), F32)],
        scratch=[pltpu.VMEM((CHUNK, BW), F32)],
        sem=("arbitrary",), aliases={9: 0})(z, lb, gain, tmat, masks, o_pre, states, scores, dy, dz)


def _lower_bounds_fwd(lb_logits):
    def body(l_ref, o_ref):
        l = l_ref[...]
        m = jnp.max(l, axis=0, keepdims=True)
        ex = jnp.exp(l - m)
        p = ex / jnp.sum(ex, axis=0, keepdims=True)
        cs = p[0:1, :]
        o_ref[0:1, :] = jnp.clip(cs - p[0:1, :], 0.0, 1.0)
        for d in range(1, DEPTH):
            cs = cs + p[d:d + 1, :]
            o_ref[d:d + 1, :] = jnp.clip(cs - p[0:1, :], 0.0, 1.0)

    full = pl.BlockSpec((DEPTH, BW), lambda: (0, 0))
    return _call(body, name="lower_bounds_fwd", grid=(), in_specs=[full], out_specs=full,
                 out_shape=jax.ShapeDtypeStruct((DEPTH, BW), F32))(lb_logits)


def _lower_bounds_bwd(lb_logits, dlow):
    def body(l_ref, d_ref, o_ref):
        l = l_ref[...]
        m = jnp.max(l, axis=0, keepdims=True)
        ex = jnp.exp(l - m)
        p = ex / jnp.sum(ex, axis=0, keepdims=True)
        dl = d_ref[...]
        cs = p[0:1, :]
        dcs = []
        for d in range(DEPTH):
            if d > 0:
                cs = cs + p[d:d + 1, :]
            val = cs - p[0:1, :]
            dcs.append(jnp.where((val > 0.0) & (val < 1.0), dl[d:d + 1, :], 0.0))
        total = dcs[0] + dcs[1] + dcs[2] + dcs[3]
        dp = []
        for j in range(DEPTH):
            s = dcs[j]
            for d in range(j + 1, DEPTH):
                s = s + dcs[d]
            dp.append(s - total if j == 0 else s)
        inner = p[0:1, :] * dp[0]
        for j in range(1, DEPTH):
            inner = inner + p[j:j + 1, :] * dp[j]
        for j in range(DEPTH):
            o_ref[j:j + 1, :] = p[j:j + 1, :] * (dp[j] - inner)

    full = pl.BlockSpec((DEPTH, BW), lambda: (0, 0))
    return _call(body, name="lower_bounds_bwd", grid=(), in_specs=[full, full], out_specs=full,
                 out_shape=jax.ShapeDtypeStruct((DEPTH, BW), F32))(lb_logits, dlow)


def _sgu_fwd(z, gv, ws, bs):
    S = z.shape[0]
    ts = _tile(S, ROW_TILE)
    nc = ts // SCHUNK

    def body(zd_ref, gv_ref, ws_ref, bs_ref, y_ref):
        gvv, bsv = gv_ref[...], bs_ref[...]
        tril = _tri(SCHUNK)
        for c in range(nc):
            rs = slice(c * SCHUNK, (c + 1) * SCHUNK)
            zd = zd_ref[rs, :]
            for h in range(NH):
                vn = _rms_fwd(zd[:, BW + h * DH:BW + (h + 1) * DH], gvv[:, _hs(h)])
                s = _dot(ws_ref[h] * tril, vn) + bsv[:, h:h + 1]
                y_ref[rs, _hs(h)] = _mx(zd[:, _hs(h)] * s * _silu(zd[:, 2 * BW + h * DH:2 * BW + (h + 1) * DH]))

    return _call(
        body, name="sgu_fwd", grid=(S // ts,),
        in_specs=[pl.BlockSpec((ts, 1024), lambda i: (i, ZD // 1024)), pl.BlockSpec((1, BW), lambda i: (0, 0)),
                  pl.BlockSpec((NH, SCHUNK, SCHUNK), lambda i: (0, 0, 0)), pl.BlockSpec((SCHUNK, 128), lambda i: (0, 0))],
        out_specs=pl.BlockSpec((ts, BW), lambda i: (i, 0)),
        out_shape=jax.ShapeDtypeStruct((S, BW), MXU_DTYPE), sem=("parallel",))(z, gv, ws, bs)


def _sgu_bwd(z, gv, ws, bs, dy, dz):
    S = z.shape[0]
    ts = _tile(S, ROW_TILE)
    nc = ts // SCHUNK

    def body(zd_ref, gv_ref, ws_ref, bs_ref, dy_ref, dz_in, dz_ref, dws_ref, dbs_ref, st_ref):
        @pl.when(pl.program_id(0) == 0)
        def _():
            dws_ref[...] = jnp.zeros_like(dws_ref)
            dbs_ref[...] = jnp.zeros_like(dbs_ref)
            st_ref[...] = jnp.zeros_like(st_ref)

        gvv, bsv = gv_ref[...], bs_ref[...]
        tril = _tri(SCHUNK)
        dz_ref[:, 3 * BW:4 * BW] = jnp.zeros((ts, BW), MXU_DTYPE)
        for c in range(nc):
            rs = slice(c * SCHUNK, (c + 1) * SCHUNK)
            zd = zd_ref[rs, :]
            dyv = dy_ref[rs, :]
            for h in range(NH):
                hs = _hs(h)
                u = zd[:, hs]
                vraw = zd[:, BW + h * DH:BW + (h + 1) * DH]
                gt = zd[:, 2 * BW + h * DH:2 * BW + (h + 1) * DH]
                gvh = gvv[:, hs]
                vn = _rms_fwd(vraw, gvh)
                wm = ws_ref[h] * tril
                s = _dot(wm, vn) + bsv[:, h:h + 1]
                sil = _silu(gt)
                d = dyv[:, hs]
                ds = d * u * sil
                dz_ref[rs, hs] = _mx(d * s * sil)
                dz_ref[rs, 2 * BW + h * DH:2 * BW + (h + 1) * DH] = _mx(d * u * s * _dsilu(gt))
                dws_ref[h] += tril * _dot_nt(ds, vn)
                dbs_ref[:, h:h + 1] += jnp.sum(ds, axis=-1, keepdims=True)
                dvn = _dot_tn(wm, ds)
                dx, dgr = _rms_bwd(vraw, gvh, dvn)
                dz_ref[rs, BW + h * DH:BW + (h + 1) * DH] = _mx(dx)
                st_ref[0:1, hs] += jnp.sum(dgr, axis=0, keepdims=True)

    return _call(
        body, name="sgu_bwd", grid=(S // ts,),
        in_specs=[pl.BlockSpec((ts, 1024), lambda i: (i, ZD // 1024)), pl.BlockSpec((1, BW), lambda i: (0, 0)),
                  pl.BlockSpec((NH, SCHUNK, SCHUNK), lambda i: (0, 0, 0)), pl.BlockSpec((SCHUNK, 128), lambda i: (0, 0)),
                  pl.BlockSpec((ts, BW), lambda i: (i, 3)), _ANY],
        out_specs=[pl.BlockSpec((ts, 1024), lambda i: (i, ZD // 1024)), pl.BlockSpec((NH, SCHUNK, SCHUNK), lambda i: (0, 0, 0)),
                   pl.BlockSpec((SCHUNK, 128), lambda i: (0, 0)), pl.BlockSpec((8, BW), lambda i: (0, 0))],
        out_shape=[jax.ShapeDtypeStruct((S, ZCOLS), MXU_DTYPE), jax.ShapeDtypeStruct((NH, SCHUNK, SCHUNK), F32),
                   jax.ShapeDtypeStruct((SCHUNK, 128), F32), jax.ShapeDtypeStruct((8, BW), F32)],
        sem=("arbitrary",), aliases={5: 0})(z, gv, ws, bs, dy, dz)


def _merge_fwd(x, z, ys, wup, mb, wo):
    S = x.shape[0]
    ts = _tile(S, MERGE_TILE)

    def body(x_ref, zm_ref, ya_ref, yb_ref, yc_ref, yd_ref, wup_ref, mb_ref, wo_ref, x1_ref, mg_ref):
        yrefs = (ya_ref, yb_ref, yc_ref, yd_ref)
        mbv = mb_ref[...]
        merged = None
        for b in range(NBR):
            cs = slice(b * D_MODEL, (b + 1) * D_MODEL)
            term = _sigmoid(zm_ref[:, cs] + mbv[b:b + 1, :]) * jnp.dot(yrefs[b][...], wup_ref[b],
                                                                      preferred_element_type=F32)
            merged = term if merged is None else merged + term
        mg = _mx(merged)
        mg_ref[...] = mg
        x1_ref[...] = x_ref[...] + jnp.dot(mg, wo_ref[...], preferred_element_type=F32)

    row = lambda w: pl.BlockSpec((ts, w), lambda i: (i, 0))
    return _call(
        body, name="merge_fwd", grid=(S // ts,),
        in_specs=[row(D_MODEL), pl.BlockSpec((ts, 4096), lambda i: (i, 0)), row(BW), row(BW), row(BW), row(BW),
                  pl.BlockSpec((NBR, BW, D_MODEL), lambda i: (0, 0, 0)), pl.BlockSpec((NBR, D_MODEL), lambda i: (0, 0)),
                  pl.BlockSpec((D_MODEL, D_MODEL), lambda i: (0, 0))],
        out_specs=[row(D_MODEL), row(D_MODEL)],
        out_shape=[jax.ShapeDtypeStruct((S, D_MODEL), F32), jax.ShapeDtypeStruct((S, D_MODEL), MXU_DTYPE)],
        sem=("parallel",))(x, z, *ys, wup, mb, wo)


def _merge_bwd(dx1, z, ys, wup, mb, wo):
    S = dx1.shape[0]
    ts = _tile(S, MERGE_TILE)

    def body(dx_ref, zm_ref, ya_ref, yb_ref, yc_ref, yd_ref, wup_ref, mb_ref, wo_ref,
             dzm_ref, du_ref, dxb_ref, dy_ref, st_ref):
        @pl.when(pl.program_id(0) == 0)
        def _():
            st_ref[...] = jnp.zeros_like(st_ref)

        yrefs = (ya_ref, yb_ref, yc_ref, yd_ref)
        mbv = mb_ref[...]
        dxb = _mx(dx_ref[...])
        dxb_ref[...] = dxb
        dmerged = _dot_nt(dxb, wo_ref[...])
        for b in range(NBR):
            cs = slice(b * D_MODEL, (b + 1) * D_MODEL)
            u = jnp.dot(yrefs[b][...], wup_ref[b], preferred_element_type=F32)
            sg = _sigmoid(zm_ref[:, cs] + mbv[b:b + 1, :])
            du = _mx(dmerged * sg)
            du_ref[:, cs] = du
            dzm = dmerged * u * sg * (1.0 - sg)
            dzm_ref[:, cs] = _mx(dzm)
            st_ref[b:b + 1, :] += jnp.sum(dzm, axis=0, keepdims=True)
            dy_ref[:, b * BW:(b + 1) * BW] = _dot_nt(du, wup_ref[b])

    row = lambda w: pl.BlockSpec((ts, w), lambda i: (i, 0))
    return _call(
        body, name="merge_bwd", grid=(S // ts,),
        in_specs=[row(D_MODEL), pl.BlockSpec((ts, 4096), lambda i: (i, 0)), row(BW), row(BW), row(BW), row(BW),
                  pl.BlockSpec((NBR, BW, D_MODEL), lambda i: (0, 0, 0)),
                  pl.BlockSpec((NBR, D_MODEL), lambda i: (0, 0)), pl.BlockSpec((D_MODEL, D_MODEL), lambda i: (0, 0))],
        out_specs=[row(4096), row(4096), row(D_MODEL), row(D_MODEL), pl.BlockSpec((8, D_MODEL), lambda i: (0, 0))],
        out_shape=[jax.ShapeDtypeStruct((S, ZCOLS), MXU_DTYPE), jax.ShapeDtypeStruct((S, 4096), MXU_DTYPE),
                   jax.ShapeDtypeStruct((S, D_MODEL), MXU_DTYPE), jax.ShapeDtypeStruct((S, D_MODEL), F32),
                   jax.ShapeDtypeStruct((8, D_MODEL), F32)],
        sem=("arbitrary",))(dx1, z, *ys, wup, mb, wo)


def _ple_fwd(x1, p, g, wg, wp):
    S = x1.shape[0]
    ts = _tile(S, ROW_TILE)

    def body(x_ref, p_ref, g_ref, wg_ref, wp_ref, o_ref):
        x = x_ref[...]
        hp = _mx(_rms_fwd(x, g_ref[...]))
        gate = _sigmoid(jnp.dot(hp, wg_ref[...], preferred_element_type=F32))
        pp = jnp.dot(_mx(p_ref[...]), wp_ref[...], preferred_element_type=F32)
        o_ref[...] = x + gate * pp

    row = lambda w: pl.BlockSpec((ts, w), lambda i: (i, 0))
    return _call(
        body, name="ple_fwd", grid=(S // ts,),
        in_specs=[row(D_MODEL), row(PLE), pl.BlockSpec((1, D_MODEL), lambda i: (0, 0)),
                  pl.BlockSpec((D_MODEL, D_MODEL), lambda i: (0, 0)), pl.BlockSpec((PLE, D_MODEL), lambda i: (0, 0))],
        out_specs=row(D_MODEL), out_shape=jax.ShapeDtypeStruct((S, D_MODEL), F32),
        sem=("parallel",))(x1, p, g, wg, wp)


def _ple_bwd(x1, p, dx2, g, wg, wp):
    S = x1.shape[0]
    ts = _tile(S, ROW_TILE)

    def body(x_ref, p_ref, dx_ref, g_ref, wg_ref, wp_ref, dx1_ref, hp_ref, dgl_ref, dpp_ref, pb_ref, st_ref):
        @pl.when(pl.program_id(0) == 0)
        def _():
            st_ref[...] = jnp.zeros_like(st_ref)

        x, gv, dx2 = x_ref[...], g_ref[...], dx_ref[...]
        hp = _mx(_rms_fwd(x, gv))
        hp_ref[...] = hp
        gate = _sigmoid(jnp.dot(hp, wg_ref[...], preferred_element_type=F32))
        pb = _mx(p_ref[...])
        pb_ref[...] = pb
        pp = jnp.dot(pb, wp_ref[...], preferred_element_type=F32)
        dgl = _mx(dx2 * pp * gate * (1.0 - gate))
        dgl_ref[...] = dgl
        dpp_ref[...] = _mx(dx2 * gate)
        dhp = _dot_nt(dgl, wg_ref[...])
        dxn, dgr = _rms_bwd(x, gv, dhp)
        dx1_ref[...] = dx2 + dxn
        st_ref[0:1, :] += jnp.sum(dgr, axis=0, keepdims=True)

    row = lambda w: pl.BlockSpec((ts, w), lambda i: (i, 0))
    sq = pl.BlockSpec((D_MODEL, D_MODEL), lambda i: (0, 0))
    return _call(
        body, name="ple_bwd", grid=(S // ts,),
        in_specs=[row(D_MODEL), row(PLE), row(D_MODEL), pl.BlockSpec((1, D_MODEL), lambda i: (0, 0)), sq,
                  pl.BlockSpec((PLE, D_MODEL), lambda i: (0, 0))],
        out_specs=[row(D_MODEL), row(D_MODEL), row(D_MODEL), row(D_MODEL), row(PLE),
                   pl.BlockSpec((8, D_MODEL), lambda i: (0, 0))],
        out_shape=[jax.ShapeDtypeStruct((S, D_MODEL), F32)] + [jax.ShapeDtypeStruct((S, D_MODEL), MXU_DTYPE)] * 3
        + [jax.ShapeDtypeStruct((S, PLE), MXU_DTYPE), jax.ShapeDtypeStruct((8, D_MODEL), F32)],
        sem=("arbitrary",))(x1, p, dx2, g, wg, wp)


def _pad_rows(a, rows=8):
    return jnp.concatenate([a, jnp.zeros((rows - a.shape[0],) + a.shape[1:], a.dtype)], axis=0)


def _pad_lanes(a, lanes=128):
    return jnp.concatenate([a, jnp.zeros(a.shape[:-1] + (lanes - a.shape[-1],), a.dtype)], axis=-1)


def _wz_from_w_in(w):
    zeros = lambda n: jnp.zeros((w.shape[0], n), w.dtype)
    return jnp.concatenate([w[:, _OM:_OEND], w[:, _OA:_OB], w[:, _OB:_OF], w[:, _OC:_OD], w[:, _OD:_OM], zeros(256),
                            w[:, _OF:_OC], zeros(124)], axis=1)


def _w_in_from_wz(g):
    return jnp.concatenate([g[:, ZA:ZB], g[:, ZB:ZC], g[:, ZF:ZF + 4], g[:, ZC:ZD], g[:, ZD:ZD + 768], g[:, ZM:ZA]], axis=1)


_W_IN_GROUPS = [(_OA, _OB, ZA), (_OB, _OF, ZB), (_OF, _OC, ZF), (_OC, _OD, ZC), (_OD, _OM, ZD), (_OM, _OEND, ZM)]


def _wz_from_shards(g):
    n = g.shape[-1]
    pieces, pos = [], 0
    for a, b, zs in sorted(_W_IN_GROUPS, key=lambda grp: grp[2]):
        if zs > pos:
            pieces.append(jnp.zeros((g.shape[1], zs - pos), g.dtype))
        for k in range(4):
            lo, hi = max(a, k * n), min(b, (k + 1) * n)
            if lo < hi:
                pieces.append(g[k][:, lo - k * n:hi - k * n])
        pos = zs + (b - a)
    pieces.append(jnp.zeros((g.shape[1], ZCOLS - pos), g.dtype))
    return jnp.concatenate(pieces, axis=1)


def _w_in_slabs_from_wz(g):
    n = _OEND // 4
    slabs = []
    for k in range(4):
        pieces = []
        for a, b, zs in _W_IN_GROUPS:
            lo, hi = max(a, k * n), min(b, (k + 1) * n)
            if lo < hi:
                pieces.append(g[:, zs + lo - a:zs + hi - a])
        slabs.append(jnp.concatenate(pieces, axis=1))
    return jnp.stack(slabs)


def _local_step(x, p, target, wts, dist=None):
    tmat_np, masks_np = _hgrn_consts()
    tmat = jnp.asarray(tmat_np, MXU_DTYPE)
    masks = jnp.asarray(masks_np, F32)
    lower = _lower_bounds_fwd(wts["lb_logits"])
    saved = []
    for li in range(DEPTH):
        big = dist.weights(li) if dist else {n: wts[n][li] for n in _GATHERED}
        wz = _wz_from_shards(big["w_in"]) if big["w_in"].ndim == 3 else _wz_from_w_in(big["w_in"])
        g_mix = wts["norm_mix"][li][None, :]
        h = _norm_fwd(x, g_mix)
        z = _mm_nn(h, wz, F32, name="mm_z", tn=1664)
        cw = _pad_rows(jnp.concatenate([big["conv_w"], wts["conv_b"][li][None, :]], axis=0))
        ya = _conv_fwd(z, cw)
        fb = _pad_lanes(wts["fgate_bias"][li][None, :])
        gq, gk = wts["q_norm"][li][None, :], wts["k_norm"][li][None, :]
        qa, ka, va, cc = _attn_prep_fwd(z, fb, gq, gk)
        o, lse, yb, got = _flash_fwd(qa, ka, va, z, comm=dist.fwd_comm(li) if dist else None)
        if dist:
            dist.fwd_done(li, got)
        lb = lower[li][None, :]
        gh = wts["hgrn_norm"][li][None, :]
        o_pre, states, hscores, yc = _hgrn_fwd(z, lb, gh, tmat, masks)
        gv = wts["sgu_norm"][li][None, :]
        ws = wts["spatial_w"][li]
        bs = _pad_lanes(wts["spatial_b"][li].T)
        yd = _sgu_fwd(z, gv, ws, bs)
        ys = (ya, yb, yc, yd)
        x1, merged = _merge_fwd(x, z, ys, big["w_up"], big["merge_b"], big["w_o"])
        g_ple = wts["norm_ple"][li][None, :]
        x2 = _ple_fwd(x1, p[li], g_ple, big["w_ple_gate"], big["w_ple_proj"])
        saved.append(dict(x=x, h=h, z=z, wz=wz, cw=cw, fb=fb, gq=gq, gk=gk, qa=qa, ka=ka, va=va, cc=cc, o=o, lse=lse,
                          lb=lb, gh=gh, o_pre=o_pre, states=states, hscores=hscores, gv=gv, ws=ws, bs=bs, ys=ys, x1=x1, merged=merged,
                          g_mix=g_mix, g_ple=g_ple, big=big))
        x = x2

    loss, dx = _loss_fwd_bwd(x, target)

    names = ["norm_mix", "w_in", "conv_w", "conv_b", "fgate_bias", "q_norm", "k_norm", "lb", "hgrn_norm", "sgu_norm",
             "spatial_w", "spatial_b", "w_up", "merge_b", "w_o", "norm_ple", "w_ple_gate", "w_ple_proj"]
    gl = {n: [None] * DEPTH for n in names}
    for li in reversed(range(DEPTH)):
        s = saved[li]
        z, big = s["z"], s["big"]
        wg, wp = big["w_ple_gate"], big["w_ple_proj"]
        dx1, hp, dgl, dpp, pb, st = _ple_bwd(s["x1"], p[li], dx, s["g_ple"], wg, wp)
        gl["norm_ple"][li] = st[0]
        gl["w_ple_gate"][li] = _mm_tn(hp, dgl, name="mm_dwg")
        gl["w_ple_proj"][li] = _mm_tn(pb, dpp, name="mm_dwp")
        wup, wo = big["w_up"], big["w_o"]
        dz, du, dxb, dy, st = _merge_bwd(dx1, z, s["ys"], wup, big["merge_b"], wo)
        gl["merge_b"][li] = st[0:NBR]
        gl["w_o"][li] = _mm_tn(s["merged"], dxb, name="mm_dwo")
        gl["w_up"][li] = jnp.stack([_mm_tn(s["ys"][b], du, name="mm_dwup", ycol=b, n=D_MODEL) for b in range(NBR)])
        dz, st = _conv_bwd(z, s["cw"], dy, dz)
        gl["conv_w"][li] = st[0:3]
        gl["conv_b"][li] = st[3]
        dgate, qb, doa = _attn_gate_bwd(dy, s["o"], z, s["qa"], s["cc"], s["lse"])
        dq, doa, got = _flash_bwd_dq(qb, s["ka"], s["va"], doa, comm=dist.bwd_comm() if dist else None)
        if dist:
            dist.bwd_done(got)
        dk, dv, dcc = _flash_bwd_dkv(qb, s["ka"], s["va"], doa)
        dz, dzf, st = _attn_prep_bwd(z, s["fb"], s["gq"], s["gk"], dq, dk, dv, dgate, dcc, dz)
        dz = _put_fgate_cols(dzf, dz)
        gl["q_norm"][li] = st[0, 0:DH]
        gl["k_norm"][li] = st[1, 0:DH]
        gl["fgate_bias"][li] = st[2, 0:NH]
        dz, st = _hgrn_bwd(z, s["lb"], s["gh"], tmat, masks, s["o_pre"], s["states"], s["hscores"], dy, dz)
        gl["hgrn_norm"][li] = st[0]
        gl["lb"][li] = st[1]
        dz, dws, dbs, st = _sgu_bwd(z, s["gv"], s["ws"], s["bs"], dy, dz)
        gl["sgu_norm"][li] = st[0]
        gl["spatial_w"][li] = dws
        gl["spatial_b"][li] = dbs[:, 0:NH].T
        dwz = _mm_tn(s["h"], dz, name="mm_dwz", tn=1664)
        gl["w_in"][li] = dwz if dist else _w_in_from_wz(dwz)
        if dist:
            dist.push(li, {n: gl[n][li] for n in _BIG})
        if dist and li == 0:
            dh, got = _mm_nn(dz, s["wz"], F32, name="mm_dh", ts=ROW_TILE // 2, b_transposed=True, comm=dist.bwd_comm())
            dist.bwd_done(got)
        else:
            dh = _mm_nn(dz, s["wz"], F32, name="mm_dh", ts=ROW_TILE // 2, b_transposed=True)
        dx, st = _norm_bwd(s["x"], s["g_mix"], dh, dx1)
        gl["norm_mix"][li] = st[0]

    if dist:
        dist.finish()
    grads = {n: jnp.stack(v) for n, v in gl.items() if not (dist and n in _BIG)}
    grads["lb_logits"] = _lower_bounds_bwd(wts["lb_logits"], grads.pop("lb"))
    return loss, dx, grads


def _my_pos():
    return lax.axis_index("x"), lax.axis_index("y"), lax.axis_index("c")


def _gather_phase(phase, ins, outs, sems):
    if phase == 1:
        return
    send, recv, lsem = sems
    x, y, c = _my_pos()
    me = 2 * x + y
    peers = [(1 - x, y), (x, 1 - y), (1 - x, 1 - y)]
    copies = []
    for t in range(len(ins)):
        copies.append(pltpu.make_async_copy(ins[t], outs[t].at[me], lsem.at[t]))
        for j, (px, py) in enumerate(peers):
            copies.append(pltpu.make_async_remote_copy(
                src_ref=ins[t], dst_ref=outs[t].at[me], send_sem=send.at[t, j], recv_sem=recv.at[t, j],
                device_id=(px, py, c), device_id_type=MESH))
    for cp in copies:
        if phase == 0:
            cp.start()
        else:
            cp.wait()


N_HALVED = 5


def _gather_halves_phase(phase, ins, outs, sems):
    send, recv, lsem = sems
    x, y, c = _my_pos()
    me = 2 * x + y
    sib = (x, y, 1 - c)
    chips = [(1 - x, y), (x, 1 - y), (1 - x, 1 - y)]

    def rc(t, k, src, dst, dev):
        return pltpu.make_async_remote_copy(src_ref=src, dst_ref=dst, send_sem=send.at[t, k], recv_sem=recv.at[t, k],
                                            device_id=dev, device_id_type=MESH)

    for t in range(len(ins)):
        local = pltpu.make_async_copy(ins[t], outs[t].at[me], lsem.at[t])
        if t >= N_HALVED:
            whole = [rc(t, j, ins[t], outs[t].at[me], (px, py, c)) for j, (px, py) in enumerate(chips)]
            for cp in [local] + whole:
                if phase == 0:
                    cp.start()
                elif phase == 2:
                    cp.wait()
            continue
        hr = ins[t].shape[0] // 2
        mine, other = pl.ds(c * hr, hr), pl.ds((1 - c) * hr, hr)
        first = [rc(t, j, ins[t].at[mine], outs[t].at[me, mine], (px, py, c)) for j, (px, py) in enumerate(chips)]
        landed = [outs[t].at[2 * px + py, mine] for px, py in chips]
        passed = [rc(t, 3 + j, slot, slot, sib) for j, slot in enumerate(landed)]
        if phase == 0:
            local.start()
            for cp in first:
                cp.start()
        elif phase == 1:
            for j, slot in enumerate(landed):
                rc(t, j, slot, slot, (x, y, c)).wait_recv()
                passed[j].start()
        else:
            for j, (px, py) in enumerate(chips):
                slot = outs[t].at[2 * px + py, other]
                rc(t, 3 + j, slot, slot, (x, y, c)).wait_recv()
            for cp in first + passed:
                cp.wait_send()
            local.wait()


def _exchange_phase(phase, ins, outs, sems):
    send, recv, lsem = sems
    x, y, c = _my_pos()
    me = 2 * x + y
    sib = (x, y, 1 - c)
    chips = [(1 - x, y), (x, 1 - y), (1 - x, 1 - y)]

    def rc(t, k, src, dst, dev):
        return pltpu.make_async_remote_copy(src_ref=src, dst_ref=dst, send_sem=send.at[t, k], recv_sem=recv.at[t, k],
                                            device_id=dev, device_id_type=MESH)

    for t in range(len(ins)):
        local = pltpu.make_async_copy(ins[t].at[me], outs[t].at[2 * me + c], lsem.at[t])
        first = [rc(t, 0, ins[t].at[me], outs[t].at[2 * me + c], sib)]
        first += [rc(t, 1 + j, ins[t].at[2 * px + py], outs[t].at[2 * me + c], (px, py, c))
                  for j, (px, py) in enumerate(chips)]
        landed = [outs[t].at[2 * (2 * px + py) + c] for px, py in chips]
        passed = [rc(t, 4 + j, slot, slot, sib) for j, slot in enumerate(landed)]
        if phase == 0:
            local.start()
            for cp in first:
                cp.start()
        elif phase == 1:
            for j, slot in enumerate(landed):
                rc(t, 1 + j, slot, slot, (x, y, c)).wait_recv()
                passed[j].start()
        else:
            s0 = outs[t].at[2 * me + (1 - c)]
            rc(t, 0, s0, s0, (x, y, c)).wait_recv()
            for j, (px, py) in enumerate(chips):
                slot = outs[t].at[2 * (2 * px + py) + (1 - c)]
                rc(t, 4 + j, slot, slot, (x, y, c)).wait_recv()
            for cp in first + passed:
                cp.wait_send()
            local.wait()


_COMM_PHASES = {"gather": _gather_phase, "gather_halves": _gather_halves_phase, "exchange": _exchange_phase}
_gathered_shapes = lambda arrays: [jax.ShapeDtypeStruct((4,) + a.shape, a.dtype) for a in arrays]
_COMM_OUT = {"gather": _gathered_shapes, "gather_halves": _gathered_shapes,
             "exchange": lambda arrays: [jax.ShapeDtypeStruct((8,) + a.shape[1:], a.dtype) for a in arrays]}


def _comm_sems(kind, nc):
    k = {"gather": 3, "gather_halves": 6, "exchange": 7}[kind]
    return [pltpu.SemaphoreType.DMA((nc, k)), pltpu.SemaphoreType.DMA((nc, k)), pltpu.SemaphoreType.DMA((nc,))]


def _comm_alone(kind, arrays, name):
    nc = len(arrays)

    def body(*refs):
        for phase in range(3):
            _COMM_PHASES[kind](phase, refs[:nc], refs[nc:2 * nc], refs[2 * nc:])

    hbm = pl.BlockSpec(memory_space=pl.ANY)
    return pl.pallas_call(
        functools.partial(body), name=name, in_specs=[hbm] * nc, out_specs=[hbm] * nc,
        out_shape=_COMM_OUT[kind](arrays), scratch_shapes=_comm_sems(kind, nc))(*arrays)


def _allreduce_small(v):
    R = v.shape[0]

    def body(v_ref, o_ref, pair, chips, send, recv):
        x, y, c = _my_pos()
        me = 2 * x + y
        pair[c] = v_ref[...]
        swap = pltpu.make_async_remote_copy(src_ref=v_ref, dst_ref=pair.at[c], send_sem=send.at[0], recv_sem=recv.at[0],
                                            device_id=(x, y, 1 - c), device_id_type=MESH)
        swap.start()
        swap.wait()
        chips[me] = pair[0] + pair[1]
        copies = [pltpu.make_async_remote_copy(src_ref=chips.at[me], dst_ref=chips.at[me], send_sem=send.at[1 + j],
                                               recv_sem=recv.at[1 + j], device_id=(px, py, c), device_id_type=MESH)
                  for j, (px, py) in enumerate([(1 - x, y), (x, 1 - y), (1 - x, 1 - y)])]
        for cp in copies:
            cp.start()
        for cp in copies:
            cp.wait()
        o_ref[...] = (chips[0] + chips[1]) + (chips[2] + chips[3])

    vm = pl.BlockSpec(memory_space=pltpu.VMEM)
    return pl.pallas_call(
        functools.partial(body), name="allreduce_small", in_specs=[vm], out_specs=vm,
        out_shape=jax.ShapeDtypeStruct((R, 128), F32),
        scratch_shapes=[pltpu.VMEM((2, R, 128), F32), pltpu.VMEM((4, R, 128), F32),
                        pltpu.SemaphoreType.DMA((4,)), pltpu.SemaphoreType.DMA((4,))],
        compiler_params=pltpu.CompilerParams(vmem_limit_bytes=VMEM_LIMIT))(v)


def _adamw(w, m, v, parts, nparts):
    A, R, C = w.shape
    per_layer = isinstance(parts, (list, tuple))
    tr = _tile(R, 64) if per_layer else (R if R <= 128 else _tile(R, 128))
    nparr = len(parts) if per_layer else 1

    def body(*refs):
        w_ref, m_ref, v_ref = refs[:3]
        p_refs = refs[3:3 + nparr]
        g_ref, d_ref, nm_ref, nv_ref = refs[3 + nparr:]

        def update(read):
            g = read(0).astype(F32)
            for k in range(1, nparts):
                g = g + read(k).astype(F32)
            mm = ADAM_B1 * m_ref[0] + (1.0 - ADAM_B1) * g
            vv = ADAM_B2 * v_ref[0] + (1.0 - ADAM_B2) * jnp.square(g)
            m_hat = mm / (1.0 - ADAM_B1 ** ADAM_STEP)
            v_hat = vv / (1.0 - ADAM_B2 ** ADAM_STEP)
            g_ref[0] = g
            d_ref[0] = -ADAM_LR * (m_hat / (jnp.sqrt(v_hat) + ADAM_EPS) + ADAM_WD * w_ref[0])
            nm_ref[0] = mm
            nv_ref[0] = vv

        if not per_layer:
            update(lambda k: p_refs[0][k, 0])
        else:
            for a in range(A):
                @pl.when(pl.program_id(0) == a)
                def _(a=a):
                    update(lambda k: p_refs[a][k])

    blk = pl.BlockSpec((1, tr, C), lambda a, r: (a, r, 0))
    if per_layer:
        pspecs = [pl.BlockSpec((nparts, tr, C), functools.partial(lambda a, r, l: (0, jnp.where(a == l, r, 0), 0), l=l))
                  for l in range(A)]
        pargs = list(parts)
    else:
        pspecs = [pl.BlockSpec((nparts, 1, tr, C), lambda a, r: (0, a, r, 0))]
        pargs = [parts]
    return _call(
        body, name="adamw", grid=(A, R // tr), in_specs=[blk, blk, blk] + pspecs,
        out_specs=[blk] * 4, out_shape=[jax.ShapeDtypeStruct((A, R, C), F32)] * 4,
        sem=("arbitrary", "arbitrary"))(w, m, v, *pargs)


def _rows128(a):
    width = a.shape[-1]
    a2 = a.reshape(-1, width)
    k = -(-width // 128)
    if width % 128:
        a2 = jnp.pad(a2, ((0, 0), (0, k * 128 - width)))
    return a2.reshape(-1, 128)


def _from_rows128(r, shape):
    width = shape[-1]
    k = -(-width // 128)
    return r.reshape(-1, k * 128)[:, :width].reshape(shape)


def _as3d(a):
    if a.ndim == 2:
        return a[None]
    if a.ndim == 3:
        return a
    return a.reshape((-1,) + a.shape[-2:])


_WEIGHTS = ["norm_mix", "w_in", "conv_w", "conv_b", "fgate_bias", "q_norm", "k_norm", "lb_logits", "hgrn_norm", "sgu_norm",
            "spatial_w", "spatial_b", "w_up", "merge_b", "w_o", "norm_ple", "w_ple_gate", "w_ple_proj"]
_BIG = ["w_in", "w_up", "w_o", "w_ple_gate", "w_ple_proj"]
_GATHERED = _BIG + ["conv_w", "merge_b"]
_SHARD_AXIS = {"w_in": 2, "w_up": 3, "w_o": 1, "w_ple_gate": 1, "w_ple_proj": 2, "conv_w": 2, "merge_b": 2}
_SMALL = [n for n in _WEIGHTS if n not in _BIG]


class _Dist:
    def __init__(self, w):
        self.w = w
        self.full = {0: self._unpack(_comm_alone("gather_halves", self._shards(0), "gather_weights"))}
        self.contrib = {}
        self.pending = None

    def _shards(self, l):
        return [_mx(self.w[n][l]) for n in _BIG] + [self.w["conv_w"][l], self.w["merge_b"][l]]

    def _unpack(self, gathered):
        return {n: g if n == "w_in" else jnp.concatenate([g[k] for k in range(4)], axis=_SHARD_AXIS[n] - 1)
                for n, g in zip(_GATHERED, gathered)}

    def weights(self, l):
        return self.full[l]

    def fwd_comm(self, l):
        return ("gather", self._shards(l + 1)) if l + 1 < DEPTH else None

    def fwd_done(self, l, got):
        if got:
            self.full[l + 1] = self._unpack(got)

    def push(self, l, grads):
        self.pending = (l, [(_w_in_slabs_from_wz(grads[n]) if n == "w_in" else
                             jnp.stack(jnp.split(grads[n], 4, axis=_SHARD_AXIS[n] - 1))).astype(GRAD_WIRE_DTYPE)
                            for n in _BIG])

    def bwd_comm(self):
        return ("exchange", self.pending[1]) if self.pending else None

    def bwd_done(self, got):
        if got:
            self.contrib[self.pending[0]] = got
            self.pending = None

    def finish(self):
        if self.pending:
            self.contrib[self.pending[0]] = _comm_alone("exchange", self.pending[1], "exchange_grads")
            self.pending = None


def kernel(x, p, norm_mix, w_in, conv_w, conv_b, fgate_bias, q_norm, k_norm, lb_logits, hgrn_norm, sgu_norm, spatial_w, spatial_b, w_up, merge_b, w_o, norm_ple, w_ple_gate, w_ple_proj, loss_target, m_norm_mix, m_w_in, m_conv_w, m_conv_b, m_fgate_bias, m_q_norm, m_k_norm, m_lb_logits, m_hgrn_norm, m_sgu_norm, m_spatial_w, m_spatial_b, m_w_up, m_merge_b, m_w_o, m_norm_ple, m_w_ple_gate, m_w_ple_proj, v_norm_mix, v_w_in, v_conv_w, v_conv_b, v_fgate_bias, v_q_norm, v_k_norm, v_lb_logits, v_hgrn_norm, v_sgu_norm, v_spatial_w, v_spatial_b, v_w_up, v_merge_b, v_w_o, v_norm_ple, v_w_ple_gate, v_w_ple_proj):
    loc = dict(locals())
    w = {n: loc[n] for n in _WEIGHTS}
    m = {n: loc["m_" + n] for n in _WEIGHTS}
    v = {n: loc["v_" + n] for n in _WEIGHTS}
    chip = 2 * lax.axis_index("x") + lax.axis_index("y")

    dist = _Dist(w)
    loss_blk, dx, grads = _local_step(x[0], p[:, 0], loss_target[0], w, dist)
    loss = lax.psum(loss_blk[0, 0], ("x", "y", "c"))

    blocks = [_rows128(grads[n]) for n in _SMALL]
    nrows = sum(b.shape[0] for b in blocks)
    packed = jnp.concatenate(blocks + [jnp.zeros(((-nrows) % 8, 128), F32)], axis=0)
    red = _allreduce_small(packed)
    small, off = {}, 0
    for n, b in zip(_SMALL, blocks):
        small[n] = _from_rows128(red[off:off + b.shape[0]], grads[n].shape)
        off += b.shape[0]
    for n in ("conv_w", "merge_b"):
        ax = _SHARD_AXIS[n]
        width = small[n].shape[ax] // 4
        small[n] = lax.dynamic_slice_in_dim(small[n], chip * width, width, axis=ax)

    out_g, out_d, out_m, out_v = {}, {}, {}, {}
    for n in _WEIGHTS:
        shp = w[n].shape
        if n in _BIG:
            w3, m3, v3 = (a.reshape((DEPTH, -1, shp[-1])) for a in (w[n], m[n], v[n]))
            parts = [dist.contrib[l][_BIG.index(n)].reshape((8,) + w3.shape[1:]) for l in range(DEPTH)]
            g, d, nm, nv = _adamw(w3, m3, v3, parts, 8)
        else:
            g, d, nm, nv = _adamw(_as3d(w[n]), _as3d(m[n]), _as3d(v[n]), _as3d(small[n])[None], 1)
        out_g[n], out_d[n], out_m[n], out_v[n] = (a.reshape(shp) for a in (g, d, nm, nv))

    return (loss, dx[None], *[out_g[n] for n in _WEIGHTS], *[out_d[n] for n in _WEIGHTS],
            *[out_m[n] for n in _WEIGHTS], *[out_v[n] for n in _WEIGHTS])
```

```python
import functools

import numpy as np
import jax
import jax.numpy as jnp
from jax import lax
from jax.experimental import pallas as pl
from jax.experimental.pallas import tpu as pltpu

F32 = jnp.float32
MXU_DTYPE = jnp.bfloat16

D_MODEL = 1024
BW = 256
NH = 4
DH = 64
DEPTH = 4
NBR = 4
PLE = 256
CHUNK = 64
SCHUNK = 128
EPS = 1e-6
MASK_VALUE = -1e30
NLEV = 6

ADAM_LR, ADAM_B1, ADAM_B2, ADAM_EPS, ADAM_WD, ADAM_STEP = 0.001, 0.9, 0.999, 1e-08, 0.01, 10

ZM, ZA, ZB, ZC, ZD, ZF = 0, 4096, 5120, 6144, 7168, 8192
ZCOLS = 8320
ZB_GATE = ZB + 3 * BW
_OA, _OB, _OF, _OC, _OD, _OM, _OEND = 0, 1024, 2048, 2052, 3076, 3844, 7940

VMEM_LIMIT = 56 * 1024 * 1024
ROW_TILE = 512
FLASH_TILE = 1024
GRAD_WIRE_DTYPE = jnp.bfloat16
MERGE_TILE = 512
MESH = pl.DeviceIdType.MESH
_ANY = pl.BlockSpec(memory_space=pl.ANY)


def _tile(n, pref):
    t = min(n, pref)
    assert n % t == 0, (n, t)
    return t


def _call(body, *, name, grid, in_specs, out_specs, out_shape, scratch=(), sem=None, aliases=None):
    return pl.pallas_call(
        functools.partial(body), name=name, grid=grid, in_specs=in_specs, out_specs=out_specs,
        out_shape=out_shape, scratch_shapes=list(scratch), input_output_aliases=aliases or {},
        compiler_params=pltpu.CompilerParams(dimension_semantics=sem, vmem_limit_bytes=VMEM_LIMIT))


def _mx(x):
    return x.astype(MXU_DTYPE)


def _dot(a, b):
    return jnp.dot(_mx(a), _mx(b), preferred_element_type=F32)


def _dot_nt(a, b):
    return lax.dot_general(_mx(a), _mx(b), (((1,), (1,)), ((), ())), preferred_element_type=F32)


def _dot_tn(a, b):
    return lax.dot_general(_mx(a), _mx(b), (((0,), (0,)), ((), ())), preferred_element_type=F32)


def _top16(x):
    u = lax.bitcast_convert_type(x, jnp.uint32) & jnp.uint32(0xFFFF0000)
    return lax.bitcast_convert_type(u, F32)


def _split3(x):
    hi = _top16(x)
    r1 = x - hi
    mid = _top16(r1)
    return _mx(hi), _mx(mid), _mx(r1 - mid)


def _dot_exact_lhs01(t, x):
    hi, mid, lo = _split3(x)
    t = _mx(t)
    return (jnp.dot(t, hi, preferred_element_type=F32) + jnp.dot(t, mid, preferred_element_type=F32)
            + jnp.dot(t, lo, preferred_element_type=F32))


def _sigmoid(x):
    return jax.nn.sigmoid(x)


def _silu(x):
    return x * _sigmoid(x)


def _dsilu(x):
    s = _sigmoid(x)
    return s * (1.0 + x * (1.0 - s))


def _log_sigmoid(x):
    return jnp.minimum(x, 0.0) - jnp.log(1.0 + jnp.exp(-jnp.abs(x)))


def _rms_fwd(x, g):
    r = lax.rsqrt(jnp.mean(x * x, axis=-1, keepdims=True) + EPS)
    return x * r * g


def _rms_bwd(x, g, dy):
    r = lax.rsqrt(jnp.mean(x * x, axis=-1, keepdims=True) + EPS)
    n = x * r
    t = dy * g
    dx = r * (t - n * jnp.mean(t * n, axis=-1, keepdims=True))
    return dx, dy * n


def _tri(n, upper=False):
    r = lax.broadcasted_iota(jnp.int32, (n, n), 0)
    c = lax.broadcasted_iota(jnp.int32, (n, n), 1)
    return jnp.where((c >= r) if upper else (r >= c), 1.0, 0.0).astype(F32)


def _hs(h):
    return slice(h * DH, (h + 1) * DH)


def _mm_nn(a, b, out_dtype, *, name, ts=None, tn=None, tk=None, b_transposed=False, comm=None):
    S, K = a.shape
    N = b.shape[0] if b_transposed else b.shape[1]
    ts = _tile(S, ts or ROW_TILE)
    tn = _tile(N, tn or N)
    tk = _tile(K, tk or K)
    nk = K // tk
    b_spec = (pl.BlockSpec((tn, tk), lambda j, i, k: (j, k)) if b_transposed
              else pl.BlockSpec((tk, tn), lambda j, i, k: (k, j)))

    def body(a_ref, b_ref, o_ref, acc_ref):
        k = pl.program_id(2)
        part = _dot_nt(a_ref[...], b_ref[...]) if b_transposed else jnp.dot(a_ref[...], b_ref[...],
                                                                          preferred_element_type=F32)
        if nk == 1:
            o_ref[...] = part.astype(o_ref.dtype)
        else:
            @pl.when(k == 0)
            def _():
                acc_ref[...] = part

            @pl.when(k > 0)
            def _():
                acc_ref[...] += part

            @pl.when(k == nk - 1)
            def _():
                o_ref[...] = acc_ref[...].astype(o_ref.dtype)

    grid = (N // tn, S // ts, nk)

    def when():
        at = [pl.program_id(d) for d in range(3)]
        first = (at[0] == 0) & (at[1] == 0) & (at[2] == 0)
        last = (at[0] == grid[0] - 1) & (at[1] == grid[1] - 1) & (at[2] == grid[2] - 1)
        return first, last, last

    xin, xout, xshape, xsem = _comm_specs(comm)
    res = _call(
        _fuse_comm(body, 2, 1, comm, when), name=name, grid=grid,
        in_specs=[pl.BlockSpec((ts, tk), lambda j, i, k: (i, k)), b_spec] + xin,
        out_specs=[pl.BlockSpec((ts, tn), lambda j, i, k: (i, j))] + xout,
        out_shape=[jax.ShapeDtypeStruct((S, N), out_dtype)] + xshape,
        scratch=[pltpu.VMEM((ts, tn) if nk > 1 else (8, 128), F32)] + xsem,
        sem=("arbitrary", "arbitrary", "arbitrary") if comm else ("parallel", "parallel", "arbitrary"),
        )(a, b, *(comm[1] if comm else []))
    return (res[0], list(res[1:])) if comm else res[0]


def _mm_tn(x, y, *, name, tn=None, ycol=0, n=None):
    S, M = x.shape
    n = n or y.shape[1]
    ts = _tile(S, 4 * ROW_TILE)
    tn = _tile(n, tn or n)
    nj = n // tn

    def body(x_ref, y_ref, o_ref):
        @pl.when(pl.program_id(1) == 0)
        def _():
            o_ref[...] = jnp.zeros_like(o_ref)

        o_ref[...] += lax.dot_general(x_ref[...], y_ref[...], (((0,), (0,)), ((), ())), preferred_element_type=F32)

    return _call(
        body, name=name, grid=(nj, S // ts),
        in_specs=[pl.BlockSpec((ts, M), lambda j, s: (s, 0)), pl.BlockSpec((ts, tn), lambda j, s: (s, ycol * nj + j))],
        out_specs=pl.BlockSpec((M, tn), lambda j, s: (0, j)),
        out_shape=jax.ShapeDtypeStruct((M, n), F32),
        sem=("parallel", "arbitrary"))(x, y)


def _norm_fwd(x, g):
    S = x.shape[0]
    ts = _tile(S, ROW_TILE)

    def body(x_ref, g_ref, h_ref):
        h_ref[...] = _mx(_rms_fwd(x_ref[...], g_ref[...]))

    return _call(
        body, name="norm_fwd", grid=(S // ts,),
        in_specs=[pl.BlockSpec((ts, D_MODEL), lambda i: (i, 0)), pl.BlockSpec((1, D_MODEL), lambda i: (0, 0))],
        out_specs=pl.BlockSpec((ts, D_MODEL), lambda i: (i, 0)),
        out_shape=jax.ShapeDtypeStruct((S, D_MODEL), MXU_DTYPE), sem=("parallel",))(x, g)


def _norm_bwd(x, g, dh, dres):
    S = x.shape[0]
    ts = _tile(S, ROW_TILE)

    def body(x_ref, g_ref, dh_ref, dr_ref, dx_ref, st_ref):
        @pl.when(pl.program_id(0) == 0)
        def _():
            st_ref[...] = jnp.zeros_like(st_ref)

        dx, dgr = _rms_bwd(x_ref[...], g_ref[...], dh_ref[...])
        dx_ref[...] = dr_ref[...] + dx
        st_ref[0:1, :] += jnp.sum(dgr, axis=0, keepdims=True)

    row = pl.BlockSpec((ts, D_MODEL), lambda i: (i, 0))
    return _call(
        body, name="norm_bwd", grid=(S // ts,),
        in_specs=[row, pl.BlockSpec((1, D_MODEL), lambda i: (0, 0)), row, row],
        out_specs=[row, pl.BlockSpec((8, D_MODEL), lambda i: (0, 0))],
        out_shape=[jax.ShapeDtypeStruct((S, D_MODEL), F32), jax.ShapeDtypeStruct((8, D_MODEL), F32)],
        sem=("arbitrary",))(x, g, dh, dres)


def _loss_fwd_bwd(y, target):
    S = y.shape[0]
    ts = _tile(S, ROW_TILE)

    def body(y_ref, t_ref, l_ref, dy_ref):
        @pl.when(pl.program_id(0) == 0)
        def _():
            l_ref[...] = jnp.zeros_like(l_ref)

        err = y_ref[...] - t_ref[...]
        dy_ref[...] = err * (1.0 / D_MODEL)
        rowloss = jnp.mean(err * err, axis=-1, keepdims=True)
        l_ref[...] += 0.5 * jnp.sum(rowloss, axis=0, keepdims=True)

    row = pl.BlockSpec((ts, D_MODEL), lambda i: (i, 0))
    return _call(
        body, name="loss", grid=(S // ts,), in_specs=[row, row],
        out_specs=[pl.BlockSpec((8, 128), lambda i: (0, 0)), row],
        out_shape=[jax.ShapeDtypeStruct((8, 128), F32), jax.ShapeDtypeStruct((S, D_MODEL), F32)],
        sem=("arbitrary",))(y, target)


def _shift_down(x, k, halo, rows):
    y = pltpu.roll(x, k, 0)
    for j in range(k):
        y = jnp.where(rows == j, halo[8 - k + j:8 - k + j + 1, :], y)
    return y


def _shift_up(x, k, halo, rows, n):
    y = pltpu.roll(x, n - k, 0)
    for j in range(k):
        y = jnp.where(rows == n - k + j, halo[j:j + 1, :], y)
    return y


def _conv_parts(za, zh, cw, first, rows):
    ax, ab, ac, ag = za[:, 0:BW], za[:, BW:2 * BW], za[:, 2 * BW:3 * BW], za[:, 3 * BW:4 * BW]
    zz = ac * ax
    hz = jnp.where(first, 0.0, zh[:, 2 * BW:3 * BW] * zh[:, 0:BW])
    zz1 = _shift_down(zz, 1, hz, rows)
    zz2 = _shift_down(zz, 2, hz, rows)
    conv = zz2 * cw[0:1, :] + zz1 * cw[1:2, :] + zz * cw[2:3, :] + cw[3:4, :]
    return ax, ab, ac, ag, zz, zz1, zz2, conv


def _conv_fwd(z, cw):
    S = z.shape[0]
    ts = _tile(S, ROW_TILE)
    hb = ts // 8

    def body(za_ref, zh_ref, cw_ref, y_ref):
        i = pl.program_id(0)
        rows = lax.broadcasted_iota(jnp.int32, (ts, BW), 0)
        ax, ab, ac, ag, zz, zz1, zz2, conv = _conv_parts(za_ref[...], zh_ref[...], cw_ref[...], i == 0, rows)
        y_ref[...] = _mx(ab * conv * _silu(ag))

    return _call(
        body, name="conv_fwd", grid=(S // ts,),
        in_specs=[pl.BlockSpec((ts, 1024), lambda i: (i, ZA // 1024)),
                  pl.BlockSpec((8, 1024), lambda i: (jnp.maximum(i * hb - 1, 0), ZA // 1024)),
                  pl.BlockSpec((8, BW), lambda i: (0, 0))],
        out_specs=pl.BlockSpec((ts, BW), lambda i: (i, 0)),
        out_shape=jax.ShapeDtypeStruct((S, BW), MXU_DTYPE), sem=("parallel",))(z, z, cw)


def _conv_bwd(z, cw, dy, dz):
    S = z.shape[0]
    ts = _tile(S, ROW_TILE)
    hb = ts // 8
    nt = S // ts

    def body(za_ref, zh_ref, zn_ref, cw_ref, dy_ref, dyn_ref, dz_in, dz_ref, st_ref):
        i = pl.program_id(0)

        @pl.when(i == 0)
        def _():
            st_ref[...] = jnp.zeros_like(st_ref)

        cw = cw_ref[...]
        rows = lax.broadcasted_iota(jnp.int32, (ts, BW), 0)
        ax, ab, ac, ag, zz, zz1, zz2, conv = _conv_parts(za_ref[...], zh_ref[...], cw, i == 0, rows)
        dy = dy_ref[...]
        sg = _silu(ag)
        dc = dy * ab * sg
        zn = zn_ref[...]
        dcn = jnp.where(i == nt - 1, 0.0, dyn_ref[...] * zn[:, BW:2 * BW] * _silu(zn[:, 3 * BW:4 * BW]))
        dc1 = _shift_up(dc, 1, dcn, rows, ts)
        dc2 = _shift_up(dc, 2, dcn, rows, ts)
        dzz = dc * cw[2:3, :] + dc1 * cw[1:2, :] + dc2 * cw[0:1, :]
        dz_ref[:, 0:BW] = _mx(dzz * ac)
        dz_ref[:, BW:2 * BW] = _mx(dy * conv * sg)
        dz_ref[:, 2 * BW:3 * BW] = _mx(dzz * ax)
        dz_ref[:, 3 * BW:4 * BW] = _mx(dy * ab * conv * _dsilu(ag))
        st_ref[0:1, :] += jnp.sum(dc * zz2, axis=0, keepdims=True)
        st_ref[1:2, :] += jnp.sum(dc * zz1, axis=0, keepdims=True)
        st_ref[2:3, :] += jnp.sum(dc * zz, axis=0, keepdims=True)
        st_ref[3:4, :] += jnp.sum(dc, axis=0, keepdims=True)

    return _call(
        body, name="conv_bwd", grid=(nt,),
        in_specs=[pl.BlockSpec((ts, 1024), lambda i: (i, ZA // 1024)),
                  pl.BlockSpec((8, 1024), lambda i: (jnp.maximum(i * hb - 1, 0), ZA // 1024)),
                  pl.BlockSpec((8, 1024), lambda i: (jnp.minimum((i + 1) * hb, S // 8 - 1), ZA // 1024)),
                  pl.BlockSpec((8, BW), lambda i: (0, 0)),
                  pl.BlockSpec((ts, BW), lambda i: (i, 0)),
                  pl.BlockSpec((8, BW), lambda i: (jnp.minimum((i + 1) * hb, S // 8 - 1), 0)), _ANY],
        out_specs=[pl.BlockSpec((ts, 1024), lambda i: (i, ZA // 1024)), pl.BlockSpec((8, BW), lambda i: (0, 0))],
        out_shape=[jax.ShapeDtypeStruct((S, ZCOLS), MXU_DTYPE), jax.ShapeDtypeStruct((8, BW), F32)],
        sem=("arbitrary",), aliases={6: 0})(z, z, z, cw, dy, dy, dz)


AW = 128
_AUG = DH


def _split3_f32(x):
    hi = _top16(x)
    r1 = x - hi
    mid = _top16(r1)
    return hi, mid, r1 - mid


def _aug_lanes(n, cols):
    lane = lax.broadcasted_iota(jnp.int32, (n, DH), 1)
    out = jnp.zeros((n, DH), F32)
    for e, c in enumerate(cols):
        out = jnp.where(lane == e, c, out)
    return out


def _put_aug(ref, h, col, parts=None, const=None):
    base = h * AW + _AUG + col
    if parts is None:
        ref[:, base:base + 3] = jnp.full((ref.shape[0], 3), const, ref.dtype)
    else:
        for e, part in enumerate(parts):
            ref[:, base + e:base + e + 1] = part.astype(ref.dtype)


def _attn_prep_fwd(z, fb, gq, gk):
    S = z.shape[0]
    ts = _tile(S, ROW_TILE)

    def body(zb_ref, zf_ref, fb_ref, gq_ref, gk_ref, qa_ref, ka_ref, va_ref, cc_ref, carry):
        @pl.when(pl.program_id(0) == 0)
        def _():
            carry[...] = jnp.zeros_like(carry)

        zb = zb_ref[...]
        gqv, gkv = gq_ref[...], gk_ref[...]
        lf = _log_sigmoid(zf_ref[...] + fb_ref[...])
        cum = _dot_exact_lhs01(_tri(ts), lf) + carry[...]
        carry[...] = cum[ts - 1:ts, :]
        cc_ref[...] = cum
        pieces = _split3_f32(cum)
        for h in range(NH):
            ph = [pc[:, h:h + 1] for pc in pieces]
            qh = _rms_fwd(zb[:, _hs(h)], gqv) * (DH ** -0.5)
            kh = _rms_fwd(zb[:, BW + h * DH:BW + (h + 1) * DH], gkv)
            vh = zb[:, 2 * BW + h * DH:2 * BW + (h + 1) * DH]
            qa_ref[:, _ah(h)] = _mx(jnp.concatenate([qh, _aug_lanes(ts, ph + [1.0] * 3)], axis=1))
            ka_ref[:, _ah(h)] = _mx(jnp.concatenate([kh, _aug_lanes(ts, [1.0] * 3 + [-x for x in ph])], axis=1))
            va_ref[:, _ah(h)] = _mx(jnp.concatenate([vh, _aug_lanes(ts, [-1.0] * 3)], axis=1))

    row = lambda w: pl.BlockSpec((ts, w), lambda i: (i, 0))
    return _call(
        body, name="attn_prep_fwd", grid=(S // ts,),
        in_specs=[pl.BlockSpec((ts, 1024), lambda i: (i, ZB // 1024)), pl.BlockSpec((ts, 128), lambda i: (i, ZF // 128)),
                  pl.BlockSpec((1, 128), lambda i: (0, 0)), pl.BlockSpec((1, DH), lambda i: (0, 0)),
                  pl.BlockSpec((1, DH), lambda i: (0, 0))],
        out_specs=[row(NH * AW), row(NH * AW), row(NH * AW), row(128)],
        out_shape=[jax.ShapeDtypeStruct((S, NH * AW), MXU_DTYPE)] * 3 + [jax.ShapeDtypeStruct((S, 128), F32)],
        scratch=[pltpu.VMEM((1, 128), F32)], sem=("arbitrary",))(z, z, fb, gq, gk)


ROW_CHUNK = 256


def _square_steps(n):
    def when():
        i, j = pl.program_id(0), pl.program_id(1)
        return (i == 0) & (j == 0), (i == n - 1) & (j == 0), (i == n - 1) & (j == n - 1)
    return when


def _fuse_comm(core, n_in, n_out, comm, when):
    nc = 0 if comm is None else len(comm[1])

    def body(*refs):
        cin, xin = refs[:n_in], refs[n_in:n_in + nc]
        a = n_in + nc
        cout, xout = refs[a:a + n_out], refs[a + n_out:a + n_out + nc]
        rest = refs[a + n_out + nc:]
        if nc == 0:
            core(*cin, *cout, *rest)
            return
        cscr, sems = rest[:-3], rest[-3:]
        first, middle, last = when()
        phase = _COMM_PHASES[comm[0]]

        @pl.when(first)
        def _():
            phase(0, xin, xout, sems)

        core(*cin, *cout, *cscr)

        @pl.when(middle)
        def _():
            phase(1, xin, xout, sems)

        @pl.when(last)
        def _():
            phase(2, xin, xout, sems)

    return body


def _comm_specs(comm):
    if comm is None:
        return [], [], [], []
    hbm = pl.BlockSpec(memory_space=pl.ANY)
    nc = len(comm[1])
    return [hbm] * nc, [hbm] * nc, _COMM_OUT[comm[0]](comm[1]), _comm_sems(comm[0], nc)


def _ah(h):
    return slice(h * AW, (h + 1) * AW)


def _ahd(h):
    return slice(h * AW, h * AW + DH)


def _causal(shape, row0, transposed=False):
    r = row0 + lax.broadcasted_iota(jnp.int32, shape, 0)
    c = lax.broadcasted_iota(jnp.int32, shape, 1)
    return (r <= c) if transposed else (r >= c)


def _flash_fwd(qa, ka, va, z, comm=None):
    S = qa.shape[0]
    t = _tile(S, FLASH_TILE)
    n = S // t
    rch = _tile(t, ROW_CHUNK)

    def core(q_ref, k_ref, v_ref, zb_ref, o_ref, lse_ref, y_ref, m_sc, l_sc, acc):
        i, j = pl.program_id(0), pl.program_id(1)

        @pl.when(j == 0)
        def _():
            m_sc[...] = jnp.full_like(m_sc, MASK_VALUE)
            l_sc[...] = jnp.zeros_like(l_sc)
            acc[...] = jnp.zeros_like(acc)

        def block(masked):
            for h in range(NH):
                for rc in range(t // rch):
                    rows = slice(rc * rch, (rc + 1) * rch)
                    s = _dot_nt(q_ref[rows, _ah(h)], k_ref[:, _ah(h)])
                    if masked:
                        s = jnp.where(_causal(s.shape, rc * rch), s, MASK_VALUE)
                    m_old = m_sc[h, rows, :]
                    m_new = jnp.maximum(m_old, jnp.max(s, axis=-1, keepdims=True))
                    p = jnp.exp(s - m_new)
                    alpha = jnp.exp(m_old - m_new)
                    l_sc[h, rows, :] = alpha * l_sc[h, rows, :] + jnp.sum(p, axis=-1, keepdims=True)
                    acc[rows, _hs(h)] = alpha * acc[rows, _hs(h)] + _dot(p, v_ref[:, _ahd(h)])
                    m_sc[h, rows, :] = m_new

        @pl.when(j < i)
        def _():
            block(False)

        @pl.when(j == i)
        def _():
            block(True)
            lse_ref[...] = jnp.zeros_like(lse_ref)
            for h in range(NH):
                o_ref[:, _hs(h)] = acc[:, _hs(h)] / l_sc[h]
                lse_ref[:, h:h + 1] = m_sc[h] + jnp.log(l_sc[h])
            y_ref[...] = _mx(o_ref[...] * _silu(zb_ref[...]))

    qspec = lambda w: pl.BlockSpec((t, w), lambda i, j: (i, 0))
    kspec = lambda w: pl.BlockSpec((t, w), lambda i, j: (jnp.minimum(j, i), 0))
    xin, xout, xshape, xsem = _comm_specs(comm)
    res = _call(
        _fuse_comm(core, 4, 3, comm, _square_steps(n)), name="flash_fwd", grid=(n, n),
        in_specs=[qspec(NH * AW), kspec(NH * AW), kspec(NH * AW),
                  pl.BlockSpec((t, BW), lambda i, j: (i, ZB_GATE // BW))] + xin,
        out_specs=[qspec(BW), qspec(128), qspec(BW)] + xout,
        out_shape=[jax.ShapeDtypeStruct((S, BW), F32), jax.ShapeDtypeStruct((S, 128), F32),
                   jax.ShapeDtypeStruct((S, BW), MXU_DTYPE)] + xshape,
        scratch=[pltpu.VMEM((NH, t, 1), F32), pltpu.VMEM((NH, t, 1), F32), pltpu.VMEM((t, BW), F32)] + xsem,
        sem=("arbitrary", "arbitrary"))(qa, ka, va, z, *(comm[1] if comm else []))
    return res[0], res[1], res[2], list(res[3:])


def _attn_gate_bwd(dy, o, z, qa, cc, lse):
    S = dy.shape[0]
    ts = _tile(S, ROW_TILE)

    def body(dy_ref, o_ref, zb_ref, qa_ref, cc_ref, lse_ref, dg_ref, qb_ref, doa_ref):
        g = zb_ref[...]
        dy, o = dy_ref[...], o_ref[...]
        do = dy * _silu(g)
        dg_ref[...] = _mx(dy * o * _dsilu(g))
        qb_ref[...] = qa_ref[...]
        doa_ref[...] = jnp.zeros_like(doa_ref)
        shifted = _split3_f32(cc_ref[...] - lse_ref[...])
        for h in range(NH):
            doh = do[:, _hs(h)]
            doa_ref[:, _ahd(h)] = _mx(doh)
            delta = jnp.sum(doh * o[:, _hs(h)], axis=-1, keepdims=True)
            _put_aug(doa_ref, h, 0, parts=_split3_f32(delta))
            _put_aug(qb_ref, h, 0, parts=[pc[:, h:h + 1] for pc in shifted])

    row = lambda w: pl.BlockSpec((ts, w), lambda i: (i, 0))
    return _call(
        body, name="attn_gate_bwd", grid=(S // ts,),
        in_specs=[pl.BlockSpec((ts, BW), lambda i: (i, 1)), row(BW), pl.BlockSpec((ts, BW), lambda i: (i, ZB_GATE // BW)),
                  row(NH * AW), row(128), row(128)],
        out_specs=[row(BW), row(NH * AW), row(NH * AW)],
        out_shape=[jax.ShapeDtypeStruct((S, BW), MXU_DTYPE), jax.ShapeDtypeStruct((S, NH * AW), MXU_DTYPE),
                   jax.ShapeDtypeStruct((S, NH * AW), MXU_DTYPE)],
        sem=("parallel",))(dy, o, z, qa, cc, lse)


def _aug_value(ref, h, rows=slice(None)):
    base = h * AW + _AUG
    x = ref[rows, base:base + 3].astype(F32)
    return x[:, 0:1] + x[:, 1:2] + x[:, 2:3]


def _flash_bwd_dq(qb, ka, va, doa, comm=None):
    S = qb.shape[0]
    t = _tile(S, FLASH_TILE)
    n = S // t
    rch = _tile(t, ROW_CHUNK)

    def core(q_ref, k_ref, v_ref, do_ref, dq_ref, do2_ref, dr_sc):
        i, j = pl.program_id(0), pl.program_id(1)

        @pl.when(j == 0)
        def _():
            dq_ref[...] = jnp.zeros_like(dq_ref)
            dr_sc[...] = jnp.zeros_like(dr_sc)

        def block(masked):
            for h in range(NH):
                for rc in range(t // rch):
                    rows = slice(rc * rch, (rc + 1) * rch)
                    p = jnp.exp(_dot_nt(q_ref[rows, _ah(h)], k_ref[:, _ah(h)]))
                    if masked:
                        p = jnp.where(_causal(p.shape, rc * rch), p, 0.0)
                    ds = p * _dot_nt(do_ref[rows, _ah(h)], v_ref[:, _ah(h)])
                    dq_ref[rows, _hs(h)] += _dot(ds, k_ref[:, _ahd(h)])
                    dr_sc[h, rows, :] += jnp.sum(ds, axis=-1, keepdims=True)

        @pl.when(j < i)
        def _():
            block(False)

        @pl.when(j == i)
        def _():
            block(True)
            do2_ref[...] = do_ref[...]
            for h in range(NH):
                _put_aug(do2_ref, h, 0, parts=_split3_f32(_aug_value(do_ref, h) + dr_sc[h]))

    qspec = lambda w: pl.BlockSpec((t, w), lambda i, j: (i, 0))
    kspec = lambda w: pl.BlockSpec((t, w), lambda i, j: (jnp.minimum(j, i), 0))
    xin, xout, xshape, xsem = _comm_specs(comm)
    res = _call(
        _fuse_comm(core, 4, 2, comm, _square_steps(n)), name="flash_bwd_dq", grid=(n, n),
        in_specs=[qspec(NH * AW), kspec(NH * AW), kspec(NH * AW), qspec(NH * AW)] + xin,
        out_specs=[qspec(BW), qspec(NH * AW)] + xout,
        out_shape=[jax.ShapeDtypeStruct((S, BW), F32), jax.ShapeDtypeStruct((S, NH * AW), MXU_DTYPE)] + xshape,
        scratch=[pltpu.VMEM((NH, t, 1), F32)] + xsem,
        sem=("arbitrary", "arbitrary"))(qb, ka, va, doa, *(comm[1] if comm else []))
    return res[0], res[1], list(res[2:])


def _flash_bwd_dkv(qb, ka, va, doa):
    S = qb.shape[0]
    t = _tile(S, FLASH_TILE)
    n = S // t

    def body(q_ref, k_ref, v_ref, do_ref, dk_ref, dv_ref, dc_ref):
        j, i = pl.program_id(0), pl.program_id(1)

        @pl.when(i == 0)
        def _():
            dk_ref[...] = jnp.zeros_like(dk_ref)
            dv_ref[...] = jnp.zeros_like(dv_ref)
            dc_ref[...] = jnp.zeros_like(dc_ref)

        def block(masked):
            for h in range(NH):
                pt = jnp.exp(_dot_nt(k_ref[:, _ah(h)], q_ref[:, _ah(h)]))
                if masked:
                    pt = jnp.where(_causal(pt.shape, 0, transposed=True), pt, 0.0)
                dst = pt * _dot_nt(v_ref[:, _ah(h)], do_ref[:, _ah(h)])
                dv_ref[:, _hs(h)] += _dot(pt, do_ref[:, _ahd(h)])
                dk_ref[:, _hs(h)] += _dot(dst, q_ref[:, _ahd(h)])
                dc_ref[:, h:h + 1] += -jnp.sum(dst, axis=-1, keepdims=True)

        @pl.when(i > j)
        def _():
            block(False)

        @pl.when(i == j)
        def _():
            block(True)

    qspec = lambda w: pl.BlockSpec((t, w), lambda j, i: (jnp.maximum(i, j), 0))
    kspec = lambda w: pl.BlockSpec((t, w), lambda j, i: (j, 0))
    return _call(
        body, name="flash_bwd_dkv", grid=(n, n),
        in_specs=[qspec(NH * AW), kspec(NH * AW), kspec(NH * AW), qspec(NH * AW)],
        out_specs=[kspec(BW), kspec(BW), kspec(128)],
        out_shape=[jax.ShapeDtypeStruct((S, BW), F32), jax.ShapeDtypeStruct((S, BW), F32),
                   jax.ShapeDtypeStruct((S, 128), F32)],
        sem=("parallel", "arbitrary"))(qb, ka, va, doa)


def _attn_prep_bwd(z, fb, gq, gk, dq, dk, dv, dgate, dcc, dz):
    S = z.shape[0]
    ts = _tile(S, ROW_TILE)
    nt = S // ts

    def body(zb_ref, zf_ref, fb_ref, gq_ref, gk_ref, dq_ref, dk_ref, dv_ref, dg_ref, dcc_ref, dz_in, dzb_ref, dzf_ref,
             st_ref, carry):
        @pl.when(pl.program_id(0) == 0)
        def _():
            carry[...] = jnp.zeros_like(carry)
            st_ref[...] = jnp.zeros_like(st_ref)

        zb = zb_ref[...]
        gqv, gkv = gq_ref[...], gk_ref[...]
        dqv, dkv = dq_ref[...], dk_ref[...]
        sq = jnp.zeros((1, DH), F32)
        sk = jnp.zeros((1, DH), F32)
        for h in range(NH):
            dx, dgr = _rms_bwd(zb[:, _hs(h)], gqv, dqv[:, _hs(h)] * (DH ** -0.5))
            dzb_ref[:, _hs(h)] = _mx(dx)
            sq = sq + jnp.sum(dgr, axis=0, keepdims=True)
            ks = slice(BW + h * DH, BW + (h + 1) * DH)
            dx, dgr = _rms_bwd(zb[:, ks], gkv, dkv[:, _hs(h)])
            dzb_ref[:, ks] = _mx(dx)
            sk = sk + jnp.sum(dgr, axis=0, keepdims=True)
        dzb_ref[:, 2 * BW:3 * BW] = _mx(dv_ref[...])
        dzb_ref[:, 3 * BW:4 * BW] = dg_ref[...]
        dc = dcc_ref[...]
        dlf = _dot_exact_lhs01(_tri(ts, upper=True), dc) + carry[...]
        carry[...] = dlf[0:1, :]
        dfz = dlf * _sigmoid(-(zf_ref[...] + fb_ref[...]))
        dzf_ref[...] = _mx(dfz)
        st_ref[0:1, 0:DH] += sq
        st_ref[1:2, 0:DH] += sk
        st_ref[2:3, :] += jnp.sum(dfz, axis=0, keepdims=True)

    rev = lambda w, c=0: pl.BlockSpec((ts, w), lambda i: (nt - 1 - i, c))
    one = lambda w: pl.BlockSpec((1, w), lambda i: (0, 0))
    return _call(
        body, name="attn_prep_bwd", grid=(nt,),
        in_specs=[rev(1024, ZB // 1024), rev(128, ZF // 128), one(128), one(DH), one(DH),
                  rev(BW), rev(BW), rev(BW), rev(BW), rev(128), _ANY],
        out_specs=[rev(1024, ZB // 1024), rev(128), pl.BlockSpec((8, 128), lambda i: (0, 0))],
        out_shape=[jax.ShapeDtypeStruct((S, ZCOLS), MXU_DTYPE), jax.ShapeDtypeStruct((S, 128), MXU_DTYPE),
                   jax.ShapeDtypeStruct((8, 128), F32)],
        scratch=[pltpu.VMEM((1, 128), F32)], sem=("arbitrary",), aliases={10: 0})(z, z, fb, gq, gk, dq, dk, dv, dgate, dcc, dz)


def _put_fgate_cols(dzf, dz):
    S = dzf.shape[0]
    ts = _tile(S, ROW_TILE)

    def body(f_ref, dz_in, o_ref):
        o_ref[...] = f_ref[...]

    return _call(
        body, name="put_fgate_cols", grid=(S // ts,),
        in_specs=[pl.BlockSpec((ts, 128), lambda i: (i, 0)), _ANY],
        out_specs=pl.BlockSpec((ts, 128), lambda i: (i, ZF // 128)),
        out_shape=jax.ShapeDtypeStruct((S, ZCOLS), MXU_DTYPE), sem=("parallel",), aliases={1: 0})(dzf, dz)


def _hgrn_consts():
    C = CHUNK
    t = np.arange(C)[:, None]
    j = np.arange(C)[None, :]
    masks = []
    for lev in range(NLEV):
        m = C >> (lev + 1)
        blk, pos = t // (2 * m), t % (2 * m)
        sblk, spos = j // (2 * m), j % (2 * m)
        masks.append((blk == sblk) & (pos >= m) & (spos < m))
    masks.append(t == j)
    return (j <= t).astype(np.float32), np.stack(masks).astype(np.float32)


def _level_exponents(b, lg):
    row = lax.broadcasted_iota(jnp.int32, (CHUNK, 1), 0)
    eqs, eks = [], []
    for lev in range(NLEV):
        m = CHUNK >> (lev + 1)
        pos = jnp.bitwise_and(row, 2 * m - 1)
        if 2 * m >= 8:
            b3 = b.reshape(CHUNK // (2 * m), 2 * m, b.shape[1])
            mid = jnp.broadcast_to(b3[:, m - 1:m, :], b3.shape).reshape(b.shape)
            eqs.append(jnp.where(pos >= m, b - mid, 0.0))
            eks.append(jnp.where(pos < m, mid - b, 0.0))
        elif m == 2:
            eqs.append(jnp.where(pos == 2, lg, jnp.where(pos == 3, lg + pltpu.roll(lg, 1, 0), 0.0)))
            eks.append(jnp.where(pos == 0, pltpu.roll(lg, CHUNK - 1, 0), 0.0))
        else:
            eqs.append(jnp.where(pos == 1, lg, 0.0))
            eks.append(jnp.zeros_like(lg))
    return jnp.concatenate([b] + eqs + eks, axis=0)


def _hgrn_chunk_fwd(zc, lb, tmat):
    cq, cf, ci = zc[:, 0:BW], zc[:, BW:2 * BW], zc[:, 2 * BW:3 * BW]
    q = _silu(cq)
    sg = _sigmoid(cf)
    g = lb + (1.0 - lb) * sg
    lg = jnp.log(g)
    kf = (1.0 - lb) * _sigmoid(-cf)
    b = _dot_exact_lhs01(tmat, lg)
    e = _level_exponents(b, lg)
    blast = b[CHUNK - 1:CHUNK, :]
    return cq, cf, q, sg, g, kf, ci, e, b, blast


def _round_mx(x):
    if MXU_DTYPE != jnp.bfloat16:
        return x
    u = lax.bitcast_convert_type(x, jnp.uint32)
    u = (u + jnp.uint32(0x7FFF) + ((u >> 16) & jnp.uint32(1))) & jnp.uint32(0xFFFF0000)
    return lax.bitcast_convert_type(u, F32)


def _hgrn_scores(q, kf, e, masks, h, scores=True):
    qh, kh = q[:, _hs(h)], kf[:, _hs(h)]
    ql, kl = [], []
    a = None
    for lev in range(NLEV + 1):
        if lev < NLEV:
            eq = jnp.exp(e[(1 + lev) * CHUNK:(2 + lev) * CHUNK, _hs(h)])
            ek = jnp.exp(e[(1 + NLEV + lev) * CHUNK:(2 + NLEV + lev) * CHUNK, _hs(h)])
            ql.append((_round_mx(qh * eq), eq))
            kl.append((_round_mx(kh * ek), ek))
        else:
            ql.append((_round_mx(qh), None))
            kl.append((_round_mx(kh), None))
        if scores:
            term = masks[lev] * _dot_nt(ql[-1][0], kl[-1][0])
            a = term if a is None else a + term
    return a, ql, kl


def _hgrn_fwd(z, lb, gain, tmat, masks):
    S = z.shape[0]
    ts = _tile(S, ROW_TILE)
    nc = ts // CHUNK

    def body(zc_ref, lb_ref, gn_ref, tm_ref, mk_ref, o_ref, st_ref, a_ref, y_ref, state):
        @pl.when(pl.program_id(0) == 0)
        def _():
            state[...] = jnp.zeros_like(state)

        lbv, gn, tm, mk = lb_ref[...], gn_ref[...], tm_ref[...], mk_ref[...]

        def chunk(c, carry):
            r0 = pl.multiple_of(c * CHUNK, CHUNK)
            zc = zc_ref[pl.ds(r0, CHUNK), :]
            cq, cf, q, sg, g, kf, v, e, b, blast = _hgrn_chunk_fwd(zc, lbv, tm)
            qe = q * jnp.exp(b)
            kd = kf * jnp.exp(blast - b)
            st_ref[pl.ds(r0, CHUNK), :] = state[...]
            for h in range(NH):
                sth = state[:, _hs(h)]
                a = _mx(_hgrn_scores(q, kf, e, mk, h)[0])
                a_ref[pl.ds(r0, CHUNK), _hs(h)] = a
                oh = _dot_nt(qe[:, _hs(h)], sth) + _dot(a, v[:, _hs(h)])
                state[:, _hs(h)] = sth * jnp.exp(blast[:, _hs(h)]) + _dot_tn(v[:, _hs(h)], kd[:, _hs(h)])
                o_ref[pl.ds(r0, CHUNK), _hs(h)] = oh
                yn = _rms_fwd(oh, gn[:, _hs(h)])
                y_ref[pl.ds(r0, CHUNK), _hs(h)] = _mx(yn * _silu(zc[:, 3 * BW + h * DH:3 * BW + (h + 1) * DH]))
            return carry

        lax.fori_loop(0, nc, chunk, 0, unroll=min(4, nc))

    row = pl.BlockSpec((ts, BW), lambda i: (i, 0))
    one = pl.BlockSpec((1, BW), lambda i: (0, 0))
    return _call(
        body, name="hgrn_fwd", grid=(S // ts,),
        in_specs=[pl.BlockSpec((ts, 1024), lambda i: (i, ZC // 1024)), one, one,
                  pl.BlockSpec(tmat.shape, lambda i: (0, 0)), pl.BlockSpec(masks.shape, lambda i: (0, 0, 0))],
        out_specs=[row, row, row, row],
        out_shape=[jax.ShapeDtypeStruct((S, BW), F32), jax.ShapeDtypeStruct((S, BW), F32),
                   jax.ShapeDtypeStruct((S, BW), MXU_DTYPE), jax.ShapeDtypeStruct((S, BW), MXU_DTYPE)],
        scratch=[pltpu.VMEM((CHUNK, BW), F32)], sem=("arbitrary",))(z, lb, gain, tmat, masks)


def _hgrn_bwd(z, lb, gain, tmat, masks, o_pre, states, scores, dy, dz):
    S = z.shape[0]
    ts = _tile(S, ROW_TILE)
    nt = S // ts
    nc = ts // CHUNK

    def body(zc_ref, lb_ref, gn_ref, tm_ref, mk_ref, o_ref, st_ref, a_ref, dy_ref, dz_in, dz_ref, stat_ref, dstate):
        @pl.when(pl.program_id(0) == 0)
        def _():
            dstate[...] = jnp.zeros_like(dstate)
            stat_ref[...] = jnp.zeros_like(stat_ref)

        lbv, gn, tm, mk = lb_ref[...], gn_ref[...], tm_ref[...], mk_ref[...]
        upper = _tri(CHUNK, upper=True)
        lower_strict = 1.0 - upper

        def chunk(cc, carry):
            c = nc - 1 - cc
            r0 = pl.multiple_of(c * CHUNK, CHUNK)
            zc = zc_ref[pl.ds(r0, CHUNK), :]
            cq, cf, q, sg, g, kf, v, e, b, blast = _hgrn_chunk_fwd(zc, lbv, tm)
            eb = jnp.exp(b)
            ebl = jnp.exp(blast - b)
            qe = q * eb
            kd = kf * ebl
            o = o_ref[pl.ds(r0, CHUNK), :]
            dyv = dy_ref[pl.ds(r0, CHUNK), :]
            stp = st_ref[pl.ds(r0, CHUNK), :]
            cg = zc[:, 3 * BW:4 * BW]
            sgate = _silu(cg)
            dq_parts, dk_parts, dv_parts, dcg_parts = [], [], [], []
            dgain, up_parts, lo_parts, const_parts = [], [], [], []
            for h in range(NH):
                hs = _hs(h)
                oh = o[:, hs]
                r = lax.rsqrt(jnp.mean(oh * oh, axis=-1, keepdims=True) + EPS)
                nrm = oh * r
                dyn = dyv[:, hs] * sgate[:, hs]
                dcg_parts.append(dyv[:, hs] * nrm * gn[:, hs] * _dsilu(cg[:, hs]))
                dgain.append(jnp.sum(dyn * nrm, axis=0, keepdims=True))
                tt = dyn * gn[:, hs]
                doh = r * (tt - nrm * jnp.mean(tt * nrm, axis=-1, keepdims=True))
                _, ql, kl = _hgrn_scores(q, kf, e, mk, h, scores=False)
                a = a_ref[pl.ds(r0, CHUNK), hs]
                da = _dot_nt(doh, v[:, hs])
                dsth = dstate[:, hs]
                ebh = jnp.exp(blast[:, hs])
                dv_parts.append(_dot_tn(a, doh) + _dot_nt(kd[:, hs], dsth))
                dq_inter = eb[:, hs] * _dot(doh, stp[:, hs])
                dk_state = ebl[:, hs] * _dot(v[:, hs], dsth)
                dqh, dkh, gh = dq_inter, dk_state, None
                for lev in range(NLEV + 1):
                    dal = mk[lev] * da
                    xq = _dot(dal, kl[lev][0])
                    yk = _dot_tn(dal, ql[lev][0])
                    gterm = ql[lev][0] * xq - kl[lev][0] * yk
                    gh = gterm if gh is None else gh + gterm
                    dqh = dqh + (xq if lev == NLEV else ql[lev][1] * xq)
                    dkh = dkh + (yk if lev == NLEV else kl[lev][1] * yk)
                up_parts.append(gh + q[:, hs] * dq_inter)
                lo_parts.append(kf[:, hs] * dk_state)
                const_parts.append(jnp.sum(dsth * stp[:, hs], axis=0, keepdims=True) * ebh)
                dstate[:, hs] = dsth * ebh + _dot_tn(doh, qe[:, hs])
                dq_parts.append(dqh)
                dk_parts.append(dkh)
            dq = jnp.concatenate(dq_parts, axis=1)
            dk = jnp.concatenate(dk_parts, axis=1)
            dlg = (_dot_exact_lhs01(upper, jnp.concatenate(up_parts, axis=1))
                   + _dot_exact_lhs01(lower_strict, jnp.concatenate(lo_parts, axis=1))
                   + jnp.concatenate(const_parts, axis=1))
            dsg = sg * (1.0 - sg)
            dz_ref[pl.ds(r0, CHUNK), 0:BW] = _mx(dq * _dsilu(cq))
            dz_ref[pl.ds(r0, CHUNK), BW:2 * BW] = _mx((dlg / g - dk) * (1.0 - lbv) * dsg)
            dz_ref[pl.ds(r0, CHUNK), 2 * BW:3 * BW] = _mx(jnp.concatenate(dv_parts, axis=1))
            dz_ref[pl.ds(r0, CHUNK), 3 * BW:4 * BW] = _mx(jnp.concatenate(dcg_parts, axis=1))
            stat_ref[0:1, :] += jnp.concatenate(dgain, axis=1)
            stat_ref[1:2, :] += jnp.sum((dlg / g - dk) * (1.0 - sg), axis=0, keepdims=True)
            return carry

        lax.fori_loop(0, nc, chunk, 0, unroll=min(4, nc))

    rev = lambda w, c=0: pl.BlockSpec((ts, w), lambda i: (nt - 1 - i, c))
    one = pl.BlockSpec((1, BW), lambda i: (0, 0))
    return _call(
        body, name="hgrn_bwd", grid=(nt,),
        in_specs=[rev(1024, ZC // 1024), one, one, pl.BlockSpec(tmat.shape, lambda i: (0, 0)),
                  pl.BlockSpec(masks.shape, lambda i: (0, 0, 0)), rev(BW), rev(BW), rev(BW), rev(BW, 2), _ANY],
        out_specs=[rev(1024, ZC // 1024), pl.BlockSpec((8, BW), lambda i: (0, 0))],
        out_shape=[jax.ShapeDtypeStruct((S, ZCOLS), MXU_DTYPE), jax.ShapeDtypeStruct((8, BW), F32)],
        scratch=[pltpu.VMEM((CHUNK, BW), F32)],
        sem=("arbitrary",), aliases={9: 0})(z, lb, gain, tmat, masks, o_pre, states, scores, dy, dz)


def _lower_bounds_fwd(lb_logits):
    def body(l_ref, o_ref):
        l = l_ref[...]
        m = jnp.max(l, axis=0, keepdims=True)
        ex = jnp.exp(l - m)
        p = ex / jnp.sum(ex, axis=0, keepdims=True)
        cs = p[0:1, :]
        o_ref[0:1, :] = jnp.clip(cs - p[0:1, :], 0.0, 1.0)
        for d in range(1, DEPTH):
            cs = cs + p[d:d + 1, :]
            o_ref[d:d + 1, :] = jnp.clip(cs - p[0:1, :], 0.0, 1.0)

    full = pl.BlockSpec((DEPTH, BW), lambda: (0, 0))
    return _call(body, name="lower_bounds_fwd", grid=(), in_specs=[full], out_specs=full,
                 out_shape=jax.ShapeDtypeStruct((DEPTH, BW), F32))(lb_logits)


def _lower_bounds_bwd(lb_logits, dlow):
    def body(l_ref, d_ref, o_ref):
        l = l_ref[...]
        m = jnp.max(l, axis=0, keepdims=True)
        ex = jnp.exp(l - m)
        p = ex / jnp.sum(ex, axis=0, keepdims=True)
        dl = d_ref[...]
        cs = p[0:1, :]
        dcs = []
        for d in range(DEPTH):
            if d > 0:
                cs = cs + p[d:d + 1, :]
            val = cs - p[0:1, :]
            dcs.append(jnp.where((val > 0.0) & (val < 1.0), dl[d:d + 1, :], 0.0))
        total = dcs[0] + dcs[1] + dcs[2] + dcs[3]
        dp = []
        for j in range(DEPTH):
            s = dcs[j]
            for d in range(j + 1, DEPTH):
                s = s + dcs[d]
            dp.append(s - total if j == 0 else s)
        inner = p[0:1, :] * dp[0]
        for j in range(1, DEPTH):
            inner = inner + p[j:j + 1, :] * dp[j]
        for j in range(DEPTH):
            o_ref[j:j + 1, :] = p[j:j + 1, :] * (dp[j] - inner)

    full = pl.BlockSpec((DEPTH, BW), lambda: (0, 0))
    return _call(body, name="lower_bounds_bwd", grid=(), in_specs=[full, full], out_specs=full,
                 out_shape=jax.ShapeDtypeStruct((DEPTH, BW), F32))(lb_logits, dlow)


def _sgu_fwd(z, gv, ws, bs):
    S = z.shape[0]
    ts = _tile(S, ROW_TILE)
    nc = ts // SCHUNK

    def body(zd_ref, gv_ref, ws_ref, bs_ref, y_ref):
        gvv, bsv = gv_ref[...], bs_ref[...]
        tril = _tri(SCHUNK)
        for c in range(nc):
            rs = slice(c * SCHUNK, (c + 1) * SCHUNK)
            zd = zd_ref[rs, :]
            for h in range(NH):
                vn = _rms_fwd(zd[:, BW + h * DH:BW + (h + 1) * DH], gvv[:, _hs(h)])
                s = _dot(ws_ref[h] * tril, vn) + bsv[:, h:h + 1]
                y_ref[rs, _hs(h)] = _mx(zd[:, _hs(h)] * s * _silu(zd[:, 2 * BW + h * DH:2 * BW + (h + 1) * DH]))

    return _call(
        body, name="sgu_fwd", grid=(S // ts,),
        in_specs=[pl.BlockSpec((ts, 1024), lambda i: (i, ZD // 1024)), pl.BlockSpec((1, BW), lambda i: (0, 0)),
                  pl.BlockSpec((NH, SCHUNK, SCHUNK), lambda i: (0, 0, 0)), pl.BlockSpec((SCHUNK, 128), lambda i: (0, 0))],
        out_specs=pl.BlockSpec((ts, BW), lambda i: (i, 0)),
        out_shape=jax.ShapeDtypeStruct((S, BW), MXU_DTYPE), sem=("parallel",))(z, gv, ws, bs)


def _sgu_bwd(z, gv, ws, bs, dy, dz):
    S = z.shape[0]
    ts = _tile(S, ROW_TILE)
    nc = ts // SCHUNK

    def body(zd_ref, gv_ref, ws_ref, bs_ref, dy_ref, dz_in, dz_ref, dws_ref, dbs_ref, st_ref):
        @pl.when(pl.program_id(0) == 0)
        def _():
            dws_ref[...] = jnp.zeros_like(dws_ref)
            dbs_ref[...] = jnp.zeros_like(dbs_ref)
            st_ref[...] = jnp.zeros_like(st_ref)

        gvv, bsv = gv_ref[...], bs_ref[...]
        tril = _tri(SCHUNK)
        dz_ref[:, 3 * BW:4 * BW] = jnp.zeros((ts, BW), MXU_DTYPE)
        for c in range(nc):
            rs = slice(c * SCHUNK, (c + 1) * SCHUNK)
            zd = zd_ref[rs, :]
            dyv = dy_ref[rs, :]
            for h in range(NH):
                hs = _hs(h)
                u = zd[:, hs]
                vraw = zd[:, BW + h * DH:BW + (h + 1) * DH]
                gt = zd[:, 2 * BW + h * DH:2 * BW + (h + 1) * DH]
                gvh = gvv[:, hs]
                vn = _rms_fwd(vraw, gvh)
                wm = ws_ref[h] * tril
                s = _dot(wm, vn) + bsv[:, h:h + 1]
                sil = _silu(gt)
                d = dyv[:, hs]
                ds = d * u * sil
                dz_ref[rs, hs] = _mx(d * s * sil)
                dz_ref[rs, 2 * BW + h * DH:2 * BW + (h + 1) * DH] = _mx(d * u * s * _dsilu(gt))
                dws_ref[h] += tril * _dot_nt(ds, vn)
                dbs_ref[:, h:h + 1] += jnp.sum(ds, axis=-1, keepdims=True)
                dvn = _dot_tn(wm, ds)
                dx, dgr = _rms_bwd(vraw, gvh, dvn)
                dz_ref[rs, BW + h * DH:BW + (h + 1) * DH] = _mx(dx)
                st_ref[0:1, hs] += jnp.sum(dgr, axis=0, keepdims=True)

    return _call(
        body, name="sgu_bwd", grid=(S // ts,),
        in_specs=[pl.BlockSpec((ts, 1024), lambda i: (i, ZD // 1024)), pl.BlockSpec((1, BW), lambda i: (0, 0)),
                  pl.BlockSpec((NH, SCHUNK, SCHUNK), lambda i: (0, 0, 0)), pl.BlockSpec((SCHUNK, 128), lambda i: (0, 0)),
                  pl.BlockSpec((ts, BW), lambda i: (i, 3)), _ANY],
        out_specs=[pl.BlockSpec((ts, 1024), lambda i: (i, ZD // 1024)), pl.BlockSpec((NH, SCHUNK, SCHUNK), lambda i: (0, 0, 0)),
                   pl.BlockSpec((SCHUNK, 128), lambda i: (0, 0)), pl.BlockSpec((8, BW), lambda i: (0, 0))],
        out_shape=[jax.ShapeDtypeStruct((S, ZCOLS), MXU_DTYPE), jax.ShapeDtypeStruct((NH, SCHUNK, SCHUNK), F32),
                   jax.ShapeDtypeStruct((SCHUNK, 128), F32), jax.ShapeDtypeStruct((8, BW), F32)],
        sem=("arbitrary",), aliases={5: 0})(z, gv, ws, bs, dy, dz)


def _merge_fwd(x, z, ys, wup, mb, wo):
    S = x.shape[0]
    ts = _tile(S, MERGE_TILE)

    def body(x_ref, zm_ref, ya_ref, yb_ref, yc_ref, yd_ref, wup_ref, mb_ref, wo_ref, x1_ref, mg_ref):
        yrefs = (ya_ref, yb_ref, yc_ref, yd_ref)
        mbv = mb_ref[...]
        merged = None
        for b in range(NBR):
            cs = slice(b * D_MODEL, (b + 1) * D_MODEL)
            term = _sigmoid(zm_ref[:, cs] + mbv[b:b + 1, :]) * jnp.dot(yrefs[b][...], wup_ref[b],
                                                                      preferred_element_type=F32)
            merged = term if merged is None else merged + term
        mg = _mx(merged)
        mg_ref[...] = mg
        x1_ref[...] = x_ref[...] + jnp.dot(mg, wo_ref[...], preferred_element_type=F32)

    row = lambda w: pl.BlockSpec((ts, w), lambda i: (i, 0))
    return _call(
        body, name="merge_fwd", grid=(S // ts,),
        in_specs=[row(D_MODEL), pl.BlockSpec((ts, 4096), lambda i: (i, 0)), row(BW), row(BW), row(BW), row(BW),
                  pl.BlockSpec((NBR, BW, D_MODEL), lambda i: (0, 0, 0)), pl.BlockSpec((NBR, D_MODEL), lambda i: (0, 0)),
                  pl.BlockSpec((D_MODEL, D_MODEL), lambda i: (0, 0))],
        out_specs=[row(D_MODEL), row(D_MODEL)],
        out_shape=[jax.ShapeDtypeStruct((S, D_MODEL), F32), jax.ShapeDtypeStruct((S, D_MODEL), MXU_DTYPE)],
        sem=("parallel",))(x, z, *ys, wup, mb, wo)


def _merge_bwd(dx1, z, ys, wup, mb, wo):
    S = dx1.shape[0]
    ts = _tile(S, MERGE_TILE)

    def body(dx_ref, zm_ref, ya_ref, yb_ref, yc_ref, yd_ref, wup_ref, mb_ref, wo_ref,
             dzm_ref, du_ref, dxb_ref, dy_ref, st_ref):
        @pl.when(pl.program_id(0) == 0)
        def _():
            st_ref[...] = jnp.zeros_like(st_ref)

        yrefs = (ya_ref, yb_ref, yc_ref, yd_ref)
        mbv = mb_ref[...]
        dxb = _mx(dx_ref[...])
        dxb_ref[...] = dxb
        dmerged = _dot_nt(dxb, wo_ref[...])
        for b in range(NBR):
            cs = slice(b * D_MODEL, (b + 1) * D_MODEL)
            u = jnp.dot(yrefs[b][...], wup_ref[b], preferred_element_type=F32)
            sg = _sigmoid(zm_ref[:, cs] + mbv[b:b + 1, :])
            du = _mx(dmerged * sg)
            du_ref[:, cs] = du
            dzm = dmerged * u * sg * (1.0 - sg)
            dzm_ref[:, cs] = _mx(dzm)
            st_ref[b:b + 1, :] += jnp.sum(dzm, axis=0, keepdims=True)
            dy_ref[:, b * BW:(b + 1) * BW] = _dot_nt(du, wup_ref[b])

    row = lambda w: pl.BlockSpec((ts, w), lambda i: (i, 0))
    return _call(
        body, name="merge_bwd", grid=(S // ts,),
        in_specs=[row(D_MODEL), pl.BlockSpec((ts, 4096), lambda i: (i, 0)), row(BW), row(BW), row(BW), row(BW),
                  pl.BlockSpec((NBR, BW, D_MODEL), lambda i: (0, 0, 0)),
                  pl.BlockSpec((NBR, D_MODEL), lambda i: (0, 0)), pl.BlockSpec((D_MODEL, D_MODEL), lambda i: (0, 0))],
        out_specs=[row(4096), row(4096), row(D_MODEL), row(D_MODEL), pl.BlockSpec((8, D_MODEL), lambda i: (0, 0))],
        out_shape=[jax.ShapeDtypeStruct((S, ZCOLS), MXU_DTYPE), jax.ShapeDtypeStruct((S, 4096), MXU_DTYPE),
                   jax.ShapeDtypeStruct((S, D_MODEL), MXU_DTYPE), jax.ShapeDtypeStruct((S, D_MODEL), F32),
                   jax.ShapeDtypeStruct((8, D_MODEL), F32)],
        sem=("arbitrary",))(dx1, z, *ys, wup, mb, wo)


def _ple_fwd(x1, p, g, wg, wp):
    S = x1.shape[0]
    ts = _tile(S, ROW_TILE)

    def body(x_ref, p_ref, g_ref, wg_ref, wp_ref, o_ref):
        x = x_ref[...]
        hp = _mx(_rms_fwd(x, g_ref[...]))
        gate = _sigmoid(jnp.dot(hp, wg_ref[...], preferred_element_type=F32))
        pp = jnp.dot(_mx(p_ref[...]), wp_ref[...], preferred_element_type=F32)
        o_ref[...] = x + gate * pp

    row = lambda w: pl.BlockSpec((ts, w), lambda i: (i, 0))
    return _call(
        body, name="ple_fwd", grid=(S // ts,),
        in_specs=[row(D_MODEL), row(PLE), pl.BlockSpec((1, D_MODEL), lambda i: (0, 0)),
                  pl.BlockSpec((D_MODEL, D_MODEL), lambda i: (0, 0)), pl.BlockSpec((PLE, D_MODEL), lambda i: (0, 0))],
        out_specs=row(D_MODEL), out_shape=jax.ShapeDtypeStruct((S, D_MODEL), F32),
        sem=("parallel",))(x1, p, g, wg, wp)


def _ple_bwd(x1, p, dx2, g, wg, wp):
    S = x1.shape[0]
    ts = _tile(S, ROW_TILE)

    def body(x_ref, p_ref, dx_ref, g_ref, wg_ref, wp_ref, dx1_ref, hp_ref, dgl_ref, dpp_ref, pb_ref, st_ref):
        @pl.when(pl.program_id(0) == 0)
        def _():
            st_ref[...] = jnp.zeros_like(st_ref)

        x, gv, dx2 = x_ref[...], g_ref[...], dx_ref[...]
        hp = _mx(_rms_fwd(x, gv))
        hp_ref[...] = hp
        gate = _sigmoid(jnp.dot(hp, wg_ref[...], preferred_element_type=F32))
        pb = _mx(p_ref[...])
        pb_ref[...] = pb
        pp = jnp.dot(pb, wp_ref[...], preferred_element_type=F32)
        dgl = _mx(dx2 * pp * gate * (1.0 - gate))
        dgl_ref[...] = dgl
        dpp_ref[...] = _mx(dx2 * gate)
        dhp = _dot_nt(dgl, wg_ref[...])
        dxn, dgr = _rms_bwd(x, gv, dhp)
        dx1_ref[...] = dx2 + dxn
        st_ref[0:1, :] += jnp.sum(dgr, axis=0, keepdims=True)

    row = lambda w: pl.BlockSpec((ts, w), lambda i: (i, 0))
    sq = pl.BlockSpec((D_MODEL, D_MODEL), lambda i: (0, 0))
    return _call(
        body, name="ple_bwd", grid=(S // ts,),
        in_specs=[row(D_MODEL), row(PLE), row(D_MODEL), pl.BlockSpec((1, D_MODEL), lambda i: (0, 0)), sq,
                  pl.BlockSpec((PLE, D_MODEL), lambda i: (0, 0))],
        out_specs=[row(D_MODEL), row(D_MODEL), row(D_MODEL), row(D_MODEL), row(PLE),
                   pl.BlockSpec((8, D_MODEL), lambda i: (0, 0))],
        out_shape=[jax.ShapeDtypeStruct((S, D_MODEL), F32)] + [jax.ShapeDtypeStruct((S, D_MODEL), MXU_DTYPE)] * 3
        + [jax.ShapeDtypeStruct((S, PLE), MXU_DTYPE), jax.ShapeDtypeStruct((8, D_MODEL), F32)],
        sem=("arbitrary",))(x1, p, dx2, g, wg, wp)


def _pad_rows(a, rows=8):
    return jnp.concatenate([a, jnp.zeros((rows - a.shape[0],) + a.shape[1:], a.dtype)], axis=0)


def _pad_lanes(a, lanes=128):
    return jnp.concatenate([a, jnp.zeros(a.shape[:-1] + (lanes - a.shape[-1],), a.dtype)], axis=-1)


def _wz_from_w_in(w):
    zeros = lambda n: jnp.zeros((w.shape[0], n), w.dtype)
    return jnp.concatenate([w[:, _OM:_OEND], w[:, _OA:_OB], w[:, _OB:_OF], w[:, _OC:_OD], w[:, _OD:_OM], zeros(256),
                            w[:, _OF:_OC], zeros(124)], axis=1)


def _w_in_from_wz(g):
    return jnp.concatenate([g[:, ZA:ZB], g[:, ZB:ZC], g[:, ZF:ZF + 4], g[:, ZC:ZD], g[:, ZD:ZD + 768], g[:, ZM:ZA]], axis=1)


_W_IN_GROUPS = [(_OA, _OB, ZA), (_OB, _OF, ZB), (_OF, _OC, ZF), (_OC, _OD, ZC), (_OD, _OM, ZD), (_OM, _OEND, ZM)]


def _wz_from_shards(g):
    n = g.shape[-1]
    pieces, pos = [], 0
    for a, b, zs in sorted(_W_IN_GROUPS, key=lambda grp: grp[2]):
        if zs > pos:
            pieces.append(jnp.zeros((g.shape[1], zs - pos), g.dtype))
        for k in range(4):
            lo, hi = max(a, k * n), min(b, (k + 1) * n)
            if lo < hi:
                pieces.append(g[k][:, lo - k * n:hi - k * n])
        pos = zs + (b - a)
    pieces.append(jnp.zeros((g.shape[1], ZCOLS - pos), g.dtype))
    return jnp.concatenate(pieces, axis=1)


def _w_in_slabs_from_wz(g):
    n = _OEND // 4
    slabs = []
    for k in range(4):
        pieces = []
        for a, b, zs in _W_IN_GROUPS:
            lo, hi = max(a, k * n), min(b, (k + 1) * n)
            if lo < hi:
                pieces.append(g[:, zs + lo - a:zs + hi - a])
        slabs.append(jnp.concatenate(pieces, axis=1))
    return jnp.stack(slabs)


def _local_step(x, p, target, wts, dist=None):
    tmat_np, masks_np = _hgrn_consts()
    tmat = jnp.asarray(tmat_np, MXU_DTYPE)
    masks = jnp.asarray(masks_np, F32)
    lower = _lower_bounds_fwd(wts["lb_logits"])
    saved = []
    for li in range(DEPTH):
        big = dist.weights(li) if dist else {n: wts[n][li] for n in _GATHERED}
        wz = _wz_from_shards(big["w_in"]) if big["w_in"].ndim == 3 else _wz_from_w_in(big["w_in"])
        g_mix = wts["norm_mix"][li][None, :]
        h = _norm_fwd(x, g_mix)
        z = _mm_nn(h, wz, F32, name="mm_z", ts=2 * ROW_TILE, tn=1664)
        cw = _pad_rows(jnp.concatenate([big["conv_w"], wts["conv_b"][li][None, :]], axis=0))
        ya = _conv_fwd(z, cw)
        fb = _pad_lanes(wts["fgate_bias"][li][None, :])
        gq, gk = wts["q_norm"][li][None, :], wts["k_norm"][li][None, :]
        qa, ka, va, cc = _attn_prep_fwd(z, fb, gq, gk)
        o, lse, yb, got = _flash_fwd(qa, ka, va, z, comm=dist.fwd_comm(li) if dist else None)
        if dist:
            dist.fwd_done(li, got)
        lb = lower[li][None, :]
        gh = wts["hgrn_norm"][li][None, :]
        o_pre, states, hscores, yc = _hgrn_fwd(z, lb, gh, tmat, masks)
        gv = wts["sgu_norm"][li][None, :]
        ws = wts["spatial_w"][li]
        bs = _pad_lanes(wts["spatial_b"][li].T)
        yd = _sgu_fwd(z, gv, ws, bs)
        ys = (ya, yb, yc, yd)
        x1, merged = _merge_fwd(x, z, ys, big["w_up"], big["merge_b"], big["w_o"])
        g_ple = wts["norm_ple"][li][None, :]
        x2 = _ple_fwd(x1, p[li], g_ple, big["w_ple_gate"], big["w_ple_proj"])
        saved.append(dict(x=x, h=h, z=z, wz=wz, cw=cw, fb=fb, gq=gq, gk=gk, qa=qa, ka=ka, va=va, cc=cc, o=o, lse=lse,
                          lb=lb, gh=gh, o_pre=o_pre, states=states, hscores=hscores, gv=gv, ws=ws, bs=bs, ys=ys, x1=x1, merged=merged,
                          g_mix=g_mix, g_ple=g_ple, big=big))
        x = x2

    loss, dx = _loss_fwd_bwd(x, target)

    names = ["norm_mix", "w_in", "conv_w", "conv_b", "fgate_bias", "q_norm", "k_norm", "lb", "hgrn_norm", "sgu_norm",
             "spatial_w", "spatial_b", "w_up", "merge_b", "w_o", "norm_ple", "w_ple_gate", "w_ple_proj"]
    gl = {n: [None] * DEPTH for n in names}
    for li in reversed(range(DEPTH)):
        s = saved[li]
        z, big = s["z"], s["big"]
        wg, wp = big["w_ple_gate"], big["w_ple_proj"]
        dx1, hp, dgl, dpp, pb, st = _ple_bwd(s["x1"], p[li], dx, s["g_ple"], wg, wp)
        gl["norm_ple"][li] = st[0]
        gl["w_ple_gate"][li] = _mm_tn(hp, dgl, name="mm_dwg")
        gl["w_ple_proj"][li] = _mm_tn(pb, dpp, name="mm_dwp")
        wup, wo = big["w_up"], big["w_o"]
        dz, du, dxb, dy, st = _merge_bwd(dx1, z, s["ys"], wup, big["merge_b"], wo)
        gl["merge_b"][li] = st[0:NBR]
        gl["w_o"][li] = _mm_tn(s["merged"], dxb, name="mm_dwo")
        gl["w_up"][li] = jnp.stack([_mm_tn(s["ys"][b], du, name="mm_dwup", ycol=b, n=D_MODEL) for b in range(NBR)])
        dz, st = _conv_bwd(z, s["cw"], dy, dz)
        gl["conv_w"][li] = st[0:3]
        gl["conv_b"][li] = st[3]
        dgate, qb, doa = _attn_gate_bwd(dy, s["o"], z, s["qa"], s["cc"], s["lse"])
        dq, doa, got = _flash_bwd_dq(qb, s["ka"], s["va"], doa, comm=dist.bwd_comm() if dist else None)
        if dist:
            dist.bwd_done(got)
        dk, dv, dcc = _flash_bwd_dkv(qb, s["ka"], s["va"], doa)
        dz, dzf, st = _attn_prep_bwd(z, s["fb"], s["gq"], s["gk"], dq, dk, dv, dgate, dcc, dz)
        dz = _put_fgate_cols(dzf, dz)
        gl["q_norm"][li] = st[0, 0:DH]
        gl["k_norm"][li] = st[1, 0:DH]
        gl["fgate_bias"][li] = st[2, 0:NH]
        dz, st = _hgrn_bwd(z, s["lb"], s["gh"], tmat, masks, s["o_pre"], s["states"], s["hscores"], dy, dz)
        gl["hgrn_norm"][li] = st[0]
        gl["lb"][li] = st[1]
        dz, dws, dbs, st = _sgu_bwd(z, s["gv"], s["ws"], s["bs"], dy, dz)
        gl["sgu_norm"][li] = st[0]
        gl["spatial_w"][li] = dws
        gl["spatial_b"][li] = dbs[:, 0:NH].T
        dwz = _mm_tn(s["h"], dz, name="mm_dwz", tn=1664)
        gl["w_in"][li] = dwz if dist else _w_in_from_wz(dwz)
        if dist:
            dist.push(li, {n: gl[n][li] for n in _BIG})
        if dist and li == 0:
            dh, got = _mm_nn(dz, s["wz"], F32, name="mm_dh", ts=ROW_TILE // 2, b_transposed=True, comm=dist.bwd_comm())
            dist.bwd_done(got)
        else:
            dh = _mm_nn(dz, s["wz"], F32, name="mm_dh", ts=ROW_TILE // 2, b_transposed=True)
        dx, st = _norm_bwd(s["x"], s["g_mix"], dh, dx1)
        gl["norm_mix"][li] = st[0]

    if dist:
        dist.finish()
    grads = {n: jnp.stack(v) for n, v in gl.items() if not (dist and n in _BIG)}
    grads["lb_logits"] = _lower_bounds_bwd(wts["lb_logits"], grads.pop("lb"))
    return loss, dx, grads


def _my_pos():
    return lax.axis_index("x"), lax.axis_index("y"), lax.axis_index("c")


def _gather_phase(phase, ins, outs, sems):
    if phase == 1:
        return
    send, recv, lsem = sems
    x, y, c = _my_pos()
    me = 2 * x + y
    peers = [(1 - x, y), (x, 1 - y), (1 - x, 1 - y)]
    copies = []
    for t in range(len(ins)):
        copies.append(pltpu.make_async_copy(ins[t], outs[t].at[me], lsem.at[t]))
        for j, (px, py) in enumerate(peers):
            copies.append(pltpu.make_async_remote_copy(
                src_ref=ins[t], dst_ref=outs[t].at[me], send_sem=send.at[t, j], recv_sem=recv.at[t, j],
                device_id=(px, py, c), device_id_type=MESH))
    for cp in copies:
        if phase == 0:
            cp.start()
        else:
            cp.wait()


N_HALVED = 5


def _gather_halves_phase(phase, ins, outs, sems):
    send, recv, lsem = sems
    x, y, c = _my_pos()
    me = 2 * x + y
    sib = (x, y, 1 - c)
    chips = [(1 - x, y), (x, 1 - y), (1 - x, 1 - y)]

    def rc(t, k, src, dst, dev):
        return pltpu.make_async_remote_copy(src_ref=src, dst_ref=dst, send_sem=send.at[t, k], recv_sem=recv.at[t, k],
                                            device_id=dev, device_id_type=MESH)

    for t in range(len(ins)):
        local = pltpu.make_async_copy(ins[t], outs[t].at[me], lsem.at[t])
        if t >= N_HALVED:
            whole = [rc(t, j, ins[t], outs[t].at[me], (px, py, c)) for j, (px, py) in enumerate(chips)]
            for cp in [local] + whole:
                if phase == 0:
                    cp.start()
                elif phase == 2:
                    cp.wait()
            continue
        hr = ins[t].shape[0] // 2
        mine, other = pl.ds(c * hr, hr), pl.ds((1 - c) * hr, hr)
        first = [rc(t, j, ins[t].at[mine], outs[t].at[me, mine], (px, py, c)) for j, (px, py) in enumerate(chips)]
        landed = [outs[t].at[2 * px + py, mine] for px, py in chips]
        passed = [rc(t, 3 + j, slot, slot, sib) for j, slot in enumerate(landed)]
        if phase == 0:
            local.start()
            for cp in first:
                cp.start()
        elif phase == 1:
            for j, slot in enumerate(landed):
                rc(t, j, slot, slot, (x, y, c)).wait_recv()
                passed[j].start()
        else:
            for j, (px, py) in enumerate(chips):
                slot = outs[t].at[2 * px + py, other]
                rc(t, 3 + j, slot, slot, (x, y, c)).wait_recv()
            for cp in first + passed:
                cp.wait_send()
            local.wait()


def _exchange_phase(phase, ins, outs, sems):
    send, recv, lsem = sems
    x, y, c = _my_pos()
    me = 2 * x + y
    sib = (x, y, 1 - c)
    chips = [(1 - x, y), (x, 1 - y), (1 - x, 1 - y)]

    def rc(t, k, src, dst, dev):
        return pltpu.make_async_remote_copy(src_ref=src, dst_ref=dst, send_sem=send.at[t, k], recv_sem=recv.at[t, k],
                                            device_id=dev, device_id_type=MESH)

    for t in range(len(ins)):
        local = pltpu.make_async_copy(ins[t].at[me], outs[t].at[2 * me + c], lsem.at[t])
        first = [rc(t, 0, ins[t].at[me], outs[t].at[2 * me + c], sib)]
        first += [rc(t, 1 + j, ins[t].at[2 * px + py], outs[t].at[2 * me + c], (px, py, c))
                  for j, (px, py) in enumerate(chips)]
        landed = [outs[t].at[2 * (2 * px + py) + c] for px, py in chips]
        passed = [rc(t, 4 + j, slot, slot, sib) for j, slot in enumerate(landed)]
        if phase == 0:
            local.start()
            for cp in first:
                cp.start()
        elif phase == 1:
            for j, slot in enumerate(landed):
                rc(t, 1 + j, slot, slot, (x, y, c)).wait_recv()
                passed[j].start()
        else:
            s0 = outs[t].at[2 * me + (1 - c)]
            rc(t, 0, s0, s0, (x, y, c)).wait_recv()
            for j, (px, py) in enumerate(chips):
                slot = outs[t].at[2 * (2 * px + py) + (1 - c)]
                rc(t, 4 + j, slot, slot, (x, y, c)).wait_recv()
            for cp in first + passed:
                cp.wait_send()
            local.wait()


_COMM_PHASES = {"gather": _gather_phase, "gather_halves": _gather_halves_phase, "exchange": _exchange_phase}
_gathered_shapes = lambda arrays: [jax.ShapeDtypeStruct((4,) + a.shape, a.dtype) for a in arrays]
_COMM_OUT = {"gather": _gathered_shapes, "gather_halves": _gathered_shapes,
             "exchange": lambda arrays: [jax.ShapeDtypeStruct((8,) + a.shape[1:], a.dtype) for a in arrays]}


def _comm_sems(kind, nc):
    k = {"gather": 3, "gather_halves": 6, "exchange": 7}[kind]
    return [pltpu.SemaphoreType.DMA((nc, k)), pltpu.SemaphoreType.DMA((nc, k)), pltpu.SemaphoreType.DMA((nc,))]


def _comm_alone(kind, arrays, name):
    nc = len(arrays)

    def body(*refs):
        for phase in range(3):
            _COMM_PHASES[kind](phase, refs[:nc], refs[nc:2 * nc], refs[2 * nc:])

    hbm = pl.BlockSpec(memory_space=pl.ANY)
    return pl.pallas_call(
        functools.partial(body), name=name, in_specs=[hbm] * nc, out_specs=[hbm] * nc,
        out_shape=_COMM_OUT[kind](arrays), scratch_shapes=_comm_sems(kind, nc))(*arrays)


def _allreduce_small(v):
    R = v.shape[0]

    def body(v_ref, o_ref, pair, chips, send, recv):
        x, y, c = _my_pos()
        me = 2 * x + y
        pair[c] = v_ref[...]
        swap = pltpu.make_async_remote_copy(src_ref=v_ref, dst_ref=pair.at[c], send_sem=send.at[0], recv_sem=recv.at[0],
                                            device_id=(x, y, 1 - c), device_id_type=MESH)
        swap.start()
        swap.wait()
        chips[me] = pair[0] + pair[1]
        copies = [pltpu.make_async_remote_copy(src_ref=chips.at[me], dst_ref=chips.at[me], send_sem=send.at[1 + j],
                                               recv_sem=recv.at[1 + j], device_id=(px, py, c), device_id_type=MESH)
                  for j, (px, py) in enumerate([(1 - x, y), (x, 1 - y), (1 - x, 1 - y)])]
        for cp in copies:
            cp.start()
        for cp in copies:
            cp.wait()
        o_ref[...] = (chips[0] + chips[1]) + (chips[2] + chips[3])

    vm = pl.BlockSpec(memory_space=pltpu.VMEM)
    return pl.pallas_call(
        functools.partial(body), name="allreduce_small", in_specs=[vm], out_specs=vm,
        out_shape=jax.ShapeDtypeStruct((R, 128), F32),
        scratch_shapes=[pltpu.VMEM((2, R, 128), F32), pltpu.VMEM((4, R, 128), F32),
                        pltpu.SemaphoreType.DMA((4,)), pltpu.SemaphoreType.DMA((4,))],
        compiler_params=pltpu.CompilerParams(vmem_limit_bytes=VMEM_LIMIT))(v)


def _adamw(w, m, v, parts, nparts):
    A, R, C = w.shape
    per_layer = isinstance(parts, (list, tuple))
    tr = _tile(R, 64) if per_layer else (R if R <= 128 else _tile(R, 128))
    nparr = len(parts) if per_layer else 1

    def body(*refs):
        w_ref, m_ref, v_ref = refs[:3]
        p_refs = refs[3:3 + nparr]
        g_ref, d_ref, nm_ref, nv_ref = refs[3 + nparr:]

        def update(read):
            g = read(0).astype(F32)
            for k in range(1, nparts):
                g = g + read(k).astype(F32)
            mm = ADAM_B1 * m_ref[0] + (1.0 - ADAM_B1) * g
            vv = ADAM_B2 * v_ref[0] + (1.0 - ADAM_B2) * jnp.square(g)
            m_hat = mm / (1.0 - ADAM_B1 ** ADAM_STEP)
            v_hat = vv / (1.0 - ADAM_B2 ** ADAM_STEP)
            g_ref[0] = g
            d_ref[0] = -ADAM_LR * (m_hat / (jnp.sqrt(v_hat) + ADAM_EPS) + ADAM_WD * w_ref[0])
            nm_ref[0] = mm
            nv_ref[0] = vv

        if not per_layer:
            update(lambda k: p_refs[0][k, 0])
        else:
            for a in range(A):
                @pl.when(pl.program_id(0) == a)
                def _(a=a):
                    update(lambda k: p_refs[a][k])

    blk = pl.BlockSpec((1, tr, C), lambda a, r: (a, r, 0))
    if per_layer:
        pspecs = [pl.BlockSpec((nparts, tr, C), functools.partial(lambda a, r, l: (0, jnp.where(a == l, r, 0), 0), l=l))
                  for l in range(A)]
        pargs = list(parts)
    else:
        pspecs = [pl.BlockSpec((nparts, 1, tr, C), lambda a, r: (0, a, r, 0))]
        pargs = [parts]
    return _call(
        body, name="adamw", grid=(A, R // tr), in_specs=[blk, blk, blk] + pspecs,
        out_specs=[blk] * 4, out_shape=[jax.ShapeDtypeStruct((A, R, C), F32)] * 4,
        sem=("arbitrary", "arbitrary"))(w, m, v, *pargs)


def _rows128(a):
    width = a.shape[-1]
    a2 = a.reshape(-1, width)
    k = -(-width // 128)
    if width % 128:
        a2 = jnp.pad(a2, ((0, 0), (0, k * 128 - width)))
    return a2.reshape(-1, 128)


def _from_rows128(r, shape):
    width = shape[-1]
    k = -(-width // 128)
    return r.reshape(-1, k * 128)[:, :width].reshape(shape)


def _as3d(a):
    if a.ndim == 2:
        return a[None]
    if a.ndim == 3:
        return a
    return a.reshape((-1,) + a.shape[-2:])


_WEIGHTS = ["norm_mix", "w_in", "conv_w", "conv_b", "fgate_bias", "q_norm", "k_norm", "lb_logits", "hgrn_norm", "sgu_norm",
            "spatial_w", "spatial_b", "w_up", "merge_b", "w_o", "norm_ple", "w_ple_gate", "w_ple_proj"]
_BIG = ["w_in", "w_up", "w_o", "w_ple_gate", "w_ple_proj"]
_GATHERED = _BIG + ["conv_w", "merge_b"]
_SHARD_AXIS = {"w_in": 2, "w_up": 3, "w_o": 1, "w_ple_gate": 1, "w_ple_proj": 2, "conv_w": 2, "merge_b": 2}
_SMALL = [n for n in _WEIGHTS if n not in _BIG]


class _Dist:
    def __init__(self, w):
        self.w = w
        self.full = {0: self._unpack(_comm_alone("gather_halves", self._shards(0), "gather_weights"))}
        self.contrib = {}
        self.pending = None

    def _shards(self, l):
        return [_mx(self.w[n][l]) for n in _BIG] + [self.w["conv_w"][l], self.w["merge_b"][l]]

    def _unpack(self, gathered):
        return {n: g if n == "w_in" else jnp.concatenate([g[k] for k in range(4)], axis=_SHARD_AXIS[n] - 1)
                for n, g in zip(_GATHERED, gathered)}

    def weights(self, l):
        return self.full[l]

    def fwd_comm(self, l):
        return ("gather", self._shards(l + 1)) if l + 1 < DEPTH else None

    def fwd_done(self, l, got):
        if got:
            self.full[l + 1] = self._unpack(got)

    def push(self, l, grads):
        self.pending = (l, [(_w_in_slabs_from_wz(grads[n]) if n == "w_in" else
                             jnp.stack(jnp.split(grads[n], 4, axis=_SHARD_AXIS[n] - 1))).astype(GRAD_WIRE_DTYPE)
                            for n in _BIG])

    def bwd_comm(self):
        return ("exchange", self.pending[1]) if self.pending else None

    def bwd_done(self, got):
        if got:
            self.contrib[self.pending[0]] = got
            self.pending = None

    def finish(self):
        if self.pending:
            self.contrib[self.pending[0]] = _comm_alone("exchange", self.pending[1], "exchange_grads")
            self.pending = None


def kernel(x, p, norm_mix, w_in, conv_w, conv_b, fgate_bias, q_norm, k_norm, lb_logits, hgrn_norm, sgu_norm, spatial_w, spatial_b, w_up, merge_b, w_o, norm_ple, w_ple_gate, w_ple_proj, loss_target, m_norm_mix, m_w_in, m_conv_w, m_conv_b, m_fgate_bias, m_q_norm, m_k_norm, m_lb_logits, m_hgrn_norm, m_sgu_norm, m_spatial_w, m_spatial_b, m_w_up, m_merge_b, m_w_o, m_norm_ple, m_w_ple_gate, m_w_ple_proj, v_norm_mix, v_w_in, v_conv_w, v_conv_b, v_fgate_bias, v_q_norm, v_k_norm, v_lb_logits, v_hgrn_norm, v_sgu_norm, v_spatial_w, v_spatial_b, v_w_up, v_merge_b, v_w_o, v_norm_ple, v_w_ple_gate, v_w_ple_proj):
    loc = dict(locals())
    w = {n: loc[n] for n in _WEIGHTS}
    m = {n: loc["m_" + n] for n in _WEIGHTS}
    v = {n: loc["v_" + n] for n in _WEIGHTS}
    chip = 2 * lax.axis_index("x") + lax.axis_index("y")

    dist = _Dist(w)
    loss_blk, dx, grads = _local_step(x[0], p[:, 0], loss_target[0], w, dist)
    loss = lax.psum(loss_blk[0, 0], ("x", "y", "c"))

    blocks = [_rows128(grads[n]) for n in _SMALL]
    nrows = sum(b.shape[0] for b in blocks)
    packed = jnp.concatenate(blocks + [jnp.zeros(((-nrows) % 8, 128), F32)], axis=0)
    red = _allreduce_small(packed)
    small, off = {}, 0
    for n, b in zip(_SMALL, blocks):
        small[n] = _from_rows128(red[off:off + b.shape[0]], grads[n].shape)
        off += b.shape[0]
    for n in ("conv_w", "merge_b"):
        ax = _SHARD_AXIS[n]
        width = small[n].shape[ax] // 4
        small[n] = lax.dynamic_slice_in_dim(small[n], chip * width, width, axis=ax)

    out_g, out_d, out_m, out_v = {}, {}, {}, {}
    for n in _WEIGHTS:
        shp = w[n].shape
        if n in _BIG:
            w3, m3, v3 = (a.reshape((DEPTH, -1, shp[-1])) for a in (w[n], m[n], v[n]))
            parts = [dist.contrib[l][_BIG.index(n)].reshape((8,) + w3.shape[1:]) for l in range(DEPTH)]
            g, d, nm, nv = _adamw(w3, m3, v3, parts, 8)
        else:
            g, d, nm, nv = _adamw(_as3d(w[n]), _as3d(m[n]), _as3d(v[n]), _as3d(small[n])[None], 1)
        out_g[n], out_d[n], out_m[n], out_v[n] = (a.reshape(shp) for a in (g, d, nm, nv))

    return (loss, dx[None], *[out_g[n] for n in _WEIGHTS], *[out_d[n] for n in _WEIGHTS],
            *[out_m[n] for n in _WEIGHTS], *[out_v[n] for n in _WEIGHTS])
```

```python
import functools

import numpy as np
import jax
import jax.numpy as jnp
from jax import lax
from jax.experimental import pallas as pl
from jax.experimental.pallas import tpu as pltpu

F32 = jnp.float32
MXU_DTYPE = jnp.bfloat16

D_MODEL = 1024
BW = 256
NH = 4
DH = 64
DEPTH = 4
NBR = 4
PLE = 256
CHUNK = 64
SCHUNK = 128
EPS = 1e-6
MASK_VALUE = -1e30
NLEV = 6

ADAM_LR, ADAM_B1, ADAM_B2, ADAM_EPS, ADAM_WD, ADAM_STEP = 0.001, 0.9, 0.999, 1e-08, 0.01, 10

ZM, ZA, ZB, ZC, ZD, ZF = 0, 4096, 5120, 6144, 7168, 8192
ZCOLS = 8320
ZB_GATE = ZB + 3 * BW
_OA, _OB, _OF, _OC, _OD, _OM, _OEND = 0, 1024, 2048, 2052, 3076, 3844, 7940

VMEM_LIMIT = 56 * 1024 * 1024
ROW_TILE = 512
FLASH_TILE = 1024
GRAD_WIRE_DTYPE = jnp.bfloat16
MERGE_TILE = 512
MESH = pl.DeviceIdType.MESH
_ANY = pl.BlockSpec(memory_space=pl.ANY)


def _tile(n, pref):
    t = min(n, pref)
    assert n % t == 0, (n, t)
    return t


def _call(body, *, name, grid, in_specs, out_specs, out_shape, scratch=(), sem=None, aliases=None):
    return pl.pallas_call(
        functools.partial(body), name=name, grid=grid, in_specs=in_specs, out_specs=out_specs,
        out_shape=out_shape, scratch_shapes=list(scratch), input_output_aliases=aliases or {},
        compiler_params=pltpu.CompilerParams(dimension_semantics=sem, vmem_limit_bytes=VMEM_LIMIT))


def _mx(x):
    return x.astype(MXU_DTYPE)


def _dot(a, b):
    return jnp.dot(_mx(a), _mx(b), preferred_element_type=F32)


def _dot_nt(a, b):
    return lax.dot_general(_mx(a), _mx(b), (((1,), (1,)), ((), ())), preferred_element_type=F32)


def _dot_tn(a, b):
    return lax.dot_general(_mx(a), _mx(b), (((0,), (0,)), ((), ())), preferred_element_type=F32)


def _top16(x):
    u = lax.bitcast_convert_type(x, jnp.uint32) & jnp.uint32(0xFFFF0000)
    return lax.bitcast_convert_type(u, F32)


def _split3(x):
    hi = _top16(x)
    r1 = x - hi
    mid = _top16(r1)
    return _mx(hi), _mx(mid), _mx(r1 - mid)


def _dot_exact_lhs01(t, x):
    hi, mid, lo = _split3(x)
    t = _mx(t)
    return (jnp.dot(t, hi, preferred_element_type=F32) + jnp.dot(t, mid, preferred_element_type=F32)
            + jnp.dot(t, lo, preferred_element_type=F32))


def _sigmoid(x):
    return jax.nn.sigmoid(x)


def _silu(x):
    return x * _sigmoid(x)


def _dsilu(x):
    s = _sigmoid(x)
    return s * (1.0 + x * (1.0 - s))


def _log_sigmoid(x):
    return jnp.minimum(x, 0.0) - jnp.log(1.0 + jnp.exp(-jnp.abs(x)))


def _rms_fwd(x, g):
    r = lax.rsqrt(jnp.mean(x * x, axis=-1, keepdims=True) + EPS)
    return x * r * g


def _rms_bwd(x, g, dy):
    r = lax.rsqrt(jnp.mean(x * x, axis=-1, keepdims=True) + EPS)
    n = x * r
    t = dy * g
    dx = r * (t - n * jnp.mean(t * n, axis=-1, keepdims=True))
    return dx, dy * n


def _tri(n, upper=False):
    r = lax.broadcasted_iota(jnp.int32, (n, n), 0)
    c = lax.broadcasted_iota(jnp.int32, (n, n), 1)
    return jnp.where((c >= r) if upper else (r >= c), 1.0, 0.0).astype(F32)


def _hs(h):
    return slice(h * DH, (h + 1) * DH)


def _mm_nn(a, b, out_dtype, *, name, ts=None, tn=None, tk=None, b_transposed=False, comm=None):
    S, K = a.shape
    N = b.shape[0] if b_transposed else b.shape[1]
    ts = _tile(S, ts or ROW_TILE)
    tn = _tile(N, tn or N)
    tk = _tile(K, tk or K)
    nk = K // tk
    b_spec = (pl.BlockSpec((tn, tk), lambda j, i, k: (j, k)) if b_transposed
              else pl.BlockSpec((tk, tn), lambda j, i, k: (k, j)))

    def body(a_ref, b_ref, o_ref, acc_ref):
        k = pl.program_id(2)
        part = _dot_nt(a_ref[...], b_ref[...]) if b_transposed else jnp.dot(a_ref[...], b_ref[...],
                                                                          preferred_element_type=F32)
        if nk == 1:
            o_ref[...] = part.astype(o_ref.dtype)
        else:
            @pl.when(k == 0)
            def _():
                acc_ref[...] = part

            @pl.when(k > 0)
            def _():
                acc_ref[...] += part

            @pl.when(k == nk - 1)
            def _():
                o_ref[...] = acc_ref[...].astype(o_ref.dtype)

    grid = (N // tn, S // ts, nk)

    def when():
        at = [pl.program_id(d) for d in range(3)]
        first = (at[0] == 0) & (at[1] == 0) & (at[2] == 0)
        last = (at[0] == grid[0] - 1) & (at[1] == grid[1] - 1) & (at[2] == grid[2] - 1)
        return first, last, last

    xin, xout, xshape, xsem = _comm_specs(comm)
    res = _call(
        _fuse_comm(body, 2, 1, comm, when), name=name, grid=grid,
        in_specs=[pl.BlockSpec((ts, tk), lambda j, i, k: (i, k)), b_spec] + xin,
        out_specs=[pl.BlockSpec((ts, tn), lambda j, i, k: (i, j))] + xout,
        out_shape=[jax.ShapeDtypeStruct((S, N), out_dtype)] + xshape,
        scratch=[pltpu.VMEM((ts, tn) if nk > 1 else (8, 128), F32)] + xsem,
        sem=("arbitrary", "arbitrary", "arbitrary") if comm else ("parallel", "parallel", "arbitrary"),
        )(a, b, *(comm[1] if comm else []))
    return (res[0], list(res[1:])) if comm else res[0]


def _mm_tn(x, y, *, name, tn=None, ycol=0, n=None):
    S, M = x.shape
    n = n or y.shape[1]
    ts = _tile(S, 4 * ROW_TILE)
    tn = _tile(n, tn or n)
    nj = n // tn

    def body(x_ref, y_ref, o_ref):
        @pl.when(pl.program_id(1) == 0)
        def _():
            o_ref[...] = jnp.zeros_like(o_ref)

        o_ref[...] += lax.dot_general(x_ref[...], y_ref[...], (((0,), (0,)), ((), ())), preferred_element_type=F32)

    return _call(
        body, name=name, grid=(nj, S // ts),
        in_specs=[pl.BlockSpec((ts, M), lambda j, s: (s, 0)), pl.BlockSpec((ts, tn), lambda j, s: (s, ycol * nj + j))],
        out_specs=pl.BlockSpec((M, tn), lambda j, s: (0, j)),
        out_shape=jax.ShapeDtypeStruct((M, n), F32),
        sem=("parallel", "arbitrary"))(x, y)


def _norm_fwd(x, g):
    S = x.shape[0]
    ts = _tile(S, ROW_TILE)

    def body(x_ref, g_ref, h_ref):
        h_ref[...] = _mx(_rms_fwd(x_ref[...], g_ref[...]))

    return _call(
        body, name="norm_fwd", grid=(S // ts,),
        in_specs=[pl.BlockSpec((ts, D_MODEL), lambda i: (i, 0)), pl.BlockSpec((1, D_MODEL), lambda i: (0, 0))],
        out_specs=pl.BlockSpec((ts, D_MODEL), lambda i: (i, 0)),
        out_shape=jax.ShapeDtypeStruct((S, D_MODEL), MXU_DTYPE), sem=("parallel",))(x, g)


def _norm_bwd(x, g, dh, dres):
    S = x.shape[0]
    ts = _tile(S, ROW_TILE)

    def body(x_ref, g_ref, dh_ref, dr_ref, dx_ref, st_ref):
        @pl.when(pl.program_id(0) == 0)
        def _():
            st_ref[...] = jnp.zeros_like(st_ref)

        dx, dgr = _rms_bwd(x_ref[...], g_ref[...], dh_ref[...])
        dx_ref[...] = dr_ref[...] + dx
        st_ref[0:1, :] += jnp.sum(dgr, axis=0, keepdims=True)

    row = pl.BlockSpec((ts, D_MODEL), lambda i: (i, 0))
    return _call(
        body, name="norm_bwd", grid=(S // ts,),
        in_specs=[row, pl.BlockSpec((1, D_MODEL), lambda i: (0, 0)), row, row],
        out_specs=[row, pl.BlockSpec((8, D_MODEL), lambda i: (0, 0))],
        out_shape=[jax.ShapeDtypeStruct((S, D_MODEL), F32), jax.ShapeDtypeStruct((8, D_MODEL), F32)],
        sem=("arbitrary",))(x, g, dh, dres)


def _loss_fwd_bwd(y, target):
    S = y.shape[0]
    ts = _tile(S, ROW_TILE)

    def body(y_ref, t_ref, l_ref, dy_ref):
        @pl.when(pl.program_id(0) == 0)
        def _():
            l_ref[...] = jnp.zeros_like(l_ref)

        err = y_ref[...] - t_ref[...]
        dy_ref[...] = err * (1.0 / D_MODEL)
        rowloss = jnp.mean(err * err, axis=-1, keepdims=True)
        l_ref[...] += 0.5 * jnp.sum(rowloss, axis=0, keepdims=True)

    row = pl.BlockSpec((ts, D_MODEL), lambda i: (i, 0))
    return _call(
        body, name="loss", grid=(S // ts,), in_specs=[row, row],
        out_specs=[pl.BlockSpec((8, 128), lambda i: (0, 0)), row],
        out_shape=[jax.ShapeDtypeStruct((8, 128), F32), jax.ShapeDtypeStruct((S, D_MODEL), F32)],
        sem=("arbitrary",))(y, target)


def _shift_down(x, k, halo, rows):
    y = pltpu.roll(x, k, 0)
    for j in range(k):
        y = jnp.where(rows == j, halo[8 - k + j:8 - k + j + 1, :], y)
    return y


def _shift_up(x, k, halo, rows, n):
    y = pltpu.roll(x, n - k, 0)
    for j in range(k):
        y = jnp.where(rows == n - k + j, halo[j:j + 1, :], y)
    return y


def _conv_parts(za, zh, cw, first, rows):
    ax, ab, ac, ag = za[:, 0:BW], za[:, BW:2 * BW], za[:, 2 * BW:3 * BW], za[:, 3 * BW:4 * BW]
    zz = ac * ax
    hz = jnp.where(first, 0.0, zh[:, 2 * BW:3 * BW] * zh[:, 0:BW])
    zz1 = _shift_down(zz, 1, hz, rows)
    zz2 = _shift_down(zz, 2, hz, rows)
    conv = zz2 * cw[0:1, :] + zz1 * cw[1:2, :] + zz * cw[2:3, :] + cw[3:4, :]
    return ax, ab, ac, ag, zz, zz1, zz2, conv


def _conv_fwd(z, cw):
    S = z.shape[0]
    ts = _tile(S, ROW_TILE)
    hb = ts // 8

    def body(za_ref, zh_ref, cw_ref, y_ref):
        i = pl.program_id(0)
        rows = lax.broadcasted_iota(jnp.int32, (ts, BW), 0)
        ax, ab, ac, ag, zz, zz1, zz2, conv = _conv_parts(za_ref[...], zh_ref[...], cw_ref[...], i == 0, rows)
        y_ref[...] = _mx(ab * conv * _silu(ag))

    return _call(
        body, name="conv_fwd", grid=(S // ts,),
        in_specs=[pl.BlockSpec((ts, 1024), lambda i: (i, ZA // 1024)),
                  pl.BlockSpec((8, 1024), lambda i: (jnp.maximum(i * hb - 1, 0), ZA // 1024)),
                  pl.BlockSpec((8, BW), lambda i: (0, 0))],
        out_specs=pl.BlockSpec((ts, BW), lambda i: (i, 0)),
        out_shape=jax.ShapeDtypeStruct((S, BW), MXU_DTYPE), sem=("parallel",))(z, z, cw)


def _conv_bwd(z, cw, dy, dz):
    S = z.shape[0]
    ts = _tile(S, ROW_TILE)
    hb = ts // 8
    nt = S // ts

    def body(za_ref, zh_ref, zn_ref, cw_ref, dy_ref, dyn_ref, dz_in, dz_ref, st_ref):
        i = pl.program_id(0)

        @pl.when(i == 0)
        def _():
            st_ref[...] = jnp.zeros_like(st_ref)

        cw = cw_ref[...]
        rows = lax.broadcasted_iota(jnp.int32, (ts, BW), 0)
        ax, ab, ac, ag, zz, zz1, zz2, conv = _conv_parts(za_ref[...], zh_ref[...], cw, i == 0, rows)
        dy = dy_ref[...]
        sg = _silu(ag)
        dc = dy * ab * sg
        zn = zn_ref[...]
        dcn = jnp.where(i == nt - 1, 0.0, dyn_ref[...] * zn[:, BW:2 * BW] * _silu(zn[:, 3 * BW:4 * BW]))
        dc1 = _shift_up(dc, 1, dcn, rows, ts)
        dc2 = _shift_up(dc, 2, dcn, rows, ts)
        dzz = dc * cw[2:3, :] + dc1 * cw[1:2, :] + dc2 * cw[0:1, :]
        dz_ref[:, 0:BW] = _mx(dzz * ac)
        dz_ref[:, BW:2 * BW] = _mx(dy * conv * sg)
        dz_ref[:, 2 * BW:3 * BW] = _mx(dzz * ax)
        dz_ref[:, 3 * BW:4 * BW] = _mx(dy * ab * conv * _dsilu(ag))
        st_ref[0:1, :] += jnp.sum(dc * zz2, axis=0, keepdims=True)
        st_ref[1:2, :] += jnp.sum(dc * zz1, axis=0, keepdims=True)
        st_ref[2:3, :] += jnp.sum(dc * zz, axis=0, keepdims=True)
        st_ref[3:4, :] += jnp.sum(dc, axis=0, keepdims=True)

    return _call(
        body, name="conv_bwd", grid=(nt,),
        in_specs=[pl.BlockSpec((ts, 1024), lambda i: (i, ZA // 1024)),
                  pl.BlockSpec((8, 1024), lambda i: (jnp.maximum(i * hb - 1, 0), ZA // 1024)),
                  pl.BlockSpec((8, 1024), lambda i: (jnp.minimum((i + 1) * hb, S // 8 - 1), ZA // 1024)),
                  pl.BlockSpec((8, BW), lambda i: (0, 0)),
                  pl.BlockSpec((ts, BW), lambda i: (i, 0)),
                  pl.BlockSpec((8, BW), lambda i: (jnp.minimum((i + 1) * hb, S // 8 - 1), 0)), _ANY],
        out_specs=[pl.BlockSpec((ts, 1024), lambda i: (i, ZA // 1024)), pl.BlockSpec((8, BW), lambda i: (0, 0))],
        out_shape=[jax.ShapeDtypeStruct((S, ZCOLS), MXU_DTYPE), jax.ShapeDtypeStruct((8, BW), F32)],
        sem=("arbitrary",), aliases={6: 0})(z, z, z, cw, dy, dy, dz)


AW = 128
_AUG = DH


def _split3_f32(x):
    hi = _top16(x)
    r1 = x - hi
    mid = _top16(r1)
    return hi, mid, r1 - mid


def _aug_lanes(n, cols):
    lane = lax.broadcasted_iota(jnp.int32, (n, DH), 1)
    out = jnp.zeros((n, DH), F32)
    for e, c in enumerate(cols):
        out = jnp.where(lane == e, c, out)
    return out


def _put_aug(ref, h, col, parts=None, const=None):
    base = h * AW + _AUG + col
    if parts is None:
        ref[:, base:base + 3] = jnp.full((ref.shape[0], 3), const, ref.dtype)
    else:
        for e, part in enumerate(parts):
            ref[:, base + e:base + e + 1] = part.astype(ref.dtype)


def _attn_prep_fwd(z, fb, gq, gk):
    S = z.shape[0]
    ts = _tile(S, ROW_TILE)

    def body(zb_ref, zf_ref, fb_ref, gq_ref, gk_ref, qa_ref, ka_ref, va_ref, cc_ref, carry):
        @pl.when(pl.program_id(0) == 0)
        def _():
            carry[...] = jnp.zeros_like(carry)

        zb = zb_ref[...]
        gqv, gkv = gq_ref[...], gk_ref[...]
        lf = _log_sigmoid(zf_ref[...] + fb_ref[...])
        cum = _dot_exact_lhs01(_tri(ts), lf) + carry[...]
        carry[...] = cum[ts - 1:ts, :]
        cc_ref[...] = cum
        pieces = _split3_f32(cum)
        for h in range(NH):
            ph = [pc[:, h:h + 1] for pc in pieces]
            qh = _rms_fwd(zb[:, _hs(h)], gqv) * (DH ** -0.5)
            kh = _rms_fwd(zb[:, BW + h * DH:BW + (h + 1) * DH], gkv)
            vh = zb[:, 2 * BW + h * DH:2 * BW + (h + 1) * DH]
            qa_ref[:, _ah(h)] = _mx(jnp.concatenate([qh, _aug_lanes(ts, ph + [1.0] * 3)], axis=1))
            ka_ref[:, _ah(h)] = _mx(jnp.concatenate([kh, _aug_lanes(ts, [1.0] * 3 + [-x for x in ph])], axis=1))
            va_ref[:, _ah(h)] = _mx(jnp.concatenate([vh, _aug_lanes(ts, [-1.0] * 3)], axis=1))

    row = lambda w: pl.BlockSpec((ts, w), lambda i: (i, 0))
    return _call(
        body, name="attn_prep_fwd", grid=(S // ts,),
        in_specs=[pl.BlockSpec((ts, 1024), lambda i: (i, ZB // 1024)), pl.BlockSpec((ts, 128), lambda i: (i, ZF // 128)),
                  pl.BlockSpec((1, 128), lambda i: (0, 0)), pl.BlockSpec((1, DH), lambda i: (0, 0)),
                  pl.BlockSpec((1, DH), lambda i: (0, 0))],
        out_specs=[row(NH * AW), row(NH * AW), row(NH * AW), row(128)],
        out_shape=[jax.ShapeDtypeStruct((S, NH * AW), MXU_DTYPE)] * 3 + [jax.ShapeDtypeStruct((S, 128), F32)],
        scratch=[pltpu.VMEM((1, 128), F32)], sem=("arbitrary",))(z, z, fb, gq, gk)


ROW_CHUNK = 256


def _square_steps(n):
    def when():
        i, j = pl.program_id(0), pl.program_id(1)
        return (i == 0) & (j == 0), (i == n - 1) & (j == 0), (i == n - 1) & (j == n - 1)
    return when


def _fuse_comm(core, n_in, n_out, comm, when):
    nc = 0 if comm is None else len(comm[1])

    def body(*refs):
        cin, xin = refs[:n_in], refs[n_in:n_in + nc]
        a = n_in + nc
        cout, xout = refs[a:a + n_out], refs[a + n_out:a + n_out + nc]
        rest = refs[a + n_out + nc:]
        if nc == 0:
            core(*cin, *cout, *rest)
            return
        cscr, sems = rest[:-3], rest[-3:]
        first, middle, last = when()
        phase = _COMM_PHASES[comm[0]]

        @pl.when(first)
        def _():
            phase(0, xin, xout, sems)

        core(*cin, *cout, *cscr)

        @pl.when(middle)
        def _():
            phase(1, xin, xout, sems)

        @pl.when(last)
        def _():
            phase(2, xin, xout, sems)

    return body


def _comm_specs(comm):
    if comm is None:
        return [], [], [], []
    hbm = pl.BlockSpec(memory_space=pl.ANY)
    nc = len(comm[1])
    return [hbm] * nc, [hbm] * nc, _COMM_OUT[comm[0]](comm[1]), _comm_sems(comm[0], nc)


def _ah(h):
    return slice(h * AW, (h + 1) * AW)


def _ahd(h):
    return slice(h * AW, h * AW + DH)


def _causal(shape, row0, transposed=False):
    r = row0 + lax.broadcasted_iota(jnp.int32, shape, 0)
    c = lax.broadcasted_iota(jnp.int32, shape, 1)
    return (r <= c) if transposed else (r >= c)


def _flash_fwd(qa, ka, va, z, comm=None):
    S = qa.shape[0]
    t = _tile(S, FLASH_TILE)
    n = S // t
    rch = _tile(t, ROW_CHUNK)

    def core(q_ref, k_ref, v_ref, zb_ref, o_ref, lse_ref, y_ref, m_sc, l_sc, acc):
        i, j = pl.program_id(0), pl.program_id(1)

        @pl.when(j == 0)
        def _():
            m_sc[...] = jnp.full_like(m_sc, MASK_VALUE)
            l_sc[...] = jnp.zeros_like(l_sc)
            acc[...] = jnp.zeros_like(acc)

        def block(masked):
            for h in range(NH):
                for rc in range(t // rch):
                    rows = slice(rc * rch, (rc + 1) * rch)
                    s = _dot_nt(q_ref[rows, _ah(h)], k_ref[:, _ah(h)])
                    if masked:
                        s = jnp.where(_causal(s.shape, rc * rch), s, MASK_VALUE)
                    m_old = m_sc[h, rows, :]
                    m_new = jnp.maximum(m_old, jnp.max(s, axis=-1, keepdims=True))
                    p = jnp.exp(s - m_new)
                    alpha = jnp.exp(m_old - m_new)
                    l_sc[h, rows, :] = alpha * l_sc[h, rows, :] + jnp.sum(p, axis=-1, keepdims=True)
                    acc[rows, _hs(h)] = alpha * acc[rows, _hs(h)] + _dot(p, v_ref[:, _ahd(h)])
                    m_sc[h, rows, :] = m_new

        @pl.when(j < i)
        def _():
            block(False)

        @pl.when(j == i)
        def _():
            block(True)
            lse_ref[...] = jnp.zeros_like(lse_ref)
            for h in range(NH):
                o_ref[:, _hs(h)] = acc[:, _hs(h)] / l_sc[h]
                lse_ref[:, h:h + 1] = m_sc[h] + jnp.log(l_sc[h])
            y_ref[...] = _mx(o_ref[...] * _silu(zb_ref[...]))

    qspec = lambda w: pl.BlockSpec((t, w), lambda i, j: (i, 0))
    kspec = lambda w: pl.BlockSpec((t, w), lambda i, j: (jnp.minimum(j, i), 0))
    xin, xout, xshape, xsem = _comm_specs(comm)
    res = _call(
        _fuse_comm(core, 4, 3, comm, _square_steps(n)), name="flash_fwd", grid=(n, n),
        in_specs=[qspec(NH * AW), kspec(NH * AW), kspec(NH * AW),
                  pl.BlockSpec((t, BW), lambda i, j: (i, ZB_GATE // BW))] + xin,
        out_specs=[qspec(BW), qspec(128), qspec(BW)] + xout,
        out_shape=[jax.ShapeDtypeStruct((S, BW), F32), jax.ShapeDtypeStruct((S, 128), F32),
                   jax.ShapeDtypeStruct((S, BW), MXU_DTYPE)] + xshape,
        scratch=[pltpu.VMEM((NH, t, 1), F32), pltpu.VMEM((NH, t, 1), F32), pltpu.VMEM((t, BW), F32)] + xsem,
        sem=("arbitrary", "arbitrary"))(qa, ka, va, z, *(comm[1] if comm else []))
    return res[0], res[1], res[2], list(res[3:])


def _attn_gate_bwd(dy, o, z, qa, cc, lse):
    S = dy.shape[0]
    ts = _tile(S, ROW_TILE)

    def body(dy_ref, o_ref, zb_ref, qa_ref, cc_ref, lse_ref, dg_ref, qb_ref, doa_ref):
        g = zb_ref[...]
        dy, o = dy_ref[...], o_ref[...]
        do = dy * _silu(g)
        dg_ref[...] = _mx(dy * o * _dsilu(g))
        qb_ref[...] = qa_ref[...]
        doa_ref[...] = jnp.zeros_like(doa_ref)
        shifted = _split3_f32(cc_ref[...] - lse_ref[...])
        for h in range(NH):
            doh = do[:, _hs(h)]
            doa_ref[:, _ahd(h)] = _mx(doh)
            delta = jnp.sum(doh * o[:, _hs(h)], axis=-1, keepdims=True)
            _put_aug(doa_ref, h, 0, parts=_split3_f32(delta))
            _put_aug(qb_ref, h, 0, parts=[pc[:, h:h + 1] for pc in shifted])

    row = lambda w: pl.BlockSpec((ts, w), lambda i: (i, 0))
    return _call(
        body, name="attn_gate_bwd", grid=(S // ts,),
        in_specs=[pl.BlockSpec((ts, BW), lambda i: (i, 1)), row(BW), pl.BlockSpec((ts, BW), lambda i: (i, ZB_GATE // BW)),
                  row(NH * AW), row(128), row(128)],
        out_specs=[row(BW), row(NH * AW), row(NH * AW)],
        out_shape=[jax.ShapeDtypeStruct((S, BW), MXU_DTYPE), jax.ShapeDtypeStruct((S, NH * AW), MXU_DTYPE),
                   jax.ShapeDtypeStruct((S, NH * AW), MXU_DTYPE)],
        sem=("parallel",))(dy, o, z, qa, cc, lse)


def _aug_value(ref, h, rows=slice(None)):
    base = h * AW + _AUG
    x = ref[rows, base:base + 3].astype(F32)
    return x[:, 0:1] + x[:, 1:2] + x[:, 2:3]


def _flash_bwd_dq(qb, ka, va, doa, comm=None):
    S = qb.shape[0]
    t = _tile(S, FLASH_TILE)
    n = S // t
    rch = _tile(t, ROW_CHUNK)

    def core(q_ref, k_ref, v_ref, do_ref, dq_ref, do2_ref, dr_sc):
        i, j = pl.program_id(0), pl.program_id(1)

        @pl.when(j == 0)
        def _():
            dq_ref[...] = jnp.zeros_like(dq_ref)
            dr_sc[...] = jnp.zeros_like(dr_sc)

        def block(masked):
            for h in range(NH):
                for rc in range(t // rch):
                    rows = slice(rc * rch, (rc + 1) * rch)
                    p = jnp.exp(_dot_nt(q_ref[rows, _ah(h)], k_ref[:, _ah(h)]))
                    if masked:
                        p = jnp.where(_causal(p.shape, rc * rch), p, 0.0)
                    ds = p * _dot_nt(do_ref[rows, _ah(h)], v_ref[:, _ah(h)])
                    dq_ref[rows, _hs(h)] += _dot(ds, k_ref[:, _ahd(h)])
                    dr_sc[h, rows, :] += jnp.sum(ds, axis=-1, keepdims=True)

        @pl.when(j < i)
        def _():
            block(False)

        @pl.when(j == i)
        def _():
            block(True)
            do2_ref[...] = do_ref[...]
            for h in range(NH):
                _put_aug(do2_ref, h, 0, parts=_split3_f32(_aug_value(do_ref, h) + dr_sc[h]))

    qspec = lambda w: pl.BlockSpec((t, w), lambda i, j: (i, 0))
    kspec = lambda w: pl.BlockSpec((t, w), lambda i, j: (jnp.minimum(j, i), 0))
    xin, xout, xshape, xsem = _comm_specs(comm)
    res = _call(
        _fuse_comm(core, 4, 2, comm, _square_steps(n)), name="flash_bwd_dq", grid=(n, n),
        in_specs=[qspec(NH * AW), kspec(NH * AW), kspec(NH * AW), qspec(NH * AW)] + xin,
        out_specs=[qspec(BW), qspec(NH * AW)] + xout,
        out_shape=[jax.ShapeDtypeStruct((S, BW), F32), jax.ShapeDtypeStruct((S, NH * AW), MXU_DTYPE)] + xshape,
        scratch=[pltpu.VMEM((NH, t, 1), F32)] + xsem,
        sem=("arbitrary", "arbitrary"))(qb, ka, va, doa, *(comm[1] if comm else []))
    return res[0], res[1], list(res[2:])


def _flash_bwd_dkv(qb, ka, va, doa):
    S = qb.shape[0]
    t = _tile(S, FLASH_TILE)
    n = S // t

    def body(q_ref, k_ref, v_ref, do_ref, dk_ref, dv_ref, dc_ref):
        j, i = pl.program_id(0), pl.program_id(1)

        @pl.when(i == 0)
        def _():
            dk_ref[...] = jnp.zeros_like(dk_ref)
            dv_ref[...] = jnp.zeros_like(dv_ref)
            dc_ref[...] = jnp.zeros_like(dc_ref)

        def block(masked):
            for h in range(NH):
                pt = jnp.exp(_dot_nt(k_ref[:, _ah(h)], q_ref[:, _ah(h)]))
                if masked:
                    pt = jnp.where(_causal(pt.shape, 0, transposed=True), pt, 0.0)
                dst = pt * _dot_nt(v_ref[:, _ah(h)], do_ref[:, _ah(h)])
                dv_ref[:, _hs(h)] += _dot(pt, do_ref[:, _ahd(h)])
                dk_ref[:, _hs(h)] += _dot(dst, q_ref[:, _ahd(h)])
                dc_ref[:, h:h + 1] += -jnp.sum(dst, axis=-1, keepdims=True)

        @pl.when(i > j)
        def _():
            block(False)

        @pl.when(i == j)
        def _():
            block(True)

    qspec = lambda w: pl.BlockSpec((t, w), lambda j, i: (jnp.maximum(i, j), 0))
    kspec = lambda w: pl.BlockSpec((t, w), lambda j, i: (j, 0))
    return _call(
        body, name="flash_bwd_dkv", grid=(n, n),
        in_specs=[qspec(NH * AW), kspec(NH * AW), kspec(NH * AW), qspec(NH * AW)],
        out_specs=[kspec(BW), kspec(BW), kspec(128)],
        out_shape=[jax.ShapeDtypeStruct((S, BW), F32), jax.ShapeDtypeStruct((S, BW), F32),
                   jax.ShapeDtypeStruct((S, 128), F32)],
        sem=("parallel", "arbitrary"))(qb, ka, va, doa)


def _attn_prep_bwd(z, fb, gq, gk, dq, dk, dv, dgate, dcc, dz):
    S = z.shape[0]
    ts = _tile(S, ROW_TILE)
    nt = S // ts

    def body(zb_ref, zf_ref, fb_ref, gq_ref, gk_ref, dq_ref, dk_ref, dv_ref, dg_ref, dcc_ref, dz_in, dzb_ref, dzf_ref,
             st_ref, carry):
        @pl.when(pl.program_id(0) == 0)
        def _():
            carry[...] = jnp.zeros_like(carry)
            st_ref[...] = jnp.zeros_like(st_ref)

        zb = zb_ref[...]
        gqv, gkv = gq_ref[...], gk_ref[...]
        dqv, dkv = dq_ref[...], dk_ref[...]
        sq = jnp.zeros((1, DH), F32)
        sk = jnp.zeros((1, DH), F32)
        for h in range(NH):
            dx, dgr = _rms_bwd(zb[:, _hs(h)], gqv, dqv[:, _hs(h)] * (DH ** -0.5))
            dzb_ref[:, _hs(h)] = _mx(dx)
            sq = sq + jnp.sum(dgr, axis=0, keepdims=True)
            ks = slice(BW + h * DH, BW + (h + 1) * DH)
            dx, dgr = _rms_bwd(zb[:, ks], gkv, dkv[:, _hs(h)])
            dzb_ref[:, ks] = _mx(dx)
            sk = sk + jnp.sum(dgr, axis=0, keepdims=True)
        dzb_ref[:, 2 * BW:3 * BW] = _mx(dv_ref[...])
        dzb_ref[:, 3 * BW:4 * BW] = dg_ref[...]
        dc = dcc_ref[...]
        dlf = _dot_exact_lhs01(_tri(ts, upper=True), dc) + carry[...]
        carry[...] = dlf[0:1, :]
        dfz = dlf * _sigmoid(-(zf_ref[...] + fb_ref[...]))
        dzf_ref[...] = _mx(dfz)
        st_ref[0:1, 0:DH] += sq
        st_ref[1:2, 0:DH] += sk
        st_ref[2:3, :] += jnp.sum(dfz, axis=0, keepdims=True)

    rev = lambda w, c=0: pl.BlockSpec((ts, w), lambda i: (nt - 1 - i, c))
    one = lambda w: pl.BlockSpec((1, w), lambda i: (0, 0))
    return _call(
        body, name="attn_prep_bwd", grid=(nt,),
        in_specs=[rev(1024, ZB // 1024), rev(128, ZF // 128), one(128), one(DH), one(DH),
                  rev(BW), rev(BW), rev(BW), rev(BW), rev(128), _ANY],
        out_specs=[rev(1024, ZB // 1024), rev(128), pl.BlockSpec((8, 128), lambda i: (0, 0))],
        out_shape=[jax.ShapeDtypeStruct((S, ZCOLS), MXU_DTYPE), jax.ShapeDtypeStruct((S, 128), MXU_DTYPE),
                   jax.ShapeDtypeStruct((8, 128), F32)],
        scratch=[pltpu.VMEM((1, 128), F32)], sem=("arbitrary",), aliases={10: 0})(z, z, fb, gq, gk, dq, dk, dv, dgate, dcc, dz)


def _put_fgate_cols(dzf, dz):
    S = dzf.shape[0]
    ts = _tile(S, ROW_TILE)

    def body(f_ref, dz_in, o_ref):
        o_ref[...] = f_ref[...]

    return _call(
        body, name="put_fgate_cols", grid=(S // ts,),
        in_specs=[pl.BlockSpec((ts, 128), lambda i: (i, 0)), _ANY],
        out_specs=pl.BlockSpec((ts, 128), lambda i: (i, ZF // 128)),
        out_shape=jax.ShapeDtypeStruct((S, ZCOLS), MXU_DTYPE), sem=("parallel",), aliases={1: 0})(dzf, dz)


def _hgrn_consts():
    C = CHUNK
    t = np.arange(C)[:, None]
    j = np.arange(C)[None, :]
    masks = []
    for lev in range(NLEV):
        m = C >> (lev + 1)
        blk, pos = t // (2 * m), t % (2 * m)
        sblk, spos = j // (2 * m), j % (2 * m)
        masks.append((blk == sblk) & (pos >= m) & (spos < m))
    masks.append(t == j)
    return (j <= t).astype(np.float32), np.stack(masks).astype(np.float32)


def _level_exponents(b, lg):
    row = lax.broadcasted_iota(jnp.int32, (CHUNK, 1), 0)
    eqs, eks = [], []
    for lev in range(NLEV):
        m = CHUNK >> (lev + 1)
        pos = jnp.bitwise_and(row, 2 * m - 1)
        if 2 * m >= 8:
            b3 = b.reshape(CHUNK // (2 * m), 2 * m, b.shape[1])
            mid = jnp.broadcast_to(b3[:, m - 1:m, :], b3.shape).reshape(b.shape)
            eqs.append(jnp.where(pos >= m, b - mid, 0.0))
            eks.append(jnp.where(pos < m, mid - b, 0.0))
        elif m == 2:
            eqs.append(jnp.where(pos == 2, lg, jnp.where(pos == 3, lg + pltpu.roll(lg, 1, 0), 0.0)))
            eks.append(jnp.where(pos == 0, pltpu.roll(lg, CHUNK - 1, 0), 0.0))
        else:
            eqs.append(jnp.where(pos == 1, lg, 0.0))
            eks.append(jnp.zeros_like(lg))
    return jnp.concatenate([b] + eqs + eks, axis=0)


def _hgrn_chunk_fwd(zc, lb, tmat):
    cq, cf, ci = zc[:, 0:BW], zc[:, BW:2 * BW], zc[:, 2 * BW:3 * BW]
    q = _silu(cq)
    sg = _sigmoid(cf)
    g = lb + (1.0 - lb) * sg
    lg = jnp.log(g)
    kf = (1.0 - lb) * _sigmoid(-cf)
    b = _dot_exact_lhs01(tmat, lg)
    e = _level_exponents(b, lg)
    blast = b[CHUNK - 1:CHUNK, :]
    return cq, cf, q, sg, g, kf, ci, e, b, blast


def _round_mx(x):
    if MXU_DTYPE != jnp.bfloat16:
        return x
    u = lax.bitcast_convert_type(x, jnp.uint32)
    u = (u + jnp.uint32(0x7FFF) + ((u >> 16) & jnp.uint32(1))) & jnp.uint32(0xFFFF0000)
    return lax.bitcast_convert_type(u, F32)


def _hgrn_scores(q, kf, e, masks, h, scores=True):
    qh, kh = q[:, _hs(h)], kf[:, _hs(h)]
    ql, kl = [], []
    a = None
    for lev in range(NLEV + 1):
        if lev < NLEV:
            eq = jnp.exp(e[(1 + lev) * CHUNK:(2 + lev) * CHUNK, _hs(h)])
            ek = jnp.exp(e[(1 + NLEV + lev) * CHUNK:(2 + NLEV + lev) * CHUNK, _hs(h)])
            ql.append((_round_mx(qh * eq), eq))
            kl.append((_round_mx(kh * ek), ek))
        else:
            ql.append((_round_mx(qh), None))
            kl.append((_round_mx(kh), None))
        if scores:
            term = masks[lev] * _dot_nt(ql[-1][0], kl[-1][0])
            a = term if a is None else a + term
    return a, ql, kl


def _hgrn_fwd(z, lb, gain, tmat, masks):
    S = z.shape[0]
    ts = _tile(S, ROW_TILE)
    nc = ts // CHUNK

    def body(zc_ref, lb_ref, gn_ref, tm_ref, mk_ref, o_ref, st_ref, a_ref, y_ref, state):
        @pl.when(pl.program_id(0) == 0)
        def _():
            state[...] = jnp.zeros_like(state)

        lbv, gn, tm, mk = lb_ref[...], gn_ref[...], tm_ref[...], mk_ref[...]

        def chunk(c, carry):
            r0 = pl.multiple_of(c * CHUNK, CHUNK)
            zc = zc_ref[pl.ds(r0, CHUNK), :]
            cq, cf, q, sg, g, kf, v, e, b, blast = _hgrn_chunk_fwd(zc, lbv, tm)
            qe = q * jnp.exp(b)
            kd = kf * jnp.exp(blast - b)
            st_ref[pl.ds(r0, CHUNK), :] = state[...]
            for h in range(NH):
                sth = state[:, _hs(h)]
                a = _mx(_hgrn_scores(q, kf, e, mk, h)[0])
                a_ref[pl.ds(r0, CHUNK), _hs(h)] = a
                oh = _dot_nt(qe[:, _hs(h)], sth) + _dot(a, v[:, _hs(h)])
                state[:, _hs(h)] = sth * jnp.exp(blast[:, _hs(h)]) + _dot_tn(v[:, _hs(h)], kd[:, _hs(h)])
                o_ref[pl.ds(r0, CHUNK), _hs(h)] = oh
                yn = _rms_fwd(oh, gn[:, _hs(h)])
                y_ref[pl.ds(r0, CHUNK), _hs(h)] = _mx(yn * _silu(zc[:, 3 * BW + h * DH:3 * BW + (h + 1) * DH]))
            return carry

        lax.fori_loop(0, nc, chunk, 0, unroll=min(4, nc))

    row = pl.BlockSpec((ts, BW), lambda i: (i, 0))
    one = pl.BlockSpec((1, BW), lambda i: (0, 0))
    return _call(
        body, name="hgrn_fwd", grid=(S // ts,),
        in_specs=[pl.BlockSpec((ts, 1024), lambda i: (i, ZC // 1024)), one, one,
                  pl.BlockSpec(tmat.shape, lambda i: (0, 0)), pl.BlockSpec(masks.shape, lambda i: (0, 0, 0))],
        out_specs=[row, row, row, row],
        out_shape=[jax.ShapeDtypeStruct((S, BW), F32), jax.ShapeDtypeStruct((S, BW), F32),
                   jax.ShapeDtypeStruct((S, BW), MXU_DTYPE), jax.ShapeDtypeStruct((S, BW), MXU_DTYPE)],
        scratch=[pltpu.VMEM((CHUNK, BW), F32)], sem=("arbitrary",))(z, lb, gain, tmat, masks)


def _hgrn_bwd(z, lb, gain, tmat, masks, o_pre, states, scores, dy, dz):
    S = z.shape[0]
    ts = _tile(S, ROW_TILE)
    nt = S // ts
    nc = ts // CHUNK

    def body(zc_ref, lb_ref, gn_ref, tm_ref, mk_ref, o_ref, st_ref, a_ref, dy_ref, dz_in, dz_ref, stat_ref, dstate):
        @pl.when(pl.program_id(0) == 0)
        def _():
            dstate[...] = jnp.zeros_like(dstate)
            stat_ref[...] = jnp.zeros_like(stat_ref)

        lbv, gn, tm, mk = lb_ref[...], gn_ref[...], tm_ref[...], mk_ref[...]
        upper = _tri(CHUNK, upper=True)
        lower_strict = 1.0 - upper

        def chunk(cc, carry):
            c = nc - 1 - cc
            r0 = pl.multiple_of(c * CHUNK, CHUNK)
            zc = zc_ref[pl.ds(r0, CHUNK), :]
            cq, cf, q, sg, g, kf, v, e, b, blast = _hgrn_chunk_fwd(zc, lbv, tm)
            eb = jnp.exp(b)
            ebl = jnp.exp(blast - b)
            qe = q * eb
            kd = kf * ebl
            o = o_ref[pl.ds(r0, CHUNK), :]
            dyv = dy_ref[pl.ds(r0, CHUNK), :]
            stp = st_ref[pl.ds(r0, CHUNK), :]
            cg = zc[:, 3 * BW:4 * BW]
            sgate = _silu(cg)
            dq_parts, dk_parts, dv_parts, dcg_parts = [], [], [], []
            dgain, up_parts, lo_parts, const_parts = [], [], [], []
            for h in range(NH):
                hs = _hs(h)
                oh = o[:, hs]
                r = lax.rsqrt(jnp.mean(oh * oh, axis=-1, keepdims=True) + EPS)
                nrm = oh * r
                dyn = dyv[:, hs] * sgate[:, hs]
                dcg_parts.append(dyv[:, hs] * nrm * gn[:, hs] * _dsilu(cg[:, hs]))
                dgain.append(jnp.sum(dyn * nrm, axis=0, keepdims=True))
                tt = dyn * gn[:, hs]
                doh = r * (tt - nrm * jnp.mean(tt * nrm, axis=-1, keepdims=True))
                _, ql, kl = _hgrn_scores(q, kf, e, mk, h, scores=False)
                a = a_ref[pl.ds(r0, CHUNK), hs]
                da = _dot_nt(doh, v[:, hs])
                dsth = dstate[:, hs]
                ebh = jnp.exp(blast[:, hs])
                dv_parts.append(_dot_tn(a, doh) + _dot_nt(kd[:, hs], dsth))
                dq_inter = eb[:, hs] * _dot(doh, stp[:, hs])
                dk_state = ebl[:, hs] * _dot(v[:, hs], dsth)
                dqh, dkh, gh = dq_inter, dk_state, None
                for lev in range(NLEV + 1):
                    dal = mk[lev] * da
                    xq = _dot(dal, kl[lev][0])
                    yk = _dot_tn(dal, ql[lev][0])
                    gterm = ql[lev][0] * xq - kl[lev][0] * yk
                    gh = gterm if gh is None else gh + gterm
                    dqh = dqh + (xq if lev == NLEV else ql[lev][1] * xq)
                    dkh = dkh + (yk if lev == NLEV else kl[lev][1] * yk)
                up_parts.append(gh + q[:, hs] * dq_inter)
                lo_parts.append(kf[:, hs] * dk_state)
                const_parts.append(jnp.sum(dsth * stp[:, hs], axis=0, keepdims=True) * ebh)
                dstate[:, hs] = dsth * ebh + _dot_tn(doh, qe[:, hs])
                dq_parts.append(dqh)
                dk_parts.append(dkh)
            dq = jnp.concatenate(dq_parts, axis=1)
            dk = jnp.concatenate(dk_parts, axis=1)
            dlg = (_dot_exact_lhs01(upper, jnp.concatenate(up_parts, axis=1))
                   + _dot_exact_lhs01(lower_strict, jnp.concatenate(lo_parts, axis=1))
                   + jnp.concatenate(const_parts, axis=1))
            dsg = sg * (1.0 - sg)
            dz_ref[pl.ds(r0, CHUNK), 0:BW] = _mx(dq * _dsilu(cq))
            dz_ref[pl.ds(r0, CHUNK), BW:2 * BW] = _mx((dlg / g - dk) * (1.0 - lbv) * dsg)
            dz_ref[pl.ds(r0, CHUNK), 2 * BW:3 * BW] = _mx(jnp.concatenate(dv_parts, axis=1))
            dz_ref[pl.ds(r0, CHUNK), 3 * BW:4 * BW] = _mx(jnp.concatenate(dcg_parts, axis=1))
            stat_ref[0:1, :] += jnp.concatenate(dgain, axis=1)
            stat_ref[1:2, :] += jnp.sum((dlg / g - dk) * (1.0 - sg), axis=0, keepdims=True)
            return carry

        lax.fori_loop(0, nc, chunk, 0, unroll=min(4, nc))

    rev = lambda w, c=0: pl.BlockSpec((ts, w), lambda i: (nt - 1 - i, c))
    one = pl.BlockSpec((1, BW), lambda i: (0, 0))
    return _call(
        body, name="hgrn_bwd", grid=(nt,),
        in_specs=[rev(1024, ZC // 1024), one, one, pl.BlockSpec(tmat.shape, lambda i: (0, 0)),
                  pl.BlockSpec(masks.shape, lambda i: (0, 0, 0)), rev(BW), rev(BW), rev(BW), rev(BW, 2), _ANY],
        out_specs=[rev(1024, ZC // 1024), pl.BlockSpec((8, BW), lambda i: (0, 0))],
        out_shape=[jax.ShapeDtypeStruct((S, ZCOLS), MXU_DTYPE), jax.ShapeDtypeStruct((8, BW), F32)],
        scratch=[pltpu.VMEM((CHUNK, BW), F32)],
        sem=("arbitrary",), aliases={9: 0})(z, lb, gain, tmat, masks, o_pre, states, scores, dy, dz)


def _lower_bounds_fwd(lb_logits):
    def body(l_ref, o_ref):
        l = l_ref[...]
        m = jnp.max(l, axis=0, keepdims=True)
        ex = jnp.exp(l - m)
        p = ex / jnp.sum(ex, axis=0, keepdims=True)
        cs = p[0:1, :]
        o_ref[0:1, :] = jnp.clip(cs - p[0:1, :], 0.0, 1.0)
        for d in range(1, DEPTH):
            cs = cs + p[d:d + 1, :]
            o_ref[d:d + 1, :] = jnp.clip(cs - p[0:1, :], 0.0, 1.0)

    full = pl.BlockSpec((DEPTH, BW), lambda: (0, 0))
    return _call(body, name="lower_bounds_fwd", grid=(), in_specs=[full], out_specs=full,
                 out_shape=jax.ShapeDtypeStruct((DEPTH, BW), F32))(lb_logits)


def _lower_bounds_bwd(lb_logits, dlow):
    def body(l_ref, d_ref, o_ref):
        l = l_ref[...]
        m = jnp.max(l, axis=0, keepdims=True)
        ex = jnp.exp(l - m)
        p = ex / jnp.sum(ex, axis=0, keepdims=True)
        dl = d_ref[...]
        cs = p[0:1, :]
        dcs = []
        for d in range(DEPTH):
            if d > 0:
                cs = cs + p[d:d + 1, :]
            val = cs - p[0:1, :]
            dcs.append(jnp.where((val > 0.0) & (val < 1.0), dl[d:d + 1, :], 0.0))
        total = dcs[0] + dcs[1] + dcs[2] + dcs[3]
        dp = []
        for j in range(DEPTH):
            s = dcs[j]
            for d in range(j + 1, DEPTH):
                s = s + dcs[d]
            dp.append(s - total if j == 0 else s)
        inner = p[0:1, :] * dp[0]
        for j in range(1, DEPTH):
            inner = inner + p[j:j + 1, :] * dp[j]
        for j in range(DEPTH):
            o_ref[j:j + 1, :] = p[j:j + 1, :] * (dp[j] - inner)

    full = pl.BlockSpec((DEPTH, BW), lambda: (0, 0))
    return _call(body, name="lower_bounds_bwd", grid=(), in_specs=[full, full], out_specs=full,
                 out_shape=jax.ShapeDtypeStruct((DEPTH, BW), F32))(lb_logits, dlow)


def _sgu_fwd(z, gv, ws, bs):
    S = z.shape[0]
    ts = _tile(S, ROW_TILE)
    nc = ts // SCHUNK

    def body(zd_ref, gv_ref, ws_ref, bs_ref, y_ref):
        gvv, bsv = gv_ref[...], bs_ref[...]
        tril = _tri(SCHUNK)
        for c in range(nc):
            rs = slice(c * SCHUNK, (c + 1) * SCHUNK)
            zd = zd_ref[rs, :]
            for h in range(NH):
                vn = _rms_fwd(zd[:, BW + h * DH:BW + (h + 1) * DH], gvv[:, _hs(h)])
                s = _dot(ws_ref[h] * tril, vn) + bsv[:, h:h + 1]
                y_ref[rs, _hs(h)] = _mx(zd[:, _hs(h)] * s * _silu(zd[:, 2 * BW + h * DH:2 * BW + (h + 1) * DH]))

    return _call(
        body, name="sgu_fwd", grid=(S // ts,),
        in_specs=[pl.BlockSpec((ts, 1024), lambda i: (i, ZD // 1024)), pl.BlockSpec((1, BW), lambda i: (0, 0)),
                  pl.BlockSpec((NH, SCHUNK, SCHUNK), lambda i: (0, 0, 0)), pl.BlockSpec((SCHUNK, 128), lambda i: (0, 0))],
        out_specs=pl.BlockSpec((ts, BW), lambda i: (i, 0)),
        out_shape=jax.ShapeDtypeStruct((S, BW), MXU_DTYPE), sem=("parallel",))(z, gv, ws, bs)


def _sgu_bwd(z, gv, ws, bs, dy, dz):
    S = z.shape[0]
    ts = _tile(S, ROW_TILE)
    nc = ts // SCHUNK

    def body(zd_ref, gv_ref, ws_ref, bs_ref, dy_ref, dz_in, dz_ref, dws_ref, dbs_ref, st_ref):
        @pl.when(pl.program_id(0) == 0)
        def _():
            dws_ref[...] = jnp.zeros_like(dws_ref)
            dbs_ref[...] = jnp.zeros_like(dbs_ref)
            st_ref[...] = jnp.zeros_like(st_ref)

        gvv, bsv = gv_ref[...], bs_ref[...]
        tril = _tri(SCHUNK)
        dz_ref[:, 3 * BW:4 * BW] = jnp.zeros((ts, BW), MXU_DTYPE)
        for c in range(nc):
            rs = slice(c * SCHUNK, (c + 1) * SCHUNK)
            zd = zd_ref[rs, :]
            dyv = dy_ref[rs, :]
            for h in range(NH):
                hs = _hs(h)
                u = zd[:, hs]
                vraw = zd[:, BW + h * DH:BW + (h + 1) * DH]
                gt = zd[:, 2 * BW + h * DH:2 * BW + (h + 1) * DH]
                gvh = gvv[:, hs]
                vn = _rms_fwd(vraw, gvh)
                wm = ws_ref[h] * tril
                s = _dot(wm, vn) + bsv[:, h:h + 1]
                sil = _silu(gt)
                d = dyv[:, hs]
                ds = d * u * sil
                dz_ref[rs, hs] = _mx(d * s * sil)
                dz_ref[rs, 2 * BW + h * DH:2 * BW + (h + 1) * DH] = _mx(d * u * s * _dsilu(gt))
                dws_ref[h] += tril * _dot_nt(ds, vn)
                dbs_ref[:, h:h + 1] += jnp.sum(ds, axis=-1, keepdims=True)
                dvn = _dot_tn(wm, ds)
                dx, dgr = _rms_bwd(vraw, gvh, dvn)
                dz_ref[rs, BW + h * DH:BW + (h + 1) * DH] = _mx(dx)
                st_ref[0:1, hs] += jnp.sum(dgr, axis=0, keepdims=True)

    return _call(
        body, name="sgu_bwd", grid=(S // ts,),
        in_specs=[pl.BlockSpec((ts, 1024), lambda i: (i, ZD // 1024)), pl.BlockSpec((1, BW), lambda i: (0, 0)),
                  pl.BlockSpec((NH, SCHUNK, SCHUNK), lambda i: (0, 0, 0)), pl.BlockSpec((SCHUNK, 128), lambda i: (0, 0)),
                  pl.BlockSpec((ts, BW), lambda i: (i, 3)), _ANY],
        out_specs=[pl.BlockSpec((ts, 1024), lambda i: (i, ZD // 1024)), pl.BlockSpec((NH, SCHUNK, SCHUNK), lambda i: (0, 0, 0)),
                   pl.BlockSpec((SCHUNK, 128), lambda i: (0, 0)), pl.BlockSpec((8, BW), lambda i: (0, 0))],
        out_shape=[jax.ShapeDtypeStruct((S, ZCOLS), MXU_DTYPE), jax.ShapeDtypeStruct((NH, SCHUNK, SCHUNK), F32),
                   jax.ShapeDtypeStruct((SCHUNK, 128), F32), jax.ShapeDtypeStruct((8, BW), F32)],
        sem=("arbitrary",), aliases={5: 0})(z, gv, ws, bs, dy, dz)


def _merge_fwd(x, z, ys, wup, mb, wo):
    S = x.shape[0]
    ts = _tile(S, MERGE_TILE)

    def body(x_ref, zm_ref, ya_ref, yb_ref, yc_ref, yd_ref, wup_ref, mb_ref, wo_ref, x1_ref, mg_ref):
        yrefs = (ya_ref, yb_ref, yc_ref, yd_ref)
        mbv = mb_ref[...]
        merged = None
        for b in range(NBR):
            cs = slice(b * D_MODEL, (b + 1) * D_MODEL)
            term = _sigmoid(zm_ref[:, cs] + mbv[b:b + 1, :]) * jnp.dot(yrefs[b][...], wup_ref[b],
                                                                      preferred_element_type=F32)
            merged = term if merged is None else merged + term
        mg = _mx(merged)
        mg_ref[...] = mg
        x1_ref[...] = x_ref[...] + jnp.dot(mg, wo_ref[...], preferred_element_type=F32)

    row = lambda w: pl.BlockSpec((ts, w), lambda i: (i, 0))
    return _call(
        body, name="merge_fwd", grid=(S // ts,),
        in_specs=[row(D_MODEL), pl.BlockSpec((ts, 4096), lambda i: (i, 0)), row(BW), row(BW), row(BW), row(BW),
                  pl.BlockSpec((NBR, BW, D_MODEL), lambda i: (0, 0, 0)), pl.BlockSpec((NBR, D_MODEL), lambda i: (0, 0)),
                  pl.BlockSpec((D_MODEL, D_MODEL), lambda i: (0, 0))],
        out_specs=[row(D_MODEL), row(D_MODEL)],
        out_shape=[jax.ShapeDtypeStruct((S, D_MODEL), F32), jax.ShapeDtypeStruct((S, D_MODEL), MXU_DTYPE)],
        sem=("parallel",))(x, z, *ys, wup, mb, wo)


def _merge_bwd(dx1, z, ys, wup, mb, wo):
    S = dx1.shape[0]
    ts = _tile(S, MERGE_TILE)

    def body(dx_ref, zm_ref, ya_ref, yb_ref, yc_ref, yd_ref, wup_ref, mb_ref, wo_ref,
             dzm_ref, du_ref, dxb_ref, dy_ref, st_ref):
        @pl.when(pl.program_id(0) == 0)
        def _():
            st_ref[...] = jnp.zeros_like(st_ref)

        yrefs = (ya_ref, yb_ref, yc_ref, yd_ref)
        mbv = mb_ref[...]
        dxb = _mx(dx_ref[...])
        dxb_ref[...] = dxb
        dmerged = _dot_nt(dxb, wo_ref[...])
        for b in range(NBR):
            cs = slice(b * D_MODEL, (b + 1) * D_MODEL)
            u = jnp.dot(yrefs[b][...], wup_ref[b], preferred_element_type=F32)
            sg = _sigmoid(zm_ref[:, cs] + mbv[b:b + 1, :])
            du = _mx(dmerged * sg)
            du_ref[:, cs] = du
            dzm = dmerged * u * sg * (1.0 - sg)
            dzm_ref[:, cs] = _mx(dzm)
            st_ref[b:b + 1, :] += jnp.sum(dzm, axis=0, keepdims=True)
            dy_ref[:, b * BW:(b + 1) * BW] = _dot_nt(du, wup_ref[b])

    row = lambda w: pl.BlockSpec((ts, w), lambda i: (i, 0))
    return _call(
        body, name="merge_bwd", grid=(S // ts,),
        in_specs=[row(D_MODEL), pl.BlockSpec((ts, 4096), lambda i: (i, 0)), row(BW), row(BW), row(BW), row(BW),
                  pl.BlockSpec((NBR, BW, D_MODEL), lambda i: (0, 0, 0)),
                  pl.BlockSpec((NBR, D_MODEL), lambda i: (0, 0)), pl.BlockSpec((D_MODEL, D_MODEL), lambda i: (0, 0))],
        out_specs=[row(4096), row(4096), row(D_MODEL), row(D_MODEL), pl.BlockSpec((8, D_MODEL), lambda i: (0, 0))],
        out_shape=[jax.ShapeDtypeStruct((S, ZCOLS), MXU_DTYPE), jax.ShapeDtypeStruct((S, 4096), MXU_DTYPE),
                   jax.ShapeDtypeStruct((S, D_MODEL), MXU_DTYPE), jax.ShapeDtypeStruct((S, D_MODEL), F32),
                   jax.ShapeDtypeStruct((8, D_MODEL), F32)],
        sem=("arbitrary",))(dx1, z, *ys, wup, mb, wo)


def _ple_fwd(x1, p, g, wg, wp, g_next):
    S = x1.shape[0]
    ts = _tile(S, ROW_TILE)

    def body(x_ref, p_ref, g_ref, wg_ref, wp_ref, gn_ref, o_ref, h_ref):
        x = x_ref[...]
        hp = _mx(_rms_fwd(x, g_ref[...]))
        gate = _sigmoid(jnp.dot(hp, wg_ref[...], preferred_element_type=F32))
        pp = jnp.dot(_mx(p_ref[...]), wp_ref[...], preferred_element_type=F32)
        x2 = x + gate * pp
        o_ref[...] = x2
        h_ref[...] = _mx(_rms_fwd(x2, gn_ref[...]))

    row = lambda w: pl.BlockSpec((ts, w), lambda i: (i, 0))
    one = pl.BlockSpec((1, D_MODEL), lambda i: (0, 0))
    return _call(
        body, name="ple_fwd", grid=(S // ts,),
        in_specs=[row(D_MODEL), row(PLE), one,
                  pl.BlockSpec((D_MODEL, D_MODEL), lambda i: (0, 0)), pl.BlockSpec((PLE, D_MODEL), lambda i: (0, 0)), one],
        out_specs=[row(D_MODEL), row(D_MODEL)],
        out_shape=[jax.ShapeDtypeStruct((S, D_MODEL), F32), jax.ShapeDtypeStruct((S, D_MODEL), MXU_DTYPE)],
        sem=("parallel",))(x1, p, g, wg, wp, g_next)


def _ple_bwd(x1, p, dx2, g, wg, wp):
    S = x1.shape[0]
    ts = _tile(S, ROW_TILE)

    def body(x_ref, p_ref, dx_ref, g_ref, wg_ref, wp_ref, dx1_ref, hp_ref, dgl_ref, dpp_ref, pb_ref, st_ref):
        @pl.when(pl.program_id(0) == 0)
        def _():
            st_ref[...] = jnp.zeros_like(st_ref)

        x, gv, dx2 = x_ref[...], g_ref[...], dx_ref[...]
        hp = _mx(_rms_fwd(x, gv))
        hp_ref[...] = hp
        gate = _sigmoid(jnp.dot(hp, wg_ref[...], preferred_element_type=F32))
        pb = _mx(p_ref[...])
        pb_ref[...] = pb
        pp = jnp.dot(pb, wp_ref[...], preferred_element_type=F32)
        dgl = _mx(dx2 * pp * gate * (1.0 - gate))
        dgl_ref[...] = dgl
        dpp_ref[...] = _mx(dx2 * gate)
        dhp = _dot_nt(dgl, wg_ref[...])
        dxn, dgr = _rms_bwd(x, gv, dhp)
        dx1_ref[...] = dx2 + dxn
        st_ref[0:1, :] += jnp.sum(dgr, axis=0, keepdims=True)

    row = lambda w: pl.BlockSpec((ts, w), lambda i: (i, 0))
    sq = pl.BlockSpec((D_MODEL, D_MODEL), lambda i: (0, 0))
    return _call(
        body, name="ple_bwd", grid=(S // ts,),
        in_specs=[row(D_MODEL), row(PLE), row(D_MODEL), pl.BlockSpec((1, D_MODEL), lambda i: (0, 0)), sq,
                  pl.BlockSpec((PLE, D_MODEL), lambda i: (0, 0))],
        out_specs=[row(D_MODEL), row(D_MODEL), row(D_MODEL), row(D_MODEL), row(PLE),
                   pl.BlockSpec((8, D_MODEL), lambda i: (0, 0))],
        out_shape=[jax.ShapeDtypeStruct((S, D_MODEL), F32)] + [jax.ShapeDtypeStruct((S, D_MODEL), MXU_DTYPE)] * 3
        + [jax.ShapeDtypeStruct((S, PLE), MXU_DTYPE), jax.ShapeDtypeStruct((8, D_MODEL), F32)],
        sem=("arbitrary",))(x1, p, dx2, g, wg, wp)


def _pad_rows(a, rows=8):
    return jnp.concatenate([a, jnp.zeros((rows - a.shape[0],) + a.shape[1:], a.dtype)], axis=0)


def _pad_lanes(a, lanes=128):
    return jnp.concatenate([a, jnp.zeros(a.shape[:-1] + (lanes - a.shape[-1],), a.dtype)], axis=-1)


def _wz_from_w_in(w):
    zeros = lambda n: jnp.zeros((w.shape[0], n), w.dtype)
    return jnp.concatenate([w[:, _OM:_OEND], w[:, _OA:_OB], w[:, _OB:_OF], w[:, _OC:_OD], w[:, _OD:_OM], zeros(256),
                            w[:, _OF:_OC], zeros(124)], axis=1)


def _w_in_from_wz(g):
    return jnp.concatenate([g[:, ZA:ZB], g[:, ZB:ZC], g[:, ZF:ZF + 4], g[:, ZC:ZD], g[:, ZD:ZD + 768], g[:, ZM:ZA]], axis=1)


_W_IN_GROUPS = [(_OA, _OB, ZA), (_OB, _OF, ZB), (_OF, _OC, ZF), (_OC, _OD, ZC), (_OD, _OM, ZD), (_OM, _OEND, ZM)]


def _wz_from_shards(g):
    n = g.shape[-1]
    pieces, pos = [], 0
    for a, b, zs in sorted(_W_IN_GROUPS, key=lambda grp: grp[2]):
        if zs > pos:
            pieces.append(jnp.zeros((g.shape[1], zs - pos), g.dtype))
        for k in range(4):
            lo, hi = max(a, k * n), min(b, (k + 1) * n)
            if lo < hi:
                pieces.append(g[k][:, lo - k * n:hi - k * n])
        pos = zs + (b - a)
    pieces.append(jnp.zeros((g.shape[1], ZCOLS - pos), g.dtype))
    return jnp.concatenate(pieces, axis=1)


def _w_in_slabs_from_wz(g):
    n = _OEND // 4
    slabs = []
    for k in range(4):
        pieces = []
        for a, b, zs in _W_IN_GROUPS:
            lo, hi = max(a, k * n), min(b, (k + 1) * n)
            if lo < hi:
                pieces.append(g[:, zs + lo - a:zs + hi - a])
        slabs.append(jnp.concatenate(pieces, axis=1))
    return jnp.stack(slabs)


def _local_step(x, p, target, wts, dist=None):
    tmat_np, masks_np = _hgrn_consts()
    tmat = jnp.asarray(tmat_np, MXU_DTYPE)
    masks = jnp.asarray(masks_np, F32)
    lower = _lower_bounds_fwd(wts["lb_logits"])
    saved = []
    h = _norm_fwd(x, wts["norm_mix"][0][None, :])
    for li in range(DEPTH):
        big = dist.weights(li) if dist else {n: wts[n][li] for n in _GATHERED}
        wz = _wz_from_shards(big["w_in"]) if big["w_in"].ndim == 3 else _wz_from_w_in(big["w_in"])
        g_mix = wts["norm_mix"][li][None, :]
        z = _mm_nn(h, wz, F32, name="mm_z", ts=2 * ROW_TILE, tn=1664)
        cw = _pad_rows(jnp.concatenate([big["conv_w"], wts["conv_b"][li][None, :]], axis=0))
        ya = _conv_fwd(z, cw)
        fb = _pad_lanes(wts["fgate_bias"][li][None, :])
        gq, gk = wts["q_norm"][li][None, :], wts["k_norm"][li][None, :]
        qa, ka, va, cc = _attn_prep_fwd(z, fb, gq, gk)
        o, lse, yb, got = _flash_fwd(qa, ka, va, z, comm=dist.fwd_comm(li) if dist else None)
        if dist:
            dist.fwd_done(li, got)
        lb = lower[li][None, :]
        gh = wts["hgrn_norm"][li][None, :]
        o_pre, states, hscores, yc = _hgrn_fwd(z, lb, gh, tmat, masks)
        gv = wts["sgu_norm"][li][None, :]
        ws = wts["spatial_w"][li]
        bs = _pad_lanes(wts["spatial_b"][li].T)
        yd = _sgu_fwd(z, gv, ws, bs)
        ys = (ya, yb, yc, yd)
        x1, merged = _merge_fwd(x, z, ys, big["w_up"], big["merge_b"], big["w_o"])
        g_ple = wts["norm_ple"][li][None, :]
        x2, h_next = _ple_fwd(x1, p[li], g_ple, big["w_ple_gate"], big["w_ple_proj"],
                              wts["norm_mix"][(li + 1) % DEPTH][None, :])
        saved.append(dict(x=x, h=h, z=z, wz=wz, cw=cw, fb=fb, gq=gq, gk=gk, qa=qa, ka=ka, va=va, cc=cc, o=o, lse=lse,
                          lb=lb, gh=gh, o_pre=o_pre, states=states, hscores=hscores, gv=gv, ws=ws, bs=bs, ys=ys, x1=x1, merged=merged,
                          g_mix=g_mix, g_ple=g_ple, big=big))
        x, h = x2, h_next

    loss, dx = _loss_fwd_bwd(x, target)

    names = ["norm_mix", "w_in", "conv_w", "conv_b", "fgate_bias", "q_norm", "k_norm", "lb", "hgrn_norm", "sgu_norm",
             "spatial_w", "spatial_b", "w_up", "merge_b", "w_o", "norm_ple", "w_ple_gate", "w_ple_proj"]
    gl = {n: [None] * DEPTH for n in names}
    for li in reversed(range(DEPTH)):
        s = saved[li]
        z, big = s["z"], s["big"]
        wg, wp = big["w_ple_gate"], big["w_ple_proj"]
        dx1, hp, dgl, dpp, pb, st = _ple_bwd(s["x1"], p[li], dx, s["g_ple"], wg, wp)
        gl["norm_ple"][li] = st[0]
        gl["w_ple_gate"][li] = _mm_tn(hp, dgl, name="mm_dwg")
        gl["w_ple_proj"][li] = _mm_tn(pb, dpp, name="mm_dwp")
        wup, wo = big["w_up"], big["w_o"]
        dz, du, dxb, dy, st = _merge_bwd(dx1, z, s["ys"], wup, big["merge_b"], wo)
        gl["merge_b"][li] = st[0:NBR]
        gl["w_o"][li] = _mm_tn(s["merged"], dxb, name="mm_dwo")
        gl["w_up"][li] = jnp.stack([_mm_tn(s["ys"][b], du, name="mm_dwup", ycol=b, n=D_MODEL) for b in range(NBR)])
        dz, st = _conv_bwd(z, s["cw"], dy, dz)
        gl["conv_w"][li] = st[0:3]
        gl["conv_b"][li] = st[3]
        dgate, qb, doa = _attn_gate_bwd(dy, s["o"], z, s["qa"], s["cc"], s["lse"])
        dq, doa, got = _flash_bwd_dq(qb, s["ka"], s["va"], doa, comm=dist.bwd_comm() if dist else None)
        if dist:
            dist.bwd_done(got)
        dk, dv, dcc = _flash_bwd_dkv(qb, s["ka"], s["va"], doa)
        dz, dzf, st = _attn_prep_bwd(z, s["fb"], s["gq"], s["gk"], dq, dk, dv, dgate, dcc, dz)
        dz = _put_fgate_cols(dzf, dz)
        gl["q_norm"][li] = st[0, 0:DH]
        gl["k_norm"][li] = st[1, 0:DH]
        gl["fgate_bias"][li] = st[2, 0:NH]
        dz, st = _hgrn_bwd(z, s["lb"], s["gh"], tmat, masks, s["o_pre"], s["states"], s["hscores"], dy, dz)
        gl["hgrn_norm"][li] = st[0]
        gl["lb"][li] = st[1]
        dz, dws, dbs, st = _sgu_bwd(z, s["gv"], s["ws"], s["bs"], dy, dz)
        gl["sgu_norm"][li] = st[0]
        gl["spatial_w"][li] = dws
        gl["spatial_b"][li] = dbs[:, 0:NH].T
        dwz = _mm_tn(s["h"], dz, name="mm_dwz", tn=1664)
        gl["w_in"][li] = dwz if dist else _w_in_from_wz(dwz)
        if dist:
            dist.push(li, {n: gl[n][li] for n in _BIG})
        if dist and li == 0:
            dh, got = _mm_nn(dz, s["wz"], F32, name="mm_dh", ts=ROW_TILE // 2, b_transposed=True, comm=dist.bwd_comm())
            dist.bwd_done(got)
        else:
            dh = _mm_nn(dz, s["wz"], F32, name="mm_dh", ts=ROW_TILE // 2, b_transposed=True)
        dx, st = _norm_bwd(s["x"], s["g_mix"], dh, dx1)
        gl["norm_mix"][li] = st[0]

    if dist:
        dist.finish()
    grads = {n: jnp.stack(v) for n, v in gl.items() if not (dist and n in _BIG)}
    grads["lb_logits"] = _lower_bounds_bwd(wts["lb_logits"], grads.pop("lb"))
    return loss, dx, grads


def _my_pos():
    return lax.axis_index("x"), lax.axis_index("y"), lax.axis_index("c")


def _gather_phase(phase, ins, outs, sems):
    if phase == 1:
        return
    send, recv, lsem = sems
    x, y, c = _my_pos()
    me = 2 * x + y
    peers = [(1 - x, y), (x, 1 - y), (1 - x, 1 - y)]
    copies = []
    for t in range(len(ins)):
        copies.append(pltpu.make_async_copy(ins[t], outs[t].at[me], lsem.at[t]))
        for j, (px, py) in enumerate(peers):
            copies.append(pltpu.make_async_remote_copy(
                src_ref=ins[t], dst_ref=outs[t].at[me], send_sem=send.at[t, j], recv_sem=recv.at[t, j],
                device_id=(px, py, c), device_id_type=MESH))
    for cp in copies:
        if phase == 0:
            cp.start()
        else:
            cp.wait()


N_HALVED = 5


def _gather_halves_phase(phase, ins, outs, sems):
    send, recv, lsem = sems
    x, y, c = _my_pos()
    me = 2 * x + y
    sib = (x, y, 1 - c)
    chips = [(1 - x, y), (x, 1 - y), (1 - x, 1 - y)]

    def rc(t, k, src, dst, dev):
        return pltpu.make_async_remote_copy(src_ref=src, dst_ref=dst, send_sem=send.at[t, k], recv_sem=recv.at[t, k],
                                            device_id=dev, device_id_type=MESH)

    for t in range(len(ins)):
        local = pltpu.make_async_copy(ins[t], outs[t].at[me], lsem.at[t])
        if t >= N_HALVED:
            whole = [rc(t, j, ins[t], outs[t].at[me], (px, py, c)) for j, (px, py) in enumerate(chips)]
            for cp in [local] + whole:
                if phase == 0:
                    cp.start()
                elif phase == 2:
                    cp.wait()
            continue
        hr = ins[t].shape[0] // 2
        mine, other = pl.ds(c * hr, hr), pl.ds((1 - c) * hr, hr)
        first = [rc(t, j, ins[t].at[mine], outs[t].at[me, mine], (px, py, c)) for j, (px, py) in enumerate(chips)]
        landed = [outs[t].at[2 * px + py, mine] for px, py in chips]
        passed = [rc(t, 3 + j, slot, slot, sib) for j, slot in enumerate(landed)]
        if phase == 0:
            local.start()
            for cp in first:
                cp.start()
        elif phase == 1:
            for j, slot in enumerate(landed):
                rc(t, j, slot, slot, (x, y, c)).wait_recv()
                passed[j].start()
        else:
            for j, (px, py) in enumerate(chips):
                slot = outs[t].at[2 * px + py, other]
                rc(t, 3 + j, slot, slot, (x, y, c)).wait_recv()
            for cp in first + passed:
                cp.wait_send()
            local.wait()


def _exchange_phase(phase, ins, outs, sems):
    send, recv, lsem = sems
    x, y, c = _my_pos()
    me = 2 * x + y
    sib = (x, y, 1 - c)
    chips = [(1 - x, y), (x, 1 - y), (1 - x, 1 - y)]

    def rc(t, k, src, dst, dev):
        return pltpu.make_async_remote_copy(src_ref=src, dst_ref=dst, send_sem=send.at[t, k], recv_sem=recv.at[t, k],
                                            device_id=dev, device_id_type=MESH)

    for t in range(len(ins)):
        local = pltpu.make_async_copy(ins[t].at[me], outs[t].at[2 * me + c], lsem.at[t])
        first = [rc(t, 0, ins[t].at[me], outs[t].at[2 * me + c], sib)]
        first += [rc(t, 1 + j, ins[t].at[2 * px + py], outs[t].at[2 * me + c], (px, py, c))
                  for j, (px, py) in enumerate(chips)]
        landed = [outs[t].at[2 * (2 * px + py) + c] for px, py in chips]
        passed = [rc(t, 4 + j, slot, slot, sib) for j, slot in enumerate(landed)]
        if phase == 0:
            local.start()
            for cp in first:
                cp.start()
        elif phase == 1:
            for j, slot in enumerate(landed):
                rc(t, 1 + j, slot, slot, (x, y, c)).wait_recv()
                passed[j].start()
        else:
            s0 = outs[t].at[2 * me + (1 - c)]
            rc(t, 0, s0, s0, (x, y, c)).wait_recv()
            for j, (px, py) in enumerate(chips):
                slot = outs[t].at[2 * (2 * px + py) + (1 - c)]
                rc(t, 4 + j, slot, slot, (x, y, c)).wait_recv()
            for cp in first + passed:
                cp.wait_send()
            local.wait()


_COMM_PHASES = {"gather": _gather_phase, "gather_halves": _gather_halves_phase, "exchange": _exchange_phase}
_gathered_shapes = lambda arrays: [jax.ShapeDtypeStruct((4,) + a.shape, a.dtype) for a in arrays]
_COMM_OUT = {"gather": _gathered_shapes, "gather_halves": _gathered_shapes,
             "exchange": lambda arrays: [jax.ShapeDtypeStruct((8,) + a.shape[1:], a.dtype) for a in arrays]}


def _comm_sems(kind, nc):
    k = {"gather": 3, "gather_halves": 6, "exchange": 7}[kind]
    return [pltpu.SemaphoreType.DMA((nc, k)), pltpu.SemaphoreType.DMA((nc, k)), pltpu.SemaphoreType.DMA((nc,))]


def _comm_alone(kind, arrays, name):
    nc = len(arrays)

    def body(*refs):
        for phase in range(3):
            _COMM_PHASES[kind](phase, refs[:nc], refs[nc:2 * nc], refs[2 * nc:])

    hbm = pl.BlockSpec(memory_space=pl.ANY)
    return pl.pallas_call(
        functools.partial(body), name=name, in_specs=[hbm] * nc, out_specs=[hbm] * nc,
        out_shape=_COMM_OUT[kind](arrays), scratch_shapes=_comm_sems(kind, nc))(*arrays)


def _allreduce_small(v):
    R = v.shape[0]

    def body(v_ref, o_ref, pair, chips, send, recv):
        x, y, c = _my_pos()
        me = 2 * x + y
        pair[c] = v_ref[...]
        swap = pltpu.make_async_remote_copy(src_ref=v_ref, dst_ref=pair.at[c], send_sem=send.at[0], recv_sem=recv.at[0],
                                            device_id=(x, y, 1 - c), device_id_type=MESH)
        swap.start()
        swap.wait()
        chips[me] = pair[0] + pair[1]
        copies = [pltpu.make_async_remote_copy(src_ref=chips.at[me], dst_ref=chips.at[me], send_sem=send.at[1 + j],
                                               recv_sem=recv.at[1 + j], device_id=(px, py, c), device_id_type=MESH)
                  for j, (px, py) in enumerate([(1 - x, y), (x, 1 - y), (1 - x, 1 - y)])]
        for cp in copies:
            cp.start()
        for cp in copies:
            cp.wait()
        o_ref[...] = (chips[0] + chips[1]) + (chips[2] + chips[3])

    vm = pl.BlockSpec(memory_space=pltpu.VMEM)
    return pl.pallas_call(
        functools.partial(body), name="allreduce_small", in_specs=[vm], out_specs=vm,
        out_shape=jax.ShapeDtypeStruct((R, 128), F32),
        scratch_shapes=[pltpu.VMEM((2, R, 128), F32), pltpu.VMEM((4, R, 128), F32),
                        pltpu.SemaphoreType.DMA((4,)), pltpu.SemaphoreType.DMA((4,))],
        compiler_params=pltpu.CompilerParams(vmem_limit_bytes=VMEM_LIMIT))(v)


def _adamw(w, m, v, parts, nparts):
    A, R, C = w.shape
    per_layer = isinstance(parts, (list, tuple))
    tr = _tile(R, 64) if per_layer else (R if R <= 128 else _tile(R, 128))
    nparr = len(parts) if per_layer else 1

    def body(*refs):
        w_ref, m_ref, v_ref = refs[:3]
        p_refs = refs[3:3 + nparr]
        g_ref, d_ref, nm_ref, nv_ref = refs[3 + nparr:]

        def update(read):
            g = read(0).astype(F32)
            for k in range(1, nparts):
                g = g + read(k).astype(F32)
            mm = ADAM_B1 * m_ref[0] + (1.0 - ADAM_B1) * g
            vv = ADAM_B2 * v_ref[0] + (1.0 - ADAM_B2) * jnp.square(g)
            m_hat = mm / (1.0 - ADAM_B1 ** ADAM_STEP)
            v_hat = vv / (1.0 - ADAM_B2 ** ADAM_STEP)
            g_ref[0] = g
            d_ref[0] = -ADAM_LR * (m_hat / (jnp.sqrt(v_hat) + ADAM_EPS) + ADAM_WD * w_ref[0])
            nm_ref[0] = mm
            nv_ref[0] = vv

        if not per_layer:
            update(lambda k: p_refs[0][k, 0])
        else:
            for a in range(A):
                @pl.when(pl.program_id(0) == a)
                def _(a=a):
                    update(lambda k: p_refs[a][k])

    blk = pl.BlockSpec((1, tr, C), lambda a, r: (a, r, 0))
    if per_layer:
        pspecs = [pl.BlockSpec((nparts, tr, C), functools.partial(lambda a, r, l: (0, jnp.where(a == l, r, 0), 0), l=l))
                  for l in range(A)]
        pargs = list(parts)
    else:
        pspecs = [pl.BlockSpec((nparts, 1, tr, C), lambda a, r: (0, a, r, 0))]
        pargs = [parts]
    return _call(
        body, name="adamw", grid=(A, R // tr), in_specs=[blk, blk, blk] + pspecs,
        out_specs=[blk] * 4, out_shape=[jax.ShapeDtypeStruct((A, R, C), F32)] * 4,
        sem=("arbitrary", "arbitrary"))(w, m, v, *pargs)


def _rows128(a):
    width = a.shape[-1]
    a2 = a.reshape(-1, width)
    k = -(-width // 128)
    if width % 128:
        a2 = jnp.pad(a2, ((0, 0), (0, k * 128 - width)))
    return a2.reshape(-1, 128)


def _from_rows128(r, shape):
    width = shape[-1]
    k = -(-width // 128)
    return r.reshape(-1, k * 128)[:, :width].reshape(shape)


def _as3d(a):
    if a.ndim == 2:
        return a[None]
    if a.ndim == 3:
        return a
    return a.reshape((-1,) + a.shape[-2:])


_WEIGHTS = ["norm_mix", "w_in", "conv_w", "conv_b", "fgate_bias", "q_norm", "k_norm", "lb_logits", "hgrn_norm", "sgu_norm",
            "spatial_w", "spatial_b", "w_up", "merge_b", "w_o", "norm_ple", "w_ple_gate", "w_ple_proj"]
_BIG = ["w_in", "w_up", "w_o", "w_ple_gate", "w_ple_proj"]
_GATHERED = _BIG + ["conv_w", "merge_b"]
_SHARD_AXIS = {"w_in": 2, "w_up": 3, "w_o": 1, "w_ple_gate": 1, "w_ple_proj": 2, "conv_w": 2, "merge_b": 2}
_SMALL = [n for n in _WEIGHTS if n not in _BIG]


class _Dist:
    def __init__(self, w):
        self.w = w
        self.full = {0: self._unpack(_comm_alone("gather_halves", self._shards(0), "gather_weights"))}
        self.contrib = {}
        self.pending = None

    def _shards(self, l):
        return [_mx(self.w[n][l]) for n in _BIG] + [self.w["conv_w"][l], self.w["merge_b"][l]]

    def _unpack(self, gathered):
        return {n: g if n == "w_in" else jnp.concatenate([g[k] for k in range(4)], axis=_SHARD_AXIS[n] - 1)
                for n, g in zip(_GATHERED, gathered)}

    def weights(self, l):
        return self.full[l]

    def fwd_comm(self, l):
        return ("gather", self._shards(l + 1)) if l + 1 < DEPTH else None

    def fwd_done(self, l, got):
        if got:
            self.full[l + 1] = self._unpack(got)

    def push(self, l, grads):
        self.pending = (l, [(_w_in_slabs_from_wz(grads[n]) if n == "w_in" else
                             jnp.stack(jnp.split(grads[n], 4, axis=_SHARD_AXIS[n] - 1))).astype(GRAD_WIRE_DTYPE)
                            for n in _BIG])

    def bwd_comm(self):
        return ("exchange", self.pending[1]) if self.pending else None

    def bwd_done(self, got):
        if got:
            self.contrib[self.pending[0]] = got
            self.pending = None

    def finish(self):
        if self.pending:
            self.contrib[self.pending[0]] = _comm_alone("exchange", self.pending[1], "exchange_grads")
            self.pending = None


def kernel(x, p, norm_mix, w_in, conv_w, conv_b, fgate_bias, q_norm, k_norm, lb_logits, hgrn_norm, sgu_norm, spatial_w, spatial_b, w_up, merge_b, w_o, norm_ple, w_ple_gate, w_ple_proj, loss_target, m_norm_mix, m_w_in, m_conv_w, m_conv_b, m_fgate_bias, m_q_norm, m_k_norm, m_lb_logits, m_hgrn_norm, m_sgu_norm, m_spatial_w, m_spatial_b, m_w_up, m_merge_b, m_w_o, m_norm_ple, m_w_ple_gate, m_w_ple_proj, v_norm_mix, v_w_in, v_conv_w, v_conv_b, v_fgate_bias, v_q_norm, v_k_norm, v_lb_logits, v_hgrn_norm, v_sgu_norm, v_spatial_w, v_spatial_b, v_w_up, v_merge_b, v_w_o, v_norm_ple, v_w_ple_gate, v_w_ple_proj):
    loc = dict(locals())
    w = {n: loc[n] for n in _WEIGHTS}
    m = {n: loc["m_" + n] for n in _WEIGHTS}
    v = {n: loc["v_" + n] for n in _WEIGHTS}
    chip = 2 * lax.axis_index("x") + lax.axis_index("y")

    dist = _Dist(w)
    loss_blk, dx, grads = _local_step(x[0], p[:, 0], loss_target[0], w, dist)
    loss = lax.psum(loss_blk[0, 0], ("x", "y", "c"))

    blocks = [_rows128(grads[n]) for n in _SMALL]
    nrows = sum(b.shape[0] for b in blocks)
    packed = jnp.concatenate(blocks + [jnp.zeros(((-nrows) % 8, 128), F32)], axis=0)
    red = _allreduce_small(packed)
    small, off = {}, 0
    for n, b in zip(_SMALL, blocks):
        small[n] = _from_rows128(red[off:off + b.shape[0]], grads[n].shape)
        off += b.shape[0]
    for n in ("conv_w", "merge_b"):
        ax = _SHARD_AXIS[n]
        width = small[n].shape[ax] // 4
        small[n] = lax.dynamic_slice_in_dim(small[n], chip * width, width, axis=ax)

    out_g, out_d, out_m, out_v = {}, {}, {}, {}
    for n in _WEIGHTS:
        shp = w[n].shape
        if n in _BIG:
            w3, m3, v3 = (a.reshape((DEPTH, -1, shp[-1])) for a in (w[n], m[n], v[n]))
            parts = [dist.contrib[l][_BIG.index(n)].reshape((8,) + w3.shape[1:]) for l in range(DEPTH)]
            g, d, nm, nv = _adamw(w3, m3, v3, parts, 8)
        else:
            g, d, nm, nv = _adamw(_as3d(w[n]), _as3d(m[n]), _as3d(v[n]), _as3d(small[n])[None], 1)
        out_g[n], out_d[n], out_m[n], out_v[n] = (a.reshape(shp) for a in (g, d, nm, nv))

    return (loss, dx[None], *[out_g[n] for n in _WEIGHTS], *[out_d[n] for n in _WEIGHTS],
            *[out_m[n] for n in _WEIGHTS], *[out_v[n] for n in _WEIGHTS])
```

```python
import functools

import numpy as np
import jax
import jax.numpy as jnp
from jax import lax
from jax.experimental import pallas as pl
from jax.experimental.pallas import tpu as pltpu

F32 = jnp.float32
MXU_DTYPE = jnp.bfloat16

D_MODEL = 1024
BW = 256
NH = 4
DH = 64
DEPTH = 4
NBR = 4
PLE = 256
CHUNK = 64
SCHUNK = 128
EPS = 1e-6
MASK_VALUE = -1e30
NLEV = 6

ADAM_LR, ADAM_B1, ADAM_B2, ADAM_EPS, ADAM_WD, ADAM_STEP = 0.001, 0.9, 0.999, 1e-08, 0.01, 10

ZM, ZA, ZB, ZC, ZD, ZF = 0, 4096, 5120, 6144, 7168, 8192
ZCOLS = 8320
ZB_GATE = ZB + 3 * BW
_OA, _OB, _OF, _OC, _OD, _OM, _OEND = 0, 1024, 2048, 2052, 3076, 3844, 7940

VMEM_LIMIT = 56 * 1024 * 1024
ROW_TILE = 512
FLASH_TILE = 1024
GRAD_WIRE_DTYPE = jnp.bfloat16
MERGE_TILE = 512
MESH = pl.DeviceIdType.MESH
_ANY = pl.BlockSpec(memory_space=pl.ANY)


def _tile(n, pref):
    t = min(n, pref)
    assert n % t == 0, (n, t)
    return t


def _call(body, *, name, grid, in_specs, out_specs, out_shape, scratch=(), sem=None, aliases=None):
    return pl.pallas_call(
        functools.partial(body), name=name, grid=grid, in_specs=in_specs, out_specs=out_specs,
        out_shape=out_shape, scratch_shapes=list(scratch), input_output_aliases=aliases or {},
        compiler_params=pltpu.CompilerParams(dimension_semantics=sem, vmem_limit_bytes=VMEM_LIMIT))


def _mx(x):
    return x.astype(MXU_DTYPE)


def _dot(a, b):
    return jnp.dot(_mx(a), _mx(b), preferred_element_type=F32)


def _dot_nt(a, b):
    return lax.dot_general(_mx(a), _mx(b), (((1,), (1,)), ((), ())), preferred_element_type=F32)


def _dot_tn(a, b):
    return lax.dot_general(_mx(a), _mx(b), (((0,), (0,)), ((), ())), preferred_element_type=F32)


def _top16(x):
    u = lax.bitcast_convert_type(x, jnp.uint32) & jnp.uint32(0xFFFF0000)
    return lax.bitcast_convert_type(u, F32)


def _split3(x):
    hi = _top16(x)
    r1 = x - hi
    mid = _top16(r1)
    return _mx(hi), _mx(mid), _mx(r1 - mid)


def _dot_exact_lhs01(t, x):
    hi, mid, lo = _split3(x)
    t = _mx(t)
    return (jnp.dot(t, hi, preferred_element_type=F32) + jnp.dot(t, mid, preferred_element_type=F32)
            + jnp.dot(t, lo, preferred_element_type=F32))


def _sigmoid(x):
    return jax.nn.sigmoid(x)


def _silu(x):
    return x * _sigmoid(x)


def _dsilu(x):
    s = _sigmoid(x)
    return s * (1.0 + x * (1.0 - s))


def _log_sigmoid(x):
    return jnp.minimum(x, 0.0) - jnp.log(1.0 + jnp.exp(-jnp.abs(x)))


def _rms_fwd(x, g):
    r = lax.rsqrt(jnp.mean(x * x, axis=-1, keepdims=True) + EPS)
    return x * r * g


def _rms_bwd(x, g, dy):
    r = lax.rsqrt(jnp.mean(x * x, axis=-1, keepdims=True) + EPS)
    n = x * r
    t = dy * g
    dx = r * (t - n * jnp.mean(t * n, axis=-1, keepdims=True))
    return dx, dy * n


def _tri(n, upper=False):
    r = lax.broadcasted_iota(jnp.int32, (n, n), 0)
    c = lax.broadcasted_iota(jnp.int32, (n, n), 1)
    return jnp.where((c >= r) if upper else (r >= c), 1.0, 0.0).astype(F32)


def _hs(h):
    return slice(h * DH, (h + 1) * DH)


def _mm_nn(a, b, out_dtype, *, name, ts=None, tn=None, tk=None, b_transposed=False, comm=None):
    S, K = a.shape
    N = b.shape[0] if b_transposed else b.shape[1]
    ts = _tile(S, ts or ROW_TILE)
    tn = _tile(N, tn or N)
    tk = _tile(K, tk or K)
    nk = K // tk
    b_spec = (pl.BlockSpec((tn, tk), lambda j, i, k: (j, k)) if b_transposed
              else pl.BlockSpec((tk, tn), lambda j, i, k: (k, j)))

    def body(a_ref, b_ref, o_ref, acc_ref):
        k = pl.program_id(2)
        part = _dot_nt(a_ref[...], b_ref[...]) if b_transposed else jnp.dot(a_ref[...], b_ref[...],
                                                                          preferred_element_type=F32)
        if nk == 1:
            o_ref[...] = part.astype(o_ref.dtype)
        else:
            @pl.when(k == 0)
            def _():
                acc_ref[...] = part

            @pl.when(k > 0)
            def _():
                acc_ref[...] += part

            @pl.when(k == nk - 1)
            def _():
                o_ref[...] = acc_ref[...].astype(o_ref.dtype)

    grid = (N // tn, S // ts, nk)

    def when():
        at = [pl.program_id(d) for d in range(3)]
        first = (at[0] == 0) & (at[1] == 0) & (at[2] == 0)
        last = (at[0] == grid[0] - 1) & (at[1] == grid[1] - 1) & (at[2] == grid[2] - 1)
        return first, last, last

    xin, xout, xshape, xsem = _comm_specs(comm)
    res = _call(
        _fuse_comm(body, 2, 1, comm, when), name=name, grid=grid,
        in_specs=[pl.BlockSpec((ts, tk), lambda j, i, k: (i, k)), b_spec] + xin,
        out_specs=[pl.BlockSpec((ts, tn), lambda j, i, k: (i, j))] + xout,
        out_shape=[jax.ShapeDtypeStruct((S, N), out_dtype)] + xshape,
        scratch=[pltpu.VMEM((ts, tn) if nk > 1 else (8, 128), F32)] + xsem,
        sem=("arbitrary", "arbitrary", "arbitrary") if comm else ("parallel", "parallel", "arbitrary"),
        )(a, b, *(comm[1] if comm else []))
    return (res[0], list(res[1:])) if comm else res[0]


def _mm_tn(x, y, *, name, tn=None, ycol=0, n=None):
    S, M = x.shape
    n = n or y.shape[1]
    ts = _tile(S, 4 * ROW_TILE)
    tn = _tile(n, tn or n)
    nj = n // tn

    def body(x_ref, y_ref, o_ref):
        @pl.when(pl.program_id(1) == 0)
        def _():
            o_ref[...] = jnp.zeros_like(o_ref)

        o_ref[...] += lax.dot_general(x_ref[...], y_ref[...], (((0,), (0,)), ((), ())), preferred_element_type=F32)

    return _call(
        body, name=name, grid=(nj, S // ts),
        in_specs=[pl.BlockSpec((ts, M), lambda j, s: (s, 0)), pl.BlockSpec((ts, tn), lambda j, s: (s, ycol * nj + j))],
        out_specs=pl.BlockSpec((M, tn), lambda j, s: (0, j)),
        out_shape=jax.ShapeDtypeStruct((M, n), F32),
        sem=("parallel", "arbitrary"))(x, y)


def _norm_fwd(x, g):
    S = x.shape[0]
    ts = _tile(S, ROW_TILE)

    def body(x_ref, g_ref, h_ref):
        h_ref[...] = _mx(_rms_fwd(x_ref[...], g_ref[...]))

    return _call(
        body, name="norm_fwd", grid=(S // ts,),
        in_specs=[pl.BlockSpec((ts, D_MODEL), lambda i: (i, 0)), pl.BlockSpec((1, D_MODEL), lambda i: (0, 0))],
        out_specs=pl.BlockSpec((ts, D_MODEL), lambda i: (i, 0)),
        out_shape=jax.ShapeDtypeStruct((S, D_MODEL), MXU_DTYPE), sem=("parallel",))(x, g)


def _norm_bwd(x, g, dh, dres):
    S = x.shape[0]
    ts = _tile(S, ROW_TILE)

    def body(x_ref, g_ref, dh_ref, dr_ref, dx_ref, st_ref):
        @pl.when(pl.program_id(0) == 0)
        def _():
            st_ref[...] = jnp.zeros_like(st_ref)

        dx, dgr = _rms_bwd(x_ref[...], g_ref[...], dh_ref[...])
        dx_ref[...] = dr_ref[...] + dx
        st_ref[0:1, :] += jnp.sum(dgr, axis=0, keepdims=True)

    row = pl.BlockSpec((ts, D_MODEL), lambda i: (i, 0))
    return _call(
        body, name="norm_bwd", grid=(S // ts,),
        in_specs=[row, pl.BlockSpec((1, D_MODEL), lambda i: (0, 0)), row, row],
        out_specs=[row, pl.BlockSpec((8, D_MODEL), lambda i: (0, 0))],
        out_shape=[jax.ShapeDtypeStruct((S, D_MODEL), F32), jax.ShapeDtypeStruct((8, D_MODEL), F32)],
        sem=("arbitrary",))(x, g, dh, dres)


def _loss_fwd_bwd(y, target):
    S = y.shape[0]
    ts = _tile(S, ROW_TILE)

    def body(y_ref, t_ref, l_ref, dy_ref):
        @pl.when(pl.program_id(0) == 0)
        def _():
            l_ref[...] = jnp.zeros_like(l_ref)

        err = y_ref[...] - t_ref[...]
        dy_ref[...] = err * (1.0 / D_MODEL)
        rowloss = jnp.mean(err * err, axis=-1, keepdims=True)
        l_ref[...] += 0.5 * jnp.sum(rowloss, axis=0, keepdims=True)

    row = pl.BlockSpec((ts, D_MODEL), lambda i: (i, 0))
    return _call(
        body, name="loss", grid=(S // ts,), in_specs=[row, row],
        out_specs=[pl.BlockSpec((8, 128), lambda i: (0, 0)), row],
        out_shape=[jax.ShapeDtypeStruct((8, 128), F32), jax.ShapeDtypeStruct((S, D_MODEL), F32)],
        sem=("arbitrary",))(y, target)


def _shift_down(x, k, halo, rows):
    y = pltpu.roll(x, k, 0)
    for j in range(k):
        y = jnp.where(rows == j, halo[8 - k + j:8 - k + j + 1, :], y)
    return y


def _shift_up(x, k, halo, rows, n):
    y = pltpu.roll(x, n - k, 0)
    for j in range(k):
        y = jnp.where(rows == n - k + j, halo[j:j + 1, :], y)
    return y


def _conv_parts(za, zh, cw, first, rows):
    ax, ab, ac, ag = za[:, 0:BW], za[:, BW:2 * BW], za[:, 2 * BW:3 * BW], za[:, 3 * BW:4 * BW]
    zz = ac * ax
    hz = jnp.where(first, 0.0, zh[:, 2 * BW:3 * BW] * zh[:, 0:BW])
    zz1 = _shift_down(zz, 1, hz, rows)
    zz2 = _shift_down(zz, 2, hz, rows)
    conv = zz2 * cw[0:1, :] + zz1 * cw[1:2, :] + zz * cw[2:3, :] + cw[3:4, :]
    return ax, ab, ac, ag, zz, zz1, zz2, conv


def _conv_fwd(z, cw):
    S = z.shape[0]
    ts = _tile(S, ROW_TILE)
    hb = ts // 8

    def body(za_ref, zh_ref, cw_ref, y_ref):
        i = pl.program_id(0)
        rows = lax.broadcasted_iota(jnp.int32, (ts, BW), 0)
        ax, ab, ac, ag, zz, zz1, zz2, conv = _conv_parts(za_ref[...], zh_ref[...], cw_ref[...], i == 0, rows)
        y_ref[...] = _mx(ab * conv * _silu(ag))

    return _call(
        body, name="conv_fwd", grid=(S // ts,),
        in_specs=[pl.BlockSpec((ts, 1024), lambda i: (i, ZA // 1024)),
                  pl.BlockSpec((8, 1024), lambda i: (jnp.maximum(i * hb - 1, 0), ZA // 1024)),
                  pl.BlockSpec((8, BW), lambda i: (0, 0))],
        out_specs=pl.BlockSpec((ts, BW), lambda i: (i, 0)),
        out_shape=jax.ShapeDtypeStruct((S, BW), MXU_DTYPE), sem=("parallel",))(z, z, cw)


def _conv_bwd(z, cw, dy, dz):
    S = z.shape[0]
    ts = _tile(S, ROW_TILE)
    hb = ts // 8
    nt = S // ts

    def body(za_ref, zh_ref, zn_ref, cw_ref, dy_ref, dyn_ref, dz_in, dz_ref, st_ref):
        i = pl.program_id(0)

        @pl.when(i == 0)
        def _():
            st_ref[...] = jnp.zeros_like(st_ref)

        cw = cw_ref[...]
        rows = lax.broadcasted_iota(jnp.int32, (ts, BW), 0)
        ax, ab, ac, ag, zz, zz1, zz2, conv = _conv_parts(za_ref[...], zh_ref[...], cw, i == 0, rows)
        dy = dy_ref[...]
        sg = _silu(ag)
        dc = dy * ab * sg
        zn = zn_ref[...]
        dcn = jnp.where(i == nt - 1, 0.0, dyn_ref[...] * zn[:, BW:2 * BW] * _silu(zn[:, 3 * BW:4 * BW]))
        dc1 = _shift_up(dc, 1, dcn, rows, ts)
        dc2 = _shift_up(dc, 2, dcn, rows, ts)
        dzz = dc * cw[2:3, :] + dc1 * cw[1:2, :] + dc2 * cw[0:1, :]
        dz_ref[:, 0:BW] = _mx(dzz * ac)
        dz_ref[:, BW:2 * BW] = _mx(dy * conv * sg)
        dz_ref[:, 2 * BW:3 * BW] = _mx(dzz * ax)
        dz_ref[:, 3 * BW:4 * BW] = _mx(dy * ab * conv * _dsilu(ag))
        st_ref[0:1, :] += jnp.sum(dc * zz2, axis=0, keepdims=True)
        st_ref[1:2, :] += jnp.sum(dc * zz1, axis=0, keepdims=True)
        st_ref[2:3, :] += jnp.sum(dc * zz, axis=0, keepdims=True)
        st_ref[3:4, :] += jnp.sum(dc, axis=0, keepdims=True)

    return _call(
        body, name="conv_bwd", grid=(nt,),
        in_specs=[pl.BlockSpec((ts, 1024), lambda i: (i, ZA // 1024)),
                  pl.BlockSpec((8, 1024), lambda i: (jnp.maximum(i * hb - 1, 0), ZA // 1024)),
                  pl.BlockSpec((8, 1024), lambda i: (jnp.minimum((i + 1) * hb, S // 8 - 1), ZA // 1024)),
                  pl.BlockSpec((8, BW), lambda i: (0, 0)),
                  pl.BlockSpec((ts, BW), lambda i: (i, 0)),
                  pl.BlockSpec((8, BW), lambda i: (jnp.minimum((i + 1) * hb, S // 8 - 1), 0)), _ANY],
        out_specs=[pl.BlockSpec((ts, 1024), lambda i: (i, ZA // 1024)), pl.BlockSpec((8, BW), lambda i: (0, 0))],
        out_shape=[jax.ShapeDtypeStruct((S, ZCOLS), MXU_DTYPE), jax.ShapeDtypeStruct((8, BW), F32)],
        sem=("arbitrary",), aliases={6: 0})(z, z, z, cw, dy, dy, dz)


AW = 128
_AUG = DH


def _split3_f32(x):
    hi = _top16(x)
    r1 = x - hi
    mid = _top16(r1)
    return hi, mid, r1 - mid


def _aug_lanes(n, cols):
    lane = lax.broadcasted_iota(jnp.int32, (n, DH), 1)
    out = jnp.zeros((n, DH), F32)
    for e, c in enumerate(cols):
        out = jnp.where(lane == e, c, out)
    return out


def _put_aug(ref, h, col, parts=None, const=None):
    base = h * AW + _AUG + col
    if parts is None:
        ref[:, base:base + 3] = jnp.full((ref.shape[0], 3), const, ref.dtype)
    else:
        for e, part in enumerate(parts):
            ref[:, base + e:base + e + 1] = part.astype(ref.dtype)


def _attn_prep_fwd(z, fb, gq, gk):
    S = z.shape[0]
    ts = _tile(S, ROW_TILE)

    def body(zb_ref, zf_ref, fb_ref, gq_ref, gk_ref, qa_ref, ka_ref, va_ref, cc_ref, carry):
        @pl.when(pl.program_id(0) == 0)
        def _():
            carry[...] = jnp.zeros_like(carry)

        zb = zb_ref[...]
        gqv, gkv = gq_ref[...], gk_ref[...]
        lf = _log_sigmoid(zf_ref[...] + fb_ref[...])
        cum = _dot_exact_lhs01(_tri(ts), lf) + carry[...]
        carry[...] = cum[ts - 1:ts, :]
        cc_ref[...] = cum
        pieces = _split3_f32(cum)
        for h in range(NH):
            ph = [pc[:, h:h + 1] for pc in pieces]
            qh = _rms_fwd(zb[:, _hs(h)], gqv) * (DH ** -0.5)
            kh = _rms_fwd(zb[:, BW + h * DH:BW + (h + 1) * DH], gkv)
            vh = zb[:, 2 * BW + h * DH:2 * BW + (h + 1) * DH]
            qa_ref[:, _ah(h)] = _mx(jnp.concatenate([qh, _aug_lanes(ts, ph + [1.0] * 3)], axis=1))
            ka_ref[:, _ah(h)] = _mx(jnp.concatenate([kh, _aug_lanes(ts, [1.0] * 3 + [-x for x in ph])], axis=1))
            va_ref[:, _ah(h)] = _mx(jnp.concatenate([vh, _aug_lanes(ts, [-1.0] * 3)], axis=1))

    row = lambda w: pl.BlockSpec((ts, w), lambda i: (i, 0))
    return _call(
        body, name="attn_prep_fwd", grid=(S // ts,),
        in_specs=[pl.BlockSpec((ts, 1024), lambda i: (i, ZB // 1024)), pl.BlockSpec((ts, 128), lambda i: (i, ZF // 128)),
                  pl.BlockSpec((1, 128), lambda i: (0, 0)), pl.BlockSpec((1, DH), lambda i: (0, 0)),
                  pl.BlockSpec((1, DH), lambda i: (0, 0))],
        out_specs=[row(NH * AW), row(NH * AW), row(NH * AW), row(128)],
        out_shape=[jax.ShapeDtypeStruct((S, NH * AW), MXU_DTYPE)] * 3 + [jax.ShapeDtypeStruct((S, 128), F32)],
        scratch=[pltpu.VMEM((1, 128), F32)], sem=("arbitrary",))(z, z, fb, gq, gk)


ROW_CHUNK = 256


def _square_steps(n):
    def when():
        i, j = pl.program_id(0), pl.program_id(1)
        return (i == 0) & (j == 0), (i == n - 1) & (j == 0), (i == n - 1) & (j == n - 1)
    return when


def _fuse_comm(core, n_in, n_out, comm, when):
    nc = 0 if comm is None else len(comm[1])

    def body(*refs):
        cin, xin = refs[:n_in], refs[n_in:n_in + nc]
        a = n_in + nc
        cout, xout = refs[a:a + n_out], refs[a + n_out:a + n_out + nc]
        rest = refs[a + n_out + nc:]
        if nc == 0:
            core(*cin, *cout, *rest)
            return
        cscr, sems = rest[:-3], rest[-3:]
        first, middle, last = when()
        phase = _COMM_PHASES[comm[0]]

        @pl.when(first)
        def _():
            phase(0, xin, xout, sems)

        core(*cin, *cout, *cscr)

        @pl.when(middle)
        def _():
            phase(1, xin, xout, sems)

        @pl.when(last)
        def _():
            phase(2, xin, xout, sems)

    return body


def _comm_specs(comm):
    if comm is None:
        return [], [], [], []
    hbm = pl.BlockSpec(memory_space=pl.ANY)
    nc = len(comm[1])
    return [hbm] * nc, [hbm] * nc, _COMM_OUT[comm[0]](comm[1]), _comm_sems(comm[0], nc)


def _ah(h):
    return slice(h * AW, (h + 1) * AW)


def _ahd(h):
    return slice(h * AW, h * AW + DH)


def _causal(shape, row0, transposed=False):
    r = row0 + lax.broadcasted_iota(jnp.int32, shape, 0)
    c = lax.broadcasted_iota(jnp.int32, shape, 1)
    return (r <= c) if transposed else (r >= c)


def _flash_fwd(qa, ka, va, z, comm=None):
    S = qa.shape[0]
    t = _tile(S, FLASH_TILE)
    n = S // t
    rch = _tile(t, ROW_CHUNK)

    def core(q_ref, k_ref, v_ref, zb_ref, o_ref, lse_ref, y_ref, m_sc, l_sc, acc):
        i, j = pl.program_id(0), pl.program_id(1)

        @pl.when(j == 0)
        def _():
            m_sc[...] = jnp.full_like(m_sc, MASK_VALUE)
            l_sc[...] = jnp.zeros_like(l_sc)
            acc[...] = jnp.zeros_like(acc)

        def block(masked):
            for h in range(NH):
                for rc in range(t // rch):
                    rows = slice(rc * rch, (rc + 1) * rch)
                    s = _dot_nt(q_ref[rows, _ah(h)], k_ref[:, _ah(h)])
                    if masked:
                        s = jnp.where(_causal(s.shape, rc * rch), s, MASK_VALUE)
                    m_old = m_sc[h, rows, :]
                    m_new = jnp.maximum(m_old, jnp.max(s, axis=-1, keepdims=True))
                    p = jnp.exp(s - m_new)
                    alpha = jnp.exp(m_old - m_new)
                    l_sc[h, rows, :] = alpha * l_sc[h, rows, :] + jnp.sum(p, axis=-1, keepdims=True)
                    acc[rows, _hs(h)] = alpha * acc[rows, _hs(h)] + _dot(p, v_ref[:, _ahd(h)])
                    m_sc[h, rows, :] = m_new

        @pl.when(j < i)
        def _():
            block(False)

        @pl.when(j == i)
        def _():
            block(True)
            lse_ref[...] = jnp.zeros_like(lse_ref)
            for h in range(NH):
                o_ref[:, _hs(h)] = acc[:, _hs(h)] / l_sc[h]
                lse_ref[:, h:h + 1] = m_sc[h] + jnp.log(l_sc[h])
            y_ref[...] = _mx(o_ref[...] * _silu(zb_ref[...]))

    qspec = lambda w: pl.BlockSpec((t, w), lambda i, j: (i, 0))
    kspec = lambda w: pl.BlockSpec((t, w), lambda i, j: (jnp.minimum(j, i), 0))
    xin, xout, xshape, xsem = _comm_specs(comm)
    res = _call(
        _fuse_comm(core, 4, 3, comm, _square_steps(n)), name="flash_fwd", grid=(n, n),
        in_specs=[qspec(NH * AW), kspec(NH * AW), kspec(NH * AW),
                  pl.BlockSpec((t, BW), lambda i, j: (i, ZB_GATE // BW))] + xin,
        out_specs=[qspec(BW), qspec(128), qspec(BW)] + xout,
        out_shape=[jax.ShapeDtypeStruct((S, BW), F32), jax.ShapeDtypeStruct((S, 128), F32),
                   jax.ShapeDtypeStruct((S, BW), MXU_DTYPE)] + xshape,
        scratch=[pltpu.VMEM((NH, t, 1), F32), pltpu.VMEM((NH, t, 1), F32), pltpu.VMEM((t, BW), F32)] + xsem,
        sem=("arbitrary", "arbitrary"))(qa, ka, va, z, *(comm[1] if comm else []))
    return res[0], res[1], res[2], list(res[3:])


def _attn_gate_bwd(dy, o, z, qa, cc, lse):
    S = dy.shape[0]
    ts = _tile(S, ROW_TILE)

    def body(dy_ref, o_ref, zb_ref, qa_ref, cc_ref, lse_ref, dg_ref, qb_ref, doa_ref):
        g = zb_ref[...]
        dy, o = dy_ref[...], o_ref[...]
        do = dy * _silu(g)
        dg_ref[...] = _mx(dy * o * _dsilu(g))
        qb_ref[...] = qa_ref[...]
        doa_ref[...] = jnp.zeros_like(doa_ref)
        shifted = _split3_f32(cc_ref[...] - lse_ref[...])
        for h in range(NH):
            doh = do[:, _hs(h)]
            doa_ref[:, _ahd(h)] = _mx(doh)
            delta = jnp.sum(doh * o[:, _hs(h)], axis=-1, keepdims=True)
            _put_aug(doa_ref, h, 0, parts=_split3_f32(delta))
            _put_aug(qb_ref, h, 0, parts=[pc[:, h:h + 1] for pc in shifted])

    row = lambda w: pl.BlockSpec((ts, w), lambda i: (i, 0))
    return _call(
        body, name="attn_gate_bwd", grid=(S // ts,),
        in_specs=[pl.BlockSpec((ts, BW), lambda i: (i, 1)), row(BW), pl.BlockSpec((ts, BW), lambda i: (i, ZB_GATE // BW)),
                  row(NH * AW), row(128), row(128)],
        out_specs=[row(BW), row(NH * AW), row(NH * AW)],
        out_shape=[jax.ShapeDtypeStruct((S, BW), MXU_DTYPE), jax.ShapeDtypeStruct((S, NH * AW), MXU_DTYPE),
                   jax.ShapeDtypeStruct((S, NH * AW), MXU_DTYPE)],
        sem=("parallel",))(dy, o, z, qa, cc, lse)


def _aug_value(ref, h, rows=slice(None)):
    base = h * AW + _AUG
    x = ref[rows, base:base + 3].astype(F32)
    return x[:, 0:1] + x[:, 1:2] + x[:, 2:3]


def _flash_bwd_dq(qb, ka, va, doa, comm=None):
    S = qb.shape[0]
    t = _tile(S, FLASH_TILE)
    n = S // t
    rch = _tile(t, ROW_CHUNK)

    def core(q_ref, k_ref, v_ref, do_ref, dq_ref, do2_ref, dr_sc):
        i, j = pl.program_id(0), pl.program_id(1)

        @pl.when(j == 0)
        def _():
            dq_ref[...] = jnp.zeros_like(dq_ref)
            dr_sc[...] = jnp.zeros_like(dr_sc)

        def block(masked):
            for h in range(NH):
                for rc in range(t // rch):
                    rows = slice(rc * rch, (rc + 1) * rch)
                    p = jnp.exp(_dot_nt(q_ref[rows, _ah(h)], k_ref[:, _ah(h)]))
                    if masked:
                        p = jnp.where(_causal(p.shape, rc * rch), p, 0.0)
                    ds = p * _dot_nt(do_ref[rows, _ah(h)], v_ref[:, _ah(h)])
                    dq_ref[rows, _hs(h)] += _dot(ds, k_ref[:, _ahd(h)])
                    dr_sc[h, rows, :] += jnp.sum(ds, axis=-1, keepdims=True)

        @pl.when(j < i)
        def _():
            block(False)

        @pl.when(j == i)
        def _():
            block(True)
            do2_ref[...] = do_ref[...]
            for h in range(NH):
                _put_aug(do2_ref, h, 0, parts=_split3_f32(_aug_value(do_ref, h) + dr_sc[h]))

    qspec = lambda w: pl.BlockSpec((t, w), lambda i, j: (i, 0))
    kspec = lambda w: pl.BlockSpec((t, w), lambda i, j: (jnp.minimum(j, i), 0))
    xin, xout, xshape, xsem = _comm_specs(comm)
    res = _call(
        _fuse_comm(core, 4, 2, comm, _square_steps(n)), name="flash_bwd_dq", grid=(n, n),
        in_specs=[qspec(NH * AW), kspec(NH * AW), kspec(NH * AW), qspec(NH * AW)] + xin,
        out_specs=[qspec(BW), qspec(NH * AW)] + xout,
        out_shape=[jax.ShapeDtypeStruct((S, BW), F32), jax.ShapeDtypeStruct((S, NH * AW), MXU_DTYPE)] + xshape,
        scratch=[pltpu.VMEM((NH, t, 1), F32)] + xsem,
        sem=("arbitrary", "arbitrary"))(qb, ka, va, doa, *(comm[1] if comm else []))
    return res[0], res[1], list(res[2:])


def _flash_bwd_dkv(qb, ka, va, doa):
    S = qb.shape[0]
    t = _tile(S, FLASH_TILE)
    n = S // t

    def body(q_ref, k_ref, v_ref, do_ref, dk_ref, dv_ref, dc_ref):
        j, i = pl.program_id(0), pl.program_id(1)

        @pl.when(i == 0)
        def _():
            dk_ref[...] = jnp.zeros_like(dk_ref)
            dv_ref[...] = jnp.zeros_like(dv_ref)
            dc_ref[...] = jnp.zeros_like(dc_ref)

        def block(masked):
            for h in range(NH):
                pt = jnp.exp(_dot_nt(k_ref[:, _ah(h)], q_ref[:, _ah(h)]))
                if masked:
                    pt = jnp.where(_causal(pt.shape, 0, transposed=True), pt, 0.0)
                dst = pt * _dot_nt(v_ref[:, _ah(h)], do_ref[:, _ah(h)])
                dv_ref[:, _hs(h)] += _dot(pt, do_ref[:, _ahd(h)])
                dk_ref[:, _hs(h)] += _dot(dst, q_ref[:, _ahd(h)])
                dc_ref[:, h:h + 1] += -jnp.sum(dst, axis=-1, keepdims=True)

        @pl.when(i > j)
        def _():
            block(False)

        @pl.when(i == j)
        def _():
            block(True)

    qspec = lambda w: pl.BlockSpec((t, w), lambda j, i: (jnp.maximum(i, j), 0))
    kspec = lambda w: pl.BlockSpec((t, w), lambda j, i: (j, 0))
    return _call(
        body, name="flash_bwd_dkv", grid=(n, n),
        in_specs=[qspec(NH * AW), kspec(NH * AW), kspec(NH * AW), qspec(NH * AW)],
        out_specs=[kspec(BW), kspec(BW), kspec(128)],
        out_shape=[jax.ShapeDtypeStruct((S, BW), F32), jax.ShapeDtypeStruct((S, BW), F32),
                   jax.ShapeDtypeStruct((S, 128), F32)],
        sem=("parallel", "arbitrary"))(qb, ka, va, doa)


def _attn_prep_bwd(z, fb, gq, gk, dq, dk, dv, dgate, dcc, dz):
    S = z.shape[0]
    ts = _tile(S, ROW_TILE)
    nt = S // ts

    def body(zb_ref, zf_ref, fb_ref, gq_ref, gk_ref, dq_ref, dk_ref, dv_ref, dg_ref, dcc_ref, dz_in, dzb_ref, dzf_ref,
             st_ref, carry):
        @pl.when(pl.program_id(0) == 0)
        def _():
            carry[...] = jnp.zeros_like(carry)
            st_ref[...] = jnp.zeros_like(st_ref)

        zb = zb_ref[...]
        gqv, gkv = gq_ref[...], gk_ref[...]
        dqv, dkv = dq_ref[...], dk_ref[...]
        sq = jnp.zeros((1, DH), F32)
        sk = jnp.zeros((1, DH), F32)
        for h in range(NH):
            dx, dgr = _rms_bwd(zb[:, _hs(h)], gqv, dqv[:, _hs(h)] * (DH ** -0.5))
            dzb_ref[:, _hs(h)] = _mx(dx)
            sq = sq + jnp.sum(dgr, axis=0, keepdims=True)
            ks = slice(BW + h * DH, BW + (h + 1) * DH)
            dx, dgr = _rms_bwd(zb[:, ks], gkv, dkv[:, _hs(h)])
            dzb_ref[:, ks] = _mx(dx)
            sk = sk + jnp.sum(dgr, axis=0, keepdims=True)
        dzb_ref[:, 2 * BW:3 * BW] = _mx(dv_ref[...])
        dzb_ref[:, 3 * BW:4 * BW] = dg_ref[...]
        dc = dcc_ref[...]
        dlf = _dot_exact_lhs01(_tri(ts, upper=True), dc) + carry[...]
        carry[...] = dlf[0:1, :]
        dfz = dlf * _sigmoid(-(zf_ref[...] + fb_ref[...]))
        dzf_ref[...] = _mx(dfz)
        st_ref[0:1, 0:DH] += sq
        st_ref[1:2, 0:DH] += sk
        st_ref[2:3, :] += jnp.sum(dfz, axis=0, keepdims=True)

    rev = lambda w, c=0: pl.BlockSpec((ts, w), lambda i: (nt - 1 - i, c))
    one = lambda w: pl.BlockSpec((1, w), lambda i: (0, 0))
    return _call(
        body, name="attn_prep_bwd", grid=(nt,),
        in_specs=[rev(1024, ZB // 1024), rev(128, ZF // 128), one(128), one(DH), one(DH),
                  rev(BW), rev(BW), rev(BW), rev(BW), rev(128), _ANY],
        out_specs=[rev(1024, ZB // 1024), rev(128), pl.BlockSpec((8, 128), lambda i: (0, 0))],
        out_shape=[jax.ShapeDtypeStruct((S, ZCOLS), MXU_DTYPE), jax.ShapeDtypeStruct((S, 128), MXU_DTYPE),
                   jax.ShapeDtypeStruct((8, 128), F32)],
        scratch=[pltpu.VMEM((1, 128), F32)], sem=("arbitrary",), aliases={10: 0})(z, z, fb, gq, gk, dq, dk, dv, dgate, dcc, dz)


def _put_fgate_cols(dzf, dz):
    S = dzf.shape[0]
    ts = _tile(S, ROW_TILE)

    def body(f_ref, dz_in, o_ref):
        o_ref[...] = f_ref[...]

    return _call(
        body, name="put_fgate_cols", grid=(S // ts,),
        in_specs=[pl.BlockSpec((ts, 128), lambda i: (i, 0)), _ANY],
        out_specs=pl.BlockSpec((ts, 128), lambda i: (i, ZF // 128)),
        out_shape=jax.ShapeDtypeStruct((S, ZCOLS), MXU_DTYPE), sem=("parallel",), aliases={1: 0})(dzf, dz)


def _hgrn_consts():
    C = CHUNK
    t = np.arange(C)[:, None]
    j = np.arange(C)[None, :]
    masks = []
    for lev in range(NLEV):
        m = C >> (lev + 1)
        blk, pos = t // (2 * m), t % (2 * m)
        sblk, spos = j // (2 * m), j % (2 * m)
        masks.append((blk == sblk) & (pos >= m) & (spos < m))
    masks.append(t == j)
    return (j <= t).astype(np.float32), np.stack(masks).astype(np.float32)


def _level_exponents(b, lg):
    row = lax.broadcasted_iota(jnp.int32, (CHUNK, 1), 0)
    eqs, eks = [], []
    for lev in range(NLEV):
        m = CHUNK >> (lev + 1)
        pos = jnp.bitwise_and(row, 2 * m - 1)
        if 2 * m >= 8:
            b3 = b.reshape(CHUNK // (2 * m), 2 * m, b.shape[1])
            mid = jnp.broadcast_to(b3[:, m - 1:m, :], b3.shape).reshape(b.shape)
            eqs.append(jnp.where(pos >= m, b - mid, 0.0))
            eks.append(jnp.where(pos < m, mid - b, 0.0))
        elif m == 2:
            eqs.append(jnp.where(pos == 2, lg, jnp.where(pos == 3, lg + pltpu.roll(lg, 1, 0), 0.0)))
            eks.append(jnp.where(pos == 0, pltpu.roll(lg, CHUNK - 1, 0), 0.0))
        else:
            eqs.append(jnp.where(pos == 1, lg, 0.0))
            eks.append(jnp.zeros_like(lg))
    return jnp.concatenate([b] + eqs + eks, axis=0)


def _hgrn_chunk_fwd(zc, lb, tmat):
    cq, cf, ci = zc[:, 0:BW], zc[:, BW:2 * BW], zc[:, 2 * BW:3 * BW]
    q = _silu(cq)
    sg = _sigmoid(cf)
    g = lb + (1.0 - lb) * sg
    lg = jnp.log(g)
    kf = (1.0 - lb) * _sigmoid(-cf)
    b = _dot_exact_lhs01(tmat, lg)
    e = _level_exponents(b, lg)
    blast = b[CHUNK - 1:CHUNK, :]
    return cq, cf, q, sg, g, kf, ci, e, b, blast


def _round_mx(x):
    if MXU_DTYPE != jnp.bfloat16:
        return x
    u = lax.bitcast_convert_type(x, jnp.uint32)
    u = (u + jnp.uint32(0x7FFF) + ((u >> 16) & jnp.uint32(1))) & jnp.uint32(0xFFFF0000)
    return lax.bitcast_convert_type(u, F32)


def _hgrn_scores(q, kf, e, masks, h, scores=True):
    qh, kh = q[:, _hs(h)], kf[:, _hs(h)]
    ql, kl = [], []
    a = None
    for lev in range(NLEV + 1):
        if lev < NLEV:
            eq = jnp.exp(e[(1 + lev) * CHUNK:(2 + lev) * CHUNK, _hs(h)])
            ek = jnp.exp(e[(1 + NLEV + lev) * CHUNK:(2 + NLEV + lev) * CHUNK, _hs(h)])
            ql.append((_round_mx(qh * eq), eq))
            kl.append((_round_mx(kh * ek), ek))
        else:
            ql.append((_round_mx(qh), None))
            kl.append((_round_mx(kh), None))
        if scores:
            term = masks[lev] * _dot_nt(ql[-1][0], kl[-1][0])
            a = term if a is None else a + term
    return a, ql, kl


def _hgrn_fwd(z, lb, gain, tmat, masks):
    S = z.shape[0]
    ts = _tile(S, ROW_TILE)
    nc = ts // CHUNK

    def body(zc_ref, lb_ref, gn_ref, tm_ref, mk_ref, o_ref, st_ref, a_ref, y_ref, state):
        @pl.when(pl.program_id(0) == 0)
        def _():
            state[...] = jnp.zeros_like(state)

        lbv, gn, tm, mk = lb_ref[...], gn_ref[...], tm_ref[...], mk_ref[...]

        def chunk(c, carry):
            r0 = pl.multiple_of(c * CHUNK, CHUNK)
            zc = zc_ref[pl.ds(r0, CHUNK), :]
            cq, cf, q, sg, g, kf, v, e, b, blast = _hgrn_chunk_fwd(zc, lbv, tm)
            qe = q * jnp.exp(b)
            kd = kf * jnp.exp(blast - b)
            st_ref[pl.ds(r0, CHUNK), :] = state[...]
            for h in range(NH):
                sth = state[:, _hs(h)]
                a = _mx(_hgrn_scores(q, kf, e, mk, h)[0])
                a_ref[pl.ds(r0, CHUNK), _hs(h)] = a
                oh = _dot_nt(qe[:, _hs(h)], sth) + _dot(a, v[:, _hs(h)])
                state[:, _hs(h)] = sth * jnp.exp(blast[:, _hs(h)]) + _dot_tn(v[:, _hs(h)], kd[:, _hs(h)])
                o_ref[pl.ds(r0, CHUNK), _hs(h)] = oh
                yn = _rms_fwd(oh, gn[:, _hs(h)])
                y_ref[pl.ds(r0, CHUNK), _hs(h)] = _mx(yn * _silu(zc[:, 3 * BW + h * DH:3 * BW + (h + 1) * DH]))
            return carry

        lax.fori_loop(0, nc, chunk, 0, unroll=min(4, nc))

    row = pl.BlockSpec((ts, BW), lambda i: (i, 0))
    one = pl.BlockSpec((1, BW), lambda i: (0, 0))
    return _call(
        body, name="hgrn_fwd", grid=(S // ts,),
        in_specs=[pl.BlockSpec((ts, 1024), lambda i: (i, ZC // 1024)), one, one,
                  pl.BlockSpec(tmat.shape, lambda i: (0, 0)), pl.BlockSpec(masks.shape, lambda i: (0, 0, 0))],
        out_specs=[row, row, row, row],
        out_shape=[jax.ShapeDtypeStruct((S, BW), F32), jax.ShapeDtypeStruct((S, BW), F32),
                   jax.ShapeDtypeStruct((S, BW), MXU_DTYPE), jax.ShapeDtypeStruct((S, BW), MXU_DTYPE)],
        scratch=[pltpu.VMEM((CHUNK, BW), F32)], sem=("arbitrary",))(z, lb, gain, tmat, masks)


def _hgrn_bwd(z, lb, gain, tmat, masks, o_pre, states, scores, dy, dz):
    S = z.shape[0]
    ts = _tile(S, ROW_TILE)
    nt = S // ts
    nc = ts // CHUNK

    def body(zc_ref, lb_ref, gn_ref, tm_ref, mk_ref, o_ref, st_ref, a_ref, dy_ref, dz_in, dz_ref, stat_ref, dstate):
        @pl.when(pl.program_id(0) == 0)
        def _():
            dstate[...] = jnp.zeros_like(dstate)
            stat_ref[...] = jnp.zeros_like(stat_ref)

        lbv, gn, tm, mk = lb_ref[...], gn_ref[...], tm_ref[...], mk_ref[...]
        upper = _tri(CHUNK, upper=True)
        lower_strict = 1.0 - upper

        def chunk(cc, carry):
            c = nc - 1 - cc
            r0 = pl.multiple_of(c * CHUNK, CHUNK)
            zc = zc_ref[pl.ds(r0, CHUNK), :]
            cq, cf, q, sg, g, kf, v, e, b, blast = _hgrn_chunk_fwd(zc, lbv, tm)
            eb = jnp.exp(b)
            ebl = jnp.exp(blast - b)
            qe = q * eb
            kd = kf * ebl
            o = o_ref[pl.ds(r0, CHUNK), :]
            dyv = dy_ref[pl.ds(r0, CHUNK), :]
            stp = st_ref[pl.ds(r0, CHUNK), :]
            cg = zc[:, 3 * BW:4 * BW]
            sgate = _silu(cg)
            dq_parts, dk_parts, dv_parts, dcg_parts = [], [], [], []
            dgain, up_parts, lo_parts, const_parts = [], [], [], []
            for h in range(NH):
                hs = _hs(h)
                oh = o[:, hs]
                r = lax.rsqrt(jnp.mean(oh * oh, axis=-1, keepdims=True) + EPS)
                nrm = oh * r
                dyn = dyv[:, hs] * sgate[:, hs]
                dcg_parts.append(dyv[:, hs] * nrm * gn[:, hs] * _dsilu(cg[:, hs]))
                dgain.append(jnp.sum(dyn * nrm, axis=0, keepdims=True))
                tt = dyn * gn[:, hs]
                doh = r * (tt - nrm * jnp.mean(tt * nrm, axis=-1, keepdims=True))
                _, ql, kl = _hgrn_scores(q, kf, e, mk, h, scores=False)
                a = a_ref[pl.ds(r0, CHUNK), hs]
                da = _dot_nt(doh, v[:, hs])
                dsth = dstate[:, hs]
                ebh = jnp.exp(blast[:, hs])
                dv_parts.append(_dot_tn(a, doh) + _dot_nt(kd[:, hs], dsth))
                dq_inter = eb[:, hs] * _dot(doh, stp[:, hs])
                dk_state = ebl[:, hs] * _dot(v[:, hs], dsth)
                dqh, dkh, gh = dq_inter, dk_state, None
                for lev in range(NLEV + 1):
                    dal = mk[lev] * da
                    xq = _dot(dal, kl[lev][0])
                    yk = _dot_tn(dal, ql[lev][0])
                    gterm = ql[lev][0] * xq - kl[lev][0] * yk
                    gh = gterm if gh is None else gh + gterm
                    dqh = dqh + (xq if lev == NLEV else ql[lev][1] * xq)
                    dkh = dkh + (yk if lev == NLEV else kl[lev][1] * yk)
                up_parts.append(gh + q[:, hs] * dq_inter)
                lo_parts.append(kf[:, hs] * dk_state)
                const_parts.append(jnp.sum(dsth * stp[:, hs], axis=0, keepdims=True) * ebh)
                dstate[:, hs] = dsth * ebh + _dot_tn(doh, qe[:, hs])
                dq_parts.append(dqh)
                dk_parts.append(dkh)
            dq = jnp.concatenate(dq_parts, axis=1)
            dk = jnp.concatenate(dk_parts, axis=1)
            dlg = (_dot_exact_lhs01(upper, jnp.concatenate(up_parts, axis=1))
                   + _dot_exact_lhs01(lower_strict, jnp.concatenate(lo_parts, axis=1))
                   + jnp.concatenate(const_parts, axis=1))
            dsg = sg * (1.0 - sg)
            dz_ref[pl.ds(r0, CHUNK), 0:BW] = _mx(dq * _dsilu(cq))
            dz_ref[pl.ds(r0, CHUNK), BW:2 * BW] = _mx((dlg / g - dk) * (1.0 - lbv) * dsg)
            dz_ref[pl.ds(r0, CHUNK), 2 * BW:3 * BW] = _mx(jnp.concatenate(dv_parts, axis=1))
            dz_ref[pl.ds(r0, CHUNK), 3 * BW:4 * BW] = _mx(jnp.concatenate(dcg_parts, axis=1))
            stat_ref[0:1, :] += jnp.concatenate(dgain, axis=1)
            stat_ref[1:2, :] += jnp.sum((dlg / g - dk) * (1.0 - sg), axis=0, keepdims=True)
            return carry

        lax.fori_loop(0, nc, chunk, 0, unroll=min(4, nc))

    rev = lambda w, c=0: pl.BlockSpec((ts, w), lambda i: (nt - 1 - i, c))
    one = pl.BlockSpec((1, BW), lambda i: (0, 0))
    return _call(
        body, name="hgrn_bwd", grid=(nt,),
        in_specs=[rev(1024, ZC // 1024), one, one, pl.BlockSpec(tmat.shape, lambda i: (0, 0)),
                  pl.BlockSpec(masks.shape, lambda i: (0, 0, 0)), rev(BW), rev(BW), rev(BW), rev(BW, 2), _ANY],
        out_specs=[rev(1024, ZC // 1024), pl.BlockSpec((8, BW), lambda i: (0, 0))],
        out_shape=[jax.ShapeDtypeStruct((S, ZCOLS), MXU_DTYPE), jax.ShapeDtypeStruct((8, BW), F32)],
        scratch=[pltpu.VMEM((CHUNK, BW), F32)],
        sem=("arbitrary",), aliases={9: 0})(z, lb, gain, tmat, masks, o_pre, states, scores, dy, dz)


def _lower_bounds_fwd(lb_logits):
    def body(l_ref, o_ref):
        l = l_ref[...]
        m = jnp.max(l, axis=0, keepdims=True)
        ex = jnp.exp(l - m)
        p = ex / jnp.sum(ex, axis=0, keepdims=True)
        cs = p[0:1, :]
        o_ref[0:1, :] = jnp.clip(cs - p[0:1, :], 0.0, 1.0)
        for d in range(1, DEPTH):
            cs = cs + p[d:d + 1, :]
            o_ref[d:d + 1, :] = jnp.clip(cs - p[0:1, :], 0.0, 1.0)

    full = pl.BlockSpec((DEPTH, BW), lambda: (0, 0))
    return _call(body, name="lower_bounds_fwd", grid=(), in_specs=[full], out_specs=full,
                 out_shape=jax.ShapeDtypeStruct((DEPTH, BW), F32))(lb_logits)


def _lower_bounds_bwd(lb_logits, dlow):
    def body(l_ref, d_ref, o_ref):
        l = l_ref[...]
        m = jnp.max(l, axis=0, keepdims=True)
        ex = jnp.exp(l - m)
        p = ex / jnp.sum(ex, axis=0, keepdims=True)
        dl = d_ref[...]
        cs = p[0:1, :]
        dcs = []
        for d in range(DEPTH):
            if d > 0:
                cs = cs + p[d:d + 1, :]
            val = cs - p[0:1, :]
            dcs.append(jnp.where((val > 0.0) & (val < 1.0), dl[d:d + 1, :], 0.0))
        total = dcs[0] + dcs[1] + dcs[2] + dcs[3]
        dp = []
        for j in range(DEPTH):
            s = dcs[j]
            for d in range(j + 1, DEPTH):
                s = s + dcs[d]
            dp.append(s - total if j == 0 else s)
        inner = p[0:1, :] * dp[0]
        for j in range(1, DEPTH):
            inner = inner + p[j:j + 1, :] * dp[j]
        for j in range(DEPTH):
            o_ref[j:j + 1, :] = p[j:j + 1, :] * (dp[j] - inner)

    full = pl.BlockSpec((DEPTH, BW), lambda: (0, 0))
    return _call(body, name="lower_bounds_bwd", grid=(), in_specs=[full, full], out_specs=full,
                 out_shape=jax.ShapeDtypeStruct((DEPTH, BW), F32))(lb_logits, dlow)


def _sgu_fwd(z, gv, ws, bs):
    S = z.shape[0]
    ts = _tile(S, ROW_TILE)
    nc = ts // SCHUNK

    def body(zd_ref, gv_ref, ws_ref, bs_ref, y_ref):
        gvv, bsv = gv_ref[...], bs_ref[...]
        tril = _tri(SCHUNK)
        for c in range(nc):
            rs = slice(c * SCHUNK, (c + 1) * SCHUNK)
            zd = zd_ref[rs, :]
            for h in range(NH):
                vn = _rms_fwd(zd[:, BW + h * DH:BW + (h + 1) * DH], gvv[:, _hs(h)])
                s = _dot(ws_ref[h] * tril, vn) + bsv[:, h:h + 1]
                y_ref[rs, _hs(h)] = _mx(zd[:, _hs(h)] * s * _silu(zd[:, 2 * BW + h * DH:2 * BW + (h + 1) * DH]))

    return _call(
        body, name="sgu_fwd", grid=(S // ts,),
        in_specs=[pl.BlockSpec((ts, 1024), lambda i: (i, ZD // 1024)), pl.BlockSpec((1, BW), lambda i: (0, 0)),
                  pl.BlockSpec((NH, SCHUNK, SCHUNK), lambda i: (0, 0, 0)), pl.BlockSpec((SCHUNK, 128), lambda i: (0, 0))],
        out_specs=pl.BlockSpec((ts, BW), lambda i: (i, 0)),
        out_shape=jax.ShapeDtypeStruct((S, BW), MXU_DTYPE), sem=("parallel",))(z, gv, ws, bs)


def _sgu_bwd(z, gv, ws, bs, dy, dz):
    S = z.shape[0]
    ts = _tile(S, ROW_TILE)
    nc = ts // SCHUNK

    def body(zd_ref, gv_ref, ws_ref, bs_ref, dy_ref, dz_in, dz_ref, dws_ref, dbs_ref, st_ref):
        @pl.when(pl.program_id(0) == 0)
        def _():
            dws_ref[...] = jnp.zeros_like(dws_ref)
            dbs_ref[...] = jnp.zeros_like(dbs_ref)
            st_ref[...] = jnp.zeros_like(st_ref)

        gvv, bsv = gv_ref[...], bs_ref[...]
        tril = _tri(SCHUNK)
        dz_ref[:, 3 * BW:4 * BW] = jnp.zeros((ts, BW), MXU_DTYPE)
        for c in range(nc):
            rs = slice(c * SCHUNK, (c + 1) * SCHUNK)
            zd = zd_ref[rs, :]
            dyv = dy_ref[rs, :]
            for h in range(NH):
                hs = _hs(h)
                u = zd[:, hs]
                vraw = zd[:, BW + h * DH:BW + (h + 1) * DH]
                gt = zd[:, 2 * BW + h * DH:2 * BW + (h + 1) * DH]
                gvh = gvv[:, hs]
                vn = _rms_fwd(vraw, gvh)
                wm = ws_ref[h] * tril
                s = _dot(wm, vn) + bsv[:, h:h + 1]
                sil = _silu(gt)
                d = dyv[:, hs]
                ds = d * u * sil
                dz_ref[rs, hs] = _mx(d * s * sil)
                dz_ref[rs, 2 * BW + h * DH:2 * BW + (h + 1) * DH] = _mx(d * u * s * _dsilu(gt))
                dws_ref[h] += tril * _dot_nt(ds, vn)
                dbs_ref[:, h:h + 1] += jnp.sum(ds, axis=-1, keepdims=True)
                dvn = _dot_tn(wm, ds)
                dx, dgr = _rms_bwd(vraw, gvh, dvn)
                dz_ref[rs, BW + h * DH:BW + (h + 1) * DH] = _mx(dx)
                st_ref[0:1, hs] += jnp.sum(dgr, axis=0, keepdims=True)

    return _call(
        body, name="sgu_bwd", grid=(S // ts,),
        in_specs=[pl.BlockSpec((ts, 1024), lambda i: (i, ZD // 1024)), pl.BlockSpec((1, BW), lambda i: (0, 0)),
                  pl.BlockSpec((NH, SCHUNK, SCHUNK), lambda i: (0, 0, 0)), pl.BlockSpec((SCHUNK, 128), lambda i: (0, 0)),
                  pl.BlockSpec((ts, BW), lambda i: (i, 3)), _ANY],
        out_specs=[pl.BlockSpec((ts, 1024), lambda i: (i, ZD // 1024)), pl.BlockSpec((NH, SCHUNK, SCHUNK), lambda i: (0, 0, 0)),
                   pl.BlockSpec((SCHUNK, 128), lambda i: (0, 0)), pl.BlockSpec((8, BW), lambda i: (0, 0))],
        out_shape=[jax.ShapeDtypeStruct((S, ZCOLS), MXU_DTYPE), jax.ShapeDtypeStruct((NH, SCHUNK, SCHUNK), F32),
                   jax.ShapeDtypeStruct((SCHUNK, 128), F32), jax.ShapeDtypeStruct((8, BW), F32)],
        sem=("arbitrary",), aliases={5: 0})(z, gv, ws, bs, dy, dz)


def _merge_fwd(x, z, ys, wup, mb, wo):
    S = x.shape[0]
    ts = _tile(S, MERGE_TILE)

    def body(x_ref, zm_ref, ya_ref, yb_ref, yc_ref, yd_ref, wup_ref, mb_ref, wo_ref, x1_ref, mg_ref):
        yrefs = (ya_ref, yb_ref, yc_ref, yd_ref)
        mbv = mb_ref[...]
        merged = None
        for b in range(NBR):
            cs = slice(b * D_MODEL, (b + 1) * D_MODEL)
            term = _sigmoid(zm_ref[:, cs] + mbv[b:b + 1, :]) * jnp.dot(yrefs[b][...], wup_ref[b],
                                                                      preferred_element_type=F32)
            merged = term if merged is None else merged + term
        mg = _mx(merged)
        mg_ref[...] = mg
        x1_ref[...] = x_ref[...] + jnp.dot(mg, wo_ref[...], preferred_element_type=F32)

    row = lambda w: pl.BlockSpec((ts, w), lambda i: (i, 0))
    return _call(
        body, name="merge_fwd", grid=(S // ts,),
        in_specs=[row(D_MODEL), pl.BlockSpec((ts, 4096), lambda i: (i, 0)), row(BW), row(BW), row(BW), row(BW),
                  pl.BlockSpec((NBR, BW, D_MODEL), lambda i: (0, 0, 0)), pl.BlockSpec((NBR, D_MODEL), lambda i: (0, 0)),
                  pl.BlockSpec((D_MODEL, D_MODEL), lambda i: (0, 0))],
        out_specs=[row(D_MODEL), row(D_MODEL)],
        out_shape=[jax.ShapeDtypeStruct((S, D_MODEL), F32), jax.ShapeDtypeStruct((S, D_MODEL), MXU_DTYPE)],
        sem=("parallel",))(x, z, *ys, wup, mb, wo)


def _merge_bwd(dx1, z, ys, wup, mb, wo):
    S = dx1.shape[0]
    ts = _tile(S, MERGE_TILE)

    def body(dx_ref, zm_ref, ya_ref, yb_ref, yc_ref, yd_ref, wup_ref, mb_ref, wo_ref,
             dzm_ref, du_ref, dxb_ref, dy_ref, st_ref):
        @pl.when(pl.program_id(0) == 0)
        def _():
            st_ref[...] = jnp.zeros_like(st_ref)

        yrefs = (ya_ref, yb_ref, yc_ref, yd_ref)
        mbv = mb_ref[...]
        dxb = _mx(dx_ref[...])
        dxb_ref[...] = dxb
        dmerged = _dot_nt(dxb, wo_ref[...])
        for b in range(NBR):
            cs = slice(b * D_MODEL, (b + 1) * D_MODEL)
            u = jnp.dot(yrefs[b][...], wup_ref[b], preferred_element_type=F32)
            sg = _sigmoid(zm_ref[:, cs] + mbv[b:b + 1, :])
            du = _mx(dmerged * sg)
            du_ref[:, cs] = du
            dzm = dmerged * u * sg * (1.0 - sg)
            dzm_ref[:, cs] = _mx(dzm)
            st_ref[b:b + 1, :] += jnp.sum(dzm, axis=0, keepdims=True)
            dy_ref[:, b * BW:(b + 1) * BW] = _dot_nt(du, wup_ref[b])

    row = lambda w: pl.BlockSpec((ts, w), lambda i: (i, 0))
    return _call(
        body, name="merge_bwd", grid=(S // ts,),
        in_specs=[row(D_MODEL), pl.BlockSpec((ts, 4096), lambda i: (i, 0)), row(BW), row(BW), row(BW), row(BW),
                  pl.BlockSpec((NBR, BW, D_MODEL), lambda i: (0, 0, 0)),
                  pl.BlockSpec((NBR, D_MODEL), lambda i: (0, 0)), pl.BlockSpec((D_MODEL, D_MODEL), lambda i: (0, 0))],
        out_specs=[row(4096), row(4096), row(D_MODEL), row(D_MODEL), pl.BlockSpec((8, D_MODEL), lambda i: (0, 0))],
        out_shape=[jax.ShapeDtypeStruct((S, ZCOLS), MXU_DTYPE), jax.ShapeDtypeStruct((S, 4096), MXU_DTYPE),
                   jax.ShapeDtypeStruct((S, D_MODEL), MXU_DTYPE), jax.ShapeDtypeStruct((S, D_MODEL), F32),
                   jax.ShapeDtypeStruct((8, D_MODEL), F32)],
        sem=("arbitrary",))(dx1, z, *ys, wup, mb, wo)


def _ple_fwd(x1, p, g, wg, wp):
    S = x1.shape[0]
    ts = _tile(S, ROW_TILE)

    def body(x_ref, p_ref, g_ref, wg_ref, wp_ref, o_ref):
        x = x_ref[...]
        hp = _mx(_rms_fwd(x, g_ref[...]))
        gate = _sigmoid(jnp.dot(hp, wg_ref[...], preferred_element_type=F32))
        pp = jnp.dot(_mx(p_ref[...]), wp_ref[...], preferred_element_type=F32)
        o_ref[...] = x + gate * pp

    row = lambda w: pl.BlockSpec((ts, w), lambda i: (i, 0))
    return _call(
        body, name="ple_fwd", grid=(S // ts,),
        in_specs=[row(D_MODEL), row(PLE), pl.BlockSpec((1, D_MODEL), lambda i: (0, 0)),
                  pl.BlockSpec((D_MODEL, D_MODEL), lambda i: (0, 0)), pl.BlockSpec((PLE, D_MODEL), lambda i: (0, 0))],
        out_specs=row(D_MODEL), out_shape=jax.ShapeDtypeStruct((S, D_MODEL), F32),
        sem=("parallel",))(x1, p, g, wg, wp)


def _ple_bwd(x1, p, dx2, g, wg, wp):
    S = x1.shape[0]
    ts = _tile(S, ROW_TILE)

    def body(x_ref, p_ref, dx_ref, g_ref, wg_ref, wp_ref, dx1_ref, hp_ref, dgl_ref, dpp_ref, pb_ref, st_ref):
        @pl.when(pl.program_id(0) == 0)
        def _():
            st_ref[...] = jnp.zeros_like(st_ref)

        x, gv, dx2 = x_ref[...], g_ref[...], dx_ref[...]
        hp = _mx(_rms_fwd(x, gv))
        hp_ref[...] = hp
        gate = _sigmoid(jnp.dot(hp, wg_ref[...], preferred_element_type=F32))
        pb = _mx(p_ref[...])
        pb_ref[...] = pb
        pp = jnp.dot(pb, wp_ref[...], preferred_element_type=F32)
        dgl = _mx(dx2 * pp * gate * (1.0 - gate))
        dgl_ref[...] = dgl
        dpp_ref[...] = _mx(dx2 * gate)
        dhp = _dot_nt(dgl, wg_ref[...])
        dxn, dgr = _rms_bwd(x, gv, dhp)
        dx1_ref[...] = dx2 + dxn
        st_ref[0:1, :] += jnp.sum(dgr, axis=0, keepdims=True)

    row = lambda w: pl.BlockSpec((ts, w), lambda i: (i, 0))
    sq = pl.BlockSpec((D_MODEL, D_MODEL), lambda i: (0, 0))
    return _call(
        body, name="ple_bwd", grid=(S // ts,),
        in_specs=[row(D_MODEL), row(PLE), row(D_MODEL), pl.BlockSpec((1, D_MODEL), lambda i: (0, 0)), sq,
                  pl.BlockSpec((PLE, D_MODEL), lambda i: (0, 0))],
        out_specs=[row(D_MODEL), row(D_MODEL), row(D_MODEL), row(D_MODEL), row(PLE),
                   pl.BlockSpec((8, D_MODEL), lambda i: (0, 0))],
        out_shape=[jax.ShapeDtypeStruct((S, D_MODEL), F32)] + [jax.ShapeDtypeStruct((S, D_MODEL), MXU_DTYPE)] * 3
        + [jax.ShapeDtypeStruct((S, PLE), MXU_DTYPE), jax.ShapeDtypeStruct((8, D_MODEL), F32)],
        sem=("arbitrary",))(x1, p, dx2, g, wg, wp)


def _pad_rows(a, rows=8):
    return jnp.concatenate([a, jnp.zeros((rows - a.shape[0],) + a.shape[1:], a.dtype)], axis=0)


def _pad_lanes(a, lanes=128):
    return jnp.concatenate([a, jnp.zeros(a.shape[:-1] + (lanes - a.shape[-1],), a.dtype)], axis=-1)


def _wz_from_w_in(w):
    zeros = lambda n: jnp.zeros((w.shape[0], n), w.dtype)
    return jnp.concatenate([w[:, _OM:_OEND], w[:, _OA:_OB], w[:, _OB:_OF], w[:, _OC:_OD], w[:, _OD:_OM], zeros(256),
                            w[:, _OF:_OC], zeros(124)], axis=1)


def _w_in_from_wz(g):
    return jnp.concatenate([g[:, ZA:ZB], g[:, ZB:ZC], g[:, ZF:ZF + 4], g[:, ZC:ZD], g[:, ZD:ZD + 768], g[:, ZM:ZA]], axis=1)


_W_IN_GROUPS = [(_OA, _OB, ZA), (_OB, _OF, ZB), (_OF, _OC, ZF), (_OC, _OD, ZC), (_OD, _OM, ZD), (_OM, _OEND, ZM)]


def _wz_from_shards(g):
    n = g.shape[-1]
    pieces, pos = [], 0
    for a, b, zs in sorted(_W_IN_GROUPS, key=lambda grp: grp[2]):
        if zs > pos:
            pieces.append(jnp.zeros((g.shape[1], zs - pos), g.dtype))
        for k in range(4):
            lo, hi = max(a, k * n), min(b, (k + 1) * n)
            if lo < hi:
                pieces.append(g[k][:, lo - k * n:hi - k * n])
        pos = zs + (b - a)
    pieces.append(jnp.zeros((g.shape[1], ZCOLS - pos), g.dtype))
    return jnp.concatenate(pieces, axis=1)


def _w_in_slabs_from_wz(g):
    n = _OEND // 4
    slabs = []
    for k in range(4):
        pieces = []
        for a, b, zs in _W_IN_GROUPS:
            lo, hi = max(a, k * n), min(b, (k + 1) * n)
            if lo < hi:
                pieces.append(g[:, zs + lo - a:zs + hi - a])
        slabs.append(jnp.concatenate(pieces, axis=1))
    return jnp.stack(slabs)


def _local_step(x, p, target, wts, dist=None):
    tmat_np, masks_np = _hgrn_consts()
    tmat = jnp.asarray(tmat_np, MXU_DTYPE)
    masks = jnp.asarray(masks_np, F32)
    lower = _lower_bounds_fwd(wts["lb_logits"])
    saved = []
    for li in range(DEPTH):
        big = dist.weights(li) if dist else {n: wts[n][li] for n in _GATHERED}
        wz = _wz_from_shards(big["w_in"]) if big["w_in"].ndim == 3 else _wz_from_w_in(big["w_in"])
        g_mix = wts["norm_mix"][li][None, :]
        h = _norm_fwd(x, g_mix)
        z = _mm_nn(h, wz, F32, name="mm_z", ts=2 * ROW_TILE, tn=1664)
        cw = _pad_rows(jnp.concatenate([big["conv_w"], wts["conv_b"][li][None, :]], axis=0))
        ya = _conv_fwd(z, cw)
        fb = _pad_lanes(wts["fgate_bias"][li][None, :])
        gq, gk = wts["q_norm"][li][None, :], wts["k_norm"][li][None, :]
        qa, ka, va, cc = _attn_prep_fwd(z, fb, gq, gk)
        o, lse, yb, got = _flash_fwd(qa, ka, va, z, comm=dist.fwd_comm(li) if dist else None)
        if dist:
            dist.fwd_done(li, got)
        lb = lower[li][None, :]
        gh = wts["hgrn_norm"][li][None, :]
        o_pre, states, hscores, yc = _hgrn_fwd(z, lb, gh, tmat, masks)
        gv = wts["sgu_norm"][li][None, :]
        ws = wts["spatial_w"][li]
        bs = _pad_lanes(wts["spatial_b"][li].T)
        yd = _sgu_fwd(z, gv, ws, bs)
        ys = (ya, yb, yc, yd)
        x1, merged = _merge_fwd(x, z, ys, big["w_up"], big["merge_b"], big["w_o"])
        g_ple = wts["norm_ple"][li][None, :]
        x2 = _ple_fwd(x1, p[li], g_ple, big["w_ple_gate"], big["w_ple_proj"])
        saved.append(dict(x=x, h=h, z=z, wz=wz, cw=cw, fb=fb, gq=gq, gk=gk, qa=qa, ka=ka, va=va, cc=cc, o=o, lse=lse,
                          lb=lb, gh=gh, o_pre=o_pre, states=states, hscores=hscores, gv=gv, ws=ws, bs=bs, ys=ys, x1=x1, merged=merged,
                          g_mix=g_mix, g_ple=g_ple, big=big))
        x = x2

    loss, dx = _loss_fwd_bwd(x, target)

    names = ["norm_mix", "w_in", "conv_w", "conv_b", "fgate_bias", "q_norm", "k_norm", "lb", "hgrn_norm", "sgu_norm",
             "spatial_w", "spatial_b", "w_up", "merge_b", "w_o", "norm_ple", "w_ple_gate", "w_ple_proj"]
    gl = {n: [None] * DEPTH for n in names}
    for li in reversed(range(DEPTH)):
        s = saved[li]
        z, big = s["z"], s["big"]
        wg, wp = big["w_ple_gate"], big["w_ple_proj"]
        dx1, hp, dgl, dpp, pb, st = _ple_bwd(s["x1"], p[li], dx, s["g_ple"], wg, wp)
        gl["norm_ple"][li] = st[0]
        gl["w_ple_gate"][li] = _mm_tn(hp, dgl, name="mm_dwg")
        gl["w_ple_proj"][li] = _mm_tn(pb, dpp, name="mm_dwp")
        wup, wo = big["w_up"], big["w_o"]
        dz, du, dxb, dy, st = _merge_bwd(dx1, z, s["ys"], wup, big["merge_b"], wo)
        gl["merge_b"][li] = st[0:NBR]
        gl["w_o"][li] = _mm_tn(s["merged"], dxb, name="mm_dwo")
        gl["w_up"][li] = jnp.stack([_mm_tn(s["ys"][b], du, name="mm_dwup", ycol=b, n=D_MODEL) for b in range(NBR)])
        dz, st = _conv_bwd(z, s["cw"], dy, dz)
        gl["conv_w"][li] = st[0:3]
        gl["conv_b"][li] = st[3]
        dgate, qb, doa = _attn_gate_bwd(dy, s["o"], z, s["qa"], s["cc"], s["lse"])
        if dist and li == 0:
            dist.push(0, {n: gl[n][0] for n in _BIG[1:]})
        dq, doa, got = _flash_bwd_dq(qb, s["ka"], s["va"], doa, comm=dist.bwd_comm() if dist else None)
        if dist:
            dist.bwd_done(got)
        dk, dv, dcc = _flash_bwd_dkv(qb, s["ka"], s["va"], doa)
        dz, dzf, st = _attn_prep_bwd(z, s["fb"], s["gq"], s["gk"], dq, dk, dv, dgate, dcc, dz)
        dz = _put_fgate_cols(dzf, dz)
        gl["q_norm"][li] = st[0, 0:DH]
        gl["k_norm"][li] = st[1, 0:DH]
        gl["fgate_bias"][li] = st[2, 0:NH]
        dz, st = _hgrn_bwd(z, s["lb"], s["gh"], tmat, masks, s["o_pre"], s["states"], s["hscores"], dy, dz)
        gl["hgrn_norm"][li] = st[0]
        gl["lb"][li] = st[1]
        dz, dws, dbs, st = _sgu_bwd(z, s["gv"], s["ws"], s["bs"], dy, dz)
        gl["sgu_norm"][li] = st[0]
        gl["spatial_w"][li] = dws
        gl["spatial_b"][li] = dbs[:, 0:NH].T
        dwz = _mm_tn(s["h"], dz, name="mm_dwz", tn=1664)
        gl["w_in"][li] = dwz if dist else _w_in_from_wz(dwz)
        if dist:
            dist.push(li, {n: gl[n][li] for n in (_BIG[:1] if li == 0 else _BIG)})
        if dist and li == 0:
            dh, got = _mm_nn(dz, s["wz"], F32, name="mm_dh", ts=ROW_TILE // 2, b_transposed=True, comm=dist.bwd_comm())
            dist.bwd_done(got)
        else:
            dh = _mm_nn(dz, s["wz"], F32, name="mm_dh", ts=ROW_TILE // 2, b_transposed=True)
        dx, st = _norm_bwd(s["x"], s["g_mix"], dh, dx1)
        gl["norm_mix"][li] = st[0]

    if dist:
        dist.finish()
    grads = {n: jnp.stack(v) for n, v in gl.items() if not (dist and n in _BIG)}
    grads["lb_logits"] = _lower_bounds_bwd(wts["lb_logits"], grads.pop("lb"))
    return loss, dx, grads


def _my_pos():
    return lax.axis_index("x"), lax.axis_index("y"), lax.axis_index("c")


def _gather_phase(phase, ins, outs, sems):
    if phase == 1:
        return
    send, recv, lsem = sems
    x, y, c = _my_pos()
    me = 2 * x + y
    peers = [(1 - x, y), (x, 1 - y), (1 - x, 1 - y)]
    copies = []
    for t in range(len(ins)):
        copies.append(pltpu.make_async_copy(ins[t], outs[t].at[me], lsem.at[t]))
        for j, (px, py) in enumerate(peers):
            copies.append(pltpu.make_async_remote_copy(
                src_ref=ins[t], dst_ref=outs[t].at[me], send_sem=send.at[t, j], recv_sem=recv.at[t, j],
                device_id=(px, py, c), device_id_type=MESH))
    for cp in copies:
        if phase == 0:
            cp.start()
        else:
            cp.wait()


N_HALVED = 5


def _gather_halves_phase(phase, ins, outs, sems):
    send, recv, lsem = sems
    x, y, c = _my_pos()
    me = 2 * x + y
    sib = (x, y, 1 - c)
    chips = [(1 - x, y), (x, 1 - y), (1 - x, 1 - y)]

    def rc(t, k, src, dst, dev):
        return pltpu.make_async_remote_copy(src_ref=src, dst_ref=dst, send_sem=send.at[t, k], recv_sem=recv.at[t, k],
                                            device_id=dev, device_id_type=MESH)

    for t in range(len(ins)):
        local = pltpu.make_async_copy(ins[t], outs[t].at[me], lsem.at[t])
        if t >= N_HALVED:
            whole = [rc(t, j, ins[t], outs[t].at[me], (px, py, c)) for j, (px, py) in enumerate(chips)]
            for cp in [local] + whole:
                if phase == 0:
                    cp.start()
                elif phase == 2:
                    cp.wait()
            continue
        hr = ins[t].shape[0] // 2
        mine, other = pl.ds(c * hr, hr), pl.ds((1 - c) * hr, hr)
        first = [rc(t, j, ins[t].at[mine], outs[t].at[me, mine], (px, py, c)) for j, (px, py) in enumerate(chips)]
        landed = [outs[t].at[2 * px + py, mine] for px, py in chips]
        passed = [rc(t, 3 + j, slot, slot, sib) for j, slot in enumerate(landed)]
        if phase == 0:
            local.start()
            for cp in first:
                cp.start()
        elif phase == 1:
            for j, slot in enumerate(landed):
                rc(t, j, slot, slot, (x, y, c)).wait_recv()
                passed[j].start()
        else:
            for j, (px, py) in enumerate(chips):
                slot = outs[t].at[2 * px + py, other]
                rc(t, 3 + j, slot, slot, (x, y, c)).wait_recv()
            for cp in first + passed:
                cp.wait_send()
            local.wait()


def _exchange_phase(phase, ins, outs, sems):
    send, recv, lsem = sems
    x, y, c = _my_pos()
    me = 2 * x + y
    sib = (x, y, 1 - c)
    chips = [(1 - x, y), (x, 1 - y), (1 - x, 1 - y)]

    def rc(t, k, src, dst, dev):
        return pltpu.make_async_remote_copy(src_ref=src, dst_ref=dst, send_sem=send.at[t, k], recv_sem=recv.at[t, k],
                                            device_id=dev, device_id_type=MESH)

    for t in range(len(ins)):
        local = pltpu.make_async_copy(ins[t].at[me], outs[t].at[2 * me + c], lsem.at[t])
        first = [rc(t, 0, ins[t].at[me], outs[t].at[2 * me + c], sib)]
        first += [rc(t, 1 + j, ins[t].at[2 * px + py], outs[t].at[2 * me + c], (px, py, c))
                  for j, (px, py) in enumerate(chips)]
        landed = [outs[t].at[2 * (2 * px + py) + c] for px, py in chips]
        passed = [rc(t, 4 + j, slot, slot, sib) for j, slot in enumerate(landed)]
        if phase == 0:
            local.start()
            for cp in first:
                cp.start()
        elif phase == 1:
            for j, slot in enumerate(landed):
                rc(t, 1 + j, slot, slot, (x, y, c)).wait_recv()
                passed[j].start()
        else:
            s0 = outs[t].at[2 * me + (1 - c)]
            rc(t, 0, s0, s0, (x, y, c)).wait_recv()
            for j, (px, py) in enumerate(chips):
                slot = outs[t].at[2 * (2 * px + py) + (1 - c)]
                rc(t, 4 + j, slot, slot, (x, y, c)).wait_recv()
            for cp in first + passed:
                cp.wait_send()
            local.wait()


_COMM_PHASES = {"gather": _gather_phase, "gather_halves": _gather_halves_phase, "exchange": _exchange_phase}
_gathered_shapes = lambda arrays: [jax.ShapeDtypeStruct((4,) + a.shape, a.dtype) for a in arrays]
_COMM_OUT = {"gather": _gathered_shapes, "gather_halves": _gathered_shapes,
             "exchange": lambda arrays: [jax.ShapeDtypeStruct((8,) + a.shape[1:], a.dtype) for a in arrays]}


def _comm_sems(kind, nc):
    k = {"gather": 3, "gather_halves": 6, "exchange": 7}[kind]
    return [pltpu.SemaphoreType.DMA((nc, k)), pltpu.SemaphoreType.DMA((nc, k)), pltpu.SemaphoreType.DMA((nc,))]


def _comm_alone(kind, arrays, name):
    nc = len(arrays)

    def body(*refs):
        for phase in range(3):
            _COMM_PHASES[kind](phase, refs[:nc], refs[nc:2 * nc], refs[2 * nc:])

    hbm = pl.BlockSpec(memory_space=pl.ANY)
    return pl.pallas_call(
        functools.partial(body), name=name, in_specs=[hbm] * nc, out_specs=[hbm] * nc,
        out_shape=_COMM_OUT[kind](arrays), scratch_shapes=_comm_sems(kind, nc))(*arrays)


def _allreduce_small(v):
    R = v.shape[0]

    def body(v_ref, o_ref, pair, chips, send, recv):
        x, y, c = _my_pos()
        me = 2 * x + y
        pair[c] = v_ref[...]
        swap = pltpu.make_async_remote_copy(src_ref=v_ref, dst_ref=pair.at[c], send_sem=send.at[0], recv_sem=recv.at[0],
                                            device_id=(x, y, 1 - c), device_id_type=MESH)
        swap.start()
        swap.wait()
        chips[me] = pair[0] + pair[1]
        copies = [pltpu.make_async_remote_copy(src_ref=chips.at[me], dst_ref=chips.at[me], send_sem=send.at[1 + j],
                                               recv_sem=recv.at[1 + j], device_id=(px, py, c), device_id_type=MESH)
                  for j, (px, py) in enumerate([(1 - x, y), (x, 1 - y), (1 - x, 1 - y)])]
        for cp in copies:
            cp.start()
        for cp in copies:
            cp.wait()
        o_ref[...] = (chips[0] + chips[1]) + (chips[2] + chips[3])

    vm = pl.BlockSpec(memory_space=pltpu.VMEM)
    return pl.pallas_call(
        functools.partial(body), name="allreduce_small", in_specs=[vm], out_specs=vm,
        out_shape=jax.ShapeDtypeStruct((R, 128), F32),
        scratch_shapes=[pltpu.VMEM((2, R, 128), F32), pltpu.VMEM((4, R, 128), F32),
                        pltpu.SemaphoreType.DMA((4,)), pltpu.SemaphoreType.DMA((4,))],
        compiler_params=pltpu.CompilerParams(vmem_limit_bytes=VMEM_LIMIT))(v)


def _adamw(w, m, v, parts, nparts):
    A, R, C = w.shape
    per_layer = isinstance(parts, (list, tuple))
    tr = _tile(R, 64) if per_layer else (R if R <= 128 else _tile(R, 128))
    nparr = len(parts) if per_layer else 1

    def body(*refs):
        w_ref, m_ref, v_ref = refs[:3]
        p_refs = refs[3:3 + nparr]
        g_ref, d_ref, nm_ref, nv_ref = refs[3 + nparr:]

        def update(read):
            g = read(0).astype(F32)
            for k in range(1, nparts):
                g = g + read(k).astype(F32)
            mm = ADAM_B1 * m_ref[0] + (1.0 - ADAM_B1) * g
            vv = ADAM_B2 * v_ref[0] + (1.0 - ADAM_B2) * jnp.square(g)
            m_hat = mm / (1.0 - ADAM_B1 ** ADAM_STEP)
            v_hat = vv / (1.0 - ADAM_B2 ** ADAM_STEP)
            g_ref[0] = g
            d_ref[0] = -ADAM_LR * (m_hat / (jnp.sqrt(v_hat) + ADAM_EPS) + ADAM_WD * w_ref[0])
            nm_ref[0] = mm
            nv_ref[0] = vv

        if not per_layer:
            update(lambda k: p_refs[0][k, 0])
        else:
            for a in range(A):
                @pl.when(pl.program_id(0) == a)
                def _(a=a):
                    update(lambda k: p_refs[a][k])

    blk = pl.BlockSpec((1, tr, C), lambda a, r: (a, r, 0))
    if per_layer:
        pspecs = [pl.BlockSpec((nparts, tr, C), functools.partial(lambda a, r, l: (0, jnp.where(a == l, r, 0), 0), l=l))
                  for l in range(A)]
        pargs = list(parts)
    else:
        pspecs = [pl.BlockSpec((nparts, 1, tr, C), lambda a, r: (0, a, r, 0))]
        pargs = [parts]
    return _call(
        body, name="adamw", grid=(A, R // tr), in_specs=[blk, blk, blk] + pspecs,
        out_specs=[blk] * 4, out_shape=[jax.ShapeDtypeStruct((A, R, C), F32)] * 4,
        sem=("arbitrary", "arbitrary"))(w, m, v, *pargs)


def _rows128(a):
    width = a.shape[-1]
    a2 = a.reshape(-1, width)
    k = -(-width // 128)
    if width % 128:
        a2 = jnp.pad(a2, ((0, 0), (0, k * 128 - width)))
    return a2.reshape(-1, 128)


def _from_rows128(r, shape):
    width = shape[-1]
    k = -(-width // 128)
    return r.reshape(-1, k * 128)[:, :width].reshape(shape)


def _as3d(a):
    if a.ndim == 2:
        return a[None]
    if a.ndim == 3:
        return a
    return a.reshape((-1,) + a.shape[-2:])


_WEIGHTS = ["norm_mix", "w_in", "conv_w", "conv_b", "fgate_bias", "q_norm", "k_norm", "lb_logits", "hgrn_norm", "sgu_norm",
            "spatial_w", "spatial_b", "w_up", "merge_b", "w_o", "norm_ple", "w_ple_gate", "w_ple_proj"]
_BIG = ["w_in", "w_up", "w_o", "w_ple_gate", "w_ple_proj"]
_GATHERED = _BIG + ["conv_w", "merge_b"]
_SHARD_AXIS = {"w_in": 2, "w_up": 3, "w_o": 1, "w_ple_gate": 1, "w_ple_proj": 2, "conv_w": 2, "merge_b": 2}
_SMALL = [n for n in _WEIGHTS if n not in _BIG]


class _Dist:
    def __init__(self, w):
        self.w = w
        self.full = {0: self._unpack(_comm_alone("gather_halves", self._shards(0), "gather_weights"))}
        self.contrib = {l: {} for l in range(DEPTH)}
        self.pending = []

    def _shards(self, l):
        return [_mx(self.w[n][l]) for n in _BIG] + [self.w["conv_w"][l], self.w["merge_b"][l]]

    def _unpack(self, gathered):
        return {n: g if n == "w_in" else jnp.concatenate([g[k] for k in range(4)], axis=_SHARD_AXIS[n] - 1)
                for n, g in zip(_GATHERED, gathered)}

    def weights(self, l):
        return self.full[l]

    def fwd_comm(self, l):
        return ("gather", self._shards(l + 1)) if l + 1 < DEPTH else None

    def fwd_done(self, l, got):
        if got:
            self.full[l + 1] = self._unpack(got)

    def push(self, l, grads):
        for n, g in grads.items():
            slabs = _w_in_slabs_from_wz(g) if n == "w_in" else jnp.stack(jnp.split(g, 4, axis=_SHARD_AXIS[n] - 1))
            self.pending.append((l, n, slabs.astype(GRAD_WIRE_DTYPE)))

    def bwd_comm(self):
        return ("exchange", [slabs for _, _, slabs in self.pending]) if self.pending else None

    def bwd_done(self, got):
        if got:
            for (l, n, _), c8 in zip(self.pending, got):
                self.contrib[l][n] = c8
            self.pending = []

    def finish(self):
        if self.pending:
            self.bwd_done(_comm_alone("exchange", self.bwd_comm()[1], "exchange_grads"))


def kernel(x, p, norm_mix, w_in, conv_w, conv_b, fgate_bias, q_norm, k_norm, lb_logits, hgrn_norm, sgu_norm, spatial_w, spatial_b, w_up, merge_b, w_o, norm_ple, w_ple_gate, w_ple_proj, loss_target, m_norm_mix, m_w_in, m_conv_w, m_conv_b, m_fgate_bias, m_q_norm, m_k_norm, m_lb_logits, m_hgrn_norm, m_sgu_norm, m_spatial_w, m_spatial_b, m_w_up, m_merge_b, m_w_o, m_norm_ple, m_w_ple_gate, m_w_ple_proj, v_norm_mix, v_w_in, v_conv_w, v_conv_b, v_fgate_bias, v_q_norm, v_k_norm, v_lb_logits, v_hgrn_norm, v_sgu_norm, v_spatial_w, v_spatial_b, v_w_up, v_merge_b, v_w_o, v_norm_ple, v_w_ple_gate, v_w_ple_proj):
    loc = dict(locals())
    w = {n: loc[n] for n in _WEIGHTS}
    m = {n: loc["m_" + n] for n in _WEIGHTS}
    v = {n: loc["v_" + n] for n in _WEIGHTS}
    chip = 2 * lax.axis_index("x") + lax.axis_index("y")

    dist = _Dist(w)
    loss_blk, dx, grads = _local_step(x[0], p[:, 0], loss_target[0], w, dist)
    loss = lax.psum(loss_blk[0, 0], ("x", "y", "c"))

    blocks = [_rows128(grads[n]) for n in _SMALL]
    nrows = sum(b.shape[0] for b in blocks)
    packed = jnp.concatenate(blocks + [jnp.zeros(((-nrows) % 8, 128), F32)], axis=0)
    red = _allreduce_small(packed)
    small, off = {}, 0
    for n, b in zip(_SMALL, blocks):
        small[n] = _from_rows128(red[off:off + b.shape[0]], grads[n].shape)
        off += b.shape[0]
    for n in ("conv_w", "merge_b"):
        ax = _SHARD_AXIS[n]
        width = small[n].shape[ax] // 4
        small[n] = lax.dynamic_slice_in_dim(small[n], chip * width, width, axis=ax)

    out_g, out_d, out_m, out_v = {}, {}, {}, {}
    for n in _WEIGHTS:
        shp = w[n].shape
        if n in _BIG:
            w3, m3, v3 = (a.reshape((DEPTH, -1, shp[-1])) for a in (w[n], m[n], v[n]))
            parts = [dist.contrib[l][n].reshape((8,) + w3.shape[1:]) for l in range(DEPTH)]
            g, d, nm, nv = _adamw(w3, m3, v3, parts, 8)
        else:
            g, d, nm, nv = _adamw(_as3d(w[n]), _as3d(m[n]), _as3d(v[n]), _as3d(small[n])[None], 1)
        out_g[n], out_d[n], out_m[n], out_v[n] = (a.reshape(shp) for a in (g, d, nm, nv))

    return (loss, dx[None], *[out_g[n] for n in _WEIGHTS], *[out_d[n] for n in _WEIGHTS],
            *[out_m[n] for n in _WEIGHTS], *[out_v[n] for n in _WEIGHTS])
```

```python
import functools

import numpy as np
import jax
import jax.numpy as jnp
from jax import lax
from jax.experimental import pallas as pl
from jax.experimental.pallas import tpu as pltpu

F32 = jnp.float32
MXU_DTYPE = jnp.bfloat16

D_MODEL = 1024
BW = 256
NH = 4
DH = 64
DEPTH = 4
NBR = 4
PLE = 256
CHUNK = 64
SCHUNK = 128
EPS = 1e-6
MASK_VALUE = -1e30
NLEV = 6

ADAM_LR, ADAM_B1, ADAM_B2, ADAM_EPS, ADAM_WD, ADAM_STEP = 0.001, 0.9, 0.999, 1e-08, 0.01, 10

ZM, ZA, ZB, ZC, ZD, ZF = 0, 4096, 5120, 6144, 7168, 8192
ZCOLS = 8320
ZB_GATE = ZB + 3 * BW
_OA, _OB, _OF, _OC, _OD, _OM, _OEND = 0, 1024, 2048, 2052, 3076, 3844, 7940

VMEM_LIMIT = 56 * 1024 * 1024
ROW_TILE = 512
FLASH_TILE = 1024
GRAD_WIRE_DTYPE = jnp.bfloat16
MERGE_TILE = 512
MESH = pl.DeviceIdType.MESH
_ANY = pl.BlockSpec(memory_space=pl.ANY)


def _tile(n, pref):
    t = min(n, pref)
    assert n % t == 0, (n, t)
    return t


def _call(body, *, name, grid, in_specs, out_specs, out_shape, scratch=(), sem=None, aliases=None):
    return pl.pallas_call(
        functools.partial(body), name=name, grid=grid, in_specs=in_specs, out_specs=out_specs,
        out_shape=out_shape, scratch_shapes=list(scratch), input_output_aliases=aliases or {},
        compiler_params=pltpu.CompilerParams(dimension_semantics=sem, vmem_limit_bytes=VMEM_LIMIT))


def _mx(x):
    return x.astype(MXU_DTYPE)


def _dot(a, b):
    return jnp.dot(_mx(a), _mx(b), preferred_element_type=F32)


def _dot_nt(a, b):
    return lax.dot_general(_mx(a), _mx(b), (((1,), (1,)), ((), ())), preferred_element_type=F32)


def _dot_tn(a, b):
    return lax.dot_general(_mx(a), _mx(b), (((0,), (0,)), ((), ())), preferred_element_type=F32)


def _top16(x):
    u = lax.bitcast_convert_type(x, jnp.uint32) & jnp.uint32(0xFFFF0000)
    return lax.bitcast_convert_type(u, F32)


def _split3(x):
    hi = _top16(x)
    r1 = x - hi
    mid = _top16(r1)
    return _mx(hi), _mx(mid), _mx(r1 - mid)


def _dot_exact_lhs01(t, x):
    hi, mid, lo = _split3(x)
    t = _mx(t)
    return (jnp.dot(t, hi, preferred_element_type=F32) + jnp.dot(t, mid, preferred_element_type=F32)
            + jnp.dot(t, lo, preferred_element_type=F32))


def _sigmoid(x):
    return jax.nn.sigmoid(x)


def _silu(x):
    return x * _sigmoid(x)


def _dsilu(x):
    s = _sigmoid(x)
    return s * (1.0 + x * (1.0 - s))


def _log_sigmoid(x):
    return jnp.minimum(x, 0.0) - jnp.log(1.0 + jnp.exp(-jnp.abs(x)))


def _rms_fwd(x, g):
    r = lax.rsqrt(jnp.mean(x * x, axis=-1, keepdims=True) + EPS)
    return x * r * g


def _rms_bwd(x, g, dy):
    r = lax.rsqrt(jnp.mean(x * x, axis=-1, keepdims=True) + EPS)
    n = x * r
    t = dy * g
    dx = r * (t - n * jnp.mean(t * n, axis=-1, keepdims=True))
    return dx, dy * n


def _tri(n, upper=False):
    r = lax.broadcasted_iota(jnp.int32, (n, n), 0)
    c = lax.broadcasted_iota(jnp.int32, (n, n), 1)
    return jnp.where((c >= r) if upper else (r >= c), 1.0, 0.0).astype(F32)


def _hs(h):
    return slice(h * DH, (h + 1) * DH)


def _mm_nn(a, b, out_dtype, *, name, ts=None, tn=None, tk=None, b_transposed=False, comm=None):
    S, K = a.shape
    N = b.shape[0] if b_transposed else b.shape[1]
    ts = _tile(S, ts or ROW_TILE)
    tn = _tile(N, tn or N)
    tk = _tile(K, tk or K)
    nk = K // tk
    b_spec = (pl.BlockSpec((tn, tk), lambda j, i, k: (j, k)) if b_transposed
              else pl.BlockSpec((tk, tn), lambda j, i, k: (k, j)))

    def body(a_ref, b_ref, o_ref, acc_ref):
        k = pl.program_id(2)
        part = _dot_nt(a_ref[...], b_ref[...]) if b_transposed else jnp.dot(a_ref[...], b_ref[...],
                                                                          preferred_element_type=F32)
        if nk == 1:
            o_ref[...] = part.astype(o_ref.dtype)
        else:
            @pl.when(k == 0)
            def _():
                acc_ref[...] = part

            @pl.when(k > 0)
            def _():
                acc_ref[...] += part

            @pl.when(k == nk - 1)
            def _():
                o_ref[...] = acc_ref[...].astype(o_ref.dtype)

    grid = (N // tn, S // ts, nk)

    def when():
        at = [pl.program_id(d) for d in range(3)]
        first = (at[0] == 0) & (at[1] == 0) & (at[2] == 0)
        last = (at[0] == grid[0] - 1) & (at[1] == grid[1] - 1) & (at[2] == grid[2] - 1)
        return first, last, last

    xin, xout, xshape, xsem = _comm_specs(comm)
    res = _call(
        _fuse_comm(body, 2, 1, comm, when), name=name, grid=grid,
        in_specs=[pl.BlockSpec((ts, tk), lambda j, i, k: (i, k)), b_spec] + xin,
        out_specs=[pl.BlockSpec((ts, tn), lambda j, i, k: (i, j))] + xout,
        out_shape=[jax.ShapeDtypeStruct((S, N), out_dtype)] + xshape,
        scratch=[pltpu.VMEM((ts, tn) if nk > 1 else (8, 128), F32)] + xsem,
        sem=("arbitrary", "arbitrary", "arbitrary") if comm else ("parallel", "parallel", "arbitrary"),
        )(a, b, *(comm[1] if comm else []))
    return (res[0], list(res[1:])) if comm else res[0]


def _mm_tn(x, y, *, name, tn=None, ycol=0, n=None):
    S, M = x.shape
    n = n or y.shape[1]
    ts = _tile(S, 4 * ROW_TILE)
    tn = _tile(n, tn or n)
    nj = n // tn

    def body(x_ref, y_ref, o_ref):
        @pl.when(pl.program_id(1) == 0)
        def _():
            o_ref[...] = jnp.zeros_like(o_ref)

        o_ref[...] += lax.dot_general(x_ref[...], y_ref[...], (((0,), (0,)), ((), ())), preferred_element_type=F32)

    return _call(
        body, name=name, grid=(nj, S // ts),
        in_specs=[pl.BlockSpec((ts, M), lambda j, s: (s, 0)), pl.BlockSpec((ts, tn), lambda j, s: (s, ycol * nj + j))],
        out_specs=pl.BlockSpec((M, tn), lambda j, s: (0, j)),
        out_shape=jax.ShapeDtypeStruct((M, n), F32),
        sem=("parallel", "arbitrary"))(x, y)


def _norm_fwd(x, g):
    S = x.shape[0]
    ts = _tile(S, ROW_TILE)

    def body(x_ref, g_ref, h_ref):
        h_ref[...] = _mx(_rms_fwd(x_ref[...], g_ref[...]))

    return _call(
        body, name="norm_fwd", grid=(S // ts,),
        in_specs=[pl.BlockSpec((ts, D_MODEL), lambda i: (i, 0)), pl.BlockSpec((1, D_MODEL), lambda i: (0, 0))],
        out_specs=pl.BlockSpec((ts, D_MODEL), lambda i: (i, 0)),
        out_shape=jax.ShapeDtypeStruct((S, D_MODEL), MXU_DTYPE), sem=("parallel",))(x, g)


def _norm_bwd(x, g, dh, dres):
    S = x.shape[0]
    ts = _tile(S, ROW_TILE)

    def body(x_ref, g_ref, dh_ref, dr_ref, dx_ref, st_ref):
        @pl.when(pl.program_id(0) == 0)
        def _():
            st_ref[...] = jnp.zeros_like(st_ref)

        dx, dgr = _rms_bwd(x_ref[...], g_ref[...], dh_ref[...])
        dx_ref[...] = dr_ref[...] + dx
        st_ref[0:1, :] += jnp.sum(dgr, axis=0, keepdims=True)

    row = pl.BlockSpec((ts, D_MODEL), lambda i: (i, 0))
    return _call(
        body, name="norm_bwd", grid=(S // ts,),
        in_specs=[row, pl.BlockSpec((1, D_MODEL), lambda i: (0, 0)), row, row],
        out_specs=[row, pl.BlockSpec((8, D_MODEL), lambda i: (0, 0))],
        out_shape=[jax.ShapeDtypeStruct((S, D_MODEL), F32), jax.ShapeDtypeStruct((8, D_MODEL), F32)],
        sem=("arbitrary",))(x, g, dh, dres)


def _loss_fwd_bwd(y, target):
    S = y.shape[0]
    ts = _tile(S, ROW_TILE)

    def body(y_ref, t_ref, l_ref, dy_ref):
        @pl.when(pl.program_id(0) == 0)
        def _():
            l_ref[...] = jnp.zeros_like(l_ref)

        err = y_ref[...] - t_ref[...]
        dy_ref[...] = err * (1.0 / D_MODEL)
        rowloss = jnp.mean(err * err, axis=-1, keepdims=True)
        l_ref[...] += 0.5 * jnp.sum(rowloss, axis=0, keepdims=True)

    row = pl.BlockSpec((ts, D_MODEL), lambda i: (i, 0))
    return _call(
        body, name="loss", grid=(S // ts,), in_specs=[row, row],
        out_specs=[pl.BlockSpec((8, 128), lambda i: (0, 0)), row],
        out_shape=[jax.ShapeDtypeStruct((8, 128), F32), jax.ShapeDtypeStruct((S, D_MODEL), F32)],
        sem=("arbitrary",))(y, target)


def _shift_down(x, k, halo, rows):
    y = pltpu.roll(x, k, 0)
    for j in range(k):
        y = jnp.where(rows == j, halo[8 - k + j:8 - k + j + 1, :], y)
    return y


def _shift_up(x, k, halo, rows, n):
    y = pltpu.roll(x, n - k, 0)
    for j in range(k):
        y = jnp.where(rows == n - k + j, halo[j:j + 1, :], y)
    return y


def _conv_parts(za, zh, cw, first, rows):
    ax, ab, ac, ag = za[:, 0:BW], za[:, BW:2 * BW], za[:, 2 * BW:3 * BW], za[:, 3 * BW:4 * BW]
    zz = ac * ax
    hz = jnp.where(first, 0.0, zh[:, 2 * BW:3 * BW] * zh[:, 0:BW])
    zz1 = _shift_down(zz, 1, hz, rows)
    zz2 = _shift_down(zz, 2, hz, rows)
    conv = zz2 * cw[0:1, :] + zz1 * cw[1:2, :] + zz * cw[2:3, :] + cw[3:4, :]
    return ax, ab, ac, ag, zz, zz1, zz2, conv


def _conv_fwd(z, cw):
    S = z.shape[0]
    ts = _tile(S, ROW_TILE)
    hb = ts // 8

    def body(za_ref, zh_ref, cw_ref, y_ref):
        i = pl.program_id(0)
        rows = lax.broadcasted_iota(jnp.int32, (ts, BW), 0)
        ax, ab, ac, ag, zz, zz1, zz2, conv = _conv_parts(za_ref[...], zh_ref[...], cw_ref[...], i == 0, rows)
        y_ref[...] = _mx(ab * conv * _silu(ag))

    return _call(
        body, name="conv_fwd", grid=(S // ts,),
        in_specs=[pl.BlockSpec((ts, 1024), lambda i: (i, ZA // 1024)),
                  pl.BlockSpec((8, 1024), lambda i: (jnp.maximum(i * hb - 1, 0), ZA // 1024)),
                  pl.BlockSpec((8, BW), lambda i: (0, 0))],
        out_specs=pl.BlockSpec((ts, BW), lambda i: (i, 0)),
        out_shape=jax.ShapeDtypeStruct((S, BW), MXU_DTYPE), sem=("parallel",))(z, z, cw)


def _conv_bwd(z, cw, dy, dz):
    S = z.shape[0]
    ts = _tile(S, ROW_TILE)
    hb = ts // 8
    nt = S // ts

    def body(za_ref, zh_ref, zn_ref, cw_ref, dy_ref, dyn_ref, dz_in, dz_ref, st_ref):
        i = pl.program_id(0)

        @pl.when(i == 0)
        def _():
            st_ref[...] = jnp.zeros_like(st_ref)

        cw = cw_ref[...]
        rows = lax.broadcasted_iota(jnp.int32, (ts, BW), 0)
        ax, ab, ac, ag, zz, zz1, zz2, conv = _conv_parts(za_ref[...], zh_ref[...], cw, i == 0, rows)
        dy = dy_ref[...]
        sg = _silu(ag)
        dc = dy * ab * sg
        zn = zn_ref[...]
        dcn = jnp.where(i == nt - 1, 0.0, dyn_ref[...] * zn[:, BW:2 * BW] * _silu(zn[:, 3 * BW:4 * BW]))
        dc1 = _shift_up(dc, 1, dcn, rows, ts)
        dc2 = _shift_up(dc, 2, dcn, rows, ts)
        dzz = dc * cw[2:3, :] + dc1 * cw[1:2, :] + dc2 * cw[0:1, :]
        dz_ref[:, 0:BW] = _mx(dzz * ac)
        dz_ref[:, BW:2 * BW] = _mx(dy * conv * sg)
        dz_ref[:, 2 * BW:3 * BW] = _mx(dzz * ax)
        dz_ref[:, 3 * BW:4 * BW] = _mx(dy * ab * conv * _dsilu(ag))
        st_ref[0:1, :] += jnp.sum(dc * zz2, axis=0, keepdims=True)
        st_ref[1:2, :] += jnp.sum(dc * zz1, axis=0, keepdims=True)
        st_ref[2:3, :] += jnp.sum(dc * zz, axis=0, keepdims=True)
        st_ref[3:4, :] += jnp.sum(dc, axis=0, keepdims=True)

    return _call(
        body, name="conv_bwd", grid=(nt,),
        in_specs=[pl.BlockSpec((ts, 1024), lambda i: (i, ZA // 1024)),
                  pl.BlockSpec((8, 1024), lambda i: (jnp.maximum(i * hb - 1, 0), ZA // 1024)),
                  pl.BlockSpec((8, 1024), lambda i: (jnp.minimum((i + 1) * hb, S // 8 - 1), ZA // 1024)),
                  pl.BlockSpec((8, BW), lambda i: (0, 0)),
                  pl.BlockSpec((ts, BW), lambda i: (i, 0)),
                  pl.BlockSpec((8, BW), lambda i: (jnp.minimum((i + 1) * hb, S // 8 - 1), 0)), _ANY],
        out_specs=[pl.BlockSpec((ts, 1024), lambda i: (i, ZA // 1024)), pl.BlockSpec((8, BW), lambda i: (0, 0))],
        out_shape=[jax.ShapeDtypeStruct((S, ZCOLS), MXU_DTYPE), jax.ShapeDtypeStruct((8, BW), F32)],
        sem=("arbitrary",), aliases={6: 0})(z, z, z, cw, dy, dy, dz)


AW = 128
_AUG = DH


def _split3_f32(x):
    hi = _top16(x)
    r1 = x - hi
    mid = _top16(r1)
    return hi, mid, r1 - mid


def _aug_lanes(n, cols):
    lane = lax.broadcasted_iota(jnp.int32, (n, DH), 1)
    out = jnp.zeros((n, DH), F32)
    for e, c in enumerate(cols):
        out = jnp.where(lane == e, c, out)
    return out


def _put_aug(ref, h, col, parts=None, const=None):
    base = h * AW + _AUG + col
    if parts is None:
        ref[:, base:base + 3] = jnp.full((ref.shape[0], 3), const, ref.dtype)
    else:
        for e, part in enumerate(parts):
            ref[:, base + e:base + e + 1] = part.astype(ref.dtype)


def _attn_prep_fwd(z, fb, gq, gk):
    S = z.shape[0]
    ts = _tile(S, ROW_TILE)

    def body(zb_ref, zf_ref, fb_ref, gq_ref, gk_ref, qa_ref, ka_ref, va_ref, cc_ref, carry):
        @pl.when(pl.program_id(0) == 0)
        def _():
            carry[...] = jnp.zeros_like(carry)

        zb = zb_ref[...]
        gqv, gkv = gq_ref[...], gk_ref[...]
        lf = _log_sigmoid(zf_ref[...] + fb_ref[...])
        cum = _dot_exact_lhs01(_tri(ts), lf) + carry[...]
        carry[...] = cum[ts - 1:ts, :]
        cc_ref[...] = cum
        pieces = _split3_f32(cum)
        for h in range(NH):
            ph = [pc[:, h:h + 1] for pc in pieces]
            qh = _rms_fwd(zb[:, _hs(h)], gqv) * (DH ** -0.5)
            kh = _rms_fwd(zb[:, BW + h * DH:BW + (h + 1) * DH], gkv)
            vh = zb[:, 2 * BW + h * DH:2 * BW + (h + 1) * DH]
            qa_ref[:, _ah(h)] = _mx(jnp.concatenate([qh, _aug_lanes(ts, ph + [1.0] * 3)], axis=1))
            ka_ref[:, _ah(h)] = _mx(jnp.concatenate([kh, _aug_lanes(ts, [1.0] * 3 + [-x for x in ph])], axis=1))
            va_ref[:, _ah(h)] = _mx(jnp.concatenate([vh, _aug_lanes(ts, [-1.0] * 3)], axis=1))

    row = lambda w: pl.BlockSpec((ts, w), lambda i: (i, 0))
    return _call(
        body, name="attn_prep_fwd", grid=(S // ts,),
        in_specs=[pl.BlockSpec((ts, 1024), lambda i: (i, ZB // 1024)), pl.BlockSpec((ts, 128), lambda i: (i, ZF // 128)),
                  pl.BlockSpec((1, 128), lambda i: (0, 0)), pl.BlockSpec((1, DH), lambda i: (0, 0)),
                  pl.BlockSpec((1, DH), lambda i: (0, 0))],
        out_specs=[row(NH * AW), row(NH * AW), row(NH * AW), row(128)],
        out_shape=[jax.ShapeDtypeStruct((S, NH * AW), MXU_DTYPE)] * 3 + [jax.ShapeDtypeStruct((S, 128), F32)],
        scratch=[pltpu.VMEM((1, 128), F32)], sem=("arbitrary",))(z, z, fb, gq, gk)


ROW_CHUNK = 256


def _square_steps(n):
    def when():
        i, j = pl.program_id(0), pl.program_id(1)
        return (i == 0) & (j == 0), (i == n - 1) & (j == 0), (i == n - 1) & (j == n - 1)
    return when


def _fuse_comm(core, n_in, n_out, comm, when):
    nc = 0 if comm is None else len(comm[1])

    def body(*refs):
        cin, xin = refs[:n_in], refs[n_in:n_in + nc]
        a = n_in + nc
        cout, xout = refs[a:a + n_out], refs[a + n_out:a + n_out + nc]
        rest = refs[a + n_out + nc:]
        if nc == 0:
            core(*cin, *cout, *rest)
            return
        cscr, sems = rest[:-3], rest[-3:]
        first, middle, last = when()
        phase = _COMM_PHASES[comm[0]]

        @pl.when(first)
        def _():
            phase(0, xin, xout, sems)

        core(*cin, *cout, *cscr)

        @pl.when(middle)
        def _():
            phase(1, xin, xout, sems)

        @pl.when(last)
        def _():
            phase(2, xin, xout, sems)

    return body


def _comm_specs(comm):
    if comm is None:
        return [], [], [], []
    hbm = pl.BlockSpec(memory_space=pl.ANY)
    nc = len(comm[1])
    return [hbm] * nc, [hbm] * nc, _COMM_OUT[comm[0]](comm[1]), _comm_sems(comm[0], nc)


def _ah(h):
    return slice(h * AW, (h + 1) * AW)


def _ahd(h):
    return slice(h * AW, h * AW + DH)


def _causal(shape, row0, transposed=False):
    r = row0 + lax.broadcasted_iota(jnp.int32, shape, 0)
    c = lax.broadcasted_iota(jnp.int32, shape, 1)
    return (r <= c) if transposed else (r >= c)


def _flash_fwd(qa, ka, va, z, comm=None):
    S = qa.shape[0]
    t = _tile(S, FLASH_TILE)
    n = S // t
    rch = _tile(t, ROW_CHUNK)

    def core(q_ref, k_ref, v_ref, zb_ref, o_ref, lse_ref, y_ref, m_sc, acc):
        i, j = pl.program_id(0), pl.program_id(1)

        @pl.when(j == 0)
        def _():
            m_sc[...] = jnp.full_like(m_sc, MASK_VALUE)
            acc[...] = jnp.zeros_like(acc)

        def block(masked):
            for h in range(NH):
                for rc in range(t // rch):
                    rows = slice(rc * rch, (rc + 1) * rch)
                    s = _dot_nt(q_ref[rows, _ah(h)], k_ref[:, _ah(h)])
                    if masked:
                        s = jnp.where(_causal(s.shape, rc * rch), s, MASK_VALUE)
                    m_old = m_sc[h, rows, :]
                    m_new = jnp.maximum(m_old, jnp.max(s, axis=-1, keepdims=True))
                    p = jnp.exp(s - m_new)
                    alpha = jnp.exp(m_old - m_new)
                    acc[rows, _ah(h)] = alpha * acc[rows, _ah(h)] + _dot(p, v_ref[:, _ah(h)])
                    m_sc[h, rows, :] = m_new

        @pl.when(j < i)
        def _():
            block(False)

        @pl.when(j == i)
        def _():
            block(True)
            lse_ref[...] = jnp.zeros_like(lse_ref)
            for h in range(NH):
                l = -acc[:, h * AW + _AUG:h * AW + _AUG + 1]
                o_ref[:, _hs(h)] = acc[:, _ahd(h)] / l
                lse_ref[:, h:h + 1] = m_sc[h] + jnp.log(l)
            y_ref[...] = _mx(o_ref[...] * _silu(zb_ref[...]))

    qspec = lambda w: pl.BlockSpec((t, w), lambda i, j: (i, 0))
    kspec = lambda w: pl.BlockSpec((t, w), lambda i, j: (jnp.minimum(j, i), 0))
    xin, xout, xshape, xsem = _comm_specs(comm)
    res = _call(
        _fuse_comm(core, 4, 3, comm, _square_steps(n)), name="flash_fwd", grid=(n, n),
        in_specs=[qspec(NH * AW), kspec(NH * AW), kspec(NH * AW),
                  pl.BlockSpec((t, BW), lambda i, j: (i, ZB_GATE // BW))] + xin,
        out_specs=[qspec(BW), qspec(128), qspec(BW)] + xout,
        out_shape=[jax.ShapeDtypeStruct((S, BW), F32), jax.ShapeDtypeStruct((S, 128), F32),
                   jax.ShapeDtypeStruct((S, BW), MXU_DTYPE)] + xshape,
        scratch=[pltpu.VMEM((NH, t, 1), F32), pltpu.VMEM((t, NH * AW), F32)] + xsem,
        sem=("arbitrary", "arbitrary"))(qa, ka, va, z, *(comm[1] if comm else []))
    return res[0], res[1], res[2], list(res[3:])


def _attn_gate_bwd(dy, o, z, qa, cc, lse):
    S = dy.shape[0]
    ts = _tile(S, ROW_TILE)

    def body(dy_ref, o_ref, zb_ref, qa_ref, cc_ref, lse_ref, dg_ref, qb_ref, doa_ref):
        g = zb_ref[...]
        dy, o = dy_ref[...], o_ref[...]
        do = dy * _silu(g)
        dg_ref[...] = _mx(dy * o * _dsilu(g))
        qb_ref[...] = qa_ref[...]
        doa_ref[...] = jnp.zeros_like(doa_ref)
        shifted = _split3_f32(cc_ref[...] - lse_ref[...])
        for h in range(NH):
            doh = do[:, _hs(h)]
            doa_ref[:, _ahd(h)] = _mx(doh)
            delta = jnp.sum(doh * o[:, _hs(h)], axis=-1, keepdims=True)
            _put_aug(doa_ref, h, 0, parts=_split3_f32(delta))
            _put_aug(qb_ref, h, 0, parts=[pc[:, h:h + 1] for pc in shifted])

    row = lambda w: pl.BlockSpec((ts, w), lambda i: (i, 0))
    return _call(
        body, name="attn_gate_bwd", grid=(S // ts,),
        in_specs=[pl.BlockSpec((ts, BW), lambda i: (i, 1)), row(BW), pl.BlockSpec((ts, BW), lambda i: (i, ZB_GATE // BW)),
                  row(NH * AW), row(128), row(128)],
        out_specs=[row(BW), row(NH * AW), row(NH * AW)],
        out_shape=[jax.ShapeDtypeStruct((S, BW), MXU_DTYPE), jax.ShapeDtypeStruct((S, NH * AW), MXU_DTYPE),
                   jax.ShapeDtypeStruct((S, NH * AW), MXU_DTYPE)],
        sem=("parallel",))(dy, o, z, qa, cc, lse)


def _aug_value(ref, h, rows=slice(None)):
    base = h * AW + _AUG
    x = ref[rows, base:base + 3].astype(F32)
    return x[:, 0:1] + x[:, 1:2] + x[:, 2:3]


def _flash_bwd_dq(qb, ka, va, doa, comm=None):
    S = qb.shape[0]
    t = _tile(S, FLASH_TILE)
    n = S // t
    rch = _tile(t, ROW_CHUNK)

    def core(q_ref, k_ref, v_ref, do_ref, dq_ref, do2_ref, dr_sc):
        i, j = pl.program_id(0), pl.program_id(1)

        @pl.when(j == 0)
        def _():
            dq_ref[...] = jnp.zeros_like(dq_ref)
            dr_sc[...] = jnp.zeros_like(dr_sc)

        def block(masked):
            for h in range(NH):
                for rc in range(t // rch):
                    rows = slice(rc * rch, (rc + 1) * rch)
                    p = jnp.exp(_dot_nt(q_ref[rows, _ah(h)], k_ref[:, _ah(h)]))
                    if masked:
                        p = jnp.where(_causal(p.shape, rc * rch), p, 0.0)
                    ds = p * _dot_nt(do_ref[rows, _ah(h)], v_ref[:, _ah(h)])
                    dq_ref[rows, _hs(h)] += _dot(ds, k_ref[:, _ahd(h)])
                    dr_sc[h, rows, :] += jnp.sum(ds, axis=-1, keepdims=True)

        @pl.when(j < i)
        def _():
            block(False)

        @pl.when(j == i)
        def _():
            block(True)
            do2_ref[...] = do_ref[...]
            for h in range(NH):
                _put_aug(do2_ref, h, 0, parts=_split3_f32(_aug_value(do_ref, h) + dr_sc[h]))

    qspec = lambda w: pl.BlockSpec((t, w), lambda i, j: (i, 0))
    kspec = lambda w: pl.BlockSpec((t, w), lambda i, j: (jnp.minimum(j, i), 0))
    xin, xout, xshape, xsem = _comm_specs(comm)
    res = _call(
        _fuse_comm(core, 4, 2, comm, _square_steps(n)), name="flash_bwd_dq", grid=(n, n),
        in_specs=[qspec(NH * AW), kspec(NH * AW), kspec(NH * AW), qspec(NH * AW)] + xin,
        out_specs=[qspec(BW), qspec(NH * AW)] + xout,
        out_shape=[jax.ShapeDtypeStruct((S, BW), F32), jax.ShapeDtypeStruct((S, NH * AW), MXU_DTYPE)] + xshape,
        scratch=[pltpu.VMEM((NH, t, 1), F32)] + xsem,
        sem=("arbitrary", "arbitrary"))(qb, ka, va, doa, *(comm[1] if comm else []))
    return res[0], res[1], list(res[2:])


def _flash_bwd_dkv(qb, ka, va, doa):
    S = qb.shape[0]
    t = _tile(S, FLASH_TILE)
    n = S // t

    def body(q_ref, k_ref, v_ref, do_ref, dk_ref, dv_ref, dc_ref):
        j, i = pl.program_id(0), pl.program_id(1)

        @pl.when(i == 0)
        def _():
            dk_ref[...] = jnp.zeros_like(dk_ref)
            dv_ref[...] = jnp.zeros_like(dv_ref)
            dc_ref[...] = jnp.zeros_like(dc_ref)

        def block(masked):
            for h in range(NH):
                pt = jnp.exp(_dot_nt(k_ref[:, _ah(h)], q_ref[:, _ah(h)]))
                if masked:
                    pt = jnp.where(_causal(pt.shape, 0, transposed=True), pt, 0.0)
                dst = pt * _dot_nt(v_ref[:, _ah(h)], do_ref[:, _ah(h)])
                dv_ref[:, _hs(h)] += _dot(pt, do_ref[:, _ahd(h)])
                dk_ref[:, _hs(h)] += _dot(dst, q_ref[:, _ahd(h)])
                dc_ref[:, h:h + 1] += -jnp.sum(dst, axis=-1, keepdims=True)

        @pl.when(i > j)
        def _():
            block(False)

        @pl.when(i == j)
        def _():
            block(True)

    qspec = lambda w: pl.BlockSpec((t, w), lambda j, i: (jnp.maximum(i, j), 0))
    kspec = lambda w: pl.BlockSpec((t, w), lambda j, i: (j, 0))
    return _call(
        body, name="flash_bwd_dkv", grid=(n, n),
        in_specs=[qspec(NH * AW), kspec(NH * AW), kspec(NH * AW), qspec(NH * AW)],
        out_specs=[kspec(BW), kspec(BW), kspec(128)],
        out_shape=[jax.ShapeDtypeStruct((S, BW), F32), jax.ShapeDtypeStruct((S, BW), F32),
                   jax.ShapeDtypeStruct((S, 128), F32)],
        sem=("parallel", "arbitrary"))(qb, ka, va, doa)


def _attn_prep_bwd(z, fb, gq, gk, dq, dk, dv, dgate, dcc, dz):
    S = z.shape[0]
    ts = _tile(S, ROW_TILE)
    nt = S // ts

    def body(zb_ref, zf_ref, fb_ref, gq_ref, gk_ref, dq_ref, dk_ref, dv_ref, dg_ref, dcc_ref, dz_in, dzb_ref, dzf_ref,
             st_ref, carry):
        @pl.when(pl.program_id(0) == 0)
        def _():
            carry[...] = jnp.zeros_like(carry)
            st_ref[...] = jnp.zeros_like(st_ref)

        zb = zb_ref[...]
        gqv, gkv = gq_ref[...], gk_ref[...]
        dqv, dkv = dq_ref[...], dk_ref[...]
        sq = jnp.zeros((1, DH), F32)
        sk = jnp.zeros((1, DH), F32)
        for h in range(NH):
            dx, dgr = _rms_bwd(zb[:, _hs(h)], gqv, dqv[:, _hs(h)] * (DH ** -0.5))
            dzb_ref[:, _hs(h)] = _mx(dx)
            sq = sq + jnp.sum(dgr, axis=0, keepdims=True)
            ks = slice(BW + h * DH, BW + (h + 1) * DH)
            dx, dgr = _rms_bwd(zb[:, ks], gkv, dkv[:, _hs(h)])
            dzb_ref[:, ks] = _mx(dx)
            sk = sk + jnp.sum(dgr, axis=0, keepdims=True)
        dzb_ref[:, 2 * BW:3 * BW] = _mx(dv_ref[...])
        dzb_ref[:, 3 * BW:4 * BW] = dg_ref[...]
        dc = dcc_ref[...]
        dlf = _dot_exact_lhs01(_tri(ts, upper=True), dc) + carry[...]
        carry[...] = dlf[0:1, :]
        dfz = dlf * _sigmoid(-(zf_ref[...] + fb_ref[...]))
        dzf_ref[...] = _mx(dfz)
        st_ref[0:1, 0:DH] += sq
        st_ref[1:2, 0:DH] += sk
        st_ref[2:3, :] += jnp.sum(dfz, axis=0, keepdims=True)

    rev = lambda w, c=0: pl.BlockSpec((ts, w), lambda i: (nt - 1 - i, c))
    one = lambda w: pl.BlockSpec((1, w), lambda i: (0, 0))
    return _call(
        body, name="attn_prep_bwd", grid=(nt,),
        in_specs=[rev(1024, ZB // 1024), rev(128, ZF // 128), one(128), one(DH), one(DH),
                  rev(BW), rev(BW), rev(BW), rev(BW), rev(128), _ANY],
        out_specs=[rev(1024, ZB // 1024), rev(128), pl.BlockSpec((8, 128), lambda i: (0, 0))],
        out_shape=[jax.ShapeDtypeStruct((S, ZCOLS), MXU_DTYPE), jax.ShapeDtypeStruct((S, 128), MXU_DTYPE),
                   jax.ShapeDtypeStruct((8, 128), F32)],
        scratch=[pltpu.VMEM((1, 128), F32)], sem=("arbitrary",), aliases={10: 0})(z, z, fb, gq, gk, dq, dk, dv, dgate, dcc, dz)


def _put_fgate_cols(dzf, dz):
    S = dzf.shape[0]
    ts = _tile(S, ROW_TILE)

    def body(f_ref, dz_in, o_ref):
        o_ref[...] = f_ref[...]

    return _call(
        body, name="put_fgate_cols", grid=(S // ts,),
        in_specs=[pl.BlockSpec((ts, 128), lambda i: (i, 0)), _ANY],
        out_specs=pl.BlockSpec((ts, 128), lambda i: (i, ZF // 128)),
        out_shape=jax.ShapeDtypeStruct((S, ZCOLS), MXU_DTYPE), sem=("parallel",), aliases={1: 0})(dzf, dz)


def _hgrn_consts():
    C = CHUNK
    t = np.arange(C)[:, None]
    j = np.arange(C)[None, :]
    masks = []
    for lev in range(NLEV):
        m = C >> (lev + 1)
        blk, pos = t // (2 * m), t % (2 * m)
        sblk, spos = j // (2 * m), j % (2 * m)
        masks.append((blk == sblk) & (pos >= m) & (spos < m))
    masks.append(t == j)
    return (j <= t).astype(np.float32), np.stack(masks).astype(np.float32)


def _level_exponents(b, lg):
    row = lax.broadcasted_iota(jnp.int32, (CHUNK, 1), 0)
    eqs, eks = [], []
    for lev in range(NLEV):
        m = CHUNK >> (lev + 1)
        pos = jnp.bitwise_and(row, 2 * m - 1)
        if 2 * m >= 8:
            b3 = b.reshape(CHUNK // (2 * m), 2 * m, b.shape[1])
            mid = jnp.broadcast_to(b3[:, m - 1:m, :], b3.shape).reshape(b.shape)
            eqs.append(jnp.where(pos >= m, b - mid, 0.0))
            eks.append(jnp.where(pos < m, mid - b, 0.0))
        elif m == 2:
            eqs.append(jnp.where(pos == 2, lg, jnp.where(pos == 3, lg + pltpu.roll(lg, 1, 0), 0.0)))
            eks.append(jnp.where(pos == 0, pltpu.roll(lg, CHUNK - 1, 0), 0.0))
        else:
            eqs.append(jnp.where(pos == 1, lg, 0.0))
            eks.append(jnp.zeros_like(lg))
    return jnp.concatenate([b] + eqs + eks, axis=0)


def _hgrn_chunk_fwd(zc, lb, tmat):
    cq, cf, ci = zc[:, 0:BW], zc[:, BW:2 * BW], zc[:, 2 * BW:3 * BW]
    q = _silu(cq)
    sg = _sigmoid(cf)
    g = lb + (1.0 - lb) * sg
    lg = jnp.log(g)
    kf = (1.0 - lb) * _sigmoid(-cf)
    b = _dot_exact_lhs01(tmat, lg)
    e = _level_exponents(b, lg)
    blast = b[CHUNK - 1:CHUNK, :]
    return cq, cf, q, sg, g, kf, ci, e, b, blast


def _round_mx(x):
    if MXU_DTYPE != jnp.bfloat16:
        return x
    u = lax.bitcast_convert_type(x, jnp.uint32)
    u = (u + jnp.uint32(0x7FFF) + ((u >> 16) & jnp.uint32(1))) & jnp.uint32(0xFFFF0000)
    return lax.bitcast_convert_type(u, F32)


def _hgrn_scores(q, kf, e, masks, h, scores=True):
    qh, kh = q[:, _hs(h)], kf[:, _hs(h)]
    ql, kl = [], []
    a = None
    for lev in range(NLEV + 1):
        if lev < NLEV:
            eq = jnp.exp(e[(1 + lev) * CHUNK:(2 + lev) * CHUNK, _hs(h)])
            ek = jnp.exp(e[(1 + NLEV + lev) * CHUNK:(2 + NLEV + lev) * CHUNK, _hs(h)])
            ql.append((_round_mx(qh * eq), eq))
            kl.append((_round_mx(kh * ek), ek))
        else:
            ql.append((_round_mx(qh), None))
            kl.append((_round_mx(kh), None))
        if scores:
            term = masks[lev] * _dot_nt(ql[-1][0], kl[-1][0])
            a = term if a is None else a + term
    return a, ql, kl


def _hgrn_fwd(z, lb, gain, tmat, masks):
    S = z.shape[0]
    ts = _tile(S, ROW_TILE)
    nc = ts // CHUNK

    def body(zc_ref, lb_ref, gn_ref, tm_ref, mk_ref, o_ref, st_ref, a_ref, y_ref, state):
        @pl.when(pl.program_id(0) == 0)
        def _():
            state[...] = jnp.zeros_like(state)

        lbv, gn, tm, mk = lb_ref[...], gn_ref[...], tm_ref[...], mk_ref[...]

        def chunk(c, carry):
            r0 = pl.multiple_of(c * CHUNK, CHUNK)
            zc = zc_ref[pl.ds(r0, CHUNK), :]
            cq, cf, q, sg, g, kf, v, e, b, blast = _hgrn_chunk_fwd(zc, lbv, tm)
            qe = q * jnp.exp(b)
            kd = kf * jnp.exp(blast - b)
            st_ref[pl.ds(r0, CHUNK), :] = state[...]
            for h in range(NH):
                sth = state[:, _hs(h)]
                a = _mx(_hgrn_scores(q, kf, e, mk, h)[0])
                a_ref[pl.ds(r0, CHUNK), _hs(h)] = a
                oh = _dot_nt(qe[:, _hs(h)], sth) + _dot(a, v[:, _hs(h)])
                state[:, _hs(h)] = sth * jnp.exp(blast[:, _hs(h)]) + _dot_tn(v[:, _hs(h)], kd[:, _hs(h)])
                o_ref[pl.ds(r0, CHUNK), _hs(h)] = oh
                yn = _rms_fwd(oh, gn[:, _hs(h)])
                y_ref[pl.ds(r0, CHUNK), _hs(h)] = _mx(yn * _silu(zc[:, 3 * BW + h * DH:3 * BW + (h + 1) * DH]))
            return carry

        lax.fori_loop(0, nc, chunk, 0, unroll=min(4, nc))

    row = pl.BlockSpec((ts, BW), lambda i: (i, 0))
    one = pl.BlockSpec((1, BW), lambda i: (0, 0))
    return _call(
        body, name="hgrn_fwd", grid=(S // ts,),
        in_specs=[pl.BlockSpec((ts, 1024), lambda i: (i, ZC // 1024)), one, one,
                  pl.BlockSpec(tmat.shape, lambda i: (0, 0)), pl.BlockSpec(masks.shape, lambda i: (0, 0, 0))],
        out_specs=[row, row, row, row],
        out_shape=[jax.ShapeDtypeStruct((S, BW), F32), jax.ShapeDtypeStruct((S, BW), F32),
                   jax.ShapeDtypeStruct((S, BW), MXU_DTYPE), jax.ShapeDtypeStruct((S, BW), MXU_DTYPE)],
        scratch=[pltpu.VMEM((CHUNK, BW), F32)], sem=("arbitrary",))(z, lb, gain, tmat, masks)


def _hgrn_bwd(z, lb, gain, tmat, masks, o_pre, states, scores, dy, dz):
    S = z.shape[0]
    ts = _tile(S, ROW_TILE)
    nt = S // ts
    nc = ts // CHUNK

    def body(zc_ref, lb_ref, gn_ref, tm_ref, mk_ref, o_ref, st_ref, a_ref, dy_ref, dz_in, dz_ref, stat_ref, dstate):
        @pl.when(pl.program_id(0) == 0)
        def _():
            dstate[...] = jnp.zeros_like(dstate)
            stat_ref[...] = jnp.zeros_like(stat_ref)

        lbv, gn, tm, mk = lb_ref[...], gn_ref[...], tm_ref[...], mk_ref[...]
        upper = _tri(CHUNK, upper=True)
        lower_strict = 1.0 - upper

        def chunk(cc, carry):
            c = nc - 1 - cc
            r0 = pl.multiple_of(c * CHUNK, CHUNK)
            zc = zc_ref[pl.ds(r0, CHUNK), :]
            cq, cf, q, sg, g, kf, v, e, b, blast = _hgrn_chunk_fwd(zc, lbv, tm)
            eb = jnp.exp(b)
            ebl = jnp.exp(blast - b)
            qe = q * eb
            kd = kf * ebl
            o = o_ref[pl.ds(r0, CHUNK), :]
            dyv = dy_ref[pl.ds(r0, CHUNK), :]
            stp = st_ref[pl.ds(r0, CHUNK), :]
            cg = zc[:, 3 * BW:4 * BW]
            sgate = _silu(cg)
            dq_parts, dk_parts, dv_parts, dcg_parts = [], [], [], []
            dgain, up_parts, lo_parts, const_parts = [], [], [], []
            for h in range(NH):
                hs = _hs(h)
                oh = o[:, hs]
                r = lax.rsqrt(jnp.mean(oh * oh, axis=-1, keepdims=True) + EPS)
                nrm = oh * r
                dyn = dyv[:, hs] * sgate[:, hs]
                dcg_parts.append(dyv[:, hs] * nrm * gn[:, hs] * _dsilu(cg[:, hs]))
                dgain.append(jnp.sum(dyn * nrm, axis=0, keepdims=True))
                tt = dyn * gn[:, hs]
                doh = r * (tt - nrm * jnp.mean(tt * nrm, axis=-1, keepdims=True))
                _, ql, kl = _hgrn_scores(q, kf, e, mk, h, scores=False)
                a = a_ref[pl.ds(r0, CHUNK), hs]
                da = _dot_nt(doh, v[:, hs])
                dsth = dstate[:, hs]
                ebh = jnp.exp(blast[:, hs])
                dv_parts.append(_dot_tn(a, doh) + _dot_nt(kd[:, hs], dsth))
                dq_inter = eb[:, hs] * _dot(doh, stp[:, hs])
                dk_state = ebl[:, hs] * _dot(v[:, hs], dsth)
                dqh, dkh, gh = dq_inter, dk_state, None
                for lev in range(NLEV + 1):
                    dal = mk[lev] * da
                    xq = _dot(dal, kl[lev][0])
                    yk = _dot_tn(dal, ql[lev][0])
                    gterm = ql[lev][0] * xq - kl[lev][0] * yk
                    gh = gterm if gh is None else gh + gterm
                    dqh = dqh + (xq if lev == NLEV else ql[lev][1] * xq)
                    dkh = dkh + (yk if lev == NLEV else kl[lev][1] * yk)
                up_parts.append(gh + q[:, hs] * dq_inter)
                lo_parts.append(kf[:, hs] * dk_state)
                const_parts.append(jnp.sum(dsth * stp[:, hs], axis=0, keepdims=True) * ebh)
                dstate[:, hs] = dsth * ebh + _dot_tn(doh, qe[:, hs])
                dq_parts.append(dqh)
                dk_parts.append(dkh)
            dq = jnp.concatenate(dq_parts, axis=1)
            dk = jnp.concatenate(dk_parts, axis=1)
            dlg = (_dot_exact_lhs01(upper, jnp.concatenate(up_parts, axis=1))
                   + _dot_exact_lhs01(lower_strict, jnp.concatenate(lo_parts, axis=1))
                   + jnp.concatenate(const_parts, axis=1))
            dsg = sg * (1.0 - sg)
            dz_ref[pl.ds(r0, CHUNK), 0:BW] = _mx(dq * _dsilu(cq))
            dz_ref[pl.ds(r0, CHUNK), BW:2 * BW] = _mx((dlg / g - dk) * (1.0 - lbv) * dsg)
            dz_ref[pl.ds(r0, CHUNK), 2 * BW:3 * BW] = _mx(jnp.concatenate(dv_parts, axis=1))
            dz_ref[pl.ds(r0, CHUNK), 3 * BW:4 * BW] = _mx(jnp.concatenate(dcg_parts, axis=1))
            stat_ref[0:1, :] += jnp.concatenate(dgain, axis=1)
            stat_ref[1:2, :] += jnp.sum((dlg / g - dk) * (1.0 - sg), axis=0, keepdims=True)
            return carry

        lax.fori_loop(0, nc, chunk, 0, unroll=min(4, nc))

    rev = lambda w, c=0: pl.BlockSpec((ts, w), lambda i: (nt - 1 - i, c))
    one = pl.BlockSpec((1, BW), lambda i: (0, 0))
    return _call(
        body, name="hgrn_bwd", grid=(nt,),
        in_specs=[rev(1024, ZC // 1024), one, one, pl.BlockSpec(tmat.shape, lambda i: (0, 0)),
                  pl.BlockSpec(masks.shape, lambda i: (0, 0, 0)), rev(BW), rev(BW), rev(BW), rev(BW, 2), _ANY],
        out_specs=[rev(1024, ZC // 1024), pl.BlockSpec((8, BW), lambda i: (0, 0))],
        out_shape=[jax.ShapeDtypeStruct((S, ZCOLS), MXU_DTYPE), jax.ShapeDtypeStruct((8, BW), F32)],
        scratch=[pltpu.VMEM((CHUNK, BW), F32)],
        sem=("arbitrary",), aliases={9: 0})(z, lb, gain, tmat, masks, o_pre, states, scores, dy, dz)


def _lower_bounds_fwd(lb_logits):
    def body(l_ref, o_ref):
        l = l_ref[...]
        m = jnp.max(l, axis=0, keepdims=True)
        ex = jnp.exp(l - m)
        p = ex / jnp.sum(ex, axis=0, keepdims=True)
        cs = p[0:1, :]
        o_ref[0:1, :] = jnp.clip(cs - p[0:1, :], 0.0, 1.0)
        for d in range(1, DEPTH):
            cs = cs + p[d:d + 1, :]
            o_ref[d:d + 1, :] = jnp.clip(cs - p[0:1, :], 0.0, 1.0)

    full = pl.BlockSpec((DEPTH, BW), lambda: (0, 0))
    return _call(body, name="lower_bounds_fwd", grid=(), in_specs=[full], out_specs=full,
                 out_shape=jax.ShapeDtypeStruct((DEPTH, BW), F32))(lb_logits)


def _lower_bounds_bwd(lb_logits, dlow):
    def body(l_ref, d_ref, o_ref):
        l = l_ref[...]
        m = jnp.max(l, axis=0, keepdims=True)
        ex = jnp.exp(l - m)
        p = ex / jnp.sum(ex, axis=0, keepdims=True)
        dl = d_ref[...]
        cs = p[0:1, :]
        dcs = []
        for d in range(DEPTH):
            if d > 0:
                cs = cs + p[d:d + 1, :]
            val = cs - p[0:1, :]
            dcs.append(jnp.where((val > 0.0) & (val < 1.0), dl[d:d + 1, :], 0.0))
        total = dcs[0] + dcs[1] + dcs[2] + dcs[3]
        dp = []
        for j in range(DEPTH):
            s = dcs[j]
            for d in range(j + 1, DEPTH):
                s = s + dcs[d]
            dp.append(s - total if j == 0 else s)
        inner = p[0:1, :] * dp[0]
        for j in range(1, DEPTH):
            inner = inner + p[j:j + 1, :] * dp[j]
        for j in range(DEPTH):
            o_ref[j:j + 1, :] = p[j:j + 1, :] * (dp[j] - inner)

    full = pl.BlockSpec((DEPTH, BW), lambda: (0, 0))
    return _call(body, name="lower_bounds_bwd", grid=(), in_specs=[full, full], out_specs=full,
                 out_shape=jax.ShapeDtypeStruct((DEPTH, BW), F32))(lb_logits, dlow)


def _sgu_fwd(z, gv, ws, bs):
    S = z.shape[0]
    ts = _tile(S, ROW_TILE)
    nc = ts // SCHUNK

    def body(zd_ref, gv_ref, ws_ref, bs_ref, y_ref):
        gvv, bsv = gv_ref[...], bs_ref[...]
        tril = _tri(SCHUNK)
        for c in range(nc):
            rs = slice(c * SCHUNK, (c + 1) * SCHUNK)
            zd = zd_ref[rs, :]
            for h in range(NH):
                vn = _rms_fwd(zd[:, BW + h * DH:BW + (h + 1) * DH], gvv[:, _hs(h)])
                s = _dot(ws_ref[h] * tril, vn) + bsv[:, h:h + 1]
                y_ref[rs, _hs(h)] = _mx(zd[:, _hs(h)] * s * _silu(zd[:, 2 * BW + h * DH:2 * BW + (h + 1) * DH]))

    return _call(
        body, name="sgu_fwd", grid=(S // ts,),
        in_specs=[pl.BlockSpec((ts, 1024), lambda i: (i, ZD // 1024)), pl.BlockSpec((1, BW), lambda i: (0, 0)),
                  pl.BlockSpec((NH, SCHUNK, SCHUNK), lambda i: (0, 0, 0)), pl.BlockSpec((SCHUNK, 128), lambda i: (0, 0))],
        out_specs=pl.BlockSpec((ts, BW), lambda i: (i, 0)),
        out_shape=jax.ShapeDtypeStruct((S, BW), MXU_DTYPE), sem=("parallel",))(z, gv, ws, bs)


def _sgu_bwd(z, gv, ws, bs, dy, dz):
    S = z.shape[0]
    ts = _tile(S, ROW_TILE)
    nc = ts // SCHUNK

    def body(zd_ref, gv_ref, ws_ref, bs_ref, dy_ref, dz_in, dz_ref, dws_ref, dbs_ref, st_ref):
        @pl.when(pl.program_id(0) == 0)
        def _():
            dws_ref[...] = jnp.zeros_like(dws_ref)
            dbs_ref[...] = jnp.zeros_like(dbs_ref)
            st_ref[...] = jnp.zeros_like(st_ref)

        gvv, bsv = gv_ref[...], bs_ref[...]
        tril = _tri(SCHUNK)
        dz_ref[:, 3 * BW:4 * BW] = jnp.zeros((ts, BW), MXU_DTYPE)
        for c in range(nc):
            rs = slice(c * SCHUNK, (c + 1) * SCHUNK)
            zd = zd_ref[rs, :]
            dyv = dy_ref[rs, :]
            for h in range(NH):
                hs = _hs(h)
                u = zd[:, hs]
                vraw = zd[:, BW + h * DH:BW + (h + 1) * DH]
                gt = zd[:, 2 * BW + h * DH:2 * BW + (h + 1) * DH]
                gvh = gvv[:, hs]
                vn = _rms_fwd(vraw, gvh)
                wm = ws_ref[h] * tril
                s = _dot(wm, vn) + bsv[:, h:h + 1]
                sil = _silu(gt)
                d = dyv[:, hs]
                ds = d * u * sil
                dz_ref[rs, hs] = _mx(d * s * sil)
                dz_ref[rs, 2 * BW + h * DH:2 * BW + (h + 1) * DH] = _mx(d * u * s * _dsilu(gt))
                dws_ref[h] += tril * _dot_nt(ds, vn)
                dbs_ref[:, h:h + 1] += jnp.sum(ds, axis=-1, keepdims=True)
                dvn = _dot_tn(wm, ds)
                dx, dgr = _rms_bwd(vraw, gvh, dvn)
                dz_ref[rs, BW + h * DH:BW + (h + 1) * DH] = _mx(dx)
                st_ref[0:1, hs] += jnp.sum(dgr, axis=0, keepdims=True)

    return _call(
        body, name="sgu_bwd", grid=(S // ts,),
        in_specs=[pl.BlockSpec((ts, 1024), lambda i: (i, ZD // 1024)), pl.BlockSpec((1, BW), lambda i: (0, 0)),
                  pl.BlockSpec((NH, SCHUNK, SCHUNK), lambda i: (0, 0, 0)), pl.BlockSpec((SCHUNK, 128), lambda i: (0, 0)),
                  pl.BlockSpec((ts, BW), lambda i: (i, 3)), _ANY],
        out_specs=[pl.BlockSpec((ts, 1024), lambda i: (i, ZD // 1024)), pl.BlockSpec((NH, SCHUNK, SCHUNK), lambda i: (0, 0, 0)),
                   pl.BlockSpec((SCHUNK, 128), lambda i: (0, 0)), pl.BlockSpec((8, BW), lambda i: (0, 0))],
        out_shape=[jax.ShapeDtypeStruct((S, ZCOLS), MXU_DTYPE), jax.ShapeDtypeStruct((NH, SCHUNK, SCHUNK), F32),
                   jax.ShapeDtypeStruct((SCHUNK, 128), F32), jax.ShapeDtypeStruct((8, BW), F32)],
        sem=("arbitrary",), aliases={5: 0})(z, gv, ws, bs, dy, dz)


def _merge_fwd(x, z, ys, wup, mb, wo):
    S = x.shape[0]
    ts = _tile(S, MERGE_TILE)

    def body(x_ref, zm_ref, ya_ref, yb_ref, yc_ref, yd_ref, wup_ref, mb_ref, wo_ref, x1_ref, mg_ref):
        yrefs = (ya_ref, yb_ref, yc_ref, yd_ref)
        mbv = mb_ref[...]
        merged = None
        for b in range(NBR):
            cs = slice(b * D_MODEL, (b + 1) * D_MODEL)
            term = _sigmoid(zm_ref[:, cs] + mbv[b:b + 1, :]) * jnp.dot(yrefs[b][...], wup_ref[b],
                                                                      preferred_element_type=F32)
            merged = term if merged is None else merged + term
        mg = _mx(merged)
        mg_ref[...] = mg
        x1_ref[...] = x_ref[...] + jnp.dot(mg, wo_ref[...], preferred_element_type=F32)

    row = lambda w: pl.BlockSpec((ts, w), lambda i: (i, 0))
    return _call(
        body, name="merge_fwd", grid=(S // ts,),
        in_specs=[row(D_MODEL), pl.BlockSpec((ts, 4096), lambda i: (i, 0)), row(BW), row(BW), row(BW), row(BW),
                  pl.BlockSpec((NBR, BW, D_MODEL), lambda i: (0, 0, 0)), pl.BlockSpec((NBR, D_MODEL), lambda i: (0, 0)),
                  pl.BlockSpec((D_MODEL, D_MODEL), lambda i: (0, 0))],
        out_specs=[row(D_MODEL), row(D_MODEL)],
        out_shape=[jax.ShapeDtypeStruct((S, D_MODEL), F32), jax.ShapeDtypeStruct((S, D_MODEL), MXU_DTYPE)],
        sem=("parallel",))(x, z, *ys, wup, mb, wo)


def _merge_bwd(dx1, z, ys, wup, mb, wo):
    S = dx1.shape[0]
    ts = _tile(S, MERGE_TILE)

    def body(dx_ref, zm_ref, ya_ref, yb_ref, yc_ref, yd_ref, wup_ref, mb_ref, wo_ref,
             dzm_ref, du_ref, dxb_ref, dy_ref, st_ref):
        @pl.when(pl.program_id(0) == 0)
        def _():
            st_ref[...] = jnp.zeros_like(st_ref)

        yrefs = (ya_ref, yb_ref, yc_ref, yd_ref)
        mbv = mb_ref[...]
        dxb = _mx(dx_ref[...])
        dxb_ref[...] = dxb
        dmerged = _dot_nt(dxb, wo_ref[...])
        for b in range(NBR):
            cs = slice(b * D_MODEL, (b + 1) * D_MODEL)
            u = jnp.dot(yrefs[b][...], wup_ref[b], preferred_element_type=F32)
            sg = _sigmoid(zm_ref[:, cs] + mbv[b:b + 1, :])
            du = _mx(dmerged * sg)
            du_ref[:, cs] = du
            dzm = dmerged * u * sg * (1.0 - sg)
            dzm_ref[:, cs] = _mx(dzm)
            st_ref[b:b + 1, :] += jnp.sum(dzm, axis=0, keepdims=True)
            dy_ref[:, b * BW:(b + 1) * BW] = _dot_nt(du, wup_ref[b])

    row = lambda w: pl.BlockSpec((ts, w), lambda i: (i, 0))
    return _call(
        body, name="merge_bwd", grid=(S // ts,),
        in_specs=[row(D_MODEL), pl.BlockSpec((ts, 4096), lambda i: (i, 0)), row(BW), row(BW), row(BW), row(BW),
                  pl.BlockSpec((NBR, BW, D_MODEL), lambda i: (0, 0, 0)),
                  pl.BlockSpec((NBR, D_MODEL), lambda i: (0, 0)), pl.BlockSpec((D_MODEL, D_MODEL), lambda i: (0, 0))],
        out_specs=[row(4096), row(4096), row(D_MODEL), row(D_MODEL), pl.BlockSpec((8, D_MODEL), lambda i: (0, 0))],
        out_shape=[jax.ShapeDtypeStruct((S, ZCOLS), MXU_DTYPE), jax.ShapeDtypeStruct((S, 4096), MXU_DTYPE),
                   jax.ShapeDtypeStruct((S, D_MODEL), MXU_DTYPE), jax.ShapeDtypeStruct((S, D_MODEL), F32),
                   jax.ShapeDtypeStruct((8, D_MODEL), F32)],
        sem=("arbitrary",))(dx1, z, *ys, wup, mb, wo)


def _ple_fwd(x1, p, g, wg, wp):
    S = x1.shape[0]
    ts = _tile(S, ROW_TILE)

    def body(x_ref, p_ref, g_ref, wg_ref, wp_ref, o_ref):
        x = x_ref[...]
        hp = _mx(_rms_fwd(x, g_ref[...]))
        gate = _sigmoid(jnp.dot(hp, wg_ref[...], preferred_element_type=F32))
        pp = jnp.dot(_mx(p_ref[...]), wp_ref[...], preferred_element_type=F32)
        o_ref[...] = x + gate * pp

    row = lambda w: pl.BlockSpec((ts, w), lambda i: (i, 0))
    return _call(
        body, name="ple_fwd", grid=(S // ts,),
        in_specs=[row(D_MODEL), row(PLE), pl.BlockSpec((1, D_MODEL), lambda i: (0, 0)),
                  pl.BlockSpec((D_MODEL, D_MODEL), lambda i: (0, 0)), pl.BlockSpec((PLE, D_MODEL), lambda i: (0, 0))],
        out_specs=row(D_MODEL), out_shape=jax.ShapeDtypeStruct((S, D_MODEL), F32),
        sem=("parallel",))(x1, p, g, wg, wp)


def _ple_bwd(x1, p, dx2, g, wg, wp):
    S = x1.shape[0]
    ts = _tile(S, ROW_TILE)

    def body(x_ref, p_ref, dx_ref, g_ref, wg_ref, wp_ref, dx1_ref, hp_ref, dgl_ref, dpp_ref, pb_ref, st_ref):
        @pl.when(pl.program_id(0) == 0)
        def _():
            st_ref[...] = jnp.zeros_like(st_ref)

        x, gv, dx2 = x_ref[...], g_ref[...], dx_ref[...]
        hp = _mx(_rms_fwd(x, gv))
        hp_ref[...] = hp
        gate = _sigmoid(jnp.dot(hp, wg_ref[...], preferred_element_type=F32))
        pb = _mx(p_ref[...])
        pb_ref[...] = pb
        pp = jnp.dot(pb, wp_ref[...], preferred_element_type=F32)
        dgl = _mx(dx2 * pp * gate * (1.0 - gate))
        dgl_ref[...] = dgl
        dpp_ref[...] = _mx(dx2 * gate)
        dhp = _dot_nt(dgl, wg_ref[...])
        dxn, dgr = _rms_bwd(x, gv, dhp)
        dx1_ref[...] = dx2 + dxn
        st_ref[0:1, :] += jnp.sum(dgr, axis=0, keepdims=True)

    row = lambda w: pl.BlockSpec((ts, w), lambda i: (i, 0))
    sq = pl.BlockSpec((D_MODEL, D_MODEL), lambda i: (0, 0))
    return _call(
        body, name="ple_bwd", grid=(S // ts,),
        in_specs=[row(D_MODEL), row(PLE), row(D_MODEL), pl.BlockSpec((1, D_MODEL), lambda i: (0, 0)), sq,
                  pl.BlockSpec((PLE, D_MODEL), lambda i: (0, 0))],
        out_specs=[row(D_MODEL), row(D_MODEL), row(D_MODEL), row(D_MODEL), row(PLE),
                   pl.BlockSpec((8, D_MODEL), lambda i: (0, 0))],
        out_shape=[jax.ShapeDtypeStruct((S, D_MODEL), F32)] + [jax.ShapeDtypeStruct((S, D_MODEL), MXU_DTYPE)] * 3
        + [jax.ShapeDtypeStruct((S, PLE), MXU_DTYPE), jax.ShapeDtypeStruct((8, D_MODEL), F32)],
        sem=("arbitrary",))(x1, p, dx2, g, wg, wp)


def _pad_rows(a, rows=8):
    return jnp.concatenate([a, jnp.zeros((rows - a.shape[0],) + a.shape[1:], a.dtype)], axis=0)


def _pad_lanes(a, lanes=128):
    return jnp.concatenate([a, jnp.zeros(a.shape[:-1] + (lanes - a.shape[-1],), a.dtype)], axis=-1)


def _wz_from_w_in(w):
    zeros = lambda n: jnp.zeros((w.shape[0], n), w.dtype)
    return jnp.concatenate([w[:, _OM:_OEND], w[:, _OA:_OB], w[:, _OB:_OF], w[:, _OC:_OD], w[:, _OD:_OM], zeros(256),
                            w[:, _OF:_OC], zeros(124)], axis=1)


def _w_in_from_wz(g):
    return jnp.concatenate([g[:, ZA:ZB], g[:, ZB:ZC], g[:, ZF:ZF + 4], g[:, ZC:ZD], g[:, ZD:ZD + 768], g[:, ZM:ZA]], axis=1)


_W_IN_GROUPS = [(_OA, _OB, ZA), (_OB, _OF, ZB), (_OF, _OC, ZF), (_OC, _OD, ZC), (_OD, _OM, ZD), (_OM, _OEND, ZM)]


def _wz_from_shards(g):
    n = g.shape[-1]
    pieces, pos = [], 0
    for a, b, zs in sorted(_W_IN_GROUPS, key=lambda grp: grp[2]):
        if zs > pos:
            pieces.append(jnp.zeros((g.shape[1], zs - pos), g.dtype))
        for k in range(4):
            lo, hi = max(a, k * n), min(b, (k + 1) * n)
            if lo < hi:
                pieces.append(g[k][:, lo - k * n:hi - k * n])
        pos = zs + (b - a)
    pieces.append(jnp.zeros((g.shape[1], ZCOLS - pos), g.dtype))
    return jnp.concatenate(pieces, axis=1)


def _w_in_slabs_from_wz(g):
    n = _OEND // 4
    slabs = []
    for k in range(4):
        pieces = []
        for a, b, zs in _W_IN_GROUPS:
            lo, hi = max(a, k * n), min(b, (k + 1) * n)
            if lo < hi:
                pieces.append(g[:, zs + lo - a:zs + hi - a])
        slabs.append(jnp.concatenate(pieces, axis=1))
    return jnp.stack(slabs)


def _local_step(x, p, target, wts, dist=None):
    tmat_np, masks_np = _hgrn_consts()
    tmat = jnp.asarray(tmat_np, MXU_DTYPE)
    masks = jnp.asarray(masks_np, F32)
    lower = _lower_bounds_fwd(wts["lb_logits"])
    saved = []
    for li in range(DEPTH):
        big = dist.weights(li) if dist else {n: wts[n][li] for n in _GATHERED}
        wz = _wz_from_shards(big["w_in"]) if big["w_in"].ndim == 3 else _wz_from_w_in(big["w_in"])
        g_mix = wts["norm_mix"][li][None, :]
        h = _norm_fwd(x, g_mix)
        z = _mm_nn(h, wz, F32, name="mm_z", ts=2 * ROW_TILE, tn=1664)
        cw = _pad_rows(jnp.concatenate([big["conv_w"], wts["conv_b"][li][None, :]], axis=0))
        ya = _conv_fwd(z, cw)
        fb = _pad_lanes(wts["fgate_bias"][li][None, :])
        gq, gk = wts["q_norm"][li][None, :], wts["k_norm"][li][None, :]
        qa, ka, va, cc = _attn_prep_fwd(z, fb, gq, gk)
        o, lse, yb, got = _flash_fwd(qa, ka, va, z, comm=dist.fwd_comm(li) if dist else None)
        if dist:
            dist.fwd_done(li, got)
        lb = lower[li][None, :]
        gh = wts["hgrn_norm"][li][None, :]
        o_pre, states, hscores, yc = _hgrn_fwd(z, lb, gh, tmat, masks)
        gv = wts["sgu_norm"][li][None, :]
        ws = wts["spatial_w"][li]
        bs = _pad_lanes(wts["spatial_b"][li].T)
        yd = _sgu_fwd(z, gv, ws, bs)
        ys = (ya, yb, yc, yd)
        x1, merged = _merge_fwd(x, z, ys, big["w_up"], big["merge_b"], big["w_o"])
        g_ple = wts["norm_ple"][li][None, :]
        x2 = _ple_fwd(x1, p[li], g_ple, big["w_ple_gate"], big["w_ple_proj"])
        saved.append(dict(x=x, h=h, z=z, wz=wz, cw=cw, fb=fb, gq=gq, gk=gk, qa=qa, ka=ka, va=va, cc=cc, o=o, lse=lse,
                          lb=lb, gh=gh, o_pre=o_pre, states=states, hscores=hscores, gv=gv, ws=ws, bs=bs, ys=ys, x1=x1, merged=merged,
                          g_mix=g_mix, g_ple=g_ple, big=big))
        x = x2

    loss, dx = _loss_fwd_bwd(x, target)

    names = ["norm_mix", "w_in", "conv_w", "conv_b", "fgate_bias", "q_norm", "k_norm", "lb", "hgrn_norm", "sgu_norm",
             "spatial_w", "spatial_b", "w_up", "merge_b", "w_o", "norm_ple", "w_ple_gate", "w_ple_proj"]
    gl = {n: [None] * DEPTH for n in names}
    for li in reversed(range(DEPTH)):
        s = saved[li]
        z, big = s["z"], s["big"]
        wg, wp = big["w_ple_gate"], big["w_ple_proj"]
        dx1, hp, dgl, dpp, pb, st = _ple_bwd(s["x1"], p[li], dx, s["g_ple"], wg, wp)
        gl["norm_ple"][li] = st[0]
        gl["w_ple_gate"][li] = _mm_tn(hp, dgl, name="mm_dwg")
        gl["w_ple_proj"][li] = _mm_tn(pb, dpp, name="mm_dwp")
        wup, wo = big["w_up"], big["w_o"]
        dz, du, dxb, dy, st = _merge_bwd(dx1, z, s["ys"], wup, big["merge_b"], wo)
        gl["merge_b"][li] = st[0:NBR]
        gl["w_o"][li] = _mm_tn(s["merged"], dxb, name="mm_dwo")
        gl["w_up"][li] = jnp.stack([_mm_tn(s["ys"][b], du, name="mm_dwup", ycol=b, n=D_MODEL) for b in range(NBR)])
        dz, st = _conv_bwd(z, s["cw"], dy, dz)
        gl["conv_w"][li] = st[0:3]
        gl["conv_b"][li] = st[3]
        dgate, qb, doa = _attn_gate_bwd(dy, s["o"], z, s["qa"], s["cc"], s["lse"])
        dq, doa, got = _flash_bwd_dq(qb, s["ka"], s["va"], doa, comm=dist.bwd_comm() if dist else None)
        if dist:
            dist.bwd_done(got)
        dk, dv, dcc = _flash_bwd_dkv(qb, s["ka"], s["va"], doa)
        dz, dzf, st = _attn_prep_bwd(z, s["fb"], s["gq"], s["gk"], dq, dk, dv, dgate, dcc, dz)
        dz = _put_fgate_cols(dzf, dz)
        gl["q_norm"][li] = st[0, 0:DH]
        gl["k_norm"][li] = st[1, 0:DH]
        gl["fgate_bias"][li] = st[2, 0:NH]
        dz, st = _hgrn_bwd(z, s["lb"], s["gh"], tmat, masks, s["o_pre"], s["states"], s["hscores"], dy, dz)
        gl["hgrn_norm"][li] = st[0]
        gl["lb"][li] = st[1]
        dz, dws, dbs, st = _sgu_bwd(z, s["gv"], s["ws"], s["bs"], dy, dz)
        gl["sgu_norm"][li] = st[0]
        gl["spatial_w"][li] = dws
        gl["spatial_b"][li] = dbs[:, 0:NH].T
        dwz = _mm_tn(s["h"], dz, name="mm_dwz", tn=1664)
        gl["w_in"][li] = dwz if dist else _w_in_from_wz(dwz)
        if dist:
            dist.push(li, {n: gl[n][li] for n in _BIG})
        if dist and li == 0:
            dh, got = _mm_nn(dz, s["wz"], F32, name="mm_dh", ts=ROW_TILE // 2, b_transposed=True, comm=dist.bwd_comm())
            dist.bwd_done(got)
        else:
            dh = _mm_nn(dz, s["wz"], F32, name="mm_dh", ts=ROW_TILE // 2, b_transposed=True)
        dx, st = _norm_bwd(s["x"], s["g_mix"], dh, dx1)
        gl["norm_mix"][li] = st[0]

    if dist:
        dist.finish()
    grads = {n: jnp.stack(v) for n, v in gl.items() if not (dist and n in _BIG)}
    grads["lb_logits"] = _lower_bounds_bwd(wts["lb_logits"], grads.pop("lb"))
    return loss, dx, grads


def _my_pos():
    return lax.axis_index("x"), lax.axis_index("y"), lax.axis_index("c")


def _gather_phase(phase, ins, outs, sems):
    if phase == 1:
        return
    send, recv, lsem = sems
    x, y, c = _my_pos()
    me = 2 * x + y
    peers = [(1 - x, y), (x, 1 - y), (1 - x, 1 - y)]
    copies = []
    for t in range(len(ins)):
        copies.append(pltpu.make_async_copy(ins[t], outs[t].at[me], lsem.at[t]))
        for j, (px, py) in enumerate(peers):
            copies.append(pltpu.make_async_remote_copy(
                src_ref=ins[t], dst_ref=outs[t].at[me], send_sem=send.at[t, j], recv_sem=recv.at[t, j],
                device_id=(px, py, c), device_id_type=MESH))
    for cp in copies:
        if phase == 0:
            cp.start()
        else:
            cp.wait()


N_HALVED = 5


def _gather_halves_phase(phase, ins, outs, sems):
    send, recv, lsem = sems
    x, y, c = _my_pos()
    me = 2 * x + y
    sib = (x, y, 1 - c)
    chips = [(1 - x, y), (x, 1 - y), (1 - x, 1 - y)]

    def rc(t, k, src, dst, dev):
        return pltpu.make_async_remote_copy(src_ref=src, dst_ref=dst, send_sem=send.at[t, k], recv_sem=recv.at[t, k],
                                            device_id=dev, device_id_type=MESH)

    for t in range(len(ins)):
        local = pltpu.make_async_copy(ins[t], outs[t].at[me], lsem.at[t])
        if t >= N_HALVED:
            whole = [rc(t, j, ins[t], outs[t].at[me], (px, py, c)) for j, (px, py) in enumerate(chips)]
            for cp in [local] + whole:
                if phase == 0:
                    cp.start()
                elif phase == 2:
                    cp.wait()
            continue
        hr = ins[t].shape[0] // 2
        mine, other = pl.ds(c * hr, hr), pl.ds((1 - c) * hr, hr)
        first = [rc(t, j, ins[t].at[mine], outs[t].at[me, mine], (px, py, c)) for j, (px, py) in enumerate(chips)]
        landed = [outs[t].at[2 * px + py, mine] for px, py in chips]
        passed = [rc(t, 3 + j, slot, slot, sib) for j, slot in enumerate(landed)]
        if phase == 0:
            local.start()
            for cp in first:
                cp.start()
        elif phase == 1:
            for j, slot in enumerate(landed):
                rc(t, j, slot, slot, (x, y, c)).wait_recv()
                passed[j].start()
        else:
            for j, (px, py) in enumerate(chips):
                slot = outs[t].at[2 * px + py, other]
                rc(t, 3 + j, slot, slot, (x, y, c)).wait_recv()
            for cp in first + passed:
                cp.wait_send()
            local.wait()


def _exchange_phase(phase, ins, outs, sems):
    send, recv, lsem = sems
    x, y, c = _my_pos()
    me = 2 * x + y
    sib = (x, y, 1 - c)
    chips = [(1 - x, y), (x, 1 - y), (1 - x, 1 - y)]

    def rc(t, k, src, dst, dev):
        return pltpu.make_async_remote_copy(src_ref=src, dst_ref=dst, send_sem=send.at[t, k], recv_sem=recv.at[t, k],
                                            device_id=dev, device_id_type=MESH)

    for t in range(len(ins)):
        local = pltpu.make_async_copy(ins[t].at[me], outs[t].at[2 * me + c], lsem.at[t])
        first = [rc(t, 0, ins[t].at[me], outs[t].at[2 * me + c], sib)]
        first += [rc(t, 1 + j, ins[t].at[2 * px + py], outs[t].at[2 * me + c], (px, py, c))
                  for j, (px, py) in enumerate(chips)]
        landed = [outs[t].at[2 * (2 * px + py) + c] for px, py in chips]
        passed = [rc(t, 4 + j, slot, slot, sib) for j, slot in enumerate(landed)]
        if phase == 0:
            local.start()
            for cp in first:
                cp.start()
        elif phase == 1:
            for j, slot in enumerate(landed):
                rc(t, 1 + j, slot, slot, (x, y, c)).wait_recv()
                passed[j].start()
        else:
            s0 = outs[t].at[2 * me + (1 - c)]
            rc(t, 0, s0, s0, (x, y, c)).wait_recv()
            for j, (px, py) in enumerate(chips):
                slot = outs[t].at[2 * (2 * px + py) + (1 - c)]
                rc(t, 4 + j, slot, slot, (x, y, c)).wait_recv()
            for cp in first + passed:
                cp.wait_send()
            local.wait()


_COMM_PHASES = {"gather": _gather_phase, "gather_halves": _gather_halves_phase, "exchange": _exchange_phase}
_gathered_shapes = lambda arrays: [jax.ShapeDtypeStruct((4,) + a.shape, a.dtype) for a in arrays]
_COMM_OUT = {"gather": _gathered_shapes, "gather_halves": _gathered_shapes,
             "exchange": lambda arrays: [jax.ShapeDtypeStruct((8,) + a.shape[1:], a.dtype) for a in arrays]}


def _comm_sems(kind, nc):
    k = {"gather": 3, "gather_halves": 6, "exchange": 7}[kind]
    return [pltpu.SemaphoreType.DMA((nc, k)), pltpu.SemaphoreType.DMA((nc, k)), pltpu.SemaphoreType.DMA((nc,))]


def _comm_alone(kind, arrays, name):
    nc = len(arrays)

    def body(*refs):
        for phase in range(3):
            _COMM_PHASES[kind](phase, refs[:nc], refs[nc:2 * nc], refs[2 * nc:])

    hbm = pl.BlockSpec(memory_space=pl.ANY)
    return pl.pallas_call(
        functools.partial(body), name=name, in_specs=[hbm] * nc, out_specs=[hbm] * nc,
        out_shape=_COMM_OUT[kind](arrays), scratch_shapes=_comm_sems(kind, nc))(*arrays)


def _allreduce_small(v):
    R = v.shape[0]

    def body(v_ref, o_ref, pair, chips, send, recv):
        x, y, c = _my_pos()
        me = 2 * x + y
        pair[c] = v_ref[...]
        swap = pltpu.make_async_remote_copy(src_ref=v_ref, dst_ref=pair.at[c], send_sem=send.at[0], recv_sem=recv.at[0],
                                            device_id=(x, y, 1 - c), device_id_type=MESH)
        swap.start()
        swap.wait()
        chips[me] = pair[0] + pair[1]
        copies = [pltpu.make_async_remote_copy(src_ref=chips.at[me], dst_ref=chips.at[me], send_sem=send.at[1 + j],
                                               recv_sem=recv.at[1 + j], device_id=(px, py, c), device_id_type=MESH)
                  for j, (px, py) in enumerate([(1 - x, y), (x, 1 - y), (1 - x, 1 - y)])]
        for cp in copies:
            cp.start()
        for cp in copies:
            cp.wait()
        o_ref[...] = (chips[0] + chips[1]) + (chips[2] + chips[3])

    vm = pl.BlockSpec(memory_space=pltpu.VMEM)
    return pl.pallas_call(
        functools.partial(body), name="allreduce_small", in_specs=[vm], out_specs=vm,
        out_shape=jax.ShapeDtypeStruct((R, 128), F32),
        scratch_shapes=[pltpu.VMEM((2, R, 128), F32), pltpu.VMEM((4, R, 128), F32),
                        pltpu.SemaphoreType.DMA((4,)), pltpu.SemaphoreType.DMA((4,))],
        compiler_params=pltpu.CompilerParams(vmem_limit_bytes=VMEM_LIMIT))(v)


def _adamw(w, m, v, parts, nparts):
    A, R, C = w.shape
    per_layer = isinstance(parts, (list, tuple))
    tr = _tile(R, 64) if per_layer else (R if R <= 128 else _tile(R, 128))
    nparr = len(parts) if per_layer else 1

    def body(*refs):
        w_ref, m_ref, v_ref = refs[:3]
        p_refs = refs[3:3 + nparr]
        g_ref, d_ref, nm_ref, nv_ref = refs[3 + nparr:]

        def update(read):
            g = read(0).astype(F32)
            for k in range(1, nparts):
                g = g + read(k).astype(F32)
            mm = ADAM_B1 * m_ref[0] + (1.0 - ADAM_B1) * g
            vv = ADAM_B2 * v_ref[0] + (1.0 - ADAM_B2) * jnp.square(g)
            m_hat = mm / (1.0 - ADAM_B1 ** ADAM_STEP)
            v_hat = vv / (1.0 - ADAM_B2 ** ADAM_STEP)
            g_ref[0] = g
            d_ref[0] = -ADAM_LR * (m_hat / (jnp.sqrt(v_hat) + ADAM_EPS) + ADAM_WD * w_ref[0])
            nm_ref[0] = mm
            nv_ref[0] = vv

        if not per_layer:
            update(lambda k: p_refs[0][k, 0])
        else:
            for a in range(A):
                @pl.when(pl.program_id(0) == a)
                def _(a=a):
                    update(lambda k: p_refs[a][k])

    blk = pl.BlockSpec((1, tr, C), lambda a, r: (a, r, 0))
    if per_layer:
        pspecs = [pl.BlockSpec((nparts, tr, C), functools.partial(lambda a, r, l: (0, jnp.where(a == l, r, 0), 0), l=l))
                  for l in range(A)]
        pargs = list(parts)
    else:
        pspecs = [pl.BlockSpec((nparts, 1, tr, C), lambda a, r: (0, a, r, 0))]
        pargs = [parts]
    return _call(
        body, name="adamw", grid=(A, R // tr), in_specs=[blk, blk, blk] + pspecs,
        out_specs=[blk] * 4, out_shape=[jax.ShapeDtypeStruct((A, R, C), F32)] * 4,
        sem=("arbitrary", "arbitrary"))(w, m, v, *pargs)


def _rows128(a):
    width = a.shape[-1]
    a2 = a.reshape(-1, width)
    k = -(-width // 128)
    if width % 128:
        a2 = jnp.pad(a2, ((0, 0), (0, k * 128 - width)))
    return a2.reshape(-1, 128)


def _from_rows128(r, shape):
    width = shape[-1]
    k = -(-width // 128)
    return r.reshape(-1, k * 128)[:, :width].reshape(shape)


def _as3d(a):
    if a.ndim == 2:
        return a[None]
    if a.ndim == 3:
        return a
    return a.reshape((-1,) + a.shape[-2:])


_WEIGHTS = ["norm_mix", "w_in", "conv_w", "conv_b", "fgate_bias", "q_norm", "k_norm", "lb_logits", "hgrn_norm", "sgu_norm",
            "spatial_w", "spatial_b", "w_up", "merge_b", "w_o", "norm_ple", "w_ple_gate", "w_ple_proj"]
_BIG = ["w_in", "w_up", "w_o", "w_ple_gate", "w_ple_proj"]
_GATHERED = _BIG + ["conv_w", "merge_b"]
_SHARD_AXIS = {"w_in": 2, "w_up": 3, "w_o": 1, "w_ple_gate": 1, "w_ple_proj": 2, "conv_w": 2, "merge_b": 2}
_SMALL = [n for n in _WEIGHTS if n not in _BIG]


class _Dist:
    def __init__(self, w):
        self.w = w
        self.full = {0: self._unpack(_comm_alone("gather_halves", self._shards(0), "gather_weights"))}
        self.contrib = {}
        self.pending = None

    def _shards(self, l):
        return [_mx(self.w[n][l]) for n in _BIG] + [self.w["conv_w"][l], self.w["merge_b"][l]]

    def _unpack(self, gathered):
        return {n: g if n == "w_in" else jnp.concatenate([g[k] for k in range(4)], axis=_SHARD_AXIS[n] - 1)
                for n, g in zip(_GATHERED, gathered)}

    def weights(self, l):
        return self.full[l]

    def fwd_comm(self, l):
        return ("gather", self._shards(l + 1)) if l + 1 < DEPTH else None

    def fwd_done(self, l, got):
        if got:
            self.full[l + 1] = self._unpack(got)

    def push(self, l, grads):
        self.pending = (l, [(_w_in_slabs_from_wz(grads[n]) if n == "w_in" else
                             jnp.stack(jnp.split(grads[n], 4, axis=_SHARD_AXIS[n] - 1))).astype(GRAD_WIRE_DTYPE)
                            for n in _BIG])

    def bwd_comm(self):
        return ("exchange", self.pending[1]) if self.pending else None

    def bwd_done(self, got):
        if got:
            self.contrib[self.pending[0]] = got
            self.pending = None

    def finish(self):
        if self.pending:
            self.contrib[self.pending[0]] = _comm_alone("exchange", self.pending[1], "exchange_grads")
            self.pending = None


def kernel(x, p, norm_mix, w_in, conv_w, conv_b, fgate_bias, q_norm, k_norm, lb_logits, hgrn_norm, sgu_norm, spatial_w, spatial_b, w_up, merge_b, w_o, norm_ple, w_ple_gate, w_ple_proj, loss_target, m_norm_mix, m_w_in, m_conv_w, m_conv_b, m_fgate_bias, m_q_norm, m_k_norm, m_lb_logits, m_hgrn_norm, m_sgu_norm, m_spatial_w, m_spatial_b, m_w_up, m_merge_b, m_w_o, m_norm_ple, m_w_ple_gate, m_w_ple_proj, v_norm_mix, v_w_in, v_conv_w, v_conv_b, v_fgate_bias, v_q_norm, v_k_norm, v_lb_logits, v_hgrn_norm, v_sgu_norm, v_spatial_w, v_spatial_b, v_w_up, v_merge_b, v_w_o, v_norm_ple, v_w_ple_gate, v_w_ple_proj):
    loc = dict(locals())
    w = {n: loc[n] for n in _WEIGHTS}
    m = {n: loc["m_" + n] for n in _WEIGHTS}
    v = {n: loc["v_" + n] for n in _WEIGHTS}
    chip = 2 * lax.axis_index("x") + lax.axis_index("y")

    dist = _Dist(w)
    loss_blk, dx, grads = _local_step(x[0], p[:, 0], loss_target[0], w, dist)
    loss = lax.psum(loss_blk[0, 0], ("x", "y", "c"))

    blocks = [_rows128(grads[n]) for n in _SMALL]
    nrows = sum(b.shape[0] for b in blocks)
    packed = jnp.concatenate(blocks + [jnp.zeros(((-nrows) % 8, 128), F32)], axis=0)
    red = _allreduce_small(packed)
    small, off = {}, 0
    for n, b in zip(_SMALL, blocks):
        small[n] = _from_rows128(red[off:off + b.shape[0]], grads[n].shape)
        off += b.shape[0]
    for n in ("conv_w", "merge_b"):
        ax = _SHARD_AXIS[n]
        width = small[n].shape[ax] // 4
        small[n] = lax.dynamic_slice_in_dim(small[n], chip * width, width, axis=ax)

    out_g, out_d, out_m, out_v = {}, {}, {}, {}
    for n in _WEIGHTS:
        shp = w[n].shape
        if n in _BIG:
            w3, m3, v3 = (a.reshape((DEPTH, -1, shp[-1])) for a in (w[n], m[n], v[n]))
            parts = [dist.contrib[l][_BIG.index(n)].reshape((8,) + w3.shape[1:]) for l in range(DEPTH)]
            g, d, nm, nv = _adamw(w3, m3, v3, parts, 8)
        else:
            g, d, nm, nv = _adamw(_as3d(w[n]), _as3d(m[n]), _as3d(v[n]), _as3d(small[n])[None], 1)
        out_g[n], out_d[n], out_m[n], out_v[n] = (a.reshape(shp) for a in (g, d, nm, nv))

    return (loss, dx[None], *[out_g[n] for n in _WEIGHTS], *[out_d[n] for n in _WEIGHTS],
            *[out_m[n] for n in _WEIGHTS], *[out_v[n] for n in _WEIGHTS])
```
